```python
import jax, jax.numpy as jnp
from jax import lax
import numpy as np

D_MODEL = 2048
BATCH = 8
SEQ = 4096
DEPTH = 2

N_MIXERS = 2
HG_HEADS = 16
HG_KEY_DIM = 128
HG_VAL_DIM = D_MODEL // HG_HEADS
HG_WIDTH = HG_HEADS * HG_KEY_DIM
HG_V_WIDTH = HG_HEADS * HG_VAL_DIM
HG_CHUNK = 64
SG_WIDTH = D_MODEL
SG_GROUPS = 16
SG_GROUP_DIM = SG_WIDTH // SG_GROUPS
SG_CHUNK = 128
D_FF = 5632
CONV_WIDTH = 3
ALPHA = (2 * DEPTH) ** 0.25
BETA = (8 * DEPTH) ** -0.25
LN_EPS = 1e-5
RMS_EPS = 1e-6
N_HG_LAYERS = (DEPTH + 1) // 2
N_SG_LAYERS = DEPTH // 2

kernel_name = 'hgrn2_gmlp_convffn_deepnorm_hybrid'


def layer_norm(x, g, b):
    xf = x.astype(jnp.float32)
    mu = jnp.mean(xf, axis=-1, keepdims=True)
    xc = xf - mu
    var = jnp.mean(xc * xc, axis=-1, keepdims=True)
    y = xc * lax.rsqrt(var + LN_EPS) * g.astype(jnp.float32) + b.astype(jnp.float32)
    return y.astype(x.dtype)


def hgrn2_mixer(x, w_in, norm_g, w_out, lb):
    b_, s_, _ = x.shape
    n = s_ // HG_CHUNK
    proj = x @ w_in
    q, f, i, g = jnp.split(proj, [HG_WIDTH, 2 * HG_WIDTH, 2 * HG_WIDTH + HG_V_WIDTH], axis=-1)
    f = f.astype(jnp.float32)
    q = jax.nn.silu(q.astype(jnp.float32))
    v = i.astype(jnp.float32)
    log_forget = jnp.logaddexp(jnp.log(lb), jnp.log1p(-lb) + jax.nn.log_sigmoid(f))
    k = (1.0 - lb) * jax.nn.sigmoid(-f)

    def to_chunks(t, d):
        return t.reshape(b_, n, HG_CHUNK, HG_HEADS, d).transpose(1, 0, 3, 2, 4)

    qc = to_chunks(q, HG_KEY_DIM)
    kc = to_chunks(k, HG_KEY_DIM)
    lfc = to_chunks(log_forget, HG_KEY_DIM)
    vc = to_chunks(v, HG_VAL_DIM)
    mask = jnp.tril(jnp.ones((HG_CHUNK, HG_CHUNK), dtype=bool))

    def step(state, inp):
        q_c, k_c, v_c, lf_c = inp
        cum = jnp.cumsum(lf_c, axis=2)
        rel = cum[:, :, :, None, :] - cum[:, :, None, :, :]
        decay = jnp.exp(jnp.where(mask[:, :, None], rel, -jnp.inf))
        scores = jnp.einsum('bhtk,bhsk,bhtsk->bhts', q_c, k_c, decay)
        out = (jnp.einsum('bhts,bhsv->bhtv', scores, v_c)
               + jnp.einsum('bhtk,bhkv->bhtv', q_c * jnp.exp(cum), state))
        last = cum[:, :, -1:, :]
        new_state = (jnp.exp(last[:, :, 0, :, None]) * state
                     + jnp.einsum('bhsk,bhsv->bhkv', k_c * jnp.exp(last - cum), v_c))
        return new_state, out

    state0 = jnp.zeros((b_, HG_HEADS, HG_KEY_DIM, HG_VAL_DIM), jnp.float32)
    _, o = lax.scan(step, state0, (qc, kc, vc, lfc))
    o = o.transpose(1, 0, 3, 2, 4).reshape(b_, s_, HG_HEADS, HG_VAL_DIM)
    o = o * lax.rsqrt(jnp.mean(o * o, axis=-1, keepdims=True) + RMS_EPS)
    o = o * norm_g.astype(jnp.float32).reshape(HG_HEADS, HG_VAL_DIM)
    o = o.reshape(b_, s_, HG_V_WIDTH) * jax.nn.silu(g.astype(jnp.float32))
    return o.astype(x.dtype) @ w_out


def chunked_gmlp(x, w_in, ln_g, ln_b, w_s, b_s, w_out):
    b_, s_, _ = x.shape
    n = s_ // SG_CHUNK
    z = jax.nn.gelu(x @ w_in, approximate=False)
    u, v = jnp.split(z, 2, axis=-1)
    v = layer_norm(v, ln_g, ln_b).reshape(b_, n, SG_CHUNK, SG_GROUPS, SG_GROUP_DIM)
    w_causal = w_s * jnp.tril(jnp.ones((SG_CHUNK, SG_CHUNK), w_s.dtype))
    gate = jnp.einsum('gts,bnsgc->bntgc', w_causal, v) + b_s.T[:, :, None]
    y = u * gate.reshape(b_, s_, SG_WIDTH)
    return y @ w_out


def conv_ffn(x, w_up, conv_w, conv_b, w_down):
    s_ = x.shape[1]
    h = x @ w_up
    a, b = jnp.split(h, 2, axis=-1)
    a_pad = jnp.pad(a, ((0, 0), (CONV_WIDTH - 1, 0), (0, 0)))
    a = sum(conv_w[j] * a_pad[:, j:j + s_] for j in range(CONV_WIDTH)) + conv_b
    return (jax.nn.silu(a) * b) @ w_down


def _fwd_setup_inputs(seed: int = 0) -> dict:
    key = jax.random.key(seed)
    ks = jax.random.split(key, 24)
    nrm = jax.random.normal
    f32 = jnp.float32
    d = D_MODEL
    hg_in_cols = 2 * HG_WIDTH + 2 * HG_V_WIDTH
    return {
        'x': nrm(ks[0], (BATCH, SEQ, d), f32),
        'lb_logits': 0.5 * nrm(ks[1], (DEPTH + 1, HG_WIDTH), f32),
        'hg_w_in': nrm(ks[2], (N_HG_LAYERS, d, hg_in_cols), f32) * d ** -0.5,
        'hg_norm_g': 1.0 + 0.02 * nrm(ks[3], (N_HG_LAYERS, HG_V_WIDTH), f32),
        'hg_w_out': nrm(ks[4], (N_HG_LAYERS, HG_V_WIDTH, d), f32) * HG_V_WIDTH ** -0.5 * BETA,
        'sg_w_in': nrm(ks[5], (N_SG_LAYERS, d, 2 * SG_WIDTH), f32) * d ** -0.5,
        'sg_ln_g': 1.0 + 0.02 * nrm(ks[6], (N_SG_LAYERS, SG_WIDTH), f32),
        'sg_ln_b': 0.02 * nrm(ks[7], (N_SG_LAYERS, SG_WIDTH), f32),
        'sg_w_s': nrm(ks[8], (N_SG_LAYERS, SG_GROUPS, SG_CHUNK, SG_CHUNK), f32) * 0.5 * SG_CHUNK ** -0.5,
        'sg_b_s': 1.0 + 0.1 * nrm(ks[9], (N_SG_LAYERS, SG_GROUPS, SG_CHUNK), f32),
        'sg_w_out': nrm(ks[10], (N_SG_LAYERS, SG_WIDTH, d), f32) * SG_WIDTH ** -0.5 * BETA,
        'ffn_w_up': nrm(ks[11], (DEPTH, d, 2 * D_FF), f32) * d ** -0.5,
        'ffn_conv_w': nrm(ks[12], (DEPTH, CONV_WIDTH, D_FF), f32) * CONV_WIDTH ** -0.5,
        'ffn_conv_b': 0.02 * nrm(ks[13], (DEPTH, D_FF), f32),
        'ffn_w_down': nrm(ks[14], (DEPTH, D_FF, d), f32) * D_FF ** -0.5 * BETA,
        'ln1_g': 1.0 + 0.02 * nrm(ks[15], (DEPTH, d), f32),
        'ln1_b': 0.02 * nrm(ks[16], (DEPTH, d), f32),
        'ln2_g': 1.0 + 0.02 * nrm(ks[17], (DEPTH, d), f32),
        'ln2_b': 0.02 * nrm(ks[18], (DEPTH, d), f32),
    }


def _fwd_reference(x, lb_logits, hg_w_in, hg_norm_g, hg_w_out, sg_w_in, sg_ln_g, sg_ln_b,
              sg_w_s, sg_b_s, sg_w_out, ffn_w_up, ffn_conv_w, ffn_conv_b, ffn_w_down,
              ln1_g, ln1_b, ln2_g, ln2_b):
    lower_bounds = jnp.cumsum(jax.nn.softmax(lb_logits.astype(jnp.float32), axis=0), axis=0)
    h = x
    for layer in range(DEPTH):
        occ = layer // N_MIXERS
        if layer % N_MIXERS == 0:
            mixed = hgrn2_mixer(h, hg_w_in[occ], hg_norm_g[occ], hg_w_out[occ], lower_bounds[layer])
        else:
            mixed = chunked_gmlp(h, sg_w_in[occ], sg_ln_g[occ], sg_ln_b[occ],
                                 sg_w_s[occ], sg_b_s[occ], sg_w_out[occ])
        h = layer_norm(ALPHA * h + mixed, ln1_g[layer], ln1_b[layer])
        ffn = conv_ffn(h, ffn_w_up[layer], ffn_conv_w[layer], ffn_conv_b[layer], ffn_w_down[layer])
        h = layer_norm(ALPHA * h + ffn, ln2_g[layer], ln2_b[layer])
    return h


import jax as _jax
import jax.numpy as _jnp

TWIN_FORMAT = 'train_step'
FWD_PARAMS = ['x', 'lb_logits', 'hg_w_in', 'hg_norm_g', 'hg_w_out', 'sg_w_in', 'sg_ln_g', 'sg_ln_b', 'sg_w_s', 'sg_b_s', 'sg_w_out', 'ffn_w_up', 'ffn_conv_w', 'ffn_conv_b', 'ffn_w_down', 'ln1_g', 'ln1_b', 'ln2_g', 'ln2_b']
TWIN_WEIGHTS = ['lb_logits', 'hg_w_in', 'hg_norm_g', 'hg_w_out', 'sg_w_in', 'sg_ln_g', 'sg_ln_b', 'sg_w_s', 'sg_b_s', 'sg_w_out', 'ffn_w_up', 'ffn_conv_w', 'ffn_conv_b', 'ffn_w_down', 'ln1_g', 'ln1_b', 'ln2_g', 'ln2_b']
TWIN_DIFF_INPUT = 'x'
TWIN_INPUTS = ['x', 'lb_logits', 'hg_w_in', 'hg_norm_g', 'hg_w_out', 'sg_w_in', 'sg_ln_g', 'sg_ln_b', 'sg_w_s', 'sg_b_s', 'sg_w_out', 'ffn_w_up', 'ffn_conv_w', 'ffn_conv_b', 'ffn_w_down', 'ln1_g', 'ln1_b', 'ln2_g', 'ln2_b', 'loss_target', 'm_lb_logits', 'm_hg_w_in', 'm_hg_norm_g', 'm_hg_w_out', 'm_sg_w_in', 'm_sg_ln_g', 'm_sg_ln_b', 'm_sg_w_s', 'm_sg_b_s', 'm_sg_w_out', 'm_ffn_w_up', 'm_ffn_conv_w', 'm_ffn_conv_b', 'm_ffn_w_down', 'm_ln1_g', 'm_ln1_b', 'm_ln2_g', 'm_ln2_b', 'v_lb_logits', 'v_hg_w_in', 'v_hg_norm_g', 'v_hg_w_out', 'v_sg_w_in', 'v_sg_ln_g', 'v_sg_ln_b', 'v_sg_w_s', 'v_sg_b_s', 'v_sg_w_out', 'v_ffn_w_up', 'v_ffn_conv_w', 'v_ffn_conv_b', 'v_ffn_w_down', 'v_ln1_g', 'v_ln1_b', 'v_ln2_g', 'v_ln2_b']
TWIN_OUTPUTS = ['loss', 'grad_x', 'grad_lb_logits', 'grad_hg_w_in', 'grad_hg_norm_g', 'grad_hg_w_out', 'grad_sg_w_in', 'grad_sg_ln_g', 'grad_sg_ln_b', 'grad_sg_w_s', 'grad_sg_b_s', 'grad_sg_w_out', 'grad_ffn_w_up', 'grad_ffn_conv_w', 'grad_ffn_conv_b', 'grad_ffn_w_down', 'grad_ln1_g', 'grad_ln1_b', 'grad_ln2_g', 'grad_ln2_b', 'delta_lb_logits', 'delta_hg_w_in', 'delta_hg_norm_g', 'delta_hg_w_out', 'delta_sg_w_in', 'delta_sg_ln_g', 'delta_sg_ln_b', 'delta_sg_w_s', 'delta_sg_b_s', 'delta_sg_w_out', 'delta_ffn_w_up', 'delta_ffn_conv_w', 'delta_ffn_conv_b', 'delta_ffn_w_down', 'delta_ln1_g', 'delta_ln1_b', 'delta_ln2_g', 'delta_ln2_b', 'new_m_lb_logits', 'new_m_hg_w_in', 'new_m_hg_norm_g', 'new_m_hg_w_out', 'new_m_sg_w_in', 'new_m_sg_ln_g', 'new_m_sg_ln_b', 'new_m_sg_w_s', 'new_m_sg_b_s', 'new_m_sg_w_out', 'new_m_ffn_w_up', 'new_m_ffn_conv_w', 'new_m_ffn_conv_b', 'new_m_ffn_w_down', 'new_m_ln1_g', 'new_m_ln1_b', 'new_m_ln2_g', 'new_m_ln2_b', 'new_v_lb_logits', 'new_v_hg_w_in', 'new_v_hg_norm_g', 'new_v_hg_w_out', 'new_v_sg_w_in', 'new_v_sg_ln_g', 'new_v_sg_ln_b', 'new_v_sg_w_s', 'new_v_sg_b_s', 'new_v_sg_w_out', 'new_v_ffn_w_up', 'new_v_ffn_conv_w', 'new_v_ffn_conv_b', 'new_v_ffn_w_down', 'new_v_ln1_g', 'new_v_ln1_b', 'new_v_ln2_g', 'new_v_ln2_b']
TWIN_LEAF_KINDS = {'loss': 'loss', 'grad_x': 'grad_x', 'grad_lb_logits': 'grad_w', 'grad_hg_w_in': 'grad_w', 'grad_hg_norm_g': 'grad_w', 'grad_hg_w_out': 'grad_w', 'grad_sg_w_in': 'grad_w', 'grad_sg_ln_g': 'grad_w', 'grad_sg_ln_b': 'grad_w', 'grad_sg_w_s': 'grad_w', 'grad_sg_b_s': 'grad_w', 'grad_sg_w_out': 'grad_w', 'grad_ffn_w_up': 'grad_w', 'grad_ffn_conv_w': 'grad_w', 'grad_ffn_conv_b': 'grad_w', 'grad_ffn_w_down': 'grad_w', 'grad_ln1_g': 'grad_w', 'grad_ln1_b': 'grad_w', 'grad_ln2_g': 'grad_w', 'grad_ln2_b': 'grad_w', 'delta_lb_logits': 'delta_w', 'delta_hg_w_in': 'delta_w', 'delta_hg_norm_g': 'delta_w', 'delta_hg_w_out': 'delta_w', 'delta_sg_w_in': 'delta_w', 'delta_sg_ln_g': 'delta_w', 'delta_sg_ln_b': 'delta_w', 'delta_sg_w_s': 'delta_w', 'delta_sg_b_s': 'delta_w', 'delta_sg_w_out': 'delta_w', 'delta_ffn_w_up': 'delta_w', 'delta_ffn_conv_w': 'delta_w', 'delta_ffn_conv_b': 'delta_w', 'delta_ffn_w_down': 'delta_w', 'delta_ln1_g': 'delta_w', 'delta_ln1_b': 'delta_w', 'delta_ln2_g': 'delta_w', 'delta_ln2_b': 'delta_w', 'new_m_lb_logits': 'new_m', 'new_m_hg_w_in': 'new_m', 'new_m_hg_norm_g': 'new_m', 'new_m_hg_w_out': 'new_m', 'new_m_sg_w_in': 'new_m', 'new_m_sg_ln_g': 'new_m', 'new_m_sg_ln_b': 'new_m', 'new_m_sg_w_s': 'new_m', 'new_m_sg_b_s': 'new_m', 'new_m_sg_w_out': 'new_m', 'new_m_ffn_w_up': 'new_m', 'new_m_ffn_conv_w': 'new_m', 'new_m_ffn_conv_b': 'new_m', 'new_m_ffn_w_down': 'new_m', 'new_m_ln1_g': 'new_m', 'new_m_ln1_b': 'new_m', 'new_m_ln2_g': 'new_m', 'new_m_ln2_b': 'new_m', 'new_v_lb_logits': 'new_v', 'new_v_hg_w_in': 'new_v', 'new_v_hg_norm_g': 'new_v', 'new_v_hg_w_out': 'new_v', 'new_v_sg_w_in': 'new_v', 'new_v_sg_ln_g': 'new_v', 'new_v_sg_ln_b': 'new_v', 'new_v_sg_w_s': 'new_v', 'new_v_sg_b_s': 'new_v', 'new_v_sg_w_out': 'new_v', 'new_v_ffn_w_up': 'new_v', 'new_v_ffn_conv_w': 'new_v', 'new_v_ffn_conv_b': 'new_v', 'new_v_ffn_w_down': 'new_v', 'new_v_ln1_g': 'new_v', 'new_v_ln1_b': 'new_v', 'new_v_ln2_g': 'new_v', 'new_v_ln2_b': 'new_v'}


def _forward(args):
    return _fwd_reference(*[args[k] for k in FWD_PARAMS])


def _output_shape():
    out = _jax.eval_shape(lambda: _forward(_fwd_setup_inputs(0)))
    return out.shape, out.dtype

N_MICROBATCH = 1
ADAM_LR = 0.001
ADAM_B1 = 0.9
ADAM_B2 = 0.999
ADAM_EPS = 1e-08
ADAM_WD = 0.01
ADAM_STEP = 10
PER_EXAMPLE_BATCH_AXIS = {'x': 0, 'loss_target': 0}
SHARED_INPUTS = []
_WEIGHT_DTYPES = {'lb_logits': _jnp.float32, 'hg_w_in': _jnp.float32, 'hg_norm_g': _jnp.float32, 'hg_w_out': _jnp.float32, 'sg_w_in': _jnp.float32, 'sg_ln_g': _jnp.float32, 'sg_ln_b': _jnp.float32, 'sg_w_s': _jnp.float32, 'sg_b_s': _jnp.float32, 'sg_w_out': _jnp.float32, 'ffn_w_up': _jnp.float32, 'ffn_conv_w': _jnp.float32, 'ffn_conv_b': _jnp.float32, 'ffn_w_down': _jnp.float32, 'ln1_g': _jnp.float32, 'ln1_b': _jnp.float32, 'ln2_g': _jnp.float32, 'ln2_b': _jnp.float32}
MOMENT_SCALE = {'lb_logits': 8.754826e-04, 'hg_w_in': 1.395829e-02, 'hg_norm_g': 1.947698e-02, 'hg_w_out': 3.845188e-02, 'sg_w_in': 1.682965e-02, 'sg_ln_g': 7.590656e-03, 'sg_ln_b': 7.220343e-03, 'sg_w_s': 1.441343e-02, 'sg_b_s': 2.057014e-02, 'sg_w_out': 5.541815e-02, 'ffn_w_up': 1.171844e-02, 'ffn_conv_w': 1.199027e-02, 'ffn_conv_b': 1.160823e-02, 'ffn_w_down': 3.836240e-02, 'ln1_g': 4.718162e-01, 'ln1_b': 2.306980e-01, 'ln2_g': 1.133304e+01, 'ln2_b': 1.063506e+00}


def _to_microbatches(a, axis):
    t = _jnp.moveaxis(a, axis, 0)
    t = t.reshape((N_MICROBATCH, t.shape[0] // N_MICROBATCH) + t.shape[1:])
    return _jnp.moveaxis(t, 1, axis + 1)


def setup_inputs(seed: int = 0) -> dict:
    inp = _fwd_setup_inputs(seed)
    key = _jax.random.fold_in(_jax.random.key(seed), 7919)
    shape, _ = _output_shape()
    out = dict(inp)
    out["loss_target"] = _jax.random.normal(_jax.random.fold_in(key, 0), shape, _jnp.float32)
    for i, name in enumerate(TWIN_WEIGHTS):
        w = inp[name].astype(_jnp.float32)
        if MOMENT_SCALE is None:
            s = _jnp.sqrt(_jnp.mean(_jnp.square(w)) + 1e-30)
        else:
            s = MOMENT_SCALE[name]
        km, kv = _jax.random.split(_jax.random.fold_in(key, i + 1))
        out[name] = w
        out["m_" + name] = s * _jax.random.normal(km, w.shape, _jnp.float32)
        out["v_" + name] = (s * s) * _jax.random.uniform(kv, w.shape, _jnp.float32, 0.5, 1.5)
    if N_MICROBATCH > 1:
        for name, axis in PER_EXAMPLE_BATCH_AXIS.items():
            out[name] = _to_microbatches(out[name], axis)
    return {'x': out['x'], 'lb_logits': out['lb_logits'], 'hg_w_in': out['hg_w_in'], 'hg_norm_g': out['hg_norm_g'], 'hg_w_out': out['hg_w_out'], 'sg_w_in': out['sg_w_in'], 'sg_ln_g': out['sg_ln_g'], 'sg_ln_b': out['sg_ln_b'], 'sg_w_s': out['sg_w_s'], 'sg_b_s': out['sg_b_s'], 'sg_w_out': out['sg_w_out'], 'ffn_w_up': out['ffn_w_up'], 'ffn_conv_w': out['ffn_conv_w'], 'ffn_conv_b': out['ffn_conv_b'], 'ffn_w_down': out['ffn_w_down'], 'ln1_g': out['ln1_g'], 'ln1_b': out['ln1_b'], 'ln2_g': out['ln2_g'], 'ln2_b': out['ln2_b'], 'loss_target': out['loss_target'], 'm_lb_logits': out['m_lb_logits'], 'm_hg_w_in': out['m_hg_w_in'], 'm_hg_norm_g': out['m_hg_norm_g'], 'm_hg_w_out': out['m_hg_w_out'], 'm_sg_w_in': out['m_sg_w_in'], 'm_sg_ln_g': out['m_sg_ln_g'], 'm_sg_ln_b': out['m_sg_ln_b'], 'm_sg_w_s': out['m_sg_w_s'], 'm_sg_b_s': out['m_sg_b_s'], 'm_sg_w_out': out['m_sg_w_out'], 'm_ffn_w_up': out['m_ffn_w_up'], 'm_ffn_conv_w': out['m_ffn_conv_w'], 'm_ffn_conv_b': out['m_ffn_conv_b'], 'm_ffn_w_down': out['m_ffn_w_down'], 'm_ln1_g': out['m_ln1_g'], 'm_ln1_b': out['m_ln1_b'], 'm_ln2_g': out['m_ln2_g'], 'm_ln2_b': out['m_ln2_b'], 'v_lb_logits': out['v_lb_logits'], 'v_hg_w_in': out['v_hg_w_in'], 'v_hg_norm_g': out['v_hg_norm_g'], 'v_hg_w_out': out['v_hg_w_out'], 'v_sg_w_in': out['v_sg_w_in'], 'v_sg_ln_g': out['v_sg_ln_g'], 'v_sg_ln_b': out['v_sg_ln_b'], 'v_sg_w_s': out['v_sg_w_s'], 'v_sg_b_s': out['v_sg_b_s'], 'v_sg_w_out': out['v_sg_w_out'], 'v_ffn_w_up': out['v_ffn_w_up'], 'v_ffn_conv_w': out['v_ffn_conv_w'], 'v_ffn_conv_b': out['v_ffn_conv_b'], 'v_ffn_w_down': out['v_ffn_w_down'], 'v_ln1_g': out['v_ln1_g'], 'v_ln1_b': out['v_ln1_b'], 'v_ln2_g': out['v_ln2_g'], 'v_ln2_b': out['v_ln2_b']}


def _loss(weights, diff, rest, loss_target):
    with _jax.named_scope("forward"):
        args = {**rest, TWIN_DIFF_INPUT: diff, **{k: w.astype(_WEIGHT_DTYPES[k]) for k, w in weights.items()}}
        y = _forward(args)
    with _jax.named_scope("loss_head"):
        err = _jnp.square(y.astype(_jnp.float32) - loss_target)
        return 0.5 * _jnp.sum(_jnp.mean(err, axis=-1)) if err.ndim else 0.5 * err


def _adamw(w, g, m, v):
    m = ADAM_B1 * m + (1.0 - ADAM_B1) * g
    v = ADAM_B2 * v + (1.0 - ADAM_B2) * _jnp.square(g)
    m_hat = m / (1.0 - ADAM_B1 ** ADAM_STEP)
    v_hat = v / (1.0 - ADAM_B2 ** ADAM_STEP)
    delta = -ADAM_LR * (m_hat / (_jnp.sqrt(v_hat) + ADAM_EPS) + ADAM_WD * w)
    return delta, m, v


def reference(x, lb_logits, hg_w_in, hg_norm_g, hg_w_out, sg_w_in, sg_ln_g, sg_ln_b, sg_w_s, sg_b_s, sg_w_out, ffn_w_up, ffn_conv_w, ffn_conv_b, ffn_w_down, ln1_g, ln1_b, ln2_g, ln2_b, loss_target, m_lb_logits, m_hg_w_in, m_hg_norm_g, m_hg_w_out, m_sg_w_in, m_sg_ln_g, m_sg_ln_b, m_sg_w_s, m_sg_b_s, m_sg_w_out, m_ffn_w_up, m_ffn_conv_w, m_ffn_conv_b, m_ffn_w_down, m_ln1_g, m_ln1_b, m_ln2_g, m_ln2_b, v_lb_logits, v_hg_w_in, v_hg_norm_g, v_hg_w_out, v_sg_w_in, v_sg_ln_g, v_sg_ln_b, v_sg_w_s, v_sg_b_s, v_sg_w_out, v_ffn_w_up, v_ffn_conv_w, v_ffn_conv_b, v_ffn_w_down, v_ln1_g, v_ln1_b, v_ln2_g, v_ln2_b):
    given = dict(x=x, lb_logits=lb_logits, hg_w_in=hg_w_in, hg_norm_g=hg_norm_g, hg_w_out=hg_w_out, sg_w_in=sg_w_in, sg_ln_g=sg_ln_g, sg_ln_b=sg_ln_b, sg_w_s=sg_w_s, sg_b_s=sg_b_s, sg_w_out=sg_w_out, ffn_w_up=ffn_w_up, ffn_conv_w=ffn_conv_w, ffn_conv_b=ffn_conv_b, ffn_w_down=ffn_w_down, ln1_g=ln1_g, ln1_b=ln1_b, ln2_g=ln2_g, ln2_b=ln2_b, loss_target=loss_target, m_lb_logits=m_lb_logits, m_hg_w_in=m_hg_w_in, m_hg_norm_g=m_hg_norm_g, m_hg_w_out=m_hg_w_out, m_sg_w_in=m_sg_w_in, m_sg_ln_g=m_sg_ln_g, m_sg_ln_b=m_sg_ln_b, m_sg_w_s=m_sg_w_s, m_sg_b_s=m_sg_b_s, m_sg_w_out=m_sg_w_out, m_ffn_w_up=m_ffn_w_up, m_ffn_conv_w=m_ffn_conv_w, m_ffn_conv_b=m_ffn_conv_b, m_ffn_w_down=m_ffn_w_down, m_ln1_g=m_ln1_g, m_ln1_b=m_ln1_b, m_ln2_g=m_ln2_g, m_ln2_b=m_ln2_b, v_lb_logits=v_lb_logits, v_hg_w_in=v_hg_w_in, v_hg_norm_g=v_hg_norm_g, v_hg_w_out=v_hg_w_out, v_sg_w_in=v_sg_w_in, v_sg_ln_g=v_sg_ln_g, v_sg_ln_b=v_sg_ln_b, v_sg_w_s=v_sg_w_s, v_sg_b_s=v_sg_b_s, v_sg_w_out=v_sg_w_out, v_ffn_w_up=v_ffn_w_up, v_ffn_conv_w=v_ffn_conv_w, v_ffn_conv_b=v_ffn_conv_b, v_ffn_w_down=v_ffn_w_down, v_ln1_g=v_ln1_g, v_ln1_b=v_ln1_b, v_ln2_g=v_ln2_g, v_ln2_b=v_ln2_b)
    weights = {n: given[n] for n in TWIN_WEIGHTS}
    shared = {n: given[n] for n in SHARED_INPUTS}
    per_example = {n: given[n] for n in ['x']}
    grad_fn = _jax.value_and_grad(_loss, argnums=(0, 1))

    def one_microbatch(ex, loss_target):
        ex = dict(ex)
        diff = ex.pop(TWIN_DIFF_INPUT)
        return grad_fn(weights, diff, {**shared, **ex}, loss_target)

    if N_MICROBATCH == 1:
        loss, (grad_w, grad_x) = one_microbatch(per_example, given["loss_target"])
    else:
        def body(carry, xs):
            loss_sum, grad_sum = carry
            l_k, (gw_k, gx_k) = one_microbatch(xs[0], xs[1])
            with _jax.named_scope("update"):
                return (loss_sum + l_k, _jax.tree.map(_jnp.add, grad_sum, gw_k)), gx_k

        init = (_jnp.zeros((), _jnp.float32), _jax.tree.map(_jnp.zeros_like, weights))
        (loss, grad_w), grad_x = _jax.lax.scan(body, init, (per_example, given["loss_target"]))
    with _jax.named_scope("update"):
        delta_w, new_m, new_v = {}, {}, {}
        for n in TWIN_WEIGHTS:
            delta_w[n], new_m[n], new_v[n] = _adamw(weights[n], grad_w[n], given["m_" + n], given["v_" + n])
    return (loss, grad_x, *[grad_w[n] for n in TWIN_WEIGHTS], *[delta_w[n] for n in TWIN_WEIGHTS],
            *[new_m[n] for n in TWIN_WEIGHTS], *[new_v[n] for n in TWIN_WEIGHTS])
```

```python
import functools

import jax
import jax.numpy as jnp
from jax import lax
from jax.experimental import pallas as pl
from jax.experimental.pallas import tpu as pltpu

F32 = jnp.float32
BF16 = jnp.bfloat16
HI = lax.Precision.HIGHEST
MESH = pl.DeviceIdType.MESH

ALPHA = (2 * 2) ** 0.25
LN_EPS = 1e-5
RMS_EPS = 1e-6
ADAM_LR, ADAM_B1, ADAM_B2, ADAM_EPS, ADAM_WD, ADAM_STEP = 0.001, 0.9, 0.999, 1e-08, 0.01, 10

LANES = 128
SUB = 16
GCHUNK = 128
VMEM_LIMIT = 56 * 1024 * 1024
N_CHIPS = 4
N_DEV = 8

NT = (((1,), (1,)), ((), ()))
TN = (((0,), (0,)), ((), ()))
NN = (((1,), (0,)), ((), ()))


def _pick(dim, prefs):
    for p in prefs:
        if dim % p == 0:
            return p
    return dim


def _params(sem=None, **kw):
    return pltpu.CompilerParams(dimension_semantics=sem, vmem_limit_bytes=VMEM_LIMIT, **kw)


def _sigmoid_pair(x):
    e = jnp.exp(-jnp.abs(x))
    inv = 1.0 / (1.0 + e)
    pos = x >= 0
    return jnp.where(pos, inv, e * inv), jnp.where(pos, e * inv, inv)


def _ln_hat(x):
    mu = jnp.mean(x, axis=-1, keepdims=True)
    xc = x - mu
    var = jnp.mean(xc * xc, axis=-1, keepdims=True)
    rstd = lax.rsqrt(var + LN_EPS)
    return xc * rstd, rstd


def _lower_bound(logits):
    m = jnp.max(logits, axis=0, keepdims=True)
    e = jnp.exp(logits - m)
    return e[0:1, :] / jnp.sum(e, axis=0, keepdims=True)


def _matmul(a, b, *, mode, name, out_dtype=F32, resid=None, alpha=1.0, b_off=0, nsh=None,
            out_init=None, out_off=0, out_shards=None):
    if mode == "nn":
        m, kdim = a.shape
        _, _, ns = b.shape
        bm = _pick(m, (1024, 512, 256, 128))
        bn = _pick(ns, (1024, 1408, 512, 256, 128))
        bk = _pick(kdim, (512, 256, 128))
        nps = ns // bn
        grid = (m // bm, nsh * nps, kdim // bk)
        a_spec = pl.BlockSpec((bm, bk), lambda i, j, k: (i, k))
        b_spec = pl.BlockSpec((None, bk, bn), lambda i, j, k: (b_off + j // nps, k, j % nps))
        o_spec = pl.BlockSpec((bm, bn), lambda i, j, k: (i, j))
        out_shape = jax.ShapeDtypeStruct((m, nsh * ns), out_dtype)
        dims = NN
    elif mode == "nt":
        m = a.shape[0]
        _, kdim, ns = b.shape
        bm = _pick(m, (1024, 512, 256, 128))
        bn = _pick(kdim, (1024, 1408, 512, 256, 128))
        bk = _pick(ns, (512, 1408, 256, 128))
        kps = ns // bk
        grid = (m // bm, kdim // bn, nsh * kps)
        a_spec = pl.BlockSpec((bm, bk), lambda i, j, k: (i, k))
        b_spec = pl.BlockSpec((None, bn, bk), lambda i, j, k: (b_off + k // kps, j, k % kps))
        o_spec = pl.BlockSpec((bm, bn), lambda i, j, k: (i, j))
        out_shape = jax.ShapeDtypeStruct((m, kdim), out_dtype)
        dims = NT
    else:
        t, kdim = a.shape
        ns = b.shape[1] // nsh
        bm = _pick(kdim, (1024, 1408, 512, 256, 128))
        bn = _pick(ns, (1024, 1408, 512, 256, 128))
        bk = _pick(t, (512, 256, 128))
        nps = ns // bn
        grid = (kdim // bm, nsh * nps, t // bk)
        a_spec = pl.BlockSpec((bk, bm), lambda i, j, k: (k, i))
        b_spec = pl.BlockSpec((bk, bn), lambda i, j, k: (k, j))
        o_spec = pl.BlockSpec((None, bm, bn), lambda i, j, k: (out_off + j // nps, i, j % nps))
        out_shape = jax.ShapeDtypeStruct((out_shards, kdim, ns), out_dtype)
        dims = TN
    nk = grid[2]
    has_resid = resid is not None
    has_init = out_init is not None

    def kern(*refs):
        a_ref, b_ref = refs[0], refs[1]
        r_ref = refs[2] if has_resid else None
        o_ref, acc_ref = refs[-2], refs[-1]
        k = pl.program_id(2)

        @pl.when(k == 0)
        def _():
            acc_ref[...] = jnp.zeros_like(acc_ref)

        acc_ref[...] += lax.dot_general(a_ref[...], b_ref[...], dims, preferred_element_type=F32)

        @pl.when(k == nk - 1)
        def _():
            r = acc_ref[...]
            if has_resid:
                r = r + alpha * r_ref[...]
            o_ref[...] = r.astype(o_ref.dtype)

    in_specs = [a_spec, b_spec]
    operands = [a, b]
    if has_resid:
        in_specs.append(pl.BlockSpec((bm, bn), lambda i, j, k: (i, j)))
        operands.append(resid)
    aliases = {}
    if has_init:
        in_specs.append(pl.BlockSpec(memory_space=pl.ANY))
        operands.append(out_init)
        aliases = {len(operands) - 1: 0}
    return pl.pallas_call(
        kern, name=name, grid=grid, in_specs=in_specs, out_specs=o_spec, out_shape=out_shape,
        scratch_shapes=[pltpu.VMEM((bm, bn), F32)], input_output_aliases=aliases,
        compiler_params=_params(("parallel", "parallel", "arbitrary")),
    )(*operands)


def _res_ln_fwd(h_prev, sub, g, b, name):
    t, d = h_prev.shape
    tb = _pick(t, (256, 128, 64, 32, 16))

    def kern(hp_ref, s_ref, g_ref, b_ref, xin_ref, h_ref, hb_ref):
        xin = ALPHA * hp_ref[...] + s_ref[...]
        xhat, _ = _ln_hat(xin)
        h = xhat * g_ref[...] + b_ref[...]
        xin_ref[...] = xin
        h_ref[...] = h
        hb_ref[...] = h.astype(BF16)

    row = pl.BlockSpec((tb, d), lambda i: (i, 0))
    vec = pl.BlockSpec((1, d), lambda i: (0, 0))
    return pl.pallas_call(
        kern, name=name, grid=(t // tb,), in_specs=[row, row, vec, vec], out_specs=[row, row, row],
        out_shape=[jax.ShapeDtypeStruct((t, d), F32), jax.ShapeDtypeStruct((t, d), F32),
                   jax.ShapeDtypeStruct((t, d), BF16)],
        compiler_params=_params(("parallel",)),
    )(h_prev, sub, g, b)


def _ln_bwd(xin, dy_or_target, g, b, name, loss_head=False):
    t, d = xin.shape
    tb = _pick(t, (256, 128, 64, 32, 16))
    nb = t // tb

    def kern(x_ref, dy_ref, g_ref, b_ref, dx_ref, dxb_ref, dg_ref, db_ref, *rest):
        i = pl.program_id(0)
        xhat, rstd = _ln_hat(x_ref[...])
        gv = g_ref[...]
        if loss_head:
            loss_ref = rest[0]
            err = xhat * gv + b_ref[...] - dy_ref[...]
            dy = err * (1.0 / d)
            part = 0.5 * jnp.sum(jnp.sum(err * err, axis=1, keepdims=True), axis=0, keepdims=True) * (1.0 / d)
        else:
            dy = dy_ref[...]

        @pl.when(i == 0)
        def _():
            dg_ref[...] = jnp.zeros_like(dg_ref)
            db_ref[...] = jnp.zeros_like(db_ref)
            if loss_head:
                loss_ref[...] = jnp.zeros_like(loss_ref)

        dg_ref[...] += jnp.sum(dy * xhat, axis=0, keepdims=True)
        db_ref[...] += jnp.sum(dy, axis=0, keepdims=True)
        if loss_head:
            loss_ref[...] += jnp.broadcast_to(part, loss_ref.shape)
        dxh = dy * gv
        m1 = jnp.mean(dxh, axis=-1, keepdims=True)
        m2 = jnp.mean(dxh * xhat, axis=-1, keepdims=True)
        dx = rstd * (dxh - m1 - xhat * m2)
        dx_ref[...] = dx
        dxb_ref[...] = dx.astype(BF16)

    row = pl.BlockSpec((tb, d), lambda i: (i, 0))
    vec = pl.BlockSpec((1, d), lambda i: (0, 0))
    out_specs = [row, row, vec, vec]
    out_shape = [jax.ShapeDtypeStruct((t, d), F32), jax.ShapeDtypeStruct((t, d), BF16),
                 jax.ShapeDtypeStruct((1, d), F32), jax.ShapeDtypeStruct((1, d), F32)]
    if loss_head:
        out_specs.append(pl.BlockSpec((1, LANES), lambda i: (0, 0)))
        out_shape.append(jax.ShapeDtypeStruct((1, LANES), F32))
    return pl.pallas_call(
        kern, name=name, grid=(nb,), in_specs=[row, row, vec, vec], out_specs=out_specs, out_shape=out_shape,
        compiler_params=_params(("arbitrary",)),
    )(xin, dy_or_target, g, b)


def _conv_gate_fwd(u, conv_w, conv_b, name):
    t, f2 = u.shape
    f = f2 // 2
    tb = _pick(t, (512, 256, 128, 64, 32, 16))
    cn = _pick(f, (1408, 1024, 512, 256, 128))
    ncb = f // cn
    hb = tb // 8

    def kern(a_ref, ah_ref, b_ref, w_ref, cb_ref, o_ref):
        i = pl.program_id(0)
        a = a_ref[...]
        halo = jnp.where(i > 0, ah_ref[...], 0.0)
        rid = lax.broadcasted_iota(jnp.int32, a.shape, 0)
        s1 = jnp.where(rid == 0, halo[7:8, :], pltpu.roll(a, 1, 0))
        s2 = jnp.where(rid == 0, halo[6:7, :], jnp.where(rid == 1, halo[7:8, :], pltpu.roll(a, 2, 0)))
        w = w_ref[...]
        conv = w[2:3, :] * a + w[1:2, :] * s1 + w[0:1, :] * s2 + cb_ref[...]
        sp, _ = _sigmoid_pair(conv)
        o_ref[...] = (conv * sp * b_ref[...]).astype(BF16)

    return pl.pallas_call(
        kern, name=name, grid=(t // tb, ncb),
        in_specs=[pl.BlockSpec((tb, cn), lambda i, j: (i, j)),
                  pl.BlockSpec((8, cn), lambda i, j: (jnp.maximum(i * hb - 1, 0), j)),
                  pl.BlockSpec((tb, cn), lambda i, j: (i, j + ncb)),
                  pl.BlockSpec((3, cn), lambda i, j: (0, j)),
                  pl.BlockSpec((1, cn), lambda i, j: (0, j))],
        out_specs=pl.BlockSpec((tb, cn), lambda i, j: (i, j)),
        out_shape=jax.ShapeDtypeStruct((t, f), BF16),
        compiler_params=_params(("parallel", "parallel")),
    )(u, u, u, conv_w, conv_b)


def _conv_gate_bwd(u, dgact, conv_w, conv_b, name):
    t, f2 = u.shape
    f = f2 // 2
    tb = _pick(t, (512, 256, 128, 64, 32, 16))
    cn = _pick(f, (1408, 1024, 512, 256, 128))
    ncb = f // cn
    hb = tb // 8
    nb = t // tb
    last8 = t // 8 - 1

    def kern(a_ref, ap_ref, an_ref, b_ref, bn_ref, dg_ref, dgn_ref, w_ref, cb_ref,
             da_ref, db_ref, dw_ref, dcb_ref):
        i = pl.program_id(1)
        a = a_ref[...]
        w = w_ref[...]
        ext = jnp.concatenate([jnp.where(i > 0, ap_ref[...], 0.0), a, an_ref[...]], axis=0)
        e1 = pltpu.roll(ext, 1, 0)
        e2 = pltpu.roll(ext, 2, 0)
        conv = (w[2:3, :] * ext + w[1:2, :] * e1 + w[0:1, :] * e2 + cb_ref[...])[8:, :]
        bmn = jnp.concatenate([b_ref[...], bn_ref[...]], axis=0)
        dgmn = jnp.concatenate([dg_ref[...], jnp.where(i < nb - 1, dgn_ref[...], 0.0)], axis=0)
        sp, sn = _sigmoid_pair(conv)
        da = dgmn * bmn * (sp * (1.0 + conv * sn))
        n = tb + 8
        dap = w[2:3, :] * da + w[1:2, :] * pltpu.roll(da, n - 1, 0) + w[0:1, :] * pltpu.roll(da, n - 2, 0)
        da_ref[...] = dap[:tb, :].astype(BF16)
        db_ref[...] = (dg_ref[...] * (conv * sp)[:tb, :]).astype(BF16)
        dam = da[:tb, :]

        @pl.when(i == 0)
        def _():
            dw_ref[...] = jnp.zeros_like(dw_ref)
            dcb_ref[...] = jnp.zeros_like(dcb_ref)

        dw = jnp.concatenate([jnp.sum(dam * e2[8:8 + tb, :], axis=0, keepdims=True),
                              jnp.sum(dam * e1[8:8 + tb, :], axis=0, keepdims=True),
                              jnp.sum(dam * a, axis=0, keepdims=True)], axis=0)
        dw_ref[...] += dw
        dcb_ref[...] += jnp.sum(dam, axis=0, keepdims=True)

    main_a = pl.BlockSpec((tb, cn), lambda j, i: (i, j))
    prev_a = pl.BlockSpec((8, cn), lambda j, i: (jnp.maximum(i * hb - 1, 0), j))
    next_a = pl.BlockSpec((8, cn), lambda j, i: (jnp.minimum((i + 1) * hb, last8), j))
    main_b = pl.BlockSpec((tb, cn), lambda j, i: (i, j + ncb))
    next_b = pl.BlockSpec((8, cn), lambda j, i: (jnp.minimum((i + 1) * hb, last8), j + ncb))
    return pl.pallas_call(
        kern, name=name, grid=(ncb, nb),
        in_specs=[main_a, prev_a, next_a, main_b, next_b, main_a, next_a,
                  pl.BlockSpec((3, cn), lambda j, i: (0, j)), pl.BlockSpec((1, cn), lambda j, i: (0, j))],
        out_specs=[main_a, main_a, pl.BlockSpec((3, cn), lambda j, i: (0, j)),
                   pl.BlockSpec((1, cn), lambda j, i: (0, j))],
        out_shape=[jax.ShapeDtypeStruct((t, f), BF16), jax.ShapeDtypeStruct((t, f), BF16),
                   jax.ShapeDtypeStruct((3, f), F32), jax.ShapeDtypeStruct((1, f), F32)],
        compiler_params=_params(("parallel", "arbitrary")),
    )(u, u, u, u, u, dgact, dgact, conv_w, conv_b)


def _hg_gates(qp, fp, lb):
    sq, _ = _sigmoid_pair(qp)
    sf, snf = _sigmoid_pair(fp)
    forget = lb + (1.0 - lb) * sf
    return sq, sf, snf, forget, jnp.log(forget), (1.0 - lb) * snf


def _tri(lower):
    r = lax.broadcasted_iota(jnp.int32, (SUB, SUB), 0)
    c = lax.broadcasted_iota(jnp.int32, (SUB, SUB), 1)
    return ((r >= c) if lower else (r <= c)).astype(F32)


def _hgrn2_fwd(proj, lb_logits, norm_g, name):
    t, d4 = proj.shape
    d = d4 // 4
    nh = d // LANES
    tb = _pick(t, (256, 128, 64, 32, 16))
    nb = t // tb
    nsc = tb // SUB

    def kern(q_ref, f_ref, i_ref, g_ref, lbl_ref, ng_ref, y_ref, o_ref, st_ref, s_ref):
        @pl.when(pl.program_id(1) == 0)
        def _():
            s_ref[...] = jnp.zeros_like(s_ref)

        lb = _lower_bound(lbl_ref[...])
        ng = ng_ref[...]
        ltri = _tri(True)
        rcol = lax.broadcasted_iota(jnp.int32, (SUB, 1), 0)

        def step(sc, carry):
            rows = pl.ds(pl.multiple_of(sc * SUB, SUB), SUB)
            qp, fp, v, gp = q_ref[rows, :], f_ref[rows, :], i_ref[rows, :], g_ref[rows, :]
            sq, _, _, _, lf, k = _hg_gates(qp, fp, lb)
            q = qp * sq
            bl = jnp.dot(ltri, lf, precision=HI, preferred_element_type=F32)
            state = s_ref[...]
            st_ref[sc] = state
            o = lax.dot_general(q * jnp.exp(bl), state, NT, precision=HI, preferred_element_type=F32)
            for s in range(SUB):
                e = jnp.exp(jnp.minimum(bl - bl[s:s + 1, :], 0.0))
                a = jnp.sum(q * e * k[s:s + 1, :], axis=1, keepdims=True)
                o = o + jnp.where(rcol >= s, a, 0.0) * v[s:s + 1, :]
            bend = bl[SUB - 1:SUB, :]
            kd = k * jnp.exp(bend - bl)
            s_ref[...] = state * jnp.exp(bend) + lax.dot_general(v, kd, TN, precision=HI, preferred_element_type=F32)
            o_ref[rows, :] = o
            r = lax.rsqrt(jnp.mean(o * o, axis=1, keepdims=True) + RMS_EPS)
            sg, _ = _sigmoid_pair(gp)
            y_ref[rows, :] = (o * r * ng * (gp * sg)).astype(BF16)
            return carry

        lax.fori_loop(0, nsc, step, 0)

    def col(off):
        return pl.BlockSpec((tb, LANES), lambda h, j: (j, h + off * nh))

    return pl.pallas_call(
        kern, name=name, grid=(nh, nb),
        in_specs=[col(0), col(1), col(2), col(3),
                  pl.BlockSpec((3, LANES), lambda h, j: (0, h)), pl.BlockSpec((1, LANES), lambda h, j: (0, h))],
        out_specs=[col(0), col(0), pl.BlockSpec((nsc, None, LANES, LANES), lambda h, j: (j, h, 0, 0))],
        out_shape=[jax.ShapeDtypeStruct((t, d), BF16), jax.ShapeDtypeStruct((t, d), F32),
                   jax.ShapeDtypeStruct((t // SUB, nh, LANES, LANES), F32)],
        scratch_shapes=[pltpu.VMEM((LANES, LANES), F32)],
        compiler_params=_params(("parallel", "arbitrary")),
    )(proj, proj, proj, proj, lb_logits, norm_g)


def _hgrn2_bwd(proj, lb_logits, norm_g, o_raw, states, dy, name):
    t, d4 = proj.shape
    d = d4 // 4
    nh = d // LANES
    tb = _pick(t, (256, 128, 64, 32, 16))
    nb = t // tb
    nsc = tb // SUB

    def kern(q_ref, f_ref, i_ref, g_ref, lbl_ref, ng_ref, o_ref, st_ref, dy_ref,
             dq_ref, df_ref, di_ref, dgp_ref, dlb_ref, dng_ref, ds_ref, gc_ref):
        j = pl.program_id(1)

        @pl.when(j == 0)
        def _():
            ds_ref[...] = jnp.zeros_like(ds_ref)
            gc_ref[...] = jnp.zeros_like(gc_ref)
            dlb_ref[...] = jnp.zeros_like(dlb_ref)
            dng_ref[...] = jnp.zeros_like(dng_ref)

        lb = _lower_bound(lbl_ref[...])
        ng = ng_ref[...]
        ltri, utri = _tri(True), _tri(False)
        rcol = lax.broadcasted_iota(jnp.int32, (SUB, 1), 0)
        rid = lax.broadcasted_iota(jnp.int32, (SUB, LANES), 0)

        def step(it, carry):
            sc = nsc - 1 - it
            rows = pl.ds(pl.multiple_of(sc * SUB, SUB), SUB)
            qp, fp, v, gp = q_ref[rows, :], f_ref[rows, :], i_ref[rows, :], g_ref[rows, :]
            sq, sf, snf, forget, lf, k = _hg_gates(qp, fp, lb)
            q = qp * sq
            bl = jnp.dot(ltri, lf, precision=HI, preferred_element_type=F32)
            ebl = jnp.exp(bl)
            qs = q * ebl
            bend = bl[SUB - 1:SUB, :]
            dte = jnp.exp(bend - bl)
            kd = k * dte
            state = st_ref[sc]
            dstate = ds_ref[...]
            o = o_ref[rows, :]
            r = lax.rsqrt(jnp.mean(o * o, axis=1, keepdims=True) + RMS_EPS)
            ohat = o * r
            sg, sng = _sigmoid_pair(gp)
            dyv = dy_ref[rows, :]
            don = dyv * (gp * sg)
            dgp_ref[rows, :] = (dyv * (ohat * ng) * (sg * (1.0 + gp * sng))).astype(BF16)
            dng_ref[...] += jnp.sum(don * ohat, axis=0, keepdims=True)
            doh = don * ng
            do = r * (doh - ohat * jnp.mean(doh * ohat, axis=1, keepdims=True))
            dq = jnp.dot(do, state, precision=HI, preferred_element_type=F32) * ebl
            dv = lax.dot_general(kd, dstate, NT, precision=HI, preferred_element_type=F32)
            dk = jnp.dot(v, dstate, precision=HI, preferred_element_type=F32) * dte
            dki = jnp.zeros((SUB, LANES), F32)
            dvi = jnp.zeros((SUB, LANES), F32)
            for s in range(SUB):
                e = jnp.exp(jnp.minimum(bl - bl[s:s + 1, :], 0.0))
                qe = q * e
                ks = k[s:s + 1, :]
                live = rcol >= s
                a = jnp.where(live, jnp.sum(qe * ks, axis=1, keepdims=True), 0.0)
                da = jnp.where(live, jnp.sum(do * v[s:s + 1, :], axis=1, keepdims=True), 0.0)
                dq = dq + da * (e * ks)
                dki = jnp.where(rid == s, jnp.sum(da * qe, axis=0, keepdims=True), dki)
                dvi = jnp.where(rid == s, jnp.sum(a * do, axis=0, keepdims=True), dvi)
            dk = dk + dki
            dv = dv + dvi
            ds_ref[...] = dstate * jnp.exp(bend) + lax.dot_general(do, qs, TN, precision=HI, preferred_element_type=F32)
            w = q * dq - k * dk
            gc = gc_ref[...]
            dlf = jnp.dot(utri, w, precision=HI, preferred_element_type=F32) + gc
            gc_ref[...] = gc + jnp.sum(w, axis=0, keepdims=True)
            t1 = dlf / forget - dk
            df_ref[rows, :] = ((1.0 - lb) * sf * snf * t1).astype(BF16)
            dlb_ref[...] += jnp.sum(snf * t1, axis=0, keepdims=True)
            dq_ref[rows, :] = (dq * (sq * (1.0 + qp * (1.0 - sq)))).astype(BF16)
            di_ref[rows, :] = dv.astype(BF16)
            return carry

        lax.fori_loop(0, nsc, step, 0)

    def col(off):
        return pl.BlockSpec((tb, LANES), lambda h, j: (nb - 1 - j, h + off * nh))

    vec = pl.BlockSpec((1, LANES), lambda h, j: (0, h))
    return pl.pallas_call(
        kern, name=name, grid=(nh, nb),
        in_specs=[col(0), col(1), col(2), col(3), pl.BlockSpec((3, LANES), lambda h, j: (0, h)), vec,
                  col(0), pl.BlockSpec((nsc, None, LANES, LANES), lambda h, j: (nb - 1 - j, h, 0, 0)), col(0)],
        out_specs=[col(0), col(0), col(0), col(0), vec, vec],
        out_shape=[jax.ShapeDtypeStruct((t, d), BF16)] * 4 + [jax.ShapeDtypeStruct((1, d), F32)] * 2,
        scratch_shapes=[pltpu.VMEM((LANES, LANES), F32), pltpu.VMEM((1, LANES), F32)],
        compiler_params=_params(("parallel", "arbitrary")),
    )(proj, proj, proj, proj, lb_logits, norm_g, o_raw, states, dy)


_INV_SQRT2 = 0.7071067811865476
_INV_SQRT2PI = 0.3989422804014327


def _gelu(x):
    return 0.5 * x * (1.0 + lax.erf(x * _INV_SQRT2))


def _gelu_grad(x):
    return 0.5 * (1.0 + lax.erf(x * _INV_SQRT2)) + x * jnp.exp(-0.5 * x * x) * _INV_SQRT2PI


def _causal(w):
    r = lax.broadcasted_iota(jnp.int32, (GCHUNK, GCHUNK), 0)
    c = lax.broadcasted_iota(jnp.int32, (GCHUNK, GCHUNK), 1)
    return jnp.where(r >= c, w, 0.0)


def _sg_gate_fwd(pre, ln_g, ln_b, w_s, b_s_t, name):
    t, d2 = pre.shape
    d = d2 // 2
    ng = d // LANES

    def kern(pre_ref, g_ref, b_ref, ws_ref, bs_ref, y_ref):
        z = _gelu(pre_ref[...])
        u = z[:, :d]
        vhat, _ = _ln_hat(z[:, d:])
        vn = (vhat * g_ref[...] + b_ref[...]).astype(BF16)
        bs = bs_ref[...]
        for g in range(ng):
            cols = slice(g * LANES, (g + 1) * LANES)
            wc = _causal(ws_ref[g]).astype(BF16)
            gate = jnp.dot(wc, vn[:, cols], preferred_element_type=F32) + bs[:, g:g + 1]
            y_ref[:, cols] = (u[:, cols] * gate).astype(BF16)

    vec = pl.BlockSpec((1, d), lambda i: (0, 0))
    return pl.pallas_call(
        kern, name=name, grid=(t // GCHUNK,),
        in_specs=[pl.BlockSpec((GCHUNK, d2), lambda i: (i, 0)), vec, vec,
                  pl.BlockSpec((ng, GCHUNK, GCHUNK), lambda i: (0, 0, 0)),
                  pl.BlockSpec((GCHUNK, ng), lambda i: (0, 0))],
        out_specs=pl.BlockSpec((GCHUNK, d), lambda i: (i, 0)),
        out_shape=jax.ShapeDtypeStruct((t, d), BF16),
        compiler_params=_params(("parallel",)),
    )(pre, ln_g, ln_b, w_s, b_s_t)


def _sg_gate_bwd(pre, dy, ln_g, ln_b, w_s, b_s_t, name):
    t, d2 = pre.shape
    d = d2 // 2
    ng = d // LANES

    def kern(pre_ref, dy_ref, g_ref, b_ref, ws_ref, bs_ref, dpre_ref, dws_ref, dbs_ref, dg_ref, db_ref, dvn_ref):
        @pl.when(pl.program_id(0) == 0)
        def _():
            dws_ref[...] = jnp.zeros_like(dws_ref)
            dbs_ref[...] = jnp.zeros_like(dbs_ref)
            dg_ref[...] = jnp.zeros_like(dg_ref)
            db_ref[...] = jnp.zeros_like(db_ref)

        pre = pre_ref[...]
        z = _gelu(pre)
        u = z[:, :d]
        vhat, rstd = _ln_hat(z[:, d:])
        gv = g_ref[...]
        vn = (vhat * gv + b_ref[...]).astype(BF16)
        bs = bs_ref[...]
        dyv = dy_ref[...]
        gp = _gelu_grad(pre)
        lane = lax.broadcasted_iota(jnp.int32, (GCHUNK, ng), 1)
        dbs = jnp.zeros((GCHUNK, ng), F32)
        for g in range(ng):
            cols = slice(g * LANES, (g + 1) * LANES)
            wc = _causal(ws_ref[g]).astype(BF16)
            vng = vn[:, cols]
            gate = jnp.dot(wc, vng, preferred_element_type=F32) + bs[:, g:g + 1]
            dpre_ref[:, cols] = (dyv[:, cols] * gate * gp[:, cols]).astype(BF16)
            dgate = dyv[:, cols] * u[:, cols]
            dbs = dbs + jnp.where(lane == g, jnp.sum(dgate, axis=1, keepdims=True), 0.0)
            dgb = dgate.astype(BF16)
            dws_ref[g] += _causal(lax.dot_general(dgb, vng, NT, preferred_element_type=F32))
            dvn_ref[:, cols] = lax.dot_general(wc, dgb, TN, preferred_element_type=F32)
        dbs_ref[...] += dbs
        dvn = dvn_ref[...]
        dg_ref[...] += jnp.sum(dvn * vhat, axis=0, keepdims=True)
        db_ref[...] += jnp.sum(dvn, axis=0, keepdims=True)
        dvh = dvn * gv
        m1 = jnp.mean(dvh, axis=-1, keepdims=True)
        m2 = jnp.mean(dvh * vhat, axis=-1, keepdims=True)
        dpre_ref[:, d:] = (rstd * (dvh - m1 - vhat * m2) * gp[:, d:]).astype(BF16)

    vec = pl.BlockSpec((1, d), lambda i: (0, 0))
    wsp = pl.BlockSpec((ng, GCHUNK, GCHUNK), lambda i: (0, 0, 0))
    bsp = pl.BlockSpec((GCHUNK, ng), lambda i: (0, 0))
    return pl.pallas_call(
        kern, name=name, grid=(t // GCHUNK,),
        in_specs=[pl.BlockSpec((GCHUNK, d2), lambda i: (i, 0)), pl.BlockSpec((GCHUNK, d), lambda i: (i, 0)),
                  vec, vec, wsp, bsp],
        out_specs=[pl.BlockSpec((GCHUNK, d2), lambda i: (i, 0)), wsp, bsp, vec, vec],
        out_shape=[jax.ShapeDtypeStruct((t, d2), BF16), jax.ShapeDtypeStruct((ng, GCHUNK, GCHUNK), F32),
                   jax.ShapeDtypeStruct((GCHUNK, ng), F32), jax.ShapeDtypeStruct((1, d), F32),
                   jax.ShapeDtypeStruct((1, d), F32)],
        scratch_shapes=[pltpu.VMEM((GCHUNK, d), F32)],
        compiler_params=_params(("arbitrary",)),
    )(pre, dy, ln_g, ln_b, w_s, b_s_t)


def _adamw_math(w, g, m, v):
    m = ADAM_B1 * m + (1.0 - ADAM_B1) * g
    v = ADAM_B2 * v + (1.0 - ADAM_B2) * (g * g)
    m_hat = m / (1.0 - ADAM_B1 ** ADAM_STEP)
    v_hat = v / (1.0 - ADAM_B2 ** ADAM_STEP)
    return -ADAM_LR * (m_hat / (jnp.sqrt(v_hat) + ADAM_EPS) + ADAM_WD * w), m, v


def _adamw(w, g, m, v, name):
    shape = w.shape
    n = w.size
    width = 512 if n % (512 * 8) == 0 else LANES
    rows = n // width
    rb = _pick(rows, (1024, 512, 256, 128, 64, 32, 16, 8))
    ops = [a.reshape(rows, width) for a in (w, g, m, v)]

    def kern(w_ref, g_ref, m_ref, v_ref, d_ref, mo_ref, vo_ref):
        dlt, mm, vv = _adamw_math(w_ref[...], g_ref[...], m_ref[...], v_ref[...])
        d_ref[...] = dlt
        mo_ref[...] = mm
        vo_ref[...] = vv

    blk = pl.BlockSpec((rb, width), lambda i: (i, 0))
    outs = pl.pallas_call(
        kern, name=name, grid=(rows // rb,), in_specs=[blk] * 4, out_specs=[blk] * 3,
        out_shape=[jax.ShapeDtypeStruct((rows, width), F32)] * 3,
        compiler_params=_params(("parallel",)),
    )(*ops)
    return [o.reshape(shape) for o in outs]


def _lb_logits_grad(lb_logits, dlb, name):
    def kern(l_ref, d_ref, o_ref):
        lg = l_ref[...]
        m = jnp.max(lg, axis=0, keepdims=True)
        e = jnp.exp(lg - m)
        p = e / jnp.sum(e, axis=0, keepdims=True)
        row = lax.broadcasted_iota(jnp.int32, lg.shape, 0)
        o_ref[...] = d_ref[...] * p[0:1, :] * (jnp.where(row == 0, 1.0, 0.0) - p)

    return pl.pallas_call(kern, name=name, out_shape=jax.ShapeDtypeStruct(lb_logits.shape, F32))(lb_logits, dlb)


def _sum_leading(a, name):
    n, r, c = a.shape
    rb = _pick(r, (512, 256, 128, 64, 32, 16, 8))

    def kern(a_ref, o_ref):
        acc = a_ref[0]
        for i in range(1, n):
            acc = acc + a_ref[i]
        o_ref[...] = acc

    return pl.pallas_call(
        kern, name=name, grid=(r // rb,), in_specs=[pl.BlockSpec((n, rb, c), lambda i: (0, i, 0))],
        out_specs=pl.BlockSpec((rb, c), lambda i: (i, 0)), out_shape=jax.ShapeDtypeStruct((r, c), F32),
        compiler_params=_params(("parallel",)),
    )(a)


def _place():
    x, y, c = lax.axis_index("x"), lax.axis_index("y"), lax.axis_index("c")
    return x, y, c


def _chip_rel(x, y, r):
    px = x if r < 2 else 1 - x
    py = y if r % 2 == 0 else 1 - y
    return px, py, 2 * px + py


def _allgather_split(arrs, name):
    n = len(arrs)

    def body(*refs):
        ins, outs = refs[:n], refs[n:2 * n]
        send_sems, recv_sems, loc_sems = refs[2 * n:]
        x, y, c = _place()
        me = 2 * x + y
        sib = (x, y, 1 - c)

        def half(a, shard, hc):
            h = ins[a].shape[0] // 2
            return outs[a].at[shard, pl.ds(hc * h, h), :]

        def src_half(a):
            h = ins[a].shape[0] // 2
            return ins[a].at[pl.ds(c * h, h), :]

        def copy(a, slot, src, dst, to):
            return pltpu.make_async_remote_copy(src_ref=src, dst_ref=dst, send_sem=send_sems.at[a * 6 + slot],
                                                recv_sem=recv_sems.at[a * 6 + slot], device_id=to, device_id_type=MESH)

        local = [pltpu.make_async_copy(ins[a], outs[a].at[me], loc_sems.at[a]) for a in range(n)]
        for cp in local:
            cp.start()
        first = []
        for r in (1, 2, 3):
            px, py, _ = _chip_rel(x, y, r)
            for a in range(n):
                first.append(copy(a, r - 1, src_half(a), half(a, me, c), (px, py, c)))
        for cp in first:
            cp.start()
        passed = []
        for r in (1, 2, 3):
            _, _, shard = _chip_rel(x, y, r)
            for a in range(n):
                copy(a, r - 1, src_half(a), half(a, shard, c), sib).wait_recv()
                cp = copy(a, 3 + r - 1, half(a, shard, c), half(a, shard, c), sib)
                cp.start()
                passed.append(cp)
        for r in (1, 2, 3):
            _, _, shard = _chip_rel(x, y, r)
            for a in range(n):
                copy(a, 3 + r - 1, src_half(a), half(a, shard, 1 - c), sib).wait_recv()
        for cp in first + passed:
            cp.wait_send()
        for cp in local:
            cp.wait()

    anyspec = pl.BlockSpec(memory_space=pl.ANY)
    return pl.pallas_call(
        body, name=name, in_specs=[anyspec] * n, out_specs=[anyspec] * n,
        out_shape=[jax.ShapeDtypeStruct((N_CHIPS,) + a.shape, a.dtype) for a in arrs],
        scratch_shapes=[pltpu.SemaphoreType.DMA((6 * n,)), pltpu.SemaphoreType.DMA((6 * n,)),
                        pltpu.SemaphoreType.DMA((n,))],
        compiler_params=pltpu.CompilerParams(has_side_effects=True),
    )(*arrs)


def _allgather_whole(arr, name):
    def body(in_ref, out_ref, send_sems, recv_sems, loc_sem):
        x, y, c = _place()
        me = 2 * x + y
        local = pltpu.make_async_copy(in_ref, out_ref.at[me], loc_sem)
        local.start()
        sends = []
        for r in (1, 2, 3):
            px, py, _ = _chip_rel(x, y, r)
            sends.append(pltpu.make_async_remote_copy(
                src_ref=in_ref, dst_ref=out_ref.at[me], send_sem=send_sems.at[r - 1], recv_sem=recv_sems.at[r - 1],
                device_id=(px, py, c), device_id_type=MESH))
        for cp in sends:
            cp.start()
        for r in (1, 2, 3):
            px, py, shard = _chip_rel(x, y, r)
            pltpu.make_async_remote_copy(
                src_ref=in_ref, dst_ref=out_ref.at[shard], send_sem=send_sems.at[r - 1], recv_sem=recv_sems.at[r - 1],
                device_id=(px, py, c), device_id_type=MESH).wait_recv()
        for cp in sends:
            cp.wait_send()
        local.wait()

    anyspec = pl.BlockSpec(memory_space=pl.ANY)
    return pl.pallas_call(
        body, name=name, in_specs=[anyspec], out_specs=anyspec,
        out_shape=jax.ShapeDtypeStruct((N_CHIPS,) + arr.shape, arr.dtype),
        scratch_shapes=[pltpu.SemaphoreType.DMA((3,)), pltpu.SemaphoreType.DMA((3,)), pltpu.SemaphoreType.DMA],
        compiler_params=pltpu.CompilerParams(has_side_effects=True),
    )(arr)


def _swap_halves(grads, name):
    n = len(grads)

    def body(*refs):
        ins, outs = refs[:n], refs[n:2 * n]
        send_sems, recv_sems = refs[2 * n:]
        x, y, c = _place()
        sib = (x, y, 1 - c)
        copies = []
        for a in range(n):
            for j in range(N_CHIPS):
                copies.append(pltpu.make_async_remote_copy(
                    src_ref=ins[a].at[j, 1 - c], dst_ref=outs[a].at[j], send_sem=send_sems.at[a * N_CHIPS + j],
                    recv_sem=recv_sems.at[a * N_CHIPS + j], device_id=sib, device_id_type=MESH))
        for cp in copies:
            cp.start()
        for cp in copies:
            cp.wait()

    anyspec = pl.BlockSpec(memory_space=pl.ANY)
    return pl.pallas_call(
        body, name=name, in_specs=[anyspec] * n, out_specs=[anyspec] * n,
        out_shape=[jax.ShapeDtypeStruct((N_CHIPS,) + g.shape[2:], g.dtype) for g in grads],
        scratch_shapes=[pltpu.SemaphoreType.DMA((N_CHIPS * n,)), pltpu.SemaphoreType.DMA((N_CHIPS * n,))],
        compiler_params=pltpu.CompilerParams(has_side_effects=True),
    )(*grads)


def _scatter_chips(parts, name):
    n = len(parts)

    def body(*refs):
        ins, outs = refs[:n], refs[n:2 * n]
        send_sems, recv_sems = refs[2 * n:]
        x, y, c = _place()
        copies = []
        for r in (1, 2, 3):
            px, py, shard = _chip_rel(x, y, r)
            for a in range(n):
                copies.append(pltpu.make_async_remote_copy(
                    src_ref=ins[a].at[shard], dst_ref=outs[a].at[r - 1], send_sem=send_sems.at[a * 3 + r - 1],
                    recv_sem=recv_sems.at[a * 3 + r - 1], device_id=(px, py, c), device_id_type=MESH))
        for cp in copies:
            cp.start()
        for cp in copies:
            cp.wait()

    anyspec = pl.BlockSpec(memory_space=pl.ANY)
    return pl.pallas_call(
        body, name=name, in_specs=[anyspec] * n, out_specs=[anyspec] * n,
        out_shape=[jax.ShapeDtypeStruct((3,) + p.shape[1:], p.dtype) for p in parts],
        scratch_shapes=[pltpu.SemaphoreType.DMA((3 * n,)), pltpu.SemaphoreType.DMA((3 * n,))],
        compiler_params=pltpu.CompilerParams(has_side_effects=True),
    )(*parts)


def _join_halves(halves, name):
    n = len(halves)

    def body(*refs):
        ins, outs = refs[:n], refs[n:2 * n]
        send_sems, recv_sems, loc_sems = refs[2 * n:]
        x, y, c = _place()
        sib = (x, y, 1 - c)
        local = [pltpu.make_async_copy(ins[a], outs[a].at[c], loc_sems.at[a]) for a in range(n)]
        copies = [pltpu.make_async_remote_copy(
            src_ref=ins[a], dst_ref=outs[a].at[c], send_sem=send_sems.at[a], recv_sem=recv_sems.at[a],
            device_id=sib, device_id_type=MESH) for a in range(n)]
        for cp in local + copies:
            cp.start()
        for cp in copies + local:
            cp.wait()

    anyspec = pl.BlockSpec(memory_space=pl.ANY)
    return pl.pallas_call(
        body, name=name, in_specs=[anyspec] * n, out_specs=[anyspec] * n,
        out_shape=[jax.ShapeDtypeStruct((2,) + h.shape, h.dtype) for h in halves],
        scratch_shapes=[pltpu.SemaphoreType.DMA((n,)), pltpu.SemaphoreType.DMA((n,)), pltpu.SemaphoreType.DMA((n,))],
        compiler_params=pltpu.CompilerParams(has_side_effects=True),
    )(*halves)


def _gather_all_devices(buf, name):
    def body(in_ref, out_ref, send_sems, recv_sems, loc_sem):
        x, y, c = _place()
        me = 4 * x + 2 * y + c
        local = pltpu.make_async_copy(in_ref, out_ref.at[me], loc_sem)
        local.start()
        sends = []
        for r in range(1, N_DEV):
            px, py, _ = _chip_rel(x, y, r // 2)
            pc = c if r % 2 == 0 else 1 - c
            sends.append(pltpu.make_async_remote_copy(
                src_ref=in_ref, dst_ref=out_ref.at[me], send_sem=send_sems.at[r - 1], recv_sem=recv_sems.at[r - 1],
                device_id=(px, py, pc), device_id_type=MESH))
        for cp in sends:
            cp.start()
        for r in range(1, N_DEV):
            px, py, _ = _chip_rel(x, y, r // 2)
            pc = c if r % 2 == 0 else 1 - c
            pltpu.make_async_remote_copy(
                src_ref=in_ref, dst_ref=out_ref.at[4 * px + 2 * py + pc], send_sem=send_sems.at[r - 1],
                recv_sem=recv_sems.at[r - 1], device_id=(px, py, pc), device_id_type=MESH).wait_recv()
        for cp in sends:
            cp.wait_send()
        local.wait()

    anyspec = pl.BlockSpec(memory_space=pl.ANY)
    return pl.pallas_call(
        body, name=name, in_specs=[anyspec], out_specs=anyspec,
        out_shape=jax.ShapeDtypeStruct((N_DEV,) + buf.shape, buf.dtype),
        scratch_shapes=[pltpu.SemaphoreType.DMA((N_DEV - 1,)), pltpu.SemaphoreType.DMA((N_DEV - 1,)),
                        pltpu.SemaphoreType.DMA],
        compiler_params=pltpu.CompilerParams(has_side_effects=True),
    )(buf)


def _add_half(grad, recv, sel, name):
    _, _, rh, cw = grad.shape
    rb = _pick(rh, (512, 256, 176, 128, 64, 32, 16, 8))

    def kern(sel_ref, g_ref, r_ref, o_ref):
        o_ref[...] = g_ref[...] + r_ref[...]

    return pl.pallas_call(
        kern, name=name,
        grid_spec=pltpu.PrefetchScalarGridSpec(
            num_scalar_prefetch=1, grid=(N_CHIPS, rh // rb),
            in_specs=[pl.BlockSpec((None, None, rb, cw), lambda j, i, s: (j, s[0], i, 0)),
                      pl.BlockSpec((None, rb, cw), lambda j, i, s: (j, i, 0))],
            out_specs=pl.BlockSpec((None, rb, cw), lambda j, i, s: (j, i, 0))),
        out_shape=jax.ShapeDtypeStruct((N_CHIPS, rh, cw), F32),
        compiler_params=_params(("parallel", "parallel")),
    )(sel, grad, recv)


def _add_own(part, recv, sel, name):
    _, rh, cw = part.shape
    rb = _pick(rh, (512, 256, 176, 128, 64, 32, 16, 8))

    def kern(sel_ref, p_ref, r_ref, o_ref):
        o_ref[...] = ((p_ref[...] + r_ref[0]) + r_ref[1]) + r_ref[2]

    return pl.pallas_call(
        kern, name=name,
        grid_spec=pltpu.PrefetchScalarGridSpec(
            num_scalar_prefetch=1, grid=(rh // rb,),
            in_specs=[pl.BlockSpec((None, rb, cw), lambda i, s: (s[0], i, 0)),
                      pl.BlockSpec((3, rb, cw), lambda i, s: (0, i, 0))],
            out_specs=pl.BlockSpec((rb, cw), lambda i, s: (i, 0))),
        out_shape=jax.ShapeDtypeStruct((rh, cw), F32),
        compiler_params=_params(("parallel",)),
    )(sel, part, recv)


def _reduce_scatter(grads, tag):
    _, _, c = _place()
    x, y, _ = _place()
    sel_c = jnp.reshape(c, (1,)).astype(jnp.int32)
    sel_me = jnp.reshape(2 * x + y, (1,)).astype(jnp.int32)
    split = [g.reshape(N_CHIPS, 2, g.shape[1] // 2, g.shape[2]) for g in grads]
    recv = _swap_halves(split, f"{tag}_swap")
    parts = [_add_half(g, r, sel_c, f"{tag}_addhalf{i}") for i, (g, r) in enumerate(zip(split, recv))]
    got = _scatter_chips(parts, f"{tag}_scatter")
    mine = [_add_own(p, r, sel_me, f"{tag}_addown{i}") for i, (p, r) in enumerate(zip(parts, got))]
    full = _join_halves(mine, f"{tag}_join")
    return [f.reshape(f.shape[0] * f.shape[1], f.shape[2]) for f in full]


def _ffn_fwd(h, hb, w_up, w_down, conv_w, conv_b, ln_g, ln_b, tag):
    u = _matmul(hb, w_up, mode="nn", nsh=N_CHIPS, name=f"{tag}_up")
    gact = _conv_gate_fwd(u, conv_w, conv_b, f"{tag}_gate")
    ffn = _matmul(gact, w_down, mode="nn", nsh=1, name=f"{tag}_down")
    xin, h2, h2b = _res_ln_fwd(h, ffn, ln_g, ln_b, f"{tag}_ln")
    return (u, gact, xin), h2, h2b


def _ffn_bwd(saved, hb_in, dxin, dxin_b, w_up, w_down, conv_w, conv_b, tag):
    u, gact, _ = saved
    dgact = _matmul(dxin_b, w_down, mode="nt", nsh=1, name=f"{tag}_ddown")
    dw_down = _matmul(gact, dxin_b, mode="tn", nsh=1, out_shards=1, name=f"{tag}_wdown")
    da, db, dcw, dcb = _conv_gate_bwd(u, dgact, conv_w, conv_b, f"{tag}_dgate")
    dh = _matmul(da, w_up, mode="nt", nsh=2, b_off=0, resid=dxin, alpha=ALPHA, name=f"{tag}_dup_a")
    dh = _matmul(db, w_up, mode="nt", nsh=2, b_off=2, resid=dh, alpha=1.0, name=f"{tag}_dup_b")
    dw_up = _matmul(hb_in, da, mode="tn", nsh=2, out_off=0, out_shards=N_CHIPS, name=f"{tag}_wup_a")
    dw_up = _matmul(hb_in, db, mode="tn", nsh=2, out_off=2, out_shards=N_CHIPS, out_init=dw_up, name=f"{tag}_wup_b")
    return dh, dw_up, dw_down, dcw, dcb


def _local_step(x2, tgt, xb, wg, sm):
    proj = _matmul(xb, wg["hg_in"], mode="nn", nsh=N_CHIPS, name="hg_in")
    yhg, o_raw, states = _hgrn2_fwd(proj, sm["lb_logits"], sm["hg_norm_g"], "hgrn2_fwd")
    mixed = _matmul(yhg, wg["hg_out"], mode="nn", nsh=1, name="hg_out")
    xin1, h1, h1b = _res_ln_fwd(x2, mixed, sm["ln1_g"][0:1], sm["ln1_b"][0:1], "l0_ln1")
    sv_f0, h2, h2b = _ffn_fwd(h1, h1b, wg["up0"], wg["dn0"], sm["conv_w"][0], sm["conv_b"][0:1],
                              sm["ln2_g"][0:1], sm["ln2_b"][0:1], "l0_ffn")
    pre = _matmul(h2b, wg["sg_in"], mode="nn", nsh=N_CHIPS, name="sg_in")
    ysg = _sg_gate_fwd(pre, sm["sg_ln_g"], sm["sg_ln_b"], sm["sg_w_s"], sm["sg_b_s_t"], "sg_gate")
    mixed = _matmul(ysg, wg["sg_out"], mode="nn", nsh=1, name="sg_out")
    xin3, h3, h3b = _res_ln_fwd(h2, mixed, sm["ln1_g"][1:2], sm["ln1_b"][1:2], "l1_ln1")
    sv_f1, _, _ = _ffn_fwd(h3, h3b, wg["up1"], wg["dn1"], sm["conv_w"][1], sm["conv_b"][1:2],
                           sm["ln2_g"][1:2], sm["ln2_b"][1:2], "l1_ffn")
    gw, gs = {}, {}
    dx, dxb, dg4, db4, loss = _ln_bwd(sv_f1[2], tgt, sm["ln2_g"][1:2], sm["ln2_b"][1:2], "l1_ln2_bwd", loss_head=True)
    dh3, gw["up1"], gw["dn1"], dcw1, dcb1 = _ffn_bwd(sv_f1, h3b, dx, dxb, wg["up1"], wg["dn1"],
                                                     sm["conv_w"][1], sm["conv_b"][1:2], "l1_ffn")
    dx, dxb, dg3, db3 = _ln_bwd(xin3, dh3, sm["ln1_g"][1:2], sm["ln1_b"][1:2], "l1_ln1_bwd")
    dysg = _matmul(dxb, wg["sg_out"], mode="nt", nsh=1, name="sg_dout")
    gw["sg_out"] = _matmul(ysg, dxb, mode="tn", nsh=1, out_shards=1, name="sg_wout")
    dpre, gs["sg_w_s"], dbs_t, gs["sg_ln_g"], gs["sg_ln_b"] = _sg_gate_bwd(
        pre, dysg, sm["sg_ln_g"], sm["sg_ln_b"], sm["sg_w_s"], sm["sg_b_s_t"], "sg_gate_bwd")
    gs["sg_b_s_t"] = dbs_t
    dh2 = _matmul(dpre, wg["sg_in"], mode="nt", nsh=N_CHIPS, resid=dx, alpha=ALPHA, name="sg_din")
    gw["sg_in"] = _matmul(h2b, dpre, mode="tn", nsh=N_CHIPS, out_shards=N_CHIPS, name="sg_win")
    dx, dxb, dg2, db2 = _ln_bwd(sv_f0[2], dh2, sm["ln2_g"][0:1], sm["ln2_b"][0:1], "l0_ln2_bwd")
    dh1, gw["up0"], gw["dn0"], dcw0, dcb0 = _ffn_bwd(sv_f0, h1b, dx, dxb, wg["up0"], wg["dn0"],
                                                     sm["conv_w"][0], sm["conv_b"][0:1], "l0_ffn")
    dx, dxb, dg1, db1 = _ln_bwd(xin1, dh1, sm["ln1_g"][0:1], sm["ln1_b"][0:1], "l0_ln1_bwd")
    dyhg = _matmul(dxb, wg["hg_out"], mode="nt", nsh=1, name="hg_dout")
    gw["hg_out"] = _matmul(yhg, dxb, mode="tn", nsh=1, out_shards=1, name="hg_wout")
    dparts = _hgrn2_bwd(proj, sm["lb_logits"], sm["hg_norm_g"], o_raw, states, dyhg, "hgrn2_bwd")
    gs["lb"], gs["hg_norm_g"] = dparts[4], dparts[5]
    gx = dx
    al = ALPHA
    gwin = None
    for j in range(4):
        gx = _matmul(dparts[j], wg["hg_in"], mode="nt", nsh=1, b_off=j, resid=gx, alpha=al, name=f"hg_din{j}")
        al = 1.0
        gwin = _matmul(xb, dparts[j], mode="tn", nsh=1, out_off=j, out_shards=N_CHIPS, out_init=gwin,
                       name=f"hg_win{j}")
    gw["hg_in"] = gwin
    gs["ln1_g"] = jnp.concatenate([dg1, dg3], axis=0)
    gs["ln1_b"] = jnp.concatenate([db1, db3], axis=0)
    gs["ln2_g"] = jnp.concatenate([dg2, dg4], axis=0)
    gs["ln2_b"] = jnp.concatenate([db2, db4], axis=0)
    gs["conv_w"] = jnp.stack([dcw0, dcw1], axis=0)
    gs["conv_b"] = jnp.concatenate([dcb0, dcb1], axis=0)
    return loss, gx, gw, gs


_BIG = ("hg_in", "hg_out", "sg_in", "sg_out", "up0", "up1", "dn0", "dn1")
_SMALL_ORDER = ("lb", "hg_norm_g", "sg_w_s", "sg_b_s_t", "conv_b", "ln1_g", "ln1_b", "ln2_g", "ln2_b",
                "conv_w", "sg_ln_g", "sg_ln_b")


def _pack(parts):
    flat, layout, off = [], [], 0
    for k in _SMALL_ORDER:
        a = parts[k]
        n = a.size
        pad = (-n) % LANES
        flat.append(jnp.pad(a.reshape(-1), (0, pad)))
        layout.append((k, off, n, a.shape))
        off += n + pad
    tail = (-off) % (8 * LANES)
    flat.append(jnp.zeros((tail,), F32))
    return jnp.concatenate(flat).reshape(-1, LANES), layout


def _unpack(buf, layout):
    flat = buf.reshape(-1)
    return {k: flat[off:off + n].reshape(shape) for k, off, n, shape in layout}


def kernel(x, lb_logits, hg_w_in, hg_norm_g, hg_w_out, sg_w_in, sg_ln_g, sg_ln_b, sg_w_s, sg_b_s, sg_w_out, ffn_w_up, ffn_conv_w, ffn_conv_b, ffn_w_down, ln1_g, ln1_b, ln2_g, ln2_b, loss_target, m_lb_logits, m_hg_w_in, m_hg_norm_g, m_hg_w_out, m_sg_w_in, m_sg_ln_g, m_sg_ln_b, m_sg_w_s, m_sg_b_s, m_sg_w_out, m_ffn_w_up, m_ffn_conv_w, m_ffn_conv_b, m_ffn_w_down, m_ln1_g, m_ln1_b, m_ln2_g, m_ln2_b, v_lb_logits, v_hg_w_in, v_hg_norm_g, v_hg_w_out, v_sg_w_in, v_sg_ln_g, v_sg_ln_b, v_sg_w_s, v_sg_b_s, v_sg_w_out, v_ffn_w_up, v_ffn_conv_w, v_ffn_conv_b, v_ffn_w_down, v_ln1_g, v_ln1_b, v_ln2_g, v_ln2_b):
    names = ("lb_logits", "hg_w_in", "hg_norm_g", "hg_w_out", "sg_w_in", "sg_ln_g", "sg_ln_b", "sg_w_s", "sg_b_s",
             "sg_w_out", "ffn_w_up", "ffn_conv_w", "ffn_conv_b", "ffn_w_down", "ln1_g", "ln1_b", "ln2_g", "ln2_b")
    w = dict(zip(names, (lb_logits, hg_w_in, hg_norm_g, hg_w_out, sg_w_in, sg_ln_g, sg_ln_b, sg_w_s, sg_b_s,
                         sg_w_out, ffn_w_up, ffn_conv_w, ffn_conv_b, ffn_w_down, ln1_g, ln1_b, ln2_g, ln2_b)))
    mom = dict(zip(names, (m_lb_logits, m_hg_w_in, m_hg_norm_g, m_hg_w_out, m_sg_w_in, m_sg_ln_g, m_sg_ln_b, m_sg_w_s,
                           m_sg_b_s, m_sg_w_out, m_ffn_w_up, m_ffn_conv_w, m_ffn_conv_b, m_ffn_w_down, m_ln1_g,
                           m_ln1_b, m_ln2_g, m_ln2_b)))
    var = dict(zip(names, (v_lb_logits, v_hg_w_in, v_hg_norm_g, v_hg_w_out, v_sg_w_in, v_sg_ln_g, v_sg_ln_b, v_sg_w_s,
                           v_sg_b_s, v_sg_w_out, v_ffn_w_up, v_ffn_conv_w, v_ffn_conv_b, v_ffn_w_down, v_ln1_g,
                           v_ln1_b, v_ln2_g, v_ln2_b)))
    x2, tgt = x[0], loss_target[0]
    d = x2.shape[1]
    fq = ffn_conv_w.shape[2]
    dq = sg_ln_g.shape[1]
    cx, cy, _ = _place()
    me = 2 * cx + cy

    shards = {"hg_in": hg_w_in[0], "hg_out": hg_w_out[0], "sg_in": sg_w_in[0], "sg_out": sg_w_out[0],
              "up0": ffn_w_up[0], "up1": ffn_w_up[1], "dn0": ffn_w_down[0], "dn1": ffn_w_down[1]}
    gathered = _allgather_split([shards[k].astype(BF16) for k in _BIG], "gather_weights")
    wg = dict(zip(_BIG, gathered))
    for k in ("hg_out", "sg_out", "dn0", "dn1"):
        g = wg[k]
        wg[k] = g.reshape(1, g.shape[0] * g.shape[1], g.shape[2])
    wide = max(fq, dq)
    tiny = jnp.concatenate([jnp.pad(ffn_conv_w.reshape(6, fq), ((0, 0), (0, wide - fq))),
                            jnp.pad(sg_ln_g, ((0, 0), (0, wide - dq))),
                            jnp.pad(sg_ln_b, ((0, 0), (0, wide - dq)))], axis=0)
    tiny_all = _allgather_whole(tiny, "gather_small")
    conv_w_full = jnp.transpose(tiny_all[:, 0:6, :fq].reshape(N_CHIPS, 2, 3, fq), (1, 2, 0, 3)).reshape(2, 3, N_CHIPS * fq)
    sm = {"lb_logits": lb_logits, "hg_norm_g": hg_norm_g, "ln1_g": ln1_g, "ln1_b": ln1_b, "ln2_g": ln2_g,
          "ln2_b": ln2_b, "conv_w": conv_w_full, "conv_b": ffn_conv_b,
          "sg_ln_g": tiny_all[:, 6, :dq].reshape(1, N_CHIPS * dq),
          "sg_ln_b": tiny_all[:, 7, :dq].reshape(1, N_CHIPS * dq),
          "sg_w_s": sg_w_s[0], "sg_b_s_t": jnp.transpose(sg_b_s[0])}

    loss_row, grad_x, gw, gs = _local_step(x2, tgt, x2.astype(BF16), wg, sm)
    loss = lax.psum(loss_row[0, 0], ("x", "y", "c"))

    big_full = []
    for k in _BIG:
        g = gw[k]
        if k in ("hg_out", "sg_out", "dn0", "dn1"):
            g = g.reshape(N_CHIPS, g.shape[1] // N_CHIPS, g.shape[2])
        big_full.append(g)
    red = dict(zip(_BIG, _reduce_scatter(big_full, "rs")))
    packed, layout = _pack(gs)
    summed = _unpack(_sum_leading(_gather_all_devices(packed, "gather_small_grads"), "sum_small_grads"), layout)

    grads = {
        "lb_logits": _lb_logits_grad(lb_logits, summed["lb"], "lb_logits_grad"),
        "hg_w_in": red["hg_in"][None], "hg_norm_g": summed["hg_norm_g"], "hg_w_out": red["hg_out"][None],
        "sg_w_in": red["sg_in"][None],
        "sg_ln_g": lax.dynamic_slice_in_dim(summed["sg_ln_g"], me * dq, dq, axis=1),
        "sg_ln_b": lax.dynamic_slice_in_dim(summed["sg_ln_b"], me * dq, dq, axis=1),
        "sg_w_s": summed["sg_w_s"][None], "sg_b_s": jnp.transpose(summed["sg_b_s_t"])[None],
        "sg_w_out": red["sg_out"][None],
        "ffn_w_up": jnp.stack([red["up0"], red["up1"]]),
        "ffn_conv_w": lax.dynamic_slice_in_dim(summed["conv_w"], me * fq, fq, axis=2),
        "ffn_conv_b": summed["conv_b"],
        "ffn_w_down": jnp.stack([red["dn0"], red["dn1"]]),
        "ln1_g": summed["ln1_g"], "ln1_b": summed["ln1_b"], "ln2_g": summed["ln2_g"], "ln2_b": summed["ln2_b"],
    }

    big_names = ("hg_w_in", "hg_w_out", "sg_w_in", "sg_w_out", "ffn_w_up", "ffn_w_down")
    delta, new_m, new_v = {}, {}, {}
    for k in big_names:
        delta[k], new_m[k], new_v[k] = _adamw(w[k], grads[k], mom[k], var[k], f"adamw_{k}")
    small_names = [k for k in names if k not in big_names]

    def pack_small(src):
        flat = [src[k].reshape(-1) for k in small_names]
        n = sum(a.size for a in flat)
        flat.append(jnp.zeros(((-n) % (8 * LANES),), F32))
        return jnp.concatenate(flat).reshape(-1, LANES)

    outs = _adamw(pack_small(w), pack_small(grads), pack_small(mom), pack_small(var), "adamw_small")
    off = 0
    for k in small_names:
        n = w[k].size
        for dst, o in zip((delta, new_m, new_v), outs):
            dst[k] = o.reshape(-1)[off:off + n].reshape(w[k].shape)
        off += n

    return (loss, grad_x[None], *[grads[k] for k in names], *[delta[k] for k in names],
            *[new_m[k] for k in names], *[new_v[k] for k in names])
```

```python
import functools

import jax
import jax.numpy as jnp
from jax import lax
from jax.experimental import pallas as pl
from jax.experimental.pallas import tpu as pltpu

F32 = jnp.float32
BF16 = jnp.bfloat16
HI = lax.Precision.HIGHEST
MESH = pl.DeviceIdType.MESH

ALPHA = (2 * 2) ** 0.25
LN_EPS = 1e-5
RMS_EPS = 1e-6
ADAM_LR, ADAM_B1, ADAM_B2, ADAM_EPS, ADAM_WD, ADAM_STEP = 0.001, 0.9, 0.999, 1e-08, 0.01, 10

LANES = 128
SUB = 16
GCHUNK = 128
VMEM_LIMIT = 56 * 1024 * 1024
N_CHIPS = 4
N_DEV = 8

NT = (((1,), (1,)), ((), ()))
TN = (((0,), (0,)), ((), ()))
NN = (((1,), (0,)), ((), ()))


def _pick(dim, prefs):
    for p in prefs:
        if dim % p == 0:
            return p
    return dim


def _params(sem=None, **kw):
    return pltpu.CompilerParams(dimension_semantics=sem, vmem_limit_bytes=VMEM_LIMIT, **kw)


def _sigmoid_pair(x):
    e = jnp.exp(-jnp.abs(x))
    inv = 1.0 / (1.0 + e)
    pos = x >= 0
    return jnp.where(pos, inv, e * inv), jnp.where(pos, e * inv, inv)


def _ln_hat(x):
    mu = jnp.mean(x, axis=-1, keepdims=True)
    xc = x - mu
    var = jnp.mean(xc * xc, axis=-1, keepdims=True)
    rstd = lax.rsqrt(var + LN_EPS)
    return xc * rstd, rstd


def _lower_bound(logits):
    m = jnp.max(logits, axis=0, keepdims=True)
    e = jnp.exp(logits - m)
    return e[0:1, :] / jnp.sum(e, axis=0, keepdims=True)


def _matmul(a, b, *, mode, name, out_dtype=F32, resid=None, alpha=1.0, b_off=0, nsh=None,
            out_init=None, out_off=0, out_shards=None):
    if mode == "nn":
        m, kdim = a.shape
        _, _, ns = b.shape
        bm = _pick(m, (1024, 512, 256, 128))
        bn = _pick(ns, (1024, 1408, 512, 256, 128))
        bk = _pick(kdim, (512, 256, 128))
        nps = ns // bn
        grid = (m // bm, nsh * nps, kdim // bk)
        a_spec = pl.BlockSpec((bm, bk), lambda i, j, k: (i, k))
        b_spec = pl.BlockSpec((None, bk, bn), lambda i, j, k: (b_off + j // nps, k, j % nps))
        o_spec = pl.BlockSpec((bm, bn), lambda i, j, k: (i, j))
        out_shape = jax.ShapeDtypeStruct((m, nsh * ns), out_dtype)
        dims = NN
    elif mode == "nt":
        m = a.shape[0]
        _, kdim, ns = b.shape
        bm = _pick(m, (1024, 512, 256, 128))
        bn = _pick(kdim, (1024, 1408, 512, 256, 128))
        bk = _pick(ns, (512, 1408, 256, 128))
        kps = ns // bk
        grid = (m // bm, kdim // bn, nsh * kps)
        a_spec = pl.BlockSpec((bm, bk), lambda i, j, k: (i, k))
        b_spec = pl.BlockSpec((None, bn, bk), lambda i, j, k: (b_off + k // kps, j, k % kps))
        o_spec = pl.BlockSpec((bm, bn), lambda i, j, k: (i, j))
        out_shape = jax.ShapeDtypeStruct((m, kdim), out_dtype)
        dims = NT
    else:
        t, kdim = a.shape
        ns = b.shape[1] // nsh
        bm = _pick(kdim, (1024, 1408, 512, 256, 128))
        bn = _pick(ns, (1024, 1408, 512, 256, 128))
        bk = _pick(t, (512, 256, 128))
        nps = ns // bn
        grid = (kdim // bm, nsh * nps, t // bk)
        a_spec = pl.BlockSpec((bk, bm), lambda i, j, k: (k, i))
        b_spec = pl.BlockSpec((bk, bn), lambda i, j, k: (k, j))
        o_spec = pl.BlockSpec((None, bm, bn), lambda i, j, k: (out_off + j // nps, i, j % nps))
        out_shape = jax.ShapeDtypeStruct((out_shards, kdim, ns), out_dtype)
        dims = TN
    nk = grid[2]
    has_resid = resid is not None
    has_init = out_init is not None

    def kern(*refs):
        a_ref, b_ref = refs[0], refs[1]
        r_ref = refs[2] if has_resid else None
        o_ref, acc_ref = refs[-2], refs[-1]
        k = pl.program_id(2)

        @pl.when(k == 0)
        def _():
            acc_ref[...] = jnp.zeros_like(acc_ref)

        acc_ref[...] += lax.dot_general(a_ref[...], b_ref[...], dims, preferred_element_type=F32)

        @pl.when(k == nk - 1)
        def _():
            r = acc_ref[...]
            if has_resid:
                r = r + alpha * r_ref[...]
            o_ref[...] = r.astype(o_ref.dtype)

    in_specs = [a_spec, b_spec]
    operands = [a, b]
    if has_resid:
        in_specs.append(pl.BlockSpec((bm, bn), lambda i, j, k: (i, j)))
        operands.append(resid)
    aliases = {}
    if has_init:
        in_specs.append(pl.BlockSpec(memory_space=pl.ANY))
        operands.append(out_init)
        aliases = {len(operands) - 1: 0}
    return pl.pallas_call(
        kern, name=name, grid=grid, in_specs=in_specs, out_specs=o_spec, out_shape=out_shape,
        scratch_shapes=[pltpu.VMEM((bm, bn), F32)], input_output_aliases=aliases,
        compiler_params=_params(("parallel", "parallel", "arbitrary")),
    )(*operands)


def _res_ln_fwd(h_prev, sub, g, b, name):
    t, d = h_prev.shape
    tb = _pick(t, (256, 128, 64, 32, 16))

    def kern(hp_ref, s_ref, g_ref, b_ref, xin_ref, h_ref, hb_ref):
        xin = ALPHA * hp_ref[...] + s_ref[...]
        xhat, _ = _ln_hat(xin)
        h = xhat * g_ref[...] + b_ref[...]
        xin_ref[...] = xin
        h_ref[...] = h
        hb_ref[...] = h.astype(BF16)

    row = pl.BlockSpec((tb, d), lambda i: (i, 0))
    vec = pl.BlockSpec((1, d), lambda i: (0, 0))
    return pl.pallas_call(
        kern, name=name, grid=(t // tb,), in_specs=[row, row, vec, vec], out_specs=[row, row, row],
        out_shape=[jax.ShapeDtypeStruct((t, d), F32), jax.ShapeDtypeStruct((t, d), F32),
                   jax.ShapeDtypeStruct((t, d), BF16)],
        compiler_params=_params(("parallel",)),
    )(h_prev, sub, g, b)


def _ln_bwd(xin, dy_or_target, g, b, name, loss_head=False):
    t, d = xin.shape
    tb = _pick(t, (256, 128, 64, 32, 16))
    nb = t // tb

    def kern(x_ref, dy_ref, g_ref, b_ref, dx_ref, dxb_ref, dg_ref, db_ref, *rest):
        i = pl.program_id(0)
        xhat, rstd = _ln_hat(x_ref[...])
        gv = g_ref[...]
        if loss_head:
            loss_ref = rest[0]
            err = xhat * gv + b_ref[...] - dy_ref[...]
            dy = err * (1.0 / d)
            part = 0.5 * jnp.sum(jnp.sum(err * err, axis=1, keepdims=True), axis=0, keepdims=True) * (1.0 / d)
        else:
            dy = dy_ref[...]

        @pl.when(i == 0)
        def _():
            dg_ref[...] = jnp.zeros_like(dg_ref)
            db_ref[...] = jnp.zeros_like(db_ref)
            if loss_head:
                loss_ref[...] = jnp.zeros_like(loss_ref)

        dg_ref[...] += jnp.sum(dy * xhat, axis=0, keepdims=True)
        db_ref[...] += jnp.sum(dy, axis=0, keepdims=True)
        if loss_head:
            loss_ref[...] += jnp.broadcast_to(part, loss_ref.shape)
        dxh = dy * gv
        m1 = jnp.mean(dxh, axis=-1, keepdims=True)
        m2 = jnp.mean(dxh * xhat, axis=-1, keepdims=True)
        dx = rstd * (dxh - m1 - xhat * m2)
        dx_ref[...] = dx
        dxb_ref[...] = dx.astype(BF16)

    row = pl.BlockSpec((tb, d), lambda i: (i, 0))
    vec = pl.BlockSpec((1, d), lambda i: (0, 0))
    out_specs = [row, row, vec, vec]
    out_shape = [jax.ShapeDtypeStruct((t, d), F32), jax.ShapeDtypeStruct((t, d), BF16),
                 jax.ShapeDtypeStruct((1, d), F32), jax.ShapeDtypeStruct((1, d), F32)]
    if loss_head:
        out_specs.append(pl.BlockSpec((1, LANES), lambda i: (0, 0)))
        out_shape.append(jax.ShapeDtypeStruct((1, LANES), F32))
    return pl.pallas_call(
        kern, name=name, grid=(nb,), in_specs=[row, row, vec, vec], out_specs=out_specs, out_shape=out_shape,
        compiler_params=_params(("arbitrary",)),
    )(xin, dy_or_target, g, b)


def _conv_gate_fwd(u, conv_w, conv_b, name):
    t, f2 = u.shape
    f = f2 // 2
    tb = _pick(t, (512, 256, 128, 64, 32, 16))
    cn = _pick(f, (1408, 1024, 512, 256, 128))
    ncb = f // cn
    hb = tb // 8

    def kern(a_ref, ah_ref, b_ref, w_ref, cb_ref, o_ref):
        i = pl.program_id(0)
        a = a_ref[...]
        halo = jnp.where(i > 0, ah_ref[...], 0.0)
        rid = lax.broadcasted_iota(jnp.int32, a.shape, 0)
        s1 = jnp.where(rid == 0, halo[7:8, :], pltpu.roll(a, 1, 0))
        s2 = jnp.where(rid == 0, halo[6:7, :], jnp.where(rid == 1, halo[7:8, :], pltpu.roll(a, 2, 0)))
        w = w_ref[...]
        conv = w[2:3, :] * a + w[1:2, :] * s1 + w[0:1, :] * s2 + cb_ref[...]
        sp, _ = _sigmoid_pair(conv)
        o_ref[...] = (conv * sp * b_ref[...]).astype(BF16)

    return pl.pallas_call(
        kern, name=name, grid=(t // tb, ncb),
        in_specs=[pl.BlockSpec((tb, cn), lambda i, j: (i, j)),
                  pl.BlockSpec((8, cn), lambda i, j: (jnp.maximum(i * hb - 1, 0), j)),
                  pl.BlockSpec((tb, cn), lambda i, j: (i, j + ncb)),
                  pl.BlockSpec((3, cn), lambda i, j: (0, j)),
                  pl.BlockSpec((1, cn), lambda i, j: (0, j))],
        out_specs=pl.BlockSpec((tb, cn), lambda i, j: (i, j)),
        out_shape=jax.ShapeDtypeStruct((t, f), BF16),
        compiler_params=_params(("parallel", "parallel")),
    )(u, u, u, conv_w, conv_b)


def _conv_gate_bwd(u, dgact, conv_w, conv_b, name):
    t, f2 = u.shape
    f = f2 // 2
    tb = _pick(t, (512, 256, 128, 64, 32, 16))
    cn = _pick(f, (1408, 1024, 512, 256, 128))
    ncb = f // cn
    hb = tb // 8
    nb = t // tb
    last8 = t // 8 - 1

    def kern(a_ref, ap_ref, an_ref, b_ref, bn_ref, dg_ref, dgn_ref, w_ref, cb_ref,
             da_ref, db_ref, dw_ref, dcb_ref):
        i = pl.program_id(1)
        a = a_ref[...]
        w = w_ref[...]
        ext = jnp.concatenate([jnp.where(i > 0, ap_ref[...], 0.0), a, an_ref[...]], axis=0)
        e1 = pltpu.roll(ext, 1, 0)
        e2 = pltpu.roll(ext, 2, 0)
        conv = (w[2:3, :] * ext + w[1:2, :] * e1 + w[0:1, :] * e2 + cb_ref[...])[8:, :]
        bmn = jnp.concatenate([b_ref[...], bn_ref[...]], axis=0)
        dgmn = jnp.concatenate([dg_ref[...], jnp.where(i < nb - 1, dgn_ref[...], 0.0)], axis=0)
        sp, sn = _sigmoid_pair(conv)
        da = dgmn * bmn * (sp * (1.0 + conv * sn))
        n = tb + 8
        dap = w[2:3, :] * da + w[1:2, :] * pltpu.roll(da, n - 1, 0) + w[0:1, :] * pltpu.roll(da, n - 2, 0)
        da_ref[...] = dap[:tb, :].astype(BF16)
        db_ref[...] = (dg_ref[...] * (conv * sp)[:tb, :]).astype(BF16)
        dam = da[:tb, :]

        @pl.when(i == 0)
        def _():
            dw_ref[...] = jnp.zeros_like(dw_ref)
            dcb_ref[...] = jnp.zeros_like(dcb_ref)

        dw = jnp.concatenate([jnp.sum(dam * e2[8:8 + tb, :], axis=0, keepdims=True),
                              jnp.sum(dam * e1[8:8 + tb, :], axis=0, keepdims=True),
                              jnp.sum(dam * a, axis=0, keepdims=True)], axis=0)
        dw_ref[...] += dw
        dcb_ref[...] += jnp.sum(dam, axis=0, keepdims=True)

    main_a = pl.BlockSpec((tb, cn), lambda j, i: (i, j))
    prev_a = pl.BlockSpec((8, cn), lambda j, i: (jnp.maximum(i * hb - 1, 0), j))
    next_a = pl.BlockSpec((8, cn), lambda j, i: (jnp.minimum((i + 1) * hb, last8), j))
    main_b = pl.BlockSpec((tb, cn), lambda j, i: (i, j + ncb))
    next_b = pl.BlockSpec((8, cn), lambda j, i: (jnp.minimum((i + 1) * hb, last8), j + ncb))
    return pl.pallas_call(
        kern, name=name, grid=(ncb, nb),
        in_specs=[main_a, prev_a, next_a, main_b, next_b, main_a, next_a,
                  pl.BlockSpec((3, cn), lambda j, i: (0, j)), pl.BlockSpec((1, cn), lambda j, i: (0, j))],
        out_specs=[main_a, main_a, pl.BlockSpec((3, cn), lambda j, i: (0, j)),
                   pl.BlockSpec((1, cn), lambda j, i: (0, j))],
        out_shape=[jax.ShapeDtypeStruct((t, f), BF16), jax.ShapeDtypeStruct((t, f), BF16),
                   jax.ShapeDtypeStruct((3, f), F32), jax.ShapeDtypeStruct((1, f), F32)],
        compiler_params=_params(("parallel", "arbitrary")),
    )(u, u, u, u, u, dgact, dgact, conv_w, conv_b)


def _hg_gates(qp, fp, lb):
    sq, _ = _sigmoid_pair(qp)
    sf, snf = _sigmoid_pair(fp)
    forget = lb + (1.0 - lb) * sf
    return sq, sf, snf, forget, jnp.log(forget), (1.0 - lb) * snf


def _tri(lower):
    r = lax.broadcasted_iota(jnp.int32, (SUB, SUB), 0)
    c = lax.broadcasted_iota(jnp.int32, (SUB, SUB), 1)
    return ((r >= c) if lower else (r <= c)).astype(F32)


def _hgrn2_fwd(proj, lb_logits, norm_g, name):
    t, d4 = proj.shape
    d = d4 // 4
    nh = d // LANES
    tb = _pick(t, (256, 128, 64, 32, 16))
    nb = t // tb
    nsc = tb // SUB

    def kern(q_ref, f_ref, i_ref, g_ref, lbl_ref, ng_ref, y_ref, o_ref, st_ref, s_ref):
        @pl.when(pl.program_id(1) == 0)
        def _():
            s_ref[...] = jnp.zeros_like(s_ref)

        lb = _lower_bound(lbl_ref[...])
        ng = ng_ref[...]
        ltri = _tri(True)
        rcol = lax.broadcasted_iota(jnp.int32, (SUB, 1), 0)

        def step(sc, carry):
            rows = pl.ds(pl.multiple_of(sc * SUB, SUB), SUB)
            qp, fp, v, gp = q_ref[rows, :], f_ref[rows, :], i_ref[rows, :], g_ref[rows, :]
            sq, _, _, _, lf, k = _hg_gates(qp, fp, lb)
            q = qp * sq
            bl = jnp.dot(ltri, lf, precision=HI, preferred_element_type=F32)
            state = s_ref[...]
            st_ref[sc] = state
            o = lax.dot_general(q * jnp.exp(bl), state, NT, precision=HI, preferred_element_type=F32)
            for s in range(SUB):
                e = jnp.exp(jnp.minimum(bl - bl[s:s + 1, :], 0.0))
                a = jnp.sum(q * e * k[s:s + 1, :], axis=1, keepdims=True)
                o = o + jnp.where(rcol >= s, a, 0.0) * v[s:s + 1, :]
            bend = bl[SUB - 1:SUB, :]
            kd = k * jnp.exp(bend - bl)
            s_ref[...] = state * jnp.exp(bend) + lax.dot_general(v, kd, TN, precision=HI, preferred_element_type=F32)
            o_ref[rows, :] = o
            r = lax.rsqrt(jnp.mean(o * o, axis=1, keepdims=True) + RMS_EPS)
            sg, _ = _sigmoid_pair(gp)
            y_ref[rows, :] = (o * r * ng * (gp * sg)).astype(BF16)
            return carry

        lax.fori_loop(0, nsc, step, 0)

    def col(off):
        return pl.BlockSpec((tb, LANES), lambda h, j: (j, h + off * nh))

    return pl.pallas_call(
        kern, name=name, grid=(nh, nb),
        in_specs=[col(0), col(1), col(2), col(3),
                  pl.BlockSpec((3, LANES), lambda h, j: (0, h)), pl.BlockSpec((1, LANES), lambda h, j: (0, h))],
        out_specs=[col(0), col(0), pl.BlockSpec((nsc, None, LANES, LANES), lambda h, j: (j, h, 0, 0))],
        out_shape=[jax.ShapeDtypeStruct((t, d), BF16), jax.ShapeDtypeStruct((t, d), F32),
                   jax.ShapeDtypeStruct((t // SUB, nh, LANES, LANES), F32)],
        scratch_shapes=[pltpu.VMEM((LANES, LANES), F32)],
        compiler_params=_params(("parallel", "arbitrary")),
    )(proj, proj, proj, proj, lb_logits, norm_g)


def _hgrn2_bwd(proj, lb_logits, norm_g, o_raw, states, dy, name):
    t, d4 = proj.shape
    d = d4 // 4
    nh = d // LANES
    tb = _pick(t, (256, 128, 64, 32, 16))
    nb = t // tb
    nsc = tb // SUB

    def kern(q_ref, f_ref, i_ref, g_ref, lbl_ref, ng_ref, o_ref, st_ref, dy_ref,
             dq_ref, df_ref, di_ref, dgp_ref, dlb_ref, dng_ref, ds_ref, gc_ref):
        j = pl.program_id(1)

        @pl.when(j == 0)
        def _():
            ds_ref[...] = jnp.zeros_like(ds_ref)
            gc_ref[...] = jnp.zeros_like(gc_ref)
            dlb_ref[...] = jnp.zeros_like(dlb_ref)
            dng_ref[...] = jnp.zeros_like(dng_ref)

        lb = _lower_bound(lbl_ref[...])
        ng = ng_ref[...]
        ltri, utri = _tri(True), _tri(False)
        rcol = lax.broadcasted_iota(jnp.int32, (SUB, 1), 0)
        rid = lax.broadcasted_iota(jnp.int32, (SUB, LANES), 0)

        def step(it, carry):
            sc = nsc - 1 - it
            rows = pl.ds(pl.multiple_of(sc * SUB, SUB), SUB)
            qp, fp, v, gp = q_ref[rows, :], f_ref[rows, :], i_ref[rows, :], g_ref[rows, :]
            sq, sf, snf, forget, lf, k = _hg_gates(qp, fp, lb)
            q = qp * sq
            bl = jnp.dot(ltri, lf, precision=HI, preferred_element_type=F32)
            ebl = jnp.exp(bl)
            qs = q * ebl
            bend = bl[SUB - 1:SUB, :]
            dte = jnp.exp(bend - bl)
            kd = k * dte
            state = st_ref[sc]
            dstate = ds_ref[...]
            o = o_ref[rows, :]
            r = lax.rsqrt(jnp.mean(o * o, axis=1, keepdims=True) + RMS_EPS)
            ohat = o * r
            sg, sng = _sigmoid_pair(gp)
            dyv = dy_ref[rows, :]
            don = dyv * (gp * sg)
            dgp_ref[rows, :] = (dyv * (ohat * ng) * (sg * (1.0 + gp * sng))).astype(BF16)
            dng_ref[...] += jnp.sum(don * ohat, axis=0, keepdims=True)
            doh = don * ng
            do = r * (doh - ohat * jnp.mean(doh * ohat, axis=1, keepdims=True))
            dq = jnp.dot(do, state, precision=HI, preferred_element_type=F32) * ebl
            dv = lax.dot_general(kd, dstate, NT, precision=HI, preferred_element_type=F32)
            dk = jnp.dot(v, dstate, precision=HI, preferred_element_type=F32) * dte
            dki = jnp.zeros((SUB, LANES), F32)
            dvi = jnp.zeros((SUB, LANES), F32)
            for s in range(SUB):
                e = jnp.exp(jnp.minimum(bl - bl[s:s + 1, :], 0.0))
                qe = q * e
                ks = k[s:s + 1, :]
                live = rcol >= s
                a = jnp.where(live, jnp.sum(qe * ks, axis=1, keepdims=True), 0.0)
                da = jnp.where(live, jnp.sum(do * v[s:s + 1, :], axis=1, keepdims=True), 0.0)
                dq = dq + da * (e * ks)
                dki = jnp.where(rid == s, jnp.sum(da * qe, axis=0, keepdims=True), dki)
                dvi = jnp.where(rid == s, jnp.sum(a * do, axis=0, keepdims=True), dvi)
            dk = dk + dki
            dv = dv + dvi
            ds_ref[...] = dstate * jnp.exp(bend) + lax.dot_general(do, qs, TN, precision=HI, preferred_element_type=F32)
            w = q * dq - k * dk
            gc = gc_ref[...]
            dlf = jnp.dot(utri, w, precision=HI, preferred_element_type=F32) + gc
            gc_ref[...] = gc + jnp.sum(w, axis=0, keepdims=True)
            t1 = dlf / forget - dk
            df_ref[rows, :] = ((1.0 - lb) * sf * snf * t1).astype(BF16)
            dlb_ref[...] += jnp.sum(snf * t1, axis=0, keepdims=True)
            dq_ref[rows, :] = (dq * (sq * (1.0 + qp * (1.0 - sq)))).astype(BF16)
            di_ref[rows, :] = dv.astype(BF16)
            return carry

        lax.fori_loop(0, nsc, step, 0)

    def col(off):
        return pl.BlockSpec((tb, LANES), lambda h, j: (nb - 1 - j, h + off * nh))

    vec = pl.BlockSpec((1, LANES), lambda h, j: (0, h))
    return pl.pallas_call(
        kern, name=name, grid=(nh, nb),
        in_specs=[col(0), col(1), col(2), col(3), pl.BlockSpec((3, LANES), lambda h, j: (0, h)), vec,
                  col(0), pl.BlockSpec((nsc, None, LANES, LANES), lambda h, j: (nb - 1 - j, h, 0, 0)), col(0)],
        out_specs=[col(0), col(0), col(0), col(0), vec, vec],
        out_shape=[jax.ShapeDtypeStruct((t, d), BF16)] * 4 + [jax.ShapeDtypeStruct((1, d), F32)] * 2,
        scratch_shapes=[pltpu.VMEM((LANES, LANES), F32), pltpu.VMEM((1, LANES), F32)],
        compiler_params=_params(("parallel", "arbitrary")),
    )(proj, proj, proj, proj, lb_logits, norm_g, o_raw, states, dy)


_INV_SQRT2 = 0.7071067811865476
_INV_SQRT2PI = 0.3989422804014327


def _gelu(x):
    return 0.5 * x * (1.0 + lax.erf(x * _INV_SQRT2))


def _gelu_grad(x):
    return 0.5 * (1.0 + lax.erf(x * _INV_SQRT2)) + x * jnp.exp(-0.5 * x * x) * _INV_SQRT2PI


def _causal(w):
    r = lax.broadcasted_iota(jnp.int32, (GCHUNK, GCHUNK), 0)
    c = lax.broadcasted_iota(jnp.int32, (GCHUNK, GCHUNK), 1)
    return jnp.where(r >= c, w, 0.0)


def _sg_gate_fwd(pre, ln_g, ln_b, w_s, b_s_t, name):
    t, d2 = pre.shape
    d = d2 // 2
    ng = d // LANES

    def kern(pre_ref, g_ref, b_ref, ws_ref, bs_ref, y_ref):
        z = _gelu(pre_ref[...])
        u = z[:, :d]
        vhat, _ = _ln_hat(z[:, d:])
        vn = (vhat * g_ref[...] + b_ref[...]).astype(BF16)
        bs = bs_ref[...]
        for g in range(ng):
            cols = slice(g * LANES, (g + 1) * LANES)
            wc = _causal(ws_ref[g]).astype(BF16)
            gate = jnp.dot(wc, vn[:, cols], preferred_element_type=F32) + bs[:, g:g + 1]
            y_ref[:, cols] = (u[:, cols] * gate).astype(BF16)

    vec = pl.BlockSpec((1, d), lambda i: (0, 0))
    return pl.pallas_call(
        kern, name=name, grid=(t // GCHUNK,),
        in_specs=[pl.BlockSpec((GCHUNK, d2), lambda i: (i, 0)), vec, vec,
                  pl.BlockSpec((ng, GCHUNK, GCHUNK), lambda i: (0, 0, 0)),
                  pl.BlockSpec((GCHUNK, ng), lambda i: (0, 0))],
        out_specs=pl.BlockSpec((GCHUNK, d), lambda i: (i, 0)),
        out_shape=jax.ShapeDtypeStruct((t, d), BF16),
        compiler_params=_params(("parallel",)),
    )(pre, ln_g, ln_b, w_s, b_s_t)


def _sg_gate_bwd(pre, dy, ln_g, ln_b, w_s, b_s_t, name):
    t, d2 = pre.shape
    d = d2 // 2
    ng = d // LANES

    def kern(pre_ref, dy_ref, g_ref, b_ref, ws_ref, bs_ref, dpre_ref, dws_ref, dbs_ref, dg_ref, db_ref, dvn_ref):
        @pl.when(pl.program_id(0) == 0)
        def _():
            dws_ref[...] = jnp.zeros_like(dws_ref)
            dbs_ref[...] = jnp.zeros_like(dbs_ref)
            dg_ref[...] = jnp.zeros_like(dg_ref)
            db_ref[...] = jnp.zeros_like(db_ref)

        pre = pre_ref[...]
        z = _gelu(pre)
        u = z[:, :d]
        vhat, rstd = _ln_hat(z[:, d:])
        gv = g_ref[...]
        vn = (vhat * gv + b_ref[...]).astype(BF16)
        bs = bs_ref[...]
        dyv = dy_ref[...]
        gp = _gelu_grad(pre)
        lane = lax.broadcasted_iota(jnp.int32, (GCHUNK, ng), 1)
        dbs = jnp.zeros((GCHUNK, ng), F32)
        for g in range(ng):
            cols = slice(g * LANES, (g + 1) * LANES)
            wc = _causal(ws_ref[g]).astype(BF16)
            vng = vn[:, cols]
            gate = jnp.dot(wc, vng, preferred_element_type=F32) + bs[:, g:g + 1]
            dpre_ref[:, cols] = (dyv[:, cols] * gate * gp[:, cols]).astype(BF16)
            dgate = dyv[:, cols] * u[:, cols]
            dbs = dbs + jnp.where(lane == g, jnp.sum(dgate, axis=1, keepdims=True), 0.0)
            dgb = dgate.astype(BF16)
            dws_ref[g] += _causal(lax.dot_general(dgb, vng, NT, preferred_element_type=F32))
            dvn_ref[:, cols] = lax.dot_general(wc, dgb, TN, preferred_element_type=F32)
        dbs_ref[...] += dbs
        dvn = dvn_ref[...]
        dg_ref[...] += jnp.sum(dvn * vhat, axis=0, keepdims=True)
        db_ref[...] += jnp.sum(dvn, axis=0, keepdims=True)
        dvh = dvn * gv
        m1 = jnp.mean(dvh, axis=-1, keepdims=True)
        m2 = jnp.mean(dvh * vhat, axis=-1, keepdims=True)
        dpre_ref[:, d:] = (rstd * (dvh - m1 - vhat * m2) * gp[:, d:]).astype(BF16)

    vec = pl.BlockSpec((1, d), lambda i: (0, 0))
    wsp = pl.BlockSpec((ng, GCHUNK, GCHUNK), lambda i: (0, 0, 0))
    bsp = pl.BlockSpec((GCHUNK, ng), lambda i: (0, 0))
    return pl.pallas_call(
        kern, name=name, grid=(t // GCHUNK,),
        in_specs=[pl.BlockSpec((GCHUNK, d2), lambda i: (i, 0)), pl.BlockSpec((GCHUNK, d), lambda i: (i, 0)),
                  vec, vec, wsp, bsp],
        out_specs=[pl.BlockSpec((GCHUNK, d2), lambda i: (i, 0)), wsp, bsp, vec, vec],
        out_shape=[jax.ShapeDtypeStruct((t, d2), BF16), jax.ShapeDtypeStruct((ng, GCHUNK, GCHUNK), F32),
                   jax.ShapeDtypeStruct((GCHUNK, ng), F32), jax.ShapeDtypeStruct((1, d), F32),
                   jax.ShapeDtypeStruct((1, d), F32)],
        scratch_shapes=[pltpu.VMEM((GCHUNK, d), F32)],
        compiler_params=_params(("arbitrary",)),
    )(pre, dy, ln_g, ln_b, w_s, b_s_t)


def _adamw_math(w, g, m, v):
    m = ADAM_B1 * m + (1.0 - ADAM_B1) * g
    v = ADAM_B2 * v + (1.0 - ADAM_B2) * (g * g)
    m_hat = m / (1.0 - ADAM_B1 ** ADAM_STEP)
    v_hat = v / (1.0 - ADAM_B2 ** ADAM_STEP)
    return -ADAM_LR * (m_hat / (jnp.sqrt(v_hat) + ADAM_EPS) + ADAM_WD * w), m, v


ADAMW_BLOCK_BYTES = 3 << 19


def _adamw(w, gs, m, v, name):
    nl, r, c = w.shape
    rb = _pick(r, tuple(p for p in (512, 256, 128, 64, 32, 16, 8) if p * c * 4 <= ADAMW_BLOCK_BYTES))

    def kern(w_ref, m_ref, v_ref, *rest):
        g_refs, (d_ref, mo_ref, vo_ref, go_ref) = rest[:nl], rest[nl:]
        layer = pl.program_id(0)
        g = g_refs[0][...]
        for k in range(1, nl):
            g = jnp.where(layer == k, g_refs[k][...], g)
        dlt, mm, vv = _adamw_math(w_ref[...], g, m_ref[...], v_ref[...])
        d_ref[...] = dlt
        mo_ref[...] = mm
        vo_ref[...] = vv
        go_ref[...] = g

    blk = pl.BlockSpec((None, rb, c), lambda l, i: (l, i, 0))
    g_specs = [pl.BlockSpec((rb, c), lambda l, i, k=k: (jnp.where(l == k, i, 0), 0)) for k in range(nl)]
    return pl.pallas_call(
        kern, name=name, grid=(nl, r // rb), in_specs=[blk] * 3 + g_specs, out_specs=[blk] * 4,
        out_shape=[jax.ShapeDtypeStruct((nl, r, c), F32)] * 4,
        compiler_params=_params(("parallel", "parallel")),
    )(w, m, v, *gs)


def _lb_logits_grad(lb_logits, dlb, name):
    def kern(l_ref, d_ref, o_ref):
        lg = l_ref[...]
        m = jnp.max(lg, axis=0, keepdims=True)
        e = jnp.exp(lg - m)
        p = e / jnp.sum(e, axis=0, keepdims=True)
        row = lax.broadcasted_iota(jnp.int32, lg.shape, 0)
        o_ref[...] = d_ref[...] * p[0:1, :] * (jnp.where(row == 0, 1.0, 0.0) - p)

    return pl.pallas_call(kern, name=name, out_shape=jax.ShapeDtypeStruct(lb_logits.shape, F32))(lb_logits, dlb)


def _sum_leading(a, name):
    n, r, c = a.shape
    rb = _pick(r, (512, 256, 128, 64, 32, 16, 8))

    def kern(a_ref, o_ref):
        acc = a_ref[0]
        for i in range(1, n):
            acc = acc + a_ref[i]
        o_ref[...] = acc

    return pl.pallas_call(
        kern, name=name, grid=(r // rb,), in_specs=[pl.BlockSpec((n, rb, c), lambda i: (0, i, 0))],
        out_specs=pl.BlockSpec((rb, c), lambda i: (i, 0)), out_shape=jax.ShapeDtypeStruct((r, c), F32),
        compiler_params=_params(("parallel",)),
    )(a)


def _place():
    x, y, c = lax.axis_index("x"), lax.axis_index("y"), lax.axis_index("c")
    return x, y, c


def _chip_rel(x, y, r):
    px = x if r < 2 else 1 - x
    py = y if r % 2 == 0 else 1 - y
    return px, py, 2 * px + py


def _allgather_split(arrs, name):
    n = len(arrs)
    slots = 7

    def body(*refs):
        ins, outs = refs[:n], refs[n:2 * n]
        send_sems, recv_sems = refs[2 * n:]
        x, y, c = _place()
        me = 2 * x + y
        sib = (x, y, 1 - c)

        def half(a, shard, hc):
            h = ins[a].shape[0] // 2
            return outs[a].at[shard, pl.ds(hc * h, h), :]

        def src_half(a):
            h = ins[a].shape[0] // 2
            return ins[a].at[pl.ds(c * h, h), :]

        def copy(a, slot, src, dst, to):
            return pltpu.make_async_remote_copy(src_ref=src, dst_ref=dst, send_sem=send_sems.at[a * slots + slot],
                                                recv_sem=recv_sems.at[a * slots + slot], device_id=to,
                                                device_id_type=MESH)

        first = []
        for r in (1, 2, 3):
            px, py, _ = _chip_rel(x, y, r)
            for a in range(n):
                first.append(copy(a, r - 1, src_half(a), half(a, me, c), (px, py, c)))
        own = [copy(a, 6, ins[a], outs[a].at[me], sib) for a in range(n)]
        for cp in first + own:
            cp.start()
        passed = []
        for r in (1, 2, 3):
            _, _, shard = _chip_rel(x, y, r)
            for a in range(n):
                copy(a, r - 1, src_half(a), half(a, shard, c), sib).wait_recv()
                cp = copy(a, 3 + r - 1, half(a, shard, c), half(a, shard, c), sib)
                cp.start()
                passed.append(cp)
        for r in (1, 2, 3):
            _, _, shard = _chip_rel(x, y, r)
            for a in range(n):
                copy(a, 3 + r - 1, src_half(a), half(a, shard, 1 - c), sib).wait_recv()
        for cp in own:
            cp.wait_recv()
        for cp in first + passed + own:
            cp.wait_send()

    anyspec = pl.BlockSpec(memory_space=pl.ANY)
    return pl.pallas_call(
        body, name=name, in_specs=[anyspec] * n, out_specs=[anyspec] * n,
        out_shape=[jax.ShapeDtypeStruct((N_CHIPS,) + a.shape, a.dtype) for a in arrs],
        scratch_shapes=[pltpu.SemaphoreType.DMA((slots * n,)), pltpu.SemaphoreType.DMA((slots * n,))],
        compiler_params=pltpu.CompilerParams(has_side_effects=True),
    )(*arrs)


def _allgather_whole(arr, name):
    def body(in_ref, out_ref, send_sems, recv_sems, loc_sem):
        x, y, c = _place()
        me = 2 * x + y
        local = pltpu.make_async_copy(in_ref, out_ref.at[me], loc_sem)
        local.start()
        sends = []
        for r in (1, 2, 3):
            px, py, _ = _chip_rel(x, y, r)
            sends.append(pltpu.make_async_remote_copy(
                src_ref=in_ref, dst_ref=out_ref.at[me], send_sem=send_sems.at[r - 1], recv_sem=recv_sems.at[r - 1],
                device_id=(px, py, c), device_id_type=MESH))
        for cp in sends:
            cp.start()
        for r in (1, 2, 3):
            px, py, shard = _chip_rel(x, y, r)
            pltpu.make_async_remote_copy(
                src_ref=in_ref, dst_ref=out_ref.at[shard], send_sem=send_sems.at[r - 1], recv_sem=recv_sems.at[r - 1],
                device_id=(px, py, c), device_id_type=MESH).wait_recv()
        for cp in sends:
            cp.wait_send()
        local.wait()

    anyspec = pl.BlockSpec(memory_space=pl.ANY)
    return pl.pallas_call(
        body, name=name, in_specs=[anyspec], out_specs=anyspec,
        out_shape=jax.ShapeDtypeStruct((N_CHIPS,) + arr.shape, arr.dtype),
        scratch_shapes=[pltpu.SemaphoreType.DMA((3,)), pltpu.SemaphoreType.DMA((3,)), pltpu.SemaphoreType.DMA],
        compiler_params=pltpu.CompilerParams(has_side_effects=True),
    )(arr)


def _swap_halves(grads, name):
    n = len(grads)

    def body(*refs):
        ins, outs = refs[:n], refs[n:2 * n]
        send_sems, recv_sems = refs[2 * n:]
        x, y, c = _place()
        sib = (x, y, 1 - c)
        copies = []
        for a in range(n):
            for j in range(N_CHIPS):
                copies.append(pltpu.make_async_remote_copy(
                    src_ref=ins[a].at[j, 1 - c], dst_ref=outs[a].at[j], send_sem=send_sems.at[a * N_CHIPS + j],
                    recv_sem=recv_sems.at[a * N_CHIPS + j], device_id=sib, device_id_type=MESH))
        for cp in copies:
            cp.start()
        for cp in copies:
            cp.wait()

    anyspec = pl.BlockSpec(memory_space=pl.ANY)
    return pl.pallas_call(
        body, name=name, in_specs=[anyspec] * n, out_specs=[anyspec] * n,
        out_shape=[jax.ShapeDtypeStruct((N_CHIPS,) + g.shape[2:], g.dtype) for g in grads],
        scratch_shapes=[pltpu.SemaphoreType.DMA((N_CHIPS * n,)), pltpu.SemaphoreType.DMA((N_CHIPS * n,))],
        compiler_params=pltpu.CompilerParams(has_side_effects=True),
    )(*grads)


def _scatter_chips(parts, name):
    n = len(parts)

    def body(*refs):
        ins, outs = refs[:n], refs[n:2 * n]
        send_sems, recv_sems = refs[2 * n:]
        x, y, c = _place()
        copies = []
        for r in (1, 2, 3):
            px, py, shard = _chip_rel(x, y, r)
            for a in range(n):
                copies.append(pltpu.make_async_remote_copy(
                    src_ref=ins[a].at[shard], dst_ref=outs[a].at[r - 1], send_sem=send_sems.at[a * 3 + r - 1],
                    recv_sem=recv_sems.at[a * 3 + r - 1], device_id=(px, py, c), device_id_type=MESH))
        for cp in copies:
            cp.start()
        for cp in copies:
            cp.wait()

    anyspec = pl.BlockSpec(memory_space=pl.ANY)
    return pl.pallas_call(
        body, name=name, in_specs=[anyspec] * n, out_specs=[anyspec] * n,
        out_shape=[jax.ShapeDtypeStruct((3,) + p.shape[1:], p.dtype) for p in parts],
        scratch_shapes=[pltpu.SemaphoreType.DMA((3 * n,)), pltpu.SemaphoreType.DMA((3 * n,))],
        compiler_params=pltpu.CompilerParams(has_side_effects=True),
    )(*parts)


def _join_halves(bufs, name):
    n = len(bufs)

    def body(*refs):
        outs = refs[n:2 * n]
        send_sems, recv_sems = refs[2 * n:]
        x, y, c = _place()
        copies = [pltpu.make_async_remote_copy(
            src_ref=outs[a].at[c], dst_ref=outs[a].at[c], send_sem=send_sems.at[a], recv_sem=recv_sems.at[a],
            device_id=(x, y, 1 - c), device_id_type=MESH) for a in range(n)]
        for cp in copies:
            cp.start()
        for cp in copies:
            cp.wait()

    anyspec = pl.BlockSpec(memory_space=pl.ANY)
    return pl.pallas_call(
        body, name=name, in_specs=[anyspec] * n, out_specs=[anyspec] * n,
        out_shape=[jax.ShapeDtypeStruct(b.shape, b.dtype) for b in bufs],
        input_output_aliases={a: a for a in range(n)},
        scratch_shapes=[pltpu.SemaphoreType.DMA((n,)), pltpu.SemaphoreType.DMA((n,))],
        compiler_params=pltpu.CompilerParams(has_side_effects=True),
    )(*bufs)


def _gather_all_devices(buf, name):
    def body(in_ref, out_ref, send_sems, recv_sems, loc_sem):
        x, y, c = _place()
        me = 4 * x + 2 * y + c
        local = pltpu.make_async_copy(in_ref, out_ref.at[me], loc_sem)
        local.start()
        sends = []
        for r in range(1, N_DEV):
            px, py, _ = _chip_rel(x, y, r // 2)
            pc = c if r % 2 == 0 else 1 - c
            sends.append(pltpu.make_async_remote_copy(
                src_ref=in_ref, dst_ref=out_ref.at[me], send_sem=send_sems.at[r - 1], recv_sem=recv_sems.at[r - 1],
                device_id=(px, py, pc), device_id_type=MESH))
        for cp in sends:
            cp.start()
        for r in range(1, N_DEV):
            px, py, _ = _chip_rel(x, y, r // 2)
            pc = c if r % 2 == 0 else 1 - c
            pltpu.make_async_remote_copy(
                src_ref=in_ref, dst_ref=out_ref.at[4 * px + 2 * py + pc], send_sem=send_sems.at[r - 1],
                recv_sem=recv_sems.at[r - 1], device_id=(px, py, pc), device_id_type=MESH).wait_recv()
        for cp in sends:
            cp.wait_send()
        local.wait()

    anyspec = pl.BlockSpec(memory_space=pl.ANY)
    return pl.pallas_call(
        body, name=name, in_specs=[anyspec], out_specs=anyspec,
        out_shape=jax.ShapeDtypeStruct((N_DEV,) + buf.shape, buf.dtype),
        scratch_shapes=[pltpu.SemaphoreType.DMA((N_DEV - 1,)), pltpu.SemaphoreType.DMA((N_DEV - 1,)),
                        pltpu.SemaphoreType.DMA],
        compiler_params=pltpu.CompilerParams(has_side_effects=True),
    )(buf)


def _add_half(grad, recv, sel, name):
    _, _, rh, cw = grad.shape
    rb = _pick(rh, (512, 256, 176, 128, 64, 32, 16, 8))

    def kern(sel_ref, g_ref, r_ref, o_ref):
        o_ref[...] = (g_ref[...] + r_ref[...]).astype(BF16)

    return pl.pallas_call(
        kern, name=name,
        grid_spec=pltpu.PrefetchScalarGridSpec(
            num_scalar_prefetch=1, grid=(N_CHIPS, rh // rb),
            in_specs=[pl.BlockSpec((None, None, rb, cw), lambda j, i, s: (j, s[0], i, 0)),
                      pl.BlockSpec((None, rb, cw), lambda j, i, s: (j, i, 0))],
            out_specs=pl.BlockSpec((None, rb, cw), lambda j, i, s: (j, i, 0))),
        out_shape=jax.ShapeDtypeStruct((N_CHIPS, rh, cw), BF16),
        compiler_params=_params(("parallel", "parallel")),
    )(sel, grad, recv)


def _add_own(grad, recv, got, sel, name):
    _, _, rh, cw = grad.shape
    rb = _pick(rh, (512, 256, 176, 128, 64, 32, 16, 8))

    def kern(sel_ref, g_ref, r_ref, b_ref, o_ref):
        own = g_ref[...] + r_ref[...]
        o_ref[...] = ((own + b_ref[0].astype(F32)) + b_ref[1].astype(F32)) + b_ref[2].astype(F32)

    return pl.pallas_call(
        kern, name=name,
        grid_spec=pltpu.PrefetchScalarGridSpec(
            num_scalar_prefetch=1, grid=(rh // rb,),
            in_specs=[pl.BlockSpec((None, None, rb, cw), lambda i, s: (s[1], s[0], i, 0)),
                      pl.BlockSpec((None, rb, cw), lambda i, s: (s[1], i, 0)),
                      pl.BlockSpec((3, rb, cw), lambda i, s: (0, i, 0))],
            out_specs=pl.BlockSpec((None, rb, cw), lambda i, s: (s[0], i, 0))),
        out_shape=jax.ShapeDtypeStruct((2, rh, cw), F32),
        compiler_params=_params(("parallel",)),
    )(sel, grad, recv, got)


def _reduce_scatter(grads, tag):
    x, y, c = _place()
    sel = jnp.stack([c, 2 * x + y]).astype(jnp.int32)
    split = [g.reshape(N_CHIPS, 2, g.shape[1] // 2, g.shape[2]) for g in grads]
    recv = _swap_halves(split, f"{tag}_swap")
    parts = [_add_half(g, r, sel, f"{tag}_addhalf{i}") for i, (g, r) in enumerate(zip(split, recv))]
    got = _scatter_chips(parts, f"{tag}_scatter")
    mine = [_add_own(g, r, b, sel, f"{tag}_addown{i}") for i, (g, r, b) in enumerate(zip(split, recv, got))]
    full = _join_halves(mine, f"{tag}_join")
    return [f.reshape(f.shape[0] * f.shape[1], f.shape[2]) for f in full]


def _ffn_fwd(h, hb, w_up, w_down, conv_w, conv_b, ln_g, ln_b, tag):
    u = _matmul(hb, w_up, mode="nn", nsh=N_CHIPS, name=f"{tag}_up")
    gact = _conv_gate_fwd(u, conv_w, conv_b, f"{tag}_gate")
    ffn = _matmul(gact, w_down, mode="nn", nsh=1, name=f"{tag}_down")
    xin, h2, h2b = _res_ln_fwd(h, ffn, ln_g, ln_b, f"{tag}_ln")
    return (u, gact, xin), h2, h2b


def _ffn_bwd(saved, hb_in, dxin, dxin_b, w_up, w_down, conv_w, conv_b, tag):
    u, gact, _ = saved
    dgact = _matmul(dxin_b, w_down, mode="nt", nsh=1, name=f"{tag}_ddown")
    dw_down = _matmul(gact, dxin_b, mode="tn", nsh=1, out_shards=1, name=f"{tag}_wdown")
    da, db, dcw, dcb = _conv_gate_bwd(u, dgact, conv_w, conv_b, f"{tag}_dgate")
    dh = _matmul(da, w_up, mode="nt", nsh=2, b_off=0, resid=dxin, alpha=ALPHA, name=f"{tag}_dup_a")
    dh = _matmul(db, w_up, mode="nt", nsh=2, b_off=2, resid=dh, alpha=1.0, name=f"{tag}_dup_b")
    dw_up = _matmul(hb_in, da, mode="tn", nsh=2, out_off=0, out_shards=N_CHIPS, name=f"{tag}_wup_a")
    dw_up = _matmul(hb_in, db, mode="tn", nsh=2, out_off=2, out_shards=N_CHIPS, out_init=dw_up, name=f"{tag}_wup_b")
    return dh, dw_up, dw_down, dcw, dcb


def _local_step(x2, tgt, xb, wg, sm):
    proj = _matmul(xb, wg["hg_in"], mode="nn", nsh=N_CHIPS, name="hg_in")
    yhg, o_raw, states = _hgrn2_fwd(proj, sm["lb_logits"], sm["hg_norm_g"], "hgrn2_fwd")
    mixed = _matmul(yhg, wg["hg_out"], mode="nn", nsh=1, name="hg_out")
    xin1, h1, h1b = _res_ln_fwd(x2, mixed, sm["ln1_g"][0:1], sm["ln1_b"][0:1], "l0_ln1")
    sv_f0, h2, h2b = _ffn_fwd(h1, h1b, wg["up0"], wg["dn0"], sm["conv_w"][0], sm["conv_b"][0:1],
                              sm["ln2_g"][0:1], sm["ln2_b"][0:1], "l0_ffn")
    pre = _matmul(h2b, wg["sg_in"], mode="nn", nsh=N_CHIPS, name="sg_in")
    ysg = _sg_gate_fwd(pre, sm["sg_ln_g"], sm["sg_ln_b"], sm["sg_w_s"], sm["sg_b_s_t"], "sg_gate")
    mixed = _matmul(ysg, wg["sg_out"], mode="nn", nsh=1, name="sg_out")
    xin3, h3, h3b = _res_ln_fwd(h2, mixed, sm["ln1_g"][1:2], sm["ln1_b"][1:2], "l1_ln1")
    sv_f1, _, _ = _ffn_fwd(h3, h3b, wg["up1"], wg["dn1"], sm["conv_w"][1], sm["conv_b"][1:2],
                           sm["ln2_g"][1:2], sm["ln2_b"][1:2], "l1_ffn")
    gw, gs = {}, {}
    dx, dxb, dg4, db4, loss = _ln_bwd(sv_f1[2], tgt, sm["ln2_g"][1:2], sm["ln2_b"][1:2], "l1_ln2_bwd", loss_head=True)
    dh3, gw["up1"], gw["dn1"], dcw1, dcb1 = _ffn_bwd(sv_f1, h3b, dx, dxb, wg["up1"], wg["dn1"],
                                                     sm["conv_w"][1], sm["conv_b"][1:2], "l1_ffn")
    dx, dxb, dg3, db3 = _ln_bwd(xin3, dh3, sm["ln1_g"][1:2], sm["ln1_b"][1:2], "l1_ln1_bwd")
    dysg = _matmul(dxb, wg["sg_out"], mode="nt", nsh=1, name="sg_dout")
    gw["sg_out"] = _matmul(ysg, dxb, mode="tn", nsh=1, out_shards=1, name="sg_wout")
    dpre, gs["sg_w_s"], dbs_t, gs["sg_ln_g"], gs["sg_ln_b"] = _sg_gate_bwd(
        pre, dysg, sm["sg_ln_g"], sm["sg_ln_b"], sm["sg_w_s"], sm["sg_b_s_t"], "sg_gate_bwd")
    gs["sg_b_s_t"] = dbs_t
    dh2 = _matmul(dpre, wg["sg_in"], mode="nt", nsh=N_CHIPS, resid=dx, alpha=ALPHA, name="sg_din")
    gw["sg_in"] = _matmul(h2b, dpre, mode="tn", nsh=N_CHIPS, out_shards=N_CHIPS, name="sg_win")
    dx, dxb, dg2, db2 = _ln_bwd(sv_f0[2], dh2, sm["ln2_g"][0:1], sm["ln2_b"][0:1], "l0_ln2_bwd")
    dh1, gw["up0"], gw["dn0"], dcw0, dcb0 = _ffn_bwd(sv_f0, h1b, dx, dxb, wg["up0"], wg["dn0"],
                                                     sm["conv_w"][0], sm["conv_b"][0:1], "l0_ffn")
    dx, dxb, dg1, db1 = _ln_bwd(xin1, dh1, sm["ln1_g"][0:1], sm["ln1_b"][0:1], "l0_ln1_bwd")
    dyhg = _matmul(dxb, wg["hg_out"], mode="nt", nsh=1, name="hg_dout")
    gw["hg_out"] = _matmul(yhg, dxb, mode="tn", nsh=1, out_shards=1, name="hg_wout")
    dparts = _hgrn2_bwd(proj, sm["lb_logits"], sm["hg_norm_g"], o_raw, states, dyhg, "hgrn2_bwd")
    gs["lb"], gs["hg_norm_g"] = dparts[4], dparts[5]
    gx = dx
    al = ALPHA
    gwin = None
    for j in range(4):
        gx = _matmul(dparts[j], wg["hg_in"], mode="nt", nsh=1, b_off=j, resid=gx, alpha=al, name=f"hg_din{j}")
        al = 1.0
        gwin = _matmul(xb, dparts[j], mode="tn", nsh=1, out_off=j, out_shards=N_CHIPS, out_init=gwin,
                       name=f"hg_win{j}")
    gw["hg_in"] = gwin
    gs["ln1_g"] = jnp.concatenate([dg1, dg3], axis=0)
    gs["ln1_b"] = jnp.concatenate([db1, db3], axis=0)
    gs["ln2_g"] = jnp.concatenate([dg2, dg4], axis=0)
    gs["ln2_b"] = jnp.concatenate([db2, db4], axis=0)
    gs["conv_w"] = jnp.stack([dcw0, dcw1], axis=0)
    gs["conv_b"] = jnp.concatenate([dcb0, dcb1], axis=0)
    return loss, gx, gw, gs


_BIG = ("hg_in", "hg_out", "sg_in", "sg_out", "up0", "up1", "dn0", "dn1")
_SMALL_ORDER = ("lb", "hg_norm_g", "sg_w_s", "sg_b_s_t", "conv_b", "ln1_g", "ln1_b", "ln2_g", "ln2_b",
                "conv_w", "sg_ln_g", "sg_ln_b")


PACK_ROWS = 512


def _pack(parts):
    flat, layout, off = [], [], 0
    for k in _SMALL_ORDER:
        a = parts[k]
        n = a.size
        pad = (-n) % LANES
        flat.append(jnp.pad(a.reshape(-1), (0, pad)))
        layout.append((k, off, n, a.shape))
        off += n + pad
    flat.append(jnp.zeros(((-off) % (PACK_ROWS * LANES),), F32))
    return jnp.concatenate(flat).reshape(-1, LANES), layout


def _unpack(buf, layout):
    flat = buf.reshape(-1)
    return {k: flat[off:off + n].reshape(shape) for k, off, n, shape in layout}


def kernel(x, lb_logits, hg_w_in, hg_norm_g, hg_w_out, sg_w_in, sg_ln_g, sg_ln_b, sg_w_s, sg_b_s, sg_w_out, ffn_w_up, ffn_conv_w, ffn_conv_b, ffn_w_down, ln1_g, ln1_b, ln2_g, ln2_b, loss_target, m_lb_logits, m_hg_w_in, m_hg_norm_g, m_hg_w_out, m_sg_w_in, m_sg_ln_g, m_sg_ln_b, m_sg_w_s, m_sg_b_s, m_sg_w_out, m_ffn_w_up, m_ffn_conv_w, m_ffn_conv_b, m_ffn_w_down, m_ln1_g, m_ln1_b, m_ln2_g, m_ln2_b, v_lb_logits, v_hg_w_in, v_hg_norm_g, v_hg_w_out, v_sg_w_in, v_sg_ln_g, v_sg_ln_b, v_sg_w_s, v_sg_b_s, v_sg_w_out, v_ffn_w_up, v_ffn_conv_w, v_ffn_conv_b, v_ffn_w_down, v_ln1_g, v_ln1_b, v_ln2_g, v_ln2_b):
    names = ("lb_logits", "hg_w_in", "hg_norm_g", "hg_w_out", "sg_w_in", "sg_ln_g", "sg_ln_b", "sg_w_s", "sg_b_s",
             "sg_w_out", "ffn_w_up", "ffn_conv_w", "ffn_conv_b", "ffn_w_down", "ln1_g", "ln1_b", "ln2_g", "ln2_b")
    w = dict(zip(names, (lb_logits, hg_w_in, hg_norm_g, hg_w_out, sg_w_in, sg_ln_g, sg_ln_b, sg_w_s, sg_b_s,
                         sg_w_out, ffn_w_up, ffn_conv_w, ffn_conv_b, ffn_w_down, ln1_g, ln1_b, ln2_g, ln2_b)))
    mom = dict(zip(names, (m_lb_logits, m_hg_w_in, m_hg_norm_g, m_hg_w_out, m_sg_w_in, m_sg_ln_g, m_sg_ln_b, m_sg_w_s,
                           m_sg_b_s, m_sg_w_out, m_ffn_w_up, m_ffn_conv_w, m_ffn_conv_b, m_ffn_w_down, m_ln1_g,
                           m_ln1_b, m_ln2_g, m_ln2_b)))
    var = dict(zip(names, (v_lb_logits, v_hg_w_in, v_hg_norm_g, v_hg_w_out, v_sg_w_in, v_sg_ln_g, v_sg_ln_b, v_sg_w_s,
                           v_sg_b_s, v_sg_w_out, v_ffn_w_up, v_ffn_conv_w, v_ffn_conv_b, v_ffn_w_down, v_ln1_g,
                           v_ln1_b, v_ln2_g, v_ln2_b)))
    x2, tgt = x[0], loss_target[0]
    d = x2.shape[1]
    fq = ffn_conv_w.shape[2]
    dq = sg_ln_g.shape[1]
    cx, cy, _ = _place()
    me = 2 * cx + cy

    shards = {"hg_in": hg_w_in[0], "hg_out": hg_w_out[0], "sg_in": sg_w_in[0], "sg_out": sg_w_out[0],
              "up0": ffn_w_up[0], "up1": ffn_w_up[1], "dn0": ffn_w_down[0], "dn1": ffn_w_down[1]}
    gathered = _allgather_split([shards[k].astype(BF16) for k in _BIG], "gather_weights")
    wg = dict(zip(_BIG, gathered))
    for k in ("hg_out", "sg_out", "dn0", "dn1"):
        g = wg[k]
        wg[k] = g.reshape(1, g.shape[0] * g.shape[1], g.shape[2])
    wide = max(fq, dq)
    tiny = jnp.concatenate([jnp.pad(ffn_conv_w.reshape(6, fq), ((0, 0), (0, wide - fq))),
                            jnp.pad(sg_ln_g, ((0, 0), (0, wide - dq))),
                            jnp.pad(sg_ln_b, ((0, 0), (0, wide - dq)))], axis=0)
    tiny_all = _allgather_whole(tiny, "gather_small")
    conv_w_full = jnp.transpose(tiny_all[:, 0:6, :fq].reshape(N_CHIPS, 2, 3, fq), (1, 2, 0, 3)).reshape(2, 3, N_CHIPS * fq)
    sm = {"lb_logits": lb_logits, "hg_norm_g": hg_norm_g, "ln1_g": ln1_g, "ln1_b": ln1_b, "ln2_g": ln2_g,
          "ln2_b": ln2_b, "conv_w": conv_w_full, "conv_b": ffn_conv_b,
          "sg_ln_g": tiny_all[:, 6, :dq].reshape(1, N_CHIPS * dq),
          "sg_ln_b": tiny_all[:, 7, :dq].reshape(1, N_CHIPS * dq),
          "sg_w_s": sg_w_s[0], "sg_b_s_t": jnp.transpose(sg_b_s[0])}

    loss_row, grad_x, gw, gs = _local_step(x2, tgt, x2.astype(BF16), wg, sm)
    loss = lax.psum(loss_row[0, 0], ("x", "y", "c"))

    big_full = []
    for k in _BIG:
        g = gw[k]
        if k in ("hg_out", "sg_out", "dn0", "dn1"):
            g = g.reshape(N_CHIPS, g.shape[1] // N_CHIPS, g.shape[2])
        big_full.append(g)
    red = dict(zip(_BIG, _reduce_scatter(big_full, "rs")))
    packed, layout = _pack(gs)
    summed = _unpack(_sum_leading(_gather_all_devices(packed, "gather_small_grads"), "sum_small_grads"), layout)

    grads = {
        "lb_logits": _lb_logits_grad(lb_logits, summed["lb"], "lb_logits_grad"),
        "hg_norm_g": summed["hg_norm_g"],
        "sg_ln_g": lax.dynamic_slice_in_dim(summed["sg_ln_g"], me * dq, dq, axis=1),
        "sg_ln_b": lax.dynamic_slice_in_dim(summed["sg_ln_b"], me * dq, dq, axis=1),
        "sg_w_s": summed["sg_w_s"][None], "sg_b_s": jnp.transpose(summed["sg_b_s_t"])[None],
        "ffn_conv_w": lax.dynamic_slice_in_dim(summed["conv_w"], me * fq, fq, axis=2),
        "ffn_conv_b": summed["conv_b"],
        "ln1_g": summed["ln1_g"], "ln1_b": summed["ln1_b"], "ln2_g": summed["ln2_g"], "ln2_b": summed["ln2_b"],
    }

    big_parts = {"hg_w_in": ("hg_in",), "hg_w_out": ("hg_out",), "sg_w_in": ("sg_in",), "sg_w_out": ("sg_out",),
                 "ffn_w_up": ("up0", "up1"), "ffn_w_down": ("dn0", "dn1")}
    delta, new_m, new_v = {}, {}, {}
    for k, parts in big_parts.items():
        delta[k], new_m[k], new_v[k], grads[k] = _adamw(w[k], [red[p] for p in parts], mom[k], var[k], f"adamw_{k}")
    small_names = [k for k in names if k not in big_parts]

    def pack_small(src):
        flat = [src[k].reshape(-1) for k in small_names]
        n = sum(a.size for a in flat)
        flat.append(jnp.zeros(((-n) % (PACK_ROWS * LANES),), F32))
        return jnp.concatenate(flat).reshape(1, -1, LANES)

    outs = _adamw(pack_small(w), [pack_small(grads)[0]], pack_small(mom), pack_small(var), "adamw_small")
    off = 0
    for k in small_names:
        n = w[k].size
        for dst, o in zip((delta, new_m, new_v), outs):
            dst[k] = o.reshape(-1)[off:off + n].reshape(w[k].shape)
        off += n

    return (loss, grad_x[None], *[grads[k] for k in names], *[delta[k] for k in names],
            *[new_m[k] for k in names], *[new_v[k] for k in names])
```

```python
import functools

import jax
import jax.numpy as jnp
from jax import lax
from jax.experimental import pallas as pl
from jax.experimental.pallas import tpu as pltpu

F32 = jnp.float32
BF16 = jnp.bfloat16
HI = lax.Precision.HIGHEST
MESH = pl.DeviceIdType.MESH

ALPHA = (2 * 2) ** 0.25
LN_EPS = 1e-5
RMS_EPS = 1e-6
ADAM_LR, ADAM_B1, ADAM_B2, ADAM_EPS, ADAM_WD, ADAM_STEP = 0.001, 0.9, 0.999, 1e-08, 0.01, 10

LANES = 128
SUB = 16
GCHUNK = 128
VMEM_LIMIT = 56 * 1024 * 1024
N_CHIPS = 4
N_DEV = 8

NT = (((1,), (1,)), ((), ()))
TN = (((0,), (0,)), ((), ()))
NN = (((1,), (0,)), ((), ()))


def _pick(dim, prefs):
    for p in prefs:
        if dim % p == 0:
            return p
    return dim


def _params(sem=None, **kw):
    return pltpu.CompilerParams(dimension_semantics=sem, vmem_limit_bytes=VMEM_LIMIT, **kw)


def _sigmoid_pair(x):
    e = jnp.exp(-jnp.abs(x))
    inv = 1.0 / (1.0 + e)
    pos = x >= 0
    return jnp.where(pos, inv, e * inv), jnp.where(pos, e * inv, inv)


def _ln_hat(x):
    mu = jnp.mean(x, axis=-1, keepdims=True)
    xc = x - mu
    var = jnp.mean(xc * xc, axis=-1, keepdims=True)
    rstd = lax.rsqrt(var + LN_EPS)
    return xc * rstd, rstd


def _lower_bound(logits):
    m = jnp.max(logits, axis=0, keepdims=True)
    e = jnp.exp(logits - m)
    return e[0:1, :] / jnp.sum(e, axis=0, keepdims=True)


def _matmul(a, b, *, mode, name, out_dtype=F32, resid=None, alpha=1.0, b_off=0, nsh=None,
            out_init=None, out_off=0, out_shards=None):
    if mode == "nn":
        m, kdim = a.shape
        _, _, ns = b.shape
        bm = _pick(m, (1024, 512, 256, 128))
        bn = _pick(ns, (1024, 1408, 512, 256, 128))
        bk = _pick(kdim, (512, 256, 128))
        nps = ns // bn
        grid = (m // bm, nsh * nps, kdim // bk)
        a_spec = pl.BlockSpec((bm, bk), lambda i, j, k: (i, k))
        b_spec = pl.BlockSpec((None, bk, bn), lambda i, j, k: (b_off + j // nps, k, j % nps))
        o_spec = pl.BlockSpec((bm, bn), lambda i, j, k: (i, j))
        out_shape = jax.ShapeDtypeStruct((m, nsh * ns), out_dtype)
        dims = NN
    elif mode == "nt":
        m = a.shape[0]
        _, kdim, ns = b.shape
        bm = _pick(m, (1024, 512, 256, 128))
        bn = _pick(kdim, (1024, 1408, 512, 256, 128))
        bk = _pick(ns, (512, 1408, 256, 128))
        kps = ns // bk
        grid = (m // bm, kdim // bn, nsh * kps)
        a_spec = pl.BlockSpec((bm, bk), lambda i, j, k: (i, k))
        b_spec = pl.BlockSpec((None, bn, bk), lambda i, j, k: (b_off + k // kps, j, k % kps))
        o_spec = pl.BlockSpec((bm, bn), lambda i, j, k: (i, j))
        out_shape = jax.ShapeDtypeStruct((m, kdim), out_dtype)
        dims = NT
    else:
        t, kdim = a.shape
        ns = b.shape[1] // nsh
        bm = _pick(kdim, (1024, 1408, 512, 256, 128))
        bn = _pick(ns, (1024, 1408, 512, 256, 128))
        bk = _pick(t, (512, 256, 128))
        nps = ns // bn
        grid = (kdim // bm, nsh * nps, t // bk)
        a_spec = pl.BlockSpec((bk, bm), lambda i, j, k: (k, i))
        b_spec = pl.BlockSpec((bk, bn), lambda i, j, k: (k, j))
        o_spec = pl.BlockSpec((None, bm, bn), lambda i, j, k: (out_off + j // nps, i, j % nps))
        out_shape = jax.ShapeDtypeStruct((out_shards, kdim, ns), out_dtype)
        dims = TN
    nk = grid[2]
    has_resid = resid is not None
    has_init = out_init is not None

    def kern(*refs):
        a_ref, b_ref = refs[0], refs[1]
        r_ref = refs[2] if has_resid else None
        o_ref, acc_ref = refs[-2], refs[-1]
        k = pl.program_id(2)

        @pl.when(k == 0)
        def _():
            acc_ref[...] = jnp.zeros_like(acc_ref)

        acc_ref[...] += lax.dot_general(a_ref[...], b_ref[...], dims, preferred_element_type=F32)

        @pl.when(k == nk - 1)
        def _():
            r = acc_ref[...]
            if has_resid:
                r = r + alpha * r_ref[...]
            o_ref[...] = r.astype(o_ref.dtype)

    in_specs = [a_spec, b_spec]
    operands = [a, b]
    if has_resid:
        in_specs.append(pl.BlockSpec((bm, bn), lambda i, j, k: (i, j)))
        operands.append(resid)
    aliases = {}
    if has_init:
        in_specs.append(pl.BlockSpec(memory_space=pl.ANY))
        operands.append(out_init)
        aliases = {len(operands) - 1: 0}
    return pl.pallas_call(
        kern, name=name, grid=grid, in_specs=in_specs, out_specs=o_spec, out_shape=out_shape,
        scratch_shapes=[pltpu.VMEM((bm, bn), F32)], input_output_aliases=aliases,
        compiler_params=_params(("parallel", "parallel", "arbitrary")),
    )(*operands)


def _res_ln_fwd(h_prev, sub, g, b, name):
    t, d = h_prev.shape
    tb = _pick(t, (256, 128, 64, 32, 16))

    def kern(hp_ref, s_ref, g_ref, b_ref, xin_ref, h_ref, hb_ref):
        xin = ALPHA * hp_ref[...] + s_ref[...]
        xhat, _ = _ln_hat(xin)
        h = xhat * g_ref[...] + b_ref[...]
        xin_ref[...] = xin
        h_ref[...] = h
        hb_ref[...] = h.astype(BF16)

    row = pl.BlockSpec((tb, d), lambda i: (i, 0))
    vec = pl.BlockSpec((1, d), lambda i: (0, 0))
    return pl.pallas_call(
        kern, name=name, grid=(t // tb,), in_specs=[row, row, vec, vec], out_specs=[row, row, row],
        out_shape=[jax.ShapeDtypeStruct((t, d), F32), jax.ShapeDtypeStruct((t, d), F32),
                   jax.ShapeDtypeStruct((t, d), BF16)],
        compiler_params=_params(("parallel",)),
    )(h_prev, sub, g, b)


def _ln_bwd(xin, dy_or_target, g, b, name, loss_head=False):
    t, d = xin.shape
    tb = _pick(t, (256, 128, 64, 32, 16))
    nb = t // tb

    def kern(x_ref, dy_ref, g_ref, b_ref, dx_ref, dxb_ref, dg_ref, db_ref, *rest):
        i = pl.program_id(0)
        xhat, rstd = _ln_hat(x_ref[...])
        gv = g_ref[...]
        if loss_head:
            loss_ref = rest[0]
            err = xhat * gv + b_ref[...] - dy_ref[...]
            dy = err * (1.0 / d)
            part = 0.5 * jnp.sum(jnp.sum(err * err, axis=1, keepdims=True), axis=0, keepdims=True) * (1.0 / d)
        else:
            dy = dy_ref[...]

        @pl.when(i == 0)
        def _():
            dg_ref[...] = jnp.zeros_like(dg_ref)
            db_ref[...] = jnp.zeros_like(db_ref)
            if loss_head:
                loss_ref[...] = jnp.zeros_like(loss_ref)

        dg_ref[...] += jnp.sum(dy * xhat, axis=0, keepdims=True)
        db_ref[...] += jnp.sum(dy, axis=0, keepdims=True)
        if loss_head:
            loss_ref[...] += jnp.broadcast_to(part, loss_ref.shape)
        dxh = dy * gv
        m1 = jnp.mean(dxh, axis=-1, keepdims=True)
        m2 = jnp.mean(dxh * xhat, axis=-1, keepdims=True)
        dx = rstd * (dxh - m1 - xhat * m2)
        dx_ref[...] = dx
        dxb_ref[...] = dx.astype(BF16)

    row = pl.BlockSpec((tb, d), lambda i: (i, 0))
    vec = pl.BlockSpec((1, d), lambda i: (0, 0))
    out_specs = [row, row, vec, vec]
    out_shape = [jax.ShapeDtypeStruct((t, d), F32), jax.ShapeDtypeStruct((t, d), BF16),
                 jax.ShapeDtypeStruct((1, d), F32), jax.ShapeDtypeStruct((1, d), F32)]
    if loss_head:
        out_specs.append(pl.BlockSpec((1, LANES), lambda i: (0, 0)))
        out_shape.append(jax.ShapeDtypeStruct((1, LANES), F32))
    return pl.pallas_call(
        kern, name=name, grid=(nb,), in_specs=[row, row, vec, vec], out_specs=out_specs, out_shape=out_shape,
        compiler_params=_params(("arbitrary",)),
    )(xin, dy_or_target, g, b)


def _conv_gate_fwd(u, conv_w, conv_b, name):
    t, f2 = u.shape
    f = f2 // 2
    tb = _pick(t, (512, 256, 128, 64, 32, 16))
    cn = _pick(f, (1408, 1024, 512, 256, 128))
    ncb = f // cn
    hb = tb // 8

    def kern(a_ref, ah_ref, b_ref, w_ref, cb_ref, o_ref):
        i = pl.program_id(0)
        a = a_ref[...]
        halo = jnp.where(i > 0, ah_ref[...], 0.0)
        rid = lax.broadcasted_iota(jnp.int32, a.shape, 0)
        s1 = jnp.where(rid == 0, halo[7:8, :], pltpu.roll(a, 1, 0))
        s2 = jnp.where(rid == 0, halo[6:7, :], jnp.where(rid == 1, halo[7:8, :], pltpu.roll(a, 2, 0)))
        w = w_ref[...]
        conv = w[2:3, :] * a + w[1:2, :] * s1 + w[0:1, :] * s2 + cb_ref[...]
        sp, _ = _sigmoid_pair(conv)
        o_ref[...] = (conv * sp * b_ref[...]).astype(BF16)

    return pl.pallas_call(
        kern, name=name, grid=(t // tb, ncb),
        in_specs=[pl.BlockSpec((tb, cn), lambda i, j: (i, j)),
                  pl.BlockSpec((8, cn), lambda i, j: (jnp.maximum(i * hb - 1, 0), j)),
                  pl.BlockSpec((tb, cn), lambda i, j: (i, j + ncb)),
                  pl.BlockSpec((3, cn), lambda i, j: (0, j)),
                  pl.BlockSpec((1, cn), lambda i, j: (0, j))],
        out_specs=pl.BlockSpec((tb, cn), lambda i, j: (i, j)),
        out_shape=jax.ShapeDtypeStruct((t, f), BF16),
        compiler_params=_params(("parallel", "parallel")),
    )(u, u, u, conv_w, conv_b)


def _conv_gate_bwd(u, dgact, conv_w, conv_b, name):
    t, f2 = u.shape
    f = f2 // 2
    tb = _pick(t, (512, 256, 128, 64, 32, 16))
    cn = _pick(f, (1408, 1024, 512, 256, 128))
    ncb = f // cn
    hb = tb // 8
    nb = t // tb
    last8 = t // 8 - 1

    def kern(a_ref, ap_ref, an_ref, b_ref, bn_ref, dg_ref, dgn_ref, w_ref, cb_ref,
             da_ref, db_ref, dw_ref, dcb_ref):
        i = pl.program_id(1)
        a = a_ref[...]
        w = w_ref[...]
        ext = jnp.concatenate([jnp.where(i > 0, ap_ref[...], 0.0), a, an_ref[...]], axis=0)
        e1 = pltpu.roll(ext, 1, 0)
        e2 = pltpu.roll(ext, 2, 0)
        conv = (w[2:3, :] * ext + w[1:2, :] * e1 + w[0:1, :] * e2 + cb_ref[...])[8:, :]
        bmn = jnp.concatenate([b_ref[...], bn_ref[...]], axis=0)
        dgmn = jnp.concatenate([dg_ref[...], jnp.where(i < nb - 1, dgn_ref[...], 0.0)], axis=0)
        sp, sn = _sigmoid_pair(conv)
        da = dgmn * bmn * (sp * (1.0 + conv * sn))
        n = tb + 8
        dap = w[2:3, :] * da + w[1:2, :] * pltpu.roll(da, n - 1, 0) + w[0:1, :] * pltpu.roll(da, n - 2, 0)
        da_ref[...] = dap[:tb, :].astype(BF16)
        db_ref[...] = (dg_ref[...] * (conv * sp)[:tb, :]).astype(BF16)
        dam = da[:tb, :]

        @pl.when(i == 0)
        def _():
            dw_ref[...] = jnp.zeros_like(dw_ref)
            dcb_ref[...] = jnp.zeros_like(dcb_ref)

        dw = jnp.concatenate([jnp.sum(dam * e2[8:8 + tb, :], axis=0, keepdims=True),
                              jnp.sum(dam * e1[8:8 + tb, :], axis=0, keepdims=True),
                              jnp.sum(dam * a, axis=0, keepdims=True)], axis=0)
        dw_ref[...] += dw
        dcb_ref[...] += jnp.sum(dam, axis=0, keepdims=True)

    main_a = pl.BlockSpec((tb, cn), lambda j, i: (i, j))
    prev_a = pl.BlockSpec((8, cn), lambda j, i: (jnp.maximum(i * hb - 1, 0), j))
    next_a = pl.BlockSpec((8, cn), lambda j, i: (jnp.minimum((i + 1) * hb, last8), j))
    main_b = pl.BlockSpec((tb, cn), lambda j, i: (i, j + ncb))
    next_b = pl.BlockSpec((8, cn), lambda j, i: (jnp.minimum((i + 1) * hb, last8), j + ncb))
    return pl.pallas_call(
        kern, name=name, grid=(ncb, nb),
        in_specs=[main_a, prev_a, next_a, main_b, next_b, main_a, next_a,
                  pl.BlockSpec((3, cn), lambda j, i: (0, j)), pl.BlockSpec((1, cn), lambda j, i: (0, j))],
        out_specs=[main_a, main_a, pl.BlockSpec((3, cn), lambda j, i: (0, j)),
                   pl.BlockSpec((1, cn), lambda j, i: (0, j))],
        out_shape=[jax.ShapeDtypeStruct((t, f), BF16), jax.ShapeDtypeStruct((t, f), BF16),
                   jax.ShapeDtypeStruct((3, f), F32), jax.ShapeDtypeStruct((1, f), F32)],
        compiler_params=_params(("parallel", "arbitrary")),
    )(u, u, u, u, u, dgact, dgact, conv_w, conv_b)


def _hg_gates(qp, fp, lb):
    sq, _ = _sigmoid_pair(qp)
    sf, snf = _sigmoid_pair(fp)
    forget = lb + (1.0 - lb) * sf
    return sq, sf, snf, forget, jnp.log(forget), (1.0 - lb) * snf


def _tri(lower):
    r = lax.broadcasted_iota(jnp.int32, (SUB, SUB), 0)
    c = lax.broadcasted_iota(jnp.int32, (SUB, SUB), 1)
    return ((r >= c) if lower else (r <= c)).astype(BF16)


def _split2(x):
    hi = x.astype(BF16)
    return hi, (x - hi.astype(F32)).astype(BF16)


def _dot3(a, b, dims):
    (ah, al), (bh, bl) = a, b
    return (lax.dot_general(ah, bh, dims, preferred_element_type=F32)
            + (lax.dot_general(ah, bl, dims, preferred_element_type=F32)
               + lax.dot_general(al, bh, dims, preferred_element_type=F32)))


def _running_sum(tri, x):
    hi, lo = _split2(x)
    rest = (x - hi.astype(F32)) - lo.astype(F32)
    return (lax.dot_general(tri, hi, NN, preferred_element_type=F32)
            + (lax.dot_general(tri, lo, NN, preferred_element_type=F32)
               + lax.dot_general(tri, rest.astype(BF16), NN, preferred_element_type=F32)))


HEADS_PER_STEP = 4
STEP_UNROLL = 2


def _hgrn2_fwd(proj, lb_logits, norm_g, name):
    t, d4 = proj.shape
    d = d4 // 4
    nh = d // LANES
    hb = _pick(nh, (HEADS_PER_STEP, 2, 1))
    wb = hb * LANES
    tb = _pick(t, (256, 128, 64, 32, 16))
    nb = t // tb
    nsc = tb // SUB

    def kern(q_ref, f_ref, i_ref, g_ref, lbl_ref, ng_ref, y_ref, o_ref, st_ref, s_ref):
        @pl.when(pl.program_id(1) == 0)
        def _():
            s_ref[...] = jnp.zeros_like(s_ref)

        lb_all = _lower_bound(lbl_ref[...])
        ng_all = ng_ref[...]
        ltri = _tri(True)
        rcol = lax.broadcasted_iota(jnp.int32, (SUB, 1), 0)

        heads = [slice(h * LANES, (h + 1) * LANES) for h in range(hb)]

        def step(sc, carry):
            rows = pl.ds(pl.multiple_of(sc * SUB, SUB), SUB)
            qp, fp, v, gp = q_ref[rows, :], f_ref[rows, :], i_ref[rows, :], g_ref[rows, :]
            sq, _, _, _, lf, k = _hg_gates(qp, fp, lb_all)
            q = qp * sq
            bl = _running_sum(ltri, lf)
            bend = bl[SUB - 1:SUB, :]
            dec = jnp.exp(bend)
            qs2 = _split2(q * jnp.exp(bl))
            kd2 = _split2(k * jnp.exp(bend - bl))
            v2 = _split2(v)
            states = [s_ref[h] for h in range(hb)]
            o = [_dot3((qs2[0][:, c], qs2[1][:, c]), _split2(states[h]), NT) for h, c in enumerate(heads)]
            for s in range(SUB):
                e = jnp.exp(jnp.minimum(bl - bl[s:s + 1, :], 0.0))
                p = q * e * k[s:s + 1, :]
                for h, c in enumerate(heads):
                    a = jnp.sum(p[:, c], axis=1, keepdims=True)
                    o[h] = o[h] + jnp.where(rcol >= s, a, 0.0) * v[s:s + 1, c]
            for h, c in enumerate(heads):
                st_ref[sc, h] = states[h]
                s_ref[h] = states[h] * dec[:, c] + _dot3((v2[0][:, c], v2[1][:, c]), (kd2[0][:, c], kd2[1][:, c]), TN)
            o_ref[rows, :] = jnp.concatenate(o, axis=1)
            on = jnp.concatenate(
                [oh * lax.rsqrt(jnp.mean(oh * oh, axis=1, keepdims=True) + RMS_EPS) for oh in o], axis=1)
            sg, _ = _sigmoid_pair(gp)
            y_ref[rows, :] = (on * ng_all * (gp * sg)).astype(BF16)
            return carry

        lax.fori_loop(0, nsc, step, 0, unroll=STEP_UNROLL)

    def col(off):
        return pl.BlockSpec((tb, wb), lambda h, j: (j, h + off * (nh // hb)))

    return pl.pallas_call(
        kern, name=name, grid=(nh // hb, nb),
        in_specs=[col(0), col(1), col(2), col(3),
                  pl.BlockSpec((3, wb), lambda h, j: (0, h)), pl.BlockSpec((1, wb), lambda h, j: (0, h))],
        out_specs=[col(0), col(0), pl.BlockSpec((nsc, hb, LANES, LANES), lambda h, j: (j, h, 0, 0))],
        out_shape=[jax.ShapeDtypeStruct((t, d), BF16), jax.ShapeDtypeStruct((t, d), F32),
                   jax.ShapeDtypeStruct((t // SUB, nh, LANES, LANES), F32)],
        scratch_shapes=[pltpu.VMEM((hb, LANES, LANES), F32)],
        compiler_params=_params(("parallel", "arbitrary")),
    )(proj, proj, proj, proj, lb_logits, norm_g)


def _hgrn2_bwd(proj, lb_logits, norm_g, o_raw, states, dy, name):
    t, d4 = proj.shape
    d = d4 // 4
    nh = d // LANES
    hb = _pick(nh, (HEADS_PER_STEP, 2, 1))
    wb = hb * LANES
    tb = _pick(t, (256, 128, 64, 32, 16))
    nb = t // tb
    nsc = tb // SUB

    def kern(q_ref, f_ref, i_ref, g_ref, lbl_ref, ng_ref, o_ref, st_ref, dy_ref,
             dq_ref, df_ref, di_ref, dgp_ref, dlb_ref, dng_ref, ds_ref, gc_ref):
        j = pl.program_id(1)

        @pl.when(j == 0)
        def _():
            ds_ref[...] = jnp.zeros_like(ds_ref)
            gc_ref[...] = jnp.zeros_like(gc_ref)
            dlb_ref[...] = jnp.zeros_like(dlb_ref)
            dng_ref[...] = jnp.zeros_like(dng_ref)

        lb_all = _lower_bound(lbl_ref[...])
        ng_all = ng_ref[...]
        ltri, utri = _tri(True), _tri(False)
        rcol = lax.broadcasted_iota(jnp.int32, (SUB, 1), 0)
        rid = lax.broadcasted_iota(jnp.int32, (SUB, wb), 0)

        heads = [slice(h * LANES, (h + 1) * LANES) for h in range(hb)]

        def per_head(fn):
            return jnp.concatenate([jnp.broadcast_to(fn(c), (SUB, LANES)) for c in heads], axis=1)

        def step(it, carry):
            sc = nsc - 1 - it
            rows = pl.ds(pl.multiple_of(sc * SUB, SUB), SUB)
            qp, fp, v, gp = q_ref[rows, :], f_ref[rows, :], i_ref[rows, :], g_ref[rows, :]
            o, dyv = o_ref[rows, :], dy_ref[rows, :]
            sq, sf, snf, forget, lf, k = _hg_gates(qp, fp, lb_all)
            q = qp * sq
            bl = _running_sum(ltri, lf)
            ebl = jnp.exp(bl)
            bend = bl[SUB - 1:SUB, :]
            dec = jnp.exp(bend)
            dte = jnp.exp(bend - bl)
            r = per_head(lambda c: lax.rsqrt(jnp.mean(o[:, c] * o[:, c], axis=1, keepdims=True) + RMS_EPS))
            ohat = o * r
            sg, sng = _sigmoid_pair(gp)
            don = dyv * (gp * sg)
            dgp_ref[rows, :] = (dyv * (ohat * ng_all) * (sg * (1.0 + gp * sng))).astype(BF16)
            dng_ref[...] += jnp.sum(don * ohat, axis=0, keepdims=True)
            doh = don * ng_all
            dot_oh = doh * ohat
            do = r * (doh - ohat * per_head(lambda c: jnp.mean(dot_oh[:, c], axis=1, keepdims=True)))
            do2, qs2, kd2, v2 = _split2(do), _split2(q * ebl), _split2(k * dte), _split2(v)
            dq_h, dk_h, dv_h = [], [], []
            for h, c in enumerate(heads):
                dstate = ds_ref[h]
                ds2 = _split2(dstate)
                doc = (do2[0][:, c], do2[1][:, c])
                dq_h.append(_dot3(doc, _split2(st_ref[sc, h]), NN))
                dv_h.append(_dot3((kd2[0][:, c], kd2[1][:, c]), ds2, NT))
                dk_h.append(_dot3((v2[0][:, c], v2[1][:, c]), ds2, NN))
                ds_ref[h] = dstate * dec[:, c] + _dot3(doc, (qs2[0][:, c], qs2[1][:, c]), TN)
            dq = jnp.concatenate(dq_h, axis=1) * ebl
            dk = jnp.concatenate(dk_h, axis=1) * dte
            dv = jnp.concatenate(dv_h, axis=1)
            dki = jnp.zeros((SUB, wb), F32)
            dvi = jnp.zeros((SUB, wb), F32)
            for s in range(SUB):
                e = jnp.exp(jnp.minimum(bl - bl[s:s + 1, :], 0.0))
                qe = q * e
                ks = k[s:s + 1, :]
                live = rcol >= s
                pk = qe * ks
                pv = do * v[s:s + 1, :]
                a = per_head(lambda c: jnp.where(live, jnp.sum(pk[:, c], axis=1, keepdims=True), 0.0))
                da = per_head(lambda c: jnp.where(live, jnp.sum(pv[:, c], axis=1, keepdims=True), 0.0))
                dq = dq + da * (e * ks)
                dki = jnp.where(rid == s, jnp.sum(da * qe, axis=0, keepdims=True), dki)
                dvi = jnp.where(rid == s, jnp.sum(a * do, axis=0, keepdims=True), dvi)
            dk = dk + dki
            dv = dv + dvi
            w = q * dq - k * dk
            gc = gc_ref[...]
            dlf = _running_sum(utri, w) + gc
            gc_ref[...] = gc + jnp.sum(w, axis=0, keepdims=True)
            t1 = dlf / forget - dk
            df_ref[rows, :] = ((1.0 - lb_all) * sf * snf * t1).astype(BF16)
            dlb_ref[...] += jnp.sum(snf * t1, axis=0, keepdims=True)
            dq_ref[rows, :] = (dq * (sq * (1.0 + qp * (1.0 - sq)))).astype(BF16)
            di_ref[rows, :] = dv.astype(BF16)
            return carry

        lax.fori_loop(0, nsc, step, 0, unroll=STEP_UNROLL)

    def col(off):
        return pl.BlockSpec((tb, wb), lambda h, j: (nb - 1 - j, h + off * (nh // hb)))

    vec = pl.BlockSpec((1, wb), lambda h, j: (0, h))
    return pl.pallas_call(
        kern, name=name, grid=(nh // hb, nb),
        in_specs=[col(0), col(1), col(2), col(3), pl.BlockSpec((3, wb), lambda h, j: (0, h)), vec,
                  col(0), pl.BlockSpec((nsc, hb, LANES, LANES), lambda h, j: (nb - 1 - j, h, 0, 0)), col(0)],
        out_specs=[col(0), col(0), col(0), col(0), vec, vec],
        out_shape=[jax.ShapeDtypeStruct((t, d), BF16)] * 4 + [jax.ShapeDtypeStruct((1, d), F32)] * 2,
        scratch_shapes=[pltpu.VMEM((hb, LANES, LANES), F32), pltpu.VMEM((1, wb), F32)],
        compiler_params=_params(("parallel", "arbitrary")),
    )(proj, proj, proj, proj, lb_logits, norm_g, o_raw, states, dy)


_INV_SQRT2 = 0.7071067811865476
_INV_SQRT2PI = 0.3989422804014327


def _gelu(x):
    return 0.5 * x * (1.0 + lax.erf(x * _INV_SQRT2))


def _gelu_grad(x):
    return 0.5 * (1.0 + lax.erf(x * _INV_SQRT2)) + x * jnp.exp(-0.5 * x * x) * _INV_SQRT2PI


def _causal(w):
    r = lax.broadcasted_iota(jnp.int32, (GCHUNK, GCHUNK), 0)
    c = lax.broadcasted_iota(jnp.int32, (GCHUNK, GCHUNK), 1)
    return jnp.where(r >= c, w, 0.0)


def _sg_gate_fwd(pre, ln_g, ln_b, w_s, b_s_t, name):
    t, d2 = pre.shape
    d = d2 // 2
    ng = d // LANES

    def kern(pre_ref, g_ref, b_ref, ws_ref, bs_ref, y_ref):
        z = _gelu(pre_ref[...])
        u = z[:, :d]
        vhat, _ = _ln_hat(z[:, d:])
        vn = (vhat * g_ref[...] + b_ref[...]).astype(BF16)
        bs = bs_ref[...]
        for g in range(ng):
            cols = slice(g * LANES, (g + 1) * LANES)
            wc = _causal(ws_ref[g]).astype(BF16)
            gate = jnp.dot(wc, vn[:, cols], preferred_element_type=F32) + bs[:, g:g + 1]
            y_ref[:, cols] = (u[:, cols] * gate).astype(BF16)

    vec = pl.BlockSpec((1, d), lambda i: (0, 0))
    return pl.pallas_call(
        kern, name=name, grid=(t // GCHUNK,),
        in_specs=[pl.BlockSpec((GCHUNK, d2), lambda i: (i, 0)), vec, vec,
                  pl.BlockSpec((ng, GCHUNK, GCHUNK), lambda i: (0, 0, 0)),
                  pl.BlockSpec((GCHUNK, ng), lambda i: (0, 0))],
        out_specs=pl.BlockSpec((GCHUNK, d), lambda i: (i, 0)),
        out_shape=jax.ShapeDtypeStruct((t, d), BF16),
        compiler_params=_params(("parallel",)),
    )(pre, ln_g, ln_b, w_s, b_s_t)


def _sg_gate_bwd(pre, dy, ln_g, ln_b, w_s, b_s_t, name):
    t, d2 = pre.shape
    d = d2 // 2
    ng = d // LANES

    def kern(pre_ref, dy_ref, g_ref, b_ref, ws_ref, bs_ref, dpre_ref, dws_ref, dbs_ref, dg_ref, db_ref, dvn_ref):
        @pl.when(pl.program_id(0) == 0)
        def _():
            dws_ref[...] = jnp.zeros_like(dws_ref)
            dbs_ref[...] = jnp.zeros_like(dbs_ref)
            dg_ref[...] = jnp.zeros_like(dg_ref)
            db_ref[...] = jnp.zeros_like(db_ref)

        pre = pre_ref[...]
        z = _gelu(pre)
        u = z[:, :d]
        vhat, rstd = _ln_hat(z[:, d:])
        gv = g_ref[...]
        vn = (vhat * gv + b_ref[...]).astype(BF16)
        bs = bs_ref[...]
        dyv = dy_ref[...]
        gp = _gelu_grad(pre)
        lane = lax.broadcasted_iota(jnp.int32, (GCHUNK, ng), 1)
        dbs = jnp.zeros((GCHUNK, ng), F32)
        for g in range(ng):
            cols = slice(g * LANES, (g + 1) * LANES)
            wc = _causal(ws_ref[g]).astype(BF16)
            vng = vn[:, cols]
            gate = jnp.dot(wc, vng, preferred_element_type=F32) + bs[:, g:g + 1]
            dpre_ref[:, cols] = (dyv[:, cols] * gate * gp[:, cols]).astype(BF16)
            dgate = dyv[:, cols] * u[:, cols]
            dbs = dbs + jnp.where(lane == g, jnp.sum(dgate, axis=1, keepdims=True), 0.0)
            dgb = dgate.astype(BF16)
            dws_ref[g] += _causal(lax.dot_general(dgb, vng, NT, preferred_element_type=F32))
            dvn_ref[:, cols] = lax.dot_general(wc, dgb, TN, preferred_element_type=F32)
        dbs_ref[...] += dbs
        dvn = dvn_ref[...]
        dg_ref[...] += jnp.sum(dvn * vhat, axis=0, keepdims=True)
        db_ref[...] += jnp.sum(dvn, axis=0, keepdims=True)
        dvh = dvn * gv
        m1 = jnp.mean(dvh, axis=-1, keepdims=True)
        m2 = jnp.mean(dvh * vhat, axis=-1, keepdims=True)
        dpre_ref[:, d:] = (rstd * (dvh - m1 - vhat * m2) * gp[:, d:]).astype(BF16)

    vec = pl.BlockSpec((1, d), lambda i: (0, 0))
    wsp = pl.BlockSpec((ng, GCHUNK, GCHUNK), lambda i: (0, 0, 0))
    bsp = pl.BlockSpec((GCHUNK, ng), lambda i: (0, 0))
    return pl.pallas_call(
        kern, name=name, grid=(t // GCHUNK,),
        in_specs=[pl.BlockSpec((GCHUNK, d2), lambda i: (i, 0)), pl.BlockSpec((GCHUNK, d), lambda i: (i, 0)),
                  vec, vec, wsp, bsp],
        out_specs=[pl.BlockSpec((GCHUNK, d2), lambda i: (i, 0)), wsp, bsp, vec, vec],
        out_shape=[jax.ShapeDtypeStruct((t, d2), BF16), jax.ShapeDtypeStruct((ng, GCHUNK, GCHUNK), F32),
                   jax.ShapeDtypeStruct((GCHUNK, ng), F32), jax.ShapeDtypeStruct((1, d), F32),
                   jax.ShapeDtypeStruct((1, d), F32)],
        scratch_shapes=[pltpu.VMEM((GCHUNK, d), F32)],
        compiler_params=_params(("arbitrary",)),
    )(pre, dy, ln_g, ln_b, w_s, b_s_t)


def _adamw_math(w, g, m, v):
    m = ADAM_B1 * m + (1.0 - ADAM_B1) * g
    v = ADAM_B2 * v + (1.0 - ADAM_B2) * (g * g)
    m_hat = m / (1.0 - ADAM_B1 ** ADAM_STEP)
    v_hat = v / (1.0 - ADAM_B2 ** ADAM_STEP)
    return -ADAM_LR * (m_hat / (jnp.sqrt(v_hat) + ADAM_EPS) + ADAM_WD * w), m, v


ADAMW_BLOCK_BYTES = 3 << 19


def _adamw(w, gs, m, v, name):
    nl, r, c = w.shape
    rb = _pick(r, tuple(p for p in (512, 256, 128, 64, 32, 16, 8) if p * c * 4 <= ADAMW_BLOCK_BYTES))

    def kern(w_ref, m_ref, v_ref, *rest):
        g_refs, (d_ref, mo_ref, vo_ref, go_ref) = rest[:nl], rest[nl:]
        layer = pl.program_id(0)
        g = g_refs[0][...]
        for k in range(1, nl):
            g = jnp.where(layer == k, g_refs[k][...], g)
        dlt, mm, vv = _adamw_math(w_ref[...], g, m_ref[...], v_ref[...])
        d_ref[...] = dlt
        mo_ref[...] = mm
        vo_ref[...] = vv
        go_ref[...] = g

    blk = pl.BlockSpec((None, rb, c), lambda l, i: (l, i, 0))
    g_specs = [pl.BlockSpec((rb, c), lambda l, i, k=k: (jnp.where(l == k, i, 0), 0)) for k in range(nl)]
    return pl.pallas_call(
        kern, name=name, grid=(nl, r // rb), in_specs=[blk] * 3 + g_specs, out_specs=[blk] * 4,
        out_shape=[jax.ShapeDtypeStruct((nl, r, c), F32)] * 4,
        compiler_params=_params(("parallel", "parallel")),
    )(w, m, v, *gs)


def _lb_logits_grad(lb_logits, dlb, name):
    def kern(l_ref, d_ref, o_ref):
        lg = l_ref[...]
        m = jnp.max(lg, axis=0, keepdims=True)
        e = jnp.exp(lg - m)
        p = e / jnp.sum(e, axis=0, keepdims=True)
        row = lax.broadcasted_iota(jnp.int32, lg.shape, 0)
        o_ref[...] = d_ref[...] * p[0:1, :] * (jnp.where(row == 0, 1.0, 0.0) - p)

    return pl.pallas_call(kern, name=name, out_shape=jax.ShapeDtypeStruct(lb_logits.shape, F32))(lb_logits, dlb)


def _sum_leading(a, name):
    n, r, c = a.shape
    rb = _pick(r, (512, 256, 128, 64, 32, 16, 8))

    def kern(a_ref, o_ref):
        acc = a_ref[0]
        for i in range(1, n):
            acc = acc + a_ref[i]
        o_ref[...] = acc

    return pl.pallas_call(
        kern, name=name, grid=(r // rb,), in_specs=[pl.BlockSpec((n, rb, c), lambda i: (0, i, 0))],
        out_specs=pl.BlockSpec((rb, c), lambda i: (i, 0)), out_shape=jax.ShapeDtypeStruct((r, c), F32),
        compiler_params=_params(("parallel",)),
    )(a)


def _place():
    x, y, c = lax.axis_index("x"), lax.axis_index("y"), lax.axis_index("c")
    return x, y, c


def _chip_rel(x, y, r):
    px = x if r < 2 else 1 - x
    py = y if r % 2 == 0 else 1 - y
    return px, py, 2 * px + py


def _allgather_split(arrs, name):
    n = len(arrs)
    slots = 7

    def body(*refs):
        ins, outs = refs[:n], refs[n:2 * n]
        send_sems, recv_sems = refs[2 * n:]
        x, y, c = _place()
        me = 2 * x + y
        sib = (x, y, 1 - c)

        def half(a, shard, hc):
            h = ins[a].shape[0] // 2
            return outs[a].at[shard, pl.ds(hc * h, h), :]

        def src_half(a):
            h = ins[a].shape[0] // 2
            return ins[a].at[pl.ds(c * h, h), :]

        def copy(a, slot, src, dst, to):
            return pltpu.make_async_remote_copy(src_ref=src, dst_ref=dst, send_sem=send_sems.at[a * slots + slot],
                                                recv_sem=recv_sems.at[a * slots + slot], device_id=to,
                                                device_id_type=MESH)

        first = []
        for r in (1, 2, 3):
            px, py, _ = _chip_rel(x, y, r)
            for a in range(n):
                first.append(copy(a, r - 1, src_half(a), half(a, me, c), (px, py, c)))
        own = [copy(a, 6, ins[a], outs[a].at[me], sib) for a in range(n)]
        for cp in first + own:
            cp.start()
        passed = []
        for r in (1, 2, 3):
            _, _, shard = _chip_rel(x, y, r)
            for a in range(n):
                copy(a, r - 1, src_half(a), half(a, shard, c), sib).wait_recv()
                cp = copy(a, 3 + r - 1, half(a, shard, c), half(a, shard, c), sib)
                cp.start()
                passed.append(cp)
        for r in (1, 2, 3):
            _, _, shard = _chip_rel(x, y, r)
            for a in range(n):
                copy(a, 3 + r - 1, src_half(a), half(a, shard, 1 - c), sib).wait_recv()
        for cp in own:
            cp.wait_recv()
        for cp in first + passed + own:
            cp.wait_send()

    anyspec = pl.BlockSpec(memory_space=pl.ANY)
    return pl.pallas_call(
        body, name=name, in_specs=[anyspec] * n, out_specs=[anyspec] * n,
        out_shape=[jax.ShapeDtypeStruct((N_CHIPS,) + a.shape, a.dtype) for a in arrs],
        scratch_shapes=[pltpu.SemaphoreType.DMA((slots * n,)), pltpu.SemaphoreType.DMA((slots * n,))],
        compiler_params=pltpu.CompilerParams(has_side_effects=True),
    )(*arrs)


def _allgather_whole(arr, name):
    def body(in_ref, out_ref, send_sems, recv_sems, loc_sem):
        x, y, c = _place()
        me = 2 * x + y
        local = pltpu.make_async_copy(in_ref, out_ref.at[me], loc_sem)
        local.start()
        sends = []
        for r in (1, 2, 3):
            px, py, _ = _chip_rel(x, y, r)
            sends.append(pltpu.make_async_remote_copy(
                src_ref=in_ref, dst_ref=out_ref.at[me], send_sem=send_sems.at[r - 1], recv_sem=recv_sems.at[r - 1],
                device_id=(px, py, c), device_id_type=MESH))
        for cp in sends:
            cp.start()
        for r in (1, 2, 3):
            px, py, shard = _chip_rel(x, y, r)
            pltpu.make_async_remote_copy(
                src_ref=in_ref, dst_ref=out_ref.at[shard], send_sem=send_sems.at[r - 1], recv_sem=recv_sems.at[r - 1],
                device_id=(px, py, c), device_id_type=MESH).wait_recv()
        for cp in sends:
            cp.wait_send()
        local.wait()

    anyspec = pl.BlockSpec(memory_space=pl.ANY)
    return pl.pallas_call(
        body, name=name, in_specs=[anyspec], out_specs=anyspec,
        out_shape=jax.ShapeDtypeStruct((N_CHIPS,) + arr.shape, arr.dtype),
        scratch_shapes=[pltpu.SemaphoreType.DMA((3,)), pltpu.SemaphoreType.DMA((3,)), pltpu.SemaphoreType.DMA],
        compiler_params=pltpu.CompilerParams(has_side_effects=True),
    )(arr)


def _swap_halves(grads, name):
    n = len(grads)

    def body(*refs):
        ins, outs = refs[:n], refs[n:2 * n]
        send_sems, recv_sems = refs[2 * n:]
        x, y, c = _place()
        sib = (x, y, 1 - c)
        copies = []
        for a in range(n):
            for j in range(N_CHIPS):
                copies.append(pltpu.make_async_remote_copy(
                    src_ref=ins[a].at[j, 1 - c], dst_ref=outs[a].at[j], send_sem=send_sems.at[a * N_CHIPS + j],
                    recv_sem=recv_sems.at[a * N_CHIPS + j], device_id=sib, device_id_type=MESH))
        for cp in copies:
            cp.start()
        for cp in copies:
            cp.wait()

    anyspec = pl.BlockSpec(memory_space=pl.ANY)
    return pl.pallas_call(
        body, name=name, in_specs=[anyspec] * n, out_specs=[anyspec] * n,
        out_shape=[jax.ShapeDtypeStruct((N_CHIPS,) + g.shape[2:], g.dtype) for g in grads],
        scratch_shapes=[pltpu.SemaphoreType.DMA((N_CHIPS * n,)), pltpu.SemaphoreType.DMA((N_CHIPS * n,))],
        compiler_params=pltpu.CompilerParams(has_side_effects=True),
    )(*grads)


def _scatter_chips(parts, name):
    n = len(parts)

    def body(*refs):
        ins, outs = refs[:n], refs[n:2 * n]
        send_sems, recv_sems = refs[2 * n:]
        x, y, c = _place()
        copies = []
        for r in (1, 2, 3):
            px, py, shard = _chip_rel(x, y, r)
            for a in range(n):
                copies.append(pltpu.make_async_remote_copy(
                    src_ref=ins[a].at[shard], dst_ref=outs[a].at[r - 1], send_sem=send_sems.at[a * 3 + r - 1],
                    recv_sem=recv_sems.at[a * 3 + r - 1], device_id=(px, py, c), device_id_type=MESH))
        for cp in copies:
            cp.start()
        for cp in copies:
            cp.wait()

    anyspec = pl.BlockSpec(memory_space=pl.ANY)
    return pl.pallas_call(
        body, name=name, in_specs=[anyspec] * n, out_specs=[anyspec] * n,
        out_shape=[jax.ShapeDtypeStruct((3,) + p.shape[1:], p.dtype) for p in parts],
        scratch_shapes=[pltpu.SemaphoreType.DMA((3 * n,)), pltpu.SemaphoreType.DMA((3 * n,))],
        compiler_params=pltpu.CompilerParams(has_side_effects=True),
    )(*parts)


def _join_halves(bufs, name):
    n = len(bufs)

    def body(*refs):
        outs = refs[n:2 * n]
        send_sems, recv_sems = refs[2 * n:]
        x, y, c = _place()
        copies = [pltpu.make_async_remote_copy(
            src_ref=outs[a].at[c], dst_ref=outs[a].at[c], send_sem=send_sems.at[a], recv_sem=recv_sems.at[a],
            device_id=(x, y, 1 - c), device_id_type=MESH) for a in range(n)]
        for cp in copies:
            cp.start()
        for cp in copies:
            cp.wait()

    anyspec = pl.BlockSpec(memory_space=pl.ANY)
    return pl.pallas_call(
        body, name=name, in_specs=[anyspec] * n, out_specs=[anyspec] * n,
        out_shape=[jax.ShapeDtypeStruct(b.shape, b.dtype) for b in bufs],
        input_output_aliases={a: a for a in range(n)},
        scratch_shapes=[pltpu.SemaphoreType.DMA((n,)), pltpu.SemaphoreType.DMA((n,))],
        compiler_params=pltpu.CompilerParams(has_side_effects=True),
    )(*bufs)


def _gather_all_devices(buf, name):
    def body(in_ref, out_ref, send_sems, recv_sems, loc_sem):
        x, y, c = _place()
        me = 4 * x + 2 * y + c
        local = pltpu.make_async_copy(in_ref, out_ref.at[me], loc_sem)
        local.start()
        sends = []
        for r in range(1, N_DEV):
            px, py, _ = _chip_rel(x, y, r // 2)
            pc = c if r % 2 == 0 else 1 - c
            sends.append(pltpu.make_async_remote_copy(
                src_ref=in_ref, dst_ref=out_ref.at[me], send_sem=send_sems.at[r - 1], recv_sem=recv_sems.at[r - 1],
                device_id=(px, py, pc), device_id_type=MESH))
        for cp in sends:
            cp.start()
        for r in range(1, N_DEV):
            px, py, _ = _chip_rel(x, y, r // 2)
            pc = c if r % 2 == 0 else 1 - c
            pltpu.make_async_remote_copy(
                src_ref=in_ref, dst_ref=out_ref.at[4 * px + 2 * py + pc], send_sem=send_sems.at[r - 1],
                recv_sem=recv_sems.at[r - 1], device_id=(px, py, pc), device_id_type=MESH).wait_recv()
        for cp in sends:
            cp.wait_send()
        local.wait()

    anyspec = pl.BlockSpec(memory_space=pl.ANY)
    return pl.pallas_call(
        body, name=name, in_specs=[anyspec], out_specs=anyspec,
        out_shape=jax.ShapeDtypeStruct((N_DEV,) + buf.shape, buf.dtype),
        scratch_shapes=[pltpu.SemaphoreType.DMA((N_DEV - 1,)), pltpu.SemaphoreType.DMA((N_DEV - 1,)),
                        pltpu.SemaphoreType.DMA],
        compiler_params=pltpu.CompilerParams(has_side_effects=True),
    )(buf)


def _add_half(grad, recv, sel, name):
    _, _, rh, cw = grad.shape
    rb = _pick(rh, (512, 256, 176, 128, 64, 32, 16, 8))

    def kern(sel_ref, g_ref, r_ref, o_ref):
        o_ref[...] = (g_ref[...] + r_ref[...]).astype(BF16)

    return pl.pallas_call(
        kern, name=name,
        grid_spec=pltpu.PrefetchScalarGridSpec(
            num_scalar_prefetch=1, grid=(N_CHIPS, rh // rb),
            in_specs=[pl.BlockSpec((None, None, rb, cw), lambda j, i, s: (j, s[0], i, 0)),
                      pl.BlockSpec((None, rb, cw), lambda j, i, s: (j, i, 0))],
            out_specs=pl.BlockSpec((None, rb, cw), lambda j, i, s: (j, i, 0))),
        out_shape=jax.ShapeDtypeStruct((N_CHIPS, rh, cw), BF16),
        compiler_params=_params(("parallel", "parallel")),
    )(sel, grad, recv)


def _add_own(grad, recv, got, sel, name):
    _, _, rh, cw = grad.shape
    rb = _pick(rh, (512, 256, 176, 128, 64, 32, 16, 8))

    def kern(sel_ref, g_ref, r_ref, b_ref, o_ref):
        own = g_ref[...] + r_ref[...]
        o_ref[...] = ((own + b_ref[0].astype(F32)) + b_ref[1].astype(F32)) + b_ref[2].astype(F32)

    return pl.pallas_call(
        kern, name=name,
        grid_spec=pltpu.PrefetchScalarGridSpec(
            num_scalar_prefetch=1, grid=(rh // rb,),
            in_specs=[pl.BlockSpec((None, None, rb, cw), lambda i, s: (s[1], s[0], i, 0)),
                      pl.BlockSpec((None, rb, cw), lambda i, s: (s[1], i, 0)),
                      pl.BlockSpec((3, rb, cw), lambda i, s: (0, i, 0))],
            out_specs=pl.BlockSpec((None, rb, cw), lambda i, s: (s[0], i, 0))),
        out_shape=jax.ShapeDtypeStruct((2, rh, cw), F32),
        compiler_params=_params(("parallel",)),
    )(sel, grad, recv, got)


def _reduce_scatter(grads, tag):
    x, y, c = _place()
    sel = jnp.stack([c, 2 * x + y]).astype(jnp.int32)
    split = [g.reshape(N_CHIPS, 2, g.shape[1] // 2, g.shape[2]) for g in grads]
    recv = _swap_halves(split, f"{tag}_swap")
    parts = [_add_half(g, r, sel, f"{tag}_addhalf{i}") for i, (g, r) in enumerate(zip(split, recv))]
    got = _scatter_chips(parts, f"{tag}_scatter")
    mine = [_add_own(g, r, b, sel, f"{tag}_addown{i}") for i, (g, r, b) in enumerate(zip(split, recv, got))]
    full = _join_halves(mine, f"{tag}_join")
    return [f.reshape(f.shape[0] * f.shape[1], f.shape[2]) for f in full]


def _ffn_fwd(h, hb, w_up, w_down, conv_w, conv_b, ln_g, ln_b, tag):
    u = _matmul(hb, w_up, mode="nn", nsh=N_CHIPS, name=f"{tag}_up")
    gact = _conv_gate_fwd(u, conv_w, conv_b, f"{tag}_gate")
    ffn = _matmul(gact, w_down, mode="nn", nsh=1, name=f"{tag}_down")
    xin, h2, h2b = _res_ln_fwd(h, ffn, ln_g, ln_b, f"{tag}_ln")
    return (u, gact, xin), h2, h2b


def _ffn_bwd(saved, hb_in, dxin, dxin_b, w_up, w_down, conv_w, conv_b, tag):
    u, gact, _ = saved
    dgact = _matmul(dxin_b, w_down, mode="nt", nsh=1, name=f"{tag}_ddown")
    dw_down = _matmul(gact, dxin_b, mode="tn", nsh=1, out_shards=1, name=f"{tag}_wdown")
    da, db, dcw, dcb = _conv_gate_bwd(u, dgact, conv_w, conv_b, f"{tag}_dgate")
    dh = _matmul(da, w_up, mode="nt", nsh=2, b_off=0, resid=dxin, alpha=ALPHA, name=f"{tag}_dup_a")
    dh = _matmul(db, w_up, mode="nt", nsh=2, b_off=2, resid=dh, alpha=1.0, name=f"{tag}_dup_b")
    dw_up = _matmul(hb_in, da, mode="tn", nsh=2, out_off=0, out_shards=N_CHIPS, name=f"{tag}_wup_a")
    dw_up = _matmul(hb_in, db, mode="tn", nsh=2, out_off=2, out_shards=N_CHIPS, out_init=dw_up, name=f"{tag}_wup_b")
    return dh, dw_up, dw_down, dcw, dcb


def _local_step(x2, tgt, xb, wg, sm):
    proj = _matmul(xb, wg["hg_in"], mode="nn", nsh=N_CHIPS, name="hg_in")
    yhg, o_raw, states = _hgrn2_fwd(proj, sm["lb_logits"], sm["hg_norm_g"], "hgrn2_fwd")
    mixed = _matmul(yhg, wg["hg_out"], mode="nn", nsh=1, name="hg_out")
    xin1, h1, h1b = _res_ln_fwd(x2, mixed, sm["ln1_g"][0:1], sm["ln1_b"][0:1], "l0_ln1")
    sv_f0, h2, h2b = _ffn_fwd(h1, h1b, wg["up0"], wg["dn0"], sm["conv_w"][0], sm["conv_b"][0:1],
                              sm["ln2_g"][0:1], sm["ln2_b"][0:1], "l0_ffn")
    pre = _matmul(h2b, wg["sg_in"], mode="nn", nsh=N_CHIPS, name="sg_in")
    ysg = _sg_gate_fwd(pre, sm["sg_ln_g"], sm["sg_ln_b"], sm["sg_w_s"], sm["sg_b_s_t"], "sg_gate")
    mixed = _matmul(ysg, wg["sg_out"], mode="nn", nsh=1, name="sg_out")
    xin3, h3, h3b = _res_ln_fwd(h2, mixed, sm["ln1_g"][1:2], sm["ln1_b"][1:2], "l1_ln1")
    sv_f1, _, _ = _ffn_fwd(h3, h3b, wg["up1"], wg["dn1"], sm["conv_w"][1], sm["conv_b"][1:2],
                           sm["ln2_g"][1:2], sm["ln2_b"][1:2], "l1_ffn")
    gw, gs = {}, {}
    dx, dxb, dg4, db4, loss = _ln_bwd(sv_f1[2], tgt, sm["ln2_g"][1:2], sm["ln2_b"][1:2], "l1_ln2_bwd", loss_head=True)
    dh3, gw["up1"], gw["dn1"], dcw1, dcb1 = _ffn_bwd(sv_f1, h3b, dx, dxb, wg["up1"], wg["dn1"],
                                                     sm["conv_w"][1], sm["conv_b"][1:2], "l1_ffn")
    dx, dxb, dg3, db3 = _ln_bwd(xin3, dh3, sm["ln1_g"][1:2], sm["ln1_b"][1:2], "l1_ln1_bwd")
    dysg = _matmul(dxb, wg["sg_out"], mode="nt", nsh=1, name="sg_dout")
    gw["sg_out"] = _matmul(ysg, dxb, mode="tn", nsh=1, out_shards=1, name="sg_wout")
    dpre, gs["sg_w_s"], dbs_t, gs["sg_ln_g"], gs["sg_ln_b"] = _sg_gate_bwd(
        pre, dysg, sm["sg_ln_g"], sm["sg_ln_b"], sm["sg_w_s"], sm["sg_b_s_t"], "sg_gate_bwd")
    gs["sg_b_s_t"] = dbs_t
    dh2 = _matmul(dpre, wg["sg_in"], mode="nt", nsh=N_CHIPS, resid=dx, alpha=ALPHA, name="sg_din")
    gw["sg_in"] = _matmul(h2b, dpre, mode="tn", nsh=N_CHIPS, out_shards=N_CHIPS, name="sg_win")
    dx, dxb, dg2, db2 = _ln_bwd(sv_f0[2], dh2, sm["ln2_g"][0:1], sm["ln2_b"][0:1], "l0_ln2_bwd")
    dh1, gw["up0"], gw["dn0"], dcw0, dcb0 = _ffn_bwd(sv_f0, h1b, dx, dxb, wg["up0"], wg["dn0"],
                                                     sm["conv_w"][0], sm["conv_b"][0:1], "l0_ffn")
    dx, dxb, dg1, db1 = _ln_bwd(xin1, dh1, sm["ln1_g"][0:1], sm["ln1_b"][0:1], "l0_ln1_bwd")
    dyhg = _matmul(dxb, wg["hg_out"], mode="nt", nsh=1, name="hg_dout")
    gw["hg_out"] = _matmul(yhg, dxb, mode="tn", nsh=1, out_shards=1, name="hg_wout")
    dparts = _hgrn2_bwd(proj, sm["lb_logits"], sm["hg_norm_g"], o_raw, states, dyhg, "hgrn2_bwd")
    gs["lb"], gs["hg_norm_g"] = dparts[4], dparts[5]
    gx = dx
    al = ALPHA
    gwin = None
    for j in range(4):
        gx = _matmul(dparts[j], wg["hg_in"], mode="nt", nsh=1, b_off=j, resid=gx, alpha=al, name=f"hg_din{j}")
        al = 1.0
        gwin = _matmul(xb, dparts[j], mode="tn", nsh=1, out_off=j, out_shards=N_CHIPS, out_init=gwin,
                       name=f"hg_win{j}")
    gw["hg_in"] = gwin
    gs["ln1_g"] = jnp.concatenate([dg1, dg3], axis=0)
    gs["ln1_b"] = jnp.concatenate([db1, db3], axis=0)
    gs["ln2_g"] = jnp.concatenate([dg2, dg4], axis=0)
    gs["ln2_b"] = jnp.concatenate([db2, db4], axis=0)
    gs["conv_w"] = jnp.stack([dcw0, dcw1], axis=0)
    gs["conv_b"] = jnp.concatenate([dcb0, dcb1], axis=0)
    return loss, gx, gw, gs


_BIG = ("hg_in", "hg_out", "sg_in", "sg_out", "up0", "up1", "dn0", "dn1")
_SMALL_ORDER = ("lb", "hg_norm_g", "sg_w_s", "sg_b_s_t", "conv_b", "ln1_g", "ln1_b", "ln2_g", "ln2_b",
                "conv_w", "sg_ln_g", "sg_ln_b")


PACK_ROWS = 512


def _pack(parts):
    flat, layout, off = [], [], 0
    for k in _SMALL_ORDER:
        a = parts[k]
        n = a.size
        pad = (-n) % LANES
        flat.append(jnp.pad(a.reshape(-1), (0, pad)))
        layout.append((k, off, n, a.shape))
        off += n + pad
    flat.append(jnp.zeros(((-off) % (PACK_ROWS * LANES),), F32))
    return jnp.concatenate(flat).reshape(-1, LANES), layout


def _unpack(buf, layout):
    flat = buf.reshape(-1)
    return {k: flat[off:off + n].reshape(shape) for k, off, n, shape in layout}


def kernel(x, lb_logits, hg_w_in, hg_norm_g, hg_w_out, sg_w_in, sg_ln_g, sg_ln_b, sg_w_s, sg_b_s, sg_w_out, ffn_w_up, ffn_conv_w, ffn_conv_b, ffn_w_down, ln1_g, ln1_b, ln2_g, ln2_b, loss_target, m_lb_logits, m_hg_w_in, m_hg_norm_g, m_hg_w_out, m_sg_w_in, m_sg_ln_g, m_sg_ln_b, m_sg_w_s, m_sg_b_s, m_sg_w_out, m_ffn_w_up, m_ffn_conv_w, m_ffn_conv_b, m_ffn_w_down, m_ln1_g, m_ln1_b, m_ln2_g, m_ln2_b, v_lb_logits, v_hg_w_in, v_hg_norm_g, v_hg_w_out, v_sg_w_in, v_sg_ln_g, v_sg_ln_b, v_sg_w_s, v_sg_b_s, v_sg_w_out, v_ffn_w_up, v_ffn_conv_w, v_ffn_conv_b, v_ffn_w_down, v_ln1_g, v_ln1_b, v_ln2_g, v_ln2_b):
    names = ("lb_logits", "hg_w_in", "hg_norm_g", "hg_w_out", "sg_w_in", "sg_ln_g", "sg_ln_b", "sg_w_s", "sg_b_s",
             "sg_w_out", "ffn_w_up", "ffn_conv_w", "ffn_conv_b", "ffn_w_down", "ln1_g", "ln1_b", "ln2_g", "ln2_b")
    w = dict(zip(names, (lb_logits, hg_w_in, hg_norm_g, hg_w_out, sg_w_in, sg_ln_g, sg_ln_b, sg_w_s, sg_b_s,
                         sg_w_out, ffn_w_up, ffn_conv_w, ffn_conv_b, ffn_w_down, ln1_g, ln1_b, ln2_g, ln2_b)))
    mom = dict(zip(names, (m_lb_logits, m_hg_w_in, m_hg_norm_g, m_hg_w_out, m_sg_w_in, m_sg_ln_g, m_sg_ln_b, m_sg_w_s,
                           m_sg_b_s, m_sg_w_out, m_ffn_w_up, m_ffn_conv_w, m_ffn_conv_b, m_ffn_w_down, m_ln1_g,
                           m_ln1_b, m_ln2_g, m_ln2_b)))
    var = dict(zip(names, (v_lb_logits, v_hg_w_in, v_hg_norm_g, v_hg_w_out, v_sg_w_in, v_sg_ln_g, v_sg_ln_b, v_sg_w_s,
                           v_sg_b_s, v_sg_w_out, v_ffn_w_up, v_ffn_conv_w, v_ffn_conv_b, v_ffn_w_down, v_ln1_g,
                           v_ln1_b, v_ln2_g, v_ln2_b)))
    x2, tgt = x[0], loss_target[0]
    d = x2.shape[1]
    fq = ffn_conv_w.shape[2]
    dq = sg_ln_g.shape[1]
    cx, cy, _ = _place()
    me = 2 * cx + cy

    shards = {"hg_in": hg_w_in[0], "hg_out": hg_w_out[0], "sg_in": sg_w_in[0], "sg_out": sg_w_out[0],
              "up0": ffn_w_up[0], "up1": ffn_w_up[1], "dn0": ffn_w_down[0], "dn1": ffn_w_down[1]}
    gathered = _allgather_split([shards[k].astype(BF16) for k in _BIG], "gather_weights")
    wg = dict(zip(_BIG, gathered))
    for k in ("hg_out", "sg_out", "dn0", "dn1"):
        g = wg[k]
        wg[k] = g.reshape(1, g.shape[0] * g.shape[1], g.shape[2])
    wide = max(fq, dq)
    tiny = jnp.concatenate([jnp.pad(ffn_conv_w.reshape(6, fq), ((0, 0), (0, wide - fq))),
                            jnp.pad(sg_ln_g, ((0, 0), (0, wide - dq))),
                            jnp.pad(sg_ln_b, ((0, 0), (0, wide - dq)))], axis=0)
    tiny_all = _allgather_whole(tiny, "gather_small")
    conv_w_full = jnp.transpose(tiny_all[:, 0:6, :fq].reshape(N_CHIPS, 2, 3, fq), (1, 2, 0, 3)).reshape(2, 3, N_CHIPS * fq)
    sm = {"lb_logits": lb_logits, "hg_norm_g": hg_norm_g, "ln1_g": ln1_g, "ln1_b": ln1_b, "ln2_g": ln2_g,
          "ln2_b": ln2_b, "conv_w": conv_w_full, "conv_b": ffn_conv_b,
          "sg_ln_g": tiny_all[:, 6, :dq].reshape(1, N_CHIPS * dq),
          "sg_ln_b": tiny_all[:, 7, :dq].reshape(1, N_CHIPS * dq),
          "sg_w_s": sg_w_s[0], "sg_b_s_t": jnp.transpose(sg_b_s[0])}

    loss_row, grad_x, gw, gs = _local_step(x2, tgt, x2.astype(BF16), wg, sm)
    loss = lax.psum(loss_row[0, 0], ("x", "y", "c"))

    big_full = []
    for k in _BIG:
        g = gw[k]
        if k in ("hg_out", "sg_out", "dn0", "dn1"):
            g = g.reshape(N_CHIPS, g.shape[1] // N_CHIPS, g.shape[2])
        big_full.append(g)
    red = dict(zip(_BIG, _reduce_scatter(big_full, "rs")))
    packed, layout = _pack(gs)
    summed = _unpack(_sum_leading(_gather_all_devices(packed, "gather_small_grads"), "sum_small_grads"), layout)

    grads = {
        "lb_logits": _lb_logits_grad(lb_logits, summed["lb"], "lb_logits_grad"),
        "hg_norm_g": summed["hg_norm_g"],
        "sg_ln_g": lax.dynamic_slice_in_dim(summed["sg_ln_g"], me * dq, dq, axis=1),
        "sg_ln_b": lax.dynamic_slice_in_dim(summed["sg_ln_b"], me * dq, dq, axis=1),
        "sg_w_s": summed["sg_w_s"][None], "sg_b_s": jnp.transpose(summed["sg_b_s_t"])[None],
        "ffn_conv_w": lax.dynamic_slice_in_dim(summed["conv_w"], me * fq, fq, axis=2),
        "ffn_conv_b": summed["conv_b"],
        "ln1_g": summed["ln1_g"], "ln1_b": summed["ln1_b"], "ln2_g": summed["ln2_g"], "ln2_b": summed["ln2_b"],
    }

    big_parts = {"hg_w_in": ("hg_in",), "hg_w_out": ("hg_out",), "sg_w_in": ("sg_in",), "sg_w_out": ("sg_out",),
                 "ffn_w_up": ("up0", "up1"), "ffn_w_down": ("dn0", "dn1")}
    delta, new_m, new_v = {}, {}, {}
    for k, parts in big_parts.items():
        delta[k], new_m[k], new_v[k], grads[k] = _adamw(w[k], [red[p] for p in parts], mom[k], var[k], f"adamw_{k}")
    small_names = [k for k in names if k not in big_parts]

    def pack_small(src):
        flat = [src[k].reshape(-1) for k in small_names]
        n = sum(a.size for a in flat)
        flat.append(jnp.zeros(((-n) % (PACK_ROWS * LANES),), F32))
        return jnp.concatenate(flat).reshape(1, -1, LANES)

    outs = _adamw(pack_small(w), [pack_small(grads)[0]], pack_small(mom), pack_small(var), "adamw_small")
    off = 0
    for k in small_names:
        n = w[k].size
        for dst, o in zip((delta, new_m, new_v), outs):
            dst[k] = o.reshape(-1)[off:off + n].reshape(w[k].shape)
        off += n

    return (loss, grad_x[None], *[grads[k] for k in names], *[delta[k] for k in names],
            *[new_m[k] for k in names], *[new_v[k] for k in names])
```

```python
import functools

import jax
import jax.numpy as jnp
from jax import lax
from jax.experimental import pallas as pl
from jax.experimental.pallas import tpu as pltpu

F32 = jnp.float32
BF16 = jnp.bfloat16
HI = lax.Precision.HIGHEST
MESH = pl.DeviceIdType.MESH

ALPHA = (2 * 2) ** 0.25
LN_EPS = 1e-5
RMS_EPS = 1e-6
ADAM_LR, ADAM_B1, ADAM_B2, ADAM_EPS, ADAM_WD, ADAM_STEP = 0.001, 0.9, 0.999, 1e-08, 0.01, 10

LANES = 128
SUB = 16
GCHUNK = 128
VMEM_LIMIT = 56 * 1024 * 1024
N_CHIPS = 4
N_DEV = 8

NT = (((1,), (1,)), ((), ()))
TN = (((0,), (0,)), ((), ()))
NN = (((1,), (0,)), ((), ()))


def _pick(dim, prefs):
    for p in prefs:
        if dim % p == 0:
            return p
    return dim


def _params(sem=None, **kw):
    return pltpu.CompilerParams(dimension_semantics=sem, vmem_limit_bytes=VMEM_LIMIT, **kw)


def _sigmoid_pair(x):
    e = jnp.exp(-jnp.abs(x))
    inv = 1.0 / (1.0 + e)
    pos = x >= 0
    return jnp.where(pos, inv, e * inv), jnp.where(pos, e * inv, inv)


def _ln_hat(x):
    mu = jnp.mean(x, axis=-1, keepdims=True)
    xc = x - mu
    var = jnp.mean(xc * xc, axis=-1, keepdims=True)
    rstd = lax.rsqrt(var + LN_EPS)
    return xc * rstd, rstd


def _lower_bound(logits):
    m = jnp.max(logits, axis=0, keepdims=True)
    e = jnp.exp(logits - m)
    return e[0:1, :] / jnp.sum(e, axis=0, keepdims=True)


MATMUL_VMEM_BUDGET = 40 * 1024 * 1024


def _fit_bk(kdim, bm, bn, out_dtype, has_resid):
    fixed = bm * bn * (4 + 2 * jnp.dtype(out_dtype).itemsize + (8 if has_resid else 0))
    best = LANES
    for bk in range(LANES, kdim + 1, LANES):
        if kdim % bk == 0 and fixed + 4 * bk * (bm + bn) <= MATMUL_VMEM_BUDGET:
            best = bk
    return best


def _matmul(a, b, *, mode, name, out_dtype=F32, resid=None, alpha=1.0, b_off=0, nsh=None,
            out_init=None, out_off=0, out_shards=None):
    if mode == "nn":
        m, kdim = a.shape
        _, _, ns = b.shape
        bm = _pick(m, (1024, 512, 256, 128))
        bn = _pick(ns, (1024, 1408, 512, 256, 128))
        bk = _fit_bk(kdim, bm, bn, out_dtype, resid is not None)
        nps = ns // bn
        grid = (m // bm, nsh * nps, kdim // bk)
        a_spec = pl.BlockSpec((bm, bk), lambda i, j, k: (i, k))
        b_spec = pl.BlockSpec((None, bk, bn), lambda i, j, k: (b_off + j // nps, k, j % nps))
        o_spec = pl.BlockSpec((bm, bn), lambda i, j, k: (i, j))
        out_shape = jax.ShapeDtypeStruct((m, nsh * ns), out_dtype)
        dims = NN
    elif mode == "nt":
        m = a.shape[0]
        _, kdim, ns = b.shape
        bm = _pick(m, (1024, 512, 256, 128))
        bn = _pick(kdim, (1024, 1408, 512, 256, 128))
        bk = _fit_bk(ns, bm, bn, out_dtype, resid is not None)
        kps = ns // bk
        grid = (m // bm, kdim // bn, nsh * kps)
        a_spec = pl.BlockSpec((bm, bk), lambda i, j, k: (i, k))
        b_spec = pl.BlockSpec((None, bn, bk), lambda i, j, k: (b_off + k // kps, j, k % kps))
        o_spec = pl.BlockSpec((bm, bn), lambda i, j, k: (i, j))
        out_shape = jax.ShapeDtypeStruct((m, kdim), out_dtype)
        dims = NT
    else:
        t, kdim = a.shape
        ns = b.shape[1] // nsh
        bm = _pick(kdim, (1024, 1408, 512, 256, 128))
        bn = _pick(ns, (1024, 1408, 512, 256, 128))
        bk = _fit_bk(t, bm, bn, out_dtype, resid is not None)
        nps = ns // bn
        grid = (kdim // bm, nsh * nps, t // bk)
        a_spec = pl.BlockSpec((bk, bm), lambda i, j, k: (k, i))
        b_spec = pl.BlockSpec((bk, bn), lambda i, j, k: (k, j))
        o_spec = pl.BlockSpec((None, bm, bn), lambda i, j, k: (out_off + j // nps, i, j % nps))
        out_shape = jax.ShapeDtypeStruct((out_shards, kdim, ns), out_dtype)
        dims = TN
    nk = grid[2]
    has_resid = resid is not None
    has_init = out_init is not None

    def kern(*refs):
        a_ref, b_ref = refs[0], refs[1]
        r_ref = refs[2] if has_resid else None

        def finish(r, o_ref):
            if has_resid:
                r = r + alpha * r_ref[...]
            o_ref[...] = r.astype(o_ref.dtype)

        part = lax.dot_general(a_ref[...], b_ref[...], dims, preferred_element_type=F32)
        if nk == 1:
            finish(part, refs[-1])
            return
        o_ref, acc_ref = refs[-2], refs[-1]
        k = pl.program_id(2)

        @pl.when(k == 0)
        def _():
            acc_ref[...] = part

        @pl.when(k > 0)
        def _():
            acc_ref[...] += part

        @pl.when(k == nk - 1)
        def _():
            finish(acc_ref[...], o_ref)

    in_specs = [a_spec, b_spec]
    operands = [a, b]
    if has_resid:
        in_specs.append(pl.BlockSpec((bm, bn), lambda i, j, k: (i, j)))
        operands.append(resid)
    aliases = {}
    if has_init:
        in_specs.append(pl.BlockSpec(memory_space=pl.ANY))
        operands.append(out_init)
        aliases = {len(operands) - 1: 0}
    return pl.pallas_call(
        kern, name=name, grid=grid, in_specs=in_specs, out_specs=o_spec, out_shape=out_shape,
        scratch_shapes=[pltpu.VMEM((bm, bn), F32)] if nk > 1 else [], input_output_aliases=aliases,
        compiler_params=_params(("parallel", "parallel", "arbitrary")),
    )(*operands)


def _res_ln_fwd(h_prev, sub, g, b, name):
    t, d = h_prev.shape
    tb = _pick(t, (256, 128, 64, 32, 16))

    def kern(hp_ref, s_ref, g_ref, b_ref, xin_ref, h_ref, hb_ref):
        xin = ALPHA * hp_ref[...] + s_ref[...]
        xhat, _ = _ln_hat(xin)
        h = xhat * g_ref[...] + b_ref[...]
        xin_ref[...] = xin
        h_ref[...] = h
        hb_ref[...] = h.astype(BF16)

    row = pl.BlockSpec((tb, d), lambda i: (i, 0))
    vec = pl.BlockSpec((1, d), lambda i: (0, 0))
    return pl.pallas_call(
        kern, name=name, grid=(t // tb,), in_specs=[row, row, vec, vec], out_specs=[row, row, row],
        out_shape=[jax.ShapeDtypeStruct((t, d), F32), jax.ShapeDtypeStruct((t, d), F32),
                   jax.ShapeDtypeStruct((t, d), BF16)],
        compiler_params=_params(("parallel",)),
    )(h_prev, sub, g, b)


def _ln_bwd(xin, dy_or_target, g, b, name, loss_head=False):
    t, d = xin.shape
    tb = _pick(t, (256, 128, 64, 32, 16))
    nb = t // tb

    def kern(x_ref, dy_ref, g_ref, b_ref, dx_ref, dxb_ref, dg_ref, db_ref, *rest):
        i = pl.program_id(0)
        xhat, rstd = _ln_hat(x_ref[...])
        gv = g_ref[...]
        if loss_head:
            loss_ref = rest[0]
            err = xhat * gv + b_ref[...] - dy_ref[...]
            dy = err * (1.0 / d)
            part = 0.5 * jnp.sum(jnp.sum(err * err, axis=1, keepdims=True), axis=0, keepdims=True) * (1.0 / d)
        else:
            dy = dy_ref[...]

        @pl.when(i == 0)
        def _():
            dg_ref[...] = jnp.zeros_like(dg_ref)
            db_ref[...] = jnp.zeros_like(db_ref)
            if loss_head:
                loss_ref[...] = jnp.zeros_like(loss_ref)

        dg_ref[...] += jnp.sum(dy * xhat, axis=0, keepdims=True)
        db_ref[...] += jnp.sum(dy, axis=0, keepdims=True)
        if loss_head:
            loss_ref[...] += jnp.broadcast_to(part, loss_ref.shape)
        dxh = dy * gv
        m1 = jnp.mean(dxh, axis=-1, keepdims=True)
        m2 = jnp.mean(dxh * xhat, axis=-1, keepdims=True)
        dx = rstd * (dxh - m1 - xhat * m2)
        dx_ref[...] = dx
        dxb_ref[...] = dx.astype(BF16)

    row = pl.BlockSpec((tb, d), lambda i: (i, 0))
    vec = pl.BlockSpec((1, d), lambda i: (0, 0))
    out_specs = [row, row, vec, vec]
    out_shape = [jax.ShapeDtypeStruct((t, d), F32), jax.ShapeDtypeStruct((t, d), BF16),
                 jax.ShapeDtypeStruct((1, d), F32), jax.ShapeDtypeStruct((1, d), F32)]
    if loss_head:
        out_specs.append(pl.BlockSpec((1, LANES), lambda i: (0, 0)))
        out_shape.append(jax.ShapeDtypeStruct((1, LANES), F32))
    return pl.pallas_call(
        kern, name=name, grid=(nb,), in_specs=[row, row, vec, vec], out_specs=out_specs, out_shape=out_shape,
        compiler_params=_params(("arbitrary",)),
    )(xin, dy_or_target, g, b)


def _conv_gate_fwd(u, conv_w, conv_b, name):
    t, f2 = u.shape
    f = f2 // 2
    tb = _pick(t, (512, 256, 128, 64, 32, 16))
    cn = _pick(f, (1408, 1024, 512, 256, 128))
    ncb = f // cn
    hb = tb // 8

    def kern(a_ref, ah_ref, b_ref, w_ref, cb_ref, o_ref):
        i = pl.program_id(0)
        a = a_ref[...]
        halo = jnp.where(i > 0, ah_ref[...], 0.0)
        rid = lax.broadcasted_iota(jnp.int32, a.shape, 0)
        s1 = jnp.where(rid == 0, halo[7:8, :], pltpu.roll(a, 1, 0))
        s2 = jnp.where(rid == 0, halo[6:7, :], jnp.where(rid == 1, halo[7:8, :], pltpu.roll(a, 2, 0)))
        w = w_ref[...]
        conv = w[2:3, :] * a + w[1:2, :] * s1 + w[0:1, :] * s2 + cb_ref[...]
        sp, _ = _sigmoid_pair(conv)
        o_ref[...] = (conv * sp * b_ref[...]).astype(BF16)

    return pl.pallas_call(
        kern, name=name, grid=(t // tb, ncb),
        in_specs=[pl.BlockSpec((tb, cn), lambda i, j: (i, j)),
                  pl.BlockSpec((8, cn), lambda i, j: (jnp.maximum(i * hb - 1, 0), j)),
                  pl.BlockSpec((tb, cn), lambda i, j: (i, j + ncb)),
                  pl.BlockSpec((3, cn), lambda i, j: (0, j)),
                  pl.BlockSpec((1, cn), lambda i, j: (0, j))],
        out_specs=pl.BlockSpec((tb, cn), lambda i, j: (i, j)),
        out_shape=jax.ShapeDtypeStruct((t, f), BF16),
        compiler_params=_params(("parallel", "parallel")),
    )(u, u, u, conv_w, conv_b)


def _conv_gate_bwd(u, dgact, conv_w, conv_b, name):
    t, f2 = u.shape
    f = f2 // 2
    tb = _pick(t, (512, 256, 128, 64, 32, 16))
    cn = _pick(f, (1408, 1024, 512, 256, 128))
    ncb = f // cn
    hb = tb // 8
    nb = t // tb
    last8 = t // 8 - 1

    def kern(a_ref, ap_ref, an_ref, b_ref, bn_ref, dg_ref, dgn_ref, w_ref, cb_ref,
             da_ref, db_ref, dw_ref, dcb_ref):
        i = pl.program_id(1)
        a = a_ref[...]
        w = w_ref[...]
        ext = jnp.concatenate([jnp.where(i > 0, ap_ref[...], 0.0), a, an_ref[...]], axis=0)
        e1 = pltpu.roll(ext, 1, 0)
        e2 = pltpu.roll(ext, 2, 0)
        conv = (w[2:3, :] * ext + w[1:2, :] * e1 + w[0:1, :] * e2 + cb_ref[...])[8:, :]
        bmn = jnp.concatenate([b_ref[...], bn_ref[...]], axis=0)
        dgmn = jnp.concatenate([dg_ref[...], jnp.where(i < nb - 1, dgn_ref[...], 0.0)], axis=0)
        sp, sn = _sigmoid_pair(conv)
        da = dgmn * bmn * (sp * (1.0 + conv * sn))
        n = tb + 8
        dap = w[2:3, :] * da + w[1:2, :] * pltpu.roll(da, n - 1, 0) + w[0:1, :] * pltpu.roll(da, n - 2, 0)
        da_ref[...] = dap[:tb, :].astype(BF16)
        db_ref[...] = (dg_ref[...] * (conv * sp)[:tb, :]).astype(BF16)
        dam = da[:tb, :]

        @pl.when(i == 0)
        def _():
            dw_ref[...] = jnp.zeros_like(dw_ref)
            dcb_ref[...] = jnp.zeros_like(dcb_ref)

        dw = jnp.concatenate([jnp.sum(dam * e2[8:8 + tb, :], axis=0, keepdims=True),
                              jnp.sum(dam * e1[8:8 + tb, :], axis=0, keepdims=True),
                              jnp.sum(dam * a, axis=0, keepdims=True)], axis=0)
        dw_ref[...] += dw
        dcb_ref[...] += jnp.sum(dam, axis=0, keepdims=True)

    main_a = pl.BlockSpec((tb, cn), lambda j, i: (i, j))
    prev_a = pl.BlockSpec((8, cn), lambda j, i: (jnp.maximum(i * hb - 1, 0), j))
    next_a = pl.BlockSpec((8, cn), lambda j, i: (jnp.minimum((i + 1) * hb, last8), j))
    main_b = pl.BlockSpec((tb, cn), lambda j, i: (i, j + ncb))
    next_b = pl.BlockSpec((8, cn), lambda j, i: (jnp.minimum((i + 1) * hb, last8), j + ncb))
    return pl.pallas_call(
        kern, name=name, grid=(ncb, nb),
        in_specs=[main_a, prev_a, next_a, main_b, next_b, main_a, next_a,
                  pl.BlockSpec((3, cn), lambda j, i: (0, j)), pl.BlockSpec((1, cn), lambda j, i: (0, j))],
        out_specs=[main_a, main_a, pl.BlockSpec((3, cn), lambda j, i: (0, j)),
                   pl.BlockSpec((1, cn), lambda j, i: (0, j))],
        out_shape=[jax.ShapeDtypeStruct((t, f), BF16), jax.ShapeDtypeStruct((t, f), BF16),
                   jax.ShapeDtypeStruct((3, f), F32), jax.ShapeDtypeStruct((1, f), F32)],
        compiler_params=_params(("parallel", "arbitrary")),
    )(u, u, u, u, u, dgact, dgact, conv_w, conv_b)


def _hg_gates(qp, fp, lb):
    sq, _ = _sigmoid_pair(qp)
    sf, snf = _sigmoid_pair(fp)
    forget = lb + (1.0 - lb) * sf
    return sq, sf, snf, forget, jnp.log(forget), (1.0 - lb) * snf


def _tri(lower):
    r = lax.broadcasted_iota(jnp.int32, (SUB, SUB), 0)
    c = lax.broadcasted_iota(jnp.int32, (SUB, SUB), 1)
    return ((r >= c) if lower else (r <= c)).astype(BF16)


def _split2(x):
    hi = x.astype(BF16)
    return hi, (x - hi.astype(F32)).astype(BF16)


def _dot3(a, b, dims):
    (ah, al), (bh, bl) = a, b
    return (lax.dot_general(ah, bh, dims, preferred_element_type=F32)
            + (lax.dot_general(ah, bl, dims, preferred_element_type=F32)
               + lax.dot_general(al, bh, dims, preferred_element_type=F32)))


def _running_sum(tri, x):
    hi, lo = _split2(x)
    rest = (x - hi.astype(F32)) - lo.astype(F32)
    return (lax.dot_general(tri, hi, NN, preferred_element_type=F32)
            + (lax.dot_general(tri, lo, NN, preferred_element_type=F32)
               + lax.dot_general(tri, rest.astype(BF16), NN, preferred_element_type=F32)))


HEADS_PER_STEP = 8
STEP_UNROLL = 2


def _hgrn2_fwd(proj, lb_logits, norm_g, name):
    t, d4 = proj.shape
    d = d4 // 4
    nh = d // LANES
    hb = _pick(nh, (HEADS_PER_STEP, 2, 1))
    wb = hb * LANES
    tb = _pick(t, (256, 128, 64, 32, 16))
    nb = t // tb
    nsc = tb // SUB

    def kern(q_ref, f_ref, i_ref, g_ref, lbl_ref, ng_ref, y_ref, o_ref, st_ref, s_ref):
        @pl.when(pl.program_id(1) == 0)
        def _():
            s_ref[...] = jnp.zeros_like(s_ref)

        lb_all = _lower_bound(lbl_ref[...])
        ng_all = ng_ref[...]
        ltri = _tri(True)
        rcol = lax.broadcasted_iota(jnp.int32, (SUB, 1), 0)

        heads = [slice(h * LANES, (h + 1) * LANES) for h in range(hb)]

        def step(sc, carry):
            rows = pl.ds(pl.multiple_of(sc * SUB, SUB), SUB)
            qp, fp, v, gp = q_ref[rows, :], f_ref[rows, :], i_ref[rows, :], g_ref[rows, :]
            sq, _, _, _, lf, k = _hg_gates(qp, fp, lb_all)
            q = qp * sq
            bl = _running_sum(ltri, lf)
            bend = bl[SUB - 1:SUB, :]
            dec = jnp.exp(bend)
            qs2 = _split2(q * jnp.exp(bl))
            kd2 = _split2(k * jnp.exp(bend - bl))
            v2 = _split2(v)
            states = [s_ref[h] for h in range(hb)]
            o = [_dot3((qs2[0][:, c], qs2[1][:, c]), _split2(states[h]), NT) for h, c in enumerate(heads)]
            for s in range(SUB):
                e = jnp.exp(jnp.minimum(bl - bl[s:s + 1, :], 0.0))
                p = q * e * k[s:s + 1, :]
                for h, c in enumerate(heads):
                    a = jnp.sum(p[:, c], axis=1, keepdims=True)
                    o[h] = o[h] + jnp.where(rcol >= s, a, 0.0) * v[s:s + 1, c]
            for h, c in enumerate(heads):
                st_ref[sc, h] = states[h]
                s_ref[h] = states[h] * dec[:, c] + _dot3((v2[0][:, c], v2[1][:, c]), (kd2[0][:, c], kd2[1][:, c]), TN)
            o_ref[rows, :] = jnp.concatenate(o, axis=1)
            on = jnp.concatenate(
                [oh * lax.rsqrt(jnp.mean(oh * oh, axis=1, keepdims=True) + RMS_EPS) for oh in o], axis=1)
            sg, _ = _sigmoid_pair(gp)
            y_ref[rows, :] = (on * ng_all * (gp * sg)).astype(BF16)
            return carry

        lax.fori_loop(0, nsc, step, 0, unroll=STEP_UNROLL)

    def col(off):
        return pl.BlockSpec((tb, wb), lambda h, j: (j, h + off * (nh // hb)))

    return pl.pallas_call(
        kern, name=name, grid=(nh // hb, nb),
        in_specs=[col(0), col(1), col(2), col(3),
                  pl.BlockSpec((3, wb), lambda h, j: (0, h)), pl.BlockSpec((1, wb), lambda h, j: (0, h))],
        out_specs=[col(0), col(0), pl.BlockSpec((nsc, hb, LANES, LANES), lambda h, j: (j, h, 0, 0))],
        out_shape=[jax.ShapeDtypeStruct((t, d), BF16), jax.ShapeDtypeStruct((t, d), F32),
                   jax.ShapeDtypeStruct((t // SUB, nh, LANES, LANES), F32)],
        scratch_shapes=[pltpu.VMEM((hb, LANES, LANES), F32)],
        compiler_params=_params(("parallel", "arbitrary")),
    )(proj, proj, proj, proj, lb_logits, norm_g)


def _hgrn2_bwd(proj, lb_logits, norm_g, o_raw, states, dy, name):
    t, d4 = proj.shape
    d = d4 // 4
    nh = d // LANES
    hb = _pick(nh, (HEADS_PER_STEP, 2, 1))
    wb = hb * LANES
    tb = _pick(t, (256, 128, 64, 32, 16))
    nb = t // tb
    nsc = tb // SUB

    def kern(q_ref, f_ref, i_ref, g_ref, lbl_ref, ng_ref, o_ref, st_ref, dy_ref,
             dq_ref, df_ref, di_ref, dgp_ref, dlb_ref, dng_ref, ds_ref, gc_ref):
        j = pl.program_id(1)

        @pl.when(j == 0)
        def _():
            ds_ref[...] = jnp.zeros_like(ds_ref)
            gc_ref[...] = jnp.zeros_like(gc_ref)
            dlb_ref[...] = jnp.zeros_like(dlb_ref)
            dng_ref[...] = jnp.zeros_like(dng_ref)

        lb_all = _lower_bound(lbl_ref[...])
        ng_all = ng_ref[...]
        ltri, utri = _tri(True), _tri(False)
        rcol = lax.broadcasted_iota(jnp.int32, (SUB, 1), 0)
        rid = lax.broadcasted_iota(jnp.int32, (SUB, wb), 0)

        heads = [slice(h * LANES, (h + 1) * LANES) for h in range(hb)]

        def per_head(fn):
            return jnp.concatenate([jnp.broadcast_to(fn(c), (SUB, LANES)) for c in heads], axis=1)

        def step(it, carry):
            sc = nsc - 1 - it
            rows = pl.ds(pl.multiple_of(sc * SUB, SUB), SUB)
            qp, fp, v, gp = q_ref[rows, :], f_ref[rows, :], i_ref[rows, :], g_ref[rows, :]
            o, dyv = o_ref[rows, :], dy_ref[rows, :]
            sq, sf, snf, forget, lf, k = _hg_gates(qp, fp, lb_all)
            q = qp * sq
            bl = _running_sum(ltri, lf)
            ebl = jnp.exp(bl)
            bend = bl[SUB - 1:SUB, :]
            dec = jnp.exp(bend)
            dte = jnp.exp(bend - bl)
            r = per_head(lambda c: lax.rsqrt(jnp.mean(o[:, c] * o[:, c], axis=1, keepdims=True) + RMS_EPS))
            ohat = o * r
            sg, sng = _sigmoid_pair(gp)
            don = dyv * (gp * sg)
            dgp_ref[rows, :] = (dyv * (ohat * ng_all) * (sg * (1.0 + gp * sng))).astype(BF16)
            dng_ref[...] += jnp.sum(don * ohat, axis=0, keepdims=True)
            doh = don * ng_all
            dot_oh = doh * ohat
            do = r * (doh - ohat * per_head(lambda c: jnp.mean(dot_oh[:, c], axis=1, keepdims=True)))
            do2, qs2, kd2, v2 = _split2(do), _split2(q * ebl), _split2(k * dte), _split2(v)
            dq_h, dk_h, dv_h = [], [], []
            for h, c in enumerate(heads):
                dstate = ds_ref[h]
                ds2 = _split2(dstate)
                doc = (do2[0][:, c], do2[1][:, c])
                dq_h.append(_dot3(doc, _split2(st_ref[sc, h]), NN))
                dv_h.append(_dot3((kd2[0][:, c], kd2[1][:, c]), ds2, NT))
                dk_h.append(_dot3((v2[0][:, c], v2[1][:, c]), ds2, NN))
                ds_ref[h] = dstate * dec[:, c] + _dot3(doc, (qs2[0][:, c], qs2[1][:, c]), TN)
            dq = jnp.concatenate(dq_h, axis=1) * ebl
            dk = jnp.concatenate(dk_h, axis=1) * dte
            dv = jnp.concatenate(dv_h, axis=1)
            dki = jnp.zeros((SUB, wb), F32)
            dvi = jnp.zeros((SUB, wb), F32)
            for s in range(SUB):
                e = jnp.exp(jnp.minimum(bl - bl[s:s + 1, :], 0.0))
                qe = q * e
                ks = k[s:s + 1, :]
                live = rcol >= s
                pk = qe * ks
                pv = do * v[s:s + 1, :]
                a = per_head(lambda c: jnp.where(live, jnp.sum(pk[:, c], axis=1, keepdims=True), 0.0))
                da = per_head(lambda c: jnp.where(live, jnp.sum(pv[:, c], axis=1, keepdims=True), 0.0))
                dq = dq + da * (e * ks)
                dki = jnp.where(rid == s, jnp.sum(da * qe, axis=0, keepdims=True), dki)
                dvi = jnp.where(rid == s, jnp.sum(a * do, axis=0, keepdims=True), dvi)
            dk = dk + dki
            dv = dv + dvi
            w = q * dq - k * dk
            gc = gc_ref[...]
            dlf = _running_sum(utri, w) + gc
            gc_ref[...] = gc + jnp.sum(w, axis=0, keepdims=True)
            t1 = dlf / forget - dk
            df_ref[rows, :] = ((1.0 - lb_all) * sf * snf * t1).astype(BF16)
            dlb_ref[...] += jnp.sum(snf * t1, axis=0, keepdims=True)
            dq_ref[rows, :] = (dq * (sq * (1.0 + qp * (1.0 - sq)))).astype(BF16)
            di_ref[rows, :] = dv.astype(BF16)
            return carry

        lax.fori_loop(0, nsc, step, 0, unroll=STEP_UNROLL)

    def col(off):
        return pl.BlockSpec((tb, wb), lambda h, j: (nb - 1 - j, h + off * (nh // hb)))

    vec = pl.BlockSpec((1, wb), lambda h, j: (0, h))
    return pl.pallas_call(
        kern, name=name, grid=(nh // hb, nb),
        in_specs=[col(0), col(1), col(2), col(3), pl.BlockSpec((3, wb), lambda h, j: (0, h)), vec,
                  col(0), pl.BlockSpec((nsc, hb, LANES, LANES), lambda h, j: (nb - 1 - j, h, 0, 0)), col(0)],
        out_specs=[col(0), col(0), col(0), col(0), vec, vec],
        out_shape=[jax.ShapeDtypeStruct((t, d), BF16)] * 4 + [jax.ShapeDtypeStruct((1, d), F32)] * 2,
        scratch_shapes=[pltpu.VMEM((hb, LANES, LANES), F32), pltpu.VMEM((1, wb), F32)],
        compiler_params=_params(("parallel", "arbitrary")),
    )(proj, proj, proj, proj, lb_logits, norm_g, o_raw, states, dy)


_INV_SQRT2 = 0.7071067811865476
_INV_SQRT2PI = 0.3989422804014327


def _gelu(x):
    return 0.5 * x * (1.0 + lax.erf(x * _INV_SQRT2))


def _gelu_grad(x):
    return 0.5 * (1.0 + lax.erf(x * _INV_SQRT2)) + x * jnp.exp(-0.5 * x * x) * _INV_SQRT2PI


def _causal(w):
    r = lax.broadcasted_iota(jnp.int32, (GCHUNK, GCHUNK), 0)
    c = lax.broadcasted_iota(jnp.int32, (GCHUNK, GCHUNK), 1)
    return jnp.where(r >= c, w, 0.0)


def _sg_gate_fwd(pre, ln_g, ln_b, w_s, b_s_t, name):
    t, d2 = pre.shape
    d = d2 // 2
    ng = d // LANES

    def kern(pre_ref, g_ref, b_ref, ws_ref, bs_ref, y_ref):
        z = _gelu(pre_ref[...])
        u = z[:, :d]
        vhat, _ = _ln_hat(z[:, d:])
        vn = (vhat * g_ref[...] + b_ref[...]).astype(BF16)
        bs = bs_ref[...]
        for g in range(ng):
            cols = slice(g * LANES, (g + 1) * LANES)
            wc = _causal(ws_ref[g]).astype(BF16)
            gate = jnp.dot(wc, vn[:, cols], preferred_element_type=F32) + bs[:, g:g + 1]
            y_ref[:, cols] = (u[:, cols] * gate).astype(BF16)

    vec = pl.BlockSpec((1, d), lambda i: (0, 0))
    return pl.pallas_call(
        kern, name=name, grid=(t // GCHUNK,),
        in_specs=[pl.BlockSpec((GCHUNK, d2), lambda i: (i, 0)), vec, vec,
                  pl.BlockSpec((ng, GCHUNK, GCHUNK), lambda i: (0, 0, 0)),
                  pl.BlockSpec((GCHUNK, ng), lambda i: (0, 0))],
        out_specs=pl.BlockSpec((GCHUNK, d), lambda i: (i, 0)),
        out_shape=jax.ShapeDtypeStruct((t, d), BF16),
        compiler_params=_params(("parallel",)),
    )(pre, ln_g, ln_b, w_s, b_s_t)


def _sg_gate_bwd(pre, dy, ln_g, ln_b, w_s, b_s_t, name):
    t, d2 = pre.shape
    d = d2 // 2
    ng = d // LANES

    def kern(pre_ref, dy_ref, g_ref, b_ref, ws_ref, bs_ref, dpre_ref, dws_ref, dbs_ref, dg_ref, db_ref, dvn_ref):
        @pl.when(pl.program_id(0) == 0)
        def _():
            dws_ref[...] = jnp.zeros_like(dws_ref)
            dbs_ref[...] = jnp.zeros_like(dbs_ref)
            dg_ref[...] = jnp.zeros_like(dg_ref)
            db_ref[...] = jnp.zeros_like(db_ref)

        pre = pre_ref[...]
        z = _gelu(pre)
        u = z[:, :d]
        vhat, rstd = _ln_hat(z[:, d:])
        gv = g_ref[...]
        vn = (vhat * gv + b_ref[...]).astype(BF16)
        bs = bs_ref[...]
        dyv = dy_ref[...]
        gp = _gelu_grad(pre)
        lane = lax.broadcasted_iota(jnp.int32, (GCHUNK, ng), 1)
        dbs = jnp.zeros((GCHUNK, ng), F32)
        for g in range(ng):
            cols = slice(g * LANES, (g + 1) * LANES)
            wc = _causal(ws_ref[g]).astype(BF16)
            vng = vn[:, cols]
            gate = jnp.dot(wc, vng, preferred_element_type=F32) + bs[:, g:g + 1]
            dpre_ref[:, cols] = (dyv[:, cols] * gate * gp[:, cols]).astype(BF16)
            dgate = dyv[:, cols] * u[:, cols]
            dbs = dbs + jnp.where(lane == g, jnp.sum(dgate, axis=1, keepdims=True), 0.0)
            dgb = dgate.astype(BF16)
            dws_ref[g] += _causal(lax.dot_general(dgb, vng, NT, preferred_element_type=F32))
            dvn_ref[:, cols] = lax.dot_general(wc, dgb, TN, preferred_element_type=F32)
        dbs_ref[...] += dbs
        dvn = dvn_ref[...]
        dg_ref[...] += jnp.sum(dvn * vhat, axis=0, keepdims=True)
        db_ref[...] += jnp.sum(dvn, axis=0, keepdims=True)
        dvh = dvn * gv
        m1 = jnp.mean(dvh, axis=-1, keepdims=True)
        m2 = jnp.mean(dvh * vhat, axis=-1, keepdims=True)
        dpre_ref[:, d:] = (rstd * (dvh - m1 - vhat * m2) * gp[:, d:]).astype(BF16)

    vec = pl.BlockSpec((1, d), lambda i: (0, 0))
    wsp = pl.BlockSpec((ng, GCHUNK, GCHUNK), lambda i: (0, 0, 0))
    bsp = pl.BlockSpec((GCHUNK, ng), lambda i: (0, 0))
    return pl.pallas_call(
        kern, name=name, grid=(t // GCHUNK,),
        in_specs=[pl.BlockSpec((GCHUNK, d2), lambda i: (i, 0)), pl.BlockSpec((GCHUNK, d), lambda i: (i, 0)),
                  vec, vec, wsp, bsp],
        out_specs=[pl.BlockSpec((GCHUNK, d2), lambda i: (i, 0)), wsp, bsp, vec, vec],
        out_shape=[jax.ShapeDtypeStruct((t, d2), BF16), jax.ShapeDtypeStruct((ng, GCHUNK, GCHUNK), F32),
                   jax.ShapeDtypeStruct((GCHUNK, ng), F32), jax.ShapeDtypeStruct((1, d), F32),
                   jax.ShapeDtypeStruct((1, d), F32)],
        scratch_shapes=[pltpu.VMEM((GCHUNK, d), F32)],
        compiler_params=_params(("arbitrary",)),
    )(pre, dy, ln_g, ln_b, w_s, b_s_t)


def _adamw_math(w, g, m, v):
    m = ADAM_B1 * m + (1.0 - ADAM_B1) * g
    v = ADAM_B2 * v + (1.0 - ADAM_B2) * (g * g)
    m_hat = m / (1.0 - ADAM_B1 ** ADAM_STEP)
    v_hat = v / (1.0 - ADAM_B2 ** ADAM_STEP)
    return -ADAM_LR * (m_hat / (jnp.sqrt(v_hat) + ADAM_EPS) + ADAM_WD * w), m, v


ADAMW_BLOCK_BYTES = 3 << 19


def _adamw(w, gs, m, v, name):
    nl, r, c = w.shape
    rb = _pick(r, tuple(p for p in (512, 256, 128, 64, 32, 16, 8) if p * c * 4 <= ADAMW_BLOCK_BYTES))

    def kern(w_ref, m_ref, v_ref, *rest):
        g_refs, (d_ref, mo_ref, vo_ref, go_ref) = rest[:nl], rest[nl:]
        layer = pl.program_id(0)
        g = g_refs[0][...]
        for k in range(1, nl):
            g = jnp.where(layer == k, g_refs[k][...], g)
        dlt, mm, vv = _adamw_math(w_ref[...], g, m_ref[...], v_ref[...])
        d_ref[...] = dlt
        mo_ref[...] = mm
        vo_ref[...] = vv
        go_ref[...] = g

    blk = pl.BlockSpec((None, rb, c), lambda l, i: (l, i, 0))
    g_specs = [pl.BlockSpec((rb, c), lambda l, i, k=k: (jnp.where(l == k, i, 0), 0)) for k in range(nl)]
    return pl.pallas_call(
        kern, name=name, grid=(nl, r // rb), in_specs=[blk] * 3 + g_specs, out_specs=[blk] * 4,
        out_shape=[jax.ShapeDtypeStruct((nl, r, c), F32)] * 4,
        compiler_params=_params(("parallel", "parallel")),
    )(w, m, v, *gs)


def _lb_logits_grad(lb_logits, dlb, name):
    def kern(l_ref, d_ref, o_ref):
        lg = l_ref[...]
        m = jnp.max(lg, axis=0, keepdims=True)
        e = jnp.exp(lg - m)
        p = e / jnp.sum(e, axis=0, keepdims=True)
        row = lax.broadcasted_iota(jnp.int32, lg.shape, 0)
        o_ref[...] = d_ref[...] * p[0:1, :] * (jnp.where(row == 0, 1.0, 0.0) - p)

    return pl.pallas_call(kern, name=name, out_shape=jax.ShapeDtypeStruct(lb_logits.shape, F32))(lb_logits, dlb)


def _sum_leading(a, name):
    n, r, c = a.shape
    rb = _pick(r, (512, 256, 128, 64, 32, 16, 8))

    def kern(a_ref, o_ref):
        acc = a_ref[0]
        for i in range(1, n):
            acc = acc + a_ref[i]
        o_ref[...] = acc

    return pl.pallas_call(
        kern, name=name, grid=(r // rb,), in_specs=[pl.BlockSpec((n, rb, c), lambda i: (0, i, 0))],
        out_specs=pl.BlockSpec((rb, c), lambda i: (i, 0)), out_shape=jax.ShapeDtypeStruct((r, c), F32),
        compiler_params=_params(("parallel",)),
    )(a)


def _place():
    x, y, c = lax.axis_index("x"), lax.axis_index("y"), lax.axis_index("c")
    return x, y, c


def _chip_rel(x, y, r):
    px = x if r < 2 else 1 - x
    py = y if r % 2 == 0 else 1 - y
    return px, py, 2 * px + py


def _allgather_split(arrs, name):
    n = len(arrs)
    slots = 7

    def body(*refs):
        ins, outs = refs[:n], refs[n:2 * n]
        send_sems, recv_sems = refs[2 * n:]
        x, y, c = _place()
        me = 2 * x + y
        sib = (x, y, 1 - c)

        def half(a, shard, hc):
            h = ins[a].shape[0] // 2
            return outs[a].at[shard, pl.ds(hc * h, h), :]

        def src_half(a):
            h = ins[a].shape[0] // 2
            return ins[a].at[pl.ds(c * h, h), :]

        def copy(a, slot, src, dst, to):
            return pltpu.make_async_remote_copy(src_ref=src, dst_ref=dst, send_sem=send_sems.at[a * slots + slot],
                                                recv_sem=recv_sems.at[a * slots + slot], device_id=to,
                                                device_id_type=MESH)

        first = []
        for r in (1, 2, 3):
            px, py, _ = _chip_rel(x, y, r)
            for a in range(n):
                first.append(copy(a, r - 1, src_half(a), half(a, me, c), (px, py, c)))
        own = [copy(a, 6, ins[a], outs[a].at[me], sib) for a in range(n)]
        for cp in first + own:
            cp.start()
        passed = []
        for r in (1, 2, 3):
            _, _, shard = _chip_rel(x, y, r)
            for a in range(n):
                copy(a, r - 1, src_half(a), half(a, shard, c), sib).wait_recv()
                cp = copy(a, 3 + r - 1, half(a, shard, c), half(a, shard, c), sib)
                cp.start()
                passed.append(cp)
        for r in (1, 2, 3):
            _, _, shard = _chip_rel(x, y, r)
            for a in range(n):
                copy(a, 3 + r - 1, src_half(a), half(a, shard, 1 - c), sib).wait_recv()
        for cp in own:
            cp.wait_recv()
        for cp in first + passed + own:
            cp.wait_send()

    anyspec = pl.BlockSpec(memory_space=pl.ANY)
    return pl.pallas_call(
        body, name=name, in_specs=[anyspec] * n, out_specs=[anyspec] * n,
        out_shape=[jax.ShapeDtypeStruct((N_CHIPS,) + a.shape, a.dtype) for a in arrs],
        scratch_shapes=[pltpu.SemaphoreType.DMA((slots * n,)), pltpu.SemaphoreType.DMA((slots * n,))],
        compiler_params=pltpu.CompilerParams(has_side_effects=True),
    )(*arrs)


def _allgather_whole(arr, name):
    def body(in_ref, out_ref, send_sems, recv_sems, loc_sem):
        x, y, c = _place()
        me = 2 * x + y
        local = pltpu.make_async_copy(in_ref, out_ref.at[me], loc_sem)
        local.start()
        sends = []
        for r in (1, 2, 3):
            px, py, _ = _chip_rel(x, y, r)
            sends.append(pltpu.make_async_remote_copy(
                src_ref=in_ref, dst_ref=out_ref.at[me], send_sem=send_sems.at[r - 1], recv_sem=recv_sems.at[r - 1],
                device_id=(px, py, c), device_id_type=MESH))
        for cp in sends:
            cp.start()
        for r in (1, 2, 3):
            px, py, shard = _chip_rel(x, y, r)
            pltpu.make_async_remote_copy(
                src_ref=in_ref, dst_ref=out_ref.at[shard], send_sem=send_sems.at[r - 1], recv_sem=recv_sems.at[r - 1],
                device_id=(px, py, c), device_id_type=MESH).wait_recv()
        for cp in sends:
            cp.wait_send()
        local.wait()

    anyspec = pl.BlockSpec(memory_space=pl.ANY)
    return pl.pallas_call(
        body, name=name, in_specs=[anyspec], out_specs=anyspec,
        out_shape=jax.ShapeDtypeStruct((N_CHIPS,) + arr.shape, arr.dtype),
        scratch_shapes=[pltpu.SemaphoreType.DMA((3,)), pltpu.SemaphoreType.DMA((3,)), pltpu.SemaphoreType.DMA],
        compiler_params=pltpu.CompilerParams(has_side_effects=True),
    )(arr)


def _swap_halves(grads, name):
    n = len(grads)

    def body(*refs):
        ins, outs = refs[:n], refs[n:2 * n]
        send_sems, recv_sems = refs[2 * n:]
        x, y, c = _place()
        sib = (x, y, 1 - c)
        copies = []
        for a in range(n):
            for j in range(N_CHIPS):
                copies.append(pltpu.make_async_remote_copy(
                    src_ref=ins[a].at[j, 1 - c], dst_ref=outs[a].at[j], send_sem=send_sems.at[a * N_CHIPS + j],
                    recv_sem=recv_sems.at[a * N_CHIPS + j], device_id=sib, device_id_type=MESH))
        for cp in copies:
            cp.start()
        for cp in copies:
            cp.wait()

    anyspec = pl.BlockSpec(memory_space=pl.ANY)
    return pl.pallas_call(
        body, name=name, in_specs=[anyspec] * n, out_specs=[anyspec] * n,
        out_shape=[jax.ShapeDtypeStruct((N_CHIPS,) + g.shape[2:], g.dtype) for g in grads],
        scratch_shapes=[pltpu.SemaphoreType.DMA((N_CHIPS * n,)), pltpu.SemaphoreType.DMA((N_CHIPS * n,))],
        compiler_params=pltpu.CompilerParams(has_side_effects=True),
    )(*grads)


def _scatter_chips(parts, name):
    n = len(parts)

    def body(*refs):
        ins, outs = refs[:n], refs[n:2 * n]
        send_sems, recv_sems = refs[2 * n:]
        x, y, c = _place()
        copies = []
        for r in (1, 2, 3):
            px, py, shard = _chip_rel(x, y, r)
            for a in range(n):
                copies.append(pltpu.make_async_remote_copy(
                    src_ref=ins[a].at[shard], dst_ref=outs[a].at[r - 1], send_sem=send_sems.at[a * 3 + r - 1],
                    recv_sem=recv_sems.at[a * 3 + r - 1], device_id=(px, py, c), device_id_type=MESH))
        for cp in copies:
            cp.start()
        for cp in copies:
            cp.wait()

    anyspec = pl.BlockSpec(memory_space=pl.ANY)
    return pl.pallas_call(
        body, name=name, in_specs=[anyspec] * n, out_specs=[anyspec] * n,
        out_shape=[jax.ShapeDtypeStruct((3,) + p.shape[1:], p.dtype) for p in parts],
        scratch_shapes=[pltpu.SemaphoreType.DMA((3 * n,)), pltpu.SemaphoreType.DMA((3 * n,))],
        compiler_params=pltpu.CompilerParams(has_side_effects=True),
    )(*parts)


def _join_halves(bufs, name):
    n = len(bufs)

    def body(*refs):
        outs = refs[n:2 * n]
        send_sems, recv_sems = refs[2 * n:]
        x, y, c = _place()
        copies = [pltpu.make_async_remote_copy(
            src_ref=outs[a].at[c], dst_ref=outs[a].at[c], send_sem=send_sems.at[a], recv_sem=recv_sems.at[a],
            device_id=(x, y, 1 - c), device_id_type=MESH) for a in range(n)]
        for cp in copies:
            cp.start()
        for cp in copies:
            cp.wait()

    anyspec = pl.BlockSpec(memory_space=pl.ANY)
    return pl.pallas_call(
        body, name=name, in_specs=[anyspec] * n, out_specs=[anyspec] * n,
        out_shape=[jax.ShapeDtypeStruct(b.shape, b.dtype) for b in bufs],
        input_output_aliases={a: a for a in range(n)},
        scratch_shapes=[pltpu.SemaphoreType.DMA((n,)), pltpu.SemaphoreType.DMA((n,))],
        compiler_params=pltpu.CompilerParams(has_side_effects=True),
    )(*bufs)


def _gather_all_devices(buf, name):
    def body(in_ref, out_ref, send_sems, recv_sems, loc_sem):
        x, y, c = _place()
        me = 4 * x + 2 * y + c
        local = pltpu.make_async_copy(in_ref, out_ref.at[me], loc_sem)
        local.start()
        sends = []
        for r in range(1, N_DEV):
            px, py, _ = _chip_rel(x, y, r // 2)
            pc = c if r % 2 == 0 else 1 - c
            sends.append(pltpu.make_async_remote_copy(
                src_ref=in_ref, dst_ref=out_ref.at[me], send_sem=send_sems.at[r - 1], recv_sem=recv_sems.at[r - 1],
                device_id=(px, py, pc), device_id_type=MESH))
        for cp in sends:
            cp.start()
        for r in range(1, N_DEV):
            px, py, _ = _chip_rel(x, y, r // 2)
            pc = c if r % 2 == 0 else 1 - c
            pltpu.make_async_remote_copy(
                src_ref=in_ref, dst_ref=out_ref.at[4 * px + 2 * py + pc], send_sem=send_sems.at[r - 1],
                recv_sem=recv_sems.at[r - 1], device_id=(px, py, pc), device_id_type=MESH).wait_recv()
        for cp in sends:
            cp.wait_send()
        local.wait()

    anyspec = pl.BlockSpec(memory_space=pl.ANY)
    return pl.pallas_call(
        body, name=name, in_specs=[anyspec], out_specs=anyspec,
        out_shape=jax.ShapeDtypeStruct((N_DEV,) + buf.shape, buf.dtype),
        scratch_shapes=[pltpu.SemaphoreType.DMA((N_DEV - 1,)), pltpu.SemaphoreType.DMA((N_DEV - 1,)),
                        pltpu.SemaphoreType.DMA],
        compiler_params=pltpu.CompilerParams(has_side_effects=True),
    )(buf)


def _add_half(grad, recv, sel, name):
    _, _, rh, cw = grad.shape
    rb = _pick(rh, (512, 256, 176, 128, 64, 32, 16, 8))

    def kern(sel_ref, g_ref, r_ref, o_ref):
        o_ref[...] = (g_ref[...] + r_ref[...]).astype(BF16)

    return pl.pallas_call(
        kern, name=name,
        grid_spec=pltpu.PrefetchScalarGridSpec(
            num_scalar_prefetch=1, grid=(N_CHIPS, rh // rb),
            in_specs=[pl.BlockSpec((None, None, rb, cw), lambda j, i, s: (j, s[0], i, 0)),
                      pl.BlockSpec((None, rb, cw), lambda j, i, s: (j, i, 0))],
            out_specs=pl.BlockSpec((None, rb, cw), lambda j, i, s: (j, i, 0))),
        out_shape=jax.ShapeDtypeStruct((N_CHIPS, rh, cw), BF16),
        compiler_params=_params(("parallel", "parallel")),
    )(sel, grad, recv)


def _add_own(grad, recv, got, sel, name):
    _, _, rh, cw = grad.shape
    rb = _pick(rh, (512, 256, 176, 128, 64, 32, 16, 8))

    def kern(sel_ref, g_ref, r_ref, b_ref, o_ref):
        own = g_ref[...] + r_ref[...]
        o_ref[...] = ((own + b_ref[0].astype(F32)) + b_ref[1].astype(F32)) + b_ref[2].astype(F32)

    return pl.pallas_call(
        kern, name=name,
        grid_spec=pltpu.PrefetchScalarGridSpec(
            num_scalar_prefetch=1, grid=(rh // rb,),
            in_specs=[pl.BlockSpec((None, None, rb, cw), lambda i, s: (s[1], s[0], i, 0)),
                      pl.BlockSpec((None, rb, cw), lambda i, s: (s[1], i, 0)),
                      pl.BlockSpec((3, rb, cw), lambda i, s: (0, i, 0))],
            out_specs=pl.BlockSpec((None, rb, cw), lambda i, s: (s[0], i, 0))),
        out_shape=jax.ShapeDtypeStruct((2, rh, cw), F32),
        compiler_params=_params(("parallel",)),
    )(sel, grad, recv, got)


def _reduce_scatter(grads, tag):
    x, y, c = _place()
    sel = jnp.stack([c, 2 * x + y]).astype(jnp.int32)
    split = [g.reshape(N_CHIPS, 2, g.shape[1] // 2, g.shape[2]) for g in grads]
    recv = _swap_halves(split, f"{tag}_swap")
    parts = [_add_half(g, r, sel, f"{tag}_addhalf{i}") for i, (g, r) in enumerate(zip(split, recv))]
    got = _scatter_chips(parts, f"{tag}_scatter")
    mine = [_add_own(g, r, b, sel, f"{tag}_addown{i}") for i, (g, r, b) in enumerate(zip(split, recv, got))]
    full = _join_halves(mine, f"{tag}_join")
    return [f.reshape(f.shape[0] * f.shape[1], f.shape[2]) for f in full]


def _ffn_fwd(h, hb, w_up, w_down, conv_w, conv_b, ln_g, ln_b, tag):
    u = _matmul(hb, w_up, mode="nn", nsh=N_CHIPS, name=f"{tag}_up")
    gact = _conv_gate_fwd(u, conv_w, conv_b, f"{tag}_gate")
    ffn = _matmul(gact, w_down, mode="nn", nsh=1, name=f"{tag}_down")
    xin, h2, h2b = _res_ln_fwd(h, ffn, ln_g, ln_b, f"{tag}_ln")
    return (u, gact, xin), h2, h2b


def _ffn_bwd(saved, hb_in, dxin, dxin_b, w_up, w_down, conv_w, conv_b, tag):
    u, gact, _ = saved
    dgact = _matmul(dxin_b, w_down, mode="nt", nsh=1, name=f"{tag}_ddown")
    dw_down = _matmul(gact, dxin_b, mode="tn", nsh=1, out_shards=1, name=f"{tag}_wdown")
    da, db, dcw, dcb = _conv_gate_bwd(u, dgact, conv_w, conv_b, f"{tag}_dgate")
    dh = _matmul(da, w_up, mode="nt", nsh=2, b_off=0, resid=dxin, alpha=ALPHA, name=f"{tag}_dup_a")
    dh = _matmul(db, w_up, mode="nt", nsh=2, b_off=2, resid=dh, alpha=1.0, name=f"{tag}_dup_b")
    dw_up = _matmul(hb_in, da, mode="tn", nsh=2, out_off=0, out_shards=N_CHIPS, name=f"{tag}_wup_a")
    dw_up = _matmul(hb_in, db, mode="tn", nsh=2, out_off=2, out_shards=N_CHIPS, out_init=dw_up, name=f"{tag}_wup_b")
    return dh, dw_up, dw_down, dcw, dcb


def _local_step(x2, tgt, xb, wg, sm):
    proj = _matmul(xb, wg["hg_in"], mode="nn", nsh=N_CHIPS, name="hg_in")
    yhg, o_raw, states = _hgrn2_fwd(proj, sm["lb_logits"], sm["hg_norm_g"], "hgrn2_fwd")
    mixed = _matmul(yhg, wg["hg_out"], mode="nn", nsh=1, name="hg_out")
    xin1, h1, h1b = _res_ln_fwd(x2, mixed, sm["ln1_g"][0:1], sm["ln1_b"][0:1], "l0_ln1")
    sv_f0, h2, h2b = _ffn_fwd(h1, h1b, wg["up0"], wg["dn0"], sm["conv_w"][0], sm["conv_b"][0:1],
                              sm["ln2_g"][0:1], sm["ln2_b"][0:1], "l0_ffn")
    pre = _matmul(h2b, wg["sg_in"], mode="nn", nsh=N_CHIPS, name="sg_in")
    ysg = _sg_gate_fwd(pre, sm["sg_ln_g"], sm["sg_ln_b"], sm["sg_w_s"], sm["sg_b_s_t"], "sg_gate")
    mixed = _matmul(ysg, wg["sg_out"], mode="nn", nsh=1, name="sg_out")
    xin3, h3, h3b = _res_ln_fwd(h2, mixed, sm["ln1_g"][1:2], sm["ln1_b"][1:2], "l1_ln1")
    sv_f1, _, _ = _ffn_fwd(h3, h3b, wg["up1"], wg["dn1"], sm["conv_w"][1], sm["conv_b"][1:2],
                           sm["ln2_g"][1:2], sm["ln2_b"][1:2], "l1_ffn")
    gw, gs = {}, {}
    dx, dxb, dg4, db4, loss = _ln_bwd(sv_f1[2], tgt, sm["ln2_g"][1:2], sm["ln2_b"][1:2], "l1_ln2_bwd", loss_head=True)
    dh3, gw["up1"], gw["dn1"], dcw1, dcb1 = _ffn_bwd(sv_f1, h3b, dx, dxb, wg["up1"], wg["dn1"],
                                                     sm["conv_w"][1], sm["conv_b"][1:2], "l1_ffn")
    dx, dxb, dg3, db3 = _ln_bwd(xin3, dh3, sm["ln1_g"][1:2], sm["ln1_b"][1:2], "l1_ln1_bwd")
    dysg = _matmul(dxb, wg["sg_out"], mode="nt", nsh=1, name="sg_dout")
    gw["sg_out"] = _matmul(ysg, dxb, mode="tn", nsh=1, out_shards=1, name="sg_wout")
    dpre, gs["sg_w_s"], dbs_t, gs["sg_ln_g"], gs["sg_ln_b"] = _sg_gate_bwd(
        pre, dysg, sm["sg_ln_g"], sm["sg_ln_b"], sm["sg_w_s"], sm["sg_b_s_t"], "sg_gate_bwd")
    gs["sg_b_s_t"] = dbs_t
    dh2 = _matmul(dpre, wg["sg_in"], mode="nt", nsh=N_CHIPS, resid=dx, alpha=ALPHA, name="sg_din")
    gw["sg_in"] = _matmul(h2b, dpre, mode="tn", nsh=N_CHIPS, out_shards=N_CHIPS, name="sg_win")
    dx, dxb, dg2, db2 = _ln_bwd(sv_f0[2], dh2, sm["ln2_g"][0:1], sm["ln2_b"][0:1], "l0_ln2_bwd")
    dh1, gw["up0"], gw["dn0"], dcw0, dcb0 = _ffn_bwd(sv_f0, h1b, dx, dxb, wg["up0"], wg["dn0"],
                                                     sm["conv_w"][0], sm["conv_b"][0:1], "l0_ffn")
    dx, dxb, dg1, db1 = _ln_bwd(xin1, dh1, sm["ln1_g"][0:1], sm["ln1_b"][0:1], "l0_ln1_bwd")
    dyhg = _matmul(dxb, wg["hg_out"], mode="nt", nsh=1, name="hg_dout")
    gw["hg_out"] = _matmul(yhg, dxb, mode="tn", nsh=1, out_shards=1, name="hg_wout")
    dparts = _hgrn2_bwd(proj, sm["lb_logits"], sm["hg_norm_g"], o_raw, states, dyhg, "hgrn2_bwd")
    gs["lb"], gs["hg_norm_g"] = dparts[4], dparts[5]
    gx = dx
    al = ALPHA
    gwin = None
    for j in range(4):
        gx = _matmul(dparts[j], wg["hg_in"], mode="nt", nsh=1, b_off=j, resid=gx, alpha=al, name=f"hg_din{j}")
        al = 1.0
        gwin = _matmul(xb, dparts[j], mode="tn", nsh=1, out_off=j, out_shards=N_CHIPS, out_init=gwin,
                       name=f"hg_win{j}")
    gw["hg_in"] = gwin
    gs["ln1_g"] = jnp.concatenate([dg1, dg3], axis=0)
    gs["ln1_b"] = jnp.concatenate([db1, db3], axis=0)
    gs["ln2_g"] = jnp.concatenate([dg2, dg4], axis=0)
    gs["ln2_b"] = jnp.concatenate([db2, db4], axis=0)
    gs["conv_w"] = jnp.stack([dcw0, dcw1], axis=0)
    gs["conv_b"] = jnp.concatenate([dcb0, dcb1], axis=0)
    return loss, gx, gw, gs


_BIG = ("hg_in", "hg_out", "sg_in", "sg_out", "up0", "up1", "dn0", "dn1")
_SMALL_ORDER = ("lb", "hg_norm_g", "sg_w_s", "sg_b_s_t", "conv_b", "ln1_g", "ln1_b", "ln2_g", "ln2_b",
                "conv_w", "sg_ln_g", "sg_ln_b")


PACK_ROWS = 512


def _pack(parts):
    flat, layout, off = [], [], 0
    for k in _SMALL_ORDER:
        a = parts[k]
        n = a.size
        pad = (-n) % LANES
        flat.append(jnp.pad(a.reshape(-1), (0, pad)))
        layout.append((k, off, n, a.shape))
        off += n + pad
    flat.append(jnp.zeros(((-off) % (PACK_ROWS * LANES),), F32))
    return jnp.concatenate(flat).reshape(-1, LANES), layout


def _unpack(buf, layout):
    flat = buf.reshape(-1)
    return {k: flat[off:off + n].reshape(shape) for k, off, n, shape in layout}


def kernel(x, lb_logits, hg_w_in, hg_norm_g, hg_w_out, sg_w_in, sg_ln_g, sg_ln_b, sg_w_s, sg_b_s, sg_w_out, ffn_w_up, ffn_conv_w, ffn_conv_b, ffn_w_down, ln1_g, ln1_b, ln2_g, ln2_b, loss_target, m_lb_logits, m_hg_w_in, m_hg_norm_g, m_hg_w_out, m_sg_w_in, m_sg_ln_g, m_sg_ln_b, m_sg_w_s, m_sg_b_s, m_sg_w_out, m_ffn_w_up, m_ffn_conv_w, m_ffn_conv_b, m_ffn_w_down, m_ln1_g, m_ln1_b, m_ln2_g, m_ln2_b, v_lb_logits, v_hg_w_in, v_hg_norm_g, v_hg_w_out, v_sg_w_in, v_sg_ln_g, v_sg_ln_b, v_sg_w_s, v_sg_b_s, v_sg_w_out, v_ffn_w_up, v_ffn_conv_w, v_ffn_conv_b, v_ffn_w_down, v_ln1_g, v_ln1_b, v_ln2_g, v_ln2_b):
    names = ("lb_logits", "hg_w_in", "hg_norm_g", "hg_w_out", "sg_w_in", "sg_ln_g", "sg_ln_b", "sg_w_s", "sg_b_s",
             "sg_w_out", "ffn_w_up", "ffn_conv_w", "ffn_conv_b", "ffn_w_down", "ln1_g", "ln1_b", "ln2_g", "ln2_b")
    w = dict(zip(names, (lb_logits, hg_w_in, hg_norm_g, hg_w_out, sg_w_in, sg_ln_g, sg_ln_b, sg_w_s, sg_b_s,
                         sg_w_out, ffn_w_up, ffn_conv_w, ffn_conv_b, ffn_w_down, ln1_g, ln1_b, ln2_g, ln2_b)))
    mom = dict(zip(names, (m_lb_logits, m_hg_w_in, m_hg_norm_g, m_hg_w_out, m_sg_w_in, m_sg_ln_g, m_sg_ln_b, m_sg_w_s,
                           m_sg_b_s, m_sg_w_out, m_ffn_w_up, m_ffn_conv_w, m_ffn_conv_b, m_ffn_w_down, m_ln1_g,
                           m_ln1_b, m_ln2_g, m_ln2_b)))
    var = dict(zip(names, (v_lb_logits, v_hg_w_in, v_hg_norm_g, v_hg_w_out, v_sg_w_in, v_sg_ln_g, v_sg_ln_b, v_sg_w_s,
                           v_sg_b_s, v_sg_w_out, v_ffn_w_up, v_ffn_conv_w, v_ffn_conv_b, v_ffn_w_down, v_ln1_g,
                           v_ln1_b, v_ln2_g, v_ln2_b)))
    x2, tgt = x[0], loss_target[0]
    d = x2.shape[1]
    fq = ffn_conv_w.shape[2]
    dq = sg_ln_g.shape[1]
    cx, cy, _ = _place()
    me = 2 * cx + cy

    shards = {"hg_in": hg_w_in[0], "hg_out": hg_w_out[0], "sg_in": sg_w_in[0], "sg_out": sg_w_out[0],
              "up0": ffn_w_up[0], "up1": ffn_w_up[1], "dn0": ffn_w_down[0], "dn1": ffn_w_down[1]}
    gathered = _allgather_split([shards[k].astype(BF16) for k in _BIG], "gather_weights")
    wg = dict(zip(_BIG, gathered))
    for k in ("hg_out", "sg_out", "dn0", "dn1"):
        g = wg[k]
        wg[k] = g.reshape(1, g.shape[0] * g.shape[1], g.shape[2])
    wide = max(fq, dq)
    tiny = jnp.concatenate([jnp.pad(ffn_conv_w.reshape(6, fq), ((0, 0), (0, wide - fq))),
                            jnp.pad(sg_ln_g, ((0, 0), (0, wide - dq))),
                            jnp.pad(sg_ln_b, ((0, 0), (0, wide - dq)))], axis=0)
    tiny_all = _allgather_whole(tiny, "gather_small")
    conv_w_full = jnp.transpose(tiny_all[:, 0:6, :fq].reshape(N_CHIPS, 2, 3, fq), (1, 2, 0, 3)).reshape(2, 3, N_CHIPS * fq)
    sm = {"lb_logits": lb_logits, "hg_norm_g": hg_norm_g, "ln1_g": ln1_g, "ln1_b": ln1_b, "ln2_g": ln2_g,
          "ln2_b": ln2_b, "conv_w": conv_w_full, "conv_b": ffn_conv_b,
          "sg_ln_g": tiny_all[:, 6, :dq].reshape(1, N_CHIPS * dq),
          "sg_ln_b": tiny_all[:, 7, :dq].reshape(1, N_CHIPS * dq),
          "sg_w_s": sg_w_s[0], "sg_b_s_t": jnp.transpose(sg_b_s[0])}

    loss_row, grad_x, gw, gs = _local_step(x2, tgt, x2.astype(BF16), wg, sm)
    loss = lax.psum(loss_row[0, 0], ("x", "y", "c"))

    big_full = []
    for k in _BIG:
        g = gw[k]
        if k in ("hg_out", "sg_out", "dn0", "dn1"):
            g = g.reshape(N_CHIPS, g.shape[1] // N_CHIPS, g.shape[2])
        big_full.append(g)
    red = dict(zip(_BIG, _reduce_scatter(big_full, "rs")))
    packed, layout = _pack(gs)
    summed = _unpack(_sum_leading(_gather_all_devices(packed, "gather_small_grads"), "sum_small_grads"), layout)

    grads = {
        "lb_logits": _lb_logits_grad(lb_logits, summed["lb"], "lb_logits_grad"),
        "hg_norm_g": summed["hg_norm_g"],
        "sg_ln_g": lax.dynamic_slice_in_dim(summed["sg_ln_g"], me * dq, dq, axis=1),
        "sg_ln_b": lax.dynamic_slice_in_dim(summed["sg_ln_b"], me * dq, dq, axis=1),
        "sg_w_s": summed["sg_w_s"][None], "sg_b_s": jnp.transpose(summed["sg_b_s_t"])[None],
        "ffn_conv_w": lax.dynamic_slice_in_dim(summed["conv_w"], me * fq, fq, axis=2),
        "ffn_conv_b": summed["conv_b"],
        "ln1_g": summed["ln1_g"], "ln1_b": summed["ln1_b"], "ln2_g": summed["ln2_g"], "ln2_b": summed["ln2_b"],
    }

    big_parts = {"hg_w_in": ("hg_in",), "hg_w_out": ("hg_out",), "sg_w_in": ("sg_in",), "sg_w_out": ("sg_out",),
                 "ffn_w_up": ("up0", "up1"), "ffn_w_down": ("dn0", "dn1")}
    delta, new_m, new_v = {}, {}, {}
    for k, parts in big_parts.items():
        delta[k], new_m[k], new_v[k], grads[k] = _adamw(w[k], [red[p] for p in parts], mom[k], var[k], f"adamw_{k}")
    small_names = [k for k in names if k not in big_parts]

    def pack_small(src):
        flat = [src[k].reshape(-1) for k in small_names]
        n = sum(a.size for a in flat)
        flat.append(jnp.zeros(((-n) % (PACK_ROWS * LANES),), F32))
        return jnp.concatenate(flat).reshape(1, -1, LANES)

    outs = _adamw(pack_small(w), [pack_small(grads)[0]], pack_small(mom), pack_small(var), "adamw_small")
    off = 0
    for k in small_names:
        n = w[k].size
        for dst, o in zip((delta, new_m, new_v), outs):
            dst[k] = o.reshape(-1)[off:off + n].reshape(w[k].shape)
        off += n

    return (loss, grad_x[None], *[grads[k] for k in names], *[delta[k] for k in names],
            *[new_m[k] for k in names], *[new_v[k] for k in names])
```

```python
import functools

import jax
import jax.numpy as jnp
from jax import lax
from jax.experimental import pallas as pl
from jax.experimental.pallas import tpu as pltpu

F32 = jnp.float32
BF16 = jnp.bfloat16
HI = lax.Precision.HIGHEST
MESH = pl.DeviceIdType.MESH

ALPHA = (2 * 2) ** 0.25
LN_EPS = 1e-5
RMS_EPS = 1e-6
ADAM_LR, ADAM_B1, ADAM_B2, ADAM_EPS, ADAM_WD, ADAM_STEP = 0.001, 0.9, 0.999, 1e-08, 0.01, 10

LANES = 128
SUB = 16
GCHUNK = 128
VMEM_LIMIT = 56 * 1024 * 1024
N_CHIPS = 4
N_DEV = 8

NT = (((1,), (1,)), ((), ()))
TN = (((0,), (0,)), ((), ()))
NN = (((1,), (0,)), ((), ()))


def _pick(dim, prefs):
    for p in prefs:
        if dim % p == 0:
            return p
    return dim


def _params(sem=None, **kw):
    return pltpu.CompilerParams(dimension_semantics=sem, vmem_limit_bytes=VMEM_LIMIT, **kw)


def _sigmoid_pair(x):
    e = jnp.exp(-jnp.abs(x))
    inv = 1.0 / (1.0 + e)
    pos = x >= 0
    return jnp.where(pos, inv, e * inv), jnp.where(pos, e * inv, inv)


def _ln_hat(x):
    mu = jnp.mean(x, axis=-1, keepdims=True)
    xc = x - mu
    var = jnp.mean(xc * xc, axis=-1, keepdims=True)
    rstd = lax.rsqrt(var + LN_EPS)
    return xc * rstd, rstd


def _lower_bound(logits):
    m = jnp.max(logits, axis=0, keepdims=True)
    e = jnp.exp(logits - m)
    return e[0:1, :] / jnp.sum(e, axis=0, keepdims=True)


MATMUL_VMEM_BUDGET = 40 * 1024 * 1024


def _fit_bk(kdim, bm, bn, out_dtype, has_resid):
    fixed = bm * bn * (4 + 2 * jnp.dtype(out_dtype).itemsize + (8 if has_resid else 0))
    best = LANES
    for bk in range(LANES, kdim + 1, LANES):
        if kdim % bk == 0 and fixed + 4 * bk * (bm + bn) <= MATMUL_VMEM_BUDGET:
            best = bk
    return best


def _matmul(a, b, *, mode, name, out_dtype=F32, resid=None, alpha=1.0, b_off=0, nsh=None,
            out_init=None, out_off=0, out_shards=None, comm=None):
    if mode == "nn":
        m, kdim = a.shape
        _, _, ns = b.shape
        bm = _pick(m, (1024, 512, 256, 128))
        bn = _pick(ns, (1024, 1408, 512, 256, 128))
        bk = _fit_bk(kdim, bm, bn, out_dtype, resid is not None)
        nps = ns // bn
        grid = (m // bm, nsh * nps, kdim // bk)
        a_spec = pl.BlockSpec((bm, bk), lambda i, j, k: (i, k))
        b_spec = pl.BlockSpec((None, bk, bn), lambda i, j, k: (b_off + j // nps, k, j % nps))
        o_spec = pl.BlockSpec((bm, bn), lambda i, j, k: (i, j))
        out_shape = jax.ShapeDtypeStruct((m, nsh * ns), out_dtype)
        dims = NN
    elif mode == "nt":
        m = a.shape[0]
        _, kdim, ns = b.shape
        bm = _pick(m, (1024, 512, 256, 128))
        bn = _pick(kdim, (1024, 1408, 512, 256, 128))
        bk = _fit_bk(ns, bm, bn, out_dtype, resid is not None)
        kps = ns // bk
        grid = (m // bm, kdim // bn, nsh * kps)
        a_spec = pl.BlockSpec((bm, bk), lambda i, j, k: (i, k))
        b_spec = pl.BlockSpec((None, bn, bk), lambda i, j, k: (b_off + k // kps, j, k % kps))
        o_spec = pl.BlockSpec((bm, bn), lambda i, j, k: (i, j))
        out_shape = jax.ShapeDtypeStruct((m, kdim), out_dtype)
        dims = NT
    else:
        t, kdim = a.shape
        ns = b.shape[1] // nsh
        bm = _pick(kdim, (1024, 1408, 512, 256, 128))
        bn = _pick(ns, (1024, 1408, 512, 256, 128))
        bk = _fit_bk(t, bm, bn, out_dtype, resid is not None)
        nps = ns // bn
        grid = (kdim // bm, nsh * nps, t // bk)
        a_spec = pl.BlockSpec((bk, bm), lambda i, j, k: (k, i))
        b_spec = pl.BlockSpec((bk, bn), lambda i, j, k: (k, j))
        o_spec = pl.BlockSpec((None, bm, bn), lambda i, j, k: (out_off + j // nps, i, j % nps))
        out_shape = jax.ShapeDtypeStruct((out_shards, kdim, ns), out_dtype)
        dims = TN
    nk = grid[2]
    has_resid = resid is not None
    has_init = out_init is not None

    def kern(*refs):
        a_ref, b_ref = refs[0], refs[1]
        r_ref = refs[2] if has_resid else None

        def finish(r, o_ref):
            if has_resid:
                r = r + alpha * r_ref[...]
            o_ref[...] = r.astype(o_ref.dtype)

        part = lax.dot_general(a_ref[...], b_ref[...], dims, preferred_element_type=F32)
        if nk == 1:
            finish(part, refs[-1])
            return
        o_ref, acc_ref = refs[-2], refs[-1]
        k = pl.program_id(2)

        @pl.when(k == 0)
        def _():
            acc_ref[...] = part

        @pl.when(k > 0)
        def _():
            acc_ref[...] += part

        @pl.when(k == nk - 1)
        def _():
            finish(acc_ref[...], o_ref)

    in_specs = [a_spec, b_spec]
    operands = [a, b]
    if has_resid:
        in_specs.append(pl.BlockSpec((bm, bn), lambda i, j, k: (i, j)))
        operands.append(resid)
    aliases = {}
    if has_init:
        in_specs.append(pl.BlockSpec(memory_space=pl.ANY))
        operands.append(out_init)
        aliases = {len(operands) - 1: 0}
    outs, carried = _carried_call(
        kern, comm, name=name, grid=grid, in_specs=in_specs, out_specs=[o_spec], out_shape=[out_shape],
        operands=operands, scratch_shapes=[pltpu.VMEM((bm, bn), F32)] if nk > 1 else [], aliases=aliases,
        sem=("parallel", "parallel", "arbitrary"))
    return outs[0] if comm is None else (outs[0], carried)


def _res_ln_fwd(h_prev, sub, g, b, name):
    t, d = h_prev.shape
    tb = _pick(t, (256, 128, 64, 32, 16))

    def kern(hp_ref, s_ref, g_ref, b_ref, xin_ref, h_ref, hb_ref):
        xin = ALPHA * hp_ref[...] + s_ref[...]
        xhat, _ = _ln_hat(xin)
        h = xhat * g_ref[...] + b_ref[...]
        xin_ref[...] = xin
        h_ref[...] = h
        hb_ref[...] = h.astype(BF16)

    row = pl.BlockSpec((tb, d), lambda i: (i, 0))
    vec = pl.BlockSpec((1, d), lambda i: (0, 0))
    return pl.pallas_call(
        kern, name=name, grid=(t // tb,), in_specs=[row, row, vec, vec], out_specs=[row, row, row],
        out_shape=[jax.ShapeDtypeStruct((t, d), F32), jax.ShapeDtypeStruct((t, d), F32),
                   jax.ShapeDtypeStruct((t, d), BF16)],
        compiler_params=_params(("parallel",)),
    )(h_prev, sub, g, b)


def _ln_bwd(xin, dy_or_target, g, b, name, loss_head=False):
    t, d = xin.shape
    tb = _pick(t, (256, 128, 64, 32, 16))
    nb = t // tb

    def kern(x_ref, dy_ref, g_ref, b_ref, dx_ref, dxb_ref, dg_ref, db_ref, *rest):
        i = pl.program_id(0)
        xhat, rstd = _ln_hat(x_ref[...])
        gv = g_ref[...]
        if loss_head:
            loss_ref = rest[0]
            err = xhat * gv + b_ref[...] - dy_ref[...]
            dy = err * (1.0 / d)
            part = 0.5 * jnp.sum(jnp.sum(err * err, axis=1, keepdims=True), axis=0, keepdims=True) * (1.0 / d)
        else:
            dy = dy_ref[...]

        @pl.when(i == 0)
        def _():
            dg_ref[...] = jnp.zeros_like(dg_ref)
            db_ref[...] = jnp.zeros_like(db_ref)
            if loss_head:
                loss_ref[...] = jnp.zeros_like(loss_ref)

        dg_ref[...] += jnp.sum(dy * xhat, axis=0, keepdims=True)
        db_ref[...] += jnp.sum(dy, axis=0, keepdims=True)
        if loss_head:
            loss_ref[...] += jnp.broadcast_to(part, loss_ref.shape)
        dxh = dy * gv
        m1 = jnp.mean(dxh, axis=-1, keepdims=True)
        m2 = jnp.mean(dxh * xhat, axis=-1, keepdims=True)
        dx = rstd * (dxh - m1 - xhat * m2)
        dx_ref[...] = dx
        dxb_ref[...] = dx.astype(BF16)

    row = pl.BlockSpec((tb, d), lambda i: (i, 0))
    vec = pl.BlockSpec((1, d), lambda i: (0, 0))
    out_specs = [row, row, vec, vec]
    out_shape = [jax.ShapeDtypeStruct((t, d), F32), jax.ShapeDtypeStruct((t, d), BF16),
                 jax.ShapeDtypeStruct((1, d), F32), jax.ShapeDtypeStruct((1, d), F32)]
    if loss_head:
        out_specs.append(pl.BlockSpec((1, LANES), lambda i: (0, 0)))
        out_shape.append(jax.ShapeDtypeStruct((1, LANES), F32))
    return pl.pallas_call(
        kern, name=name, grid=(nb,), in_specs=[row, row, vec, vec], out_specs=out_specs, out_shape=out_shape,
        compiler_params=_params(("arbitrary",)),
    )(xin, dy_or_target, g, b)


def _conv_gate_fwd(u, conv_w, conv_b, name):
    t, f2 = u.shape
    f = f2 // 2
    tb = _pick(t, (512, 256, 128, 64, 32, 16))
    cn = _pick(f, (1408, 1024, 512, 256, 128))
    ncb = f // cn
    hb = tb // 8

    def kern(a_ref, ah_ref, b_ref, w_ref, cb_ref, o_ref):
        i = pl.program_id(0)
        a = a_ref[...]
        halo = jnp.where(i > 0, ah_ref[...], 0.0)
        rid = lax.broadcasted_iota(jnp.int32, a.shape, 0)
        s1 = jnp.where(rid == 0, halo[7:8, :], pltpu.roll(a, 1, 0))
        s2 = jnp.where(rid == 0, halo[6:7, :], jnp.where(rid == 1, halo[7:8, :], pltpu.roll(a, 2, 0)))
        w = w_ref[...]
        conv = w[2:3, :] * a + w[1:2, :] * s1 + w[0:1, :] * s2 + cb_ref[...]
        sp, _ = _sigmoid_pair(conv)
        o_ref[...] = (conv * sp * b_ref[...]).astype(BF16)

    return pl.pallas_call(
        kern, name=name, grid=(t // tb, ncb),
        in_specs=[pl.BlockSpec((tb, cn), lambda i, j: (i, j)),
                  pl.BlockSpec((8, cn), lambda i, j: (jnp.maximum(i * hb - 1, 0), j)),
                  pl.BlockSpec((tb, cn), lambda i, j: (i, j + ncb)),
                  pl.BlockSpec((3, cn), lambda i, j: (0, j)),
                  pl.BlockSpec((1, cn), lambda i, j: (0, j))],
        out_specs=pl.BlockSpec((tb, cn), lambda i, j: (i, j)),
        out_shape=jax.ShapeDtypeStruct((t, f), BF16),
        compiler_params=_params(("parallel", "parallel")),
    )(u, u, u, conv_w, conv_b)


def _conv_gate_bwd(u, dgact, conv_w, conv_b, name):
    t, f2 = u.shape
    f = f2 // 2
    tb = _pick(t, (512, 256, 128, 64, 32, 16))
    cn = _pick(f, (1408, 1024, 512, 256, 128))
    ncb = f // cn
    hb = tb // 8
    nb = t // tb
    last8 = t // 8 - 1

    def kern(a_ref, ap_ref, an_ref, b_ref, bn_ref, dg_ref, dgn_ref, w_ref, cb_ref,
             da_ref, db_ref, dw_ref, dcb_ref):
        i = pl.program_id(1)
        a = a_ref[...]
        w = w_ref[...]
        ext = jnp.concatenate([jnp.where(i > 0, ap_ref[...], 0.0), a, an_ref[...]], axis=0)
        e1 = pltpu.roll(ext, 1, 0)
        e2 = pltpu.roll(ext, 2, 0)
        conv = (w[2:3, :] * ext + w[1:2, :] * e1 + w[0:1, :] * e2 + cb_ref[...])[8:, :]
        bmn = jnp.concatenate([b_ref[...], bn_ref[...]], axis=0)
        dgmn = jnp.concatenate([dg_ref[...], jnp.where(i < nb - 1, dgn_ref[...], 0.0)], axis=0)
        sp, sn = _sigmoid_pair(conv)
        da = dgmn * bmn * (sp * (1.0 + conv * sn))
        n = tb + 8
        dap = w[2:3, :] * da + w[1:2, :] * pltpu.roll(da, n - 1, 0) + w[0:1, :] * pltpu.roll(da, n - 2, 0)
        da_ref[...] = dap[:tb, :].astype(BF16)
        db_ref[...] = (dg_ref[...] * (conv * sp)[:tb, :]).astype(BF16)
        dam = da[:tb, :]

        @pl.when(i == 0)
        def _():
            dw_ref[...] = jnp.zeros_like(dw_ref)
            dcb_ref[...] = jnp.zeros_like(dcb_ref)

        dw = jnp.concatenate([jnp.sum(dam * e2[8:8 + tb, :], axis=0, keepdims=True),
                              jnp.sum(dam * e1[8:8 + tb, :], axis=0, keepdims=True),
                              jnp.sum(dam * a, axis=0, keepdims=True)], axis=0)
        dw_ref[...] += dw
        dcb_ref[...] += jnp.sum(dam, axis=0, keepdims=True)

    main_a = pl.BlockSpec((tb, cn), lambda j, i: (i, j))
    prev_a = pl.BlockSpec((8, cn), lambda j, i: (jnp.maximum(i * hb - 1, 0), j))
    next_a = pl.BlockSpec((8, cn), lambda j, i: (jnp.minimum((i + 1) * hb, last8), j))
    main_b = pl.BlockSpec((tb, cn), lambda j, i: (i, j + ncb))
    next_b = pl.BlockSpec((8, cn), lambda j, i: (jnp.minimum((i + 1) * hb, last8), j + ncb))
    return pl.pallas_call(
        kern, name=name, grid=(ncb, nb),
        in_specs=[main_a, prev_a, next_a, main_b, next_b, main_a, next_a,
                  pl.BlockSpec((3, cn), lambda j, i: (0, j)), pl.BlockSpec((1, cn), lambda j, i: (0, j))],
        out_specs=[main_a, main_a, pl.BlockSpec((3, cn), lambda j, i: (0, j)),
                   pl.BlockSpec((1, cn), lambda j, i: (0, j))],
        out_shape=[jax.ShapeDtypeStruct((t, f), BF16), jax.ShapeDtypeStruct((t, f), BF16),
                   jax.ShapeDtypeStruct((3, f), F32), jax.ShapeDtypeStruct((1, f), F32)],
        compiler_params=_params(("parallel", "arbitrary")),
    )(u, u, u, u, u, dgact, dgact, conv_w, conv_b)


def _hg_gates(qp, fp, lb):
    sq, _ = _sigmoid_pair(qp)
    sf, snf = _sigmoid_pair(fp)
    forget = lb + (1.0 - lb) * sf
    return sq, sf, snf, forget, jnp.log(forget), (1.0 - lb) * snf


def _tri(lower):
    r = lax.broadcasted_iota(jnp.int32, (SUB, SUB), 0)
    c = lax.broadcasted_iota(jnp.int32, (SUB, SUB), 1)
    return ((r >= c) if lower else (r <= c)).astype(BF16)


def _split2(x):
    hi = x.astype(BF16)
    return hi, (x - hi.astype(F32)).astype(BF16)


def _dot3(a, b, dims):
    (ah, al), (bh, bl) = a, b
    return (lax.dot_general(ah, bh, dims, preferred_element_type=F32)
            + (lax.dot_general(ah, bl, dims, preferred_element_type=F32)
               + lax.dot_general(al, bh, dims, preferred_element_type=F32)))


def _running_sum(tri, x):
    hi, lo = _split2(x)
    rest = (x - hi.astype(F32)) - lo.astype(F32)
    return (lax.dot_general(tri, hi, NN, preferred_element_type=F32)
            + (lax.dot_general(tri, lo, NN, preferred_element_type=F32)
               + lax.dot_general(tri, rest.astype(BF16), NN, preferred_element_type=F32)))


HEADS_PER_STEP = 8
STEP_UNROLL = 2


def _hgrn2_fwd(proj, lb_logits, norm_g, name, comm=None):
    t, d4 = proj.shape
    d = d4 // 4
    nh = d // LANES
    hb = _pick(nh, (HEADS_PER_STEP, 2, 1))
    wb = hb * LANES
    tb = _pick(t, (256, 128, 64, 32, 16))
    nb = t // tb
    nsc = tb // SUB

    def kern(q_ref, f_ref, i_ref, g_ref, lbl_ref, ng_ref, y_ref, o_ref, st_ref, s_ref):
        @pl.when(pl.program_id(1) == 0)
        def _():
            s_ref[...] = jnp.zeros_like(s_ref)

        lb_all = _lower_bound(lbl_ref[...])
        ng_all = ng_ref[...]
        ltri = _tri(True)
        rcol = lax.broadcasted_iota(jnp.int32, (SUB, 1), 0)

        heads = [slice(h * LANES, (h + 1) * LANES) for h in range(hb)]

        def step(sc, carry):
            rows = pl.ds(pl.multiple_of(sc * SUB, SUB), SUB)
            qp, fp, v, gp = q_ref[rows, :], f_ref[rows, :], i_ref[rows, :], g_ref[rows, :]
            sq, _, _, _, lf, k = _hg_gates(qp, fp, lb_all)
            q = qp * sq
            bl = _running_sum(ltri, lf)
            bend = bl[SUB - 1:SUB, :]
            dec = jnp.exp(bend)
            qs2 = _split2(q * jnp.exp(bl))
            kd2 = _split2(k * jnp.exp(bend - bl))
            v2 = _split2(v)
            states = [s_ref[h] for h in range(hb)]
            o = [_dot3((qs2[0][:, c], qs2[1][:, c]), _split2(states[h]), NT) for h, c in enumerate(heads)]
            for s in range(SUB):
                e = jnp.exp(jnp.minimum(bl - bl[s:s + 1, :], 0.0))
                p = q * e * k[s:s + 1, :]
                for h, c in enumerate(heads):
                    a = jnp.sum(p[:, c], axis=1, keepdims=True)
                    o[h] = o[h] + jnp.where(rcol >= s, a, 0.0) * v[s:s + 1, c]
            for h, c in enumerate(heads):
                st_ref[sc, h] = states[h]
                s_ref[h] = states[h] * dec[:, c] + _dot3((v2[0][:, c], v2[1][:, c]), (kd2[0][:, c], kd2[1][:, c]), TN)
            o_ref[rows, :] = jnp.concatenate(o, axis=1)
            on = jnp.concatenate(
                [oh * lax.rsqrt(jnp.mean(oh * oh, axis=1, keepdims=True) + RMS_EPS) for oh in o], axis=1)
            sg, _ = _sigmoid_pair(gp)
            y_ref[rows, :] = (on * ng_all * (gp * sg)).astype(BF16)
            return carry

        lax.fori_loop(0, nsc, step, 0, unroll=STEP_UNROLL)

    def col(off):
        return pl.BlockSpec((tb, wb), lambda h, j: (j, h + off * (nh // hb)))

    return _carried_call(
        kern, comm, name=name, grid=(nh // hb, nb),
        in_specs=[col(0), col(1), col(2), col(3),
                  pl.BlockSpec((3, wb), lambda h, j: (0, h)), pl.BlockSpec((1, wb), lambda h, j: (0, h))],
        out_specs=[col(0), col(0), pl.BlockSpec((nsc, hb, LANES, LANES), lambda h, j: (j, h, 0, 0))],
        out_shape=[jax.ShapeDtypeStruct((t, d), BF16), jax.ShapeDtypeStruct((t, d), F32),
                   jax.ShapeDtypeStruct((t // SUB, nh, LANES, LANES), F32)],
        scratch_shapes=[pltpu.VMEM((hb, LANES, LANES), F32)],
        operands=[proj, proj, proj, proj, lb_logits, norm_g], sem=("parallel", "arbitrary"))


def _hgrn2_bwd(proj, lb_logits, norm_g, o_raw, states, dy, name, comm=None):
    t, d4 = proj.shape
    d = d4 // 4
    nh = d // LANES
    hb = _pick(nh, (HEADS_PER_STEP, 2, 1))
    wb = hb * LANES
    tb = _pick(t, (256, 128, 64, 32, 16))
    nb = t // tb
    nsc = tb // SUB

    def kern(q_ref, f_ref, i_ref, g_ref, lbl_ref, ng_ref, o_ref, st_ref, dy_ref,
             dq_ref, df_ref, di_ref, dgp_ref, dlb_ref, dng_ref, ds_ref, gc_ref):
        j = pl.program_id(1)

        @pl.when(j == 0)
        def _():
            ds_ref[...] = jnp.zeros_like(ds_ref)
            gc_ref[...] = jnp.zeros_like(gc_ref)
            dlb_ref[...] = jnp.zeros_like(dlb_ref)
            dng_ref[...] = jnp.zeros_like(dng_ref)

        lb_all = _lower_bound(lbl_ref[...])
        ng_all = ng_ref[...]
        ltri, utri = _tri(True), _tri(False)
        rcol = lax.broadcasted_iota(jnp.int32, (SUB, 1), 0)
        rid = lax.broadcasted_iota(jnp.int32, (SUB, wb), 0)

        heads = [slice(h * LANES, (h + 1) * LANES) for h in range(hb)]

        def per_head(fn):
            return jnp.concatenate([jnp.broadcast_to(fn(c), (SUB, LANES)) for c in heads], axis=1)

        def step(it, carry):
            sc = nsc - 1 - it
            rows = pl.ds(pl.multiple_of(sc * SUB, SUB), SUB)
            qp, fp, v, gp = q_ref[rows, :], f_ref[rows, :], i_ref[rows, :], g_ref[rows, :]
            o, dyv = o_ref[rows, :], dy_ref[rows, :]
            sq, sf, snf, forget, lf, k = _hg_gates(qp, fp, lb_all)
            q = qp * sq
            bl = _running_sum(ltri, lf)
            ebl = jnp.exp(bl)
            bend = bl[SUB - 1:SUB, :]
            dec = jnp.exp(bend)
            dte = jnp.exp(bend - bl)
            r = per_head(lambda c: lax.rsqrt(jnp.mean(o[:, c] * o[:, c], axis=1, keepdims=True) + RMS_EPS))
            ohat = o * r
            sg, sng = _sigmoid_pair(gp)
            don = dyv * (gp * sg)
            dgp_ref[rows, :] = (dyv * (ohat * ng_all) * (sg * (1.0 + gp * sng))).astype(BF16)
            dng_ref[...] += jnp.sum(don * ohat, axis=0, keepdims=True)
            doh = don * ng_all
            dot_oh = doh * ohat
            do = r * (doh - ohat * per_head(lambda c: jnp.mean(dot_oh[:, c], axis=1, keepdims=True)))
            do2, qs2, kd2, v2 = _split2(do), _split2(q * ebl), _split2(k * dte), _split2(v)
            dq_h, dk_h, dv_h = [], [], []
            for h, c in enumerate(heads):
                dstate = ds_ref[h]
                ds2 = _split2(dstate)
                doc = (do2[0][:, c], do2[1][:, c])
                dq_h.append(_dot3(doc, _split2(st_ref[sc, h]), NN))
                dv_h.append(_dot3((kd2[0][:, c], kd2[1][:, c]), ds2, NT))
                dk_h.append(_dot3((v2[0][:, c], v2[1][:, c]), ds2, NN))
                ds_ref[h] = dstate * dec[:, c] + _dot3(doc, (qs2[0][:, c], qs2[1][:, c]), TN)
            dq = jnp.concatenate(dq_h, axis=1) * ebl
            dk = jnp.concatenate(dk_h, axis=1) * dte
            dv = jnp.concatenate(dv_h, axis=1)
            dki = jnp.zeros((SUB, wb), F32)
            dvi = jnp.zeros((SUB, wb), F32)
            for s in range(SUB):
                e = jnp.exp(jnp.minimum(bl - bl[s:s + 1, :], 0.0))
                qe = q * e
                ks = k[s:s + 1, :]
                live = rcol >= s
                pk = qe * ks
                pv = do * v[s:s + 1, :]
                a = per_head(lambda c: jnp.where(live, jnp.sum(pk[:, c], axis=1, keepdims=True), 0.0))
                da = per_head(lambda c: jnp.where(live, jnp.sum(pv[:, c], axis=1, keepdims=True), 0.0))
                dq = dq + da * (e * ks)
                dki = jnp.where(rid == s, jnp.sum(da * qe, axis=0, keepdims=True), dki)
                dvi = jnp.where(rid == s, jnp.sum(a * do, axis=0, keepdims=True), dvi)
            dk = dk + dki
            dv = dv + dvi
            w = q * dq - k * dk
            gc = gc_ref[...]
            dlf = _running_sum(utri, w) + gc
            gc_ref[...] = gc + jnp.sum(w, axis=0, keepdims=True)
            t1 = dlf / forget - dk
            df_ref[rows, :] = ((1.0 - lb_all) * sf * snf * t1).astype(BF16)
            dlb_ref[...] += jnp.sum(snf * t1, axis=0, keepdims=True)
            dq_ref[rows, :] = (dq * (sq * (1.0 + qp * (1.0 - sq)))).astype(BF16)
            di_ref[rows, :] = dv.astype(BF16)
            return carry

        lax.fori_loop(0, nsc, step, 0, unroll=STEP_UNROLL)

    def col(off):
        return pl.BlockSpec((tb, wb), lambda h, j: (nb - 1 - j, h + off * (nh // hb)))

    vec = pl.BlockSpec((1, wb), lambda h, j: (0, h))
    return _carried_call(
        kern, comm, name=name, grid=(nh // hb, nb),
        in_specs=[col(0), col(1), col(2), col(3), pl.BlockSpec((3, wb), lambda h, j: (0, h)), vec,
                  col(0), pl.BlockSpec((nsc, hb, LANES, LANES), lambda h, j: (nb - 1 - j, h, 0, 0)), col(0)],
        out_specs=[col(0), col(0), col(0), col(0), vec, vec],
        out_shape=[jax.ShapeDtypeStruct((t, d), BF16)] * 4 + [jax.ShapeDtypeStruct((1, d), F32)] * 2,
        scratch_shapes=[pltpu.VMEM((hb, LANES, LANES), F32), pltpu.VMEM((1, wb), F32)],
        operands=[proj, proj, proj, proj, lb_logits, norm_g, o_raw, states, dy], sem=("parallel", "arbitrary"))


_INV_SQRT2 = 0.7071067811865476
_INV_SQRT2PI = 0.3989422804014327


def _gelu(x):
    return 0.5 * x * (1.0 + lax.erf(x * _INV_SQRT2))


def _gelu_grad(x):
    return 0.5 * (1.0 + lax.erf(x * _INV_SQRT2)) + x * jnp.exp(-0.5 * x * x) * _INV_SQRT2PI


def _causal(w):
    r = lax.broadcasted_iota(jnp.int32, (GCHUNK, GCHUNK), 0)
    c = lax.broadcasted_iota(jnp.int32, (GCHUNK, GCHUNK), 1)
    return jnp.where(r >= c, w, 0.0)


def _sg_gate_fwd(pre, ln_g, ln_b, w_s, b_s_t, name):
    t, d2 = pre.shape
    d = d2 // 2
    ng = d // LANES

    def kern(pre_ref, g_ref, b_ref, ws_ref, bs_ref, y_ref):
        z = _gelu(pre_ref[...])
        u = z[:, :d]
        vhat, _ = _ln_hat(z[:, d:])
        vn = (vhat * g_ref[...] + b_ref[...]).astype(BF16)
        bs = bs_ref[...]
        for g in range(ng):
            cols = slice(g * LANES, (g + 1) * LANES)
            wc = _causal(ws_ref[g]).astype(BF16)
            gate = jnp.dot(wc, vn[:, cols], preferred_element_type=F32) + bs[:, g:g + 1]
            y_ref[:, cols] = (u[:, cols] * gate).astype(BF16)

    vec = pl.BlockSpec((1, d), lambda i: (0, 0))
    return pl.pallas_call(
        kern, name=name, grid=(t // GCHUNK,),
        in_specs=[pl.BlockSpec((GCHUNK, d2), lambda i: (i, 0)), vec, vec,
                  pl.BlockSpec((ng, GCHUNK, GCHUNK), lambda i: (0, 0, 0)),
                  pl.BlockSpec((GCHUNK, ng), lambda i: (0, 0))],
        out_specs=pl.BlockSpec((GCHUNK, d), lambda i: (i, 0)),
        out_shape=jax.ShapeDtypeStruct((t, d), BF16),
        compiler_params=_params(("parallel",)),
    )(pre, ln_g, ln_b, w_s, b_s_t)


def _sg_gate_bwd(pre, dy, ln_g, ln_b, w_s, b_s_t, name):
    t, d2 = pre.shape
    d = d2 // 2
    ng = d // LANES

    def kern(pre_ref, dy_ref, g_ref, b_ref, ws_ref, bs_ref, dpre_ref, dws_ref, dbs_ref, dg_ref, db_ref, dvn_ref):
        @pl.when(pl.program_id(0) == 0)
        def _():
            dws_ref[...] = jnp.zeros_like(dws_ref)
            dbs_ref[...] = jnp.zeros_like(dbs_ref)
            dg_ref[...] = jnp.zeros_like(dg_ref)
            db_ref[...] = jnp.zeros_like(db_ref)

        pre = pre_ref[...]
        z = _gelu(pre)
        u = z[:, :d]
        vhat, rstd = _ln_hat(z[:, d:])
        gv = g_ref[...]
        vn = (vhat * gv + b_ref[...]).astype(BF16)
        bs = bs_ref[...]
        dyv = dy_ref[...]
        gp = _gelu_grad(pre)
        lane = lax.broadcasted_iota(jnp.int32, (GCHUNK, ng), 1)
        dbs = jnp.zeros((GCHUNK, ng), F32)
        for g in range(ng):
            cols = slice(g * LANES, (g + 1) * LANES)
            wc = _causal(ws_ref[g]).astype(BF16)
            vng = vn[:, cols]
            gate = jnp.dot(wc, vng, preferred_element_type=F32) + bs[:, g:g + 1]
            dpre_ref[:, cols] = (dyv[:, cols] * gate * gp[:, cols]).astype(BF16)
            dgate = dyv[:, cols] * u[:, cols]
            dbs = dbs + jnp.where(lane == g, jnp.sum(dgate, axis=1, keepdims=True), 0.0)
            dgb = dgate.astype(BF16)
            dws_ref[g] += _causal(lax.dot_general(dgb, vng, NT, preferred_element_type=F32))
            dvn_ref[:, cols] = lax.dot_general(wc, dgb, TN, preferred_element_type=F32)
        dbs_ref[...] += dbs
        dvn = dvn_ref[...]
        dg_ref[...] += jnp.sum(dvn * vhat, axis=0, keepdims=True)
        db_ref[...] += jnp.sum(dvn, axis=0, keepdims=True)
        dvh = dvn * gv
        m1 = jnp.mean(dvh, axis=-1, keepdims=True)
        m2 = jnp.mean(dvh * vhat, axis=-1, keepdims=True)
        dpre_ref[:, d:] = (rstd * (dvh - m1 - vhat * m2) * gp[:, d:]).astype(BF16)

    vec = pl.BlockSpec((1, d), lambda i: (0, 0))
    wsp = pl.BlockSpec((ng, GCHUNK, GCHUNK), lambda i: (0, 0, 0))
    bsp = pl.BlockSpec((GCHUNK, ng), lambda i: (0, 0))
    return pl.pallas_call(
        kern, name=name, grid=(t // GCHUNK,),
        in_specs=[pl.BlockSpec((GCHUNK, d2), lambda i: (i, 0)), pl.BlockSpec((GCHUNK, d), lambda i: (i, 0)),
                  vec, vec, wsp, bsp],
        out_specs=[pl.BlockSpec((GCHUNK, d2), lambda i: (i, 0)), wsp, bsp, vec, vec],
        out_shape=[jax.ShapeDtypeStruct((t, d2), BF16), jax.ShapeDtypeStruct((ng, GCHUNK, GCHUNK), F32),
                   jax.ShapeDtypeStruct((GCHUNK, ng), F32), jax.ShapeDtypeStruct((1, d), F32),
                   jax.ShapeDtypeStruct((1, d), F32)],
        scratch_shapes=[pltpu.VMEM((GCHUNK, d), F32)],
        compiler_params=_params(("arbitrary",)),
    )(pre, dy, ln_g, ln_b, w_s, b_s_t)


def _adamw_math(w, g, m, v):
    m = ADAM_B1 * m + (1.0 - ADAM_B1) * g
    v = ADAM_B2 * v + (1.0 - ADAM_B2) * (g * g)
    m_hat = m / (1.0 - ADAM_B1 ** ADAM_STEP)
    v_hat = v / (1.0 - ADAM_B2 ** ADAM_STEP)
    return -ADAM_LR * (m_hat / (jnp.sqrt(v_hat) + ADAM_EPS) + ADAM_WD * w), m, v


ADAMW_BLOCK_BYTES = 3 << 19


def _adamw(w, gs, m, v, name, comm=None):
    nl, r, c = w.shape
    rb = _pick(r, tuple(p for p in (512, 256, 128, 64, 32, 16, 8) if p * c * 4 <= ADAMW_BLOCK_BYTES))

    def kern(w_ref, m_ref, v_ref, *rest):
        g_refs, (d_ref, mo_ref, vo_ref, go_ref) = rest[:nl], rest[nl:]
        layer = pl.program_id(0)
        g = g_refs[0][...]
        for k in range(1, nl):
            g = jnp.where(layer == k, g_refs[k][...], g)
        dlt, mm, vv = _adamw_math(w_ref[...], g, m_ref[...], v_ref[...])
        d_ref[...] = dlt
        mo_ref[...] = mm
        vo_ref[...] = vv
        go_ref[...] = g

    blk = pl.BlockSpec((None, rb, c), lambda l, i: (l, i, 0))
    g_specs = [pl.BlockSpec((rb, c), lambda l, i, k=k: (jnp.where(l == k, i, 0), 0)) for k in range(nl)]
    outs, carried = _carried_call(
        kern, comm, name=name, grid=(nl, r // rb), in_specs=[blk] * 3 + g_specs, out_specs=[blk] * 4,
        out_shape=[jax.ShapeDtypeStruct((nl, r, c), F32)] * 4, operands=[w, m, v, *gs], sem=("parallel", "parallel"))
    return outs if comm is None else (outs, carried)


def _lb_logits_grad(lb_logits, dlb, name):
    def kern(l_ref, d_ref, o_ref):
        lg = l_ref[...]
        m = jnp.max(lg, axis=0, keepdims=True)
        e = jnp.exp(lg - m)
        p = e / jnp.sum(e, axis=0, keepdims=True)
        row = lax.broadcasted_iota(jnp.int32, lg.shape, 0)
        o_ref[...] = d_ref[...] * p[0:1, :] * (jnp.where(row == 0, 1.0, 0.0) - p)

    return pl.pallas_call(kern, name=name, out_shape=jax.ShapeDtypeStruct(lb_logits.shape, F32))(lb_logits, dlb)


def _sum_leading(a, name):
    n, r, c = a.shape
    rb = _pick(r, (512, 256, 128, 64, 32, 16, 8))

    def kern(a_ref, o_ref):
        acc = a_ref[0]
        for i in range(1, n):
            acc = acc + a_ref[i]
        o_ref[...] = acc

    return pl.pallas_call(
        kern, name=name, grid=(r // rb,), in_specs=[pl.BlockSpec((n, rb, c), lambda i: (0, i, 0))],
        out_specs=pl.BlockSpec((rb, c), lambda i: (i, 0)), out_shape=jax.ShapeDtypeStruct((r, c), F32),
        compiler_params=_params(("parallel",)),
    )(a)


def _place():
    x, y, c = lax.axis_index("x"), lax.axis_index("y"), lax.axis_index("c")
    return x, y, c


class _Plan:
    def __init__(self, ins, out_shapes, aliases, n_sems, build):
        self.ins, self.out_shapes, self.aliases, self.n_sems, self.build = list(ins), list(out_shapes), aliases, n_sems, build


def _merge(*plans):
    ins, outs, aliases, subs, sems = [], [], {}, [], 0
    for p in plans:
        for k, v in p.aliases.items():
            aliases[len(ins) + k] = len(outs) + v
        subs.append((p, len(ins), len(outs), sems))
        ins += p.ins
        outs += p.out_shapes
        sems += p.n_sems

    def build(in_refs, out_refs, send_sems, recv_sems, base):
        copies = []
        for p, i0, o0, s0 in subs:
            copies += p.build(in_refs[i0:i0 + len(p.ins)], out_refs[o0:o0 + len(p.out_shapes)], send_sems, recv_sems,
                              base + s0)
        return copies

    return _Plan(ins, outs, aliases, sems, build)


def _remote(src, dst, send_sems, recv_sems, k, to):
    return pltpu.make_async_remote_copy(src_ref=src, dst_ref=dst, send_sem=send_sems.at[k], recv_sem=recv_sems.at[k],
                                        device_id=to, device_id_type=MESH)


def _plan_gather_ici(shards):
    n = len(shards)

    def build(ins, outs, send_sems, recv_sems, base):
        x, y, c = _place()
        me = 2 * x + y
        copies = []
        for a in range(n):
            h = ins[a].shape[0] // 2
            rows = pl.ds(c * h, h)
            for r in (1, 2, 3):
                px, py, _ = _chip_rel(x, y, r)
                copies.append(_remote(ins[a].at[rows, :], outs[a].at[me, rows, :], send_sems, recv_sems,
                                      base + 4 * a + r - 1, (px, py, c)))
            copies.append(_remote(ins[a], outs[a].at[me], send_sems, recv_sems, base + 4 * a + 3, (x, y, 1 - c)))
        return copies

    return _Plan(shards, [jax.ShapeDtypeStruct((N_CHIPS,) + s.shape, s.dtype) for s in shards], {}, 4 * n, build)


def _plan_gather_pass(gathered):
    n = len(gathered)

    def build(ins, outs, send_sems, recv_sems, base):
        x, y, c = _place()
        copies = []
        for a in range(n):
            h = outs[a].shape[1] // 2
            rows = pl.ds(c * h, h)
            for r in (1, 2, 3):
                _, _, shard = _chip_rel(x, y, r)
                piece = outs[a].at[shard, rows, :]
                copies.append(_remote(piece, piece, send_sems, recv_sems, base + 3 * a + r - 1, (x, y, 1 - c)))
        return copies

    return _Plan(gathered, [jax.ShapeDtypeStruct(g.shape, g.dtype) for g in gathered], {a: a for a in range(n)},
                 3 * n, build)


def _plan_swap(split):
    n = len(split)

    def build(ins, outs, send_sems, recv_sems, base):
        x, y, c = _place()
        return [_remote(ins[a].at[j, 1 - c], outs[a].at[j], send_sems, recv_sems, base + N_CHIPS * a + j, (x, y, 1 - c))
                for a in range(n) for j in range(N_CHIPS)]

    return _Plan(split, [jax.ShapeDtypeStruct((N_CHIPS,) + g.shape[2:], g.dtype) for g in split], {}, N_CHIPS * n, build)


def _plan_scatter(parts):
    n = len(parts)

    def build(ins, outs, send_sems, recv_sems, base):
        x, y, c = _place()
        copies = []
        for a in range(n):
            for r in (1, 2, 3):
                px, py, shard = _chip_rel(x, y, r)
                copies.append(_remote(ins[a].at[shard], outs[a].at[r - 1], send_sems, recv_sems, base + 3 * a + r - 1,
                                      (px, py, c)))
        return copies

    return _Plan(parts, [jax.ShapeDtypeStruct((3,) + p.shape[1:], p.dtype) for p in parts], {}, 3 * n, build)


def _plan_join(bufs):
    n = len(bufs)

    def build(ins, outs, send_sems, recv_sems, base):
        x, y, c = _place()
        return [_remote(outs[a].at[c], outs[a].at[c], send_sems, recv_sems, base + a, (x, y, 1 - c)) for a in range(n)]

    return _Plan(bufs, [jax.ShapeDtypeStruct(b.shape, b.dtype) for b in bufs], {a: a for a in range(n)}, n, build)


def _carried_call(kern, plan, *, name, grid, in_specs, out_specs, out_shape, operands, scratch_shapes=(),
                  aliases=None, sem=None):
    n_in, n_out, n_sc = len(operands), len(out_shape), len(scratch_shapes)
    aliases = dict(aliases or {})
    if plan is None:
        outs = pl.pallas_call(kern, name=name, grid=grid, in_specs=in_specs, out_specs=out_specs, out_shape=out_shape,
                              scratch_shapes=list(scratch_shapes), input_output_aliases=aliases,
                              compiler_params=_params(sem))(*operands)
        return list(outs), []
    ci, co = len(plan.ins), len(plan.out_shapes)
    for k, v in plan.aliases.items():
        aliases[n_in + k] = n_out + v
    steps = tuple(grid)

    def body(*refs):
        ins, cins = refs[:n_in], refs[n_in:n_in + ci]
        outs = refs[n_in + ci:n_in + ci + n_out]
        couts = refs[n_in + ci + n_out:n_in + ci + n_out + co]
        scratch = refs[n_in + ci + n_out + co:n_in + ci + n_out + co + n_sc]
        send_sems, recv_sems = refs[-2], refs[-1]
        first = functools.reduce(jnp.logical_and, [pl.program_id(a) == 0 for a in range(len(steps))])
        last = functools.reduce(jnp.logical_and, [pl.program_id(a) == steps[a] - 1 for a in range(len(steps))])

        @pl.when(first)
        def _():
            for cp in plan.build(cins, couts, send_sems, recv_sems, 0):
                cp.start()

        kern(*ins, *outs, *scratch)

        @pl.when(last)
        def _():
            for cp in plan.build(cins, couts, send_sems, recv_sems, 0):
                cp.wait()

    anyspec = pl.BlockSpec(memory_space=pl.ANY)
    outs = pl.pallas_call(
        body, name=name, grid=grid, in_specs=list(in_specs) + [anyspec] * ci,
        out_specs=list(out_specs) + [anyspec] * co, out_shape=list(out_shape) + plan.out_shapes,
        scratch_shapes=list(scratch_shapes) + [pltpu.SemaphoreType.DMA((plan.n_sems,)),
                                               pltpu.SemaphoreType.DMA((plan.n_sems,))],
        input_output_aliases=aliases,
        compiler_params=_params(("arbitrary",) * len(steps)),
    )(*operands, *plan.ins)
    return list(outs[:n_out]), list(outs[n_out:])


def _chip_rel(x, y, r):
    px = x if r < 2 else 1 - x
    py = y if r % 2 == 0 else 1 - y
    return px, py, 2 * px + py


def _allgather_split(arrs, name):
    n = len(arrs)
    slots = 7

    def body(*refs):
        ins, outs = refs[:n], refs[n:2 * n]
        send_sems, recv_sems = refs[2 * n:]
        x, y, c = _place()
        me = 2 * x + y
        sib = (x, y, 1 - c)

        def half(a, shard, hc):
            h = ins[a].shape[0] // 2
            return outs[a].at[shard, pl.ds(hc * h, h), :]

        def src_half(a):
            h = ins[a].shape[0] // 2
            return ins[a].at[pl.ds(c * h, h), :]

        def copy(a, slot, src, dst, to):
            return pltpu.make_async_remote_copy(src_ref=src, dst_ref=dst, send_sem=send_sems.at[a * slots + slot],
                                                recv_sem=recv_sems.at[a * slots + slot], device_id=to,
                                                device_id_type=MESH)

        first = []
        for r in (1, 2, 3):
            px, py, _ = _chip_rel(x, y, r)
            for a in range(n):
                first.append(copy(a, r - 1, src_half(a), half(a, me, c), (px, py, c)))
        own = [copy(a, 6, ins[a], outs[a].at[me], sib) for a in range(n)]
        for cp in first + own:
            cp.start()
        passed = []
        for r in (1, 2, 3):
            _, _, shard = _chip_rel(x, y, r)
            for a in range(n):
                copy(a, r - 1, src_half(a), half(a, shard, c), sib).wait_recv()
                cp = copy(a, 3 + r - 1, half(a, shard, c), half(a, shard, c), sib)
                cp.start()
                passed.append(cp)
        for r in (1, 2, 3):
            _, _, shard = _chip_rel(x, y, r)
            for a in range(n):
                copy(a, 3 + r - 1, src_half(a), half(a, shard, 1 - c), sib).wait_recv()
        for cp in own:
            cp.wait_recv()
        for cp in first + passed + own:
            cp.wait_send()

    anyspec = pl.BlockSpec(memory_space=pl.ANY)
    return pl.pallas_call(
        body, name=name, in_specs=[anyspec] * n, out_specs=[anyspec] * n,
        out_shape=[jax.ShapeDtypeStruct((N_CHIPS,) + a.shape, a.dtype) for a in arrs],
        scratch_shapes=[pltpu.SemaphoreType.DMA((slots * n,)), pltpu.SemaphoreType.DMA((slots * n,))],
        compiler_params=pltpu.CompilerParams(has_side_effects=True),
    )(*arrs)


def _allgather_whole(arr, name):
    def body(in_ref, out_ref, send_sems, recv_sems, loc_sem):
        x, y, c = _place()
        me = 2 * x + y
        local = pltpu.make_async_copy(in_ref, out_ref.at[me], loc_sem)
        local.start()
        sends = []
        for r in (1, 2, 3):
            px, py, _ = _chip_rel(x, y, r)
            sends.append(pltpu.make_async_remote_copy(
                src_ref=in_ref, dst_ref=out_ref.at[me], send_sem=send_sems.at[r - 1], recv_sem=recv_sems.at[r - 1],
                device_id=(px, py, c), device_id_type=MESH))
        for cp in sends:
            cp.start()
        for r in (1, 2, 3):
            px, py, shard = _chip_rel(x, y, r)
            pltpu.make_async_remote_copy(
                src_ref=in_ref, dst_ref=out_ref.at[shard], send_sem=send_sems.at[r - 1], recv_sem=recv_sems.at[r - 1],
                device_id=(px, py, c), device_id_type=MESH).wait_recv()
        for cp in sends:
            cp.wait_send()
        local.wait()

    anyspec = pl.BlockSpec(memory_space=pl.ANY)
    return pl.pallas_call(
        body, name=name, in_specs=[anyspec], out_specs=anyspec,
        out_shape=jax.ShapeDtypeStruct((N_CHIPS,) + arr.shape, arr.dtype),
        scratch_shapes=[pltpu.SemaphoreType.DMA((3,)), pltpu.SemaphoreType.DMA((3,)), pltpu.SemaphoreType.DMA],
        compiler_params=pltpu.CompilerParams(has_side_effects=True),
    )(arr)


def _gather_all_devices(buf, name):
    def body(in_ref, out_ref, send_sems, recv_sems, loc_sem):
        x, y, c = _place()
        me = 4 * x + 2 * y + c
        local = pltpu.make_async_copy(in_ref, out_ref.at[me], loc_sem)
        local.start()
        sends = []
        for r in range(1, N_DEV):
            px, py, _ = _chip_rel(x, y, r // 2)
            pc = c if r % 2 == 0 else 1 - c
            sends.append(pltpu.make_async_remote_copy(
                src_ref=in_ref, dst_ref=out_ref.at[me], send_sem=send_sems.at[r - 1], recv_sem=recv_sems.at[r - 1],
                device_id=(px, py, pc), device_id_type=MESH))
        for cp in sends:
            cp.start()
        for r in range(1, N_DEV):
            px, py, _ = _chip_rel(x, y, r // 2)
            pc = c if r % 2 == 0 else 1 - c
            pltpu.make_async_remote_copy(
                src_ref=in_ref, dst_ref=out_ref.at[4 * px + 2 * py + pc], send_sem=send_sems.at[r - 1],
                recv_sem=recv_sems.at[r - 1], device_id=(px, py, pc), device_id_type=MESH).wait_recv()
        for cp in sends:
            cp.wait_send()
        local.wait()

    anyspec = pl.BlockSpec(memory_space=pl.ANY)
    return pl.pallas_call(
        body, name=name, in_specs=[anyspec], out_specs=anyspec,
        out_shape=jax.ShapeDtypeStruct((N_DEV,) + buf.shape, buf.dtype),
        scratch_shapes=[pltpu.SemaphoreType.DMA((N_DEV - 1,)), pltpu.SemaphoreType.DMA((N_DEV - 1,)),
                        pltpu.SemaphoreType.DMA],
        compiler_params=pltpu.CompilerParams(has_side_effects=True),
    )(buf)


def _add_half(grad, recv, sel, name):
    _, _, rh, cw = grad.shape
    rb = _pick(rh, (512, 256, 176, 128, 64, 32, 16, 8))

    def kern(sel_ref, g_ref, r_ref, o_ref):
        o_ref[...] = (g_ref[...] + r_ref[...]).astype(BF16)

    return pl.pallas_call(
        kern, name=name,
        grid_spec=pltpu.PrefetchScalarGridSpec(
            num_scalar_prefetch=1, grid=(N_CHIPS, rh // rb),
            in_specs=[pl.BlockSpec((None, None, rb, cw), lambda j, i, s: (j, s[0], i, 0)),
                      pl.BlockSpec((None, rb, cw), lambda j, i, s: (j, i, 0))],
            out_specs=pl.BlockSpec((None, rb, cw), lambda j, i, s: (j, i, 0))),
        out_shape=jax.ShapeDtypeStruct((N_CHIPS, rh, cw), BF16),
        compiler_params=_params(("parallel", "parallel")),
    )(sel, grad, recv)


def _add_own(grad, recv, got, sel, name):
    _, _, rh, cw = grad.shape
    rb = _pick(rh, (512, 256, 176, 128, 64, 32, 16, 8))

    def kern(sel_ref, g_ref, r_ref, b_ref, o_ref):
        own = g_ref[...] + r_ref[...]
        o_ref[...] = ((own + b_ref[0].astype(F32)) + b_ref[1].astype(F32)) + b_ref[2].astype(F32)

    return pl.pallas_call(
        kern, name=name,
        grid_spec=pltpu.PrefetchScalarGridSpec(
            num_scalar_prefetch=1, grid=(rh // rb,),
            in_specs=[pl.BlockSpec((None, None, rb, cw), lambda i, s: (s[1], s[0], i, 0)),
                      pl.BlockSpec((None, rb, cw), lambda i, s: (s[1], i, 0)),
                      pl.BlockSpec((3, rb, cw), lambda i, s: (0, i, 0))],
            out_specs=pl.BlockSpec((None, rb, cw), lambda i, s: (s[0], i, 0))),
        out_shape=jax.ShapeDtypeStruct((2, rh, cw), F32),
        compiler_params=_params(("parallel",)),
    )(sel, grad, recv, got)


def _stacked(g):
    return g.reshape(1, g.shape[0] * g.shape[1], g.shape[2])


def _halves(g):
    g = g.reshape(N_CHIPS, g.shape[0] * g.shape[1] // N_CHIPS, g.shape[2])
    return g.reshape(N_CHIPS, 2, g.shape[1] // 2, g.shape[2])


def _whole(f):
    return f.reshape(f.shape[0] * f.shape[1], f.shape[2])


def _step(x2, tgt, xb, sh, sm, w, mom, var):
    x, y, c = _place()
    sel = jnp.stack([c, 2 * x + y]).astype(jnp.int32)
    wg = {}
    wg["hg_in"] = _allgather_split([sh["hg_in"]], "gather_hg_in")[0]
    proj, landed = _matmul(xb, wg["hg_in"], mode="nn", nsh=N_CHIPS, name="hg_in",
                           comm=_plan_gather_ici([sh["hg_out"], sh["dn0"]]))
    (yhg, o_raw, states), got = _hgrn2_fwd(
        proj, sm["lb_logits"], sm["hg_norm_g"], "hgrn2_fwd",
        comm=_merge(_plan_gather_pass(landed), _plan_gather_ici([sh[k] for k in ("up0", "sg_in", "sg_out", "dn1")])))
    wg["hg_out"], wg["dn0"], landed = got[0], got[1], got[2:]
    mixed, got = _matmul(yhg, _stacked(wg["hg_out"]), mode="nn", nsh=1, name="hg_out", comm=_plan_gather_pass(landed))
    wg["up0"], wg["sg_in"], wg["sg_out"], wg["dn1"] = got
    xin1, h1, h1b = _res_ln_fwd(x2, mixed, sm["ln1_g"][0:1], sm["ln1_b"][0:1], "l0_ln1")
    u0, landed = _matmul(h1b, wg["up0"], mode="nn", nsh=N_CHIPS, name="l0_ffn_up", comm=_plan_gather_ici([sh["up1"]]))
    gact0 = _conv_gate_fwd(u0, sm["conv_w"][0], sm["conv_b"][0:1], "l0_ffn_gate")
    ffn, got = _matmul(gact0, _stacked(wg["dn0"]), mode="nn", nsh=1, name="l0_ffn_down", comm=_plan_gather_pass(landed))
    wg["up1"] = got[0]
    xin2, h2, h2b = _res_ln_fwd(h1, ffn, sm["ln2_g"][0:1], sm["ln2_b"][0:1], "l0_ffn_ln")
    pre = _matmul(h2b, wg["sg_in"], mode="nn", nsh=N_CHIPS, name="sg_in")
    ysg = _sg_gate_fwd(pre, sm["sg_ln_g"], sm["sg_ln_b"], sm["sg_w_s"], sm["sg_b_s_t"], "sg_gate")
    mixed = _matmul(ysg, _stacked(wg["sg_out"]), mode="nn", nsh=1, name="sg_out")
    xin3, h3, h3b = _res_ln_fwd(h2, mixed, sm["ln1_g"][1:2], sm["ln1_b"][1:2], "l1_ln1")
    u1 = _matmul(h3b, wg["up1"], mode="nn", nsh=N_CHIPS, name="l1_ffn_up")
    gact1 = _conv_gate_fwd(u1, sm["conv_w"][1], sm["conv_b"][1:2], "l1_ffn_gate")
    ffn = _matmul(gact1, _stacked(wg["dn1"]), mode="nn", nsh=1, name="l1_ffn_down")
    xin4, _, _ = _res_ln_fwd(h3, ffn, sm["ln2_g"][1:2], sm["ln2_b"][1:2], "l1_ffn_ln")

    gs, grad = {}, {}

    def ffn_bwd(u, gact, hb_in, dxin, dxin_b, w_up, w_down, layer, tag, comm_dup_a=None):
        dgact = _matmul(dxin_b, _stacked(w_down), mode="nt", nsh=1, name=f"{tag}_ddown")
        g_down = _matmul(gact, dxin_b, mode="tn", nsh=1, out_shards=1, name=f"{tag}_wdown")
        da, db, dcw, dcb = _conv_gate_bwd(u, dgact, sm["conv_w"][layer], sm["conv_b"][layer:layer + 1], f"{tag}_dgate")
        carried = None
        dh = _matmul(da, w_up, mode="nt", nsh=2, b_off=0, resid=dxin, alpha=ALPHA, name=f"{tag}_dup_a", comm=comm_dup_a)
        if comm_dup_a is not None:
            dh, carried = dh
        dh = _matmul(db, w_up, mode="nt", nsh=2, b_off=2, resid=dh, alpha=1.0, name=f"{tag}_dup_b")
        g_up = _matmul(hb_in, da, mode="tn", nsh=2, out_off=0, out_shards=N_CHIPS, name=f"{tag}_wup_a")
        g_up = _matmul(hb_in, db, mode="tn", nsh=2, out_off=2, out_shards=N_CHIPS, out_init=g_up, name=f"{tag}_wup_b")
        return dh, g_up, g_down, dcw, dcb, carried

    dx, dxb, dg4, db4, loss = _ln_bwd(xin4, tgt, sm["ln2_g"][1:2], sm["ln2_b"][1:2], "l1_ln2_bwd", loss_head=True)
    dh3, grad["up1"], grad["dn1"], dcw1, dcb1, _ = ffn_bwd(u1, gact1, h3b, dx, dxb, wg["up1"], wg["dn1"], 1, "l1_ffn")
    dx, dxb, dg3, db3 = _ln_bwd(xin3, dh3, sm["ln1_g"][1:2], sm["ln1_b"][1:2], "l1_ln1_bwd")
    dysg = _matmul(dxb, _stacked(wg["sg_out"]), mode="nt", nsh=1, name="sg_dout")
    grad["sg_out"] = _matmul(ysg, dxb, mode="tn", nsh=1, out_shards=1, name="sg_wout")
    dpre, gs["sg_w_s"], gs["sg_b_s_t"], gs["sg_ln_g"], gs["sg_ln_b"] = _sg_gate_bwd(
        pre, dysg, sm["sg_ln_g"], sm["sg_ln_b"], sm["sg_w_s"], sm["sg_b_s_t"], "sg_gate_bwd")
    split = {k: _halves(grad[k]) for k in ("dn1", "up1", "sg_out")}
    recv = {}
    dh2, got = _matmul(dpre, wg["sg_in"], mode="nt", nsh=N_CHIPS, resid=dx, alpha=ALPHA, name="sg_din",
                       comm=_plan_swap([split[k] for k in ("dn1", "up1", "sg_out")]))
    recv["dn1"], recv["up1"], recv["sg_out"] = got
    grad["sg_in"] = _matmul(h2b, dpre, mode="tn", nsh=N_CHIPS, out_shards=N_CHIPS, name="sg_win")
    part = {k: _add_half(split[k], recv[k], sel, f"rs_addhalf_{k}") for k in ("dn1", "up1", "sg_out")}
    dx, dxb, dg2, db2 = _ln_bwd(xin2, dh2, sm["ln2_g"][0:1], sm["ln2_b"][0:1], "l0_ln2_bwd")
    split["sg_in"] = _halves(grad["sg_in"])
    dh1, grad["up0"], grad["dn0"], dcw0, dcb0, got = ffn_bwd(
        u0, gact0, h1b, dx, dxb, wg["up0"], wg["dn0"], 0, "l0_ffn", comm_dup_a=_plan_swap([split["sg_in"]]))
    recv["sg_in"] = got[0]
    dx, dxb, dg1, db1 = _ln_bwd(xin1, dh1, sm["ln1_g"][0:1], sm["ln1_b"][0:1], "l0_ln1_bwd")
    grad["hg_out"] = _matmul(yhg, dxb, mode="tn", nsh=1, out_shards=1, name="hg_wout")
    for k in ("dn0", "up0", "hg_out"):
        split[k] = _halves(grad[k])
    dyhg, got = _matmul(dxb, _stacked(wg["hg_out"]), mode="nt", nsh=1, name="hg_dout",
                        comm=_plan_swap([split[k] for k in ("dn0", "up0", "hg_out")]))
    recv["dn0"], recv["up0"], recv["hg_out"] = got
    for k in ("sg_in", "dn0", "up0", "hg_out"):
        part[k] = _add_half(split[k], recv[k], sel, f"rs_addhalf_{k}")
    early = ("dn1", "up1", "sg_out", "sg_in", "dn0", "up0", "hg_out")
    dparts, got = _hgrn2_bwd(proj, sm["lb_logits"], sm["hg_norm_g"], o_raw, states, dyhg, "hgrn2_bwd",
                             comm=_plan_scatter([part[k] for k in early]))
    gs["lb"], gs["hg_norm_g"] = dparts[4], dparts[5]
    mine = [_add_own(split[k], recv[k], b, sel, f"rs_addown_{k}") for k, b in zip(early, got)]
    g_in = None
    for j in range(4):
        g_in = _matmul(xb, dparts[j], mode="tn", nsh=1, out_off=j, out_shards=N_CHIPS, out_init=g_in, name=f"hg_win{j}")
    split["hg_in"] = _halves(g_in)
    gx, full = _matmul(dparts[0], wg["hg_in"], mode="nt", nsh=1, b_off=0, resid=dx, alpha=ALPHA, name="hg_din0",
                       comm=_plan_join(mine))
    red = {k: _whole(f) for k, f in zip(early, full)}
    gx, got = _matmul(dparts[1], wg["hg_in"], mode="nt", nsh=1, b_off=1, resid=gx, alpha=1.0, name="hg_din1",
                      comm=_plan_swap([split["hg_in"]]))
    recv["hg_in"] = got[0]
    gx = _matmul(dparts[2], wg["hg_in"], mode="nt", nsh=1, b_off=2, resid=gx, alpha=1.0, name="hg_din2")
    gx = _matmul(dparts[3], wg["hg_in"], mode="nt", nsh=1, b_off=3, resid=gx, alpha=1.0, name="hg_din3")
    part["hg_in"] = _add_half(split["hg_in"], recv["hg_in"], sel, "rs_addhalf_hg_in")
    upd = {}
    upd["ffn_w_up"], got = _adamw(w["ffn_w_up"], [red["up0"], red["up1"]], mom["ffn_w_up"], var["ffn_w_up"],
                                  "adamw_ffn_w_up", comm=_plan_scatter([part["hg_in"]]))
    mine = _add_own(split["hg_in"], recv["hg_in"], got[0], sel, "rs_addown_hg_in")
    upd["ffn_w_down"], full = _adamw(w["ffn_w_down"], [red["dn0"], red["dn1"]], mom["ffn_w_down"], var["ffn_w_down"],
                                     "adamw_ffn_w_down", comm=_plan_join([mine]))
    red["hg_in"] = _whole(full[0])
    for k, src in (("hg_w_out", "hg_out"), ("sg_w_in", "sg_in"), ("sg_w_out", "sg_out"), ("hg_w_in", "hg_in")):
        upd[k] = _adamw(w[k], [red[src]], mom[k], var[k], f"adamw_{k}")
    gs["ln1_g"] = jnp.concatenate([dg1, dg3], axis=0)
    gs["ln1_b"] = jnp.concatenate([db1, db3], axis=0)
    gs["ln2_g"] = jnp.concatenate([dg2, dg4], axis=0)
    gs["ln2_b"] = jnp.concatenate([db2, db4], axis=0)
    gs["conv_w"] = jnp.stack([dcw0, dcw1], axis=0)
    gs["conv_b"] = jnp.concatenate([dcb0, dcb1], axis=0)
    return loss, gx, gs, upd


_SMALL_ORDER = ("lb", "hg_norm_g", "sg_w_s", "sg_b_s_t", "conv_b", "ln1_g", "ln1_b", "ln2_g", "ln2_b",
                "conv_w", "sg_ln_g", "sg_ln_b")


PACK_ROWS = 512


def _pack(parts):
    flat, layout, off = [], [], 0
    for k in _SMALL_ORDER:
        a = parts[k]
        n = a.size
        pad = (-n) % LANES
        flat.append(jnp.pad(a.reshape(-1), (0, pad)))
        layout.append((k, off, n, a.shape))
        off += n + pad
    flat.append(jnp.zeros(((-off) % (PACK_ROWS * LANES),), F32))
    return jnp.concatenate(flat).reshape(-1, LANES), layout


def _unpack(buf, layout):
    flat = buf.reshape(-1)
    return {k: flat[off:off + n].reshape(shape) for k, off, n, shape in layout}


def kernel(x, lb_logits, hg_w_in, hg_norm_g, hg_w_out, sg_w_in, sg_ln_g, sg_ln_b, sg_w_s, sg_b_s, sg_w_out, ffn_w_up, ffn_conv_w, ffn_conv_b, ffn_w_down, ln1_g, ln1_b, ln2_g, ln2_b, loss_target, m_lb_logits, m_hg_w_in, m_hg_norm_g, m_hg_w_out, m_sg_w_in, m_sg_ln_g, m_sg_ln_b, m_sg_w_s, m_sg_b_s, m_sg_w_out, m_ffn_w_up, m_ffn_conv_w, m_ffn_conv_b, m_ffn_w_down, m_ln1_g, m_ln1_b, m_ln2_g, m_ln2_b, v_lb_logits, v_hg_w_in, v_hg_norm_g, v_hg_w_out, v_sg_w_in, v_sg_ln_g, v_sg_ln_b, v_sg_w_s, v_sg_b_s, v_sg_w_out, v_ffn_w_up, v_ffn_conv_w, v_ffn_conv_b, v_ffn_w_down, v_ln1_g, v_ln1_b, v_ln2_g, v_ln2_b):
    names = ("lb_logits", "hg_w_in", "hg_norm_g", "hg_w_out", "sg_w_in", "sg_ln_g", "sg_ln_b", "sg_w_s", "sg_b_s",
             "sg_w_out", "ffn_w_up", "ffn_conv_w", "ffn_conv_b", "ffn_w_down", "ln1_g", "ln1_b", "ln2_g", "ln2_b")
    w = dict(zip(names, (lb_logits, hg_w_in, hg_norm_g, hg_w_out, sg_w_in, sg_ln_g, sg_ln_b, sg_w_s, sg_b_s,
                         sg_w_out, ffn_w_up, ffn_conv_w, ffn_conv_b, ffn_w_down, ln1_g, ln1_b, ln2_g, ln2_b)))
    mom = dict(zip(names, (m_lb_logits, m_hg_w_in, m_hg_norm_g, m_hg_w_out, m_sg_w_in, m_sg_ln_g, m_sg_ln_b, m_sg_w_s,
                           m_sg_b_s, m_sg_w_out, m_ffn_w_up, m_ffn_conv_w, m_ffn_conv_b, m_ffn_w_down, m_ln1_g,
                           m_ln1_b, m_ln2_g, m_ln2_b)))
    var = dict(zip(names, (v_lb_logits, v_hg_w_in, v_hg_norm_g, v_hg_w_out, v_sg_w_in, v_sg_ln_g, v_sg_ln_b, v_sg_w_s,
                           v_sg_b_s, v_sg_w_out, v_ffn_w_up, v_ffn_conv_w, v_ffn_conv_b, v_ffn_w_down, v_ln1_g,
                           v_ln1_b, v_ln2_g, v_ln2_b)))
    x2, tgt = x[0], loss_target[0]
    d = x2.shape[1]
    fq = ffn_conv_w.shape[2]
    dq = sg_ln_g.shape[1]
    cx, cy, _ = _place()
    me = 2 * cx + cy

    shards = {"hg_in": hg_w_in[0], "hg_out": hg_w_out[0], "sg_in": sg_w_in[0], "sg_out": sg_w_out[0],
              "up0": ffn_w_up[0], "up1": ffn_w_up[1], "dn0": ffn_w_down[0], "dn1": ffn_w_down[1]}
    shards = {k: v.astype(BF16) for k, v in shards.items()}
    wide = max(fq, dq)
    tiny = jnp.concatenate([jnp.pad(ffn_conv_w.reshape(6, fq), ((0, 0), (0, wide - fq))),
                            jnp.pad(sg_ln_g, ((0, 0), (0, wide - dq))),
                            jnp.pad(sg_ln_b, ((0, 0), (0, wide - dq)))], axis=0)
    tiny_all = _allgather_whole(tiny, "gather_small")
    conv_w_full = jnp.transpose(tiny_all[:, 0:6, :fq].reshape(N_CHIPS, 2, 3, fq), (1, 2, 0, 3)).reshape(2, 3, N_CHIPS * fq)
    sm = {"lb_logits": lb_logits, "hg_norm_g": hg_norm_g, "ln1_g": ln1_g, "ln1_b": ln1_b, "ln2_g": ln2_g,
          "ln2_b": ln2_b, "conv_w": conv_w_full, "conv_b": ffn_conv_b,
          "sg_ln_g": tiny_all[:, 6, :dq].reshape(1, N_CHIPS * dq),
          "sg_ln_b": tiny_all[:, 7, :dq].reshape(1, N_CHIPS * dq),
          "sg_w_s": sg_w_s[0], "sg_b_s_t": jnp.transpose(sg_b_s[0])}

    loss_row, grad_x, gs, upd = _step(x2, tgt, x2.astype(BF16), shards, sm, w, mom, var)
    loss = lax.psum(loss_row[0, 0], ("x", "y", "c"))

    packed, layout = _pack(gs)
    summed = _unpack(_sum_leading(_gather_all_devices(packed, "gather_small_grads"), "sum_small_grads"), layout)

    grads = {
        "lb_logits": _lb_logits_grad(lb_logits, summed["lb"], "lb_logits_grad"),
        "hg_norm_g": summed["hg_norm_g"],
        "sg_ln_g": lax.dynamic_slice_in_dim(summed["sg_ln_g"], me * dq, dq, axis=1),
        "sg_ln_b": lax.dynamic_slice_in_dim(summed["sg_ln_b"], me * dq, dq, axis=1),
        "sg_w_s": summed["sg_w_s"][None], "sg_b_s": jnp.transpose(summed["sg_b_s_t"])[None],
        "ffn_conv_w": lax.dynamic_slice_in_dim(summed["conv_w"], me * fq, fq, axis=2),
        "ffn_conv_b": summed["conv_b"],
        "ln1_g": summed["ln1_g"], "ln1_b": summed["ln1_b"], "ln2_g": summed["ln2_g"], "ln2_b": summed["ln2_b"],
    }

    delta, new_m, new_v = {}, {}, {}
    for k, (dlt, mm, vv, gg) in upd.items():
        delta[k], new_m[k], new_v[k], grads[k] = dlt, mm, vv, gg
    small_names = [k for k in names if k not in upd]

    def pack_small(src):
        flat = [src[k].reshape(-1) for k in small_names]
        n = sum(a.size for a in flat)
        flat.append(jnp.zeros(((-n) % (PACK_ROWS * LANES),), F32))
        return jnp.concatenate(flat).reshape(1, -1, LANES)

    outs = _adamw(pack_small(w), [pack_small(grads)[0]], pack_small(mom), pack_small(var), "adamw_small")
    off = 0
    for k in small_names:
        n = w[k].size
        for dst, o in zip((delta, new_m, new_v), outs):
            dst[k] = o.reshape(-1)[off:off + n].reshape(w[k].shape)
        off += n

    return (loss, grad_x[None], *[grads[k] for k in names], *[delta[k] for k in names],
            *[new_m[k] for k in names], *[new_v[k] for k in names])
```

```python
import functools

import jax
import jax.numpy as jnp
from jax import lax
from jax.experimental import pallas as pl
from jax.experimental.pallas import tpu as pltpu

F32 = jnp.float32
BF16 = jnp.bfloat16
HI = lax.Precision.HIGHEST
MESH = pl.DeviceIdType.MESH

ALPHA = (2 * 2) ** 0.25
LN_EPS = 1e-5
RMS_EPS = 1e-6
ADAM_LR, ADAM_B1, ADAM_B2, ADAM_EPS, ADAM_WD, ADAM_STEP = 0.001, 0.9, 0.999, 1e-08, 0.01, 10

LANES = 128
SUB = 16
GCHUNK = 128
VMEM_LIMIT = 56 * 1024 * 1024
N_CHIPS = 4
N_DEV = 8

NT = (((1,), (1,)), ((), ()))
TN = (((0,), (0,)), ((), ()))
NN = (((1,), (0,)), ((), ()))


def _pick(dim, prefs):
    for p in prefs:
        if dim % p == 0:
            return p
    return dim


def _params(sem=None, **kw):
    return pltpu.CompilerParams(dimension_semantics=sem, vmem_limit_bytes=VMEM_LIMIT, **kw)


def _sigmoid_pair(x):
    e = jnp.exp(-jnp.abs(x))
    inv = 1.0 / (1.0 + e)
    pos = x >= 0
    return jnp.where(pos, inv, e * inv), jnp.where(pos, e * inv, inv)


def _ln_hat(x):
    mu = jnp.mean(x, axis=-1, keepdims=True)
    xc = x - mu
    var = jnp.mean(xc * xc, axis=-1, keepdims=True)
    rstd = lax.rsqrt(var + LN_EPS)
    return xc * rstd, rstd


def _lower_bound(logits):
    m = jnp.max(logits, axis=0, keepdims=True)
    e = jnp.exp(logits - m)
    return e[0:1, :] / jnp.sum(e, axis=0, keepdims=True)


MATMUL_VMEM_BUDGET = 40 * 1024 * 1024


def _fit_bk(kdim, bm, bn, out_dtype, has_resid, na=1, nb=1):
    fixed = bm * bn * (4 + 2 * jnp.dtype(out_dtype).itemsize + (8 if has_resid else 0))
    best = LANES
    for bk in range(LANES, kdim + 1, LANES):
        if kdim % bk == 0 and fixed + 4 * bk * (bm * na + bn * nb) <= MATMUL_VMEM_BUDGET:
            best = bk
    return best


def _matmul(a, b, *, mode, name, out_dtype=F32, resid=None, alpha=1.0, b_off=0, nsh=None, comm=None):
    a_parts = a if isinstance(a, (list, tuple)) else [a]
    b_parts = b if isinstance(b, (list, tuple)) else [b]
    n_parts = max(len(a_parts), len(b_parts))
    if mode == "nn":
        m, kdim = a.shape
        _, _, ns = b.shape
        bm = _pick(m, (1024, 512, 256, 128))
        bn = _pick(ns, (1024, 1408, 512, 256, 128))
        bk = _fit_bk(kdim, bm, bn, out_dtype, resid is not None)
        nps = ns // bn
        grid = (m // bm, nsh * nps, kdim // bk)
        a_specs = [pl.BlockSpec((bm, bk), lambda i, j, k: (i, k))]
        b_specs = [pl.BlockSpec((None, bk, bn), lambda i, j, k: (b_off + j // nps, k, j % nps))]
        o_spec = pl.BlockSpec((bm, bn), lambda i, j, k: (i, j))
        out_shape = jax.ShapeDtypeStruct((m, nsh * ns), out_dtype)
        dims, part_axis, per_part = NN, 2, grid[2]
    elif mode == "nt":
        m = a_parts[0].shape[0]
        _, kdim, ns = b.shape
        bm = _pick(m, (1024, 512, 256, 128))
        bn = _pick(kdim, (1024, 1408, 512, 256, 128))
        bk = _fit_bk(ns, bm, bn, out_dtype, resid is not None, na=n_parts)
        kps = ns // bk
        per_part = nsh // n_parts * kps
        grid = (m // bm, kdim // bn, nsh * kps)
        a_specs = [pl.BlockSpec((bm, bk), lambda i, j, k, p=p: (jnp.where(k // per_part == p, i, 0),
                                                                  jnp.where(k // per_part == p, k % per_part, 0)))
                   for p in range(n_parts)]
        b_specs = [pl.BlockSpec((None, bn, bk), lambda i, j, k: (b_off + k // kps, j, k % kps))]
        o_spec = pl.BlockSpec((bm, bn), lambda i, j, k: (i, j))
        out_shape = jax.ShapeDtypeStruct((m, kdim), out_dtype)
        dims, part_axis = NT, 2
    else:
        t, kdim = a.shape
        ns = b_parts[0].shape[1] * n_parts // nsh
        bm = _pick(kdim, (1024, 1408, 512, 256, 128))
        bn = _pick(ns, (1024, 1408, 512, 256, 128))
        bk = _fit_bk(t, bm, bn, out_dtype, resid is not None, nb=n_parts)
        nps = ns // bn
        per_part = nsh // n_parts * nps
        grid = (kdim // bm, nsh * nps, t // bk)
        a_specs = [pl.BlockSpec((bk, bm), lambda i, j, k: (k, i))]
        b_specs = [pl.BlockSpec((bk, bn), lambda i, j, k, p=p: (jnp.where(j // per_part == p, k, 0),
                                                                  jnp.where(j // per_part == p, j % per_part, 0)))
                   for p in range(n_parts)]
        o_spec = pl.BlockSpec((None, bm, bn), lambda i, j, k: (j // nps, i, j % nps))
        out_shape = jax.ShapeDtypeStruct((nsh, kdim, ns), out_dtype)
        dims, part_axis = TN, 1
    nk = grid[2]
    na, nb_ = len(a_parts), len(b_parts)
    has_resid = resid is not None

    def kern(*refs):
        a_refs, b_refs = refs[:na], refs[na:na + nb_]
        r_ref = refs[na + nb_] if has_resid else None
        k = pl.program_id(2)

        def finish(r, o_ref):
            if has_resid:
                r = r + alpha * r_ref[...]
            o_ref[...] = r.astype(o_ref.dtype)

        def add(a_ref, b_ref):
            part = lax.dot_general(a_ref[...], b_ref[...], dims, preferred_element_type=F32)
            if nk == 1:
                finish(part, refs[-1])
                return
            acc_ref = refs[-1]

            @pl.when(k == 0)
            def _():
                acc_ref[...] = part

            @pl.when(k > 0)
            def _():
                acc_ref[...] += part

        if n_parts == 1:
            add(a_refs[0], b_refs[0])
        else:
            which = pl.program_id(part_axis) // per_part
            for p in range(n_parts):
                pl.when(which == p)(functools.partial(add, a_refs[min(p, na - 1)], b_refs[min(p, nb_ - 1)]))
        if nk > 1:
            @pl.when(k == nk - 1)
            def _():
                finish(refs[-1][...], refs[-2])

    in_specs = a_specs + b_specs
    operands = list(a_parts) + list(b_parts)
    if has_resid:
        in_specs.append(pl.BlockSpec((bm, bn), lambda i, j, k: (i, j)))
        operands.append(resid)
    outs, carried = _carried_call(
        kern, comm, name=name, grid=grid, in_specs=in_specs, out_specs=[o_spec], out_shape=[out_shape],
        operands=operands, scratch_shapes=[pltpu.VMEM((bm, bn), F32)] if nk > 1 else [],
        sem=("parallel", "parallel", "arbitrary"))
    return outs[0] if comm is None else (outs[0], carried)


def _res_ln_fwd(h_prev, sub, g, b, name):
    t, d = h_prev.shape
    tb = _pick(t, (256, 128, 64, 32, 16))

    def kern(hp_ref, s_ref, g_ref, b_ref, xin_ref, h_ref, hb_ref):
        xin = ALPHA * hp_ref[...] + s_ref[...]
        xhat, _ = _ln_hat(xin)
        h = xhat * g_ref[...] + b_ref[...]
        xin_ref[...] = xin
        h_ref[...] = h
        hb_ref[...] = h.astype(BF16)

    row = pl.BlockSpec((tb, d), lambda i: (i, 0))
    vec = pl.BlockSpec((1, d), lambda i: (0, 0))
    return pl.pallas_call(
        kern, name=name, grid=(t // tb,), in_specs=[row, row, vec, vec], out_specs=[row, row, row],
        out_shape=[jax.ShapeDtypeStruct((t, d), F32), jax.ShapeDtypeStruct((t, d), F32),
                   jax.ShapeDtypeStruct((t, d), BF16)],
        compiler_params=_params(("parallel",)),
    )(h_prev, sub, g, b)


def _ln_bwd(xin, dy_or_target, g, b, name, loss_head=False):
    t, d = xin.shape
    tb = _pick(t, (256, 128, 64, 32, 16))
    nb = t // tb

    def kern(x_ref, dy_ref, g_ref, b_ref, dx_ref, dxb_ref, dg_ref, db_ref, *rest):
        i = pl.program_id(0)
        xhat, rstd = _ln_hat(x_ref[...])
        gv = g_ref[...]
        if loss_head:
            loss_ref = rest[0]
            err = xhat * gv + b_ref[...] - dy_ref[...]
            dy = err * (1.0 / d)
            part = 0.5 * jnp.sum(jnp.sum(err * err, axis=1, keepdims=True), axis=0, keepdims=True) * (1.0 / d)
        else:
            dy = dy_ref[...]

        @pl.when(i == 0)
        def _():
            dg_ref[...] = jnp.zeros_like(dg_ref)
            db_ref[...] = jnp.zeros_like(db_ref)
            if loss_head:
                loss_ref[...] = jnp.zeros_like(loss_ref)

        dg_ref[...] += jnp.sum(dy * xhat, axis=0, keepdims=True)
        db_ref[...] += jnp.sum(dy, axis=0, keepdims=True)
        if loss_head:
            loss_ref[...] += jnp.broadcast_to(part, loss_ref.shape)
        dxh = dy * gv
        m1 = jnp.mean(dxh, axis=-1, keepdims=True)
        m2 = jnp.mean(dxh * xhat, axis=-1, keepdims=True)
        dx = rstd * (dxh - m1 - xhat * m2)
        dx_ref[...] = dx
        dxb_ref[...] = dx.astype(BF16)

    row = pl.BlockSpec((tb, d), lambda i: (i, 0))
    vec = pl.BlockSpec((1, d), lambda i: (0, 0))
    out_specs = [row, row, vec, vec]
    out_shape = [jax.ShapeDtypeStruct((t, d), F32), jax.ShapeDtypeStruct((t, d), BF16),
                 jax.ShapeDtypeStruct((1, d), F32), jax.ShapeDtypeStruct((1, d), F32)]
    if loss_head:
        out_specs.append(pl.BlockSpec((1, LANES), lambda i: (0, 0)))
        out_shape.append(jax.ShapeDtypeStruct((1, LANES), F32))
    return pl.pallas_call(
        kern, name=name, grid=(nb,), in_specs=[row, row, vec, vec], out_specs=out_specs, out_shape=out_shape,
        compiler_params=_params(("arbitrary",)),
    )(xin, dy_or_target, g, b)


def _conv_gate_fwd(u, conv_w, conv_b, name):
    t, f2 = u.shape
    f = f2 // 2
    tb = _pick(t, (512, 256, 128, 64, 32, 16))
    cn = _pick(f, (1408, 1024, 512, 256, 128))
    ncb = f // cn
    hb = tb // 8

    def kern(a_ref, ah_ref, b_ref, w_ref, cb_ref, o_ref):
        i = pl.program_id(0)
        a = a_ref[...]
        halo = jnp.where(i > 0, ah_ref[...], 0.0)
        rid = lax.broadcasted_iota(jnp.int32, a.shape, 0)
        s1 = jnp.where(rid == 0, halo[7:8, :], pltpu.roll(a, 1, 0))
        s2 = jnp.where(rid == 0, halo[6:7, :], jnp.where(rid == 1, halo[7:8, :], pltpu.roll(a, 2, 0)))
        w = w_ref[...]
        conv = w[2:3, :] * a + w[1:2, :] * s1 + w[0:1, :] * s2 + cb_ref[...]
        sp, _ = _sigmoid_pair(conv)
        o_ref[...] = (conv * sp * b_ref[...]).astype(BF16)

    return pl.pallas_call(
        kern, name=name, grid=(t // tb, ncb),
        in_specs=[pl.BlockSpec((tb, cn), lambda i, j: (i, j)),
                  pl.BlockSpec((8, cn), lambda i, j: (jnp.maximum(i * hb - 1, 0), j)),
                  pl.BlockSpec((tb, cn), lambda i, j: (i, j + ncb)),
                  pl.BlockSpec((3, cn), lambda i, j: (0, j)),
                  pl.BlockSpec((1, cn), lambda i, j: (0, j))],
        out_specs=pl.BlockSpec((tb, cn), lambda i, j: (i, j)),
        out_shape=jax.ShapeDtypeStruct((t, f), BF16),
        compiler_params=_params(("parallel", "parallel")),
    )(u, u, u, conv_w, conv_b)


def _conv_gate_bwd(u, dgact, conv_w, conv_b, name):
    t, f2 = u.shape
    f = f2 // 2
    tb = _pick(t, (512, 256, 128, 64, 32, 16))
    cn = _pick(f, (1408, 1024, 512, 256, 128))
    ncb = f // cn
    hb = tb // 8
    nb = t // tb
    last8 = t // 8 - 1

    def kern(a_ref, ap_ref, an_ref, b_ref, bn_ref, dg_ref, dgn_ref, w_ref, cb_ref,
             da_ref, db_ref, dw_ref, dcb_ref):
        i = pl.program_id(1)
        a = a_ref[...]
        w = w_ref[...]
        ext = jnp.concatenate([jnp.where(i > 0, ap_ref[...], 0.0), a, an_ref[...]], axis=0)
        e1 = pltpu.roll(ext, 1, 0)
        e2 = pltpu.roll(ext, 2, 0)
        conv = (w[2:3, :] * ext + w[1:2, :] * e1 + w[0:1, :] * e2 + cb_ref[...])[8:, :]
        bmn = jnp.concatenate([b_ref[...], bn_ref[...]], axis=0)
        dgmn = jnp.concatenate([dg_ref[...], jnp.where(i < nb - 1, dgn_ref[...], 0.0)], axis=0)
        sp, sn = _sigmoid_pair(conv)
        da = dgmn * bmn * (sp * (1.0 + conv * sn))
        n = tb + 8
        dap = w[2:3, :] * da + w[1:2, :] * pltpu.roll(da, n - 1, 0) + w[0:1, :] * pltpu.roll(da, n - 2, 0)
        da_ref[...] = dap[:tb, :].astype(BF16)
        db_ref[...] = (dg_ref[...] * (conv * sp)[:tb, :]).astype(BF16)
        dam = da[:tb, :]

        @pl.when(i == 0)
        def _():
            dw_ref[...] = jnp.zeros_like(dw_ref)
            dcb_ref[...] = jnp.zeros_like(dcb_ref)

        dw = jnp.concatenate([jnp.sum(dam * e2[8:8 + tb, :], axis=0, keepdims=True),
                              jnp.sum(dam * e1[8:8 + tb, :], axis=0, keepdims=True),
                              jnp.sum(dam * a, axis=0, keepdims=True)], axis=0)
        dw_ref[...] += dw
        dcb_ref[...] += jnp.sum(dam, axis=0, keepdims=True)

    main_a = pl.BlockSpec((tb, cn), lambda j, i: (i, j))
    prev_a = pl.BlockSpec((8, cn), lambda j, i: (jnp.maximum(i * hb - 1, 0), j))
    next_a = pl.BlockSpec((8, cn), lambda j, i: (jnp.minimum((i + 1) * hb, last8), j))
    main_b = pl.BlockSpec((tb, cn), lambda j, i: (i, j + ncb))
    next_b = pl.BlockSpec((8, cn), lambda j, i: (jnp.minimum((i + 1) * hb, last8), j + ncb))
    return pl.pallas_call(
        kern, name=name, grid=(ncb, nb),
        in_specs=[main_a, prev_a, next_a, main_b, next_b, main_a, next_a,
                  pl.BlockSpec((3, cn), lambda j, i: (0, j)), pl.BlockSpec((1, cn), lambda j, i: (0, j))],
        out_specs=[main_a, main_a, pl.BlockSpec((3, cn), lambda j, i: (0, j)),
                   pl.BlockSpec((1, cn), lambda j, i: (0, j))],
        out_shape=[jax.ShapeDtypeStruct((t, f), BF16), jax.ShapeDtypeStruct((t, f), BF16),
                   jax.ShapeDtypeStruct((3, f), F32), jax.ShapeDtypeStruct((1, f), F32)],
        compiler_params=_params(("parallel", "arbitrary")),
    )(u, u, u, u, u, dgact, dgact, conv_w, conv_b)


def _hg_gates(qp, fp, lb):
    sq, _ = _sigmoid_pair(qp)
    sf, snf = _sigmoid_pair(fp)
    forget = lb + (1.0 - lb) * sf
    return sq, sf, snf, forget, jnp.log(forget), (1.0 - lb) * snf


def _tri(lower):
    r = lax.broadcasted_iota(jnp.int32, (SUB, SUB), 0)
    c = lax.broadcasted_iota(jnp.int32, (SUB, SUB), 1)
    return ((r >= c) if lower else (r <= c)).astype(BF16)


def _split2(x):
    hi = x.astype(BF16)
    return hi, (x - hi.astype(F32)).astype(BF16)


def _dot3(a, b, dims):
    (ah, al), (bh, bl) = a, b
    return (lax.dot_general(ah, bh, dims, preferred_element_type=F32)
            + (lax.dot_general(ah, bl, dims, preferred_element_type=F32)
               + lax.dot_general(al, bh, dims, preferred_element_type=F32)))


def _running_sum(tri, x):
    hi, lo = _split2(x)
    rest = (x - hi.astype(F32)) - lo.astype(F32)
    return (lax.dot_general(tri, hi, NN, preferred_element_type=F32)
            + (lax.dot_general(tri, lo, NN, preferred_element_type=F32)
               + lax.dot_general(tri, rest.astype(BF16), NN, preferred_element_type=F32)))


HEADS_PER_STEP = 8
STEP_UNROLL = 2


def _hgrn2_fwd(proj, lb_logits, norm_g, name, comm=None):
    t, d4 = proj.shape
    d = d4 // 4
    nh = d // LANES
    hb = _pick(nh, (HEADS_PER_STEP, 2, 1))
    wb = hb * LANES
    tb = _pick(t, (256, 128, 64, 32, 16))
    nb = t // tb
    nsc = tb // SUB

    def kern(q_ref, f_ref, i_ref, g_ref, lbl_ref, ng_ref, y_ref, o_ref, st_ref, s_ref):
        @pl.when(pl.program_id(1) == 0)
        def _():
            s_ref[...] = jnp.zeros_like(s_ref)

        lb_all = _lower_bound(lbl_ref[...])
        ng_all = ng_ref[...]
        ltri = _tri(True)
        rcol = lax.broadcasted_iota(jnp.int32, (SUB, 1), 0)

        heads = [slice(h * LANES, (h + 1) * LANES) for h in range(hb)]

        def step(sc, carry):
            rows = pl.ds(pl.multiple_of(sc * SUB, SUB), SUB)
            qp, fp, v, gp = q_ref[rows, :], f_ref[rows, :], i_ref[rows, :], g_ref[rows, :]
            sq, _, _, _, lf, k = _hg_gates(qp, fp, lb_all)
            q = qp * sq
            bl = _running_sum(ltri, lf)
            bend = bl[SUB - 1:SUB, :]
            dec = jnp.exp(bend)
            qs2 = _split2(q * jnp.exp(bl))
            kd2 = _split2(k * jnp.exp(bend - bl))
            v2 = _split2(v)
            states = [s_ref[h] for h in range(hb)]
            o = [_dot3((qs2[0][:, c], qs2[1][:, c]), _split2(states[h]), NT) for h, c in enumerate(heads)]
            for s in range(SUB):
                e = jnp.exp(jnp.minimum(bl - bl[s:s + 1, :], 0.0))
                p = q * e * k[s:s + 1, :]
                for h, c in enumerate(heads):
                    a = jnp.sum(p[:, c], axis=1, keepdims=True)
                    o[h] = o[h] + jnp.where(rcol >= s, a, 0.0) * v[s:s + 1, c]
            for h, c in enumerate(heads):
                st_ref[sc, h] = states[h]
                s_ref[h] = states[h] * dec[:, c] + _dot3((v2[0][:, c], v2[1][:, c]), (kd2[0][:, c], kd2[1][:, c]), TN)
            o_ref[rows, :] = jnp.concatenate(o, axis=1)
            on = jnp.concatenate(
                [oh * lax.rsqrt(jnp.mean(oh * oh, axis=1, keepdims=True) + RMS_EPS) for oh in o], axis=1)
            sg, _ = _sigmoid_pair(gp)
            y_ref[rows, :] = (on * ng_all * (gp * sg)).astype(BF16)
            return carry

        lax.fori_loop(0, nsc, step, 0, unroll=STEP_UNROLL)

    def col(off):
        return pl.BlockSpec((tb, wb), lambda h, j: (j, h + off * (nh // hb)))

    return _carried_call(
        kern, comm, name=name, grid=(nh // hb, nb),
        in_specs=[col(0), col(1), col(2), col(3),
                  pl.BlockSpec((3, wb), lambda h, j: (0, h)), pl.BlockSpec((1, wb), lambda h, j: (0, h))],
        out_specs=[col(0), col(0), pl.BlockSpec((nsc, hb, LANES, LANES), lambda h, j: (j, h, 0, 0))],
        out_shape=[jax.ShapeDtypeStruct((t, d), BF16), jax.ShapeDtypeStruct((t, d), F32),
                   jax.ShapeDtypeStruct((t // SUB, nh, LANES, LANES), F32)],
        scratch_shapes=[pltpu.VMEM((hb, LANES, LANES), F32)],
        operands=[proj, proj, proj, proj, lb_logits, norm_g], sem=("parallel", "arbitrary"))


def _hgrn2_bwd(proj, lb_logits, norm_g, o_raw, states, dy, name, comm=None):
    t, d4 = proj.shape
    d = d4 // 4
    nh = d // LANES
    hb = _pick(nh, (HEADS_PER_STEP, 2, 1))
    wb = hb * LANES
    tb = _pick(t, (256, 128, 64, 32, 16))
    nb = t // tb
    nsc = tb // SUB

    def kern(q_ref, f_ref, i_ref, g_ref, lbl_ref, ng_ref, o_ref, st_ref, dy_ref,
             dq_ref, df_ref, di_ref, dgp_ref, dlb_ref, dng_ref, ds_ref, gc_ref):
        j = pl.program_id(1)

        @pl.when(j == 0)
        def _():
            ds_ref[...] = jnp.zeros_like(ds_ref)
            gc_ref[...] = jnp.zeros_like(gc_ref)
            dlb_ref[...] = jnp.zeros_like(dlb_ref)
            dng_ref[...] = jnp.zeros_like(dng_ref)

        lb_all = _lower_bound(lbl_ref[...])
        ng_all = ng_ref[...]
        ltri, utri = _tri(True), _tri(False)
        rcol = lax.broadcasted_iota(jnp.int32, (SUB, 1), 0)
        rid = lax.broadcasted_iota(jnp.int32, (SUB, wb), 0)

        heads = [slice(h * LANES, (h + 1) * LANES) for h in range(hb)]

        def per_head(fn):
            return jnp.concatenate([jnp.broadcast_to(fn(c), (SUB, LANES)) for c in heads], axis=1)

        def step(it, carry):
            sc = nsc - 1 - it
            rows = pl.ds(pl.multiple_of(sc * SUB, SUB), SUB)
            qp, fp, v, gp = q_ref[rows, :], f_ref[rows, :], i_ref[rows, :], g_ref[rows, :]
            o, dyv = o_ref[rows, :], dy_ref[rows, :]
            sq, sf, snf, forget, lf, k = _hg_gates(qp, fp, lb_all)
            q = qp * sq
            bl = _running_sum(ltri, lf)
            ebl = jnp.exp(bl)
            bend = bl[SUB - 1:SUB, :]
            dec = jnp.exp(bend)
            dte = jnp.exp(bend - bl)
            r = per_head(lambda c: lax.rsqrt(jnp.mean(o[:, c] * o[:, c], axis=1, keepdims=True) + RMS_EPS))
            ohat = o * r
            sg, sng = _sigmoid_pair(gp)
            don = dyv * (gp * sg)
            dgp_ref[rows, :] = (dyv * (ohat * ng_all) * (sg * (1.0 + gp * sng))).astype(BF16)
            dng_ref[...] += jnp.sum(don * ohat, axis=0, keepdims=True)
            doh = don * ng_all
            dot_oh = doh * ohat
            do = r * (doh - ohat * per_head(lambda c: jnp.mean(dot_oh[:, c], axis=1, keepdims=True)))
            do2, qs2, kd2, v2 = _split2(do), _split2(q * ebl), _split2(k * dte), _split2(v)
            dq_h, dk_h, dv_h = [], [], []
            for h, c in enumerate(heads):
                dstate = ds_ref[h]
                ds2 = _split2(dstate)
                doc = (do2[0][:, c], do2[1][:, c])
                dq_h.append(_dot3(doc, _split2(st_ref[sc, h]), NN))
                dv_h.append(_dot3((kd2[0][:, c], kd2[1][:, c]), ds2, NT))
                dk_h.append(_dot3((v2[0][:, c], v2[1][:, c]), ds2, NN))
                ds_ref[h] = dstate * dec[:, c] + _dot3(doc, (qs2[0][:, c], qs2[1][:, c]), TN)
            dq = jnp.concatenate(dq_h, axis=1) * ebl
            dk = jnp.concatenate(dk_h, axis=1) * dte
            dv = jnp.concatenate(dv_h, axis=1)
            dki = jnp.zeros((SUB, wb), F32)
            dvi = jnp.zeros((SUB, wb), F32)
            for s in range(SUB):
                e = jnp.exp(jnp.minimum(bl - bl[s:s + 1, :], 0.0))
                qe = q * e
                ks = k[s:s + 1, :]
                live = rcol >= s
                pk = qe * ks
                pv = do * v[s:s + 1, :]
                a = per_head(lambda c: jnp.where(live, jnp.sum(pk[:, c], axis=1, keepdims=True), 0.0))
                da = per_head(lambda c: jnp.where(live, jnp.sum(pv[:, c], axis=1, keepdims=True), 0.0))
                dq = dq + da * (e * ks)
                dki = jnp.where(rid == s, jnp.sum(da * qe, axis=0, keepdims=True), dki)
                dvi = jnp.where(rid == s, jnp.sum(a * do, axis=0, keepdims=True), dvi)
            dk = dk + dki
            dv = dv + dvi
            w = q * dq - k * dk
            gc = gc_ref[...]
            dlf = _running_sum(utri, w) + gc
            gc_ref[...] = gc + jnp.sum(w, axis=0, keepdims=True)
            t1 = dlf / forget - dk
            df_ref[rows, :] = ((1.0 - lb_all) * sf * snf * t1).astype(BF16)
            dlb_ref[...] += jnp.sum(snf * t1, axis=0, keepdims=True)
            dq_ref[rows, :] = (dq * (sq * (1.0 + qp * (1.0 - sq)))).astype(BF16)
            di_ref[rows, :] = dv.astype(BF16)
            return carry

        lax.fori_loop(0, nsc, step, 0, unroll=STEP_UNROLL)

    def col(off):
        return pl.BlockSpec((tb, wb), lambda h, j: (nb - 1 - j, h + off * (nh // hb)))

    vec = pl.BlockSpec((1, wb), lambda h, j: (0, h))
    return _carried_call(
        kern, comm, name=name, grid=(nh // hb, nb),
        in_specs=[col(0), col(1), col(2), col(3), pl.BlockSpec((3, wb), lambda h, j: (0, h)), vec,
                  col(0), pl.BlockSpec((nsc, hb, LANES, LANES), lambda h, j: (nb - 1 - j, h, 0, 0)), col(0)],
        out_specs=[col(0), col(0), col(0), col(0), vec, vec],
        out_shape=[jax.ShapeDtypeStruct((t, d), BF16)] * 4 + [jax.ShapeDtypeStruct((1, d), F32)] * 2,
        scratch_shapes=[pltpu.VMEM((hb, LANES, LANES), F32), pltpu.VMEM((1, wb), F32)],
        operands=[proj, proj, proj, proj, lb_logits, norm_g, o_raw, states, dy], sem=("parallel", "arbitrary"))


_INV_SQRT2 = 0.7071067811865476
_INV_SQRT2PI = 0.3989422804014327


def _gelu(x):
    return 0.5 * x * (1.0 + lax.erf(x * _INV_SQRT2))


def _gelu_grad(x):
    return 0.5 * (1.0 + lax.erf(x * _INV_SQRT2)) + x * jnp.exp(-0.5 * x * x) * _INV_SQRT2PI


def _causal(w):
    r = lax.broadcasted_iota(jnp.int32, (GCHUNK, GCHUNK), 0)
    c = lax.broadcasted_iota(jnp.int32, (GCHUNK, GCHUNK), 1)
    return jnp.where(r >= c, w, 0.0)


def _sg_gate_fwd(pre, ln_g, ln_b, w_s, b_s_t, name):
    t, d2 = pre.shape
    d = d2 // 2
    ng = d // LANES

    def kern(pre_ref, g_ref, b_ref, ws_ref, bs_ref, y_ref):
        z = _gelu(pre_ref[...])
        u = z[:, :d]
        vhat, _ = _ln_hat(z[:, d:])
        vn = (vhat * g_ref[...] + b_ref[...]).astype(BF16)
        bs = bs_ref[...]
        for g in range(ng):
            cols = slice(g * LANES, (g + 1) * LANES)
            wc = _causal(ws_ref[g]).astype(BF16)
            gate = jnp.dot(wc, vn[:, cols], preferred_element_type=F32) + bs[:, g:g + 1]
            y_ref[:, cols] = (u[:, cols] * gate).astype(BF16)

    vec = pl.BlockSpec((1, d), lambda i: (0, 0))
    return pl.pallas_call(
        kern, name=name, grid=(t // GCHUNK,),
        in_specs=[pl.BlockSpec((GCHUNK, d2), lambda i: (i, 0)), vec, vec,
                  pl.BlockSpec((ng, GCHUNK, GCHUNK), lambda i: (0, 0, 0)),
                  pl.BlockSpec((GCHUNK, ng), lambda i: (0, 0))],
        out_specs=pl.BlockSpec((GCHUNK, d), lambda i: (i, 0)),
        out_shape=jax.ShapeDtypeStruct((t, d), BF16),
        compiler_params=_params(("parallel",)),
    )(pre, ln_g, ln_b, w_s, b_s_t)


def _sg_gate_bwd(pre, dy, ln_g, ln_b, w_s, b_s_t, name):
    t, d2 = pre.shape
    d = d2 // 2
    ng = d // LANES

    def kern(pre_ref, dy_ref, g_ref, b_ref, ws_ref, bs_ref, dpre_ref, dws_ref, dbs_ref, dg_ref, db_ref, dvn_ref):
        @pl.when(pl.program_id(0) == 0)
        def _():
            dws_ref[...] = jnp.zeros_like(dws_ref)
            dbs_ref[...] = jnp.zeros_like(dbs_ref)
            dg_ref[...] = jnp.zeros_like(dg_ref)
            db_ref[...] = jnp.zeros_like(db_ref)

        pre = pre_ref[...]
        z = _gelu(pre)
        u = z[:, :d]
        vhat, rstd = _ln_hat(z[:, d:])
        gv = g_ref[...]
        vn = (vhat * gv + b_ref[...]).astype(BF16)
        bs = bs_ref[...]
        dyv = dy_ref[...]
        gp = _gelu_grad(pre)
        lane = lax.broadcasted_iota(jnp.int32, (GCHUNK, ng), 1)
        dbs = jnp.zeros((GCHUNK, ng), F32)
        for g in range(ng):
            cols = slice(g * LANES, (g + 1) * LANES)
            wc = _causal(ws_ref[g]).astype(BF16)
            vng = vn[:, cols]
            gate = jnp.dot(wc, vng, preferred_element_type=F32) + bs[:, g:g + 1]
            dpre_ref[:, cols] = (dyv[:, cols] * gate * gp[:, cols]).astype(BF16)
            dgate = dyv[:, cols] * u[:, cols]
            dbs = dbs + jnp.where(lane == g, jnp.sum(dgate, axis=1, keepdims=True), 0.0)
            dgb = dgate.astype(BF16)
            dws_ref[g] += _causal(lax.dot_general(dgb, vng, NT, preferred_element_type=F32))
            dvn_ref[:, cols] = lax.dot_general(wc, dgb, TN, preferred_element_type=F32)
        dbs_ref[...] += dbs
        dvn = dvn_ref[...]
        dg_ref[...] += jnp.sum(dvn * vhat, axis=0, keepdims=True)
        db_ref[...] += jnp.sum(dvn, axis=0, keepdims=True)
        dvh = dvn * gv
        m1 = jnp.mean(dvh, axis=-1, keepdims=True)
        m2 = jnp.mean(dvh * vhat, axis=-1, keepdims=True)
        dpre_ref[:, d:] = (rstd * (dvh - m1 - vhat * m2) * gp[:, d:]).astype(BF16)

    vec = pl.BlockSpec((1, d), lambda i: (0, 0))
    wsp = pl.BlockSpec((ng, GCHUNK, GCHUNK), lambda i: (0, 0, 0))
    bsp = pl.BlockSpec((GCHUNK, ng), lambda i: (0, 0))
    return pl.pallas_call(
        kern, name=name, grid=(t // GCHUNK,),
        in_specs=[pl.BlockSpec((GCHUNK, d2), lambda i: (i, 0)), pl.BlockSpec((GCHUNK, d), lambda i: (i, 0)),
                  vec, vec, wsp, bsp],
        out_specs=[pl.BlockSpec((GCHUNK, d2), lambda i: (i, 0)), wsp, bsp, vec, vec],
        out_shape=[jax.ShapeDtypeStruct((t, d2), BF16), jax.ShapeDtypeStruct((ng, GCHUNK, GCHUNK), F32),
                   jax.ShapeDtypeStruct((GCHUNK, ng), F32), jax.ShapeDtypeStruct((1, d), F32),
                   jax.ShapeDtypeStruct((1, d), F32)],
        scratch_shapes=[pltpu.VMEM((GCHUNK, d), F32)],
        compiler_params=_params(("arbitrary",)),
    )(pre, dy, ln_g, ln_b, w_s, b_s_t)


def _adamw_math(w, g, m, v):
    m = ADAM_B1 * m + (1.0 - ADAM_B1) * g
    v = ADAM_B2 * v + (1.0 - ADAM_B2) * (g * g)
    m_hat = m / (1.0 - ADAM_B1 ** ADAM_STEP)
    v_hat = v / (1.0 - ADAM_B2 ** ADAM_STEP)
    return -ADAM_LR * (m_hat / (jnp.sqrt(v_hat) + ADAM_EPS) + ADAM_WD * w), m, v


ADAMW_BLOCK_BYTES = 3 << 19


def _adamw(w, gs, m, v, name, comm=None):
    nl, r, c = w.shape
    rb = _pick(r, tuple(p for p in (512, 256, 128, 64, 32, 16, 8) if p * c * 4 <= ADAMW_BLOCK_BYTES))

    def kern(w_ref, m_ref, v_ref, *rest):
        g_refs, (d_ref, mo_ref, vo_ref, go_ref) = rest[:nl], rest[nl:]
        layer = pl.program_id(0)
        g = g_refs[0][...]
        for k in range(1, nl):
            g = jnp.where(layer == k, g_refs[k][...], g)
        dlt, mm, vv = _adamw_math(w_ref[...], g, m_ref[...], v_ref[...])
        d_ref[...] = dlt
        mo_ref[...] = mm
        vo_ref[...] = vv
        go_ref[...] = g

    blk = pl.BlockSpec((None, rb, c), lambda l, i: (l, i, 0))
    g_specs = [pl.BlockSpec((rb, c), lambda l, i, k=k: (jnp.where(l == k, i, 0), 0)) for k in range(nl)]
    outs, carried = _carried_call(
        kern, comm, name=name, grid=(nl, r // rb), in_specs=[blk] * 3 + g_specs, out_specs=[blk] * 4,
        out_shape=[jax.ShapeDtypeStruct((nl, r, c), F32)] * 4, operands=[w, m, v, *gs], sem=("parallel", "parallel"))
    return outs if comm is None else (outs, carried)


def _lb_logits_grad(lb_logits, dlb, name):
    def kern(l_ref, d_ref, o_ref):
        lg = l_ref[...]
        m = jnp.max(lg, axis=0, keepdims=True)
        e = jnp.exp(lg - m)
        p = e / jnp.sum(e, axis=0, keepdims=True)
        row = lax.broadcasted_iota(jnp.int32, lg.shape, 0)
        o_ref[...] = d_ref[...] * p[0:1, :] * (jnp.where(row == 0, 1.0, 0.0) - p)

    return pl.pallas_call(kern, name=name, out_shape=jax.ShapeDtypeStruct(lb_logits.shape, F32))(lb_logits, dlb)


def _sum_leading(a, name):
    n, r, c = a.shape
    rb = _pick(r, (512, 256, 128, 64, 32, 16, 8))

    def kern(a_ref, o_ref):
        acc = a_ref[0]
        for i in range(1, n):
            acc = acc + a_ref[i]
        o_ref[...] = acc

    return pl.pallas_call(
        kern, name=name, grid=(r // rb,), in_specs=[pl.BlockSpec((n, rb, c), lambda i: (0, i, 0))],
        out_specs=pl.BlockSpec((rb, c), lambda i: (i, 0)), out_shape=jax.ShapeDtypeStruct((r, c), F32),
        compiler_params=_params(("parallel",)),
    )(a)


def _place():
    x, y, c = lax.axis_index("x"), lax.axis_index("y"), lax.axis_index("c")
    return x, y, c


class _Plan:
    def __init__(self, ins, out_shapes, aliases, n_sems, build):
        self.ins, self.out_shapes, self.aliases, self.n_sems, self.build = list(ins), list(out_shapes), aliases, n_sems, build


def _merge(*plans):
    ins, outs, aliases, subs, sems = [], [], {}, [], 0
    for p in plans:
        for k, v in p.aliases.items():
            aliases[len(ins) + k] = len(outs) + v
        subs.append((p, len(ins), len(outs), sems))
        ins += p.ins
        outs += p.out_shapes
        sems += p.n_sems

    def build(in_refs, out_refs, send_sems, recv_sems, base):
        copies = []
        for p, i0, o0, s0 in subs:
            copies += p.build(in_refs[i0:i0 + len(p.ins)], out_refs[o0:o0 + len(p.out_shapes)], send_sems, recv_sems,
                              base + s0)
        return copies

    return _Plan(ins, outs, aliases, sems, build)


def _remote(src, dst, send_sems, recv_sems, k, to):
    return pltpu.make_async_remote_copy(src_ref=src, dst_ref=dst, send_sem=send_sems.at[k], recv_sem=recv_sems.at[k],
                                        device_id=to, device_id_type=MESH)


def _plan_gather_ici(shards):
    n = len(shards)

    def build(ins, outs, send_sems, recv_sems, base):
        x, y, c = _place()
        me = 2 * x + y
        copies = []
        for a in range(n):
            h = ins[a].shape[0] // 2
            rows = pl.ds(c * h, h)
            for r in (1, 2, 3):
                px, py, _ = _chip_rel(x, y, r)
                copies.append(_remote(ins[a].at[rows, :], outs[a].at[me, rows, :], send_sems, recv_sems,
                                      base + 4 * a + r - 1, (px, py, c)))
            copies.append(_remote(ins[a], outs[a].at[me], send_sems, recv_sems, base + 4 * a + 3, (x, y, 1 - c)))
        return copies

    return _Plan(shards, [jax.ShapeDtypeStruct((N_CHIPS,) + s.shape, s.dtype) for s in shards], {}, 4 * n, build)


def _plan_gather_pass(gathered):
    n = len(gathered)

    def build(ins, outs, send_sems, recv_sems, base):
        x, y, c = _place()
        copies = []
        for a in range(n):
            h = outs[a].shape[1] // 2
            rows = pl.ds(c * h, h)
            for r in (1, 2, 3):
                _, _, shard = _chip_rel(x, y, r)
                piece = outs[a].at[shard, rows, :]
                copies.append(_remote(piece, piece, send_sems, recv_sems, base + 3 * a + r - 1, (x, y, 1 - c)))
        return copies

    return _Plan(gathered, [jax.ShapeDtypeStruct(g.shape, g.dtype) for g in gathered], {a: a for a in range(n)},
                 3 * n, build)


def _plan_swap(split):
    n = len(split)

    def build(ins, outs, send_sems, recv_sems, base):
        x, y, c = _place()
        return [_remote(ins[a].at[j, 1 - c], outs[a].at[j], send_sems, recv_sems, base + N_CHIPS * a + j, (x, y, 1 - c))
                for a in range(n) for j in range(N_CHIPS)]

    return _Plan(split, [jax.ShapeDtypeStruct((N_CHIPS,) + g.shape[2:], g.dtype) for g in split], {}, N_CHIPS * n, build)


def _plan_scatter(parts):
    n = len(parts)

    def build(ins, outs, send_sems, recv_sems, base):
        x, y, c = _place()
        copies = []
        for a in range(n):
            for r in (1, 2, 3):
                px, py, shard = _chip_rel(x, y, r)
                copies.append(_remote(ins[a].at[shard], outs[a].at[r - 1], send_sems, recv_sems, base + 3 * a + r - 1,
                                      (px, py, c)))
        return copies

    return _Plan(parts, [jax.ShapeDtypeStruct((3,) + p.shape[1:], p.dtype) for p in parts], {}, 3 * n, build)


def _plan_join(bufs):
    n = len(bufs)

    def build(ins, outs, send_sems, recv_sems, base):
        x, y, c = _place()
        return [_remote(outs[a].at[c], outs[a].at[c], send_sems, recv_sems, base + a, (x, y, 1 - c)) for a in range(n)]

    return _Plan(bufs, [jax.ShapeDtypeStruct(b.shape, b.dtype) for b in bufs], {a: a for a in range(n)}, n, build)


def _carried_call(kern, plan, *, name, grid, in_specs, out_specs, out_shape, operands, scratch_shapes=(),
                  aliases=None, sem=None):
    n_in, n_out, n_sc = len(operands), len(out_shape), len(scratch_shapes)
    aliases = dict(aliases or {})
    if plan is None:
        outs = pl.pallas_call(kern, name=name, grid=grid, in_specs=in_specs, out_specs=out_specs, out_shape=out_shape,
                              scratch_shapes=list(scratch_shapes), input_output_aliases=aliases,
                              compiler_params=_params(sem))(*operands)
        return list(outs), []
    ci, co = len(plan.ins), len(plan.out_shapes)
    for k, v in plan.aliases.items():
        aliases[n_in + k] = n_out + v
    steps = tuple(grid)

    def body(*refs):
        ins, cins = refs[:n_in], refs[n_in:n_in + ci]
        outs = refs[n_in + ci:n_in + ci + n_out]
        couts = refs[n_in + ci + n_out:n_in + ci + n_out + co]
        scratch = refs[n_in + ci + n_out + co:n_in + ci + n_out + co + n_sc]
        send_sems, recv_sems = refs[-2], refs[-1]
        first = functools.reduce(jnp.logical_and, [pl.program_id(a) == 0 for a in range(len(steps))])
        last = functools.reduce(jnp.logical_and, [pl.program_id(a) == steps[a] - 1 for a in range(len(steps))])

        @pl.when(first)
        def _():
            for cp in plan.build(cins, couts, send_sems, recv_sems, 0):
                cp.start()

        kern(*ins, *outs, *scratch)

        @pl.when(last)
        def _():
            for cp in plan.build(cins, couts, send_sems, recv_sems, 0):
                cp.wait()

    anyspec = pl.BlockSpec(memory_space=pl.ANY)
    outs = pl.pallas_call(
        body, name=name, grid=grid, in_specs=list(in_specs) + [anyspec] * ci,
        out_specs=list(out_specs) + [anyspec] * co, out_shape=list(out_shape) + plan.out_shapes,
        scratch_shapes=list(scratch_shapes) + [pltpu.SemaphoreType.DMA((plan.n_sems,)),
                                               pltpu.SemaphoreType.DMA((plan.n_sems,))],
        input_output_aliases=aliases,
        compiler_params=_params(("arbitrary",) * len(steps)),
    )(*operands, *plan.ins)
    return list(outs[:n_out]), list(outs[n_out:])


def _chip_rel(x, y, r):
    px = x if r < 2 else 1 - x
    py = y if r % 2 == 0 else 1 - y
    return px, py, 2 * px + py


def _allgather_split(arrs, name):
    n = len(arrs)
    slots = 7

    def body(*refs):
        ins, outs = refs[:n], refs[n:2 * n]
        send_sems, recv_sems = refs[2 * n:]
        x, y, c = _place()
        me = 2 * x + y
        sib = (x, y, 1 - c)

        def half(a, shard, hc):
            h = ins[a].shape[0] // 2
            return outs[a].at[shard, pl.ds(hc * h, h), :]

        def src_half(a):
            h = ins[a].shape[0] // 2
            return ins[a].at[pl.ds(c * h, h), :]

        def copy(a, slot, src, dst, to):
            return pltpu.make_async_remote_copy(src_ref=src, dst_ref=dst, send_sem=send_sems.at[a * slots + slot],
                                                recv_sem=recv_sems.at[a * slots + slot], device_id=to,
                                                device_id_type=MESH)

        first = []
        for r in (1, 2, 3):
            px, py, _ = _chip_rel(x, y, r)
            for a in range(n):
                first.append(copy(a, r - 1, src_half(a), half(a, me, c), (px, py, c)))
        own = [copy(a, 6, ins[a], outs[a].at[me], sib) for a in range(n)]
        for cp in first + own:
            cp.start()
        passed = []
        for r in (1, 2, 3):
            _, _, shard = _chip_rel(x, y, r)
            for a in range(n):
                copy(a, r - 1, src_half(a), half(a, shard, c), sib).wait_recv()
                cp = copy(a, 3 + r - 1, half(a, shard, c), half(a, shard, c), sib)
                cp.start()
                passed.append(cp)
        for r in (1, 2, 3):
            _, _, shard = _chip_rel(x, y, r)
            for a in range(n):
                copy(a, 3 + r - 1, src_half(a), half(a, shard, 1 - c), sib).wait_recv()
        for cp in own:
            cp.wait_recv()
        for cp in first + passed + own:
            cp.wait_send()

    anyspec = pl.BlockSpec(memory_space=pl.ANY)
    return pl.pallas_call(
        body, name=name, in_specs=[anyspec] * n, out_specs=[anyspec] * n,
        out_shape=[jax.ShapeDtypeStruct((N_CHIPS,) + a.shape, a.dtype) for a in arrs],
        scratch_shapes=[pltpu.SemaphoreType.DMA((slots * n,)), pltpu.SemaphoreType.DMA((slots * n,))],
        compiler_params=pltpu.CompilerParams(has_side_effects=True),
    )(*arrs)


def _allgather_whole(arr, name):
    def body(in_ref, out_ref, send_sems, recv_sems, loc_sem):
        x, y, c = _place()
        me = 2 * x + y
        local = pltpu.make_async_copy(in_ref, out_ref.at[me], loc_sem)
        local.start()
        sends = []
        for r in (1, 2, 3):
            px, py, _ = _chip_rel(x, y, r)
            sends.append(pltpu.make_async_remote_copy(
                src_ref=in_ref, dst_ref=out_ref.at[me], send_sem=send_sems.at[r - 1], recv_sem=recv_sems.at[r - 1],
                device_id=(px, py, c), device_id_type=MESH))
        for cp in sends:
            cp.start()
        for r in (1, 2, 3):
            px, py, shard = _chip_rel(x, y, r)
            pltpu.make_async_remote_copy(
                src_ref=in_ref, dst_ref=out_ref.at[shard], send_sem=send_sems.at[r - 1], recv_sem=recv_sems.at[r - 1],
                device_id=(px, py, c), device_id_type=MESH).wait_recv()
        for cp in sends:
            cp.wait_send()
        local.wait()

    anyspec = pl.BlockSpec(memory_space=pl.ANY)
    return pl.pallas_call(
        body, name=name, in_specs=[anyspec], out_specs=anyspec,
        out_shape=jax.ShapeDtypeStruct((N_CHIPS,) + arr.shape, arr.dtype),
        scratch_shapes=[pltpu.SemaphoreType.DMA((3,)), pltpu.SemaphoreType.DMA((3,)), pltpu.SemaphoreType.DMA],
        compiler_params=pltpu.CompilerParams(has_side_effects=True),
    )(arr)


def _gather_all_devices(buf, name):
    def body(in_ref, out_ref, send_sems, recv_sems, loc_sem):
        x, y, c = _place()
        me = 4 * x + 2 * y + c
        local = pltpu.make_async_copy(in_ref, out_ref.at[me], loc_sem)
        local.start()
        sends = []
        for r in range(1, N_DEV):
            px, py, _ = _chip_rel(x, y, r // 2)
            pc = c if r % 2 == 0 else 1 - c
            sends.append(pltpu.make_async_remote_copy(
                src_ref=in_ref, dst_ref=out_ref.at[me], send_sem=send_sems.at[r - 1], recv_sem=recv_sems.at[r - 1],
                device_id=(px, py, pc), device_id_type=MESH))
        for cp in sends:
            cp.start()
        for r in range(1, N_DEV):
            px, py, _ = _chip_rel(x, y, r // 2)
            pc = c if r % 2 == 0 else 1 - c
            pltpu.make_async_remote_copy(
                src_ref=in_ref, dst_ref=out_ref.at[4 * px + 2 * py + pc], send_sem=send_sems.at[r - 1],
                recv_sem=recv_sems.at[r - 1], device_id=(px, py, pc), device_id_type=MESH).wait_recv()
        for cp in sends:
            cp.wait_send()
        local.wait()

    anyspec = pl.BlockSpec(memory_space=pl.ANY)
    return pl.pallas_call(
        body, name=name, in_specs=[anyspec], out_specs=anyspec,
        out_shape=jax.ShapeDtypeStruct((N_DEV,) + buf.shape, buf.dtype),
        scratch_shapes=[pltpu.SemaphoreType.DMA((N_DEV - 1,)), pltpu.SemaphoreType.DMA((N_DEV - 1,)),
                        pltpu.SemaphoreType.DMA],
        compiler_params=pltpu.CompilerParams(has_side_effects=True),
    )(buf)


def _add_half(grad, recv, sel, name):
    _, _, rh, cw = grad.shape
    rb = _pick(rh, (512, 256, 176, 128, 64, 32, 16, 8))

    def kern(sel_ref, g_ref, r_ref, o_ref):
        o_ref[...] = (g_ref[...] + r_ref[...]).astype(BF16)

    return pl.pallas_call(
        kern, name=name,
        grid_spec=pltpu.PrefetchScalarGridSpec(
            num_scalar_prefetch=1, grid=(N_CHIPS, rh // rb),
            in_specs=[pl.BlockSpec((None, None, rb, cw), lambda j, i, s: (j, s[0], i, 0)),
                      pl.BlockSpec((None, rb, cw), lambda j, i, s: (j, i, 0))],
            out_specs=pl.BlockSpec((None, rb, cw), lambda j, i, s: (j, i, 0))),
        out_shape=jax.ShapeDtypeStruct((N_CHIPS, rh, cw), BF16),
        compiler_params=_params(("parallel", "parallel")),
    )(sel, grad, recv)


def _add_own(grad, recv, got, sel, name):
    _, _, rh, cw = grad.shape
    rb = _pick(rh, (512, 256, 176, 128, 64, 32, 16, 8))

    def kern(sel_ref, g_ref, r_ref, b_ref, o_ref):
        own = g_ref[...] + r_ref[...]
        o_ref[...] = ((own + b_ref[0].astype(F32)) + b_ref[1].astype(F32)) + b_ref[2].astype(F32)

    return pl.pallas_call(
        kern, name=name,
        grid_spec=pltpu.PrefetchScalarGridSpec(
            num_scalar_prefetch=1, grid=(rh // rb,),
            in_specs=[pl.BlockSpec((None, None, rb, cw), lambda i, s: (s[1], s[0], i, 0)),
                      pl.BlockSpec((None, rb, cw), lambda i, s: (s[1], i, 0)),
                      pl.BlockSpec((3, rb, cw), lambda i, s: (0, i, 0))],
            out_specs=pl.BlockSpec((None, rb, cw), lambda i, s: (s[0], i, 0))),
        out_shape=jax.ShapeDtypeStruct((2, rh, cw), F32),
        compiler_params=_params(("parallel",)),
    )(sel, grad, recv, got)


def _stacked(g):
    return g.reshape(1, g.shape[0] * g.shape[1], g.shape[2])


def _halves(g):
    g = g.reshape(N_CHIPS, g.shape[0] * g.shape[1] // N_CHIPS, g.shape[2])
    return g.reshape(N_CHIPS, 2, g.shape[1] // 2, g.shape[2])


def _whole(f):
    return f.reshape(f.shape[0] * f.shape[1], f.shape[2])


def _step(x2, tgt, xb, sh, sm, w, mom, var):
    x, y, c = _place()
    sel = jnp.stack([c, 2 * x + y]).astype(jnp.int32)
    wg = {}
    wg["hg_in"] = _allgather_split([sh["hg_in"]], "gather_hg_in")[0]
    proj, landed = _matmul(xb, wg["hg_in"], mode="nn", nsh=N_CHIPS, name="hg_in",
                           comm=_plan_gather_ici([sh["hg_out"], sh["dn0"]]))
    (yhg, o_raw, states), got = _hgrn2_fwd(
        proj, sm["lb_logits"], sm["hg_norm_g"], "hgrn2_fwd",
        comm=_merge(_plan_gather_pass(landed), _plan_gather_ici([sh[k] for k in ("up0", "sg_in", "sg_out")])))
    wg["hg_out"], wg["dn0"], landed = got[0], got[1], got[2:]
    mixed, got = _matmul(yhg, _stacked(wg["hg_out"]), mode="nn", nsh=1, name="hg_out", comm=_plan_gather_pass(landed))
    wg["up0"], wg["sg_in"], wg["sg_out"] = got
    xin1, h1, h1b = _res_ln_fwd(x2, mixed, sm["ln1_g"][0:1], sm["ln1_b"][0:1], "l0_ln1")
    u0, landed = _matmul(h1b, wg["up0"], mode="nn", nsh=N_CHIPS, name="l0_ffn_up", comm=_plan_gather_ici([sh["up1"]]))
    gact0 = _conv_gate_fwd(u0, sm["conv_w"][0], sm["conv_b"][0:1], "l0_ffn_gate")
    ffn, got = _matmul(gact0, _stacked(wg["dn0"]), mode="nn", nsh=1, name="l0_ffn_down",
                       comm=_merge(_plan_gather_pass(landed), _plan_gather_ici([sh["dn1"]])))
    wg["up1"], landed = got[0], got[1:]
    xin2, h2, h2b = _res_ln_fwd(h1, ffn, sm["ln2_g"][0:1], sm["ln2_b"][0:1], "l0_ffn_ln")
    pre, got = _matmul(h2b, wg["sg_in"], mode="nn", nsh=N_CHIPS, name="sg_in", comm=_plan_gather_pass(landed))
    wg["dn1"] = got[0]
    ysg = _sg_gate_fwd(pre, sm["sg_ln_g"], sm["sg_ln_b"], sm["sg_w_s"], sm["sg_b_s_t"], "sg_gate")
    mixed = _matmul(ysg, _stacked(wg["sg_out"]), mode="nn", nsh=1, name="sg_out")
    xin3, h3, h3b = _res_ln_fwd(h2, mixed, sm["ln1_g"][1:2], sm["ln1_b"][1:2], "l1_ln1")
    u1 = _matmul(h3b, wg["up1"], mode="nn", nsh=N_CHIPS, name="l1_ffn_up")
    gact1 = _conv_gate_fwd(u1, sm["conv_w"][1], sm["conv_b"][1:2], "l1_ffn_gate")
    ffn = _matmul(gact1, _stacked(wg["dn1"]), mode="nn", nsh=1, name="l1_ffn_down")
    xin4, _, _ = _res_ln_fwd(h3, ffn, sm["ln2_g"][1:2], sm["ln2_b"][1:2], "l1_ffn_ln")

    gs, grad, split, recv, part = {}, {}, {}, {}, {}

    def swap_on(call, keys):
        for k in keys:
            split[k] = _halves(grad[k])
        out, got = call(_plan_swap([split[k] for k in keys]))
        for k, r in zip(keys, got):
            recv[k] = r
            part[k] = _add_half(split[k], r, sel, f"rs_addhalf_{k}")
        return out

    def ffn_bwd(u, gact, hb_in, dxin, dxin_b, w_up, w_down, layer, tag, up, down, waiting):
        dgact = _matmul(dxin_b, _stacked(w_down), mode="nt", nsh=1, name=f"{tag}_ddown")
        grad[down] = _matmul(gact, dxin_b, mode="tn", nsh=1, name=f"{tag}_wdown")
        da, db, dcw, dcb = _conv_gate_bwd(u, dgact, sm["conv_w"][layer], sm["conv_b"][layer:layer + 1], f"{tag}_dgate")
        grad[up] = swap_on(lambda plan: _matmul(hb_in, [da, db], mode="tn", nsh=N_CHIPS, name=f"{tag}_wup", comm=plan),
                           waiting + [down])
        dh = swap_on(lambda plan: _matmul([da, db], w_up, mode="nt", nsh=N_CHIPS, resid=dxin, alpha=ALPHA,
                                          name=f"{tag}_dup", comm=plan), [up])
        return dh, dcw, dcb

    dx, dxb, dg4, db4, loss = _ln_bwd(xin4, tgt, sm["ln2_g"][1:2], sm["ln2_b"][1:2], "l1_ln2_bwd", loss_head=True)
    dh3, dcw1, dcb1 = ffn_bwd(u1, gact1, h3b, dx, dxb, wg["up1"], wg["dn1"], 1, "l1_ffn", "up1", "dn1", [])
    dx, dxb, dg3, db3 = _ln_bwd(xin3, dh3, sm["ln1_g"][1:2], sm["ln1_b"][1:2], "l1_ln1_bwd")
    grad["sg_out"] = _matmul(ysg, dxb, mode="tn", nsh=1, name="sg_wout")
    dysg = swap_on(lambda plan: _matmul(dxb, _stacked(wg["sg_out"]), mode="nt", nsh=1, name="sg_dout", comm=plan),
                   ["sg_out"])
    dpre, gs["sg_w_s"], gs["sg_b_s_t"], gs["sg_ln_g"], gs["sg_ln_b"] = _sg_gate_bwd(
        pre, dysg, sm["sg_ln_g"], sm["sg_ln_b"], sm["sg_w_s"], sm["sg_b_s_t"], "sg_gate_bwd")
    grad["sg_in"] = _matmul(h2b, dpre, mode="tn", nsh=N_CHIPS, name="sg_win")
    dh2 = _matmul(dpre, wg["sg_in"], mode="nt", nsh=N_CHIPS, resid=dx, alpha=ALPHA, name="sg_din")
    dx, dxb, dg2, db2 = _ln_bwd(xin2, dh2, sm["ln2_g"][0:1], sm["ln2_b"][0:1], "l0_ln2_bwd")
    dh1, dcw0, dcb0 = ffn_bwd(u0, gact0, h1b, dx, dxb, wg["up0"], wg["dn0"], 0, "l0_ffn", "up0", "dn0", ["sg_in"])
    dx, dxb, dg1, db1 = _ln_bwd(xin1, dh1, sm["ln1_g"][0:1], sm["ln1_b"][0:1], "l0_ln1_bwd")
    grad["hg_out"] = _matmul(yhg, dxb, mode="tn", nsh=1, name="hg_wout")
    dyhg = swap_on(lambda plan: _matmul(dxb, _stacked(wg["hg_out"]), mode="nt", nsh=1, name="hg_dout", comm=plan),
                   ["hg_out"])
    early = ("dn1", "up1", "sg_out", "sg_in", "dn0", "up0", "hg_out")
    dparts, got = _hgrn2_bwd(proj, sm["lb_logits"], sm["hg_norm_g"], o_raw, states, dyhg, "hgrn2_bwd",
                             comm=_plan_scatter([part[k] for k in early]))
    gs["lb"], gs["hg_norm_g"] = dparts[4], dparts[5]
    mine = [_add_own(split[k], recv[k], b, sel, f"rs_addown_{k}") for k, b in zip(early, got)]
    grad["hg_in"], full = _matmul(xb, list(dparts[:4]), mode="tn", nsh=N_CHIPS, name="hg_win", comm=_plan_join(mine))
    red = {k: _whole(f) for k, f in zip(early, full)}
    gx = swap_on(lambda plan: _matmul(dparts[0], wg["hg_in"], mode="nt", nsh=1, b_off=0, resid=dx, alpha=ALPHA,
                                      name="hg_din_q", comm=plan), ["hg_in"])
    gx, got = _matmul(list(dparts[1:4]), wg["hg_in"], mode="nt", nsh=3, b_off=1, resid=gx, alpha=1.0, name="hg_din_fig",
                      comm=_plan_scatter([part["hg_in"]]))
    mine = _add_own(split["hg_in"], recv["hg_in"], got[0], sel, "rs_addown_hg_in")
    upd = {}
    upd["hg_w_out"], full = _adamw(w["hg_w_out"], [red["hg_out"]], mom["hg_w_out"], var["hg_w_out"], "adamw_hg_w_out",
                                   comm=_plan_join([mine]))
    red["hg_in"] = _whole(full[0])
    for k, src in (("ffn_w_up", ("up0", "up1")), ("ffn_w_down", ("dn0", "dn1")), ("sg_w_in", ("sg_in",)),
                   ("sg_w_out", ("sg_out",)), ("hg_w_in", ("hg_in",))):
        upd[k] = _adamw(w[k], [red[s] for s in src], mom[k], var[k], f"adamw_{k}")
    gs["ln1_g"] = jnp.concatenate([dg1, dg3], axis=0)
    gs["ln1_b"] = jnp.concatenate([db1, db3], axis=0)
    gs["ln2_g"] = jnp.concatenate([dg2, dg4], axis=0)
    gs["ln2_b"] = jnp.concatenate([db2, db4], axis=0)
    gs["conv_w"] = jnp.stack([dcw0, dcw1], axis=0)
    gs["conv_b"] = jnp.concatenate([dcb0, dcb1], axis=0)
    return loss, gx, gs, upd


_SMALL_ORDER = ("lb", "hg_norm_g", "sg_w_s", "sg_b_s_t", "conv_b", "ln1_g", "ln1_b", "ln2_g", "ln2_b",
                "conv_w", "sg_ln_g", "sg_ln_b")


PACK_ROWS = 512


def _pack(parts):
    flat, layout, off = [], [], 0
    for k in _SMALL_ORDER:
        a = parts[k]
        n = a.size
        pad = (-n) % LANES
        flat.append(jnp.pad(a.reshape(-1), (0, pad)))
        layout.append((k, off, n, a.shape))
        off += n + pad
    flat.append(jnp.zeros(((-off) % (PACK_ROWS * LANES),), F32))
    return jnp.concatenate(flat).reshape(-1, LANES), layout


def _unpack(buf, layout):
    flat = buf.reshape(-1)
    return {k: flat[off:off + n].reshape(shape) for k, off, n, shape in layout}


def kernel(x, lb_logits, hg_w_in, hg_norm_g, hg_w_out, sg_w_in, sg_ln_g, sg_ln_b, sg_w_s, sg_b_s, sg_w_out, ffn_w_up, ffn_conv_w, ffn_conv_b, ffn_w_down, ln1_g, ln1_b, ln2_g, ln2_b, loss_target, m_lb_logits, m_hg_w_in, m_hg_norm_g, m_hg_w_out, m_sg_w_in, m_sg_ln_g, m_sg_ln_b, m_sg_w_s, m_sg_b_s, m_sg_w_out, m_ffn_w_up, m_ffn_conv_w, m_ffn_conv_b, m_ffn_w_down, m_ln1_g, m_ln1_b, m_ln2_g, m_ln2_b, v_lb_logits, v_hg_w_in, v_hg_norm_g, v_hg_w_out, v_sg_w_in, v_sg_ln_g, v_sg_ln_b, v_sg_w_s, v_sg_b_s, v_sg_w_out, v_ffn_w_up, v_ffn_conv_w, v_ffn_conv_b, v_ffn_w_down, v_ln1_g, v_ln1_b, v_ln2_g, v_ln2_b):
    names = ("lb_logits", "hg_w_in", "hg_norm_g", "hg_w_out", "sg_w_in", "sg_ln_g", "sg_ln_b", "sg_w_s", "sg_b_s",
             "sg_w_out", "ffn_w_up", "ffn_conv_w", "ffn_conv_b", "ffn_w_down", "ln1_g", "ln1_b", "ln2_g", "ln2_b")
    w = dict(zip(names, (lb_logits, hg_w_in, hg_norm_g, hg_w_out, sg_w_in, sg_ln_g, sg_ln_b, sg_w_s, sg_b_s,
                         sg_w_out, ffn_w_up, ffn_conv_w, ffn_conv_b, ffn_w_down, ln1_g, ln1_b, ln2_g, ln2_b)))
    mom = dict(zip(names, (m_lb_logits, m_hg_w_in, m_hg_norm_g, m_hg_w_out, m_sg_w_in, m_sg_ln_g, m_sg_ln_b, m_sg_w_s,
                           m_sg_b_s, m_sg_w_out, m_ffn_w_up, m_ffn_conv_w, m_ffn_conv_b, m_ffn_w_down, m_ln1_g,
                           m_ln1_b, m_ln2_g, m_ln2_b)))
    var = dict(zip(names, (v_lb_logits, v_hg_w_in, v_hg_norm_g, v_hg_w_out, v_sg_w_in, v_sg_ln_g, v_sg_ln_b, v_sg_w_s,
                           v_sg_b_s, v_sg_w_out, v_ffn_w_up, v_ffn_conv_w, v_ffn_conv_b, v_ffn_w_down, v_ln1_g,
                           v_ln1_b, v_ln2_g, v_ln2_b)))
    x2, tgt = x[0], loss_target[0]
    d = x2.shape[1]
    fq = ffn_conv_w.shape[2]
    dq = sg_ln_g.shape[1]
    cx, cy, _ = _place()
    me = 2 * cx + cy

    shards = {"hg_in": hg_w_in[0], "hg_out": hg_w_out[0], "sg_in": sg_w_in[0], "sg_out": sg_w_out[0],
              "up0": ffn_w_up[0], "up1": ffn_w_up[1], "dn0": ffn_w_down[0], "dn1": ffn_w_down[1]}
    shards = {k: v.astype(BF16) for k, v in shards.items()}
    wide = max(fq, dq)
    tiny = jnp.concatenate([jnp.pad(ffn_conv_w.reshape(6, fq), ((0, 0), (0, wide - fq))),
                            jnp.pad(sg_ln_g, ((0, 0), (0, wide - dq))),
                            jnp.pad(sg_ln_b, ((0, 0), (0, wide - dq)))], axis=0)
    tiny_all = _allgather_whole(tiny, "gather_small")
    conv_w_full = jnp.transpose(tiny_all[:, 0:6, :fq].reshape(N_CHIPS, 2, 3, fq), (1, 2, 0, 3)).reshape(2, 3, N_CHIPS * fq)
    sm = {"lb_logits": lb_logits, "hg_norm_g": hg_norm_g, "ln1_g": ln1_g, "ln1_b": ln1_b, "ln2_g": ln2_g,
          "ln2_b": ln2_b, "conv_w": conv_w_full, "conv_b": ffn_conv_b,
          "sg_ln_g": tiny_all[:, 6, :dq].reshape(1, N_CHIPS * dq),
          "sg_ln_b": tiny_all[:, 7, :dq].reshape(1, N_CHIPS * dq),
          "sg_w_s": sg_w_s[0], "sg_b_s_t": jnp.transpose(sg_b_s[0])}

    loss_row, grad_x, gs, upd = _step(x2, tgt, x2.astype(BF16), shards, sm, w, mom, var)
    loss = lax.psum(loss_row[0, 0], ("x", "y", "c"))

    packed, layout = _pack(gs)
    summed = _unpack(_sum_leading(_gather_all_devices(packed, "gather_small_grads"), "sum_small_grads"), layout)

    grads = {
        "lb_logits": _lb_logits_grad(lb_logits, summed["lb"], "lb_logits_grad"),
        "hg_norm_g": summed["hg_norm_g"],
        "sg_ln_g": lax.dynamic_slice_in_dim(summed["sg_ln_g"], me * dq, dq, axis=1),
        "sg_ln_b": lax.dynamic_slice_in_dim(summed["sg_ln_b"], me * dq, dq, axis=1),
        "sg_w_s": summed["sg_w_s"][None], "sg_b_s": jnp.transpose(summed["sg_b_s_t"])[None],
        "ffn_conv_w": lax.dynamic_slice_in_dim(summed["conv_w"], me * fq, fq, axis=2),
        "ffn_conv_b": summed["conv_b"],
        "ln1_g": summed["ln1_g"], "ln1_b": summed["ln1_b"], "ln2_g": summed["ln2_g"], "ln2_b": summed["ln2_b"],
    }

    delta, new_m, new_v = {}, {}, {}
    for k, (dlt, mm, vv, gg) in upd.items():
        delta[k], new_m[k], new_v[k], grads[k] = dlt, mm, vv, gg
    small_names = [k for k in names if k not in upd]

    def pack_small(src):
        flat = [src[k].reshape(-1) for k in small_names]
        n = sum(a.size for a in flat)
        flat.append(jnp.zeros(((-n) % (PACK_ROWS * LANES),), F32))
        return jnp.concatenate(flat).reshape(1, -1, LANES)

    outs = _adamw(pack_small(w), [pack_small(grads)[0]], pack_small(mom), pack_small(var), "adamw_small")
    off = 0
    for k in small_names:
        n = w[k].size
        for dst, o in zip((delta, new_m, new_v), outs):
            dst[k] = o.reshape(-1)[off:off + n].reshape(w[k].shape)
        off += n

    return (loss, grad_x[None], *[grads[k] for k in names], *[delta[k] for k in names],
            *[new_m[k] for k in names], *[new_v[k] for k in names])
```

```python
import functools

import jax
import jax.numpy as jnp
from jax import lax
from jax.experimental import pallas as pl
from jax.experimental.pallas import tpu as pltpu

F32 = jnp.float32
BF16 = jnp.bfloat16
HI = lax.Precision.HIGHEST
MESH = pl.DeviceIdType.MESH

ALPHA = (2 * 2) ** 0.25
LN_EPS = 1e-5
RMS_EPS = 1e-6
ADAM_LR, ADAM_B1, ADAM_B2, ADAM_EPS, ADAM_WD, ADAM_STEP = 0.001, 0.9, 0.999, 1e-08, 0.01, 10

LANES = 128
SUB = 16
GCHUNK = 128
VMEM_LIMIT = 56 * 1024 * 1024
N_CHIPS = 4
N_DEV = 8

NT = (((1,), (1,)), ((), ()))
TN = (((0,), (0,)), ((), ()))
NN = (((1,), (0,)), ((), ()))


def _pick(dim, prefs):
    for p in prefs:
        if dim % p == 0:
            return p
    return dim


def _params(sem=None, **kw):
    return pltpu.CompilerParams(dimension_semantics=sem, vmem_limit_bytes=VMEM_LIMIT, **kw)


def _sigmoid_pair(x):
    e = jnp.exp(-jnp.abs(x))
    inv = 1.0 / (1.0 + e)
    pos = x >= 0
    return jnp.where(pos, inv, e * inv), jnp.where(pos, e * inv, inv)


def _ln_hat(x):
    mu = jnp.mean(x, axis=-1, keepdims=True)
    xc = x - mu
    var = jnp.mean(xc * xc, axis=-1, keepdims=True)
    rstd = lax.rsqrt(var + LN_EPS)
    return xc * rstd, rstd


def _lower_bound(logits):
    m = jnp.max(logits, axis=0, keepdims=True)
    e = jnp.exp(logits - m)
    return e[0:1, :] / jnp.sum(e, axis=0, keepdims=True)


MATMUL_VMEM_BUDGET = 40 * 1024 * 1024


def _fit_bk(kdim, bm, bn, out_dtype, has_resid, na=1, nb=1):
    fixed = bm * bn * (4 + 2 * jnp.dtype(out_dtype).itemsize + (8 if has_resid else 0))
    best = LANES
    for bk in range(LANES, kdim + 1, LANES):
        if kdim % bk == 0 and fixed + 4 * bk * (bm * na + bn * nb) <= MATMUL_VMEM_BUDGET:
            best = bk
    return best


def _fit_bm_bk(mdim, prefs, kdim, bn, out_dtype, has_resid, na=1, nb=1):
    best = None
    fits = [bm for bm in prefs if mdim % bm == 0][:2] or [mdim]
    for bm in fits:
        bk = _fit_bk(kdim, bm, bn, out_dtype, has_resid, na, nb)
        if best is None or kdim // bk < kdim // best[1]:
            best = (bm, bk)
    return best


def _matmul(a, b, *, mode, name, out_dtype=F32, resid=None, alpha=1.0, b_off=0, nsh=None, comm=None):
    a_parts = a if isinstance(a, (list, tuple)) else [a]
    b_parts = b if isinstance(b, (list, tuple)) else [b]
    n_parts = max(len(a_parts), len(b_parts))
    if mode == "nn":
        m, kdim = a.shape
        _, _, ns = b.shape
        bn = _pick(ns, (1024, 1408, 512, 256, 128))
        bm, bk = _fit_bm_bk(m, (1024, 512, 256, 128), kdim, bn, out_dtype, resid is not None)
        nps = ns // bn
        grid = (m // bm, nsh * nps, kdim // bk)
        a_specs = [pl.BlockSpec((bm, bk), lambda i, j, k: (i, k))]
        b_specs = [pl.BlockSpec((None, bk, bn), lambda i, j, k: (b_off + j // nps, k, j % nps))]
        o_spec = pl.BlockSpec((bm, bn), lambda i, j, k: (i, j))
        out_shape = jax.ShapeDtypeStruct((m, nsh * ns), out_dtype)
        dims, part_axis, per_part = NN, 2, grid[2]
    elif mode == "nt":
        m = a_parts[0].shape[0]
        _, kdim, ns = b.shape
        bn = _pick(kdim, (1024, 1408, 512, 256, 128))
        bm, bk = _fit_bm_bk(m, (1024, 512, 256, 128), ns, bn, out_dtype, resid is not None, na=n_parts)
        kps = ns // bk
        per_part = nsh // n_parts * kps
        grid = (m // bm, kdim // bn, nsh * kps)
        a_specs = [pl.BlockSpec((bm, bk), lambda i, j, k, p=p: (jnp.where(k // per_part == p, i, 0),
                                                                  jnp.where(k // per_part == p, k % per_part, 0)))
                   for p in range(n_parts)]
        b_specs = [pl.BlockSpec((None, bn, bk), lambda i, j, k: (b_off + k // kps, j, k % kps))]
        o_spec = pl.BlockSpec((bm, bn), lambda i, j, k: (i, j))
        out_shape = jax.ShapeDtypeStruct((m, kdim), out_dtype)
        dims, part_axis = NT, 2
    else:
        t, kdim = a.shape
        ns = b_parts[0].shape[1] * n_parts // nsh
        bn = _pick(ns, (1024, 1408, 512, 256, 128))
        bm, bk = _fit_bm_bk(kdim, (1024, 1408, 512, 256, 128), t, bn, out_dtype, resid is not None, nb=n_parts)
        nps = ns // bn
        per_part = nsh // n_parts * nps
        grid = (kdim // bm, nsh * nps, t // bk)
        a_specs = [pl.BlockSpec((bk, bm), lambda i, j, k: (k, i))]
        b_specs = [pl.BlockSpec((bk, bn), lambda i, j, k, p=p: (jnp.where(j // per_part == p, k, 0),
                                                                  jnp.where(j // per_part == p, j % per_part, 0)))
                   for p in range(n_parts)]
        o_spec = pl.BlockSpec((None, bm, bn), lambda i, j, k: (j // nps, i, j % nps))
        out_shape = jax.ShapeDtypeStruct((nsh, kdim, ns), out_dtype)
        dims, part_axis = TN, 1
    nk = grid[2]
    na, nb_ = len(a_parts), len(b_parts)
    has_resid = resid is not None

    def kern(*refs):
        a_refs, b_refs = refs[:na], refs[na:na + nb_]
        r_ref = refs[na + nb_] if has_resid else None
        k = pl.program_id(2)

        def finish(r, o_ref):
            if has_resid:
                r = r + alpha * r_ref[...]
            o_ref[...] = r.astype(o_ref.dtype)

        def add(a_ref, b_ref):
            part = lax.dot_general(a_ref[...], b_ref[...], dims, preferred_element_type=F32)
            if nk == 1:
                finish(part, refs[-1])
                return
            acc_ref = refs[-1]

            @pl.when(k == 0)
            def _():
                acc_ref[...] = part

            @pl.when(k > 0)
            def _():
                acc_ref[...] += part

        if n_parts == 1:
            add(a_refs[0], b_refs[0])
        else:
            which = pl.program_id(part_axis) // per_part
            for p in range(n_parts):
                pl.when(which == p)(functools.partial(add, a_refs[min(p, na - 1)], b_refs[min(p, nb_ - 1)]))
        if nk > 1:
            @pl.when(k == nk - 1)
            def _():
                finish(refs[-1][...], refs[-2])

    in_specs = a_specs + b_specs
    operands = list(a_parts) + list(b_parts)
    if has_resid:
        in_specs.append(pl.BlockSpec((bm, bn), lambda i, j, k: (i, j)))
        operands.append(resid)
    outs, carried = _carried_call(
        kern, comm, name=name, grid=grid, in_specs=in_specs, out_specs=[o_spec], out_shape=[out_shape],
        operands=operands, scratch_shapes=[pltpu.VMEM((bm, bn), F32)] if nk > 1 else [],
        sem=("parallel", "parallel", "arbitrary"))
    return outs[0] if comm is None else (outs[0], carried)


def _res_ln_fwd(h_prev, sub, g, b, name):
    t, d = h_prev.shape
    tb = _pick(t, (256, 128, 64, 32, 16))

    def kern(hp_ref, s_ref, g_ref, b_ref, xin_ref, h_ref, hb_ref):
        xin = ALPHA * hp_ref[...] + s_ref[...]
        xhat, _ = _ln_hat(xin)
        h = xhat * g_ref[...] + b_ref[...]
        xin_ref[...] = xin
        h_ref[...] = h
        hb_ref[...] = h.astype(BF16)

    row = pl.BlockSpec((tb, d), lambda i: (i, 0))
    vec = pl.BlockSpec((1, d), lambda i: (0, 0))
    return pl.pallas_call(
        kern, name=name, grid=(t // tb,), in_specs=[row, row, vec, vec], out_specs=[row, row, row],
        out_shape=[jax.ShapeDtypeStruct((t, d), F32), jax.ShapeDtypeStruct((t, d), F32),
                   jax.ShapeDtypeStruct((t, d), BF16)],
        compiler_params=_params(("parallel",)),
    )(h_prev, sub, g, b)


def _ln_bwd(xin, dy_or_target, g, b, name, loss_head=False):
    t, d = xin.shape
    tb = _pick(t, (256, 128, 64, 32, 16))
    nb = t // tb

    def kern(x_ref, dy_ref, g_ref, b_ref, dx_ref, dxb_ref, dg_ref, db_ref, *rest):
        i = pl.program_id(0)
        xhat, rstd = _ln_hat(x_ref[...])
        gv = g_ref[...]
        if loss_head:
            loss_ref = rest[0]
            err = xhat * gv + b_ref[...] - dy_ref[...]
            dy = err * (1.0 / d)
            part = 0.5 * jnp.sum(jnp.sum(err * err, axis=1, keepdims=True), axis=0, keepdims=True) * (1.0 / d)
        else:
            dy = dy_ref[...]

        @pl.when(i == 0)
        def _():
            dg_ref[...] = jnp.zeros_like(dg_ref)
            db_ref[...] = jnp.zeros_like(db_ref)
            if loss_head:
                loss_ref[...] = jnp.zeros_like(loss_ref)

        dg_ref[...] += jnp.sum(dy * xhat, axis=0, keepdims=True)
        db_ref[...] += jnp.sum(dy, axis=0, keepdims=True)
        if loss_head:
            loss_ref[...] += jnp.broadcast_to(part, loss_ref.shape)
        dxh = dy * gv
        m1 = jnp.mean(dxh, axis=-1, keepdims=True)
        m2 = jnp.mean(dxh * xhat, axis=-1, keepdims=True)
        dx = rstd * (dxh - m1 - xhat * m2)
        dx_ref[...] = dx
        dxb_ref[...] = dx.astype(BF16)

    row = pl.BlockSpec((tb, d), lambda i: (i, 0))
    vec = pl.BlockSpec((1, d), lambda i: (0, 0))
    out_specs = [row, row, vec, vec]
    out_shape = [jax.ShapeDtypeStruct((t, d), F32), jax.ShapeDtypeStruct((t, d), BF16),
                 jax.ShapeDtypeStruct((1, d), F32), jax.ShapeDtypeStruct((1, d), F32)]
    if loss_head:
        out_specs.append(pl.BlockSpec((1, LANES), lambda i: (0, 0)))
        out_shape.append(jax.ShapeDtypeStruct((1, LANES), F32))
    return pl.pallas_call(
        kern, name=name, grid=(nb,), in_specs=[row, row, vec, vec], out_specs=out_specs, out_shape=out_shape,
        compiler_params=_params(("arbitrary",)),
    )(xin, dy_or_target, g, b)


def _conv_gate_fwd(u, conv_w, conv_b, name):
    t, f2 = u.shape
    f = f2 // 2
    tb = _pick(t, (512, 256, 128, 64, 32, 16))
    cn = _pick(f, (1408, 1024, 512, 256, 128))
    ncb = f // cn
    hb = tb // 8

    def kern(a_ref, ah_ref, b_ref, w_ref, cb_ref, o_ref):
        i = pl.program_id(0)
        a = a_ref[...]
        halo = jnp.where(i > 0, ah_ref[...], 0.0)
        rid = lax.broadcasted_iota(jnp.int32, a.shape, 0)
        s1 = jnp.where(rid == 0, halo[7:8, :], pltpu.roll(a, 1, 0))
        s2 = jnp.where(rid == 0, halo[6:7, :], jnp.where(rid == 1, halo[7:8, :], pltpu.roll(a, 2, 0)))
        w = w_ref[...]
        conv = w[2:3, :] * a + w[1:2, :] * s1 + w[0:1, :] * s2 + cb_ref[...]
        sp, _ = _sigmoid_pair(conv)
        o_ref[...] = (conv * sp * b_ref[...]).astype(BF16)

    return pl.pallas_call(
        kern, name=name, grid=(t // tb, ncb),
        in_specs=[pl.BlockSpec((tb, cn), lambda i, j: (i, j)),
                  pl.BlockSpec((8, cn), lambda i, j: (jnp.maximum(i * hb - 1, 0), j)),
                  pl.BlockSpec((tb, cn), lambda i, j: (i, j + ncb)),
                  pl.BlockSpec((3, cn), lambda i, j: (0, j)),
                  pl.BlockSpec((1, cn), lambda i, j: (0, j))],
        out_specs=pl.BlockSpec((tb, cn), lambda i, j: (i, j)),
        out_shape=jax.ShapeDtypeStruct((t, f), BF16),
        compiler_params=_params(("parallel", "parallel")),
    )(u, u, u, conv_w, conv_b)


def _conv_gate_bwd(u, dgact, conv_w, conv_b, name):
    t, f2 = u.shape
    f = f2 // 2
    tb = _pick(t, (512, 256, 128, 64, 32, 16))
    cn = _pick(f, (1408, 1024, 512, 256, 128))
    ncb = f // cn
    hb = tb // 8
    nb = t // tb
    last8 = t // 8 - 1

    def kern(a_ref, ap_ref, an_ref, b_ref, bn_ref, dg_ref, dgn_ref, w_ref, cb_ref,
             da_ref, db_ref, dw_ref, dcb_ref):
        i = pl.program_id(1)
        a = a_ref[...]
        w = w_ref[...]
        ext = jnp.concatenate([jnp.where(i > 0, ap_ref[...], 0.0), a, an_ref[...]], axis=0)
        e1 = pltpu.roll(ext, 1, 0)
        e2 = pltpu.roll(ext, 2, 0)
        conv = (w[2:3, :] * ext + w[1:2, :] * e1 + w[0:1, :] * e2 + cb_ref[...])[8:, :]
        bmn = jnp.concatenate([b_ref[...], bn_ref[...]], axis=0)
        dgmn = jnp.concatenate([dg_ref[...], jnp.where(i < nb - 1, dgn_ref[...], 0.0)], axis=0)
        sp, sn = _sigmoid_pair(conv)
        da = dgmn * bmn * (sp * (1.0 + conv * sn))
        n = tb + 8
        dap = w[2:3, :] * da + w[1:2, :] * pltpu.roll(da, n - 1, 0) + w[0:1, :] * pltpu.roll(da, n - 2, 0)
        da_ref[...] = dap[:tb, :].astype(BF16)
        db_ref[...] = (dg_ref[...] * (conv * sp)[:tb, :]).astype(BF16)
        dam = da[:tb, :]

        @pl.when(i == 0)
        def _():
            dw_ref[...] = jnp.zeros_like(dw_ref)
            dcb_ref[...] = jnp.zeros_like(dcb_ref)

        dw = jnp.concatenate([jnp.sum(dam * e2[8:8 + tb, :], axis=0, keepdims=True),
                              jnp.sum(dam * e1[8:8 + tb, :], axis=0, keepdims=True),
                              jnp.sum(dam * a, axis=0, keepdims=True)], axis=0)
        dw_ref[...] += dw
        dcb_ref[...] += jnp.sum(dam, axis=0, keepdims=True)

    main_a = pl.BlockSpec((tb, cn), lambda j, i: (i, j))
    prev_a = pl.BlockSpec((8, cn), lambda j, i: (jnp.maximum(i * hb - 1, 0), j))
    next_a = pl.BlockSpec((8, cn), lambda j, i: (jnp.minimum((i + 1) * hb, last8), j))
    main_b = pl.BlockSpec((tb, cn), lambda j, i: (i, j + ncb))
    next_b = pl.BlockSpec((8, cn), lambda j, i: (jnp.minimum((i + 1) * hb, last8), j + ncb))
    return pl.pallas_call(
        kern, name=name, grid=(ncb, nb),
        in_specs=[main_a, prev_a, next_a, main_b, next_b, main_a, next_a,
                  pl.BlockSpec((3, cn), lambda j, i: (0, j)), pl.BlockSpec((1, cn), lambda j, i: (0, j))],
        out_specs=[main_a, main_a, pl.BlockSpec((3, cn), lambda j, i: (0, j)),
                   pl.BlockSpec((1, cn), lambda j, i: (0, j))],
        out_shape=[jax.ShapeDtypeStruct((t, f), BF16), jax.ShapeDtypeStruct((t, f), BF16),
                   jax.ShapeDtypeStruct((3, f), F32), jax.ShapeDtypeStruct((1, f), F32)],
        compiler_params=_params(("parallel", "arbitrary")),
    )(u, u, u, u, u, dgact, dgact, conv_w, conv_b)


def _hg_gates(qp, fp, lb):
    sq, _ = _sigmoid_pair(qp)
    sf, snf = _sigmoid_pair(fp)
    forget = lb + (1.0 - lb) * sf
    return sq, sf, snf, forget, jnp.log(forget), (1.0 - lb) * snf


def _tri(lower):
    r = lax.broadcasted_iota(jnp.int32, (SUB, SUB), 0)
    c = lax.broadcasted_iota(jnp.int32, (SUB, SUB), 1)
    return ((r >= c) if lower else (r <= c)).astype(BF16)


def _split2(x):
    hi = x.astype(BF16)
    return hi, (x - hi.astype(F32)).astype(BF16)


def _dot3(a, b, dims):
    (ah, al), (bh, bl) = a, b
    return (lax.dot_general(ah, bh, dims, preferred_element_type=F32)
            + (lax.dot_general(ah, bl, dims, preferred_element_type=F32)
               + lax.dot_general(al, bh, dims, preferred_element_type=F32)))


def _running_sum(tri, x):
    hi, lo = _split2(x)
    rest = (x - hi.astype(F32)) - lo.astype(F32)
    return (lax.dot_general(tri, hi, NN, preferred_element_type=F32)
            + (lax.dot_general(tri, lo, NN, preferred_element_type=F32)
               + lax.dot_general(tri, rest.astype(BF16), NN, preferred_element_type=F32)))


HEADS_PER_STEP = 8
STEP_UNROLL = 2


def _hgrn2_fwd(proj, lb_logits, norm_g, name, comm=None):
    t, d4 = proj.shape
    d = d4 // 4
    nh = d // LANES
    hb = _pick(nh, (HEADS_PER_STEP, 2, 1))
    wb = hb * LANES
    tb = _pick(t, (256, 128, 64, 32, 16))
    nb = t // tb
    nsc = tb // SUB

    def kern(q_ref, f_ref, i_ref, g_ref, lbl_ref, ng_ref, y_ref, o_ref, st_ref, s_ref):
        @pl.when(pl.program_id(1) == 0)
        def _():
            s_ref[...] = jnp.zeros_like(s_ref)

        lb_all = _lower_bound(lbl_ref[...])
        ng_all = ng_ref[...]
        ltri = _tri(True)
        rcol = lax.broadcasted_iota(jnp.int32, (SUB, 1), 0)

        heads = [slice(h * LANES, (h + 1) * LANES) for h in range(hb)]

        def step(sc, carry):
            rows = pl.ds(pl.multiple_of(sc * SUB, SUB), SUB)
            qp, fp, v, gp = q_ref[rows, :], f_ref[rows, :], i_ref[rows, :], g_ref[rows, :]
            sq, _, _, _, lf, k = _hg_gates(qp, fp, lb_all)
            q = qp * sq
            bl = _running_sum(ltri, lf)
            bend = bl[SUB - 1:SUB, :]
            dec = jnp.exp(bend)
            qs2 = _split2(q * jnp.exp(bl))
            kd2 = _split2(k * jnp.exp(bend - bl))
            v2 = _split2(v)
            states = [s_ref[h] for h in range(hb)]
            o = [_dot3((qs2[0][:, c], qs2[1][:, c]), _split2(states[h]), NT) for h, c in enumerate(heads)]
            for s in range(SUB):
                e = jnp.exp(jnp.minimum(bl - bl[s:s + 1, :], 0.0))
                p = q * e * k[s:s + 1, :]
                for h, c in enumerate(heads):
                    a = jnp.sum(p[:, c], axis=1, keepdims=True)
                    o[h] = o[h] + jnp.where(rcol >= s, a, 0.0) * v[s:s + 1, c]
            for h, c in enumerate(heads):
                st_ref[sc, h] = states[h]
                s_ref[h] = states[h] * dec[:, c] + _dot3((v2[0][:, c], v2[1][:, c]), (kd2[0][:, c], kd2[1][:, c]), TN)
            o_ref[rows, :] = jnp.concatenate(o, axis=1)
            on = jnp.concatenate(
                [oh * lax.rsqrt(jnp.mean(oh * oh, axis=1, keepdims=True) + RMS_EPS) for oh in o], axis=1)
            sg, _ = _sigmoid_pair(gp)
            y_ref[rows, :] = (on * ng_all * (gp * sg)).astype(BF16)
            return carry

        lax.fori_loop(0, nsc, step, 0, unroll=STEP_UNROLL)

    def col(off):
        return pl.BlockSpec((tb, wb), lambda h, j: (j, h + off * (nh // hb)))

    return _carried_call(
        kern, comm, name=name, grid=(nh // hb, nb),
        in_specs=[col(0), col(1), col(2), col(3),
                  pl.BlockSpec((3, wb), lambda h, j: (0, h)), pl.BlockSpec((1, wb), lambda h, j: (0, h))],
        out_specs=[col(0), col(0), pl.BlockSpec((nsc, hb, LANES, LANES), lambda h, j: (j, h, 0, 0))],
        out_shape=[jax.ShapeDtypeStruct((t, d), BF16), jax.ShapeDtypeStruct((t, d), F32),
                   jax.ShapeDtypeStruct((t // SUB, nh, LANES, LANES), F32)],
        scratch_shapes=[pltpu.VMEM((hb, LANES, LANES), F32)],
        operands=[proj, proj, proj, proj, lb_logits, norm_g], sem=("parallel", "arbitrary"))


def _hgrn2_bwd(proj, lb_logits, norm_g, o_raw, states, dy, name, comm=None):
    t, d4 = proj.shape
    d = d4 // 4
    nh = d // LANES
    hb = _pick(nh, (HEADS_PER_STEP, 2, 1))
    wb = hb * LANES
    tb = _pick(t, (256, 128, 64, 32, 16))
    nb = t // tb
    nsc = tb // SUB

    def kern(q_ref, f_ref, i_ref, g_ref, lbl_ref, ng_ref, o_ref, st_ref, dy_ref,
             dq_ref, df_ref, di_ref, dgp_ref, dlb_ref, dng_ref, ds_ref, gc_ref):
        j = pl.program_id(1)

        @pl.when(j == 0)
        def _():
            ds_ref[...] = jnp.zeros_like(ds_ref)
            gc_ref[...] = jnp.zeros_like(gc_ref)
            dlb_ref[...] = jnp.zeros_like(dlb_ref)
            dng_ref[...] = jnp.zeros_like(dng_ref)

        lb_all = _lower_bound(lbl_ref[...])
        ng_all = ng_ref[...]
        ltri, utri = _tri(True), _tri(False)
        rcol = lax.broadcasted_iota(jnp.int32, (SUB, 1), 0)
        rid = lax.broadcasted_iota(jnp.int32, (SUB, wb), 0)

        heads = [slice(h * LANES, (h + 1) * LANES) for h in range(hb)]

        def per_head(fn):
            return jnp.concatenate([jnp.broadcast_to(fn(c), (SUB, LANES)) for c in heads], axis=1)

        def step(it, carry):
            sc = nsc - 1 - it
            rows = pl.ds(pl.multiple_of(sc * SUB, SUB), SUB)
            qp, fp, v, gp = q_ref[rows, :], f_ref[rows, :], i_ref[rows, :], g_ref[rows, :]
            o, dyv = o_ref[rows, :], dy_ref[rows, :]
            sq, sf, snf, forget, lf, k = _hg_gates(qp, fp, lb_all)
            q = qp * sq
            bl = _running_sum(ltri, lf)
            ebl = jnp.exp(bl)
            bend = bl[SUB - 1:SUB, :]
            dec = jnp.exp(bend)
            dte = jnp.exp(bend - bl)
            r = per_head(lambda c: lax.rsqrt(jnp.mean(o[:, c] * o[:, c], axis=1, keepdims=True) + RMS_EPS))
            ohat = o * r
            sg, sng = _sigmoid_pair(gp)
            don = dyv * (gp * sg)
            dgp_ref[rows, :] = (dyv * (ohat * ng_all) * (sg * (1.0 + gp * sng))).astype(BF16)
            dng_ref[...] += jnp.sum(don * ohat, axis=0, keepdims=True)
            doh = don * ng_all
            dot_oh = doh * ohat
            do = r * (doh - ohat * per_head(lambda c: jnp.mean(dot_oh[:, c], axis=1, keepdims=True)))
            do2, qs2, kd2, v2 = _split2(do), _split2(q * ebl), _split2(k * dte), _split2(v)
            dq_h, dk_h, dv_h = [], [], []
            for h, c in enumerate(heads):
                dstate = ds_ref[h]
                ds2 = _split2(dstate)
                doc = (do2[0][:, c], do2[1][:, c])
                dq_h.append(_dot3(doc, _split2(st_ref[sc, h]), NN))
                dv_h.append(_dot3((kd2[0][:, c], kd2[1][:, c]), ds2, NT))
                dk_h.append(_dot3((v2[0][:, c], v2[1][:, c]), ds2, NN))
                ds_ref[h] = dstate * dec[:, c] + _dot3(doc, (qs2[0][:, c], qs2[1][:, c]), TN)
            dq = jnp.concatenate(dq_h, axis=1) * ebl
            dk = jnp.concatenate(dk_h, axis=1) * dte
            dv = jnp.concatenate(dv_h, axis=1)
            dki = jnp.zeros((SUB, wb), F32)
            dvi = jnp.zeros((SUB, wb), F32)
            for s in range(SUB):
                e = jnp.exp(jnp.minimum(bl - bl[s:s + 1, :], 0.0))
                qe = q * e
                ks = k[s:s + 1, :]
                live = rcol >= s
                pk = qe * ks
                pv = do * v[s:s + 1, :]
                a = per_head(lambda c: jnp.where(live, jnp.sum(pk[:, c], axis=1, keepdims=True), 0.0))
                da = per_head(lambda c: jnp.where(live, jnp.sum(pv[:, c], axis=1, keepdims=True), 0.0))
                dq = dq + da * (e * ks)
                dki = jnp.where(rid == s, jnp.sum(da * qe, axis=0, keepdims=True), dki)
                dvi = jnp.where(rid == s, jnp.sum(a * do, axis=0, keepdims=True), dvi)
            dk = dk + dki
            dv = dv + dvi
            w = q * dq - k * dk
            gc = gc_ref[...]
            dlf = _running_sum(utri, w) + gc
            gc_ref[...] = gc + jnp.sum(w, axis=0, keepdims=True)
            t1 = dlf / forget - dk
            df_ref[rows, :] = ((1.0 - lb_all) * sf * snf * t1).astype(BF16)
            dlb_ref[...] += jnp.sum(snf * t1, axis=0, keepdims=True)
            dq_ref[rows, :] = (dq * (sq * (1.0 + qp * (1.0 - sq)))).astype(BF16)
            di_ref[rows, :] = dv.astype(BF16)
            return carry

        lax.fori_loop(0, nsc, step, 0, unroll=STEP_UNROLL)

    def col(off):
        return pl.BlockSpec((tb, wb), lambda h, j: (nb - 1 - j, h + off * (nh // hb)))

    vec = pl.BlockSpec((1, wb), lambda h, j: (0, h))
    return _carried_call(
        kern, comm, name=name, grid=(nh // hb, nb),
        in_specs=[col(0), col(1), col(2), col(3), pl.BlockSpec((3, wb), lambda h, j: (0, h)), vec,
                  col(0), pl.BlockSpec((nsc, hb, LANES, LANES), lambda h, j: (nb - 1 - j, h, 0, 0)), col(0)],
        out_specs=[col(0), col(0), col(0), col(0), vec, vec],
        out_shape=[jax.ShapeDtypeStruct((t, d), BF16)] * 4 + [jax.ShapeDtypeStruct((1, d), F32)] * 2,
        scratch_shapes=[pltpu.VMEM((hb, LANES, LANES), F32), pltpu.VMEM((1, wb), F32)],
        operands=[proj, proj, proj, proj, lb_logits, norm_g, o_raw, states, dy], sem=("parallel", "arbitrary"))


_INV_SQRT2 = 0.7071067811865476
_INV_SQRT2PI = 0.3989422804014327


def _gelu(x):
    return 0.5 * x * (1.0 + lax.erf(x * _INV_SQRT2))


def _gelu_grad(x):
    return 0.5 * (1.0 + lax.erf(x * _INV_SQRT2)) + x * jnp.exp(-0.5 * x * x) * _INV_SQRT2PI


def _causal(w):
    r = lax.broadcasted_iota(jnp.int32, (GCHUNK, GCHUNK), 0)
    c = lax.broadcasted_iota(jnp.int32, (GCHUNK, GCHUNK), 1)
    return jnp.where(r >= c, w, 0.0)


def _sg_gate_fwd(pre, ln_g, ln_b, w_s, b_s_t, name):
    t, d2 = pre.shape
    d = d2 // 2
    ng = d // LANES

    def kern(pre_ref, g_ref, b_ref, ws_ref, bs_ref, y_ref):
        z = _gelu(pre_ref[...])
        u = z[:, :d]
        vhat, _ = _ln_hat(z[:, d:])
        vn = (vhat * g_ref[...] + b_ref[...]).astype(BF16)
        bs = bs_ref[...]
        for g in range(ng):
            cols = slice(g * LANES, (g + 1) * LANES)
            wc = _causal(ws_ref[g]).astype(BF16)
            gate = jnp.dot(wc, vn[:, cols], preferred_element_type=F32) + bs[:, g:g + 1]
            y_ref[:, cols] = (u[:, cols] * gate).astype(BF16)

    vec = pl.BlockSpec((1, d), lambda i: (0, 0))
    return pl.pallas_call(
        kern, name=name, grid=(t // GCHUNK,),
        in_specs=[pl.BlockSpec((GCHUNK, d2), lambda i: (i, 0)), vec, vec,
                  pl.BlockSpec((ng, GCHUNK, GCHUNK), lambda i: (0, 0, 0)),
                  pl.BlockSpec((GCHUNK, ng), lambda i: (0, 0))],
        out_specs=pl.BlockSpec((GCHUNK, d), lambda i: (i, 0)),
        out_shape=jax.ShapeDtypeStruct((t, d), BF16),
        compiler_params=_params(("parallel",)),
    )(pre, ln_g, ln_b, w_s, b_s_t)


def _sg_gate_bwd(pre, dy, ln_g, ln_b, w_s, b_s_t, name):
    t, d2 = pre.shape
    d = d2 // 2
    ng = d // LANES

    def kern(pre_ref, dy_ref, g_ref, b_ref, ws_ref, bs_ref, dpre_ref, dws_ref, dbs_ref, dg_ref, db_ref, dvn_ref):
        @pl.when(pl.program_id(0) == 0)
        def _():
            dws_ref[...] = jnp.zeros_like(dws_ref)
            dbs_ref[...] = jnp.zeros_like(dbs_ref)
            dg_ref[...] = jnp.zeros_like(dg_ref)
            db_ref[...] = jnp.zeros_like(db_ref)

        pre = pre_ref[...]
        z = _gelu(pre)
        u = z[:, :d]
        vhat, rstd = _ln_hat(z[:, d:])
        gv = g_ref[...]
        vn = (vhat * gv + b_ref[...]).astype(BF16)
        bs = bs_ref[...]
        dyv = dy_ref[...]
        gp = _gelu_grad(pre)
        lane = lax.broadcasted_iota(jnp.int32, (GCHUNK, ng), 1)
        dbs = jnp.zeros((GCHUNK, ng), F32)
        for g in range(ng):
            cols = slice(g * LANES, (g + 1) * LANES)
            wc = _causal(ws_ref[g]).astype(BF16)
            vng = vn[:, cols]
            gate = jnp.dot(wc, vng, preferred_element_type=F32) + bs[:, g:g + 1]
            dpre_ref[:, cols] = (dyv[:, cols] * gate * gp[:, cols]).astype(BF16)
            dgate = dyv[:, cols] * u[:, cols]
            dbs = dbs + jnp.where(lane == g, jnp.sum(dgate, axis=1, keepdims=True), 0.0)
            dgb = dgate.astype(BF16)
            dws_ref[g] += _causal(lax.dot_general(dgb, vng, NT, preferred_element_type=F32))
            dvn_ref[:, cols] = lax.dot_general(wc, dgb, TN, preferred_element_type=F32)
        dbs_ref[...] += dbs
        dvn = dvn_ref[...]
        dg_ref[...] += jnp.sum(dvn * vhat, axis=0, keepdims=True)
        db_ref[...] += jnp.sum(dvn, axis=0, keepdims=True)
        dvh = dvn * gv
        m1 = jnp.mean(dvh, axis=-1, keepdims=True)
        m2 = jnp.mean(dvh * vhat, axis=-1, keepdims=True)
        dpre_ref[:, d:] = (rstd * (dvh - m1 - vhat * m2) * gp[:, d:]).astype(BF16)

    vec = pl.BlockSpec((1, d), lambda i: (0, 0))
    wsp = pl.BlockSpec((ng, GCHUNK, GCHUNK), lambda i: (0, 0, 0))
    bsp = pl.BlockSpec((GCHUNK, ng), lambda i: (0, 0))
    return pl.pallas_call(
        kern, name=name, grid=(t // GCHUNK,),
        in_specs=[pl.BlockSpec((GCHUNK, d2), lambda i: (i, 0)), pl.BlockSpec((GCHUNK, d), lambda i: (i, 0)),
                  vec, vec, wsp, bsp],
        out_specs=[pl.BlockSpec((GCHUNK, d2), lambda i: (i, 0)), wsp, bsp, vec, vec],
        out_shape=[jax.ShapeDtypeStruct((t, d2), BF16), jax.ShapeDtypeStruct((ng, GCHUNK, GCHUNK), F32),
                   jax.ShapeDtypeStruct((GCHUNK, ng), F32), jax.ShapeDtypeStruct((1, d), F32),
                   jax.ShapeDtypeStruct((1, d), F32)],
        scratch_shapes=[pltpu.VMEM((GCHUNK, d), F32)],
        compiler_params=_params(("arbitrary",)),
    )(pre, dy, ln_g, ln_b, w_s, b_s_t)


def _adamw_math(w, g, m, v):
    m = ADAM_B1 * m + (1.0 - ADAM_B1) * g
    v = ADAM_B2 * v + (1.0 - ADAM_B2) * (g * g)
    m_hat = m / (1.0 - ADAM_B1 ** ADAM_STEP)
    v_hat = v / (1.0 - ADAM_B2 ** ADAM_STEP)
    return -ADAM_LR * (m_hat / (jnp.sqrt(v_hat) + ADAM_EPS) + ADAM_WD * w), m, v


ADAMW_BLOCK_BYTES = 3 << 19


def _adamw(w, gs, m, v, name, comm=None):
    nl, r, c = w.shape
    rb = _pick(r, tuple(p for p in (512, 256, 128, 64, 32, 16, 8) if p * c * 4 <= ADAMW_BLOCK_BYTES))

    def kern(w_ref, m_ref, v_ref, *rest):
        g_refs, (d_ref, mo_ref, vo_ref, go_ref) = rest[:nl], rest[nl:]
        layer = pl.program_id(0)
        g = g_refs[0][...]
        for k in range(1, nl):
            g = jnp.where(layer == k, g_refs[k][...], g)
        dlt, mm, vv = _adamw_math(w_ref[...], g, m_ref[...], v_ref[...])
        d_ref[...] = dlt
        mo_ref[...] = mm
        vo_ref[...] = vv
        go_ref[...] = g

    blk = pl.BlockSpec((None, rb, c), lambda l, i: (l, i, 0))
    g_specs = [pl.BlockSpec((rb, c), lambda l, i, k=k: (jnp.where(l == k, i, 0), 0)) for k in range(nl)]
    outs, carried = _carried_call(
        kern, comm, name=name, grid=(nl, r // rb), in_specs=[blk] * 3 + g_specs, out_specs=[blk] * 4,
        out_shape=[jax.ShapeDtypeStruct((nl, r, c), F32)] * 4, operands=[w, m, v, *gs], sem=("parallel", "parallel"))
    return outs if comm is None else (outs, carried)


def _lb_logits_grad(lb_logits, dlb, name):
    def kern(l_ref, d_ref, o_ref):
        lg = l_ref[...]
        m = jnp.max(lg, axis=0, keepdims=True)
        e = jnp.exp(lg - m)
        p = e / jnp.sum(e, axis=0, keepdims=True)
        row = lax.broadcasted_iota(jnp.int32, lg.shape, 0)
        o_ref[...] = d_ref[...] * p[0:1, :] * (jnp.where(row == 0, 1.0, 0.0) - p)

    return pl.pallas_call(kern, name=name, out_shape=jax.ShapeDtypeStruct(lb_logits.shape, F32))(lb_logits, dlb)


def _sum_devices(others, own, me, name):
    n, r, c = others.shape
    rb = _pick(r, (512, 256, 128, 64, 32, 16, 8))

    def kern(me_ref, a_ref, own_ref, o_ref):
        mine = own_ref[...]
        acc = jnp.where(me_ref[0] == 0, mine, a_ref[0])
        for i in range(1, n):
            acc = acc + jnp.where(me_ref[0] == i, mine, a_ref[i])
        o_ref[...] = acc

    return pl.pallas_call(
        kern, name=name,
        grid_spec=pltpu.PrefetchScalarGridSpec(
            num_scalar_prefetch=1, grid=(r // rb,),
            in_specs=[pl.BlockSpec((n, rb, c), lambda i, s: (0, i, 0)), pl.BlockSpec((rb, c), lambda i, s: (i, 0))],
            out_specs=pl.BlockSpec((rb, c), lambda i, s: (i, 0))),
        out_shape=jax.ShapeDtypeStruct((r, c), F32),
        compiler_params=_params(("parallel",)),
    )(me, others, own)


def _place():
    x, y, c = lax.axis_index("x"), lax.axis_index("y"), lax.axis_index("c")
    return x, y, c


class _Plan:
    def __init__(self, ins, out_shapes, aliases, n_sems, build):
        self.ins, self.out_shapes, self.aliases, self.n_sems, self.build = list(ins), list(out_shapes), aliases, n_sems, build


def _merge(*plans):
    ins, outs, aliases, subs, sems = [], [], {}, [], 0
    for p in plans:
        for k, v in p.aliases.items():
            aliases[len(ins) + k] = len(outs) + v
        subs.append((p, len(ins), len(outs), sems))
        ins += p.ins
        outs += p.out_shapes
        sems += p.n_sems

    def build(in_refs, out_refs, send_sems, recv_sems, base):
        copies = []
        for p, i0, o0, s0 in subs:
            copies += p.build(in_refs[i0:i0 + len(p.ins)], out_refs[o0:o0 + len(p.out_shapes)], send_sems, recv_sems,
                              base + s0)
        return copies

    return _Plan(ins, outs, aliases, sems, build)


def _remote(src, dst, send_sems, recv_sems, k, to):
    return pltpu.make_async_remote_copy(src_ref=src, dst_ref=dst, send_sem=send_sems.at[k], recv_sem=recv_sems.at[k],
                                        device_id=to, device_id_type=MESH)


def _plan_gather_ici(shards):
    n = len(shards)

    def build(ins, outs, send_sems, recv_sems, base):
        x, y, c = _place()
        me = 2 * x + y
        copies = []
        for a in range(n):
            h = ins[a].shape[0] // 2
            rows = pl.ds(c * h, h)
            for r in (1, 2, 3):
                px, py, _ = _chip_rel(x, y, r)
                copies.append(_remote(ins[a].at[rows, :], outs[a].at[me, rows, :], send_sems, recv_sems,
                                      base + 4 * a + r - 1, (px, py, c)))
            copies.append(_remote(ins[a], outs[a].at[me], send_sems, recv_sems, base + 4 * a + 3, (x, y, 1 - c)))
        return copies

    return _Plan(shards, [jax.ShapeDtypeStruct((N_CHIPS,) + s.shape, s.dtype) for s in shards], {}, 4 * n, build)


def _plan_gather_pass(gathered):
    n = len(gathered)

    def build(ins, outs, send_sems, recv_sems, base):
        x, y, c = _place()
        copies = []
        for a in range(n):
            h = outs[a].shape[1] // 2
            rows = pl.ds(c * h, h)
            for r in (1, 2, 3):
                _, _, shard = _chip_rel(x, y, r)
                piece = outs[a].at[shard, rows, :]
                copies.append(_remote(piece, piece, send_sems, recv_sems, base + 3 * a + r - 1, (x, y, 1 - c)))
        return copies

    return _Plan(gathered, [jax.ShapeDtypeStruct(g.shape, g.dtype) for g in gathered], {a: a for a in range(n)},
                 3 * n, build)


def _plan_swap(split):
    n = len(split)

    def build(ins, outs, send_sems, recv_sems, base):
        x, y, c = _place()
        return [_remote(ins[a].at[j, 1 - c], outs[a].at[j], send_sems, recv_sems, base + N_CHIPS * a + j, (x, y, 1 - c))
                for a in range(n) for j in range(N_CHIPS)]

    return _Plan(split, [jax.ShapeDtypeStruct((N_CHIPS,) + g.shape[2:], g.dtype) for g in split], {}, N_CHIPS * n, build)


def _plan_scatter(parts):
    n = len(parts)

    def build(ins, outs, send_sems, recv_sems, base):
        x, y, c = _place()
        copies = []
        for a in range(n):
            for r in (1, 2, 3):
                px, py, shard = _chip_rel(x, y, r)
                copies.append(_remote(ins[a].at[shard], outs[a].at[r - 1], send_sems, recv_sems, base + 3 * a + r - 1,
                                      (px, py, c)))
        return copies

    return _Plan(parts, [jax.ShapeDtypeStruct((3,) + p.shape[1:], p.dtype) for p in parts], {}, 3 * n, build)


def _plan_join(bufs):
    n = len(bufs)

    def build(ins, outs, send_sems, recv_sems, base):
        x, y, c = _place()
        return [_remote(outs[a].at[c], outs[a].at[c], send_sems, recv_sems, base + a, (x, y, 1 - c)) for a in range(n)]

    return _Plan(bufs, [jax.ShapeDtypeStruct(b.shape, b.dtype) for b in bufs], {a: a for a in range(n)}, n, build)


def _carried_call(kern, plan, *, name, grid, in_specs, out_specs, out_shape, operands, scratch_shapes=(),
                  aliases=None, sem=None):
    n_in, n_out, n_sc = len(operands), len(out_shape), len(scratch_shapes)
    aliases = dict(aliases or {})
    if plan is None:
        outs = pl.pallas_call(kern, name=name, grid=grid, in_specs=in_specs, out_specs=out_specs, out_shape=out_shape,
                              scratch_shapes=list(scratch_shapes), input_output_aliases=aliases,
                              compiler_params=_params(sem))(*operands)
        return list(outs), []
    ci, co = len(plan.ins), len(plan.out_shapes)
    for k, v in plan.aliases.items():
        aliases[n_in + k] = n_out + v
    steps = tuple(grid)

    def body(*refs):
        ins, cins = refs[:n_in], refs[n_in:n_in + ci]
        outs = refs[n_in + ci:n_in + ci + n_out]
        couts = refs[n_in + ci + n_out:n_in + ci + n_out + co]
        scratch = refs[n_in + ci + n_out + co:n_in + ci + n_out + co + n_sc]
        send_sems, recv_sems = refs[-2], refs[-1]
        first = functools.reduce(jnp.logical_and, [pl.program_id(a) == 0 for a in range(len(steps))])
        last = functools.reduce(jnp.logical_and, [pl.program_id(a) == steps[a] - 1 for a in range(len(steps))])

        @pl.when(first)
        def _():
            for cp in plan.build(cins, couts, send_sems, recv_sems, 0):
                cp.start()

        kern(*ins, *outs, *scratch)

        @pl.when(last)
        def _():
            for cp in plan.build(cins, couts, send_sems, recv_sems, 0):
                cp.wait()

    anyspec = pl.BlockSpec(memory_space=pl.ANY)
    outs = pl.pallas_call(
        body, name=name, grid=grid, in_specs=list(in_specs) + [anyspec] * ci,
        out_specs=list(out_specs) + [anyspec] * co, out_shape=list(out_shape) + plan.out_shapes,
        scratch_shapes=list(scratch_shapes) + [pltpu.SemaphoreType.DMA((plan.n_sems,)),
                                               pltpu.SemaphoreType.DMA((plan.n_sems,))],
        input_output_aliases=aliases,
        compiler_params=_params(("arbitrary",) * len(steps)),
    )(*operands, *plan.ins)
    return list(outs[:n_out]), list(outs[n_out:])


def _chip_rel(x, y, r):
    px = x if r < 2 else 1 - x
    py = y if r % 2 == 0 else 1 - y
    return px, py, 2 * px + py


def _allgather_split(arrs, name):
    n = len(arrs)
    slots = 7

    def body(*refs):
        ins, outs = refs[:n], refs[n:2 * n]
        send_sems, recv_sems = refs[2 * n:]
        x, y, c = _place()
        me = 2 * x + y
        sib = (x, y, 1 - c)

        def half(a, shard, hc):
            h = ins[a].shape[0] // 2
            return outs[a].at[shard, pl.ds(hc * h, h), :]

        def src_half(a):
            h = ins[a].shape[0] // 2
            return ins[a].at[pl.ds(c * h, h), :]

        def copy(a, slot, src, dst, to):
            return pltpu.make_async_remote_copy(src_ref=src, dst_ref=dst, send_sem=send_sems.at[a * slots + slot],
                                                recv_sem=recv_sems.at[a * slots + slot], device_id=to,
                                                device_id_type=MESH)

        first = []
        for r in (1, 2, 3):
            px, py, _ = _chip_rel(x, y, r)
            for a in range(n):
                first.append(copy(a, r - 1, src_half(a), half(a, me, c), (px, py, c)))
        own = [copy(a, 6, ins[a], outs[a].at[me], sib) for a in range(n)]
        for cp in first + own:
            cp.start()
        passed = []
        for r in (1, 2, 3):
            _, _, shard = _chip_rel(x, y, r)
            for a in range(n):
                copy(a, r - 1, src_half(a), half(a, shard, c), sib).wait_recv()
                cp = copy(a, 3 + r - 1, half(a, shard, c), half(a, shard, c), sib)
                cp.start()
                passed.append(cp)
        for r in (1, 2, 3):
            _, _, shard = _chip_rel(x, y, r)
            for a in range(n):
                copy(a, 3 + r - 1, src_half(a), half(a, shard, 1 - c), sib).wait_recv()
        for cp in own:
            cp.wait_recv()
        for cp in first + passed + own:
            cp.wait_send()

    anyspec = pl.BlockSpec(memory_space=pl.ANY)
    return pl.pallas_call(
        body, name=name, in_specs=[anyspec] * n, out_specs=[anyspec] * n,
        out_shape=[jax.ShapeDtypeStruct((N_CHIPS,) + a.shape, a.dtype) for a in arrs],
        scratch_shapes=[pltpu.SemaphoreType.DMA((slots * n,)), pltpu.SemaphoreType.DMA((slots * n,))],
        compiler_params=pltpu.CompilerParams(has_side_effects=True),
    )(*arrs)


def _allgather_whole(arr, name):
    def body(in_ref, out_ref, send_sems, recv_sems, loc_sem):
        x, y, c = _place()
        me = 2 * x + y
        local = pltpu.make_async_copy(in_ref, out_ref.at[me], loc_sem)
        local.start()
        sends = []
        for r in (1, 2, 3):
            px, py, _ = _chip_rel(x, y, r)
            sends.append(pltpu.make_async_remote_copy(
                src_ref=in_ref, dst_ref=out_ref.at[me], send_sem=send_sems.at[r - 1], recv_sem=recv_sems.at[r - 1],
                device_id=(px, py, c), device_id_type=MESH))
        for cp in sends:
            cp.start()
        for r in (1, 2, 3):
            px, py, shard = _chip_rel(x, y, r)
            pltpu.make_async_remote_copy(
                src_ref=in_ref, dst_ref=out_ref.at[shard], send_sem=send_sems.at[r - 1], recv_sem=recv_sems.at[r - 1],
                device_id=(px, py, c), device_id_type=MESH).wait_recv()
        for cp in sends:
            cp.wait_send()
        local.wait()

    anyspec = pl.BlockSpec(memory_space=pl.ANY)
    return pl.pallas_call(
        body, name=name, in_specs=[anyspec], out_specs=anyspec,
        out_shape=jax.ShapeDtypeStruct((N_CHIPS,) + arr.shape, arr.dtype),
        scratch_shapes=[pltpu.SemaphoreType.DMA((3,)), pltpu.SemaphoreType.DMA((3,)), pltpu.SemaphoreType.DMA],
        compiler_params=pltpu.CompilerParams(has_side_effects=True),
    )(arr)


def _plan_gather_all(buf):
    def build(ins, outs, send_sems, recv_sems, base):
        x, y, c = _place()
        me = 4 * x + 2 * y + c
        copies = []
        for r in range(1, N_DEV):
            px, py, _ = _chip_rel(x, y, r // 2)
            pc = c if r % 2 == 0 else 1 - c
            copies.append(_remote(ins[0], outs[0].at[me], send_sems, recv_sems, base + r - 1, (px, py, pc)))
        return copies

    return _Plan([buf, jnp.zeros((N_DEV,) + buf.shape, buf.dtype)],
                 [jax.ShapeDtypeStruct((N_DEV,) + buf.shape, buf.dtype)], {1: 0}, N_DEV - 1, build)


def _add_half(grad, recv, sel, name):
    _, _, rh, cw = grad.shape
    rb = _pick(rh, (512, 256, 176, 128, 64, 32, 16, 8))

    def kern(sel_ref, g_ref, r_ref, o_ref):
        o_ref[...] = (g_ref[...] + r_ref[...]).astype(BF16)

    return pl.pallas_call(
        kern, name=name,
        grid_spec=pltpu.PrefetchScalarGridSpec(
            num_scalar_prefetch=1, grid=(N_CHIPS, rh // rb),
            in_specs=[pl.BlockSpec((None, None, rb, cw), lambda j, i, s: (j, s[0], i, 0)),
                      pl.BlockSpec((None, rb, cw), lambda j, i, s: (j, i, 0))],
            out_specs=pl.BlockSpec((None, rb, cw), lambda j, i, s: (j, i, 0))),
        out_shape=jax.ShapeDtypeStruct((N_CHIPS, rh, cw), BF16),
        compiler_params=_params(("parallel", "parallel")),
    )(sel, grad, recv)


def _add_own(grad, recv, got, sel, name):
    _, _, rh, cw = grad.shape
    rb = _pick(rh, (512, 256, 176, 128, 64, 32, 16, 8))

    def kern(sel_ref, g_ref, r_ref, b_ref, o_ref):
        own = g_ref[...] + r_ref[...]
        o_ref[...] = ((own + b_ref[0].astype(F32)) + b_ref[1].astype(F32)) + b_ref[2].astype(F32)

    return pl.pallas_call(
        kern, name=name,
        grid_spec=pltpu.PrefetchScalarGridSpec(
            num_scalar_prefetch=1, grid=(rh // rb,),
            in_specs=[pl.BlockSpec((None, None, rb, cw), lambda i, s: (s[1], s[0], i, 0)),
                      pl.BlockSpec((None, rb, cw), lambda i, s: (s[1], i, 0)),
                      pl.BlockSpec((3, rb, cw), lambda i, s: (0, i, 0))],
            out_specs=pl.BlockSpec((None, rb, cw), lambda i, s: (s[0], i, 0))),
        out_shape=jax.ShapeDtypeStruct((2, rh, cw), F32),
        compiler_params=_params(("parallel",)),
    )(sel, grad, recv, got)


def _stacked(g):
    return g.reshape(1, g.shape[0] * g.shape[1], g.shape[2])


def _halves(g):
    g = g.reshape(N_CHIPS, g.shape[0] * g.shape[1] // N_CHIPS, g.shape[2])
    return g.reshape(N_CHIPS, 2, g.shape[1] // 2, g.shape[2])


def _whole(f):
    return f.reshape(f.shape[0] * f.shape[1], f.shape[2])


def _step(x2, tgt, xb, sh, sm, w, mom, var):
    x, y, c = _place()
    sel = jnp.stack([c, 2 * x + y]).astype(jnp.int32)
    wg = {}
    wg["hg_in"] = _allgather_split([sh["hg_in"]], "gather_hg_in")[0]
    proj, landed = _matmul(xb, wg["hg_in"], mode="nn", nsh=N_CHIPS, name="hg_in",
                           comm=_plan_gather_ici([sh["hg_out"], sh["dn0"]]))
    (yhg, o_raw, states), got = _hgrn2_fwd(
        proj, sm["lb_logits"], sm["hg_norm_g"], "hgrn2_fwd",
        comm=_merge(_plan_gather_pass(landed), _plan_gather_ici([sh[k] for k in ("up0", "sg_in", "sg_out")])))
    wg["hg_out"], wg["dn0"], landed = got[0], got[1], got[2:]
    mixed, got = _matmul(yhg, _stacked(wg["hg_out"]), mode="nn", nsh=1, name="hg_out", comm=_plan_gather_pass(landed))
    wg["up0"], wg["sg_in"], wg["sg_out"] = got
    xin1, h1, h1b = _res_ln_fwd(x2, mixed, sm["ln1_g"][0:1], sm["ln1_b"][0:1], "l0_ln1")
    u0, landed = _matmul(h1b, wg["up0"], mode="nn", nsh=N_CHIPS, name="l0_ffn_up", comm=_plan_gather_ici([sh["up1"]]))
    gact0 = _conv_gate_fwd(u0, sm["conv_w"][0], sm["conv_b"][0:1], "l0_ffn_gate")
    ffn, got = _matmul(gact0, _stacked(wg["dn0"]), mode="nn", nsh=1, name="l0_ffn_down",
                       comm=_merge(_plan_gather_pass(landed), _plan_gather_ici([sh["dn1"]])))
    wg["up1"], landed = got[0], got[1:]
    xin2, h2, h2b = _res_ln_fwd(h1, ffn, sm["ln2_g"][0:1], sm["ln2_b"][0:1], "l0_ffn_ln")
    pre, got = _matmul(h2b, wg["sg_in"], mode="nn", nsh=N_CHIPS, name="sg_in", comm=_plan_gather_pass(landed))
    wg["dn1"] = got[0]
    ysg = _sg_gate_fwd(pre, sm["sg_ln_g"], sm["sg_ln_b"], sm["sg_w_s"], sm["sg_b_s_t"], "sg_gate")
    mixed = _matmul(ysg, _stacked(wg["sg_out"]), mode="nn", nsh=1, name="sg_out")
    xin3, h3, h3b = _res_ln_fwd(h2, mixed, sm["ln1_g"][1:2], sm["ln1_b"][1:2], "l1_ln1")
    u1 = _matmul(h3b, wg["up1"], mode="nn", nsh=N_CHIPS, name="l1_ffn_up")
    gact1 = _conv_gate_fwd(u1, sm["conv_w"][1], sm["conv_b"][1:2], "l1_ffn_gate")
    ffn = _matmul(gact1, _stacked(wg["dn1"]), mode="nn", nsh=1, name="l1_ffn_down")
    xin4, _, _ = _res_ln_fwd(h3, ffn, sm["ln2_g"][1:2], sm["ln2_b"][1:2], "l1_ffn_ln")

    gs, grad, split, recv, part = {}, {}, {}, {}, {}

    def swap_on(call, keys):
        for k in keys:
            split[k] = _halves(grad[k])
        out, got = call(_plan_swap([split[k] for k in keys]))
        for k, r in zip(keys, got):
            recv[k] = r
            part[k] = _add_half(split[k], r, sel, f"rs_addhalf_{k}")
        return out

    def ffn_bwd(u, gact, hb_in, dxin, dxin_b, w_up, w_down, layer, tag, up, down, waiting):
        dgact = _matmul(dxin_b, _stacked(w_down), mode="nt", nsh=1, name=f"{tag}_ddown")
        grad[down] = _matmul(gact, dxin_b, mode="tn", nsh=1, name=f"{tag}_wdown")
        da, db, dcw, dcb = _conv_gate_bwd(u, dgact, sm["conv_w"][layer], sm["conv_b"][layer:layer + 1], f"{tag}_dgate")
        grad[up] = swap_on(lambda plan: _matmul(hb_in, [da, db], mode="tn", nsh=N_CHIPS, name=f"{tag}_wup", comm=plan),
                           waiting + [down])
        dh = swap_on(lambda plan: _matmul([da, db], w_up, mode="nt", nsh=N_CHIPS, resid=dxin, alpha=ALPHA,
                                          name=f"{tag}_dup", comm=plan), [up])
        return dh, dcw, dcb

    dx, dxb, dg4, db4, loss = _ln_bwd(xin4, tgt, sm["ln2_g"][1:2], sm["ln2_b"][1:2], "l1_ln2_bwd", loss_head=True)
    dh3, dcw1, dcb1 = ffn_bwd(u1, gact1, h3b, dx, dxb, wg["up1"], wg["dn1"], 1, "l1_ffn", "up1", "dn1", [])
    dx, dxb, dg3, db3 = _ln_bwd(xin3, dh3, sm["ln1_g"][1:2], sm["ln1_b"][1:2], "l1_ln1_bwd")
    grad["sg_out"] = _matmul(ysg, dxb, mode="tn", nsh=1, name="sg_wout")
    dysg = swap_on(lambda plan: _matmul(dxb, _stacked(wg["sg_out"]), mode="nt", nsh=1, name="sg_dout", comm=plan),
                   ["sg_out"])
    dpre, gs["sg_w_s"], gs["sg_b_s_t"], gs["sg_ln_g"], gs["sg_ln_b"] = _sg_gate_bwd(
        pre, dysg, sm["sg_ln_g"], sm["sg_ln_b"], sm["sg_w_s"], sm["sg_b_s_t"], "sg_gate_bwd")
    grad["sg_in"] = _matmul(h2b, dpre, mode="tn", nsh=N_CHIPS, name="sg_win")
    dh2 = _matmul(dpre, wg["sg_in"], mode="nt", nsh=N_CHIPS, resid=dx, alpha=ALPHA, name="sg_din")
    dx, dxb, dg2, db2 = _ln_bwd(xin2, dh2, sm["ln2_g"][0:1], sm["ln2_b"][0:1], "l0_ln2_bwd")
    dh1, dcw0, dcb0 = ffn_bwd(u0, gact0, h1b, dx, dxb, wg["up0"], wg["dn0"], 0, "l0_ffn", "up0", "dn0", ["sg_in"])
    dx, dxb, dg1, db1 = _ln_bwd(xin1, dh1, sm["ln1_g"][0:1], sm["ln1_b"][0:1], "l0_ln1_bwd")
    grad["hg_out"] = _matmul(yhg, dxb, mode="tn", nsh=1, name="hg_wout")
    dyhg = swap_on(lambda plan: _matmul(dxb, _stacked(wg["hg_out"]), mode="nt", nsh=1, name="hg_dout", comm=plan),
                   ["hg_out"])
    early = ("dn1", "up1", "sg_out", "sg_in", "dn0", "up0", "hg_out")
    dparts, got = _hgrn2_bwd(proj, sm["lb_logits"], sm["hg_norm_g"], o_raw, states, dyhg, "hgrn2_bwd",
                             comm=_plan_scatter([part[k] for k in early]))
    gs["lb"], gs["hg_norm_g"] = dparts[4], dparts[5]
    gs["ln1_g"] = jnp.concatenate([dg1, dg3], axis=0)
    gs["ln1_b"] = jnp.concatenate([db1, db3], axis=0)
    gs["ln2_g"] = jnp.concatenate([dg2, dg4], axis=0)
    gs["ln2_b"] = jnp.concatenate([db2, db4], axis=0)
    gs["conv_w"] = jnp.stack([dcw0, dcw1], axis=0)
    gs["conv_b"] = jnp.concatenate([dcb0, dcb1], axis=0)
    packed, layout = _pack(gs)
    mine = [_add_own(split[k], recv[k], b, sel, f"rs_addown_{k}") for k, b in zip(early, got)]
    grad["hg_in"], got = _matmul(xb, list(dparts[:4]), mode="tn", nsh=N_CHIPS, name="hg_win",
                                 comm=_merge(_plan_join(mine), _plan_gather_all(packed)))
    red = {k: _whole(f) for k, f in zip(early, got)}
    me8 = jnp.reshape(4 * x + 2 * y + c, (1,)).astype(jnp.int32)
    summed = _unpack(_sum_devices(got[len(early)], packed, me8, "sum_small_grads"), layout)
    gx = swap_on(lambda plan: _matmul(dparts[0], wg["hg_in"], mode="nt", nsh=1, b_off=0, resid=dx, alpha=ALPHA,
                                      name="hg_din_q", comm=plan), ["hg_in"])
    gx, got = _matmul(list(dparts[1:4]), wg["hg_in"], mode="nt", nsh=3, b_off=1, resid=gx, alpha=1.0, name="hg_din_fig",
                      comm=_plan_scatter([part["hg_in"]]))
    mine = _add_own(split["hg_in"], recv["hg_in"], got[0], sel, "rs_addown_hg_in")
    upd = {}
    upd["hg_w_out"], full = _adamw(w["hg_w_out"], [red["hg_out"]], mom["hg_w_out"], var["hg_w_out"], "adamw_hg_w_out",
                                   comm=_plan_join([mine]))
    red["hg_in"] = _whole(full[0])
    for k, src in (("ffn_w_up", ("up0", "up1")), ("ffn_w_down", ("dn0", "dn1")), ("sg_w_in", ("sg_in",)),
                   ("sg_w_out", ("sg_out",)), ("hg_w_in", ("hg_in",))):
        upd[k] = _adamw(w[k], [red[s] for s in src], mom[k], var[k], f"adamw_{k}")
    return loss, gx, summed, upd


_SMALL_ORDER = ("lb", "hg_norm_g", "sg_w_s", "sg_b_s_t", "conv_b", "ln1_g", "ln1_b", "ln2_g", "ln2_b",
                "conv_w", "sg_ln_g", "sg_ln_b")


PACK_ROWS = 512


def _pack(parts):
    flat, layout, off = [], [], 0
    for k in _SMALL_ORDER:
        a = parts[k]
        n = a.size
        pad = (-n) % LANES
        flat.append(jnp.pad(a.reshape(-1), (0, pad)))
        layout.append((k, off, n, a.shape))
        off += n + pad
    flat.append(jnp.zeros(((-off) % (PACK_ROWS * LANES),), F32))
    return jnp.concatenate(flat).reshape(-1, LANES), layout


def _unpack(buf, layout):
    flat = buf.reshape(-1)
    return {k: flat[off:off + n].reshape(shape) for k, off, n, shape in layout}


def kernel(x, lb_logits, hg_w_in, hg_norm_g, hg_w_out, sg_w_in, sg_ln_g, sg_ln_b, sg_w_s, sg_b_s, sg_w_out, ffn_w_up, ffn_conv_w, ffn_conv_b, ffn_w_down, ln1_g, ln1_b, ln2_g, ln2_b, loss_target, m_lb_logits, m_hg_w_in, m_hg_norm_g, m_hg_w_out, m_sg_w_in, m_sg_ln_g, m_sg_ln_b, m_sg_w_s, m_sg_b_s, m_sg_w_out, m_ffn_w_up, m_ffn_conv_w, m_ffn_conv_b, m_ffn_w_down, m_ln1_g, m_ln1_b, m_ln2_g, m_ln2_b, v_lb_logits, v_hg_w_in, v_hg_norm_g, v_hg_w_out, v_sg_w_in, v_sg_ln_g, v_sg_ln_b, v_sg_w_s, v_sg_b_s, v_sg_w_out, v_ffn_w_up, v_ffn_conv_w, v_ffn_conv_b, v_ffn_w_down, v_ln1_g, v_ln1_b, v_ln2_g, v_ln2_b):
    names = ("lb_logits", "hg_w_in", "hg_norm_g", "hg_w_out", "sg_w_in", "sg_ln_g", "sg_ln_b", "sg_w_s", "sg_b_s",
             "sg_w_out", "ffn_w_up", "ffn_conv_w", "ffn_conv_b", "ffn_w_down", "ln1_g", "ln1_b", "ln2_g", "ln2_b")
    w = dict(zip(names, (lb_logits, hg_w_in, hg_norm_g, hg_w_out, sg_w_in, sg_ln_g, sg_ln_b, sg_w_s, sg_b_s,
                         sg_w_out, ffn_w_up, ffn_conv_w, ffn_conv_b, ffn_w_down, ln1_g, ln1_b, ln2_g, ln2_b)))
    mom = dict(zip(names, (m_lb_logits, m_hg_w_in, m_hg_norm_g, m_hg_w_out, m_sg_w_in, m_sg_ln_g, m_sg_ln_b, m_sg_w_s,
                           m_sg_b_s, m_sg_w_out, m_ffn_w_up, m_ffn_conv_w, m_ffn_conv_b, m_ffn_w_down, m_ln1_g,
                           m_ln1_b, m_ln2_g, m_ln2_b)))
    var = dict(zip(names, (v_lb_logits, v_hg_w_in, v_hg_norm_g, v_hg_w_out, v_sg_w_in, v_sg_ln_g, v_sg_ln_b, v_sg_w_s,
                           v_sg_b_s, v_sg_w_out, v_ffn_w_up, v_ffn_conv_w, v_ffn_conv_b, v_ffn_w_down, v_ln1_g,
                           v_ln1_b, v_ln2_g, v_ln2_b)))
    x2, tgt = x[0], loss_target[0]
    d = x2.shape[1]
    fq = ffn_conv_w.shape[2]
    dq = sg_ln_g.shape[1]
    cx, cy, _ = _place()
    me = 2 * cx + cy

    shards = {"hg_in": hg_w_in[0], "hg_out": hg_w_out[0], "sg_in": sg_w_in[0], "sg_out": sg_w_out[0],
              "up0": ffn_w_up[0], "up1": ffn_w_up[1], "dn0": ffn_w_down[0], "dn1": ffn_w_down[1]}
    shards = {k: v.astype(BF16) for k, v in shards.items()}
    wide = max(fq, dq)
    tiny = jnp.concatenate([jnp.pad(ffn_conv_w.reshape(6, fq), ((0, 0), (0, wide - fq))),
                            jnp.pad(sg_ln_g, ((0, 0), (0, wide - dq))),
                            jnp.pad(sg_ln_b, ((0, 0), (0, wide - dq)))], axis=0)
    tiny_all = _allgather_whole(tiny, "gather_small")
    conv_w_full = jnp.transpose(tiny_all[:, 0:6, :fq].reshape(N_CHIPS, 2, 3, fq), (1, 2, 0, 3)).reshape(2, 3, N_CHIPS * fq)
    sm = {"lb_logits": lb_logits, "hg_norm_g": hg_norm_g, "ln1_g": ln1_g, "ln1_b": ln1_b, "ln2_g": ln2_g,
          "ln2_b": ln2_b, "conv_w": conv_w_full, "conv_b": ffn_conv_b,
          "sg_ln_g": tiny_all[:, 6, :dq].reshape(1, N_CHIPS * dq),
          "sg_ln_b": tiny_all[:, 7, :dq].reshape(1, N_CHIPS * dq),
          "sg_w_s": sg_w_s[0], "sg_b_s_t": jnp.transpose(sg_b_s[0])}

    loss_row, grad_x, summed, upd = _step(x2, tgt, x2.astype(BF16), shards, sm, w, mom, var)
    loss = lax.psum(loss_row[0, 0], ("x", "y", "c"))

    grads = {
        "lb_logits": _lb_logits_grad(lb_logits, summed["lb"], "lb_logits_grad"),
        "hg_norm_g": summed["hg_norm_g"],
        "sg_ln_g": lax.dynamic_slice_in_dim(summed["sg_ln_g"], me * dq, dq, axis=1),
        "sg_ln_b": lax.dynamic_slice_in_dim(summed["sg_ln_b"], me * dq, dq, axis=1),
        "sg_w_s": summed["sg_w_s"][None], "sg_b_s": jnp.transpose(summed["sg_b_s_t"])[None],
        "ffn_conv_w": lax.dynamic_slice_in_dim(summed["conv_w"], me * fq, fq, axis=2),
        "ffn_conv_b": summed["conv_b"],
        "ln1_g": summed["ln1_g"], "ln1_b": summed["ln1_b"], "ln2_g": summed["ln2_g"], "ln2_b": summed["ln2_b"],
    }

    delta, new_m, new_v = {}, {}, {}
    for k, (dlt, mm, vv, gg) in upd.items():
        delta[k], new_m[k], new_v[k], grads[k] = dlt, mm, vv, gg
    small_names = [k for k in names if k not in upd]

    def pack_small(src):
        flat = [src[k].reshape(-1) for k in small_names]
        n = sum(a.size for a in flat)
        flat.append(jnp.zeros(((-n) % (PACK_ROWS * LANES),), F32))
        return jnp.concatenate(flat).reshape(1, -1, LANES)

    outs = _adamw(pack_small(w), [pack_small(grads)[0]], pack_small(mom), pack_small(var), "adamw_small")
    off = 0
    for k in small_names:
        n = w[k].size
        for dst, o in zip((delta, new_m, new_v), outs):
            dst[k] = o.reshape(-1)[off:off + n].reshape(w[k].shape)
        off += n

    return (loss, grad_x[None], *[grads[k] for k in names], *[delta[k] for k in names],
            *[new_m[k] for k in names], *[new_v[k] for k in names])
```

```python
import functools

import jax
import jax.numpy as jnp
from jax import lax
from jax.experimental import pallas as pl
from jax.experimental.pallas import tpu as pltpu

F32 = jnp.float32
BF16 = jnp.bfloat16
HI = lax.Precision.HIGHEST
MESH = pl.DeviceIdType.MESH

ALPHA = (2 * 2) ** 0.25
LN_EPS = 1e-5
RMS_EPS = 1e-6
ADAM_LR, ADAM_B1, ADAM_B2, ADAM_EPS, ADAM_WD, ADAM_STEP = 0.001, 0.9, 0.999, 1e-08, 0.01, 10

LANES = 128
SUB = 16
TILE = 8
GCHUNK = 128
VMEM_LIMIT = 56 * 1024 * 1024
N_CHIPS = 4
N_DEV = 8

NT = (((1,), (1,)), ((), ()))
TN = (((0,), (0,)), ((), ()))
NN = (((1,), (0,)), ((), ()))


def _pick(dim, prefs):
    for p in prefs:
        if dim % p == 0:
            return p
    return dim


def _params(sem=None, **kw):
    return pltpu.CompilerParams(dimension_semantics=sem, vmem_limit_bytes=VMEM_LIMIT, **kw)


def _sigmoid_pair(x):
    e = jnp.exp(-jnp.abs(x))
    inv = 1.0 / (1.0 + e)
    pos = x >= 0
    return jnp.where(pos, inv, e * inv), jnp.where(pos, e * inv, inv)


def _ln_hat(x):
    mu = jnp.mean(x, axis=-1, keepdims=True)
    xc = x - mu
    var = jnp.mean(xc * xc, axis=-1, keepdims=True)
    rstd = lax.rsqrt(var + LN_EPS)
    return xc * rstd, rstd


def _lower_bound(logits):
    m = jnp.max(logits, axis=0, keepdims=True)
    e = jnp.exp(logits - m)
    return e[0:1, :] / jnp.sum(e, axis=0, keepdims=True)


MATMUL_VMEM_BUDGET = 40 * 1024 * 1024


def _fit_bk(kdim, bm, bn, out_dtype, has_resid, na=1, nb=1):
    fixed = bm * bn * (4 + 2 * jnp.dtype(out_dtype).itemsize + (8 if has_resid else 0))
    best = LANES
    for bk in range(LANES, kdim + 1, LANES):
        if kdim % bk == 0 and fixed + 4 * bk * (bm * na + bn * nb) <= MATMUL_VMEM_BUDGET:
            best = bk
    return best


def _fit_bm_bk(mdim, prefs, kdim, bn, out_dtype, has_resid, na=1, nb=1):
    best = None
    fits = [bm for bm in prefs if mdim % bm == 0][:2] or [mdim]
    for bm in fits:
        bk = _fit_bk(kdim, bm, bn, out_dtype, has_resid, na, nb)
        if best is None or kdim // bk < kdim // best[1]:
            best = (bm, bk)
    return best


def _matmul(a, b, *, mode, name, out_dtype=F32, resid=None, alpha=1.0, b_off=0, nsh=None, comm=None):
    a_parts = a if isinstance(a, (list, tuple)) else [a]
    b_parts = b if isinstance(b, (list, tuple)) else [b]
    n_parts = max(len(a_parts), len(b_parts))
    if mode == "nn":
        m, kdim = a.shape
        _, _, ns = b.shape
        bn = _pick(ns, (1024, 1408, 512, 256, 128))
        bm, bk = _fit_bm_bk(m, (1024, 512, 256, 128), kdim, bn, out_dtype, resid is not None)
        nps = ns // bn
        grid = (m // bm, nsh * nps, kdim // bk)
        a_specs = [pl.BlockSpec((bm, bk), lambda i, j, k: (i, k))]
        b_specs = [pl.BlockSpec((None, bk, bn), lambda i, j, k: (b_off + j // nps, k, j % nps))]
        o_spec = pl.BlockSpec((bm, bn), lambda i, j, k: (i, j))
        out_shape = jax.ShapeDtypeStruct((m, nsh * ns), out_dtype)
        dims, part_axis, per_part = NN, 2, grid[2]
    elif mode == "nt":
        m = a_parts[0].shape[0]
        _, kdim, ns = b.shape
        bn = _pick(kdim, (1024, 1408, 512, 256, 128))
        bm, bk = _fit_bm_bk(m, (1024, 512, 256, 128), ns, bn, out_dtype, resid is not None, na=n_parts)
        kps = ns // bk
        per_part = nsh // n_parts * kps
        grid = (m // bm, kdim // bn, nsh * kps)
        a_specs = [pl.BlockSpec((bm, bk), lambda i, j, k, p=p: (jnp.where(k // per_part == p, i, 0),
                                                                  jnp.where(k // per_part == p, k % per_part, 0)))
                   for p in range(n_parts)]
        b_specs = [pl.BlockSpec((None, bn, bk), lambda i, j, k: (b_off + k // kps, j, k % kps))]
        o_spec = pl.BlockSpec((bm, bn), lambda i, j, k: (i, j))
        out_shape = jax.ShapeDtypeStruct((m, kdim), out_dtype)
        dims, part_axis = NT, 2
    else:
        t, kdim = a.shape
        ns = b_parts[0].shape[1] * n_parts // nsh
        bn = _pick(ns, (1024, 1408, 512, 256, 128))
        bm, bk = _fit_bm_bk(kdim, (1024, 1408, 512, 256, 128), t, bn, out_dtype, resid is not None, nb=n_parts)
        nps = ns // bn
        per_part = nsh // n_parts * nps
        grid = (kdim // bm, nsh * nps, t // bk)
        a_specs = [pl.BlockSpec((bk, bm), lambda i, j, k: (k, i))]
        b_specs = [pl.BlockSpec((bk, bn), lambda i, j, k, p=p: (jnp.where(j // per_part == p, k, 0),
                                                                  jnp.where(j // per_part == p, j % per_part, 0)))
                   for p in range(n_parts)]
        o_spec = pl.BlockSpec((None, bm, bn), lambda i, j, k: (j // nps, i, j % nps))
        out_shape = jax.ShapeDtypeStruct((nsh, kdim, ns), out_dtype)
        dims, part_axis = TN, 1
    nk = grid[2]
    na, nb_ = len(a_parts), len(b_parts)
    has_resid = resid is not None

    def kern(*refs):
        a_refs, b_refs = refs[:na], refs[na:na + nb_]
        r_ref = refs[na + nb_] if has_resid else None
        k = pl.program_id(2)

        def finish(r, o_ref):
            if has_resid:
                r = r + alpha * r_ref[...]
            o_ref[...] = r.astype(o_ref.dtype)

        def add(a_ref, b_ref):
            if nk == 1:
                finish(lax.dot_general(a_ref[...], b_ref[...], dims, preferred_element_type=F32), refs[-1])
                return
            refs[-1][...] += lax.dot_general(a_ref[...], b_ref[...], dims, preferred_element_type=F32)

        if nk > 1:
            @pl.when(k == 0)
            def _():
                refs[-1][...] = jnp.zeros_like(refs[-1])

        if n_parts == 1:
            add(a_refs[0], b_refs[0])
        else:
            which = pl.program_id(part_axis) // per_part
            for p in range(n_parts):
                pl.when(which == p)(functools.partial(add, a_refs[min(p, na - 1)], b_refs[min(p, nb_ - 1)]))
        if nk > 1:
            @pl.when(k == nk - 1)
            def _():
                finish(refs[-1][...], refs[-2])

    in_specs = a_specs + b_specs
    operands = list(a_parts) + list(b_parts)
    if has_resid:
        in_specs.append(pl.BlockSpec((bm, bn), lambda i, j, k: (i, j)))
        operands.append(resid)
    outs, carried = _carried_call(
        kern, comm, name=name, grid=grid, in_specs=in_specs, out_specs=[o_spec], out_shape=[out_shape],
        operands=operands, scratch_shapes=[pltpu.VMEM((bm, bn), F32)] if nk > 1 else [],
        sem=("parallel", "parallel", "arbitrary"))
    return outs[0] if comm is None else (outs[0], carried)


def _res_ln_fwd(h_prev, sub, g, b, name):
    t, d = h_prev.shape
    tb = _pick(t, (256, 128, 64, 32, 16))

    def kern(hp_ref, s_ref, g_ref, b_ref, xin_ref, h_ref, hb_ref):
        xin = ALPHA * hp_ref[...] + s_ref[...]
        xhat, _ = _ln_hat(xin)
        h = xhat * g_ref[...] + b_ref[...]
        xin_ref[...] = xin
        h_ref[...] = h
        hb_ref[...] = h.astype(BF16)

    row = pl.BlockSpec((tb, d), lambda i: (i, 0))
    vec = pl.BlockSpec((1, d), lambda i: (0, 0))
    return pl.pallas_call(
        kern, name=name, grid=(t // tb,), in_specs=[row, row, vec, vec], out_specs=[row, row, row],
        out_shape=[jax.ShapeDtypeStruct((t, d), F32), jax.ShapeDtypeStruct((t, d), F32),
                   jax.ShapeDtypeStruct((t, d), BF16)],
        compiler_params=_params(("parallel",)),
    )(h_prev, sub, g, b)


def _ln_bwd(xin, dy_or_target, g, b, name, loss_head=False):
    t, d = xin.shape
    tb = _pick(t, (256, 128, 64, 32, 16))
    nb = t // tb

    def kern(x_ref, dy_ref, g_ref, b_ref, dx_ref, dxb_ref, dg_ref, db_ref, *rest):
        i = pl.program_id(0)
        xhat, rstd = _ln_hat(x_ref[...])
        gv = g_ref[...]
        if loss_head:
            loss_ref = rest[0]
            err = xhat * gv + b_ref[...] - dy_ref[...]
            dy = err * (1.0 / d)
            part = 0.5 * jnp.sum(jnp.sum(err * err, axis=1, keepdims=True), axis=0, keepdims=True) * (1.0 / d)
        else:
            dy = dy_ref[...]

        @pl.when(i == 0)
        def _():
            dg_ref[...] = jnp.zeros_like(dg_ref)
            db_ref[...] = jnp.zeros_like(db_ref)
            if loss_head:
                loss_ref[...] = jnp.zeros_like(loss_ref)

        dg_ref[...] += jnp.sum(dy * xhat, axis=0, keepdims=True)
        db_ref[...] += jnp.sum(dy, axis=0, keepdims=True)
        if loss_head:
            loss_ref[...] += jnp.broadcast_to(part, loss_ref.shape)
        dxh = dy * gv
        m1 = jnp.mean(dxh, axis=-1, keepdims=True)
        m2 = jnp.mean(dxh * xhat, axis=-1, keepdims=True)
        dx = rstd * (dxh - m1 - xhat * m2)
        dx_ref[...] = dx
        dxb_ref[...] = dx.astype(BF16)

    row = pl.BlockSpec((tb, d), lambda i: (i, 0))
    vec = pl.BlockSpec((1, d), lambda i: (0, 0))
    out_specs = [row, row, vec, vec]
    out_shape = [jax.ShapeDtypeStruct((t, d), F32), jax.ShapeDtypeStruct((t, d), BF16),
                 jax.ShapeDtypeStruct((1, d), F32), jax.ShapeDtypeStruct((1, d), F32)]
    if loss_head:
        out_specs.append(pl.BlockSpec((1, LANES), lambda i: (0, 0)))
        out_shape.append(jax.ShapeDtypeStruct((1, LANES), F32))
    return pl.pallas_call(
        kern, name=name, grid=(nb,), in_specs=[row, row, vec, vec], out_specs=out_specs, out_shape=out_shape,
        compiler_params=_params(("arbitrary",)),
    )(xin, dy_or_target, g, b)


def _conv_gate_fwd(u, conv_w, conv_b, name):
    t, f2 = u.shape
    f = f2 // 2
    tb = _pick(t, (512, 256, 128, 64, 32, 16))
    cn = _pick(f, (1408, 1024, 512, 256, 128))
    ncb = f // cn
    hb = tb // 8

    def kern(a_ref, ah_ref, b_ref, w_ref, cb_ref, o_ref):
        i = pl.program_id(0)
        a = a_ref[...]
        halo = jnp.where(i > 0, ah_ref[...], 0.0)
        rid = lax.broadcasted_iota(jnp.int32, a.shape, 0)
        s1 = jnp.where(rid == 0, halo[7:8, :], pltpu.roll(a, 1, 0))
        s2 = jnp.where(rid == 0, halo[6:7, :], jnp.where(rid == 1, halo[7:8, :], pltpu.roll(a, 2, 0)))
        w = w_ref[...]
        conv = w[2:3, :] * a + w[1:2, :] * s1 + w[0:1, :] * s2 + cb_ref[...]
        sp, _ = _sigmoid_pair(conv)
        o_ref[...] = (conv * sp * b_ref[...]).astype(BF16)

    return pl.pallas_call(
        kern, name=name, grid=(t // tb, ncb),
        in_specs=[pl.BlockSpec((tb, cn), lambda i, j: (i, j)),
                  pl.BlockSpec((8, cn), lambda i, j: (jnp.maximum(i * hb - 1, 0), j)),
                  pl.BlockSpec((tb, cn), lambda i, j: (i, j + ncb)),
                  pl.BlockSpec((3, cn), lambda i, j: (0, j)),
                  pl.BlockSpec((1, cn), lambda i, j: (0, j))],
        out_specs=pl.BlockSpec((tb, cn), lambda i, j: (i, j)),
        out_shape=jax.ShapeDtypeStruct((t, f), BF16),
        compiler_params=_params(("parallel", "parallel")),
    )(u, u, u, conv_w, conv_b)


def _conv_gate_bwd(u, dgact, conv_w, conv_b, name):
    t, f2 = u.shape
    f = f2 // 2
    tb = _pick(t, (512, 256, 128, 64, 32, 16))
    cn = _pick(f, (1408, 1024, 512, 256, 128))
    ncb = f // cn
    hb = tb // 8
    nb = t // tb
    last8 = t // 8 - 1

    def kern(a_ref, ap_ref, an_ref, b_ref, bn_ref, dg_ref, dgn_ref, w_ref, cb_ref,
             da_ref, db_ref, dw_ref, dcb_ref):
        i = pl.program_id(1)
        a = a_ref[...]
        w = w_ref[...]
        ext = jnp.concatenate([jnp.where(i > 0, ap_ref[...], 0.0), a, an_ref[...]], axis=0)
        e1 = pltpu.roll(ext, 1, 0)
        e2 = pltpu.roll(ext, 2, 0)
        conv = (w[2:3, :] * ext + w[1:2, :] * e1 + w[0:1, :] * e2 + cb_ref[...])[8:, :]
        bmn = jnp.concatenate([b_ref[...], bn_ref[...]], axis=0)
        dgmn = jnp.concatenate([dg_ref[...], jnp.where(i < nb - 1, dgn_ref[...], 0.0)], axis=0)
        sp, sn = _sigmoid_pair(conv)
        da = dgmn * bmn * (sp * (1.0 + conv * sn))
        n = tb + 8
        dap = w[2:3, :] * da + w[1:2, :] * pltpu.roll(da, n - 1, 0) + w[0:1, :] * pltpu.roll(da, n - 2, 0)
        da_ref[...] = dap[:tb, :].astype(BF16)
        db_ref[...] = (dg_ref[...] * (conv * sp)[:tb, :]).astype(BF16)
        dam = da[:tb, :]

        @pl.when(i == 0)
        def _():
            dw_ref[...] = jnp.zeros_like(dw_ref)
            dcb_ref[...] = jnp.zeros_like(dcb_ref)

        dw = jnp.concatenate([jnp.sum(dam * e2[8:8 + tb, :], axis=0, keepdims=True),
                              jnp.sum(dam * e1[8:8 + tb, :], axis=0, keepdims=True),
                              jnp.sum(dam * a, axis=0, keepdims=True)], axis=0)
        dw_ref[...] += dw
        dcb_ref[...] += jnp.sum(dam, axis=0, keepdims=True)

    main_a = pl.BlockSpec((tb, cn), lambda j, i: (i, j))
    prev_a = pl.BlockSpec((8, cn), lambda j, i: (jnp.maximum(i * hb - 1, 0), j))
    next_a = pl.BlockSpec((8, cn), lambda j, i: (jnp.minimum((i + 1) * hb, last8), j))
    main_b = pl.BlockSpec((tb, cn), lambda j, i: (i, j + ncb))
    next_b = pl.BlockSpec((8, cn), lambda j, i: (jnp.minimum((i + 1) * hb, last8), j + ncb))
    return pl.pallas_call(
        kern, name=name, grid=(ncb, nb),
        in_specs=[main_a, prev_a, next_a, main_b, next_b, main_a, next_a,
                  pl.BlockSpec((3, cn), lambda j, i: (0, j)), pl.BlockSpec((1, cn), lambda j, i: (0, j))],
        out_specs=[main_a, main_a, pl.BlockSpec((3, cn), lambda j, i: (0, j)),
                   pl.BlockSpec((1, cn), lambda j, i: (0, j))],
        out_shape=[jax.ShapeDtypeStruct((t, f), BF16), jax.ShapeDtypeStruct((t, f), BF16),
                   jax.ShapeDtypeStruct((3, f), F32), jax.ShapeDtypeStruct((1, f), F32)],
        compiler_params=_params(("parallel", "arbitrary")),
    )(u, u, u, u, u, dgact, dgact, conv_w, conv_b)


def _hg_gates(qp, fp, lb):
    sq, _ = _sigmoid_pair(qp)
    sf, snf = _sigmoid_pair(fp)
    forget = lb + (1.0 - lb) * sf
    return sq, sf, snf, forget, jnp.log(forget), (1.0 - lb) * snf


def _tri(lower):
    r = lax.broadcasted_iota(jnp.int32, (SUB, SUB), 0)
    c = lax.broadcasted_iota(jnp.int32, (SUB, SUB), 1)
    return ((r >= c) if lower else (r <= c)).astype(BF16)


def _split2(x):
    hi = x.astype(BF16)
    return hi, (x - hi.astype(F32)).astype(BF16)


def _dot3(a, b, dims):
    (ah, al), (bh, bl) = a, b
    return (lax.dot_general(ah, bh, dims, preferred_element_type=F32)
            + (lax.dot_general(ah, bl, dims, preferred_element_type=F32)
               + lax.dot_general(al, bh, dims, preferred_element_type=F32)))


def _running_sum(tri, x):
    hi, lo = _split2(x)
    rest = (x - hi.astype(F32)) - lo.astype(F32)
    return (lax.dot_general(tri, hi, NN, preferred_element_type=F32)
            + (lax.dot_general(tri, lo, NN, preferred_element_type=F32)
               + lax.dot_general(tri, rest.astype(BF16), NN, preferred_element_type=F32)))


HEADS_PER_STEP = 8
STEP_UNROLL = 2


def _hgrn2_fwd(proj, lb_logits, norm_g, name, comm=None):
    t, d4 = proj.shape
    d = d4 // 4
    nh = d // LANES
    hb = _pick(nh, (HEADS_PER_STEP, 2, 1))
    wb = hb * LANES
    tb = _pick(t, (256, 128, 64, 32, 16))
    nb = t // tb
    nsc = tb // SUB

    def kern(q_ref, f_ref, i_ref, g_ref, lbl_ref, ng_ref, y_ref, o_ref, st_ref, s_ref):
        @pl.when(pl.program_id(1) == 0)
        def _():
            s_ref[...] = jnp.zeros_like(s_ref)

        lb_all = _lower_bound(lbl_ref[...])
        ng_all = ng_ref[...]
        ltri = _tri(True)
        rcol = lax.broadcasted_iota(jnp.int32, (SUB, 1), 0)

        heads = [slice(h * LANES, (h + 1) * LANES) for h in range(hb)]

        def step(sc, carry):
            rows = pl.ds(pl.multiple_of(sc * SUB, SUB), SUB)
            qp, fp, v, gp = q_ref[rows, :], f_ref[rows, :], i_ref[rows, :], g_ref[rows, :]
            sq, _, _, _, lf, k = _hg_gates(qp, fp, lb_all)
            q = qp * sq
            bl = _running_sum(ltri, lf)
            bend = bl[SUB - 1:SUB, :]
            dec = jnp.exp(bend)
            qs2 = _split2(q * jnp.exp(bl))
            kd2 = _split2(k * jnp.exp(bend - bl))
            v2 = _split2(v)
            states = [s_ref[h] for h in range(hb)]
            o = [_dot3((qs2[0][:, c], qs2[1][:, c]), _split2(states[h]), NT) for h, c in enumerate(heads)]
            top, bot = [oh[:TILE] for oh in o], [oh[TILE:] for oh in o]
            for s in range(SUB):
                lo = 0 if s < TILE else TILE
                e = jnp.exp(jnp.minimum(bl[lo:] - bl[s:s + 1, :], 0.0))
                p = q[lo:] * e * k[s:s + 1, :]
                for h, c in enumerate(heads):
                    a = jnp.sum(p[:, c], axis=1, keepdims=True)
                    add = jnp.where(rcol[lo:] >= s, a, 0.0) * v[s:s + 1, c]
                    if lo == 0:
                        top[h], bot[h] = top[h] + add[:TILE], bot[h] + add[TILE:]
                    else:
                        bot[h] = bot[h] + add
            o = [jnp.concatenate([a, b], axis=0) for a, b in zip(top, bot)]
            for h, c in enumerate(heads):
                st_ref[sc, h] = states[h]
                s_ref[h] = states[h] * dec[:, c] + _dot3((v2[0][:, c], v2[1][:, c]), (kd2[0][:, c], kd2[1][:, c]), TN)
            o_ref[rows, :] = jnp.concatenate(o, axis=1)
            on = jnp.concatenate(
                [oh * lax.rsqrt(jnp.mean(oh * oh, axis=1, keepdims=True) + RMS_EPS) for oh in o], axis=1)
            sg, _ = _sigmoid_pair(gp)
            y_ref[rows, :] = (on * ng_all * (gp * sg)).astype(BF16)
            return carry

        lax.fori_loop(0, nsc, step, 0, unroll=STEP_UNROLL)

    def col(off):
        return pl.BlockSpec((tb, wb), lambda h, j: (j, h + off * (nh // hb)))

    return _carried_call(
        kern, comm, name=name, grid=(nh // hb, nb),
        in_specs=[col(0), col(1), col(2), col(3),
                  pl.BlockSpec((3, wb), lambda h, j: (0, h)), pl.BlockSpec((1, wb), lambda h, j: (0, h))],
        out_specs=[col(0), col(0), pl.BlockSpec((nsc, hb, LANES, LANES), lambda h, j: (j, h, 0, 0))],
        out_shape=[jax.ShapeDtypeStruct((t, d), BF16), jax.ShapeDtypeStruct((t, d), F32),
                   jax.ShapeDtypeStruct((t // SUB, nh, LANES, LANES), F32)],
        scratch_shapes=[pltpu.VMEM((hb, LANES, LANES), F32)],
        operands=[proj, proj, proj, proj, lb_logits, norm_g], sem=("parallel", "arbitrary"))


def _hgrn2_bwd(proj, lb_logits, norm_g, o_raw, states, dy, name, comm=None):
    t, d4 = proj.shape
    d = d4 // 4
    nh = d // LANES
    hb = _pick(nh, (HEADS_PER_STEP, 2, 1))
    wb = hb * LANES
    tb = _pick(t, (256, 128, 64, 32, 16))
    nb = t // tb
    nsc = tb // SUB

    def kern(q_ref, f_ref, i_ref, g_ref, lbl_ref, ng_ref, o_ref, st_ref, dy_ref,
             dq_ref, df_ref, di_ref, dgp_ref, dlb_ref, dng_ref, ds_ref, gc_ref):
        j = pl.program_id(1)

        @pl.when(j == 0)
        def _():
            ds_ref[...] = jnp.zeros_like(ds_ref)
            gc_ref[...] = jnp.zeros_like(gc_ref)
            dlb_ref[...] = jnp.zeros_like(dlb_ref)
            dng_ref[...] = jnp.zeros_like(dng_ref)

        lb_all = _lower_bound(lbl_ref[...])
        ng_all = ng_ref[...]
        ltri, utri = _tri(True), _tri(False)
        rcol = lax.broadcasted_iota(jnp.int32, (SUB, 1), 0)
        rid = lax.broadcasted_iota(jnp.int32, (SUB, wb), 0)

        heads = [slice(h * LANES, (h + 1) * LANES) for h in range(hb)]

        def per_head(fn, n=SUB):
            return jnp.concatenate([jnp.broadcast_to(fn(c), (n, LANES)) for c in heads], axis=1)

        def step(it, carry):
            sc = nsc - 1 - it
            rows = pl.ds(pl.multiple_of(sc * SUB, SUB), SUB)
            qp, fp, v, gp = q_ref[rows, :], f_ref[rows, :], i_ref[rows, :], g_ref[rows, :]
            o, dyv = o_ref[rows, :], dy_ref[rows, :]
            sq, sf, snf, forget, lf, k = _hg_gates(qp, fp, lb_all)
            q = qp * sq
            bl = _running_sum(ltri, lf)
            ebl = jnp.exp(bl)
            bend = bl[SUB - 1:SUB, :]
            dec = jnp.exp(bend)
            dte = jnp.exp(bend - bl)
            r = per_head(lambda c: lax.rsqrt(jnp.mean(o[:, c] * o[:, c], axis=1, keepdims=True) + RMS_EPS))
            ohat = o * r
            sg, sng = _sigmoid_pair(gp)
            don = dyv * (gp * sg)
            dgp_ref[rows, :] = (dyv * (ohat * ng_all) * (sg * (1.0 + gp * sng))).astype(BF16)
            dng_ref[...] += jnp.sum(don * ohat, axis=0, keepdims=True)
            doh = don * ng_all
            dot_oh = doh * ohat
            do = r * (doh - ohat * per_head(lambda c: jnp.mean(dot_oh[:, c], axis=1, keepdims=True)))
            do2, qs2, kd2, v2 = _split2(do), _split2(q * ebl), _split2(k * dte), _split2(v)
            dq_h, dk_h, dv_h = [], [], []
            for h, c in enumerate(heads):
                dstate = ds_ref[h]
                ds2 = _split2(dstate)
                doc = (do2[0][:, c], do2[1][:, c])
                dq_h.append(_dot3(doc, _split2(st_ref[sc, h]), NN))
                dv_h.append(_dot3((kd2[0][:, c], kd2[1][:, c]), ds2, NT))
                dk_h.append(_dot3((v2[0][:, c], v2[1][:, c]), ds2, NN))
                ds_ref[h] = dstate * dec[:, c] + _dot3(doc, (qs2[0][:, c], qs2[1][:, c]), TN)
            dq = jnp.concatenate(dq_h, axis=1) * ebl
            dk = jnp.concatenate(dk_h, axis=1) * dte
            dv = jnp.concatenate(dv_h, axis=1)
            dq_t, dq_b = dq[:TILE], dq[TILE:]
            dk_i = [jnp.zeros((TILE, wb), F32), jnp.zeros((TILE, wb), F32)]
            dv_i = [jnp.zeros((TILE, wb), F32), jnp.zeros((TILE, wb), F32)]
            for s in range(SUB):
                lo = 0 if s < TILE else TILE
                n = SUB - lo
                e = jnp.exp(jnp.minimum(bl[lo:] - bl[s:s + 1, :], 0.0))
                qe = q[lo:] * e
                ks = k[s:s + 1, :]
                live = rcol[lo:] >= s
                pk = qe * ks
                dor = do[lo:]
                pv = dor * v[s:s + 1, :]
                a = per_head(lambda c: jnp.where(live, jnp.sum(pk[:, c], axis=1, keepdims=True), 0.0), n)
                da = per_head(lambda c: jnp.where(live, jnp.sum(pv[:, c], axis=1, keepdims=True), 0.0), n)
                ddq = da * (e * ks)
                if lo == 0:
                    dq_t, dq_b = dq_t + ddq[:TILE], dq_b + ddq[TILE:]
                else:
                    dq_b = dq_b + ddq
                here = rid[:TILE] == s - lo
                dk_i[lo // TILE] = jnp.where(here, jnp.sum(da * qe, axis=0, keepdims=True), dk_i[lo // TILE])
                dv_i[lo // TILE] = jnp.where(here, jnp.sum(a * dor, axis=0, keepdims=True), dv_i[lo // TILE])
            dq = jnp.concatenate([dq_t, dq_b], axis=0)
            dk = dk + jnp.concatenate(dk_i, axis=0)
            dv = dv + jnp.concatenate(dv_i, axis=0)
            w = q * dq - k * dk
            gc = gc_ref[...]
            dlf = _running_sum(utri, w) + gc
            gc_ref[...] = gc + jnp.sum(w, axis=0, keepdims=True)
            t1 = dlf / forget - dk
            df_ref[rows, :] = ((1.0 - lb_all) * sf * snf * t1).astype(BF16)
            dlb_ref[...] += jnp.sum(snf * t1, axis=0, keepdims=True)
            dq_ref[rows, :] = (dq * (sq * (1.0 + qp * (1.0 - sq)))).astype(BF16)
            di_ref[rows, :] = dv.astype(BF16)
            return carry

        lax.fori_loop(0, nsc, step, 0, unroll=STEP_UNROLL)

    def col(off):
        return pl.BlockSpec((tb, wb), lambda h, j: (nb - 1 - j, h + off * (nh // hb)))

    vec = pl.BlockSpec((1, wb), lambda h, j: (0, h))
    return _carried_call(
        kern, comm, name=name, grid=(nh // hb, nb),
        in_specs=[col(0), col(1), col(2), col(3), pl.BlockSpec((3, wb), lambda h, j: (0, h)), vec,
                  col(0), pl.BlockSpec((nsc, hb, LANES, LANES), lambda h, j: (nb - 1 - j, h, 0, 0)), col(0)],
        out_specs=[col(0), col(0), col(0), col(0), vec, vec],
        out_shape=[jax.ShapeDtypeStruct((t, d), BF16)] * 4 + [jax.ShapeDtypeStruct((1, d), F32)] * 2,
        scratch_shapes=[pltpu.VMEM((hb, LANES, LANES), F32), pltpu.VMEM((1, wb), F32)],
        operands=[proj, proj, proj, proj, lb_logits, norm_g, o_raw, states, dy], sem=("parallel", "arbitrary"))


_INV_SQRT2 = 0.7071067811865476
_INV_SQRT2PI = 0.3989422804014327


def _gelu(x):
    return 0.5 * x * (1.0 + lax.erf(x * _INV_SQRT2))


def _gelu_grad(x):
    return 0.5 * (1.0 + lax.erf(x * _INV_SQRT2)) + x * jnp.exp(-0.5 * x * x) * _INV_SQRT2PI


def _causal(w):
    r = lax.broadcasted_iota(jnp.int32, (GCHUNK, GCHUNK), 0)
    c = lax.broadcasted_iota(jnp.int32, (GCHUNK, GCHUNK), 1)
    return jnp.where(r >= c, w, 0.0)


def _sg_gate_fwd(pre, ln_g, ln_b, w_s, b_s_t, name):
    t, d2 = pre.shape
    d = d2 // 2
    ng = d // LANES

    def kern(pre_ref, g_ref, b_ref, ws_ref, bs_ref, y_ref):
        z = _gelu(pre_ref[...])
        u = z[:, :d]
        vhat, _ = _ln_hat(z[:, d:])
        vn = (vhat * g_ref[...] + b_ref[...]).astype(BF16)
        bs = bs_ref[...]
        for g in range(ng):
            cols = slice(g * LANES, (g + 1) * LANES)
            wc = _causal(ws_ref[g]).astype(BF16)
            gate = jnp.dot(wc, vn[:, cols], preferred_element_type=F32) + bs[:, g:g + 1]
            y_ref[:, cols] = (u[:, cols] * gate).astype(BF16)

    vec = pl.BlockSpec((1, d), lambda i: (0, 0))
    return pl.pallas_call(
        kern, name=name, grid=(t // GCHUNK,),
        in_specs=[pl.BlockSpec((GCHUNK, d2), lambda i: (i, 0)), vec, vec,
                  pl.BlockSpec((ng, GCHUNK, GCHUNK), lambda i: (0, 0, 0)),
                  pl.BlockSpec((GCHUNK, ng), lambda i: (0, 0))],
        out_specs=pl.BlockSpec((GCHUNK, d), lambda i: (i, 0)),
        out_shape=jax.ShapeDtypeStruct((t, d), BF16),
        compiler_params=_params(("parallel",)),
    )(pre, ln_g, ln_b, w_s, b_s_t)


def _sg_gate_bwd(pre, dy, ln_g, ln_b, w_s, b_s_t, name):
    t, d2 = pre.shape
    d = d2 // 2
    ng = d // LANES

    def kern(pre_ref, dy_ref, g_ref, b_ref, ws_ref, bs_ref, dpre_ref, dws_ref, dbs_ref, dg_ref, db_ref, dvn_ref):
        @pl.when(pl.program_id(0) == 0)
        def _():
            dws_ref[...] = jnp.zeros_like(dws_ref)
            dbs_ref[...] = jnp.zeros_like(dbs_ref)
            dg_ref[...] = jnp.zeros_like(dg_ref)
            db_ref[...] = jnp.zeros_like(db_ref)

        pre = pre_ref[...]
        z = _gelu(pre)
        u = z[:, :d]
        vhat, rstd = _ln_hat(z[:, d:])
        gv = g_ref[...]
        vn = (vhat * gv + b_ref[...]).astype(BF16)
        bs = bs_ref[...]
        dyv = dy_ref[...]
        gp = _gelu_grad(pre)
        lane = lax.broadcasted_iota(jnp.int32, (GCHUNK, ng), 1)
        dbs = jnp.zeros((GCHUNK, ng), F32)
        for g in range(ng):
            cols = slice(g * LANES, (g + 1) * LANES)
            wc = _causal(ws_ref[g]).astype(BF16)
            vng = vn[:, cols]
            gate = jnp.dot(wc, vng, preferred_element_type=F32) + bs[:, g:g + 1]
            dpre_ref[:, cols] = (dyv[:, cols] * gate * gp[:, cols]).astype(BF16)
            dgate = dyv[:, cols] * u[:, cols]
            dbs = dbs + jnp.where(lane == g, jnp.sum(dgate, axis=1, keepdims=True), 0.0)
            dgb = dgate.astype(BF16)
            dws_ref[g] += _causal(lax.dot_general(dgb, vng, NT, preferred_element_type=F32))
            dvn_ref[:, cols] = lax.dot_general(wc, dgb, TN, preferred_element_type=F32)
        dbs_ref[...] += dbs
        dvn = dvn_ref[...]
        dg_ref[...] += jnp.sum(dvn * vhat, axis=0, keepdims=True)
        db_ref[...] += jnp.sum(dvn, axis=0, keepdims=True)
        dvh = dvn * gv
        m1 = jnp.mean(dvh, axis=-1, keepdims=True)
        m2 = jnp.mean(dvh * vhat, axis=-1, keepdims=True)
        dpre_ref[:, d:] = (rstd * (dvh - m1 - vhat * m2) * gp[:, d:]).astype(BF16)

    vec = pl.BlockSpec((1, d), lambda i: (0, 0))
    wsp = pl.BlockSpec((ng, GCHUNK, GCHUNK), lambda i: (0, 0, 0))
    bsp = pl.BlockSpec((GCHUNK, ng), lambda i: (0, 0))
    return pl.pallas_call(
        kern, name=name, grid=(t // GCHUNK,),
        in_specs=[pl.BlockSpec((GCHUNK, d2), lambda i: (i, 0)), pl.BlockSpec((GCHUNK, d), lambda i: (i, 0)),
                  vec, vec, wsp, bsp],
        out_specs=[pl.BlockSpec((GCHUNK, d2), lambda i: (i, 0)), wsp, bsp, vec, vec],
        out_shape=[jax.ShapeDtypeStruct((t, d2), BF16), jax.ShapeDtypeStruct((ng, GCHUNK, GCHUNK), F32),
                   jax.ShapeDtypeStruct((GCHUNK, ng), F32), jax.ShapeDtypeStruct((1, d), F32),
                   jax.ShapeDtypeStruct((1, d), F32)],
        scratch_shapes=[pltpu.VMEM((GCHUNK, d), F32)],
        compiler_params=_params(("arbitrary",)),
    )(pre, dy, ln_g, ln_b, w_s, b_s_t)


def _adamw_math(w, g, m, v):
    m = ADAM_B1 * m + (1.0 - ADAM_B1) * g
    v = ADAM_B2 * v + (1.0 - ADAM_B2) * (g * g)
    m_hat = m / (1.0 - ADAM_B1 ** ADAM_STEP)
    v_hat = v / (1.0 - ADAM_B2 ** ADAM_STEP)
    return -ADAM_LR * (m_hat / (jnp.sqrt(v_hat) + ADAM_EPS) + ADAM_WD * w), m, v


ADAMW_BLOCK_BYTES = 3 << 19


def _adamw(w, gs, m, v, name, comm=None):
    nl, r, c = w.shape
    rb = _pick(r, tuple(p for p in (512, 256, 128, 64, 32, 16, 8) if p * c * 4 <= ADAMW_BLOCK_BYTES))

    def kern(w_ref, m_ref, v_ref, *rest):
        g_refs, (d_ref, mo_ref, vo_ref, go_ref) = rest[:nl], rest[nl:]
        layer = pl.program_id(0)
        g = g_refs[0][...]
        for k in range(1, nl):
            g = jnp.where(layer == k, g_refs[k][...], g)
        dlt, mm, vv = _adamw_math(w_ref[...], g, m_ref[...], v_ref[...])
        d_ref[...] = dlt
        mo_ref[...] = mm
        vo_ref[...] = vv
        go_ref[...] = g

    blk = pl.BlockSpec((None, rb, c), lambda l, i: (l, i, 0))
    g_specs = [pl.BlockSpec((rb, c), lambda l, i, k=k: (jnp.where(l == k, i, 0), 0)) for k in range(nl)]
    outs, carried = _carried_call(
        kern, comm, name=name, grid=(nl, r // rb), in_specs=[blk] * 3 + g_specs, out_specs=[blk] * 4,
        out_shape=[jax.ShapeDtypeStruct((nl, r, c), F32)] * 4, operands=[w, m, v, *gs], sem=("parallel", "parallel"))
    return outs if comm is None else (outs, carried)


def _lb_logits_grad(lb_logits, dlb, name):
    def kern(l_ref, d_ref, o_ref):
        lg = l_ref[...]
        m = jnp.max(lg, axis=0, keepdims=True)
        e = jnp.exp(lg - m)
        p = e / jnp.sum(e, axis=0, keepdims=True)
        row = lax.broadcasted_iota(jnp.int32, lg.shape, 0)
        o_ref[...] = d_ref[...] * p[0:1, :] * (jnp.where(row == 0, 1.0, 0.0) - p)

    return pl.pallas_call(kern, name=name, out_shape=jax.ShapeDtypeStruct(lb_logits.shape, F32))(lb_logits, dlb)


def _sum_devices(others, own, me, name):
    n, r, c = others.shape
    rb = _pick(r, (512, 256, 128, 64, 32, 16, 8))

    def kern(me_ref, a_ref, own_ref, o_ref):
        mine = own_ref[...]
        acc = jnp.where(me_ref[0] == 0, mine, a_ref[0])
        for i in range(1, n):
            acc = acc + jnp.where(me_ref[0] == i, mine, a_ref[i])
        o_ref[...] = acc

    return pl.pallas_call(
        kern, name=name,
        grid_spec=pltpu.PrefetchScalarGridSpec(
            num_scalar_prefetch=1, grid=(r // rb,),
            in_specs=[pl.BlockSpec((n, rb, c), lambda i, s: (0, i, 0)), pl.BlockSpec((rb, c), lambda i, s: (i, 0))],
            out_specs=pl.BlockSpec((rb, c), lambda i, s: (i, 0))),
        out_shape=jax.ShapeDtypeStruct((r, c), F32),
        compiler_params=_params(("parallel",)),
    )(me, others, own)


def _place():
    x, y, c = lax.axis_index("x"), lax.axis_index("y"), lax.axis_index("c")
    return x, y, c


class _Plan:
    def __init__(self, ins, out_shapes, aliases, n_sems, build):
        self.ins, self.out_shapes, self.aliases, self.n_sems, self.build = list(ins), list(out_shapes), aliases, n_sems, build


def _merge(*plans):
    ins, outs, aliases, subs, sems = [], [], {}, [], 0
    for p in plans:
        for k, v in p.aliases.items():
            aliases[len(ins) + k] = len(outs) + v
        subs.append((p, len(ins), len(outs), sems))
        ins += p.ins
        outs += p.out_shapes
        sems += p.n_sems

    def build(in_refs, out_refs, send_sems, recv_sems, base):
        copies = []
        for p, i0, o0, s0 in subs:
            copies += p.build(in_refs[i0:i0 + len(p.ins)], out_refs[o0:o0 + len(p.out_shapes)], send_sems, recv_sems,
                              base + s0)
        return copies

    return _Plan(ins, outs, aliases, sems, build)


def _remote(src, dst, send_sems, recv_sems, k, to):
    return pltpu.make_async_remote_copy(src_ref=src, dst_ref=dst, send_sem=send_sems.at[k], recv_sem=recv_sems.at[k],
                                        device_id=to, device_id_type=MESH)


def _plan_gather_ici(shards):
    n = len(shards)

    def build(ins, outs, send_sems, recv_sems, base):
        x, y, c = _place()
        me = 2 * x + y
        copies = []
        for a in range(n):
            h = ins[a].shape[0] // 2
            rows = pl.ds(c * h, h)
            for r in (1, 2, 3):
                px, py, _ = _chip_rel(x, y, r)
                copies.append(_remote(ins[a].at[rows, :], outs[a].at[me, rows, :], send_sems, recv_sems,
                                      base + 4 * a + r - 1, (px, py, c)))
            copies.append(_remote(ins[a], outs[a].at[me], send_sems, recv_sems, base + 4 * a + 3, (x, y, 1 - c)))
        return copies

    return _Plan(shards, [jax.ShapeDtypeStruct((N_CHIPS,) + s.shape, s.dtype) for s in shards], {}, 4 * n, build)


def _plan_gather_pass(gathered):
    n = len(gathered)

    def build(ins, outs, send_sems, recv_sems, base):
        x, y, c = _place()
        copies = []
        for a in range(n):
            h = outs[a].shape[1] // 2
            rows = pl.ds(c * h, h)
            for r in (1, 2, 3):
                _, _, shard = _chip_rel(x, y, r)
                piece = outs[a].at[shard, rows, :]
                copies.append(_remote(piece, piece, send_sems, recv_sems, base + 3 * a + r - 1, (x, y, 1 - c)))
        return copies

    return _Plan(gathered, [jax.ShapeDtypeStruct(g.shape, g.dtype) for g in gathered], {a: a for a in range(n)},
                 3 * n, build)


def _plan_swap(split):
    n = len(split)

    def build(ins, outs, send_sems, recv_sems, base):
        x, y, c = _place()
        return [_remote(ins[a].at[j, 1 - c], outs[a].at[j], send_sems, recv_sems, base + N_CHIPS * a + j, (x, y, 1 - c))
                for a in range(n) for j in range(N_CHIPS)]

    return _Plan(split, [jax.ShapeDtypeStruct((N_CHIPS,) + g.shape[2:], g.dtype) for g in split], {}, N_CHIPS * n, build)


def _plan_scatter(parts):
    n = len(parts)

    def build(ins, outs, send_sems, recv_sems, base):
        x, y, c = _place()
        copies = []
        for a in range(n):
            for r in (1, 2, 3):
                px, py, shard = _chip_rel(x, y, r)
                copies.append(_remote(ins[a].at[shard], outs[a].at[r - 1], send_sems, recv_sems, base + 3 * a + r - 1,
                                      (px, py, c)))
        return copies

    return _Plan(parts, [jax.ShapeDtypeStruct((3,) + p.shape[1:], p.dtype) for p in parts], {}, 3 * n, build)


def _plan_join(bufs):
    n = len(bufs)

    def build(ins, outs, send_sems, recv_sems, base):
        x, y, c = _place()
        return [_remote(outs[a].at[c], outs[a].at[c], send_sems, recv_sems, base + a, (x, y, 1 - c)) for a in range(n)]

    return _Plan(bufs, [jax.ShapeDtypeStruct(b.shape, b.dtype) for b in bufs], {a: a for a in range(n)}, n, build)


def _carried_call(kern, plan, *, name, grid, in_specs, out_specs, out_shape, operands, scratch_shapes=(),
                  aliases=None, sem=None):
    n_in, n_out, n_sc = len(operands), len(out_shape), len(scratch_shapes)
    aliases = dict(aliases or {})
    if plan is None:
        outs = pl.pallas_call(kern, name=name, grid=grid, in_specs=in_specs, out_specs=out_specs, out_shape=out_shape,
                              scratch_shapes=list(scratch_shapes), input_output_aliases=aliases,
                              compiler_params=_params(sem))(*operands)
        return list(outs), []
    ci, co = len(plan.ins), len(plan.out_shapes)
    for k, v in plan.aliases.items():
        aliases[n_in + k] = n_out + v
    steps = tuple(grid)

    def body(*refs):
        ins, cins = refs[:n_in], refs[n_in:n_in + ci]
        outs = refs[n_in + ci:n_in + ci + n_out]
        couts = refs[n_in + ci + n_out:n_in + ci + n_out + co]
        scratch = refs[n_in + ci + n_out + co:n_in + ci + n_out + co + n_sc]
        send_sems, recv_sems = refs[-2], refs[-1]
        first = functools.reduce(jnp.logical_and, [pl.program_id(a) == 0 for a in range(len(steps))])
        last = functools.reduce(jnp.logical_and, [pl.program_id(a) == steps[a] - 1 for a in range(len(steps))])

        @pl.when(first)
        def _():
            for cp in plan.build(cins, couts, send_sems, recv_sems, 0):
                cp.start()

        kern(*ins, *outs, *scratch)

        @pl.when(last)
        def _():
            for cp in plan.build(cins, couts, send_sems, recv_sems, 0):
                cp.wait()

    anyspec = pl.BlockSpec(memory_space=pl.ANY)
    outs = pl.pallas_call(
        body, name=name, grid=grid, in_specs=list(in_specs) + [anyspec] * ci,
        out_specs=list(out_specs) + [anyspec] * co, out_shape=list(out_shape) + plan.out_shapes,
        scratch_shapes=list(scratch_shapes) + [pltpu.SemaphoreType.DMA((plan.n_sems,)),
                                               pltpu.SemaphoreType.DMA((plan.n_sems,))],
        input_output_aliases=aliases,
        compiler_params=_params(("arbitrary",) * len(steps)),
    )(*operands, *plan.ins)
    return list(outs[:n_out]), list(outs[n_out:])


def _chip_rel(x, y, r):
    px = x if r < 2 else 1 - x
    py = y if r % 2 == 0 else 1 - y
    return px, py, 2 * px + py


def _allgather_split(arrs, name):
    n = len(arrs)
    slots = 7

    def body(*refs):
        ins, outs = refs[:n], refs[n:2 * n]
        send_sems, recv_sems = refs[2 * n:]
        x, y, c = _place()
        me = 2 * x + y
        sib = (x, y, 1 - c)

        def half(a, shard, hc):
            h = ins[a].shape[0] // 2
            return outs[a].at[shard, pl.ds(hc * h, h), :]

        def src_half(a):
            h = ins[a].shape[0] // 2
            return ins[a].at[pl.ds(c * h, h), :]

        def copy(a, slot, src, dst, to):
            return pltpu.make_async_remote_copy(src_ref=src, dst_ref=dst, send_sem=send_sems.at[a * slots + slot],
                                                recv_sem=recv_sems.at[a * slots + slot], device_id=to,
                                                device_id_type=MESH)

        first = []
        for r in (1, 2, 3):
            px, py, _ = _chip_rel(x, y, r)
            for a in range(n):
                first.append(copy(a, r - 1, src_half(a), half(a, me, c), (px, py, c)))
        own = [copy(a, 6, ins[a], outs[a].at[me], sib) for a in range(n)]
        for cp in first + own:
            cp.start()
        passed = []
        for r in (1, 2, 3):
            _, _, shard = _chip_rel(x, y, r)
            for a in range(n):
                copy(a, r - 1, src_half(a), half(a, shard, c), sib).wait_recv()
                cp = copy(a, 3 + r - 1, half(a, shard, c), half(a, shard, c), sib)
                cp.start()
                passed.append(cp)
        for r in (1, 2, 3):
            _, _, shard = _chip_rel(x, y, r)
            for a in range(n):
                copy(a, 3 + r - 1, src_half(a), half(a, shard, 1 - c), sib).wait_recv()
        for cp in own:
            cp.wait_recv()
        for cp in first + passed + own:
            cp.wait_send()

    anyspec = pl.BlockSpec(memory_space=pl.ANY)
    return pl.pallas_call(
        body, name=name, in_specs=[anyspec] * n, out_specs=[anyspec] * n,
        out_shape=[jax.ShapeDtypeStruct((N_CHIPS,) + a.shape, a.dtype) for a in arrs],
        scratch_shapes=[pltpu.SemaphoreType.DMA((slots * n,)), pltpu.SemaphoreType.DMA((slots * n,))],
        compiler_params=pltpu.CompilerParams(has_side_effects=True),
    )(*arrs)


def _allgather_whole(arr, name):
    def body(in_ref, out_ref, send_sems, recv_sems, loc_sem):
        x, y, c = _place()
        me = 2 * x + y
        local = pltpu.make_async_copy(in_ref, out_ref.at[me], loc_sem)
        local.start()
        sends = []
        for r in (1, 2, 3):
            px, py, _ = _chip_rel(x, y, r)
            sends.append(pltpu.make_async_remote_copy(
                src_ref=in_ref, dst_ref=out_ref.at[me], send_sem=send_sems.at[r - 1], recv_sem=recv_sems.at[r - 1],
                device_id=(px, py, c), device_id_type=MESH))
        for cp in sends:
            cp.start()
        for r in (1, 2, 3):
            px, py, shard = _chip_rel(x, y, r)
            pltpu.make_async_remote_copy(
                src_ref=in_ref, dst_ref=out_ref.at[shard], send_sem=send_sems.at[r - 1], recv_sem=recv_sems.at[r - 1],
                device_id=(px, py, c), device_id_type=MESH).wait_recv()
        for cp in sends:
            cp.wait_send()
        local.wait()

    anyspec = pl.BlockSpec(memory_space=pl.ANY)
    return pl.pallas_call(
        body, name=name, in_specs=[anyspec], out_specs=anyspec,
        out_shape=jax.ShapeDtypeStruct((N_CHIPS,) + arr.shape, arr.dtype),
        scratch_shapes=[pltpu.SemaphoreType.DMA((3,)), pltpu.SemaphoreType.DMA((3,)), pltpu.SemaphoreType.DMA],
        compiler_params=pltpu.CompilerParams(has_side_effects=True),
    )(arr)


def _plan_gather_all(buf):
    def build(ins, outs, send_sems, recv_sems, base):
        x, y, c = _place()
        me = 4 * x + 2 * y + c
        copies = []
        for r in range(1, N_DEV):
            px, py, _ = _chip_rel(x, y, r // 2)
            pc = c if r % 2 == 0 else 1 - c
            copies.append(_remote(ins[0], outs[0].at[me], send_sems, recv_sems, base + r - 1, (px, py, pc)))
        return copies

    return _Plan([buf, jnp.zeros((N_DEV,) + buf.shape, buf.dtype)],
                 [jax.ShapeDtypeStruct((N_DEV,) + buf.shape, buf.dtype)], {1: 0}, N_DEV - 1, build)


def _add_half(grad, recv, sel, name):
    _, _, rh, cw = grad.shape
    rb = _pick(rh, (512, 256, 176, 128, 64, 32, 16, 8))

    def kern(sel_ref, g_ref, r_ref, o_ref):
        o_ref[...] = (g_ref[...] + r_ref[...]).astype(BF16)

    return pl.pallas_call(
        kern, name=name,
        grid_spec=pltpu.PrefetchScalarGridSpec(
            num_scalar_prefetch=1, grid=(N_CHIPS - 1, rh // rb),
            in_specs=[pl.BlockSpec((None, None, rb, cw), lambda j, i, s: (s[2 + j], s[0], i, 0)),
                      pl.BlockSpec((None, rb, cw), lambda j, i, s: (s[2 + j], i, 0))],
            out_specs=pl.BlockSpec((None, rb, cw), lambda j, i, s: (s[2 + j], i, 0))),
        out_shape=jax.ShapeDtypeStruct((N_CHIPS, rh, cw), BF16),
        compiler_params=_params(("parallel", "parallel")),
    )(sel, grad, recv)


def _add_own(grad, recv, got, sel, name):
    _, _, rh, cw = grad.shape
    rb = _pick(rh, (512, 256, 176, 128, 64, 32, 16, 8))

    def kern(sel_ref, g_ref, r_ref, b_ref, o_ref):
        own = g_ref[...] + r_ref[...]
        o_ref[...] = ((own + b_ref[0].astype(F32)) + b_ref[1].astype(F32)) + b_ref[2].astype(F32)

    return pl.pallas_call(
        kern, name=name,
        grid_spec=pltpu.PrefetchScalarGridSpec(
            num_scalar_prefetch=1, grid=(rh // rb,),
            in_specs=[pl.BlockSpec((None, None, rb, cw), lambda i, s: (s[1], s[0], i, 0)),
                      pl.BlockSpec((None, rb, cw), lambda i, s: (s[1], i, 0)),
                      pl.BlockSpec((3, rb, cw), lambda i, s: (0, i, 0))],
            out_specs=pl.BlockSpec((None, rb, cw), lambda i, s: (s[0], i, 0))),
        out_shape=jax.ShapeDtypeStruct((2, rh, cw), F32),
        compiler_params=_params(("parallel",)),
    )(sel, grad, recv, got)


def _stacked(g):
    return g.reshape(1, g.shape[0] * g.shape[1], g.shape[2])


def _halves(g):
    g = g.reshape(N_CHIPS, g.shape[0] * g.shape[1] // N_CHIPS, g.shape[2])
    return g.reshape(N_CHIPS, 2, g.shape[1] // 2, g.shape[2])


def _whole(f):
    return f.reshape(f.shape[0] * f.shape[1], f.shape[2])


def _step(x2, tgt, xb, sh, sm, w, mom, var):
    x, y, c = _place()
    sel = jnp.stack([c, 2 * x + y] + [_chip_rel(x, y, r)[2] for r in (1, 2, 3)]).astype(jnp.int32)
    wg = {}
    wg["hg_in"] = _allgather_split([sh["hg_in"]], "gather_hg_in")[0]
    proj, landed = _matmul(xb, wg["hg_in"], mode="nn", nsh=N_CHIPS, name="hg_in",
                           comm=_plan_gather_ici([sh["hg_out"], sh["dn0"]]))
    (yhg, o_raw, states), got = _hgrn2_fwd(
        proj, sm["lb_logits"], sm["hg_norm_g"], "hgrn2_fwd",
        comm=_merge(_plan_gather_pass(landed), _plan_gather_ici([sh[k] for k in ("up0", "sg_in", "sg_out")])))
    wg["hg_out"], wg["dn0"], landed = got[0], got[1], got[2:]
    mixed, got = _matmul(yhg, _stacked(wg["hg_out"]), mode="nn", nsh=1, name="hg_out", comm=_plan_gather_pass(landed))
    wg["up0"], wg["sg_in"], wg["sg_out"] = got
    xin1, h1, h1b = _res_ln_fwd(x2, mixed, sm["ln1_g"][0:1], sm["ln1_b"][0:1], "l0_ln1")
    u0, landed = _matmul(h1b, wg["up0"], mode="nn", nsh=N_CHIPS, name="l0_ffn_up", comm=_plan_gather_ici([sh["up1"]]))
    gact0 = _conv_gate_fwd(u0, sm["conv_w"][0], sm["conv_b"][0:1], "l0_ffn_gate")
    ffn, got = _matmul(gact0, _stacked(wg["dn0"]), mode="nn", nsh=1, name="l0_ffn_down",
                       comm=_merge(_plan_gather_pass(landed), _plan_gather_ici([sh["dn1"]])))
    wg["up1"], landed = got[0], got[1:]
    xin2, h2, h2b = _res_ln_fwd(h1, ffn, sm["ln2_g"][0:1], sm["ln2_b"][0:1], "l0_ffn_ln")
    pre, got = _matmul(h2b, wg["sg_in"], mode="nn", nsh=N_CHIPS, name="sg_in", comm=_plan_gather_pass(landed))
    wg["dn1"] = got[0]
    ysg = _sg_gate_fwd(pre, sm["sg_ln_g"], sm["sg_ln_b"], sm["sg_w_s"], sm["sg_b_s_t"], "sg_gate")
    mixed = _matmul(ysg, _stacked(wg["sg_out"]), mode="nn", nsh=1, name="sg_out")
    xin3, h3, h3b = _res_ln_fwd(h2, mixed, sm["ln1_g"][1:2], sm["ln1_b"][1:2], "l1_ln1")
    u1 = _matmul(h3b, wg["up1"], mode="nn", nsh=N_CHIPS, name="l1_ffn_up")
    gact1 = _conv_gate_fwd(u1, sm["conv_w"][1], sm["conv_b"][1:2], "l1_ffn_gate")
    ffn = _matmul(gact1, _stacked(wg["dn1"]), mode="nn", nsh=1, name="l1_ffn_down")
    xin4, _, _ = _res_ln_fwd(h3, ffn, sm["ln2_g"][1:2], sm["ln2_b"][1:2], "l1_ffn_ln")

    gs, grad, split, recv, part = {}, {}, {}, {}, {}

    def swap_on(call, keys):
        for k in keys:
            split[k] = _halves(grad[k])
        out, got = call(_plan_swap([split[k] for k in keys]))
        for k, r in zip(keys, got):
            recv[k] = r
            part[k] = _add_half(split[k], r, sel, f"rs_addhalf_{k}")
        return out

    def ffn_bwd(u, gact, hb_in, dxin, dxin_b, w_up, w_down, layer, tag, up, down, waiting):
        dgact = _matmul(dxin_b, _stacked(w_down), mode="nt", nsh=1, name=f"{tag}_ddown")
        grad[down] = _matmul(gact, dxin_b, mode="tn", nsh=1, name=f"{tag}_wdown")
        da, db, dcw, dcb = _conv_gate_bwd(u, dgact, sm["conv_w"][layer], sm["conv_b"][layer:layer + 1], f"{tag}_dgate")
        grad[up] = swap_on(lambda plan: _matmul(hb_in, [da, db], mode="tn", nsh=N_CHIPS, name=f"{tag}_wup", comm=plan),
                           waiting + [down])
        dh = swap_on(lambda plan: _matmul([da, db], w_up, mode="nt", nsh=N_CHIPS, resid=dxin, alpha=ALPHA,
                                          name=f"{tag}_dup", comm=plan), [up])
        return dh, dcw, dcb

    dx, dxb, dg4, db4, loss = _ln_bwd(xin4, tgt, sm["ln2_g"][1:2], sm["ln2_b"][1:2], "l1_ln2_bwd", loss_head=True)
    dh3, dcw1, dcb1 = ffn_bwd(u1, gact1, h3b, dx, dxb, wg["up1"], wg["dn1"], 1, "l1_ffn", "up1", "dn1", [])
    dx, dxb, dg3, db3 = _ln_bwd(xin3, dh3, sm["ln1_g"][1:2], sm["ln1_b"][1:2], "l1_ln1_bwd")
    grad["sg_out"] = _matmul(ysg, dxb, mode="tn", nsh=1, name="sg_wout")
    dysg = swap_on(lambda plan: _matmul(dxb, _stacked(wg["sg_out"]), mode="nt", nsh=1, name="sg_dout", comm=plan),
                   ["sg_out"])
    dpre, gs["sg_w_s"], gs["sg_b_s_t"], gs["sg_ln_g"], gs["sg_ln_b"] = _sg_gate_bwd(
        pre, dysg, sm["sg_ln_g"], sm["sg_ln_b"], sm["sg_w_s"], sm["sg_b_s_t"], "sg_gate_bwd")
    grad["sg_in"] = _matmul(h2b, dpre, mode="tn", nsh=N_CHIPS, name="sg_win")
    dh2 = _matmul(dpre, wg["sg_in"], mode="nt", nsh=N_CHIPS, resid=dx, alpha=ALPHA, name="sg_din")
    dx, dxb, dg2, db2 = _ln_bwd(xin2, dh2, sm["ln2_g"][0:1], sm["ln2_b"][0:1], "l0_ln2_bwd")
    dh1, dcw0, dcb0 = ffn_bwd(u0, gact0, h1b, dx, dxb, wg["up0"], wg["dn0"], 0, "l0_ffn", "up0", "dn0", ["sg_in"])
    dx, dxb, dg1, db1 = _ln_bwd(xin1, dh1, sm["ln1_g"][0:1], sm["ln1_b"][0:1], "l0_ln1_bwd")
    grad["hg_out"] = _matmul(yhg, dxb, mode="tn", nsh=1, name="hg_wout")
    dyhg = swap_on(lambda plan: _matmul(dxb, _stacked(wg["hg_out"]), mode="nt", nsh=1, name="hg_dout", comm=plan),
                   ["hg_out"])
    early = ("dn1", "up1", "sg_out", "sg_in", "dn0", "up0", "hg_out")
    dparts, got = _hgrn2_bwd(proj, sm["lb_logits"], sm["hg_norm_g"], o_raw, states, dyhg, "hgrn2_bwd",
                             comm=_plan_scatter([part[k] for k in early]))
    gs["lb"], gs["hg_norm_g"] = dparts[4], dparts[5]
    gs["ln1_g"] = jnp.concatenate([dg1, dg3], axis=0)
    gs["ln1_b"] = jnp.concatenate([db1, db3], axis=0)
    gs["ln2_g"] = jnp.concatenate([dg2, dg4], axis=0)
    gs["ln2_b"] = jnp.concatenate([db2, db4], axis=0)
    gs["conv_w"] = jnp.stack([dcw0, dcw1], axis=0)
    gs["conv_b"] = jnp.concatenate([dcb0, dcb1], axis=0)
    packed, layout = _pack(gs)
    mine = [_add_own(split[k], recv[k], b, sel, f"rs_addown_{k}") for k, b in zip(early, got)]
    grad["hg_in"], got = _matmul(xb, list(dparts[:4]), mode="tn", nsh=N_CHIPS, name="hg_win",
                                 comm=_merge(_plan_join(mine), _plan_gather_all(packed)))
    red = {k: _whole(f) for k, f in zip(early, got)}
    me8 = jnp.reshape(4 * x + 2 * y + c, (1,)).astype(jnp.int32)
    summed = _unpack(_sum_devices(got[len(early)], packed, me8, "sum_small_grads"), layout)
    gx = swap_on(lambda plan: _matmul(dparts[0], wg["hg_in"], mode="nt", nsh=1, b_off=0, resid=dx, alpha=ALPHA,
                                      name="hg_din_q", comm=plan), ["hg_in"])
    gx, got = _matmul(list(dparts[1:4]), wg["hg_in"], mode="nt", nsh=3, b_off=1, resid=gx, alpha=1.0, name="hg_din_fig",
                      comm=_plan_scatter([part["hg_in"]]))
    mine = _add_own(split["hg_in"], recv["hg_in"], got[0], sel, "rs_addown_hg_in")
    upd = {}
    upd["hg_w_out"], full = _adamw(w["hg_w_out"], [red["hg_out"]], mom["hg_w_out"], var["hg_w_out"], "adamw_hg_w_out",
                                   comm=_plan_join([mine]))
    red["hg_in"] = _whole(full[0])
    for k, src in (("ffn_w_up", ("up0", "up1")), ("ffn_w_down", ("dn0", "dn1")), ("sg_w_in", ("sg_in",)),
                   ("sg_w_out", ("sg_out",)), ("hg_w_in", ("hg_in",))):
        upd[k] = _adamw(w[k], [red[s] for s in src], mom[k], var[k], f"adamw_{k}")
    return loss, gx, summed, upd


_SMALL_ORDER = ("lb", "hg_norm_g", "sg_w_s", "sg_b_s_t", "conv_b", "ln1_g", "ln1_b", "ln2_g", "ln2_b",
                "conv_w", "sg_ln_g", "sg_ln_b")


PACK_ROWS = 512


def _pack(parts):
    flat, layout, off = [], [], 0
    for k in _SMALL_ORDER:
        a = parts[k]
        n = a.size
        pad = (-n) % LANES
        flat.append(jnp.pad(a.reshape(-1), (0, pad)))
        layout.append((k, off, n, a.shape))
        off += n + pad
    flat.append(jnp.zeros(((-off) % (PACK_ROWS * LANES),), F32))
    return jnp.concatenate(flat).reshape(-1, LANES), layout


def _unpack(buf, layout):
    flat = buf.reshape(-1)
    return {k: flat[off:off + n].reshape(shape) for k, off, n, shape in layout}


def kernel(x, lb_logits, hg_w_in, hg_norm_g, hg_w_out, sg_w_in, sg_ln_g, sg_ln_b, sg_w_s, sg_b_s, sg_w_out, ffn_w_up, ffn_conv_w, ffn_conv_b, ffn_w_down, ln1_g, ln1_b, ln2_g, ln2_b, loss_target, m_lb_logits, m_hg_w_in, m_hg_norm_g, m_hg_w_out, m_sg_w_in, m_sg_ln_g, m_sg_ln_b, m_sg_w_s, m_sg_b_s, m_sg_w_out, m_ffn_w_up, m_ffn_conv_w, m_ffn_conv_b, m_ffn_w_down, m_ln1_g, m_ln1_b, m_ln2_g, m_ln2_b, v_lb_logits, v_hg_w_in, v_hg_norm_g, v_hg_w_out, v_sg_w_in, v_sg_ln_g, v_sg_ln_b, v_sg_w_s, v_sg_b_s, v_sg_w_out, v_ffn_w_up, v_ffn_conv_w, v_ffn_conv_b, v_ffn_w_down, v_ln1_g, v_ln1_b, v_ln2_g, v_ln2_b):
    names = ("lb_logits", "hg_w_in", "hg_norm_g", "hg_w_out", "sg_w_in", "sg_ln_g", "sg_ln_b", "sg_w_s", "sg_b_s",
             "sg_w_out", "ffn_w_up", "ffn_conv_w", "ffn_conv_b", "ffn_w_down", "ln1_g", "ln1_b", "ln2_g", "ln2_b")
    w = dict(zip(names, (lb_logits, hg_w_in, hg_norm_g, hg_w_out, sg_w_in, sg_ln_g, sg_ln_b, sg_w_s, sg_b_s,
                         sg_w_out, ffn_w_up, ffn_conv_w, ffn_conv_b, ffn_w_down, ln1_g, ln1_b, ln2_g, ln2_b)))
    mom = dict(zip(names, (m_lb_logits, m_hg_w_in, m_hg_norm_g, m_hg_w_out, m_sg_w_in, m_sg_ln_g, m_sg_ln_b, m_sg_w_s,
                           m_sg_b_s, m_sg_w_out, m_ffn_w_up, m_ffn_conv_w, m_ffn_conv_b, m_ffn_w_down, m_ln1_g,
                           m_ln1_b, m_ln2_g, m_ln2_b)))
    var = dict(zip(names, (v_lb_logits, v_hg_w_in, v_hg_norm_g, v_hg_w_out, v_sg_w_in, v_sg_ln_g, v_sg_ln_b, v_sg_w_s,
                           v_sg_b_s, v_sg_w_out, v_ffn_w_up, v_ffn_conv_w, v_ffn_conv_b, v_ffn_w_down, v_ln1_g,
                           v_ln1_b, v_ln2_g, v_ln2_b)))
    x2, tgt = x[0], loss_target[0]
    d = x2.shape[1]
    fq = ffn_conv_w.shape[2]
    dq = sg_ln_g.shape[1]
    cx, cy, _ = _place()
    me = 2 * cx + cy

    shards = {"hg_in": hg_w_in[0], "hg_out": hg_w_out[0], "sg_in": sg_w_in[0], "sg_out": sg_w_out[0],
              "up0": ffn_w_up[0], "up1": ffn_w_up[1], "dn0": ffn_w_down[0], "dn1": ffn_w_down[1]}
    shards = {k: v.astype(BF16) for k, v in shards.items()}
    wide = max(fq, dq)
    tiny = jnp.concatenate([jnp.pad(ffn_conv_w.reshape(6, fq), ((0, 0), (0, wide - fq))),
                            jnp.pad(sg_ln_g, ((0, 0), (0, wide - dq))),
                            jnp.pad(sg_ln_b, ((0, 0), (0, wide - dq)))], axis=0)
    tiny_all = _allgather_whole(tiny, "gather_small")
    conv_w_full = jnp.transpose(tiny_all[:, 0:6, :fq].reshape(N_CHIPS, 2, 3, fq), (1, 2, 0, 3)).reshape(2, 3, N_CHIPS * fq)
    sm = {"lb_logits": lb_logits, "hg_norm_g": hg_norm_g, "ln1_g": ln1_g, "ln1_b": ln1_b, "ln2_g": ln2_g,
          "ln2_b": ln2_b, "conv_w": conv_w_full, "conv_b": ffn_conv_b,
          "sg_ln_g": tiny_all[:, 6, :dq].reshape(1, N_CHIPS * dq),
          "sg_ln_b": tiny_all[:, 7, :dq].reshape(1, N_CHIPS * dq),
          "sg_w_s": sg_w_s[0], "sg_b_s_t": jnp.transpose(sg_b_s[0])}

    loss_row, grad_x, summed, upd = _step(x2, tgt, x2.astype(BF16), shards, sm, w, mom, var)
    loss = lax.psum(loss_row[0, 0], ("x", "y", "c"))

    grads = {
        "lb_logits": _lb_logits_grad(lb_logits, summed["lb"], "lb_logits_grad"),
        "hg_norm_g": summed["hg_norm_g"],
        "sg_ln_g": lax.dynamic_slice_in_dim(summed["sg_ln_g"], me * dq, dq, axis=1),
        "sg_ln_b": lax.dynamic_slice_in_dim(summed["sg_ln_b"], me * dq, dq, axis=1),
        "sg_w_s": summed["sg_w_s"][None], "sg_b_s": jnp.transpose(summed["sg_b_s_t"])[None],
        "ffn_conv_w": lax.dynamic_slice_in_dim(summed["conv_w"], me * fq, fq, axis=2),
        "ffn_conv_b": summed["conv_b"],
        "ln1_g": summed["ln1_g"], "ln1_b": summed["ln1_b"], "ln2_g": summed["ln2_g"], "ln2_b": summed["ln2_b"],
    }

    delta, new_m, new_v = {}, {}, {}
    for k, (dlt, mm, vv, gg) in upd.items():
        delta[k], new_m[k], new_v[k], grads[k] = dlt, mm, vv, gg
    small_names = [k for k in names if k not in upd]

    def pack_small(src):
        flat = [src[k].reshape(-1) for k in small_names]
        n = sum(a.size for a in flat)
        flat.append(jnp.zeros(((-n) % (PACK_ROWS * LANES),), F32))
        return jnp.concatenate(flat).reshape(1, -1, LANES)

    outs = _adamw(pack_small(w), [pack_small(grads)[0]], pack_small(mom), pack_small(var), "adamw_small")
    off = 0
    for k in small_names:
        n = w[k].size
        for dst, o in zip((delta, new_m, new_v), outs):
            dst[k] = o.reshape(-1)[off:off + n].reshape(w[k].shape)
        off += n

    return (loss, grad_x[None], *[grads[k] for k in names], *[delta[k] for k in names],
            *[new_m[k] for k in names], *[new_v[k] for k in names])
```

```python
import functools

import jax
import jax.numpy as jnp
from jax import lax
from jax.experimental import pallas as pl
from jax.experimental.pallas import tpu as pltpu

F32 = jnp.float32
BF16 = jnp.bfloat16
HI = lax.Precision.HIGHEST
MESH = pl.DeviceIdType.MESH

ALPHA = (2 * 2) ** 0.25
LN_EPS = 1e-5
RMS_EPS = 1e-6
ADAM_LR, ADAM_B1, ADAM_B2, ADAM_EPS, ADAM_WD, ADAM_STEP = 0.001, 0.9, 0.999, 1e-08, 0.01, 10

LANES = 128
SUB = 16
TILE = 8
GCHUNK = 128
VMEM_LIMIT = 56 * 1024 * 1024
N_CHIPS = 4
N_DEV = 8

NT = (((1,), (1,)), ((), ()))
TN = (((0,), (0,)), ((), ()))
NN = (((1,), (0,)), ((), ()))


def _pick(dim, prefs):
    for p in prefs:
        if dim % p == 0:
            return p
    return dim


def _params(sem=None, **kw):
    return pltpu.CompilerParams(dimension_semantics=sem, vmem_limit_bytes=VMEM_LIMIT, **kw)


def _sigmoid_pair(x):
    e = jnp.exp(-jnp.abs(x))
    inv = 1.0 / (1.0 + e)
    pos = x >= 0
    return jnp.where(pos, inv, e * inv), jnp.where(pos, e * inv, inv)


def _sigmoid_gate(x):
    t = 0.5 * jnp.tanh(0.5 * x)
    return 0.5 + t, 0.5 - t


def _ln_hat(x):
    mu = jnp.mean(x, axis=-1, keepdims=True)
    xc = x - mu
    var = jnp.mean(xc * xc, axis=-1, keepdims=True)
    rstd = lax.rsqrt(var + LN_EPS)
    return xc * rstd, rstd


def _lower_bound(logits):
    m = jnp.max(logits, axis=0, keepdims=True)
    e = jnp.exp(logits - m)
    return e[0:1, :] / jnp.sum(e, axis=0, keepdims=True)


MATMUL_VMEM_BUDGET = 40 * 1024 * 1024


def _fit_bk(kdim, bm, bn, out_dtype, has_resid, na=1, nb=1):
    fixed = bm * bn * (4 + 2 * jnp.dtype(out_dtype).itemsize + (8 if has_resid else 0))
    best = LANES
    for bk in range(LANES, kdim + 1, LANES):
        if kdim % bk == 0 and fixed + 4 * bk * (bm * na + bn * nb) <= MATMUL_VMEM_BUDGET:
            best = bk
    return best


def _fit_bm_bk(mdim, prefs, kdim, bn, out_dtype, has_resid, na=1, nb=1):
    best = None
    fits = [bm for bm in prefs if mdim % bm == 0][:2] or [mdim]
    for bm in fits:
        bk = _fit_bk(kdim, bm, bn, out_dtype, has_resid, na, nb)
        if best is None or kdim // bk < kdim // best[1]:
            best = (bm, bk)
    return best


def _matmul(a, b, *, mode, name, out_dtype=F32, resid=None, alpha=1.0, b_off=0, nsh=None, comm=None):
    a_parts = a if isinstance(a, (list, tuple)) else [a]
    b_parts = b if isinstance(b, (list, tuple)) else [b]
    n_parts = max(len(a_parts), len(b_parts))
    if mode == "nn":
        m, kdim = a.shape
        _, _, ns = b.shape
        bn = _pick(ns, (1024, 1408, 512, 256, 128))
        bm, bk = _fit_bm_bk(m, (1024, 512, 256, 128), kdim, bn, out_dtype, resid is not None)
        nps = ns // bn
        grid = (m // bm, nsh * nps, kdim // bk)
        a_specs = [pl.BlockSpec((bm, bk), lambda i, j, k: (i, k))]
        b_specs = [pl.BlockSpec((None, bk, bn), lambda i, j, k: (b_off + j // nps, k, j % nps))]
        o_spec = pl.BlockSpec((bm, bn), lambda i, j, k: (i, j))
        out_shape = jax.ShapeDtypeStruct((m, nsh * ns), out_dtype)
        dims, part_axis, per_part = NN, 2, grid[2]
    elif mode == "nt":
        m = a_parts[0].shape[0]
        _, kdim, ns = b.shape
        bn = _pick(kdim, (512, 256, 128))
        bm, bk = _fit_bm_bk(m, (2048, 1024, 512, 256, 128), ns, bn, out_dtype, resid is not None, na=n_parts)
        kps = ns // bk
        per_part = nsh // n_parts * kps
        grid = (m // bm, kdim // bn, nsh * kps)
        a_specs = [pl.BlockSpec((bm, bk), lambda i, j, k, p=p: (jnp.where(k // per_part == p, i, 0),
                                                                  jnp.where(k // per_part == p, k % per_part, 0)))
                   for p in range(n_parts)]
        b_specs = [pl.BlockSpec((None, bn, bk), lambda i, j, k: (b_off + k // kps, j, k % kps))]
        o_spec = pl.BlockSpec((bm, bn), lambda i, j, k: (i, j))
        out_shape = jax.ShapeDtypeStruct((m, kdim), out_dtype)
        dims, part_axis = NT, 2
    else:
        t, kdim = a.shape
        ns = b_parts[0].shape[1] * n_parts // nsh
        bn = _pick(ns, (1024, 1408, 512, 256, 128))
        bm, bk = _fit_bm_bk(kdim, (1024, 1408, 512, 256, 128), t, bn, out_dtype, resid is not None, nb=n_parts)
        nps = ns // bn
        per_part = nsh // n_parts * nps
        grid = (kdim // bm, nsh * nps, t // bk)
        a_specs = [pl.BlockSpec((bk, bm), lambda i, j, k: (k, i))]
        b_specs = [pl.BlockSpec((bk, bn), lambda i, j, k, p=p: (jnp.where(j // per_part == p, k, 0),
                                                                  jnp.where(j // per_part == p, j % per_part, 0)))
                   for p in range(n_parts)]
        o_spec = pl.BlockSpec((None, bm, bn), lambda i, j, k: (j // nps, i, j % nps))
        out_shape = jax.ShapeDtypeStruct((nsh, kdim, ns), out_dtype)
        dims, part_axis = TN, 1
    nk = grid[2]
    na, nb_ = len(a_parts), len(b_parts)
    has_resid = resid is not None

    def kern(*refs):
        a_refs, b_refs = refs[:na], refs[na:na + nb_]
        r_ref = refs[na + nb_] if has_resid else None
        k = pl.program_id(2)

        def finish(r, o_ref):
            if has_resid:
                r = r + alpha * r_ref[...]
            o_ref[...] = r.astype(o_ref.dtype)

        def add(a_ref, b_ref):
            if nk == 1:
                finish(lax.dot_general(a_ref[...], b_ref[...], dims, preferred_element_type=F32), refs[-1])
                return
            refs[-1][...] += lax.dot_general(a_ref[...], b_ref[...], dims, preferred_element_type=F32)

        if nk > 1:
            @pl.when(k == 0)
            def _():
                refs[-1][...] = jnp.zeros_like(refs[-1])

        if n_parts == 1:
            add(a_refs[0], b_refs[0])
        else:
            which = pl.program_id(part_axis) // per_part
            for p in range(n_parts):
                pl.when(which == p)(functools.partial(add, a_refs[min(p, na - 1)], b_refs[min(p, nb_ - 1)]))
        if nk > 1:
            @pl.when(k == nk - 1)
            def _():
                finish(refs[-1][...], refs[-2])

    in_specs = a_specs + b_specs
    operands = list(a_parts) + list(b_parts)
    if has_resid:
        in_specs.append(pl.BlockSpec((bm, bn), lambda i, j, k: (i, j)))
        operands.append(resid)
    outs, carried = _carried_call(
        kern, comm, name=name, grid=grid, in_specs=in_specs, out_specs=[o_spec], out_shape=[out_shape],
        operands=operands, scratch_shapes=[pltpu.VMEM((bm, bn), F32)] if nk > 1 else [],
        sem=("parallel", "parallel", "arbitrary"))
    return outs[0] if comm is None else (outs[0], carried)


def _res_ln_fwd(h_prev, sub, g, b, name):
    t, d = h_prev.shape
    tb = _pick(t, (256, 128, 64, 32, 16))

    def kern(hp_ref, s_ref, g_ref, b_ref, xin_ref, h_ref, hb_ref):
        xin = ALPHA * hp_ref[...] + s_ref[...]
        xhat, _ = _ln_hat(xin)
        h = xhat * g_ref[...] + b_ref[...]
        xin_ref[...] = xin
        h_ref[...] = h
        hb_ref[...] = h.astype(BF16)

    row = pl.BlockSpec((tb, d), lambda i: (i, 0))
    vec = pl.BlockSpec((1, d), lambda i: (0, 0))
    return pl.pallas_call(
        kern, name=name, grid=(t // tb,), in_specs=[row, row, vec, vec], out_specs=[row, row, row],
        out_shape=[jax.ShapeDtypeStruct((t, d), F32), jax.ShapeDtypeStruct((t, d), F32),
                   jax.ShapeDtypeStruct((t, d), BF16)],
        compiler_params=_params(("parallel",)),
    )(h_prev, sub, g, b)


def _ln_bwd(xin, dy_or_target, g, b, name, loss_head=False):
    t, d = xin.shape
    tb = _pick(t, (256, 128, 64, 32, 16))
    nb = t // tb

    def kern(x_ref, dy_ref, g_ref, b_ref, dx_ref, dxb_ref, dg_ref, db_ref, *rest):
        i = pl.program_id(0)
        xhat, rstd = _ln_hat(x_ref[...])
        gv = g_ref[...]
        if loss_head:
            loss_ref = rest[0]
            err = xhat * gv + b_ref[...] - dy_ref[...]
            dy = err * (1.0 / d)
            part = 0.5 * jnp.sum(jnp.sum(err * err, axis=1, keepdims=True), axis=0, keepdims=True) * (1.0 / d)
        else:
            dy = dy_ref[...]

        @pl.when(i == 0)
        def _():
            dg_ref[...] = jnp.zeros_like(dg_ref)
            db_ref[...] = jnp.zeros_like(db_ref)
            if loss_head:
                loss_ref[...] = jnp.zeros_like(loss_ref)

        dg_ref[...] += jnp.sum(dy * xhat, axis=0, keepdims=True)
        db_ref[...] += jnp.sum(dy, axis=0, keepdims=True)
        if loss_head:
            loss_ref[...] += jnp.broadcast_to(part, loss_ref.shape)
        dxh = dy * gv
        m1 = jnp.mean(dxh, axis=-1, keepdims=True)
        m2 = jnp.mean(dxh * xhat, axis=-1, keepdims=True)
        dx = rstd * (dxh - m1 - xhat * m2)
        dx_ref[...] = dx
        dxb_ref[...] = dx.astype(BF16)

    row = pl.BlockSpec((tb, d), lambda i: (i, 0))
    vec = pl.BlockSpec((1, d), lambda i: (0, 0))
    out_specs = [row, row, vec, vec]
    out_shape = [jax.ShapeDtypeStruct((t, d), F32), jax.ShapeDtypeStruct((t, d), BF16),
                 jax.ShapeDtypeStruct((1, d), F32), jax.ShapeDtypeStruct((1, d), F32)]
    if loss_head:
        out_specs.append(pl.BlockSpec((1, LANES), lambda i: (0, 0)))
        out_shape.append(jax.ShapeDtypeStruct((1, LANES), F32))
    return pl.pallas_call(
        kern, name=name, grid=(nb,), in_specs=[row, row, vec, vec], out_specs=out_specs, out_shape=out_shape,
        compiler_params=_params(("arbitrary",)),
    )(xin, dy_or_target, g, b)


def _conv_gate_fwd(u, conv_w, conv_b, name):
    t, f2 = u.shape
    f = f2 // 2
    tb = _pick(t, (512, 256, 128, 64, 32, 16))
    cn = _pick(f, (1408, 1024, 512, 256, 128))
    ncb = f // cn
    hb = tb // 8

    def kern(a_ref, ah_ref, b_ref, w_ref, cb_ref, o_ref):
        i = pl.program_id(0)
        a = a_ref[...]
        halo = jnp.where(i > 0, ah_ref[...], 0.0)
        rid = lax.broadcasted_iota(jnp.int32, a.shape, 0)
        s1 = jnp.where(rid == 0, halo[7:8, :], pltpu.roll(a, 1, 0))
        s2 = jnp.where(rid == 0, halo[6:7, :], jnp.where(rid == 1, halo[7:8, :], pltpu.roll(a, 2, 0)))
        w = w_ref[...]
        conv = w[2:3, :] * a + w[1:2, :] * s1 + w[0:1, :] * s2 + cb_ref[...]
        sp, _ = _sigmoid_gate(conv)
        o_ref[...] = (conv * sp * b_ref[...]).astype(BF16)

    return pl.pallas_call(
        kern, name=name, grid=(t // tb, ncb),
        in_specs=[pl.BlockSpec((tb, cn), lambda i, j: (i, j)),
                  pl.BlockSpec((8, cn), lambda i, j: (jnp.maximum(i * hb - 1, 0), j)),
                  pl.BlockSpec((tb, cn), lambda i, j: (i, j + ncb)),
                  pl.BlockSpec((3, cn), lambda i, j: (0, j)),
                  pl.BlockSpec((1, cn), lambda i, j: (0, j))],
        out_specs=pl.BlockSpec((tb, cn), lambda i, j: (i, j)),
        out_shape=jax.ShapeDtypeStruct((t, f), BF16),
        compiler_params=_params(("parallel", "parallel")),
    )(u, u, u, conv_w, conv_b)


def _conv_gate_bwd(u, dgact, conv_w, conv_b, name):
    t, f2 = u.shape
    f = f2 // 2
    tb = _pick(t, (512, 256, 128, 64, 32, 16))
    cn = _pick(f, (1408, 1024, 512, 256, 128))
    ncb = f // cn
    hb = tb // 8
    nb = t // tb
    last8 = t // 8 - 1

    def kern(a_ref, ap_ref, an_ref, b_ref, bn_ref, dg_ref, dgn_ref, w_ref, cb_ref,
             da_ref, db_ref, dw_ref, dcb_ref):
        i = pl.program_id(1)
        a = a_ref[...]
        w = w_ref[...]
        ext = jnp.concatenate([jnp.where(i > 0, ap_ref[...], 0.0), a, an_ref[...]], axis=0)
        e1 = pltpu.roll(ext, 1, 0)
        e2 = pltpu.roll(ext, 2, 0)
        conv = (w[2:3, :] * ext + w[1:2, :] * e1 + w[0:1, :] * e2 + cb_ref[...])[8:, :]
        bmn = jnp.concatenate([b_ref[...], bn_ref[...]], axis=0)
        dgmn = jnp.concatenate([dg_ref[...], jnp.where(i < nb - 1, dgn_ref[...], 0.0)], axis=0)
        sp, sn = _sigmoid_gate(conv)
        da = dgmn * bmn * (sp * (1.0 + conv * sn))
        n = tb + 8
        dap = w[2:3, :] * da + w[1:2, :] * pltpu.roll(da, n - 1, 0) + w[0:1, :] * pltpu.roll(da, n - 2, 0)
        da_ref[...] = dap[:tb, :].astype(BF16)
        db_ref[...] = (dg_ref[...] * (conv * sp)[:tb, :]).astype(BF16)
        dam = da[:tb, :]

        @pl.when(i == 0)
        def _():
            dw_ref[...] = jnp.zeros_like(dw_ref)
            dcb_ref[...] = jnp.zeros_like(dcb_ref)

        dw = jnp.concatenate([jnp.sum(dam * e2[8:8 + tb, :], axis=0, keepdims=True),
                              jnp.sum(dam * e1[8:8 + tb, :], axis=0, keepdims=True),
                              jnp.sum(dam * a, axis=0, keepdims=True)], axis=0)
        dw_ref[...] += dw
        dcb_ref[...] += jnp.sum(dam, axis=0, keepdims=True)

    main_a = pl.BlockSpec((tb, cn), lambda j, i: (i, j))
    prev_a = pl.BlockSpec((8, cn), lambda j, i: (jnp.maximum(i * hb - 1, 0), j))
    next_a = pl.BlockSpec((8, cn), lambda j, i: (jnp.minimum((i + 1) * hb, last8), j))
    main_b = pl.BlockSpec((tb, cn), lambda j, i: (i, j + ncb))
    next_b = pl.BlockSpec((8, cn), lambda j, i: (jnp.minimum((i + 1) * hb, last8), j + ncb))
    return pl.pallas_call(
        kern, name=name, grid=(ncb, nb),
        in_specs=[main_a, prev_a, next_a, main_b, next_b, main_a, next_a,
                  pl.BlockSpec((3, cn), lambda j, i: (0, j)), pl.BlockSpec((1, cn), lambda j, i: (0, j))],
        out_specs=[main_a, main_a, pl.BlockSpec((3, cn), lambda j, i: (0, j)),
                   pl.BlockSpec((1, cn), lambda j, i: (0, j))],
        out_shape=[jax.ShapeDtypeStruct((t, f), BF16), jax.ShapeDtypeStruct((t, f), BF16),
                   jax.ShapeDtypeStruct((3, f), F32), jax.ShapeDtypeStruct((1, f), F32)],
        compiler_params=_params(("parallel", "arbitrary")),
    )(u, u, u, u, u, dgact, dgact, conv_w, conv_b)


def _hg_gates(qp, fp, lb):
    sq, _ = _sigmoid_gate(qp)
    sf, snf = _sigmoid_pair(fp)
    forget = lb + (1.0 - lb) * sf
    return sq, sf, snf, forget, jnp.log(forget), (1.0 - lb) * snf


def _tri(lower):
    r = lax.broadcasted_iota(jnp.int32, (SUB, SUB), 0)
    c = lax.broadcasted_iota(jnp.int32, (SUB, SUB), 1)
    return ((r >= c) if lower else (r <= c)).astype(BF16)


def _split2(x):
    hi = x.astype(BF16)
    return hi, (x - hi.astype(F32)).astype(BF16)


def _dot3(a, b, dims):
    (ah, al), (bh, bl) = a, b
    return (lax.dot_general(ah, bh, dims, preferred_element_type=F32)
            + (lax.dot_general(ah, bl, dims, preferred_element_type=F32)
               + lax.dot_general(al, bh, dims, preferred_element_type=F32)))


def _running_sum(tri, x):
    hi, lo = _split2(x)
    rest = (x - hi.astype(F32)) - lo.astype(F32)
    return (lax.dot_general(tri, hi, NN, preferred_element_type=F32)
            + (lax.dot_general(tri, lo, NN, preferred_element_type=F32)
               + lax.dot_general(tri, rest.astype(BF16), NN, preferred_element_type=F32)))


HEADS_PER_STEP = 8
STEP_UNROLL = 2


def _hgrn2_fwd(proj, lb_logits, norm_g, name, comm=None):
    t, d4 = proj.shape
    d = d4 // 4
    nh = d // LANES
    hb = _pick(nh, (HEADS_PER_STEP, 2, 1))
    wb = hb * LANES
    tb = _pick(t, (256, 128, 64, 32, 16))
    nb = t // tb
    nsc = tb // SUB

    def kern(q_ref, f_ref, i_ref, g_ref, lbl_ref, ng_ref, y_ref, o_ref, st_ref, s_ref):
        @pl.when(pl.program_id(1) == 0)
        def _():
            s_ref[...] = jnp.zeros_like(s_ref)

        lb_all = _lower_bound(lbl_ref[...])
        ng_all = ng_ref[...]
        ltri = _tri(True)
        rcol = lax.broadcasted_iota(jnp.int32, (SUB, 1), 0)

        heads = [slice(h * LANES, (h + 1) * LANES) for h in range(hb)]

        def step(sc, carry):
            rows = pl.ds(pl.multiple_of(sc * SUB, SUB), SUB)
            qp, fp, v, gp = q_ref[rows, :], f_ref[rows, :], i_ref[rows, :], g_ref[rows, :]
            sq, _, _, _, lf, k = _hg_gates(qp, fp, lb_all)
            q = qp * sq
            bl = _running_sum(ltri, lf)
            bend = bl[SUB - 1:SUB, :]
            dec = jnp.exp(bend)
            qs2 = _split2(q * jnp.exp(bl))
            kd2 = _split2(k * jnp.exp(bend - bl))
            v2 = _split2(v)
            states = [s_ref[h] for h in range(hb)]
            o = [_dot3((qs2[0][:, c], qs2[1][:, c]), _split2(states[h]), NT) for h, c in enumerate(heads)]
            top, bot = [oh[:TILE] for oh in o], [oh[TILE:] for oh in o]
            for s in range(SUB):
                lo = 0 if s < TILE else TILE
                e = jnp.exp(jnp.minimum(bl[lo:] - bl[s:s + 1, :], 0.0))
                p = q[lo:] * e * k[s:s + 1, :]
                for h, c in enumerate(heads):
                    a = jnp.sum(p[:, c], axis=1, keepdims=True)
                    add = jnp.where(rcol[lo:] >= s, a, 0.0) * v[s:s + 1, c]
                    if lo == 0:
                        top[h], bot[h] = top[h] + add[:TILE], bot[h] + add[TILE:]
                    else:
                        bot[h] = bot[h] + add
            o = [jnp.concatenate([a, b], axis=0) for a, b in zip(top, bot)]
            for h, c in enumerate(heads):
                st_ref[sc, h] = states[h]
                s_ref[h] = states[h] * dec[:, c] + _dot3((v2[0][:, c], v2[1][:, c]), (kd2[0][:, c], kd2[1][:, c]), TN)
            o_ref[rows, :] = jnp.concatenate(o, axis=1)
            on = jnp.concatenate(
                [oh * lax.rsqrt(jnp.mean(oh * oh, axis=1, keepdims=True) + RMS_EPS) for oh in o], axis=1)
            sg, _ = _sigmoid_gate(gp)
            y_ref[rows, :] = (on * ng_all * (gp * sg)).astype(BF16)
            return carry

        lax.fori_loop(0, nsc, step, 0, unroll=STEP_UNROLL)

    def col(off):
        return pl.BlockSpec((tb, wb), lambda h, j: (j, h + off * (nh // hb)))

    return _carried_call(
        kern, comm, name=name, grid=(nh // hb, nb),
        in_specs=[col(0), col(1), col(2), col(3),
                  pl.BlockSpec((3, wb), lambda h, j: (0, h)), pl.BlockSpec((1, wb), lambda h, j: (0, h))],
        out_specs=[col(0), col(0), pl.BlockSpec((nsc, hb, LANES, LANES), lambda h, j: (j, h, 0, 0))],
        out_shape=[jax.ShapeDtypeStruct((t, d), BF16), jax.ShapeDtypeStruct((t, d), F32),
                   jax.ShapeDtypeStruct((t // SUB, nh, LANES, LANES), F32)],
        scratch_shapes=[pltpu.VMEM((hb, LANES, LANES), F32)],
        operands=[proj, proj, proj, proj, lb_logits, norm_g], sem=("parallel", "arbitrary"))


def _hgrn2_bwd(proj, lb_logits, norm_g, o_raw, states, dy, name, comm=None):
    t, d4 = proj.shape
    d = d4 // 4
    nh = d // LANES
    hb = _pick(nh, (HEADS_PER_STEP, 2, 1))
    wb = hb * LANES
    tb = _pick(t, (256, 128, 64, 32, 16))
    nb = t // tb
    nsc = tb // SUB

    def kern(q_ref, f_ref, i_ref, g_ref, lbl_ref, ng_ref, o_ref, st_ref, dy_ref,
             dq_ref, df_ref, di_ref, dgp_ref, dlb_ref, dng_ref, ds_ref, gc_ref):
        j = pl.program_id(1)

        @pl.when(j == 0)
        def _():
            ds_ref[...] = jnp.zeros_like(ds_ref)
            gc_ref[...] = jnp.zeros_like(gc_ref)
            dlb_ref[...] = jnp.zeros_like(dlb_ref)
            dng_ref[...] = jnp.zeros_like(dng_ref)

        lb_all = _lower_bound(lbl_ref[...])
        ng_all = ng_ref[...]
        ltri, utri = _tri(True), _tri(False)
        rcol = lax.broadcasted_iota(jnp.int32, (SUB, 1), 0)
        rid = lax.broadcasted_iota(jnp.int32, (SUB, wb), 0)

        heads = [slice(h * LANES, (h + 1) * LANES) for h in range(hb)]

        def per_head(fn, n=SUB):
            return jnp.concatenate([jnp.broadcast_to(fn(c), (n, LANES)) for c in heads], axis=1)

        def step(it, carry):
            sc = nsc - 1 - it
            rows = pl.ds(pl.multiple_of(sc * SUB, SUB), SUB)
            qp, fp, v, gp = q_ref[rows, :], f_ref[rows, :], i_ref[rows, :], g_ref[rows, :]
            o, dyv = o_ref[rows, :], dy_ref[rows, :]
            sq, sf, snf, forget, lf, k = _hg_gates(qp, fp, lb_all)
            q = qp * sq
            bl = _running_sum(ltri, lf)
            ebl = jnp.exp(bl)
            bend = bl[SUB - 1:SUB, :]
            dec = jnp.exp(bend)
            dte = jnp.exp(bend - bl)
            r = per_head(lambda c: lax.rsqrt(jnp.mean(o[:, c] * o[:, c], axis=1, keepdims=True) + RMS_EPS))
            ohat = o * r
            sg, sng = _sigmoid_gate(gp)
            don = dyv * (gp * sg)
            dgp_ref[rows, :] = (dyv * (ohat * ng_all) * (sg * (1.0 + gp * sng))).astype(BF16)
            dng_ref[...] += jnp.sum(don * ohat, axis=0, keepdims=True)
            doh = don * ng_all
            dot_oh = doh * ohat
            do = r * (doh - ohat * per_head(lambda c: jnp.mean(dot_oh[:, c], axis=1, keepdims=True)))
            do2, qs2, kd2, v2 = _split2(do), _split2(q * ebl), _split2(k * dte), _split2(v)
            dq_h, dk_h, dv_h = [], [], []
            for h, c in enumerate(heads):
                dstate = ds_ref[h]
                ds2 = _split2(dstate)
                doc = (do2[0][:, c], do2[1][:, c])
                dq_h.append(_dot3(doc, _split2(st_ref[sc, h]), NN))
                dv_h.append(_dot3((kd2[0][:, c], kd2[1][:, c]), ds2, NT))
                dk_h.append(_dot3((v2[0][:, c], v2[1][:, c]), ds2, NN))
                ds_ref[h] = dstate * dec[:, c] + _dot3(doc, (qs2[0][:, c], qs2[1][:, c]), TN)
            dq = jnp.concatenate(dq_h, axis=1) * ebl
            dk = jnp.concatenate(dk_h, axis=1) * dte
            dv = jnp.concatenate(dv_h, axis=1)
            dq_t, dq_b = dq[:TILE], dq[TILE:]
            dk_i = [jnp.zeros((TILE, wb), F32), jnp.zeros((TILE, wb), F32)]
            dv_i = [jnp.zeros((TILE, wb), F32), jnp.zeros((TILE, wb), F32)]
            for s in range(SUB):
                lo = 0 if s < TILE else TILE
                n = SUB - lo
                e = jnp.exp(jnp.minimum(bl[lo:] - bl[s:s + 1, :], 0.0))
                qe = q[lo:] * e
                ks = k[s:s + 1, :]
                live = rcol[lo:] >= s
                pk = qe * ks
                dor = do[lo:]
                pv = dor * v[s:s + 1, :]
                a = per_head(lambda c: jnp.where(live, jnp.sum(pk[:, c], axis=1, keepdims=True), 0.0), n)
                da = per_head(lambda c: jnp.where(live, jnp.sum(pv[:, c], axis=1, keepdims=True), 0.0), n)
                ddq = da * (e * ks)
                if lo == 0:
                    dq_t, dq_b = dq_t + ddq[:TILE], dq_b + ddq[TILE:]
                else:
                    dq_b = dq_b + ddq
                here = rid[:TILE] == s - lo
                dk_i[lo // TILE] = jnp.where(here, jnp.sum(da * qe, axis=0, keepdims=True), dk_i[lo // TILE])
                dv_i[lo // TILE] = jnp.where(here, jnp.sum(a * dor, axis=0, keepdims=True), dv_i[lo // TILE])
            dq = jnp.concatenate([dq_t, dq_b], axis=0)
            dk = dk + jnp.concatenate(dk_i, axis=0)
            dv = dv + jnp.concatenate(dv_i, axis=0)
            w = q * dq - k * dk
            gc = gc_ref[...]
            dlf = _running_sum(utri, w) + gc
            gc_ref[...] = gc + jnp.sum(w, axis=0, keepdims=True)
            t1 = dlf / forget - dk
            df_ref[rows, :] = ((1.0 - lb_all) * sf * snf * t1).astype(BF16)
            dlb_ref[...] += jnp.sum(snf * t1, axis=0, keepdims=True)
            dq_ref[rows, :] = (dq * (sq * (1.0 + qp * (1.0 - sq)))).astype(BF16)
            di_ref[rows, :] = dv.astype(BF16)
            return carry

        lax.fori_loop(0, nsc, step, 0, unroll=STEP_UNROLL)

    def col(off):
        return pl.BlockSpec((tb, wb), lambda h, j: (nb - 1 - j, h + off * (nh // hb)))

    vec = pl.BlockSpec((1, wb), lambda h, j: (0, h))
    return _carried_call(
        kern, comm, name=name, grid=(nh // hb, nb),
        in_specs=[col(0), col(1), col(2), col(3), pl.BlockSpec((3, wb), lambda h, j: (0, h)), vec,
                  col(0), pl.BlockSpec((nsc, hb, LANES, LANES), lambda h, j: (nb - 1 - j, h, 0, 0)), col(0)],
        out_specs=[col(0), col(0), col(0), col(0), vec, vec],
        out_shape=[jax.ShapeDtypeStruct((t, d), BF16)] * 4 + [jax.ShapeDtypeStruct((1, d), F32)] * 2,
        scratch_shapes=[pltpu.VMEM((hb, LANES, LANES), F32), pltpu.VMEM((1, wb), F32)],
        operands=[proj, proj, proj, proj, lb_logits, norm_g, o_raw, states, dy], sem=("parallel", "arbitrary"))


_INV_SQRT2 = 0.7071067811865476
_INV_SQRT2PI = 0.3989422804014327


def _gelu(x):
    return 0.5 * x * (1.0 + lax.erf(x * _INV_SQRT2))


def _gelu_grad(x):
    return 0.5 * (1.0 + lax.erf(x * _INV_SQRT2)) + x * jnp.exp(-0.5 * x * x) * _INV_SQRT2PI


def _causal(w):
    r = lax.broadcasted_iota(jnp.int32, (GCHUNK, GCHUNK), 0)
    c = lax.broadcasted_iota(jnp.int32, (GCHUNK, GCHUNK), 1)
    return jnp.where(r >= c, w, 0.0)


def _sg_gate_fwd(pre, ln_g, ln_b, w_s, b_s_t, name):
    t, d2 = pre.shape
    d = d2 // 2
    ng = d // LANES

    def kern(pre_ref, g_ref, b_ref, ws_ref, bs_ref, y_ref):
        z = _gelu(pre_ref[...])
        u = z[:, :d]
        vhat, _ = _ln_hat(z[:, d:])
        vn = (vhat * g_ref[...] + b_ref[...]).astype(BF16)
        bs = bs_ref[...]
        for g in range(ng):
            cols = slice(g * LANES, (g + 1) * LANES)
            wc = _causal(ws_ref[g]).astype(BF16)
            gate = jnp.dot(wc, vn[:, cols], preferred_element_type=F32) + bs[:, g:g + 1]
            y_ref[:, cols] = (u[:, cols] * gate).astype(BF16)

    vec = pl.BlockSpec((1, d), lambda i: (0, 0))
    return pl.pallas_call(
        kern, name=name, grid=(t // GCHUNK,),
        in_specs=[pl.BlockSpec((GCHUNK, d2), lambda i: (i, 0)), vec, vec,
                  pl.BlockSpec((ng, GCHUNK, GCHUNK), lambda i: (0, 0, 0)),
                  pl.BlockSpec((GCHUNK, ng), lambda i: (0, 0))],
        out_specs=pl.BlockSpec((GCHUNK, d), lambda i: (i, 0)),
        out_shape=jax.ShapeDtypeStruct((t, d), BF16),
        compiler_params=_params(("parallel",)),
    )(pre, ln_g, ln_b, w_s, b_s_t)


def _sg_gate_bwd(pre, dy, ln_g, ln_b, w_s, b_s_t, name):
    t, d2 = pre.shape
    d = d2 // 2
    ng = d // LANES

    def kern(pre_ref, dy_ref, g_ref, b_ref, ws_ref, bs_ref, dpre_ref, dws_ref, dbs_ref, dg_ref, db_ref, dvn_ref):
        @pl.when(pl.program_id(0) == 0)
        def _():
            dws_ref[...] = jnp.zeros_like(dws_ref)
            dbs_ref[...] = jnp.zeros_like(dbs_ref)
            dg_ref[...] = jnp.zeros_like(dg_ref)
            db_ref[...] = jnp.zeros_like(db_ref)

        pre = pre_ref[...]
        z = _gelu(pre)
        u = z[:, :d]
        vhat, rstd = _ln_hat(z[:, d:])
        gv = g_ref[...]
        vn = (vhat * gv + b_ref[...]).astype(BF16)
        bs = bs_ref[...]
        dyv = dy_ref[...]
        gp = _gelu_grad(pre)
        lane = lax.broadcasted_iota(jnp.int32, (GCHUNK, ng), 1)
        dbs = jnp.zeros((GCHUNK, ng), F32)
        for g in range(ng):
            cols = slice(g * LANES, (g + 1) * LANES)
            wc = _causal(ws_ref[g]).astype(BF16)
            vng = vn[:, cols]
            gate = jnp.dot(wc, vng, preferred_element_type=F32) + bs[:, g:g + 1]
            dpre_ref[:, cols] = (dyv[:, cols] * gate * gp[:, cols]).astype(BF16)
            dgate = dyv[:, cols] * u[:, cols]
            dbs = dbs + jnp.where(lane == g, jnp.sum(dgate, axis=1, keepdims=True), 0.0)
            dgb = dgate.astype(BF16)
            dws_ref[g] += _causal(lax.dot_general(dgb, vng, NT, preferred_element_type=F32))
            dvn_ref[:, cols] = lax.dot_general(wc, dgb, TN, preferred_element_type=F32)
        dbs_ref[...] += dbs
        dvn = dvn_ref[...]
        dg_ref[...] += jnp.sum(dvn * vhat, axis=0, keepdims=True)
        db_ref[...] += jnp.sum(dvn, axis=0, keepdims=True)
        dvh = dvn * gv
        m1 = jnp.mean(dvh, axis=-1, keepdims=True)
        m2 = jnp.mean(dvh * vhat, axis=-1, keepdims=True)
        dpre_ref[:, d:] = (rstd * (dvh - m1 - vhat * m2) * gp[:, d:]).astype(BF16)

    vec = pl.BlockSpec((1, d), lambda i: (0, 0))
    wsp = pl.BlockSpec((ng, GCHUNK, GCHUNK), lambda i: (0, 0, 0))
    bsp = pl.BlockSpec((GCHUNK, ng), lambda i: (0, 0))
    return pl.pallas_call(
        kern, name=name, grid=(t // GCHUNK,),
        in_specs=[pl.BlockSpec((GCHUNK, d2), lambda i: (i, 0)), pl.BlockSpec((GCHUNK, d), lambda i: (i, 0)),
                  vec, vec, wsp, bsp],
        out_specs=[pl.BlockSpec((GCHUNK, d2), lambda i: (i, 0)), wsp, bsp, vec, vec],
        out_shape=[jax.ShapeDtypeStruct((t, d2), BF16), jax.ShapeDtypeStruct((ng, GCHUNK, GCHUNK), F32),
                   jax.ShapeDtypeStruct((GCHUNK, ng), F32), jax.ShapeDtypeStruct((1, d), F32),
                   jax.ShapeDtypeStruct((1, d), F32)],
        scratch_shapes=[pltpu.VMEM((GCHUNK, d), F32)],
        compiler_params=_params(("arbitrary",)),
    )(pre, dy, ln_g, ln_b, w_s, b_s_t)


def _adamw_math(w, g, m, v):
    m = ADAM_B1 * m + (1.0 - ADAM_B1) * g
    v = ADAM_B2 * v + (1.0 - ADAM_B2) * (g * g)
    m_hat = m / (1.0 - ADAM_B1 ** ADAM_STEP)
    v_hat = v / (1.0 - ADAM_B2 ** ADAM_STEP)
    return -ADAM_LR * (m_hat / (jnp.sqrt(v_hat) + ADAM_EPS) + ADAM_WD * w), m, v


ADAMW_BLOCK_BYTES = 3 << 19


def _adamw(w, gs, m, v, name, comm=None):
    nl, r, c = w.shape
    rb = _pick(r, tuple(p for p in (512, 256, 128, 64, 32, 16, 8) if p * c * 4 <= ADAMW_BLOCK_BYTES))

    def kern(w_ref, m_ref, v_ref, *rest):
        g_refs, (d_ref, mo_ref, vo_ref, go_ref) = rest[:nl], rest[nl:]
        layer = pl.program_id(0)
        g = g_refs[0][...]
        for k in range(1, nl):
            g = jnp.where(layer == k, g_refs[k][...], g)
        dlt, mm, vv = _adamw_math(w_ref[...], g, m_ref[...], v_ref[...])
        d_ref[...] = dlt
        mo_ref[...] = mm
        vo_ref[...] = vv
        go_ref[...] = g

    blk = pl.BlockSpec((None, rb, c), lambda l, i: (l, i, 0))
    g_specs = [pl.BlockSpec((rb, c), lambda l, i, k=k: (jnp.where(l == k, i, 0), 0)) for k in range(nl)]
    outs, carried = _carried_call(
        kern, comm, name=name, grid=(nl, r // rb), in_specs=[blk] * 3 + g_specs, out_specs=[blk] * 4,
        out_shape=[jax.ShapeDtypeStruct((nl, r, c), F32)] * 4, operands=[w, m, v, *gs], sem=("parallel", "parallel"))
    return outs if comm is None else (outs, carried)


CAST_BLOCK_BYTES = 1 << 21


def _cast_bf16(items, name, comm=None):
    metas, start = [], 0
    for arr, _ in items:
        _, r, c = arr.shape
        rb = _pick(r, tuple(p for p in (1024, 512, 256, 128, 64, 32, 16) if p * c * 4 <= CAST_BLOCK_BYTES))
        metas.append((start, r // rb, rb, c))
        start += r // rb
    n = len(items)

    def spec(p, layer=None):
        s0, steps, rb, c = metas[p]
        if layer is None:
            return pl.BlockSpec((rb, c), lambda s: (jnp.clip(s - s0, 0, steps - 1), 0))
        return pl.BlockSpec((None, rb, c), lambda s: (layer, jnp.clip(s - s0, 0, steps - 1), 0))

    def kern(*refs):
        s = pl.program_id(0)
        for p in range(n):
            s0, steps, _, _ = metas[p]

            @pl.when(jnp.logical_and(s >= s0, s < s0 + steps))
            def _(p=p):
                refs[n + p][...] = refs[p][...].astype(BF16)

    outs, carried = _carried_call(
        kern, comm, name=name, grid=(start,), in_specs=[spec(p, layer) for p, (_, layer) in enumerate(items)],
        out_specs=[spec(p) for p in range(n)],
        out_shape=[jax.ShapeDtypeStruct(arr.shape[1:], BF16) for arr, _ in items],
        operands=[arr for arr, _ in items], sem=("arbitrary",))
    return outs, carried


def _comm_call(plan, name):
    ni, no = len(plan.ins), len(plan.out_shapes)

    def body(*refs):
        copies = plan.build(refs[:ni], refs[ni:ni + no], refs[-2], refs[-1], 0)
        for cp in copies:
            cp.start()
        for cp in copies:
            cp.wait()

    anyspec = pl.BlockSpec(memory_space=pl.ANY)
    return pl.pallas_call(
        body, name=name, in_specs=[anyspec] * ni, out_specs=[anyspec] * no, out_shape=plan.out_shapes,
        input_output_aliases=plan.aliases,
        scratch_shapes=[pltpu.SemaphoreType.DMA((plan.n_sems,)), pltpu.SemaphoreType.DMA((plan.n_sems,))],
        compiler_params=pltpu.CompilerParams(has_side_effects=True),
    )(*plan.ins)


def _lb_logits_grad(lb_logits, dlb, name):
    def kern(l_ref, d_ref, o_ref):
        lg = l_ref[...]
        m = jnp.max(lg, axis=0, keepdims=True)
        e = jnp.exp(lg - m)
        p = e / jnp.sum(e, axis=0, keepdims=True)
        row = lax.broadcasted_iota(jnp.int32, lg.shape, 0)
        o_ref[...] = d_ref[...] * p[0:1, :] * (jnp.where(row == 0, 1.0, 0.0) - p)

    return pl.pallas_call(kern, name=name, out_shape=jax.ShapeDtypeStruct(lb_logits.shape, F32))(lb_logits, dlb)


def _sum_devices(others, own, me, name):
    n, r, c = others.shape
    rb = _pick(r, (512, 256, 128, 64, 32, 16, 8))

    def kern(me_ref, a_ref, own_ref, o_ref):
        mine = own_ref[...]
        acc = jnp.where(me_ref[0] == 0, mine, a_ref[0])
        for i in range(1, n):
            acc = acc + jnp.where(me_ref[0] == i, mine, a_ref[i])
        o_ref[...] = acc

    return pl.pallas_call(
        kern, name=name,
        grid_spec=pltpu.PrefetchScalarGridSpec(
            num_scalar_prefetch=1, grid=(r // rb,),
            in_specs=[pl.BlockSpec((n, rb, c), lambda i, s: (0, i, 0)), pl.BlockSpec((rb, c), lambda i, s: (i, 0))],
            out_specs=pl.BlockSpec((rb, c), lambda i, s: (i, 0))),
        out_shape=jax.ShapeDtypeStruct((r, c), F32),
        compiler_params=_params(("parallel",)),
    )(me, others, own)


def _place():
    x, y, c = lax.axis_index("x"), lax.axis_index("y"), lax.axis_index("c")
    return x, y, c


class _Plan:
    def __init__(self, ins, out_shapes, aliases, n_sems, build):
        self.ins, self.out_shapes, self.aliases, self.n_sems, self.build = list(ins), list(out_shapes), aliases, n_sems, build


def _merge(*plans):
    ins, outs, aliases, subs, sems = [], [], {}, [], 0
    for p in plans:
        for k, v in p.aliases.items():
            aliases[len(ins) + k] = len(outs) + v
        subs.append((p, len(ins), len(outs), sems))
        ins += p.ins
        outs += p.out_shapes
        sems += p.n_sems

    def build(in_refs, out_refs, send_sems, recv_sems, base):
        copies = []
        for p, i0, o0, s0 in subs:
            copies += p.build(in_refs[i0:i0 + len(p.ins)], out_refs[o0:o0 + len(p.out_shapes)], send_sems, recv_sems,
                              base + s0)
        return copies

    return _Plan(ins, outs, aliases, sems, build)


def _remote(src, dst, send_sems, recv_sems, k, to):
    return pltpu.make_async_remote_copy(src_ref=src, dst_ref=dst, send_sem=send_sems.at[k], recv_sem=recv_sems.at[k],
                                        device_id=to, device_id_type=MESH)


def _plan_gather_ici(shards):
    n = len(shards)

    def build(ins, outs, send_sems, recv_sems, base):
        x, y, c = _place()
        me = 2 * x + y
        copies = []
        for a in range(n):
            h = ins[a].shape[0] // 2
            rows = pl.ds(c * h, h)
            for r in (1, 2, 3):
                px, py, _ = _chip_rel(x, y, r)
                copies.append(_remote(ins[a].at[rows, :], outs[a].at[me, rows, :], send_sems, recv_sems,
                                      base + 4 * a + r - 1, (px, py, c)))
            copies.append(_remote(ins[a], outs[a].at[me], send_sems, recv_sems, base + 4 * a + 3, (x, y, 1 - c)))
        return copies

    return _Plan(shards, [jax.ShapeDtypeStruct((N_CHIPS,) + s.shape, s.dtype) for s in shards], {}, 4 * n, build)


def _plan_gather_pass(gathered):
    n = len(gathered)

    def build(ins, outs, send_sems, recv_sems, base):
        x, y, c = _place()
        copies = []
        for a in range(n):
            h = outs[a].shape[1] // 2
            rows = pl.ds(c * h, h)
            for r in (1, 2, 3):
                _, _, shard = _chip_rel(x, y, r)
                piece = outs[a].at[shard, rows, :]
                copies.append(_remote(piece, piece, send_sems, recv_sems, base + 3 * a + r - 1, (x, y, 1 - c)))
        return copies

    return _Plan(gathered, [jax.ShapeDtypeStruct(g.shape, g.dtype) for g in gathered], {a: a for a in range(n)},
                 3 * n, build)


def _plan_swap(split):
    n = len(split)

    def build(ins, outs, send_sems, recv_sems, base):
        x, y, c = _place()
        return [_remote(ins[a].at[j, 1 - c], outs[a].at[j], send_sems, recv_sems, base + N_CHIPS * a + j, (x, y, 1 - c))
                for a in range(n) for j in range(N_CHIPS)]

    return _Plan(split, [jax.ShapeDtypeStruct((N_CHIPS,) + g.shape[2:], g.dtype) for g in split], {}, N_CHIPS * n, build)


def _plan_scatter(parts):
    n = len(parts)

    def build(ins, outs, send_sems, recv_sems, base):
        x, y, c = _place()
        copies = []
        for a in range(n):
            for r in (1, 2, 3):
                px, py, shard = _chip_rel(x, y, r)
                copies.append(_remote(ins[a].at[shard], outs[a].at[r - 1], send_sems, recv_sems, base + 3 * a + r - 1,
                                      (px, py, c)))
        return copies

    return _Plan(parts, [jax.ShapeDtypeStruct((3,) + p.shape[1:], p.dtype) for p in parts], {}, 3 * n, build)


def _plan_join(bufs):
    n = len(bufs)

    def build(ins, outs, send_sems, recv_sems, base):
        x, y, c = _place()
        return [_remote(outs[a].at[c], outs[a].at[c], send_sems, recv_sems, base + a, (x, y, 1 - c)) for a in range(n)]

    return _Plan(bufs, [jax.ShapeDtypeStruct(b.shape, b.dtype) for b in bufs], {a: a for a in range(n)}, n, build)


def _carried_call(kern, plan, *, name, grid, in_specs, out_specs, out_shape, operands, scratch_shapes=(),
                  aliases=None, sem=None):
    n_in, n_out, n_sc = len(operands), len(out_shape), len(scratch_shapes)
    aliases = dict(aliases or {})
    if plan is None:
        outs = pl.pallas_call(kern, name=name, grid=grid, in_specs=in_specs, out_specs=out_specs, out_shape=out_shape,
                              scratch_shapes=list(scratch_shapes), input_output_aliases=aliases,
                              compiler_params=_params(sem))(*operands)
        return list(outs), []
    ci, co = len(plan.ins), len(plan.out_shapes)
    for k, v in plan.aliases.items():
        aliases[n_in + k] = n_out + v
    steps = tuple(grid)

    def body(*refs):
        ins, cins = refs[:n_in], refs[n_in:n_in + ci]
        outs = refs[n_in + ci:n_in + ci + n_out]
        couts = refs[n_in + ci + n_out:n_in + ci + n_out + co]
        scratch = refs[n_in + ci + n_out + co:n_in + ci + n_out + co + n_sc]
        send_sems, recv_sems = refs[-2], refs[-1]
        first = functools.reduce(jnp.logical_and, [pl.program_id(a) == 0 for a in range(len(steps))])
        last = functools.reduce(jnp.logical_and, [pl.program_id(a) == steps[a] - 1 for a in range(len(steps))])

        @pl.when(first)
        def _():
            for cp in plan.build(cins, couts, send_sems, recv_sems, 0):
                cp.start()

        kern(*ins, *outs, *scratch)

        @pl.when(last)
        def _():
            for cp in plan.build(cins, couts, send_sems, recv_sems, 0):
                cp.wait()

    anyspec = pl.BlockSpec(memory_space=pl.ANY)
    outs = pl.pallas_call(
        body, name=name, grid=grid, in_specs=list(in_specs) + [anyspec] * ci,
        out_specs=list(out_specs) + [anyspec] * co, out_shape=list(out_shape) + plan.out_shapes,
        scratch_shapes=list(scratch_shapes) + [pltpu.SemaphoreType.DMA((plan.n_sems,)),
                                               pltpu.SemaphoreType.DMA((plan.n_sems,))],
        input_output_aliases=aliases,
        compiler_params=_params(("arbitrary",) * len(steps)),
    )(*operands, *plan.ins)
    return list(outs[:n_out]), list(outs[n_out:])


def _chip_rel(x, y, r):
    px = x if r < 2 else 1 - x
    py = y if r % 2 == 0 else 1 - y
    return px, py, 2 * px + py


def _allgather_whole(arr, name):
    def body(in_ref, out_ref, send_sems, recv_sems, loc_sem):
        x, y, c = _place()
        me = 2 * x + y
        local = pltpu.make_async_copy(in_ref, out_ref.at[me], loc_sem)
        local.start()
        sends = []
        for r in (1, 2, 3):
            px, py, _ = _chip_rel(x, y, r)
            sends.append(pltpu.make_async_remote_copy(
                src_ref=in_ref, dst_ref=out_ref.at[me], send_sem=send_sems.at[r - 1], recv_sem=recv_sems.at[r - 1],
                device_id=(px, py, c), device_id_type=MESH))
        for cp in sends:
            cp.start()
        for r in (1, 2, 3):
            px, py, shard = _chip_rel(x, y, r)
            pltpu.make_async_remote_copy(
                src_ref=in_ref, dst_ref=out_ref.at[shard], send_sem=send_sems.at[r - 1], recv_sem=recv_sems.at[r - 1],
                device_id=(px, py, c), device_id_type=MESH).wait_recv()
        for cp in sends:
            cp.wait_send()
        local.wait()

    anyspec = pl.BlockSpec(memory_space=pl.ANY)
    return pl.pallas_call(
        body, name=name, in_specs=[anyspec], out_specs=anyspec,
        out_shape=jax.ShapeDtypeStruct((N_CHIPS,) + arr.shape, arr.dtype),
        scratch_shapes=[pltpu.SemaphoreType.DMA((3,)), pltpu.SemaphoreType.DMA((3,)), pltpu.SemaphoreType.DMA],
        compiler_params=pltpu.CompilerParams(has_side_effects=True),
    )(arr)


def _plan_gather_all(buf):
    def build(ins, outs, send_sems, recv_sems, base):
        x, y, c = _place()
        me = 4 * x + 2 * y + c
        copies = []
        for r in range(1, N_DEV):
            px, py, _ = _chip_rel(x, y, r // 2)
            pc = c if r % 2 == 0 else 1 - c
            copies.append(_remote(ins[0], outs[0].at[me], send_sems, recv_sems, base + r - 1, (px, py, pc)))
        return copies

    return _Plan([buf, jnp.zeros((N_DEV,) + buf.shape, buf.dtype)],
                 [jax.ShapeDtypeStruct((N_DEV,) + buf.shape, buf.dtype)], {1: 0}, N_DEV - 1, build)


def _add_half(grad, recv, sel, name):
    _, _, rh, cw = grad.shape
    rb = _pick(rh, (512, 256, 176, 128, 64, 32, 16, 8))

    def kern(sel_ref, g_ref, r_ref, o_ref):
        o_ref[...] = (g_ref[...] + r_ref[...]).astype(BF16)

    return pl.pallas_call(
        kern, name=name,
        grid_spec=pltpu.PrefetchScalarGridSpec(
            num_scalar_prefetch=1, grid=(N_CHIPS - 1, rh // rb),
            in_specs=[pl.BlockSpec((None, None, rb, cw), lambda j, i, s: (s[2 + j], s[0], i, 0)),
                      pl.BlockSpec((None, rb, cw), lambda j, i, s: (s[2 + j], i, 0))],
            out_specs=pl.BlockSpec((None, rb, cw), lambda j, i, s: (s[2 + j], i, 0))),
        out_shape=jax.ShapeDtypeStruct((N_CHIPS, rh, cw), BF16),
        compiler_params=_params(("parallel", "parallel")),
    )(sel, grad, recv)


def _add_own(grad, recv, got, sel, name):
    _, _, rh, cw = grad.shape
    rb = _pick(rh, (512, 256, 176, 128, 64, 32, 16, 8))

    def kern(sel_ref, g_ref, r_ref, b_ref, o_ref):
        own = g_ref[...] + r_ref[...]
        o_ref[...] = ((own + b_ref[0].astype(F32)) + b_ref[1].astype(F32)) + b_ref[2].astype(F32)

    return pl.pallas_call(
        kern, name=name,
        grid_spec=pltpu.PrefetchScalarGridSpec(
            num_scalar_prefetch=1, grid=(rh // rb,),
            in_specs=[pl.BlockSpec((None, None, rb, cw), lambda i, s: (s[1], s[0], i, 0)),
                      pl.BlockSpec((None, rb, cw), lambda i, s: (s[1], i, 0)),
                      pl.BlockSpec((3, rb, cw), lambda i, s: (0, i, 0))],
            out_specs=pl.BlockSpec((None, rb, cw), lambda i, s: (s[0], i, 0))),
        out_shape=jax.ShapeDtypeStruct((2, rh, cw), F32),
        compiler_params=_params(("parallel",)),
    )(sel, grad, recv, got)


def _stacked(g):
    return g.reshape(1, g.shape[0] * g.shape[1], g.shape[2])


def _halves(g):
    g = g.reshape(N_CHIPS, g.shape[0] * g.shape[1] // N_CHIPS, g.shape[2])
    return g.reshape(N_CHIPS, 2, g.shape[1] // 2, g.shape[2])


def _whole(f):
    return f.reshape(f.shape[0] * f.shape[1], f.shape[2])


def _step(x3, tgt, sm, w, mom, var):
    x, y, c = _place()
    sel = jnp.stack([c, 2 * x + y] + [_chip_rel(x, y, r)[2] for r in (1, 2, 3)]).astype(jnp.int32)
    wg = {}
    x2 = x3[0]
    cast, landed = _cast_bf16(
        [(x3, 0), (w["hg_w_out"], 0), (w["sg_w_in"], 0), (w["sg_w_out"], 0), (w["ffn_w_up"], 0), (w["ffn_w_up"], 1),
         (w["ffn_w_down"], 0), (w["ffn_w_down"], 1)], "cast_shards",
        comm=_plan_gather_ici([w["hg_w_in"][0].astype(BF16)]))
    xb = cast[0]
    sh = dict(zip(("hg_out", "sg_in", "sg_out", "up0", "up1", "dn0", "dn1"), cast[1:]))
    wg["hg_in"] = _comm_call(_plan_gather_pass(landed), "gather_hg_in_pass")[0]
    proj, landed = _matmul(xb, wg["hg_in"], mode="nn", nsh=N_CHIPS, name="hg_in",
                           comm=_plan_gather_ici([sh["hg_out"], sh["dn0"]]))
    (yhg, o_raw, states), got = _hgrn2_fwd(
        proj, sm["lb_logits"], sm["hg_norm_g"], "hgrn2_fwd",
        comm=_merge(_plan_gather_pass(landed), _plan_gather_ici([sh[k] for k in ("up0", "sg_in", "sg_out")])))
    wg["hg_out"], wg["dn0"], landed = got[0], got[1], got[2:]
    mixed, got = _matmul(yhg, _stacked(wg["hg_out"]), mode="nn", nsh=1, name="hg_out", comm=_plan_gather_pass(landed))
    wg["up0"], wg["sg_in"], wg["sg_out"] = got
    xin1, h1, h1b = _res_ln_fwd(x2, mixed, sm["ln1_g"][0:1], sm["ln1_b"][0:1], "l0_ln1")
    u0, landed = _matmul(h1b, wg["up0"], mode="nn", nsh=N_CHIPS, name="l0_ffn_up", comm=_plan_gather_ici([sh["up1"]]))
    gact0 = _conv_gate_fwd(u0, sm["conv_w"][0], sm["conv_b"][0:1], "l0_ffn_gate")
    ffn, got = _matmul(gact0, _stacked(wg["dn0"]), mode="nn", nsh=1, name="l0_ffn_down",
                       comm=_merge(_plan_gather_pass(landed), _plan_gather_ici([sh["dn1"]])))
    wg["up1"], landed = got[0], got[1:]
    xin2, h2, h2b = _res_ln_fwd(h1, ffn, sm["ln2_g"][0:1], sm["ln2_b"][0:1], "l0_ffn_ln")
    pre, got = _matmul(h2b, wg["sg_in"], mode="nn", nsh=N_CHIPS, name="sg_in", comm=_plan_gather_pass(landed))
    wg["dn1"] = got[0]
    ysg = _sg_gate_fwd(pre, sm["sg_ln_g"], sm["sg_ln_b"], sm["sg_w_s"], sm["sg_b_s_t"], "sg_gate")
    mixed = _matmul(ysg, _stacked(wg["sg_out"]), mode="nn", nsh=1, name="sg_out")
    xin3, h3, h3b = _res_ln_fwd(h2, mixed, sm["ln1_g"][1:2], sm["ln1_b"][1:2], "l1_ln1")
    u1 = _matmul(h3b, wg["up1"], mode="nn", nsh=N_CHIPS, name="l1_ffn_up")
    gact1 = _conv_gate_fwd(u1, sm["conv_w"][1], sm["conv_b"][1:2], "l1_ffn_gate")
    ffn = _matmul(gact1, _stacked(wg["dn1"]), mode="nn", nsh=1, name="l1_ffn_down")
    xin4, _, _ = _res_ln_fwd(h3, ffn, sm["ln2_g"][1:2], sm["ln2_b"][1:2], "l1_ffn_ln")

    gs, grad, split, recv, part = {}, {}, {}, {}, {}

    def swap_on(call, keys):
        for k in keys:
            split[k] = _halves(grad[k])
        out, got = call(_plan_swap([split[k] for k in keys]))
        for k, r in zip(keys, got):
            recv[k] = r
            part[k] = _add_half(split[k], r, sel, f"rs_addhalf_{k}")
        return out

    def ffn_bwd(u, gact, hb_in, dxin, dxin_b, w_up, w_down, layer, tag, up, down, waiting):
        dgact = _matmul(dxin_b, _stacked(w_down), mode="nt", nsh=1, name=f"{tag}_ddown")
        grad[down] = _matmul(gact, dxin_b, mode="tn", nsh=1, name=f"{tag}_wdown")
        da, db, dcw, dcb = _conv_gate_bwd(u, dgact, sm["conv_w"][layer], sm["conv_b"][layer:layer + 1], f"{tag}_dgate")
        grad[up] = swap_on(lambda plan: _matmul(hb_in, [da, db], mode="tn", nsh=N_CHIPS, name=f"{tag}_wup", comm=plan),
                           waiting + [down])
        dh = swap_on(lambda plan: _matmul([da, db], w_up, mode="nt", nsh=N_CHIPS, resid=dxin, alpha=ALPHA,
                                          name=f"{tag}_dup", comm=plan), [up])
        return dh, dcw, dcb

    dx, dxb, dg4, db4, loss = _ln_bwd(xin4, tgt, sm["ln2_g"][1:2], sm["ln2_b"][1:2], "l1_ln2_bwd", loss_head=True)
    dh3, dcw1, dcb1 = ffn_bwd(u1, gact1, h3b, dx, dxb, wg["up1"], wg["dn1"], 1, "l1_ffn", "up1", "dn1", [])
    dx, dxb, dg3, db3 = _ln_bwd(xin3, dh3, sm["ln1_g"][1:2], sm["ln1_b"][1:2], "l1_ln1_bwd")
    grad["sg_out"] = _matmul(ysg, dxb, mode="tn", nsh=1, name="sg_wout")
    dysg = swap_on(lambda plan: _matmul(dxb, _stacked(wg["sg_out"]), mode="nt", nsh=1, name="sg_dout", comm=plan),
                   ["sg_out"])
    dpre, gs["sg_w_s"], gs["sg_b_s_t"], gs["sg_ln_g"], gs["sg_ln_b"] = _sg_gate_bwd(
        pre, dysg, sm["sg_ln_g"], sm["sg_ln_b"], sm["sg_w_s"], sm["sg_b_s_t"], "sg_gate_bwd")
    grad["sg_in"] = _matmul(h2b, dpre, mode="tn", nsh=N_CHIPS, name="sg_win")
    dh2 = _matmul(dpre, wg["sg_in"], mode="nt", nsh=N_CHIPS, resid=dx, alpha=ALPHA, name="sg_din")
    dx, dxb, dg2, db2 = _ln_bwd(xin2, dh2, sm["ln2_g"][0:1], sm["ln2_b"][0:1], "l0_ln2_bwd")
    dh1, dcw0, dcb0 = ffn_bwd(u0, gact0, h1b, dx, dxb, wg["up0"], wg["dn0"], 0, "l0_ffn", "up0", "dn0", ["sg_in"])
    dx, dxb, dg1, db1 = _ln_bwd(xin1, dh1, sm["ln1_g"][0:1], sm["ln1_b"][0:1], "l0_ln1_bwd")
    grad["hg_out"] = _matmul(yhg, dxb, mode="tn", nsh=1, name="hg_wout")
    dyhg = swap_on(lambda plan: _matmul(dxb, _stacked(wg["hg_out"]), mode="nt", nsh=1, name="hg_dout", comm=plan),
                   ["hg_out"])
    early = ("dn1", "up1", "sg_out", "sg_in", "dn0", "up0", "hg_out")
    dparts, got = _hgrn2_bwd(proj, sm["lb_logits"], sm["hg_norm_g"], o_raw, states, dyhg, "hgrn2_bwd",
                             comm=_plan_scatter([part[k] for k in early]))
    gs["lb"], gs["hg_norm_g"] = dparts[4], dparts[5]
    gs["ln1_g"] = jnp.concatenate([dg1, dg3], axis=0)
    gs["ln1_b"] = jnp.concatenate([db1, db3], axis=0)
    gs["ln2_g"] = jnp.concatenate([dg2, dg4], axis=0)
    gs["ln2_b"] = jnp.concatenate([db2, db4], axis=0)
    gs["conv_w"] = jnp.stack([dcw0, dcw1], axis=0)
    gs["conv_b"] = jnp.concatenate([dcb0, dcb1], axis=0)
    packed, layout = _pack(gs)
    mine = [_add_own(split[k], recv[k], b, sel, f"rs_addown_{k}") for k, b in zip(early, got)]
    grad["hg_in"], got = _matmul(xb, list(dparts[:4]), mode="tn", nsh=N_CHIPS, name="hg_win",
                                 comm=_merge(_plan_join(mine), _plan_gather_all(packed)))
    red = {k: _whole(f) for k, f in zip(early, got)}
    me8 = jnp.reshape(4 * x + 2 * y + c, (1,)).astype(jnp.int32)
    summed = _unpack(_sum_devices(got[len(early)], packed, me8, "sum_small_grads"), layout)
    gx = swap_on(lambda plan: _matmul(dparts[0], wg["hg_in"], mode="nt", nsh=1, b_off=0, resid=dx, alpha=ALPHA,
                                      name="hg_din_q", comm=plan), ["hg_in"])
    gx, got = _matmul(list(dparts[1:4]), wg["hg_in"], mode="nt", nsh=3, b_off=1, resid=gx, alpha=1.0, name="hg_din_fig",
                      comm=_plan_scatter([part["hg_in"]]))
    mine = _add_own(split["hg_in"], recv["hg_in"], got[0], sel, "rs_addown_hg_in")
    upd = {}
    upd["hg_w_out"], full = _adamw(w["hg_w_out"], [red["hg_out"]], mom["hg_w_out"], var["hg_w_out"], "adamw_hg_w_out",
                                   comm=_plan_join([mine]))
    red["hg_in"] = _whole(full[0])
    for k, src in (("ffn_w_up", ("up0", "up1")), ("ffn_w_down", ("dn0", "dn1")), ("sg_w_in", ("sg_in",)),
                   ("sg_w_out", ("sg_out",)), ("hg_w_in", ("hg_in",))):
        upd[k] = _adamw(w[k], [red[s] for s in src], mom[k], var[k], f"adamw_{k}")
    return loss, gx, summed, upd


_SMALL_ORDER = ("lb", "hg_norm_g", "sg_w_s", "sg_b_s_t", "conv_b", "ln1_g", "ln1_b", "ln2_g", "ln2_b",
                "conv_w", "sg_ln_g", "sg_ln_b")


PACK_ROWS = 512


def _pack(parts):
    flat, layout, off = [], [], 0
    for k in _SMALL_ORDER:
        a = parts[k]
        n = a.size
        pad = (-n) % LANES
        flat.append(jnp.pad(a.reshape(-1), (0, pad)))
        layout.append((k, off, n, a.shape))
        off += n + pad
    flat.append(jnp.zeros(((-off) % (PACK_ROWS * LANES),), F32))
    return jnp.concatenate(flat).reshape(-1, LANES), layout


def _unpack(buf, layout):
    flat = buf.reshape(-1)
    return {k: flat[off:off + n].reshape(shape) for k, off, n, shape in layout}


def kernel(x, lb_logits, hg_w_in, hg_norm_g, hg_w_out, sg_w_in, sg_ln_g, sg_ln_b, sg_w_s, sg_b_s, sg_w_out, ffn_w_up, ffn_conv_w, ffn_conv_b, ffn_w_down, ln1_g, ln1_b, ln2_g, ln2_b, loss_target, m_lb_logits, m_hg_w_in, m_hg_norm_g, m_hg_w_out, m_sg_w_in, m_sg_ln_g, m_sg_ln_b, m_sg_w_s, m_sg_b_s, m_sg_w_out, m_ffn_w_up, m_ffn_conv_w, m_ffn_conv_b, m_ffn_w_down, m_ln1_g, m_ln1_b, m_ln2_g, m_ln2_b, v_lb_logits, v_hg_w_in, v_hg_norm_g, v_hg_w_out, v_sg_w_in, v_sg_ln_g, v_sg_ln_b, v_sg_w_s, v_sg_b_s, v_sg_w_out, v_ffn_w_up, v_ffn_conv_w, v_ffn_conv_b, v_ffn_w_down, v_ln1_g, v_ln1_b, v_ln2_g, v_ln2_b):
    names = ("lb_logits", "hg_w_in", "hg_norm_g", "hg_w_out", "sg_w_in", "sg_ln_g", "sg_ln_b", "sg_w_s", "sg_b_s",
             "sg_w_out", "ffn_w_up", "ffn_conv_w", "ffn_conv_b", "ffn_w_down", "ln1_g", "ln1_b", "ln2_g", "ln2_b")
    w = dict(zip(names, (lb_logits, hg_w_in, hg_norm_g, hg_w_out, sg_w_in, sg_ln_g, sg_ln_b, sg_w_s, sg_b_s,
                         sg_w_out, ffn_w_up, ffn_conv_w, ffn_conv_b, ffn_w_down, ln1_g, ln1_b, ln2_g, ln2_b)))
    mom = dict(zip(names, (m_lb_logits, m_hg_w_in, m_hg_norm_g, m_hg_w_out, m_sg_w_in, m_sg_ln_g, m_sg_ln_b, m_sg_w_s,
                           m_sg_b_s, m_sg_w_out, m_ffn_w_up, m_ffn_conv_w, m_ffn_conv_b, m_ffn_w_down, m_ln1_g,
                           m_ln1_b, m_ln2_g, m_ln2_b)))
    var = dict(zip(names, (v_lb_logits, v_hg_w_in, v_hg_norm_g, v_hg_w_out, v_sg_w_in, v_sg_ln_g, v_sg_ln_b, v_sg_w_s,
                           v_sg_b_s, v_sg_w_out, v_ffn_w_up, v_ffn_conv_w, v_ffn_conv_b, v_ffn_w_down, v_ln1_g,
                           v_ln1_b, v_ln2_g, v_ln2_b)))
    tgt = loss_target[0]
    fq = ffn_conv_w.shape[2]
    dq = sg_ln_g.shape[1]
    cx, cy, _ = _place()
    me = 2 * cx + cy

    wide = max(fq, dq)
    tiny = jnp.concatenate([jnp.pad(ffn_conv_w.reshape(6, fq), ((0, 0), (0, wide - fq))),
                            jnp.pad(sg_ln_g, ((0, 0), (0, wide - dq))),
                            jnp.pad(sg_ln_b, ((0, 0), (0, wide - dq)))], axis=0)
    tiny_all = _allgather_whole(tiny, "gather_small")
    conv_w_full = jnp.transpose(tiny_all[:, 0:6, :fq].reshape(N_CHIPS, 2, 3, fq), (1, 2, 0, 3)).reshape(2, 3, N_CHIPS * fq)
    sm = {"lb_logits": lb_logits, "hg_norm_g": hg_norm_g, "ln1_g": ln1_g, "ln1_b": ln1_b, "ln2_g": ln2_g,
          "ln2_b": ln2_b, "conv_w": conv_w_full, "conv_b": ffn_conv_b,
          "sg_ln_g": tiny_all[:, 6, :dq].reshape(1, N_CHIPS * dq),
          "sg_ln_b": tiny_all[:, 7, :dq].reshape(1, N_CHIPS * dq),
          "sg_w_s": sg_w_s[0], "sg_b_s_t": jnp.transpose(sg_b_s[0])}

    loss_row, grad_x, summed, upd = _step(x, tgt, sm, w, mom, var)
    loss = lax.psum(loss_row[0, 0], ("x", "y", "c"))

    grads = {
        "lb_logits": _lb_logits_grad(lb_logits, summed["lb"], "lb_logits_grad"),
        "hg_norm_g": summed["hg_norm_g"],
        "sg_ln_g": lax.dynamic_slice_in_dim(summed["sg_ln_g"], me * dq, dq, axis=1),
        "sg_ln_b": lax.dynamic_slice_in_dim(summed["sg_ln_b"], me * dq, dq, axis=1),
        "sg_w_s": summed["sg_w_s"][None], "sg_b_s": jnp.transpose(summed["sg_b_s_t"])[None],
        "ffn_conv_w": lax.dynamic_slice_in_dim(summed["conv_w"], me * fq, fq, axis=2),
        "ffn_conv_b": summed["conv_b"],
        "ln1_g": summed["ln1_g"], "ln1_b": summed["ln1_b"], "ln2_g": summed["ln2_g"], "ln2_b": summed["ln2_b"],
    }

    delta, new_m, new_v = {}, {}, {}
    for k, (dlt, mm, vv, gg) in upd.items():
        delta[k], new_m[k], new_v[k], grads[k] = dlt, mm, vv, gg
    small_names = [k for k in names if k not in upd]

    def pack_small(src):
        flat = [src[k].reshape(-1) for k in small_names]
        n = sum(a.size for a in flat)
        flat.append(jnp.zeros(((-n) % (PACK_ROWS * LANES),), F32))
        return jnp.concatenate(flat).reshape(1, -1, LANES)

    outs = _adamw(pack_small(w), [pack_small(grads)[0]], pack_small(mom), pack_small(var), "adamw_small")
    off = 0
    for k in small_names:
        n = w[k].size
        for dst, o in zip((delta, new_m, new_v), outs):
            dst[k] = o.reshape(-1)[off:off + n].reshape(w[k].shape)
        off += n

    return (loss, grad_x[None], *[grads[k] for k in names], *[delta[k] for k in names],
            *[new_m[k] for k in names], *[new_v[k] for k in names])
```

```python
import functools

import jax
import jax.numpy as jnp
from jax import lax
from jax.experimental import pallas as pl
from jax.experimental.pallas import tpu as pltpu

F32 = jnp.float32
BF16 = jnp.bfloat16
HI = lax.Precision.HIGHEST
MESH = pl.DeviceIdType.MESH

ALPHA = (2 * 2) ** 0.25
LN_EPS = 1e-5
RMS_EPS = 1e-6
ADAM_LR, ADAM_B1, ADAM_B2, ADAM_EPS, ADAM_WD, ADAM_STEP = 0.001, 0.9, 0.999, 1e-08, 0.01, 10

LANES = 128
SUB = 16
TILE = 8
GCHUNK = 128
VMEM_LIMIT = 56 * 1024 * 1024
N_CHIPS = 4
N_DEV = 8

NT = (((1,), (1,)), ((), ()))
TN = (((0,), (0,)), ((), ()))
NN = (((1,), (0,)), ((), ()))


def _pick(dim, prefs):
    for p in prefs:
        if dim % p == 0:
            return p
    return dim


def _params(sem=None, **kw):
    return pltpu.CompilerParams(dimension_semantics=sem, vmem_limit_bytes=VMEM_LIMIT, **kw)


def _sigmoid_pair(x):
    e = jnp.exp(-jnp.abs(x))
    inv = 1.0 / (1.0 + e)
    pos = x >= 0
    return jnp.where(pos, inv, e * inv), jnp.where(pos, e * inv, inv)


def _sigmoid_gate(x):
    t = 0.5 * jnp.tanh(0.5 * x)
    return 0.5 + t, 0.5 - t


def _ln_hat(x):
    mu = jnp.mean(x, axis=-1, keepdims=True)
    xc = x - mu
    var = jnp.mean(xc * xc, axis=-1, keepdims=True)
    rstd = lax.rsqrt(var + LN_EPS)
    return xc * rstd, rstd


def _lower_bound(logits):
    m = jnp.max(logits, axis=0, keepdims=True)
    e = jnp.exp(logits - m)
    return e[0:1, :] / jnp.sum(e, axis=0, keepdims=True)


MATMUL_VMEM_BUDGET = 40 * 1024 * 1024


def _fit_bk(kdim, bm, bn, out_dtype, has_resid, na=1, nb=1):
    fixed = bm * bn * (4 + 2 * jnp.dtype(out_dtype).itemsize + (8 if has_resid else 0))
    best = LANES
    for bk in range(LANES, kdim + 1, LANES):
        if kdim % bk == 0 and fixed + 4 * bk * (bm * na + bn * nb) <= MATMUL_VMEM_BUDGET:
            best = bk
    return best


def _fit_bm_bk(mdim, prefs, kdim, bn, out_dtype, has_resid, na=1, nb=1):
    best = None
    fits = [bm for bm in prefs if mdim % bm == 0][:2] or [mdim]
    for bm in fits:
        bk = _fit_bk(kdim, bm, bn, out_dtype, has_resid, na, nb)
        if best is None or kdim // bk < kdim // best[1]:
            best = (bm, bk)
    return best


def _matmul(a, b, *, mode, name, out_dtype=F32, resid=None, alpha=1.0, b_off=0, nsh=None, comm=None):
    a_parts = a if isinstance(a, (list, tuple)) else [a]
    b_parts = b if isinstance(b, (list, tuple)) else [b]
    n_parts = max(len(a_parts), len(b_parts))
    if mode == "nn":
        m, kdim = a.shape
        _, _, ns = b.shape
        bn = _pick(ns, (1024, 1408, 512, 256, 128))
        bm, bk = _fit_bm_bk(m, (1024, 512, 256, 128), kdim, bn, out_dtype, resid is not None)
        nps = ns // bn
        grid = (m // bm, nsh * nps, kdim // bk)
        a_specs = [pl.BlockSpec((bm, bk), lambda i, j, k: (i, k))]
        b_specs = [pl.BlockSpec((None, bk, bn), lambda i, j, k: (b_off + j // nps, k, j % nps))]
        o_spec = pl.BlockSpec((bm, bn), lambda i, j, k: (i, j))
        out_shape = jax.ShapeDtypeStruct((m, nsh * ns), out_dtype)
        dims, part_axis, per_part = NN, 2, grid[2]
    elif mode == "nt":
        m = a_parts[0].shape[0]
        _, kdim, ns = b.shape
        wide = n_parts > 1
        bn = _pick(kdim, (1024, 1408, 512, 256, 128) if wide else (512, 256, 128))
        bm, bk = _fit_bm_bk(m, (1024, 512, 256, 128) if wide else (2048, 1024, 512, 256, 128), ns, bn, out_dtype,
                            resid is not None, na=n_parts)
        kps = ns // bk
        per_part = nsh // n_parts * kps
        grid = (m // bm, kdim // bn, nsh * kps)
        a_specs = [pl.BlockSpec((bm, bk), lambda i, j, k, p=p: (jnp.where(k // per_part == p, i, 0),
                                                                  jnp.where(k // per_part == p, k % per_part, 0)))
                   for p in range(n_parts)]
        b_specs = [pl.BlockSpec((None, bn, bk), lambda i, j, k: (b_off + k // kps, j, k % kps))]
        o_spec = pl.BlockSpec((bm, bn), lambda i, j, k: (i, j))
        out_shape = jax.ShapeDtypeStruct((m, kdim), out_dtype)
        dims, part_axis = NT, 2
    else:
        t, kdim = a.shape
        ns = b_parts[0].shape[1] * n_parts // nsh
        bn = _pick(ns, (1024, 1408, 512, 256, 128))
        bm, bk = _fit_bm_bk(kdim, (1024, 1408, 512, 256, 128), t, bn, out_dtype, resid is not None, nb=n_parts)
        nps = ns // bn
        per_part = nsh // n_parts * nps
        grid = (kdim // bm, nsh * nps, t // bk)
        a_specs = [pl.BlockSpec((bk, bm), lambda i, j, k: (k, i))]
        b_specs = [pl.BlockSpec((bk, bn), lambda i, j, k, p=p: (jnp.where(j // per_part == p, k, 0),
                                                                  jnp.where(j // per_part == p, j % per_part, 0)))
                   for p in range(n_parts)]
        o_spec = pl.BlockSpec((None, bm, bn), lambda i, j, k: (j // nps, i, j % nps))
        out_shape = jax.ShapeDtypeStruct((nsh, kdim, ns), out_dtype)
        dims, part_axis = TN, 1
    nk = grid[2]
    na, nb_ = len(a_parts), len(b_parts)
    has_resid = resid is not None

    def kern(*refs):
        a_refs, b_refs = refs[:na], refs[na:na + nb_]
        r_ref = refs[na + nb_] if has_resid else None
        k = pl.program_id(2)

        def finish(r, o_ref):
            if has_resid:
                r = r + alpha * r_ref[...]
            o_ref[...] = r.astype(o_ref.dtype)

        def add(a_ref, b_ref):
            if nk == 1:
                finish(lax.dot_general(a_ref[...], b_ref[...], dims, preferred_element_type=F32), refs[-1])
                return
            refs[-1][...] += lax.dot_general(a_ref[...], b_ref[...], dims, preferred_element_type=F32)

        if nk > 1:
            @pl.when(k == 0)
            def _():
                refs[-1][...] = jnp.zeros_like(refs[-1])

        if n_parts == 1:
            add(a_refs[0], b_refs[0])
        else:
            which = pl.program_id(part_axis) // per_part
            for p in range(n_parts):
                pl.when(which == p)(functools.partial(add, a_refs[min(p, na - 1)], b_refs[min(p, nb_ - 1)]))
        if nk > 1:
            @pl.when(k == nk - 1)
            def _():
                finish(refs[-1][...], refs[-2])

    in_specs = a_specs + b_specs
    operands = list(a_parts) + list(b_parts)
    if has_resid:
        in_specs.append(pl.BlockSpec((bm, bn), lambda i, j, k: (i, j)))
        operands.append(resid)
    outs, carried = _carried_call(
        kern, comm, name=name, grid=grid, in_specs=in_specs, out_specs=[o_spec], out_shape=[out_shape],
        operands=operands, scratch_shapes=[pltpu.VMEM((bm, bn), F32)] if nk > 1 else [],
        sem=("parallel", "parallel", "arbitrary"))
    return outs[0] if comm is None else (outs[0], carried)


def _ln_fwd(xin, g, b, name):
    t, d = xin.shape
    tb = _pick(t, (256, 128, 64, 32, 16))

    def kern(x_ref, g_ref, b_ref, h_ref, hb_ref):
        xhat, _ = _ln_hat(x_ref[...])
        h = xhat * g_ref[...] + b_ref[...]
        h_ref[...] = h
        hb_ref[...] = h.astype(BF16)

    row = pl.BlockSpec((tb, d), lambda i: (i, 0))
    vec = pl.BlockSpec((1, d), lambda i: (0, 0))
    return pl.pallas_call(
        kern, name=name, grid=(t // tb,), in_specs=[row, vec, vec], out_specs=[row, row],
        out_shape=[jax.ShapeDtypeStruct((t, d), F32), jax.ShapeDtypeStruct((t, d), BF16)],
        compiler_params=_params(("parallel",)),
    )(xin, g, b)


def _ln_bwd(xin, dy_or_target, g, b, name, loss_head=False):
    t, d = xin.shape
    tb = _pick(t, (256, 128, 64, 32, 16))
    nb = t // tb

    def kern(x_ref, dy_ref, g_ref, b_ref, dx_ref, dxb_ref, dg_ref, db_ref, *rest):
        i = pl.program_id(0)
        xhat, rstd = _ln_hat(x_ref[...])
        gv = g_ref[...]
        if loss_head:
            loss_ref = rest[0]
            err = xhat * gv + b_ref[...] - dy_ref[...]
            dy = err * (1.0 / d)
            part = 0.5 * jnp.sum(jnp.sum(err * err, axis=1, keepdims=True), axis=0, keepdims=True) * (1.0 / d)
        else:
            dy = dy_ref[...]

        @pl.when(i == 0)
        def _():
            dg_ref[...] = jnp.zeros_like(dg_ref)
            db_ref[...] = jnp.zeros_like(db_ref)
            if loss_head:
                loss_ref[...] = jnp.zeros_like(loss_ref)

        dg_ref[...] += jnp.sum(dy * xhat, axis=0, keepdims=True)
        db_ref[...] += jnp.sum(dy, axis=0, keepdims=True)
        if loss_head:
            loss_ref[...] += jnp.broadcast_to(part, loss_ref.shape)
        dxh = dy * gv
        m1 = jnp.mean(dxh, axis=-1, keepdims=True)
        m2 = jnp.mean(dxh * xhat, axis=-1, keepdims=True)
        dx = rstd * (dxh - m1 - xhat * m2)
        dx_ref[...] = dx
        dxb_ref[...] = dx.astype(BF16)

    row = pl.BlockSpec((tb, d), lambda i: (i, 0))
    vec = pl.BlockSpec((1, d), lambda i: (0, 0))
    out_specs = [row, row, vec, vec]
    out_shape = [jax.ShapeDtypeStruct((t, d), F32), jax.ShapeDtypeStruct((t, d), BF16),
                 jax.ShapeDtypeStruct((1, d), F32), jax.ShapeDtypeStruct((1, d), F32)]
    if loss_head:
        out_specs.append(pl.BlockSpec((1, LANES), lambda i: (0, 0)))
        out_shape.append(jax.ShapeDtypeStruct((1, LANES), F32))
    return pl.pallas_call(
        kern, name=name, grid=(nb,), in_specs=[row, row, vec, vec], out_specs=out_specs, out_shape=out_shape,
        compiler_params=_params(("arbitrary",)),
    )(xin, dy_or_target, g, b)


def _conv_gate_fwd(u, conv_w, conv_b, name):
    t, f2 = u.shape
    f = f2 // 2
    tb = _pick(t, (512, 256, 128, 64, 32, 16))
    cn = _pick(f, (1408, 1024, 512, 256, 128))
    ncb = f // cn
    hb = tb // 8

    def kern(a_ref, ah_ref, b_ref, w_ref, cb_ref, o_ref):
        i = pl.program_id(0)
        a = a_ref[...]
        halo = jnp.where(i > 0, ah_ref[...], 0.0)
        rid = lax.broadcasted_iota(jnp.int32, a.shape, 0)
        s1 = jnp.where(rid == 0, halo[7:8, :], pltpu.roll(a, 1, 0))
        s2 = jnp.where(rid == 0, halo[6:7, :], jnp.where(rid == 1, halo[7:8, :], pltpu.roll(a, 2, 0)))
        w = w_ref[...]
        conv = w[2:3, :] * a + w[1:2, :] * s1 + w[0:1, :] * s2 + cb_ref[...]
        sp, _ = _sigmoid_gate(conv)
        o_ref[...] = (conv * sp * b_ref[...]).astype(BF16)

    return pl.pallas_call(
        kern, name=name, grid=(t // tb, ncb),
        in_specs=[pl.BlockSpec((tb, cn), lambda i, j: (i, j)),
                  pl.BlockSpec((8, cn), lambda i, j: (jnp.maximum(i * hb - 1, 0), j)),
                  pl.BlockSpec((tb, cn), lambda i, j: (i, j + ncb)),
                  pl.BlockSpec((3, cn), lambda i, j: (0, j)),
                  pl.BlockSpec((1, cn), lambda i, j: (0, j))],
        out_specs=pl.BlockSpec((tb, cn), lambda i, j: (i, j)),
        out_shape=jax.ShapeDtypeStruct((t, f), BF16),
        compiler_params=_params(("parallel", "parallel")),
    )(u, u, u, conv_w, conv_b)


def _conv_gate_bwd(u, dgact, conv_w, conv_b, name):
    t, f2 = u.shape
    f = f2 // 2
    tb = _pick(t, (512, 256, 128, 64, 32, 16))
    cn = _pick(f, (1408, 1024, 512, 256, 128))
    ncb = f // cn
    hb = tb // 8
    nb = t // tb
    last8 = t // 8 - 1

    def kern(a_ref, ap_ref, an_ref, b_ref, bn_ref, dg_ref, dgn_ref, w_ref, cb_ref,
             da_ref, db_ref, dw_ref, dcb_ref):
        i = pl.program_id(1)
        a = a_ref[...]
        w = w_ref[...]
        ext = jnp.concatenate([jnp.where(i > 0, ap_ref[...], 0.0), a, an_ref[...]], axis=0)
        e1 = pltpu.roll(ext, 1, 0)
        e2 = pltpu.roll(ext, 2, 0)
        conv = (w[2:3, :] * ext + w[1:2, :] * e1 + w[0:1, :] * e2 + cb_ref[...])[8:, :]
        bmn = jnp.concatenate([b_ref[...], bn_ref[...]], axis=0)
        dgmn = jnp.concatenate([dg_ref[...], jnp.where(i < nb - 1, dgn_ref[...], 0.0)], axis=0)
        sp, sn = _sigmoid_gate(conv)
        da = dgmn * bmn * (sp * (1.0 + conv * sn))
        n = tb + 8
        dap = w[2:3, :] * da + w[1:2, :] * pltpu.roll(da, n - 1, 0) + w[0:1, :] * pltpu.roll(da, n - 2, 0)
        da_ref[...] = dap[:tb, :].astype(BF16)
        db_ref[...] = (dg_ref[...] * (conv * sp)[:tb, :]).astype(BF16)
        dam = da[:tb, :]

        @pl.when(i == 0)
        def _():
            dw_ref[...] = jnp.zeros_like(dw_ref)
            dcb_ref[...] = jnp.zeros_like(dcb_ref)

        dw = jnp.concatenate([jnp.sum(dam * e2[8:8 + tb, :], axis=0, keepdims=True),
                              jnp.sum(dam * e1[8:8 + tb, :], axis=0, keepdims=True),
                              jnp.sum(dam * a, axis=0, keepdims=True)], axis=0)
        dw_ref[...] += dw
        dcb_ref[...] += jnp.sum(dam, axis=0, keepdims=True)

    main_a = pl.BlockSpec((tb, cn), lambda j, i: (i, j))
    prev_a = pl.BlockSpec((8, cn), lambda j, i: (jnp.maximum(i * hb - 1, 0), j))
    next_a = pl.BlockSpec((8, cn), lambda j, i: (jnp.minimum((i + 1) * hb, last8), j))
    main_b = pl.BlockSpec((tb, cn), lambda j, i: (i, j + ncb))
    next_b = pl.BlockSpec((8, cn), lambda j, i: (jnp.minimum((i + 1) * hb, last8), j + ncb))
    return pl.pallas_call(
        kern, name=name, grid=(ncb, nb),
        in_specs=[main_a, prev_a, next_a, main_b, next_b, main_a, next_a,
                  pl.BlockSpec((3, cn), lambda j, i: (0, j)), pl.BlockSpec((1, cn), lambda j, i: (0, j))],
        out_specs=[main_a, main_a, pl.BlockSpec((3, cn), lambda j, i: (0, j)),
                   pl.BlockSpec((1, cn), lambda j, i: (0, j))],
        out_shape=[jax.ShapeDtypeStruct((t, f), BF16), jax.ShapeDtypeStruct((t, f), BF16),
                   jax.ShapeDtypeStruct((3, f), F32), jax.ShapeDtypeStruct((1, f), F32)],
        compiler_params=_params(("parallel", "arbitrary")),
    )(u, u, u, u, u, dgact, dgact, conv_w, conv_b)


def _hg_gates(qp, fp, lb):
    sq, _ = _sigmoid_gate(qp)
    sf, snf = _sigmoid_pair(fp)
    forget = lb + (1.0 - lb) * sf
    return sq, sf, snf, forget, jnp.log(forget), (1.0 - lb) * snf


def _tri(lower):
    r = lax.broadcasted_iota(jnp.int32, (SUB, SUB), 0)
    c = lax.broadcasted_iota(jnp.int32, (SUB, SUB), 1)
    return ((r >= c) if lower else (r <= c)).astype(BF16)


def _split2(x):
    hi = x.astype(BF16)
    return hi, (x - hi.astype(F32)).astype(BF16)


def _dot3(a, b, dims):
    (ah, al), (bh, bl) = a, b
    return (lax.dot_general(ah, bh, dims, preferred_element_type=F32)
            + (lax.dot_general(ah, bl, dims, preferred_element_type=F32)
               + lax.dot_general(al, bh, dims, preferred_element_type=F32)))


def _running_sum(tri, x):
    hi, lo = _split2(x)
    rest = (x - hi.astype(F32)) - lo.astype(F32)
    return (lax.dot_general(tri, hi, NN, preferred_element_type=F32)
            + (lax.dot_general(tri, lo, NN, preferred_element_type=F32)
               + lax.dot_general(tri, rest.astype(BF16), NN, preferred_element_type=F32)))


HEADS_PER_STEP = 8
STEP_UNROLL = 2


def _hgrn2_fwd(proj, lb_logits, norm_g, name, comm=None):
    t, d4 = proj.shape
    d = d4 // 4
    nh = d // LANES
    hb = _pick(nh, (HEADS_PER_STEP, 2, 1))
    wb = hb * LANES
    tb = _pick(t, (256, 128, 64, 32, 16))
    nb = t // tb
    nsc = tb // SUB

    def kern(q_ref, f_ref, i_ref, g_ref, lbl_ref, ng_ref, y_ref, o_ref, st_ref, s_ref):
        @pl.when(pl.program_id(1) == 0)
        def _():
            s_ref[...] = jnp.zeros_like(s_ref)

        lb_all = _lower_bound(lbl_ref[...])
        ng_all = ng_ref[...]
        ltri = _tri(True)
        rcol = lax.broadcasted_iota(jnp.int32, (SUB, 1), 0)

        heads = [slice(h * LANES, (h + 1) * LANES) for h in range(hb)]

        def step(sc, carry):
            rows = pl.ds(pl.multiple_of(sc * SUB, SUB), SUB)
            qp, fp, v, gp = q_ref[rows, :], f_ref[rows, :], i_ref[rows, :], g_ref[rows, :]
            sq, _, _, _, lf, k = _hg_gates(qp, fp, lb_all)
            q = qp * sq
            bl = _running_sum(ltri, lf)
            bend = bl[SUB - 1:SUB, :]
            dec = jnp.exp(bend)
            qs2 = _split2(q * jnp.exp(bl))
            kd2 = _split2(k * jnp.exp(bend - bl))
            v2 = _split2(v)
            states = [s_ref[h] for h in range(hb)]
            o = [_dot3((qs2[0][:, c], qs2[1][:, c]), _split2(states[h]), NT) for h, c in enumerate(heads)]
            top, bot = [oh[:TILE] for oh in o], [oh[TILE:] for oh in o]
            for s in range(SUB):
                lo = 0 if s < TILE else TILE
                e = jnp.exp(jnp.minimum(bl[lo:] - bl[s:s + 1, :], 0.0))
                p = q[lo:] * e * k[s:s + 1, :]
                for h, c in enumerate(heads):
                    a = jnp.sum(p[:, c], axis=1, keepdims=True)
                    add = jnp.where(rcol[lo:] >= s, a, 0.0) * v[s:s + 1, c]
                    if lo == 0:
                        top[h], bot[h] = top[h] + add[:TILE], bot[h] + add[TILE:]
                    else:
                        bot[h] = bot[h] + add
            o = [jnp.concatenate([a, b], axis=0) for a, b in zip(top, bot)]
            for h, c in enumerate(heads):
                st_ref[sc, h] = states[h]
                s_ref[h] = states[h] * dec[:, c] + _dot3((v2[0][:, c], v2[1][:, c]), (kd2[0][:, c], kd2[1][:, c]), TN)
            o_ref[rows, :] = jnp.concatenate(o, axis=1)
            on = jnp.concatenate(
                [oh * lax.rsqrt(jnp.mean(oh * oh, axis=1, keepdims=True) + RMS_EPS) for oh in o], axis=1)
            sg, _ = _sigmoid_gate(gp)
            y_ref[rows, :] = (on * ng_all * (gp * sg)).astype(BF16)
            return carry

        lax.fori_loop(0, nsc, step, 0, unroll=STEP_UNROLL)

    def col(off):
        return pl.BlockSpec((tb, wb), lambda h, j: (j, h + off * (nh // hb)))

    return _carried_call(
        kern, comm, name=name, grid=(nh // hb, nb),
        in_specs=[col(0), col(1), col(2), col(3),
                  pl.BlockSpec((3, wb), lambda h, j: (0, h)), pl.BlockSpec((1, wb), lambda h, j: (0, h))],
        out_specs=[col(0), col(0), pl.BlockSpec((nsc, hb, LANES, LANES), lambda h, j: (j, h, 0, 0))],
        out_shape=[jax.ShapeDtypeStruct((t, d), BF16), jax.ShapeDtypeStruct((t, d), F32),
                   jax.ShapeDtypeStruct((t // SUB, nh, LANES, LANES), F32)],
        scratch_shapes=[pltpu.VMEM((hb, LANES, LANES), F32)],
        operands=[proj, proj, proj, proj, lb_logits, norm_g], sem=("parallel", "arbitrary"))


def _hgrn2_bwd(proj, lb_logits, norm_g, o_raw, states, dy, name, comm=None):
    t, d4 = proj.shape
    d = d4 // 4
    nh = d // LANES
    hb = _pick(nh, (HEADS_PER_STEP, 2, 1))
    wb = hb * LANES
    tb = _pick(t, (256, 128, 64, 32, 16))
    nb = t // tb
    nsc = tb // SUB

    def kern(q_ref, f_ref, i_ref, g_ref, lbl_ref, ng_ref, o_ref, st_ref, dy_ref,
             dq_ref, df_ref, di_ref, dgp_ref, dlb_ref, dng_ref, ds_ref, gc_ref):
        j = pl.program_id(1)

        @pl.when(j == 0)
        def _():
            ds_ref[...] = jnp.zeros_like(ds_ref)
            gc_ref[...] = jnp.zeros_like(gc_ref)
            dlb_ref[...] = jnp.zeros_like(dlb_ref)
            dng_ref[...] = jnp.zeros_like(dng_ref)

        lb_all = _lower_bound(lbl_ref[...])
        ng_all = ng_ref[...]
        ltri, utri = _tri(True), _tri(False)
        rcol = lax.broadcasted_iota(jnp.int32, (SUB, 1), 0)
        rid = lax.broadcasted_iota(jnp.int32, (SUB, wb), 0)

        heads = [slice(h * LANES, (h + 1) * LANES) for h in range(hb)]

        def per_head(fn, n=SUB):
            return jnp.concatenate([jnp.broadcast_to(fn(c), (n, LANES)) for c in heads], axis=1)

        def step(it, carry):
            sc = nsc - 1 - it
            rows = pl.ds(pl.multiple_of(sc * SUB, SUB), SUB)
            qp, fp, v, gp = q_ref[rows, :], f_ref[rows, :], i_ref[rows, :], g_ref[rows, :]
            o, dyv = o_ref[rows, :], dy_ref[rows, :]
            sq, sf, snf, forget, lf, k = _hg_gates(qp, fp, lb_all)
            q = qp * sq
            bl = _running_sum(ltri, lf)
            ebl = jnp.exp(bl)
            bend = bl[SUB - 1:SUB, :]
            dec = jnp.exp(bend)
            dte = jnp.exp(bend - bl)
            r = per_head(lambda c: lax.rsqrt(jnp.mean(o[:, c] * o[:, c], axis=1, keepdims=True) + RMS_EPS))
            ohat = o * r
            sg, sng = _sigmoid_gate(gp)
            don = dyv * (gp * sg)
            dgp_ref[rows, :] = (dyv * (ohat * ng_all) * (sg * (1.0 + gp * sng))).astype(BF16)
            dng_ref[...] += jnp.sum(don * ohat, axis=0, keepdims=True)
            doh = don * ng_all
            dot_oh = doh * ohat
            do = r * (doh - ohat * per_head(lambda c: jnp.mean(dot_oh[:, c], axis=1, keepdims=True)))
            do2, qs2, kd2, v2 = _split2(do), _split2(q * ebl), _split2(k * dte), _split2(v)
            dq_h, dk_h, dv_h = [], [], []
            for h, c in enumerate(heads):
                dstate = ds_ref[h]
                ds2 = _split2(dstate)
                doc = (do2[0][:, c], do2[1][:, c])
                dq_h.append(_dot3(doc, _split2(st_ref[sc, h]), NN))
                dv_h.append(_dot3((kd2[0][:, c], kd2[1][:, c]), ds2, NT))
                dk_h.append(_dot3((v2[0][:, c], v2[1][:, c]), ds2, NN))
                ds_ref[h] = dstate * dec[:, c] + _dot3(doc, (qs2[0][:, c], qs2[1][:, c]), TN)
            dq = jnp.concatenate(dq_h, axis=1) * ebl
            dk = jnp.concatenate(dk_h, axis=1) * dte
            dv = jnp.concatenate(dv_h, axis=1)
            dq_t, dq_b = dq[:TILE], dq[TILE:]
            dk_i = [jnp.zeros((TILE, wb), F32), jnp.zeros((TILE, wb), F32)]
            dv_i = [jnp.zeros((TILE, wb), F32), jnp.zeros((TILE, wb), F32)]
            for s in range(SUB):
                lo = 0 if s < TILE else TILE
                n = SUB - lo
                e = jnp.exp(jnp.minimum(bl[lo:] - bl[s:s + 1, :], 0.0))
                qe = q[lo:] * e
                ks = k[s:s + 1, :]
                live = rcol[lo:] >= s
                pk = qe * ks
                dor = do[lo:]
                pv = dor * v[s:s + 1, :]
                a = per_head(lambda c: jnp.where(live, jnp.sum(pk[:, c], axis=1, keepdims=True), 0.0), n)
                da = per_head(lambda c: jnp.where(live, jnp.sum(pv[:, c], axis=1, keepdims=True), 0.0), n)
                ddq = da * (e * ks)
                if lo == 0:
                    dq_t, dq_b = dq_t + ddq[:TILE], dq_b + ddq[TILE:]
                else:
                    dq_b = dq_b + ddq
                here = rid[:TILE] == s - lo
                dk_i[lo // TILE] = jnp.where(here, jnp.sum(da * qe, axis=0, keepdims=True), dk_i[lo // TILE])
                dv_i[lo // TILE] = jnp.where(here, jnp.sum(a * dor, axis=0, keepdims=True), dv_i[lo // TILE])
            dq = jnp.concatenate([dq_t, dq_b], axis=0)
            dk = dk + jnp.concatenate(dk_i, axis=0)
            dv = dv + jnp.concatenate(dv_i, axis=0)
            w = q * dq - k * dk
            gc = gc_ref[...]
            dlf = _running_sum(utri, w) + gc
            gc_ref[...] = gc + jnp.sum(w, axis=0, keepdims=True)
            t1 = dlf / forget - dk
            df_ref[rows, :] = ((1.0 - lb_all) * sf * snf * t1).astype(BF16)
            dlb_ref[...] += jnp.sum(snf * t1, axis=0, keepdims=True)
            dq_ref[rows, :] = (dq * (sq * (1.0 + qp * (1.0 - sq)))).astype(BF16)
            di_ref[rows, :] = dv.astype(BF16)
            return carry

        lax.fori_loop(0, nsc, step, 0, unroll=STEP_UNROLL)

    def col(off):
        return pl.BlockSpec((tb, wb), lambda h, j: (nb - 1 - j, h + off * (nh // hb)))

    vec = pl.BlockSpec((1, wb), lambda h, j: (0, h))
    return _carried_call(
        kern, comm, name=name, grid=(nh // hb, nb),
        in_specs=[col(0), col(1), col(2), col(3), pl.BlockSpec((3, wb), lambda h, j: (0, h)), vec,
                  col(0), pl.BlockSpec((nsc, hb, LANES, LANES), lambda h, j: (nb - 1 - j, h, 0, 0)), col(0)],
        out_specs=[col(0), col(0), col(0), col(0), vec, vec],
        out_shape=[jax.ShapeDtypeStruct((t, d), BF16)] * 4 + [jax.ShapeDtypeStruct((1, d), F32)] * 2,
        scratch_shapes=[pltpu.VMEM((hb, LANES, LANES), F32), pltpu.VMEM((1, wb), F32)],
        operands=[proj, proj, proj, proj, lb_logits, norm_g, o_raw, states, dy], sem=("parallel", "arbitrary"))


_INV_SQRT2 = 0.7071067811865476
_INV_SQRT2PI = 0.3989422804014327


def _gelu(x):
    return 0.5 * x * (1.0 + lax.erf(x * _INV_SQRT2))


def _gelu_grad(x):
    return 0.5 * (1.0 + lax.erf(x * _INV_SQRT2)) + x * jnp.exp(-0.5 * x * x) * _INV_SQRT2PI


def _causal(w):
    r = lax.broadcasted_iota(jnp.int32, (GCHUNK, GCHUNK), 0)
    c = lax.broadcasted_iota(jnp.int32, (GCHUNK, GCHUNK), 1)
    return jnp.where(r >= c, w, 0.0)


def _sg_gate_fwd(pre, ln_g, ln_b, w_s, b_s_t, name):
    t, d2 = pre.shape
    d = d2 // 2
    ng = d // LANES

    def kern(pre_ref, g_ref, b_ref, ws_ref, bs_ref, y_ref):
        z = _gelu(pre_ref[...])
        u = z[:, :d]
        vhat, _ = _ln_hat(z[:, d:])
        vn = (vhat * g_ref[...] + b_ref[...]).astype(BF16)
        bs = bs_ref[...]
        for g in range(ng):
            cols = slice(g * LANES, (g + 1) * LANES)
            wc = _causal(ws_ref[g]).astype(BF16)
            gate = jnp.dot(wc, vn[:, cols], preferred_element_type=F32) + bs[:, g:g + 1]
            y_ref[:, cols] = (u[:, cols] * gate).astype(BF16)

    vec = pl.BlockSpec((1, d), lambda i: (0, 0))
    return pl.pallas_call(
        kern, name=name, grid=(t // GCHUNK,),
        in_specs=[pl.BlockSpec((GCHUNK, d2), lambda i: (i, 0)), vec, vec,
                  pl.BlockSpec((ng, GCHUNK, GCHUNK), lambda i: (0, 0, 0)),
                  pl.BlockSpec((GCHUNK, ng), lambda i: (0, 0))],
        out_specs=pl.BlockSpec((GCHUNK, d), lambda i: (i, 0)),
        out_shape=jax.ShapeDtypeStruct((t, d), BF16),
        compiler_params=_params(("parallel",)),
    )(pre, ln_g, ln_b, w_s, b_s_t)


def _sg_gate_bwd(pre, dy, ln_g, ln_b, w_s, b_s_t, name):
    t, d2 = pre.shape
    d = d2 // 2
    ng = d // LANES

    def kern(pre_ref, dy_ref, g_ref, b_ref, ws_ref, bs_ref, dpre_ref, dws_ref, dbs_ref, dg_ref, db_ref, dvn_ref):
        @pl.when(pl.program_id(0) == 0)
        def _():
            dws_ref[...] = jnp.zeros_like(dws_ref)
            dbs_ref[...] = jnp.zeros_like(dbs_ref)
            dg_ref[...] = jnp.zeros_like(dg_ref)
            db_ref[...] = jnp.zeros_like(db_ref)

        pre = pre_ref[...]
        z = _gelu(pre)
        u = z[:, :d]
        vhat, rstd = _ln_hat(z[:, d:])
        gv = g_ref[...]
        vn = (vhat * gv + b_ref[...]).astype(BF16)
        bs = bs_ref[...]
        dyv = dy_ref[...]
        gp = _gelu_grad(pre)
        lane = lax.broadcasted_iota(jnp.int32, (GCHUNK, ng), 1)
        dbs = jnp.zeros((GCHUNK, ng), F32)
        for g in range(ng):
            cols = slice(g * LANES, (g + 1) * LANES)
            wc = _causal(ws_ref[g]).astype(BF16)
            vng = vn[:, cols]
            gate = jnp.dot(wc, vng, preferred_element_type=F32) + bs[:, g:g + 1]
            dpre_ref[:, cols] = (dyv[:, cols] * gate * gp[:, cols]).astype(BF16)
            dgate = dyv[:, cols] * u[:, cols]
            dbs = dbs + jnp.where(lane == g, jnp.sum(dgate, axis=1, keepdims=True), 0.0)
            dgb = dgate.astype(BF16)
            dws_ref[g] += _causal(lax.dot_general(dgb, vng, NT, preferred_element_type=F32))
            dvn_ref[:, cols] = lax.dot_general(wc, dgb, TN, preferred_element_type=F32)
        dbs_ref[...] += dbs
        dvn = dvn_ref[...]
        dg_ref[...] += jnp.sum(dvn * vhat, axis=0, keepdims=True)
        db_ref[...] += jnp.sum(dvn, axis=0, keepdims=True)
        dvh = dvn * gv
        m1 = jnp.mean(dvh, axis=-1, keepdims=True)
        m2 = jnp.mean(dvh * vhat, axis=-1, keepdims=True)
        dpre_ref[:, d:] = (rstd * (dvh - m1 - vhat * m2) * gp[:, d:]).astype(BF16)

    vec = pl.BlockSpec((1, d), lambda i: (0, 0))
    wsp = pl.BlockSpec((ng, GCHUNK, GCHUNK), lambda i: (0, 0, 0))
    bsp = pl.BlockSpec((GCHUNK, ng), lambda i: (0, 0))
    return pl.pallas_call(
        kern, name=name, grid=(t // GCHUNK,),
        in_specs=[pl.BlockSpec((GCHUNK, d2), lambda i: (i, 0)), pl.BlockSpec((GCHUNK, d), lambda i: (i, 0)),
                  vec, vec, wsp, bsp],
        out_specs=[pl.BlockSpec((GCHUNK, d2), lambda i: (i, 0)), wsp, bsp, vec, vec],
        out_shape=[jax.ShapeDtypeStruct((t, d2), BF16), jax.ShapeDtypeStruct((ng, GCHUNK, GCHUNK), F32),
                   jax.ShapeDtypeStruct((GCHUNK, ng), F32), jax.ShapeDtypeStruct((1, d), F32),
                   jax.ShapeDtypeStruct((1, d), F32)],
        scratch_shapes=[pltpu.VMEM((GCHUNK, d), F32)],
        compiler_params=_params(("arbitrary",)),
    )(pre, dy, ln_g, ln_b, w_s, b_s_t)


def _adamw_math(w, g, m, v):
    m = ADAM_B1 * m + (1.0 - ADAM_B1) * g
    v = ADAM_B2 * v + (1.0 - ADAM_B2) * (g * g)
    m_hat = m / (1.0 - ADAM_B1 ** ADAM_STEP)
    v_hat = v / (1.0 - ADAM_B2 ** ADAM_STEP)
    return -ADAM_LR * (m_hat / (jnp.sqrt(v_hat) + ADAM_EPS) + ADAM_WD * w), m, v


ADAMW_BLOCK_BYTES = 3 << 19


def _adamw(w, gs, m, v, name, comm=None):
    nl, r, c = w.shape
    rb = _pick(r, tuple(p for p in (512, 256, 128, 64, 32, 16, 8) if p * c * 4 <= ADAMW_BLOCK_BYTES))

    def kern(w_ref, m_ref, v_ref, *rest):
        g_refs, (d_ref, mo_ref, vo_ref, go_ref) = rest[:nl], rest[nl:]
        layer = pl.program_id(0)
        g = g_refs[0][...]
        for k in range(1, nl):
            g = jnp.where(layer == k, g_refs[k][...], g)
        dlt, mm, vv = _adamw_math(w_ref[...], g, m_ref[...], v_ref[...])
        d_ref[...] = dlt
        mo_ref[...] = mm
        vo_ref[...] = vv
        go_ref[...] = g

    blk = pl.BlockSpec((None, rb, c), lambda l, i: (l, i, 0))
    g_specs = [pl.BlockSpec((rb, c), lambda l, i, k=k: (jnp.where(l == k, i, 0), 0)) for k in range(nl)]
    outs, carried = _carried_call(
        kern, comm, name=name, grid=(nl, r // rb), in_specs=[blk] * 3 + g_specs, out_specs=[blk] * 4,
        out_shape=[jax.ShapeDtypeStruct((nl, r, c), F32)] * 4, operands=[w, m, v, *gs], sem=("parallel", "parallel"))
    return outs if comm is None else (outs, carried)


CAST_BLOCK_BYTES = 1 << 21


def _cast_bf16(items, name, comm=None):
    metas, start = [], 0
    for arr, _ in items:
        _, r, c = arr.shape
        rb = _pick(r, tuple(p for p in (1024, 512, 256, 128, 64, 32, 16) if p * c * 4 <= CAST_BLOCK_BYTES))
        metas.append((start, r // rb, rb, c))
        start += r // rb
    n = len(items)

    def spec(p, layer=None):
        s0, steps, rb, c = metas[p]
        if layer is None:
            return pl.BlockSpec((rb, c), lambda s: (jnp.clip(s - s0, 0, steps - 1), 0))
        return pl.BlockSpec((None, rb, c), lambda s: (layer, jnp.clip(s - s0, 0, steps - 1), 0))

    def kern(*refs):
        s = pl.program_id(0)
        for p in range(n):
            s0, steps, _, _ = metas[p]

            @pl.when(jnp.logical_and(s >= s0, s < s0 + steps))
            def _(p=p):
                refs[n + p][...] = refs[p][...].astype(BF16)

    outs, carried = _carried_call(
        kern, comm, name=name, grid=(start,), in_specs=[spec(p, layer) for p, (_, layer) in enumerate(items)],
        out_specs=[spec(p) for p in range(n)],
        out_shape=[jax.ShapeDtypeStruct(arr.shape[1:], BF16) for arr, _ in items],
        operands=[arr for arr, _ in items], sem=("arbitrary",))
    return outs, carried


def _comm_call(plan, name):
    ni, no = len(plan.ins), len(plan.out_shapes)

    def body(*refs):
        copies = plan.build(refs[:ni], refs[ni:ni + no], refs[-2], refs[-1], 0)
        for cp in copies:
            cp.start()
        for cp in copies:
            cp.wait()

    anyspec = pl.BlockSpec(memory_space=pl.ANY)
    return pl.pallas_call(
        body, name=name, in_specs=[anyspec] * ni, out_specs=[anyspec] * no, out_shape=plan.out_shapes,
        input_output_aliases=plan.aliases,
        scratch_shapes=[pltpu.SemaphoreType.DMA((plan.n_sems,)), pltpu.SemaphoreType.DMA((plan.n_sems,))],
        compiler_params=pltpu.CompilerParams(has_side_effects=True),
    )(*plan.ins)


def _lb_logits_grad(lb_logits, dlb, name):
    def kern(l_ref, d_ref, o_ref):
        lg = l_ref[...]
        m = jnp.max(lg, axis=0, keepdims=True)
        e = jnp.exp(lg - m)
        p = e / jnp.sum(e, axis=0, keepdims=True)
        row = lax.broadcasted_iota(jnp.int32, lg.shape, 0)
        o_ref[...] = d_ref[...] * p[0:1, :] * (jnp.where(row == 0, 1.0, 0.0) - p)

    return pl.pallas_call(kern, name=name, out_shape=jax.ShapeDtypeStruct(lb_logits.shape, F32))(lb_logits, dlb)


def _sum_devices(others, own, me, name):
    n, r, c = others.shape
    rb = _pick(r, (512, 256, 128, 64, 32, 16, 8))

    def kern(me_ref, a_ref, own_ref, o_ref):
        mine = own_ref[...]
        acc = jnp.where(me_ref[0] == 0, mine, a_ref[0])
        for i in range(1, n):
            acc = acc + jnp.where(me_ref[0] == i, mine, a_ref[i])
        o_ref[...] = acc

    return pl.pallas_call(
        kern, name=name,
        grid_spec=pltpu.PrefetchScalarGridSpec(
            num_scalar_prefetch=1, grid=(r // rb,),
            in_specs=[pl.BlockSpec((n, rb, c), lambda i, s: (0, i, 0)), pl.BlockSpec((rb, c), lambda i, s: (i, 0))],
            out_specs=pl.BlockSpec((rb, c), lambda i, s: (i, 0))),
        out_shape=jax.ShapeDtypeStruct((r, c), F32),
        compiler_params=_params(("parallel",)),
    )(me, others, own)


def _place():
    x, y, c = lax.axis_index("x"), lax.axis_index("y"), lax.axis_index("c")
    return x, y, c


class _Plan:
    def __init__(self, ins, out_shapes, aliases, n_sems, build):
        self.ins, self.out_shapes, self.aliases, self.n_sems, self.build = list(ins), list(out_shapes), aliases, n_sems, build


def _merge(*plans):
    ins, outs, aliases, subs, sems = [], [], {}, [], 0
    for p in plans:
        for k, v in p.aliases.items():
            aliases[len(ins) + k] = len(outs) + v
        subs.append((p, len(ins), len(outs), sems))
        ins += p.ins
        outs += p.out_shapes
        sems += p.n_sems

    def build(in_refs, out_refs, send_sems, recv_sems, base):
        copies = []
        for p, i0, o0, s0 in subs:
            copies += p.build(in_refs[i0:i0 + len(p.ins)], out_refs[o0:o0 + len(p.out_shapes)], send_sems, recv_sems,
                              base + s0)
        return copies

    return _Plan(ins, outs, aliases, sems, build)


def _remote(src, dst, send_sems, recv_sems, k, to):
    return pltpu.make_async_remote_copy(src_ref=src, dst_ref=dst, send_sem=send_sems.at[k], recv_sem=recv_sems.at[k],
                                        device_id=to, device_id_type=MESH)


def _plan_gather_ici(shards):
    n = len(shards)

    def build(ins, outs, send_sems, recv_sems, base):
        x, y, c = _place()
        me = 2 * x + y
        copies = []
        for a in range(n):
            h = ins[a].shape[0] // 2
            rows = pl.ds(c * h, h)
            for r in (1, 2, 3):
                px, py, _ = _chip_rel(x, y, r)
                copies.append(_remote(ins[a].at[rows, :], outs[a].at[me, rows, :], send_sems, recv_sems,
                                      base + 4 * a + r - 1, (px, py, c)))
            copies.append(_remote(ins[a], outs[a].at[me], send_sems, recv_sems, base + 4 * a + 3, (x, y, 1 - c)))
        return copies

    return _Plan(shards, [jax.ShapeDtypeStruct((N_CHIPS,) + s.shape, s.dtype) for s in shards], {}, 4 * n, build)


def _plan_gather_pass(gathered):
    n = len(gathered)

    def build(ins, outs, send_sems, recv_sems, base):
        x, y, c = _place()
        copies = []
        for a in range(n):
            h = outs[a].shape[1] // 2
            rows = pl.ds(c * h, h)
            for r in (1, 2, 3):
                _, _, shard = _chip_rel(x, y, r)
                piece = outs[a].at[shard, rows, :]
                copies.append(_remote(piece, piece, send_sems, recv_sems, base + 3 * a + r - 1, (x, y, 1 - c)))
        return copies

    return _Plan(gathered, [jax.ShapeDtypeStruct(g.shape, g.dtype) for g in gathered], {a: a for a in range(n)},
                 3 * n, build)


def _plan_swap(split):
    n = len(split)

    def build(ins, outs, send_sems, recv_sems, base):
        x, y, c = _place()
        return [_remote(ins[a].at[j, 1 - c], outs[a].at[j], send_sems, recv_sems, base + N_CHIPS * a + j, (x, y, 1 - c))
                for a in range(n) for j in range(N_CHIPS)]

    return _Plan(split, [jax.ShapeDtypeStruct((N_CHIPS,) + g.shape[2:], g.dtype) for g in split], {}, N_CHIPS * n, build)


def _plan_scatter(parts):
    n = len(parts)

    def build(ins, outs, send_sems, recv_sems, base):
        x, y, c = _place()
        copies = []
        for a in range(n):
            for r in (1, 2, 3):
                px, py, shard = _chip_rel(x, y, r)
                copies.append(_remote(ins[a].at[shard], outs[a].at[r - 1], send_sems, recv_sems, base + 3 * a + r - 1,
                                      (px, py, c)))
        return copies

    return _Plan(parts, [jax.ShapeDtypeStruct((3,) + p.shape[1:], p.dtype) for p in parts], {}, 3 * n, build)


def _plan_join(bufs):
    n = len(bufs)

    def build(ins, outs, send_sems, recv_sems, base):
        x, y, c = _place()
        return [_remote(outs[a].at[c], outs[a].at[c], send_sems, recv_sems, base + a, (x, y, 1 - c)) for a in range(n)]

    return _Plan(bufs, [jax.ShapeDtypeStruct(b.shape, b.dtype) for b in bufs], {a: a for a in range(n)}, n, build)


def _carried_call(kern, plan, *, name, grid, in_specs, out_specs, out_shape, operands, scratch_shapes=(),
                  aliases=None, sem=None):
    n_in, n_out, n_sc = len(operands), len(out_shape), len(scratch_shapes)
    aliases = dict(aliases or {})
    if plan is None:
        outs = pl.pallas_call(kern, name=name, grid=grid, in_specs=in_specs, out_specs=out_specs, out_shape=out_shape,
                              scratch_shapes=list(scratch_shapes), input_output_aliases=aliases,
                              compiler_params=_params(sem))(*operands)
        return list(outs), []
    ci, co = len(plan.ins), len(plan.out_shapes)
    for k, v in plan.aliases.items():
        aliases[n_in + k] = n_out + v
    steps = tuple(grid)

    def body(*refs):
        ins, cins = refs[:n_in], refs[n_in:n_in + ci]
        outs = refs[n_in + ci:n_in + ci + n_out]
        couts = refs[n_in + ci + n_out:n_in + ci + n_out + co]
        scratch = refs[n_in + ci + n_out + co:n_in + ci + n_out + co + n_sc]
        send_sems, recv_sems = refs[-2], refs[-1]
        first = functools.reduce(jnp.logical_and, [pl.program_id(a) == 0 for a in range(len(steps))])
        last = functools.reduce(jnp.logical_and, [pl.program_id(a) == steps[a] - 1 for a in range(len(steps))])

        @pl.when(first)
        def _():
            for cp in plan.build(cins, couts, send_sems, recv_sems, 0):
                cp.start()

        kern(*ins, *outs, *scratch)

        @pl.when(last)
        def _():
            for cp in plan.build(cins, couts, send_sems, recv_sems, 0):
                cp.wait()

    anyspec = pl.BlockSpec(memory_space=pl.ANY)
    outs = pl.pallas_call(
        body, name=name, grid=grid, in_specs=list(in_specs) + [anyspec] * ci,
        out_specs=list(out_specs) + [anyspec] * co, out_shape=list(out_shape) + plan.out_shapes,
        scratch_shapes=list(scratch_shapes) + [pltpu.SemaphoreType.DMA((plan.n_sems,)),
                                               pltpu.SemaphoreType.DMA((plan.n_sems,))],
        input_output_aliases=aliases,
        compiler_params=_params(("arbitrary",) * len(steps)),
    )(*operands, *plan.ins)
    return list(outs[:n_out]), list(outs[n_out:])


def _chip_rel(x, y, r):
    px = x if r < 2 else 1 - x
    py = y if r % 2 == 0 else 1 - y
    return px, py, 2 * px + py


def _allgather_whole(arr, name):
    def body(in_ref, out_ref, send_sems, recv_sems, loc_sem):
        x, y, c = _place()
        me = 2 * x + y
        local = pltpu.make_async_copy(in_ref, out_ref.at[me], loc_sem)
        local.start()
        sends = []
        for r in (1, 2, 3):
            px, py, _ = _chip_rel(x, y, r)
            sends.append(pltpu.make_async_remote_copy(
                src_ref=in_ref, dst_ref=out_ref.at[me], send_sem=send_sems.at[r - 1], recv_sem=recv_sems.at[r - 1],
                device_id=(px, py, c), device_id_type=MESH))
        for cp in sends:
            cp.start()
        for r in (1, 2, 3):
            px, py, shard = _chip_rel(x, y, r)
            pltpu.make_async_remote_copy(
                src_ref=in_ref, dst_ref=out_ref.at[shard], send_sem=send_sems.at[r - 1], recv_sem=recv_sems.at[r - 1],
                device_id=(px, py, c), device_id_type=MESH).wait_recv()
        for cp in sends:
            cp.wait_send()
        local.wait()

    anyspec = pl.BlockSpec(memory_space=pl.ANY)
    return pl.pallas_call(
        body, name=name, in_specs=[anyspec], out_specs=anyspec,
        out_shape=jax.ShapeDtypeStruct((N_CHIPS,) + arr.shape, arr.dtype),
        scratch_shapes=[pltpu.SemaphoreType.DMA((3,)), pltpu.SemaphoreType.DMA((3,)), pltpu.SemaphoreType.DMA],
        compiler_params=pltpu.CompilerParams(has_side_effects=True),
    )(arr)


def _plan_gather_all(buf):
    def build(ins, outs, send_sems, recv_sems, base):
        x, y, c = _place()
        me = 4 * x + 2 * y + c
        copies = []
        for r in range(1, N_DEV):
            px, py, _ = _chip_rel(x, y, r // 2)
            pc = c if r % 2 == 0 else 1 - c
            copies.append(_remote(ins[0], outs[0].at[me], send_sems, recv_sems, base + r - 1, (px, py, pc)))
        return copies

    return _Plan([buf, jnp.zeros((N_DEV,) + buf.shape, buf.dtype)],
                 [jax.ShapeDtypeStruct((N_DEV,) + buf.shape, buf.dtype)], {1: 0}, N_DEV - 1, build)


def _add_half(grad, recv, sel, name):
    _, _, rh, cw = grad.shape
    rb = _pick(rh, (512, 256, 176, 128, 64, 32, 16, 8))

    def kern(sel_ref, g_ref, r_ref, o_ref):
        o_ref[...] = (g_ref[...] + r_ref[...]).astype(BF16)

    return pl.pallas_call(
        kern, name=name,
        grid_spec=pltpu.PrefetchScalarGridSpec(
            num_scalar_prefetch=1, grid=(N_CHIPS - 1, rh // rb),
            in_specs=[pl.BlockSpec((None, None, rb, cw), lambda j, i, s: (s[2 + j], s[0], i, 0)),
                      pl.BlockSpec((None, rb, cw), lambda j, i, s: (s[2 + j], i, 0))],
            out_specs=pl.BlockSpec((None, rb, cw), lambda j, i, s: (s[2 + j], i, 0))),
        out_shape=jax.ShapeDtypeStruct((N_CHIPS, rh, cw), BF16),
        compiler_params=_params(("parallel", "parallel")),
    )(sel, grad, recv)


def _add_own(grad, recv, got, sel, name):
    _, _, rh, cw = grad.shape
    rb = _pick(rh, (512, 256, 176, 128, 64, 32, 16, 8))

    def kern(sel_ref, g_ref, r_ref, b_ref, o_ref):
        own = g_ref[...] + r_ref[...]
        o_ref[...] = ((own + b_ref[0].astype(F32)) + b_ref[1].astype(F32)) + b_ref[2].astype(F32)

    return pl.pallas_call(
        kern, name=name,
        grid_spec=pltpu.PrefetchScalarGridSpec(
            num_scalar_prefetch=1, grid=(rh // rb,),
            in_specs=[pl.BlockSpec((None, None, rb, cw), lambda i, s: (s[1], s[0], i, 0)),
                      pl.BlockSpec((None, rb, cw), lambda i, s: (s[1], i, 0)),
                      pl.BlockSpec((3, rb, cw), lambda i, s: (0, i, 0))],
            out_specs=pl.BlockSpec((None, rb, cw), lambda i, s: (s[0], i, 0))),
        out_shape=jax.ShapeDtypeStruct((2, rh, cw), F32),
        compiler_params=_params(("parallel",)),
    )(sel, grad, recv, got)


def _stacked(g):
    return g.reshape(1, g.shape[0] * g.shape[1], g.shape[2])


def _halves(g):
    g = g.reshape(N_CHIPS, g.shape[0] * g.shape[1] // N_CHIPS, g.shape[2])
    return g.reshape(N_CHIPS, 2, g.shape[1] // 2, g.shape[2])


def _whole(f):
    return f.reshape(f.shape[0] * f.shape[1], f.shape[2])


def _step(x3, tgt, sm, w, mom, var):
    x, y, c = _place()
    sel = jnp.stack([c, 2 * x + y] + [_chip_rel(x, y, r)[2] for r in (1, 2, 3)]).astype(jnp.int32)
    wg = {}
    x2 = x3[0]
    cast, landed = _cast_bf16(
        [(x3, 0), (w["hg_w_out"], 0), (w["sg_w_in"], 0), (w["sg_w_out"], 0), (w["ffn_w_up"], 0), (w["ffn_w_up"], 1),
         (w["ffn_w_down"], 0), (w["ffn_w_down"], 1)], "cast_shards",
        comm=_plan_gather_ici([w["hg_w_in"][0].astype(BF16)]))
    xb = cast[0]
    sh = dict(zip(("hg_out", "sg_in", "sg_out", "up0", "up1", "dn0", "dn1"), cast[1:]))
    wg["hg_in"] = _comm_call(_plan_gather_pass(landed), "gather_hg_in_pass")[0]
    proj, landed = _matmul(xb, wg["hg_in"], mode="nn", nsh=N_CHIPS, name="hg_in",
                           comm=_plan_gather_ici([sh["hg_out"], sh["dn0"]]))
    (yhg, o_raw, states), got = _hgrn2_fwd(
        proj, sm["lb_logits"], sm["hg_norm_g"], "hgrn2_fwd",
        comm=_merge(_plan_gather_pass(landed), _plan_gather_ici([sh[k] for k in ("up0", "sg_in", "sg_out")])))
    wg["hg_out"], wg["dn0"], landed = got[0], got[1], got[2:]
    xin1, got = _matmul(yhg, _stacked(wg["hg_out"]), mode="nn", nsh=1, resid=x2, alpha=ALPHA, name="hg_out",
                        comm=_plan_gather_pass(landed))
    wg["up0"], wg["sg_in"], wg["sg_out"] = got
    h1, h1b = _ln_fwd(xin1, sm["ln1_g"][0:1], sm["ln1_b"][0:1], "l0_ln1")
    u0, landed = _matmul(h1b, wg["up0"], mode="nn", nsh=N_CHIPS, name="l0_ffn_up", comm=_plan_gather_ici([sh["up1"]]))
    gact0 = _conv_gate_fwd(u0, sm["conv_w"][0], sm["conv_b"][0:1], "l0_ffn_gate")
    xin2, got = _matmul(gact0, _stacked(wg["dn0"]), mode="nn", nsh=1, resid=h1, alpha=ALPHA, name="l0_ffn_down",
                        comm=_merge(_plan_gather_pass(landed), _plan_gather_ici([sh["dn1"]])))
    wg["up1"], landed = got[0], got[1:]
    h2, h2b = _ln_fwd(xin2, sm["ln2_g"][0:1], sm["ln2_b"][0:1], "l0_ffn_ln")
    pre, got = _matmul(h2b, wg["sg_in"], mode="nn", nsh=N_CHIPS, name="sg_in", comm=_plan_gather_pass(landed))
    wg["dn1"] = got[0]
    ysg = _sg_gate_fwd(pre, sm["sg_ln_g"], sm["sg_ln_b"], sm["sg_w_s"], sm["sg_b_s_t"], "sg_gate")
    xin3 = _matmul(ysg, _stacked(wg["sg_out"]), mode="nn", nsh=1, resid=h2, alpha=ALPHA, name="sg_out")
    h3, h3b = _ln_fwd(xin3, sm["ln1_g"][1:2], sm["ln1_b"][1:2], "l1_ln1")
    u1 = _matmul(h3b, wg["up1"], mode="nn", nsh=N_CHIPS, name="l1_ffn_up")
    gact1 = _conv_gate_fwd(u1, sm["conv_w"][1], sm["conv_b"][1:2], "l1_ffn_gate")
    xin4 = _matmul(gact1, _stacked(wg["dn1"]), mode="nn", nsh=1, resid=h3, alpha=ALPHA, name="l1_ffn_down")

    gs, grad, split, recv, part = {}, {}, {}, {}, {}

    def swap_on(call, keys):
        for k in keys:
            split[k] = _halves(grad[k])
        out, got = call(_plan_swap([split[k] for k in keys]))
        for k, r in zip(keys, got):
            recv[k] = r
            part[k] = _add_half(split[k], r, sel, f"rs_addhalf_{k}")
        return out

    def ffn_bwd(u, gact, hb_in, dxin, dxin_b, w_up, w_down, layer, tag, up, down, waiting):
        dgact = _matmul(dxin_b, _stacked(w_down), mode="nt", nsh=1, name=f"{tag}_ddown")
        grad[down] = _matmul(gact, dxin_b, mode="tn", nsh=1, name=f"{tag}_wdown")
        da, db, dcw, dcb = _conv_gate_bwd(u, dgact, sm["conv_w"][layer], sm["conv_b"][layer:layer + 1], f"{tag}_dgate")
        grad[up] = swap_on(lambda plan: _matmul(hb_in, [da, db], mode="tn", nsh=N_CHIPS, name=f"{tag}_wup", comm=plan),
                           waiting + [down])
        dh = swap_on(lambda plan: _matmul([da, db], w_up, mode="nt", nsh=N_CHIPS, resid=dxin, alpha=ALPHA,
                                          name=f"{tag}_dup", comm=plan), [up])
        return dh, dcw, dcb

    dx, dxb, dg4, db4, loss = _ln_bwd(xin4, tgt, sm["ln2_g"][1:2], sm["ln2_b"][1:2], "l1_ln2_bwd", loss_head=True)
    dh3, dcw1, dcb1 = ffn_bwd(u1, gact1, h3b, dx, dxb, wg["up1"], wg["dn1"], 1, "l1_ffn", "up1", "dn1", [])
    dx, dxb, dg3, db3 = _ln_bwd(xin3, dh3, sm["ln1_g"][1:2], sm["ln1_b"][1:2], "l1_ln1_bwd")
    grad["sg_out"] = _matmul(ysg, dxb, mode="tn", nsh=1, name="sg_wout")
    dysg = swap_on(lambda plan: _matmul(dxb, _stacked(wg["sg_out"]), mode="nt", nsh=1, name="sg_dout", comm=plan),
                   ["sg_out"])
    dpre, gs["sg_w_s"], gs["sg_b_s_t"], gs["sg_ln_g"], gs["sg_ln_b"] = _sg_gate_bwd(
        pre, dysg, sm["sg_ln_g"], sm["sg_ln_b"], sm["sg_w_s"], sm["sg_b_s_t"], "sg_gate_bwd")
    grad["sg_in"] = _matmul(h2b, dpre, mode="tn", nsh=N_CHIPS, name="sg_win")
    dh2 = _matmul(dpre, wg["sg_in"], mode="nt", nsh=N_CHIPS, resid=dx, alpha=ALPHA, name="sg_din")
    dx, dxb, dg2, db2 = _ln_bwd(xin2, dh2, sm["ln2_g"][0:1], sm["ln2_b"][0:1], "l0_ln2_bwd")
    dh1, dcw0, dcb0 = ffn_bwd(u0, gact0, h1b, dx, dxb, wg["up0"], wg["dn0"], 0, "l0_ffn", "up0", "dn0", ["sg_in"])
    dx, dxb, dg1, db1 = _ln_bwd(xin1, dh1, sm["ln1_g"][0:1], sm["ln1_b"][0:1], "l0_ln1_bwd")
    grad["hg_out"] = _matmul(yhg, dxb, mode="tn", nsh=1, name="hg_wout")
    dyhg = swap_on(lambda plan: _matmul(dxb, _stacked(wg["hg_out"]), mode="nt", nsh=1, name="hg_dout", comm=plan),
                   ["hg_out"])
    early = ("dn1", "up1", "sg_out", "sg_in", "dn0", "up0", "hg_out")
    dparts, got = _hgrn2_bwd(proj, sm["lb_logits"], sm["hg_norm_g"], o_raw, states, dyhg, "hgrn2_bwd",
                             comm=_plan_scatter([part[k] for k in early]))
    gs["lb"], gs["hg_norm_g"] = dparts[4], dparts[5]
    gs["ln1_g"] = jnp.concatenate([dg1, dg3], axis=0)
    gs["ln1_b"] = jnp.concatenate([db1, db3], axis=0)
    gs["ln2_g"] = jnp.concatenate([dg2, dg4], axis=0)
    gs["ln2_b"] = jnp.concatenate([db2, db4], axis=0)
    gs["conv_w"] = jnp.stack([dcw0, dcw1], axis=0)
    gs["conv_b"] = jnp.concatenate([dcb0, dcb1], axis=0)
    packed, layout = _pack(gs)
    mine = [_add_own(split[k], recv[k], b, sel, f"rs_addown_{k}") for k, b in zip(early, got)]
    grad["hg_in"], got = _matmul(xb, list(dparts[:4]), mode="tn", nsh=N_CHIPS, name="hg_win",
                                 comm=_merge(_plan_join(mine), _plan_gather_all(packed)))
    red = {k: _whole(f) for k, f in zip(early, got)}
    me8 = jnp.reshape(4 * x + 2 * y + c, (1,)).astype(jnp.int32)
    summed = _unpack(_sum_devices(got[len(early)], packed, me8, "sum_small_grads"), layout)
    gx = swap_on(lambda plan: _matmul(dparts[0], wg["hg_in"], mode="nt", nsh=1, b_off=0, resid=dx, alpha=ALPHA,
                                      name="hg_din_q", comm=plan), ["hg_in"])
    gx, got = _matmul(list(dparts[1:4]), wg["hg_in"], mode="nt", nsh=3, b_off=1, resid=gx, alpha=1.0, name="hg_din_fig",
                      comm=_plan_scatter([part["hg_in"]]))
    mine = _add_own(split["hg_in"], recv["hg_in"], got[0], sel, "rs_addown_hg_in")
    upd = {}
    upd["hg_w_out"], full = _adamw(w["hg_w_out"], [red["hg_out"]], mom["hg_w_out"], var["hg_w_out"], "adamw_hg_w_out",
                                   comm=_plan_join([mine]))
    red["hg_in"] = _whole(full[0])
    for k, src in (("ffn_w_up", ("up0", "up1")), ("ffn_w_down", ("dn0", "dn1")), ("sg_w_in", ("sg_in",)),
                   ("sg_w_out", ("sg_out",)), ("hg_w_in", ("hg_in",))):
        upd[k] = _adamw(w[k], [red[s] for s in src], mom[k], var[k], f"adamw_{k}")
    return loss, gx, summed, upd


_SMALL_ORDER = ("lb", "hg_norm_g", "sg_w_s", "sg_b_s_t", "conv_b", "ln1_g", "ln1_b", "ln2_g", "ln2_b",
                "conv_w", "sg_ln_g", "sg_ln_b")


PACK_ROWS = 512


def _pack(parts):
    flat, layout, off = [], [], 0
    for k in _SMALL_ORDER:
        a = parts[k]
        n = a.size
        pad = (-n) % LANES
        flat.append(jnp.pad(a.reshape(-1), (0, pad)))
        layout.append((k, off, n, a.shape))
        off += n + pad
    flat.append(jnp.zeros(((-off) % (PACK_ROWS * LANES),), F32))
    return jnp.concatenate(flat).reshape(-1, LANES), layout


def _unpack(buf, layout):
    flat = buf.reshape(-1)
    return {k: flat[off:off + n].reshape(shape) for k, off, n, shape in layout}


def kernel(x, lb_logits, hg_w_in, hg_norm_g, hg_w_out, sg_w_in, sg_ln_g, sg_ln_b, sg_w_s, sg_b_s, sg_w_out, ffn_w_up, ffn_conv_w, ffn_conv_b, ffn_w_down, ln1_g, ln1_b, ln2_g, ln2_b, loss_target, m_lb_logits, m_hg_w_in, m_hg_norm_g, m_hg_w_out, m_sg_w_in, m_sg_ln_g, m_sg_ln_b, m_sg_w_s, m_sg_b_s, m_sg_w_out, m_ffn_w_up, m_ffn_conv_w, m_ffn_conv_b, m_ffn_w_down, m_ln1_g, m_ln1_b, m_ln2_g, m_ln2_b, v_lb_logits, v_hg_w_in, v_hg_norm_g, v_hg_w_out, v_sg_w_in, v_sg_ln_g, v_sg_ln_b, v_sg_w_s, v_sg_b_s, v_sg_w_out, v_ffn_w_up, v_ffn_conv_w, v_ffn_conv_b, v_ffn_w_down, v_ln1_g, v_ln1_b, v_ln2_g, v_ln2_b):
    names = ("lb_logits", "hg_w_in", "hg_norm_g", "hg_w_out", "sg_w_in", "sg_ln_g", "sg_ln_b", "sg_w_s", "sg_b_s",
             "sg_w_out", "ffn_w_up", "ffn_conv_w", "ffn_conv_b", "ffn_w_down", "ln1_g", "ln1_b", "ln2_g", "ln2_b")
    w = dict(zip(names, (lb_logits, hg_w_in, hg_norm_g, hg_w_out, sg_w_in, sg_ln_g, sg_ln_b, sg_w_s, sg_b_s,
                         sg_w_out, ffn_w_up, ffn_conv_w, ffn_conv_b, ffn_w_down, ln1_g, ln1_b, ln2_g, ln2_b)))
    mom = dict(zip(names, (m_lb_logits, m_hg_w_in, m_hg_norm_g, m_hg_w_out, m_sg_w_in, m_sg_ln_g, m_sg_ln_b, m_sg_w_s,
                           m_sg_b_s, m_sg_w_out, m_ffn_w_up, m_ffn_conv_w, m_ffn_conv_b, m_ffn_w_down, m_ln1_g,
                           m_ln1_b, m_ln2_g, m_ln2_b)))
    var = dict(zip(names, (v_lb_logits, v_hg_w_in, v_hg_norm_g, v_hg_w_out, v_sg_w_in, v_sg_ln_g, v_sg_ln_b, v_sg_w_s,
                           v_sg_b_s, v_sg_w_out, v_ffn_w_up, v_ffn_conv_w, v_ffn_conv_b, v_ffn_w_down, v_ln1_g,
                           v_ln1_b, v_ln2_g, v_ln2_b)))
    tgt = loss_target[0]
    fq = ffn_conv_w.shape[2]
    dq = sg_ln_g.shape[1]
    cx, cy, _ = _place()
    me = 2 * cx + cy

    wide = max(fq, dq)
    tiny = jnp.concatenate([jnp.pad(ffn_conv_w.reshape(6, fq), ((0, 0), (0, wide - fq))),
                            jnp.pad(sg_ln_g, ((0, 0), (0, wide - dq))),
                            jnp.pad(sg_ln_b, ((0, 0), (0, wide - dq)))], axis=0)
    tiny_all = _allgather_whole(tiny, "gather_small")
    conv_w_full = jnp.transpose(tiny_all[:, 0:6, :fq].reshape(N_CHIPS, 2, 3, fq), (1, 2, 0, 3)).reshape(2, 3, N_CHIPS * fq)
    sm = {"lb_logits": lb_logits, "hg_norm_g": hg_norm_g, "ln1_g": ln1_g, "ln1_b": ln1_b, "ln2_g": ln2_g,
          "ln2_b": ln2_b, "conv_w": conv_w_full, "conv_b": ffn_conv_b,
          "sg_ln_g": tiny_all[:, 6, :dq].reshape(1, N_CHIPS * dq),
          "sg_ln_b": tiny_all[:, 7, :dq].reshape(1, N_CHIPS * dq),
          "sg_w_s": sg_w_s[0], "sg_b_s_t": jnp.transpose(sg_b_s[0])}

    loss_row, grad_x, summed, upd = _step(x, tgt, sm, w, mom, var)
    loss = lax.psum(loss_row[0, 0], ("x", "y", "c"))

    grads = {
        "lb_logits": _lb_logits_grad(lb_logits, summed["lb"], "lb_logits_grad"),
        "hg_norm_g": summed["hg_norm_g"],
        "sg_ln_g": lax.dynamic_slice_in_dim(summed["sg_ln_g"], me * dq, dq, axis=1),
        "sg_ln_b": lax.dynamic_slice_in_dim(summed["sg_ln_b"], me * dq, dq, axis=1),
        "sg_w_s": summed["sg_w_s"][None], "sg_b_s": jnp.transpose(summed["sg_b_s_t"])[None],
        "ffn_conv_w": lax.dynamic_slice_in_dim(summed["conv_w"], me * fq, fq, axis=2),
        "ffn_conv_b": summed["conv_b"],
        "ln1_g": summed["ln1_g"], "ln1_b": summed["ln1_b"], "ln2_g": summed["ln2_g"], "ln2_b": summed["ln2_b"],
    }

    delta, new_m, new_v = {}, {}, {}
    for k, (dlt, mm, vv, gg) in upd.items():
        delta[k], new_m[k], new_v[k], grads[k] = dlt, mm, vv, gg
    small_names = [k for k in names if k not in upd]

    def pack_small(src):
        flat = [src[k].reshape(-1) for k in small_names]
        n = sum(a.size for a in flat)
        flat.append(jnp.zeros(((-n) % (PACK_ROWS * LANES),), F32))
        return jnp.concatenate(flat).reshape(1, -1, LANES)

    outs = _adamw(pack_small(w), [pack_small(grads)[0]], pack_small(mom), pack_small(var), "adamw_small")
    off = 0
    for k in small_names:
        n = w[k].size
        for dst, o in zip((delta, new_m, new_v), outs):
            dst[k] = o.reshape(-1)[off:off + n].reshape(w[k].shape)
        off += n

    return (loss, grad_x[None], *[grads[k] for k in names], *[delta[k] for k in names],
            *[new_m[k] for k in names], *[new_v[k] for k in names])
```

```python
import functools

import jax
import jax.numpy as jnp
from jax import lax
from jax.experimental import pallas as pl
from jax.experimental.pallas import tpu as pltpu

F32 = jnp.float32
BF16 = jnp.bfloat16
HI = lax.Precision.HIGHEST
MESH = pl.DeviceIdType.MESH

ALPHA = (2 * 2) ** 0.25
LN_EPS = 1e-5
RMS_EPS = 1e-6
ADAM_LR, ADAM_B1, ADAM_B2, ADAM_EPS, ADAM_WD, ADAM_STEP = 0.001, 0.9, 0.999, 1e-08, 0.01, 10

LANES = 128
SUB = 16
TILE = 8
GCHUNK = 128
VMEM_LIMIT = 56 * 1024 * 1024
N_CHIPS = 4
N_DEV = 8

NT = (((1,), (1,)), ((), ()))
TN = (((0,), (0,)), ((), ()))
NN = (((1,), (0,)), ((), ()))


def _pick(dim, prefs):
    for p in prefs:
        if dim % p == 0:
            return p
    return dim


def _params(sem=None, **kw):
    return pltpu.CompilerParams(dimension_semantics=sem, vmem_limit_bytes=VMEM_LIMIT, **kw)


def _sigmoid_pair(x):
    e = jnp.exp(-jnp.abs(x))
    inv = 1.0 / (1.0 + e)
    pos = x >= 0
    return jnp.where(pos, inv, e * inv), jnp.where(pos, e * inv, inv)


def _sigmoid_gate(x):
    t = 0.5 * jnp.tanh(0.5 * x)
    return 0.5 + t, 0.5 - t


def _ln_hat(x):
    mu = jnp.mean(x, axis=-1, keepdims=True)
    xc = x - mu
    var = jnp.mean(xc * xc, axis=-1, keepdims=True)
    rstd = lax.rsqrt(var + LN_EPS)
    return xc * rstd, rstd


def _lower_bound(logits):
    m = jnp.max(logits, axis=0, keepdims=True)
    e = jnp.exp(logits - m)
    return e[0:1, :] / jnp.sum(e, axis=0, keepdims=True)


MATMUL_VMEM_BUDGET = 40 * 1024 * 1024


def _fit_bk(kdim, bm, bn, out_dtype, has_resid, na=1, nb=1):
    fixed = bm * bn * (4 + 2 * jnp.dtype(out_dtype).itemsize + (8 if has_resid else 0))
    best = LANES
    for bk in range(LANES, kdim + 1, LANES):
        if kdim % bk == 0 and fixed + 4 * bk * (bm * na + bn * nb) <= MATMUL_VMEM_BUDGET:
            best = bk
    return best


def _fit_bm_bk(mdim, prefs, kdim, bn, out_dtype, has_resid, na=1, nb=1):
    best = None
    fits = [bm for bm in prefs if mdim % bm == 0][:2] or [mdim]
    for bm in fits:
        bk = _fit_bk(kdim, bm, bn, out_dtype, has_resid, na, nb)
        if best is None or kdim // bk < kdim // best[1]:
            best = (bm, bk)
    return best


def _matmul(a, b, *, mode, name, out_dtype=F32, resid=None, alpha=1.0, b_off=0, nsh=None, comm=None):
    a_parts = a if isinstance(a, (list, tuple)) else [a]
    b_parts = b if isinstance(b, (list, tuple)) else [b]
    n_parts = max(len(a_parts), len(b_parts))
    if mode == "nn":
        m, kdim = a.shape
        _, _, ns = b.shape
        bn = _pick(ns, (1024, 1408, 512, 256, 128))
        bm, bk = _fit_bm_bk(m, (1024, 512, 256, 128), kdim, bn, out_dtype, resid is not None)
        nps = ns // bn
        grid = (m // bm, nsh * nps, kdim // bk)
        a_specs = [pl.BlockSpec((bm, bk), lambda i, j, k: (i, k))]
        b_specs = [pl.BlockSpec((None, bk, bn), lambda i, j, k: (b_off + j // nps, k, j % nps))]
        o_spec = pl.BlockSpec((bm, bn), lambda i, j, k: (i, j))
        out_shape = jax.ShapeDtypeStruct((m, nsh * ns), out_dtype)
        dims, part_axis, per_part = NN, 2, grid[2]
    elif mode == "nt":
        m = a_parts[0].shape[0]
        _, kdim, ns = b.shape
        wide = n_parts > 1
        bn = _pick(kdim, (1024, 1408, 512, 256, 128) if wide else (512, 256, 128))
        bm, bk = _fit_bm_bk(m, (1024, 512, 256, 128) if wide else (2048, 1024, 512, 256, 128), ns, bn, out_dtype,
                            resid is not None, na=n_parts)
        kps = ns // bk
        per_part = nsh // n_parts * kps
        grid = (m // bm, kdim // bn, nsh * kps)
        a_specs = [pl.BlockSpec((bm, bk), lambda i, j, k, p=p: (jnp.where(k // per_part == p, i, 0),
                                                                  jnp.where(k // per_part == p, k % per_part, 0)))
                   for p in range(n_parts)]
        b_specs = [pl.BlockSpec((None, bn, bk), lambda i, j, k: (b_off + k // kps, j, k % kps))]
        o_spec = pl.BlockSpec((bm, bn), lambda i, j, k: (i, j))
        out_shape = jax.ShapeDtypeStruct((m, kdim), out_dtype)
        dims, part_axis = NT, 2
    else:
        t, kdim = a.shape
        ns = b_parts[0].shape[1] * n_parts // nsh
        bn = _pick(ns, (1024, 1408, 512, 256, 128))
        bm, bk = _fit_bm_bk(kdim, (1024, 1408, 512, 256, 128), t, bn, out_dtype, resid is not None, nb=n_parts)
        nps = ns // bn
        per_part = nsh // n_parts * nps
        grid = (kdim // bm, nsh * nps, t // bk)
        a_specs = [pl.BlockSpec((bk, bm), lambda i, j, k: (k, i))]
        b_specs = [pl.BlockSpec((bk, bn), lambda i, j, k, p=p: (jnp.where(j // per_part == p, k, 0),
                                                                  jnp.where(j // per_part == p, j % per_part, 0)))
                   for p in range(n_parts)]
        o_spec = pl.BlockSpec((None, bm, bn), lambda i, j, k: (j // nps, i, j % nps))
        out_shape = jax.ShapeDtypeStruct((nsh, kdim, ns), out_dtype)
        dims, part_axis = TN, 1
    nk = grid[2]
    na, nb_ = len(a_parts), len(b_parts)
    has_resid = resid is not None

    def kern(*refs):
        a_refs, b_refs = refs[:na], refs[na:na + nb_]
        r_ref = refs[na + nb_] if has_resid else None
        k = pl.program_id(2)

        def finish(r, o_ref):
            if has_resid:
                r = r + alpha * r_ref[...]
            o_ref[...] = r.astype(o_ref.dtype)

        def add(a_ref, b_ref):
            if nk == 1:
                finish(lax.dot_general(a_ref[...], b_ref[...], dims, preferred_element_type=F32), refs[-1])
                return
            refs[-1][...] += lax.dot_general(a_ref[...], b_ref[...], dims, preferred_element_type=F32)

        if nk > 1:
            @pl.when(k == 0)
            def _():
                refs[-1][...] = jnp.zeros_like(refs[-1])

        if n_parts == 1:
            add(a_refs[0], b_refs[0])
        else:
            which = pl.program_id(part_axis) // per_part
            for p in range(n_parts):
                pl.when(which == p)(functools.partial(add, a_refs[min(p, na - 1)], b_refs[min(p, nb_ - 1)]))
        if nk > 1:
            @pl.when(k == nk - 1)
            def _():
                finish(refs[-1][...], refs[-2])

    in_specs = a_specs + b_specs
    operands = list(a_parts) + list(b_parts)
    if has_resid:
        in_specs.append(pl.BlockSpec((bm, bn), lambda i, j, k: (i, j)))
        operands.append(resid)
    outs, carried = _carried_call(
        kern, comm, name=name, grid=grid, in_specs=in_specs, out_specs=[o_spec], out_shape=[out_shape],
        operands=operands, scratch_shapes=[pltpu.VMEM((bm, bn), F32)] if nk > 1 else [],
        sem=("parallel", "parallel", "arbitrary"))
    return outs[0] if comm is None else (outs[0], carried)


def _ln_fwd(xin, g, b, name):
    t, d = xin.shape
    tb = _pick(t, (256, 128, 64, 32, 16))

    def kern(x_ref, g_ref, b_ref, h_ref, hb_ref):
        xhat, _ = _ln_hat(x_ref[...])
        h = xhat * g_ref[...] + b_ref[...]
        h_ref[...] = h
        hb_ref[...] = h.astype(BF16)

    row = pl.BlockSpec((tb, d), lambda i: (i, 0))
    vec = pl.BlockSpec((1, d), lambda i: (0, 0))
    return pl.pallas_call(
        kern, name=name, grid=(t // tb,), in_specs=[row, vec, vec], out_specs=[row, row],
        out_shape=[jax.ShapeDtypeStruct((t, d), F32), jax.ShapeDtypeStruct((t, d), BF16)],
        compiler_params=_params(("parallel",)),
    )(xin, g, b)


def _ln_bwd(xin, dy_or_target, g, b, name, loss_head=False):
    t, d = xin.shape
    tb = _pick(t, (256, 128, 64, 32, 16))
    nb = t // tb

    def kern(x_ref, dy_ref, g_ref, b_ref, dx_ref, dxb_ref, dg_ref, db_ref, *rest):
        i = pl.program_id(0)
        xhat, rstd = _ln_hat(x_ref[...])
        gv = g_ref[...]
        if loss_head:
            loss_ref = rest[0]
            err = xhat * gv + b_ref[...] - dy_ref[...]
            dy = err * (1.0 / d)
            part = 0.5 * jnp.sum(jnp.sum(err * err, axis=1, keepdims=True), axis=0, keepdims=True) * (1.0 / d)
        else:
            dy = dy_ref[...]

        @pl.when(i == 0)
        def _():
            dg_ref[...] = jnp.zeros_like(dg_ref)
            db_ref[...] = jnp.zeros_like(db_ref)
            if loss_head:
                loss_ref[...] = jnp.zeros_like(loss_ref)

        dg_ref[...] += jnp.sum(dy * xhat, axis=0, keepdims=True)
        db_ref[...] += jnp.sum(dy, axis=0, keepdims=True)
        if loss_head:
            loss_ref[...] += jnp.broadcast_to(part, loss_ref.shape)
        dxh = dy * gv
        m1 = jnp.mean(dxh, axis=-1, keepdims=True)
        m2 = jnp.mean(dxh * xhat, axis=-1, keepdims=True)
        dx = rstd * (dxh - m1 - xhat * m2)
        dx_ref[...] = dx
        dxb_ref[...] = dx.astype(BF16)

    row = pl.BlockSpec((tb, d), lambda i: (i, 0))
    vec = pl.BlockSpec((1, d), lambda i: (0, 0))
    out_specs = [row, row, vec, vec]
    out_shape = [jax.ShapeDtypeStruct((t, d), F32), jax.ShapeDtypeStruct((t, d), BF16),
                 jax.ShapeDtypeStruct((1, d), F32), jax.ShapeDtypeStruct((1, d), F32)]
    if loss_head:
        out_specs.append(pl.BlockSpec((1, LANES), lambda i: (0, 0)))
        out_shape.append(jax.ShapeDtypeStruct((1, LANES), F32))
    return pl.pallas_call(
        kern, name=name, grid=(nb,), in_specs=[row, row, vec, vec], out_specs=out_specs, out_shape=out_shape,
        compiler_params=_params(("arbitrary",)),
    )(xin, dy_or_target, g, b)


def _conv_gate_fwd(u, conv_w, conv_b, name, comm=None):
    t, f2 = u.shape
    f = f2 // 2
    tb = _pick(t, (512, 256, 128, 64, 32, 16))
    cn = _pick(f, (1408, 1024, 512, 256, 128))
    ncb = f // cn
    hb = tb // 8

    def kern(a_ref, ah_ref, b_ref, w_ref, cb_ref, o_ref):
        i = pl.program_id(0)
        a = a_ref[...]
        halo = jnp.where(i > 0, ah_ref[...], 0.0)
        rid = lax.broadcasted_iota(jnp.int32, a.shape, 0)
        s1 = jnp.where(rid == 0, halo[7:8, :], pltpu.roll(a, 1, 0))
        s2 = jnp.where(rid == 0, halo[6:7, :], jnp.where(rid == 1, halo[7:8, :], pltpu.roll(a, 2, 0)))
        w = w_ref[...]
        conv = w[2:3, :] * a + w[1:2, :] * s1 + w[0:1, :] * s2 + cb_ref[...]
        sp, _ = _sigmoid_gate(conv)
        o_ref[...] = (conv * sp * b_ref[...]).astype(BF16)

    outs, carried = _carried_call(
        kern, comm, name=name, grid=(t // tb, ncb),
        in_specs=[pl.BlockSpec((tb, cn), lambda i, j: (i, j)),
                  pl.BlockSpec((8, cn), lambda i, j: (jnp.maximum(i * hb - 1, 0), j)),
                  pl.BlockSpec((tb, cn), lambda i, j: (i, j + ncb)),
                  pl.BlockSpec((3, cn), lambda i, j: (0, j)),
                  pl.BlockSpec((1, cn), lambda i, j: (0, j))],
        out_specs=[pl.BlockSpec((tb, cn), lambda i, j: (i, j))],
        out_shape=[jax.ShapeDtypeStruct((t, f), BF16)],
        operands=[u, u, u, conv_w, conv_b], sem=("parallel", "parallel"))
    return outs[0] if comm is None else (outs[0], carried)


def _conv_gate_bwd(u, dgact, conv_w, conv_b, name):
    t, f2 = u.shape
    f = f2 // 2
    tb = _pick(t, (512, 256, 128, 64, 32, 16))
    cn = _pick(f, (1408, 1024, 512, 256, 128))
    ncb = f // cn
    hb = tb // 8
    nb = t // tb
    last8 = t // 8 - 1

    def kern(a_ref, ap_ref, an_ref, b_ref, bn_ref, dg_ref, dgn_ref, w_ref, cb_ref,
             da_ref, db_ref, dw_ref, dcb_ref):
        i = pl.program_id(1)
        a = a_ref[...]
        w = w_ref[...]
        ext = jnp.concatenate([jnp.where(i > 0, ap_ref[...], 0.0), a, an_ref[...]], axis=0)
        e1 = pltpu.roll(ext, 1, 0)
        e2 = pltpu.roll(ext, 2, 0)
        conv = (w[2:3, :] * ext + w[1:2, :] * e1 + w[0:1, :] * e2 + cb_ref[...])[8:, :]
        bmn = jnp.concatenate([b_ref[...], bn_ref[...]], axis=0)
        dgmn = jnp.concatenate([dg_ref[...], jnp.where(i < nb - 1, dgn_ref[...], 0.0)], axis=0)
        sp, sn = _sigmoid_gate(conv)
        da = dgmn * bmn * (sp * (1.0 + conv * sn))
        n = tb + 8
        dap = w[2:3, :] * da + w[1:2, :] * pltpu.roll(da, n - 1, 0) + w[0:1, :] * pltpu.roll(da, n - 2, 0)
        da_ref[...] = dap[:tb, :].astype(BF16)
        db_ref[...] = (dg_ref[...] * (conv * sp)[:tb, :]).astype(BF16)
        dam = da[:tb, :]

        @pl.when(i == 0)
        def _():
            dw_ref[...] = jnp.zeros_like(dw_ref)
            dcb_ref[...] = jnp.zeros_like(dcb_ref)

        dw = jnp.concatenate([jnp.sum(dam * e2[8:8 + tb, :], axis=0, keepdims=True),
                              jnp.sum(dam * e1[8:8 + tb, :], axis=0, keepdims=True),
                              jnp.sum(dam * a, axis=0, keepdims=True)], axis=0)
        dw_ref[...] += dw
        dcb_ref[...] += jnp.sum(dam, axis=0, keepdims=True)

    main_a = pl.BlockSpec((tb, cn), lambda j, i: (i, j))
    prev_a = pl.BlockSpec((8, cn), lambda j, i: (jnp.maximum(i * hb - 1, 0), j))
    next_a = pl.BlockSpec((8, cn), lambda j, i: (jnp.minimum((i + 1) * hb, last8), j))
    main_b = pl.BlockSpec((tb, cn), lambda j, i: (i, j + ncb))
    next_b = pl.BlockSpec((8, cn), lambda j, i: (jnp.minimum((i + 1) * hb, last8), j + ncb))
    return pl.pallas_call(
        kern, name=name, grid=(ncb, nb),
        in_specs=[main_a, prev_a, next_a, main_b, next_b, main_a, next_a,
                  pl.BlockSpec((3, cn), lambda j, i: (0, j)), pl.BlockSpec((1, cn), lambda j, i: (0, j))],
        out_specs=[main_a, main_a, pl.BlockSpec((3, cn), lambda j, i: (0, j)),
                   pl.BlockSpec((1, cn), lambda j, i: (0, j))],
        out_shape=[jax.ShapeDtypeStruct((t, f), BF16), jax.ShapeDtypeStruct((t, f), BF16),
                   jax.ShapeDtypeStruct((3, f), F32), jax.ShapeDtypeStruct((1, f), F32)],
        compiler_params=_params(("parallel", "arbitrary")),
    )(u, u, u, u, u, dgact, dgact, conv_w, conv_b)


def _hg_gates(qp, fp, lb):
    sq, _ = _sigmoid_gate(qp)
    sf, snf = _sigmoid_pair(fp)
    forget = lb + (1.0 - lb) * sf
    return sq, sf, snf, forget, jnp.log(forget), (1.0 - lb) * snf


def _tri(lower):
    r = lax.broadcasted_iota(jnp.int32, (SUB, SUB), 0)
    c = lax.broadcasted_iota(jnp.int32, (SUB, SUB), 1)
    return ((r >= c) if lower else (r <= c)).astype(BF16)


def _split2(x):
    hi = x.astype(BF16)
    return hi, (x - hi.astype(F32)).astype(BF16)


def _dot3(a, b, dims):
    (ah, al), (bh, bl) = a, b
    return (lax.dot_general(ah, bh, dims, preferred_element_type=F32)
            + (lax.dot_general(ah, bl, dims, preferred_element_type=F32)
               + lax.dot_general(al, bh, dims, preferred_element_type=F32)))


def _running_sum(tri, x):
    hi, lo = _split2(x)
    rest = (x - hi.astype(F32)) - lo.astype(F32)
    return (lax.dot_general(tri, hi, NN, preferred_element_type=F32)
            + (lax.dot_general(tri, lo, NN, preferred_element_type=F32)
               + lax.dot_general(tri, rest.astype(BF16), NN, preferred_element_type=F32)))


HEADS_PER_STEP = 8
STEP_UNROLL = 2


def _hgrn2_fwd(proj, lb_logits, norm_g, name, comm=None):
    t, d4 = proj.shape
    d = d4 // 4
    nh = d // LANES
    hb = _pick(nh, (HEADS_PER_STEP, 2, 1))
    wb = hb * LANES
    tb = _pick(t, (256, 128, 64, 32, 16))
    nb = t // tb
    nsc = tb // SUB

    def kern(q_ref, f_ref, i_ref, g_ref, lbl_ref, ng_ref, y_ref, o_ref, st_ref, s_ref):
        @pl.when(pl.program_id(1) == 0)
        def _():
            s_ref[...] = jnp.zeros_like(s_ref)

        lb_all = _lower_bound(lbl_ref[...])
        ng_all = ng_ref[...]
        ltri = _tri(True)
        rcol = lax.broadcasted_iota(jnp.int32, (SUB, 1), 0)

        heads = [slice(h * LANES, (h + 1) * LANES) for h in range(hb)]

        def step(sc, carry):
            rows = pl.ds(pl.multiple_of(sc * SUB, SUB), SUB)
            qp, fp, v, gp = q_ref[rows, :], f_ref[rows, :], i_ref[rows, :], g_ref[rows, :]
            sq, _, _, _, lf, k = _hg_gates(qp, fp, lb_all)
            q = qp * sq
            bl = _running_sum(ltri, lf)
            bend = bl[SUB - 1:SUB, :]
            dec = jnp.exp(bend)
            qs2 = _split2(q * jnp.exp(bl))
            kd2 = _split2(k * jnp.exp(bend - bl))
            v2 = _split2(v)
            states = [s_ref[h] for h in range(hb)]
            o = [_dot3((qs2[0][:, c], qs2[1][:, c]), _split2(states[h]), NT) for h, c in enumerate(heads)]
            top, bot = [oh[:TILE] for oh in o], [oh[TILE:] for oh in o]
            for s in range(SUB):
                lo = 0 if s < TILE else TILE
                e = jnp.exp(jnp.minimum(bl[lo:] - bl[s:s + 1, :], 0.0))
                p = q[lo:] * e * k[s:s + 1, :]
                for h, c in enumerate(heads):
                    a = jnp.sum(p[:, c], axis=1, keepdims=True)
                    add = jnp.where(rcol[lo:] >= s, a, 0.0) * v[s:s + 1, c]
                    if lo == 0:
                        top[h], bot[h] = top[h] + add[:TILE], bot[h] + add[TILE:]
                    else:
                        bot[h] = bot[h] + add
            o = [jnp.concatenate([a, b], axis=0) for a, b in zip(top, bot)]
            for h, c in enumerate(heads):
                st_ref[sc, h] = states[h]
                s_ref[h] = states[h] * dec[:, c] + _dot3((v2[0][:, c], v2[1][:, c]), (kd2[0][:, c], kd2[1][:, c]), TN)
            o_ref[rows, :] = jnp.concatenate(o, axis=1)
            on = jnp.concatenate(
                [oh * lax.rsqrt(jnp.mean(oh * oh, axis=1, keepdims=True) + RMS_EPS) for oh in o], axis=1)
            sg, _ = _sigmoid_gate(gp)
            y_ref[rows, :] = (on * ng_all * (gp * sg)).astype(BF16)
            return carry

        lax.fori_loop(0, nsc, step, 0, unroll=STEP_UNROLL)

    def col(off):
        return pl.BlockSpec((tb, wb), lambda h, j: (j, h + off * (nh // hb)))

    return _carried_call(
        kern, comm, name=name, grid=(nh // hb, nb),
        in_specs=[col(0), col(1), col(2), col(3),
                  pl.BlockSpec((3, wb), lambda h, j: (0, h)), pl.BlockSpec((1, wb), lambda h, j: (0, h))],
        out_specs=[col(0), col(0), pl.BlockSpec((nsc, hb, LANES, LANES), lambda h, j: (j, h, 0, 0))],
        out_shape=[jax.ShapeDtypeStruct((t, d), BF16), jax.ShapeDtypeStruct((t, d), F32),
                   jax.ShapeDtypeStruct((t // SUB, nh, LANES, LANES), F32)],
        scratch_shapes=[pltpu.VMEM((hb, LANES, LANES), F32)],
        operands=[proj, proj, proj, proj, lb_logits, norm_g], sem=("parallel", "arbitrary"))


def _hgrn2_bwd(proj, lb_logits, norm_g, o_raw, states, dy, name, comm=None):
    t, d4 = proj.shape
    d = d4 // 4
    nh = d // LANES
    hb = _pick(nh, (HEADS_PER_STEP, 2, 1))
    wb = hb * LANES
    tb = _pick(t, (256, 128, 64, 32, 16))
    nb = t // tb
    nsc = tb // SUB

    def kern(q_ref, f_ref, i_ref, g_ref, lbl_ref, ng_ref, o_ref, st_ref, dy_ref,
             dq_ref, df_ref, di_ref, dgp_ref, dlb_ref, dng_ref, ds_ref, gc_ref):
        j = pl.program_id(1)

        @pl.when(j == 0)
        def _():
            ds_ref[...] = jnp.zeros_like(ds_ref)
            gc_ref[...] = jnp.zeros_like(gc_ref)
            dlb_ref[...] = jnp.zeros_like(dlb_ref)
            dng_ref[...] = jnp.zeros_like(dng_ref)

        lb_all = _lower_bound(lbl_ref[...])
        ng_all = ng_ref[...]
        ltri, utri = _tri(True), _tri(False)
        rcol = lax.broadcasted_iota(jnp.int32, (SUB, 1), 0)
        rid = lax.broadcasted_iota(jnp.int32, (SUB, wb), 0)

        heads = [slice(h * LANES, (h + 1) * LANES) for h in range(hb)]

        def per_head(fn, n=SUB):
            return jnp.concatenate([jnp.broadcast_to(fn(c), (n, LANES)) for c in heads], axis=1)

        def step(it, carry):
            sc = nsc - 1 - it
            rows = pl.ds(pl.multiple_of(sc * SUB, SUB), SUB)
            qp, fp, v, gp = q_ref[rows, :], f_ref[rows, :], i_ref[rows, :], g_ref[rows, :]
            o, dyv = o_ref[rows, :], dy_ref[rows, :]
            sq, sf, snf, forget, lf, k = _hg_gates(qp, fp, lb_all)
            q = qp * sq
            bl = _running_sum(ltri, lf)
            ebl = jnp.exp(bl)
            bend = bl[SUB - 1:SUB, :]
            dec = jnp.exp(bend)
            dte = jnp.exp(bend - bl)
            r = per_head(lambda c: lax.rsqrt(jnp.mean(o[:, c] * o[:, c], axis=1, keepdims=True) + RMS_EPS))
            ohat = o * r
            sg, sng = _sigmoid_gate(gp)
            don = dyv * (gp * sg)
            dgp_ref[rows, :] = (dyv * (ohat * ng_all) * (sg * (1.0 + gp * sng))).astype(BF16)
            dng_ref[...] += jnp.sum(don * ohat, axis=0, keepdims=True)
            doh = don * ng_all
            dot_oh = doh * ohat
            do = r * (doh - ohat * per_head(lambda c: jnp.mean(dot_oh[:, c], axis=1, keepdims=True)))
            do2, qs2, kd2, v2 = _split2(do), _split2(q * ebl), _split2(k * dte), _split2(v)
            dq_h, dk_h, dv_h = [], [], []
            for h, c in enumerate(heads):
                dstate = ds_ref[h]
                ds2 = _split2(dstate)
                doc = (do2[0][:, c], do2[1][:, c])
                dq_h.append(_dot3(doc, _split2(st_ref[sc, h]), NN))
                dv_h.append(_dot3((kd2[0][:, c], kd2[1][:, c]), ds2, NT))
                dk_h.append(_dot3((v2[0][:, c], v2[1][:, c]), ds2, NN))
                ds_ref[h] = dstate * dec[:, c] + _dot3(doc, (qs2[0][:, c], qs2[1][:, c]), TN)
            dq = jnp.concatenate(dq_h, axis=1) * ebl
            dk = jnp.concatenate(dk_h, axis=1) * dte
            dv = jnp.concatenate(dv_h, axis=1)
            dq_t, dq_b = dq[:TILE], dq[TILE:]
            dk_i = [jnp.zeros((TILE, wb), F32), jnp.zeros((TILE, wb), F32)]
            dv_i = [jnp.zeros((TILE, wb), F32), jnp.zeros((TILE, wb), F32)]
            for s in range(SUB):
                lo = 0 if s < TILE else TILE
                n = SUB - lo
                e = jnp.exp(jnp.minimum(bl[lo:] - bl[s:s + 1, :], 0.0))
                qe = q[lo:] * e
                ks = k[s:s + 1, :]
                live = rcol[lo:] >= s
                pk = qe * ks
                dor = do[lo:]
                pv = dor * v[s:s + 1, :]
                a = per_head(lambda c: jnp.where(live, jnp.sum(pk[:, c], axis=1, keepdims=True), 0.0), n)
                da = per_head(lambda c: jnp.where(live, jnp.sum(pv[:, c], axis=1, keepdims=True), 0.0), n)
                ddq = da * (e * ks)
                if lo == 0:
                    dq_t, dq_b = dq_t + ddq[:TILE], dq_b + ddq[TILE:]
                else:
                    dq_b = dq_b + ddq
                here = rid[:TILE] == s - lo
                dk_i[lo // TILE] = jnp.where(here, jnp.sum(da * qe, axis=0, keepdims=True), dk_i[lo // TILE])
                dv_i[lo // TILE] = jnp.where(here, jnp.sum(a * dor, axis=0, keepdims=True), dv_i[lo // TILE])
            dq = jnp.concatenate([dq_t, dq_b], axis=0)
            dk = dk + jnp.concatenate(dk_i, axis=0)
            dv = dv + jnp.concatenate(dv_i, axis=0)
            w = q * dq - k * dk
            gc = gc_ref[...]
            dlf = _running_sum(utri, w) + gc
            gc_ref[...] = gc + jnp.sum(w, axis=0, keepdims=True)
            t1 = dlf / forget - dk
            df_ref[rows, :] = ((1.0 - lb_all) * sf * snf * t1).astype(BF16)
            dlb_ref[...] += jnp.sum(snf * t1, axis=0, keepdims=True)
            dq_ref[rows, :] = (dq * (sq * (1.0 + qp * (1.0 - sq)))).astype(BF16)
            di_ref[rows, :] = dv.astype(BF16)
            return carry

        lax.fori_loop(0, nsc, step, 0, unroll=STEP_UNROLL)

    def col(off):
        return pl.BlockSpec((tb, wb), lambda h, j: (nb - 1 - j, h + off * (nh // hb)))

    vec = pl.BlockSpec((1, wb), lambda h, j: (0, h))
    return _carried_call(
        kern, comm, name=name, grid=(nh // hb, nb),
        in_specs=[col(0), col(1), col(2), col(3), pl.BlockSpec((3, wb), lambda h, j: (0, h)), vec,
                  col(0), pl.BlockSpec((nsc, hb, LANES, LANES), lambda h, j: (nb - 1 - j, h, 0, 0)), col(0)],
        out_specs=[col(0), col(0), col(0), col(0), vec, vec],
        out_shape=[jax.ShapeDtypeStruct((t, d), BF16)] * 4 + [jax.ShapeDtypeStruct((1, d), F32)] * 2,
        scratch_shapes=[pltpu.VMEM((hb, LANES, LANES), F32), pltpu.VMEM((1, wb), F32)],
        operands=[proj, proj, proj, proj, lb_logits, norm_g, o_raw, states, dy], sem=("parallel", "arbitrary"))


_INV_SQRT2 = 0.7071067811865476
_INV_SQRT2PI = 0.3989422804014327


def _gelu(x):
    return 0.5 * x * (1.0 + lax.erf(x * _INV_SQRT2))


def _gelu_grad(x):
    return 0.5 * (1.0 + lax.erf(x * _INV_SQRT2)) + x * jnp.exp(-0.5 * x * x) * _INV_SQRT2PI


def _causal(w):
    r = lax.broadcasted_iota(jnp.int32, (GCHUNK, GCHUNK), 0)
    c = lax.broadcasted_iota(jnp.int32, (GCHUNK, GCHUNK), 1)
    return jnp.where(r >= c, w, 0.0)


def _sg_gate_fwd(pre, ln_g, ln_b, w_s, b_s_t, name):
    t, d2 = pre.shape
    d = d2 // 2
    ng = d // LANES

    def kern(pre_ref, g_ref, b_ref, ws_ref, bs_ref, y_ref):
        z = _gelu(pre_ref[...])
        u = z[:, :d]
        vhat, _ = _ln_hat(z[:, d:])
        vn = (vhat * g_ref[...] + b_ref[...]).astype(BF16)
        bs = bs_ref[...]
        for g in range(ng):
            cols = slice(g * LANES, (g + 1) * LANES)
            wc = _causal(ws_ref[g]).astype(BF16)
            gate = jnp.dot(wc, vn[:, cols], preferred_element_type=F32) + bs[:, g:g + 1]
            y_ref[:, cols] = (u[:, cols] * gate).astype(BF16)

    vec = pl.BlockSpec((1, d), lambda i: (0, 0))
    return pl.pallas_call(
        kern, name=name, grid=(t // GCHUNK,),
        in_specs=[pl.BlockSpec((GCHUNK, d2), lambda i: (i, 0)), vec, vec,
                  pl.BlockSpec((ng, GCHUNK, GCHUNK), lambda i: (0, 0, 0)),
                  pl.BlockSpec((GCHUNK, ng), lambda i: (0, 0))],
        out_specs=pl.BlockSpec((GCHUNK, d), lambda i: (i, 0)),
        out_shape=jax.ShapeDtypeStruct((t, d), BF16),
        compiler_params=_params(("parallel",)),
    )(pre, ln_g, ln_b, w_s, b_s_t)


def _sg_gate_bwd(pre, dy, ln_g, ln_b, w_s, b_s_t, name):
    t, d2 = pre.shape
    d = d2 // 2
    ng = d // LANES

    def kern(pre_ref, dy_ref, g_ref, b_ref, ws_ref, bs_ref, dpre_ref, dws_ref, dbs_ref, dg_ref, db_ref, dvn_ref):
        @pl.when(pl.program_id(0) == 0)
        def _():
            dws_ref[...] = jnp.zeros_like(dws_ref)
            dbs_ref[...] = jnp.zeros_like(dbs_ref)
            dg_ref[...] = jnp.zeros_like(dg_ref)
            db_ref[...] = jnp.zeros_like(db_ref)

        pre = pre_ref[...]
        z = _gelu(pre)
        u = z[:, :d]
        vhat, rstd = _ln_hat(z[:, d:])
        gv = g_ref[...]
        vn = (vhat * gv + b_ref[...]).astype(BF16)
        bs = bs_ref[...]
        dyv = dy_ref[...]
        gp = _gelu_grad(pre)
        lane = lax.broadcasted_iota(jnp.int32, (GCHUNK, ng), 1)
        dbs = jnp.zeros((GCHUNK, ng), F32)
        for g in range(ng):
            cols = slice(g * LANES, (g + 1) * LANES)
            wc = _causal(ws_ref[g]).astype(BF16)
            vng = vn[:, cols]
            gate = jnp.dot(wc, vng, preferred_element_type=F32) + bs[:, g:g + 1]
            dpre_ref[:, cols] = (dyv[:, cols] * gate * gp[:, cols]).astype(BF16)
            dgate = dyv[:, cols] * u[:, cols]
            dbs = dbs + jnp.where(lane == g, jnp.sum(dgate, axis=1, keepdims=True), 0.0)
            dgb = dgate.astype(BF16)
            dws_ref[g] += _causal(lax.dot_general(dgb, vng, NT, preferred_element_type=F32))
            dvn_ref[:, cols] = lax.dot_general(wc, dgb, TN, preferred_element_type=F32)
        dbs_ref[...] += dbs
        dvn = dvn_ref[...]
        dg_ref[...] += jnp.sum(dvn * vhat, axis=0, keepdims=True)
        db_ref[...] += jnp.sum(dvn, axis=0, keepdims=True)
        dvh = dvn * gv
        m1 = jnp.mean(dvh, axis=-1, keepdims=True)
        m2 = jnp.mean(dvh * vhat, axis=-1, keepdims=True)
        dpre_ref[:, d:] = (rstd * (dvh - m1 - vhat * m2) * gp[:, d:]).astype(BF16)

    vec = pl.BlockSpec((1, d), lambda i: (0, 0))
    wsp = pl.BlockSpec((ng, GCHUNK, GCHUNK), lambda i: (0, 0, 0))
    bsp = pl.BlockSpec((GCHUNK, ng), lambda i: (0, 0))
    return pl.pallas_call(
        kern, name=name, grid=(t // GCHUNK,),
        in_specs=[pl.BlockSpec((GCHUNK, d2), lambda i: (i, 0)), pl.BlockSpec((GCHUNK, d), lambda i: (i, 0)),
                  vec, vec, wsp, bsp],
        out_specs=[pl.BlockSpec((GCHUNK, d2), lambda i: (i, 0)), wsp, bsp, vec, vec],
        out_shape=[jax.ShapeDtypeStruct((t, d2), BF16), jax.ShapeDtypeStruct((ng, GCHUNK, GCHUNK), F32),
                   jax.ShapeDtypeStruct((GCHUNK, ng), F32), jax.ShapeDtypeStruct((1, d), F32),
                   jax.ShapeDtypeStruct((1, d), F32)],
        scratch_shapes=[pltpu.VMEM((GCHUNK, d), F32)],
        compiler_params=_params(("arbitrary",)),
    )(pre, dy, ln_g, ln_b, w_s, b_s_t)


def _adamw_math(w, g, m, v):
    m = ADAM_B1 * m + (1.0 - ADAM_B1) * g
    v = ADAM_B2 * v + (1.0 - ADAM_B2) * (g * g)
    m_hat = m / (1.0 - ADAM_B1 ** ADAM_STEP)
    v_hat = v / (1.0 - ADAM_B2 ** ADAM_STEP)
    return -ADAM_LR * (m_hat / (jnp.sqrt(v_hat) + ADAM_EPS) + ADAM_WD * w), m, v


ADAMW_BLOCK_BYTES = 3 << 19


def _adamw(w, gs, m, v, name, comm=None):
    nl, r, c = w.shape
    rb = _pick(r, tuple(p for p in (512, 256, 128, 64, 32, 16, 8) if p * c * 4 <= ADAMW_BLOCK_BYTES))

    def kern(w_ref, m_ref, v_ref, *rest):
        g_refs, (d_ref, mo_ref, vo_ref, go_ref) = rest[:nl], rest[nl:]
        layer = pl.program_id(0)
        g = g_refs[0][...]
        for k in range(1, nl):
            g = jnp.where(layer == k, g_refs[k][...], g)
        dlt, mm, vv = _adamw_math(w_ref[...], g, m_ref[...], v_ref[...])
        d_ref[...] = dlt
        mo_ref[...] = mm
        vo_ref[...] = vv
        go_ref[...] = g

    blk = pl.BlockSpec((None, rb, c), lambda l, i: (l, i, 0))
    g_specs = [pl.BlockSpec((rb, c), lambda l, i, k=k: (jnp.where(l == k, i, 0), 0)) for k in range(nl)]
    outs, carried = _carried_call(
        kern, comm, name=name, grid=(nl, r // rb), in_specs=[blk] * 3 + g_specs, out_specs=[blk] * 4,
        out_shape=[jax.ShapeDtypeStruct((nl, r, c), F32)] * 4, operands=[w, m, v, *gs], sem=("parallel", "parallel"))
    return outs if comm is None else (outs, carried)


CAST_BLOCK_BYTES = 1 << 21


def _cast_bf16(items, name, comm=None):
    metas, start = [], 0
    for arr, _ in items:
        _, r, c = arr.shape
        rb = _pick(r, tuple(p for p in (1024, 512, 256, 128, 64, 32, 16) if p * c * 4 <= CAST_BLOCK_BYTES))
        metas.append((start, r // rb, rb, c))
        start += r // rb
    n = len(items)

    def spec(p, layer=None):
        s0, steps, rb, c = metas[p]
        if layer is None:
            return pl.BlockSpec((rb, c), lambda s: (jnp.clip(s - s0, 0, steps - 1), 0))
        return pl.BlockSpec((None, rb, c), lambda s: (layer, jnp.clip(s - s0, 0, steps - 1), 0))

    def kern(*refs):
        s = pl.program_id(0)
        for p in range(n):
            s0, steps, _, _ = metas[p]

            @pl.when(jnp.logical_and(s >= s0, s < s0 + steps))
            def _(p=p):
                refs[n + p][...] = refs[p][...].astype(BF16)

    outs, carried = _carried_call(
        kern, comm, name=name, grid=(start,), in_specs=[spec(p, layer) for p, (_, layer) in enumerate(items)],
        out_specs=[spec(p) for p in range(n)],
        out_shape=[jax.ShapeDtypeStruct(arr.shape[1:], BF16) for arr, _ in items],
        operands=[arr for arr, _ in items], sem=("arbitrary",))
    return outs, carried


def _comm_call(plan, name):
    ni, no = len(plan.ins), len(plan.out_shapes)

    def body(*refs):
        copies = plan.build(refs[:ni], refs[ni:ni + no], refs[-2], refs[-1], 0)
        for cp in copies:
            cp.start()
        for cp in copies:
            cp.wait()

    anyspec = pl.BlockSpec(memory_space=pl.ANY)
    return pl.pallas_call(
        body, name=name, in_specs=[anyspec] * ni, out_specs=[anyspec] * no, out_shape=plan.out_shapes,
        input_output_aliases=plan.aliases,
        scratch_shapes=[pltpu.SemaphoreType.DMA((plan.n_sems,)), pltpu.SemaphoreType.DMA((plan.n_sems,))],
        compiler_params=pltpu.CompilerParams(has_side_effects=True),
    )(*plan.ins)


def _lb_logits_grad(lb_logits, dlb, name):
    def kern(l_ref, d_ref, o_ref):
        lg = l_ref[...]
        m = jnp.max(lg, axis=0, keepdims=True)
        e = jnp.exp(lg - m)
        p = e / jnp.sum(e, axis=0, keepdims=True)
        row = lax.broadcasted_iota(jnp.int32, lg.shape, 0)
        o_ref[...] = d_ref[...] * p[0:1, :] * (jnp.where(row == 0, 1.0, 0.0) - p)

    return pl.pallas_call(kern, name=name, out_shape=jax.ShapeDtypeStruct(lb_logits.shape, F32))(lb_logits, dlb)


def _sum_devices(others, own, me, name):
    n, r, c = others.shape
    rb = _pick(r, (512, 256, 128, 64, 32, 16, 8))

    def kern(me_ref, a_ref, own_ref, o_ref):
        mine = own_ref[...]
        acc = jnp.where(me_ref[0] == 0, mine, a_ref[0])
        for i in range(1, n):
            acc = acc + jnp.where(me_ref[0] == i, mine, a_ref[i])
        o_ref[...] = acc

    return pl.pallas_call(
        kern, name=name,
        grid_spec=pltpu.PrefetchScalarGridSpec(
            num_scalar_prefetch=1, grid=(r // rb,),
            in_specs=[pl.BlockSpec((n, rb, c), lambda i, s: (0, i, 0)), pl.BlockSpec((rb, c), lambda i, s: (i, 0))],
            out_specs=pl.BlockSpec((rb, c), lambda i, s: (i, 0))),
        out_shape=jax.ShapeDtypeStruct((r, c), F32),
        compiler_params=_params(("parallel",)),
    )(me, others, own)


def _place():
    x, y, c = lax.axis_index("x"), lax.axis_index("y"), lax.axis_index("c")
    return x, y, c


class _Plan:
    def __init__(self, ins, out_shapes, aliases, n_sems, build):
        self.ins, self.out_shapes, self.aliases, self.n_sems, self.build = list(ins), list(out_shapes), aliases, n_sems, build


def _merge(*plans):
    ins, outs, aliases, subs, sems = [], [], {}, [], 0
    for p in plans:
        for k, v in p.aliases.items():
            aliases[len(ins) + k] = len(outs) + v
        subs.append((p, len(ins), len(outs), sems))
        ins += p.ins
        outs += p.out_shapes
        sems += p.n_sems

    def build(in_refs, out_refs, send_sems, recv_sems, base):
        copies = []
        for p, i0, o0, s0 in subs:
            copies += p.build(in_refs[i0:i0 + len(p.ins)], out_refs[o0:o0 + len(p.out_shapes)], send_sems, recv_sems,
                              base + s0)
        return copies

    return _Plan(ins, outs, aliases, sems, build)


def _remote(src, dst, send_sems, recv_sems, k, to):
    return pltpu.make_async_remote_copy(src_ref=src, dst_ref=dst, send_sem=send_sems.at[k], recv_sem=recv_sems.at[k],
                                        device_id=to, device_id_type=MESH)


def _plan_gather_ici(shards):
    n = len(shards)

    def build(ins, outs, send_sems, recv_sems, base):
        x, y, c = _place()
        me = 2 * x + y
        copies = []
        for a in range(n):
            h = ins[a].shape[0] // 2
            rows = pl.ds(c * h, h)
            for r in (1, 2, 3):
                px, py, _ = _chip_rel(x, y, r)
                copies.append(_remote(ins[a].at[rows, :], outs[a].at[me, rows, :], send_sems, recv_sems,
                                      base + 4 * a + r - 1, (px, py, c)))
            copies.append(_remote(ins[a], outs[a].at[me], send_sems, recv_sems, base + 4 * a + 3, (x, y, 1 - c)))
        return copies

    return _Plan(shards, [jax.ShapeDtypeStruct((N_CHIPS,) + s.shape, s.dtype) for s in shards], {}, 4 * n, build)


def _plan_gather_pass(gathered):
    n = len(gathered)

    def build(ins, outs, send_sems, recv_sems, base):
        x, y, c = _place()
        copies = []
        for a in range(n):
            h = outs[a].shape[1] // 2
            rows = pl.ds(c * h, h)
            for r in (1, 2, 3):
                _, _, shard = _chip_rel(x, y, r)
                piece = outs[a].at[shard, rows, :]
                copies.append(_remote(piece, piece, send_sems, recv_sems, base + 3 * a + r - 1, (x, y, 1 - c)))
        return copies

    return _Plan(gathered, [jax.ShapeDtypeStruct(g.shape, g.dtype) for g in gathered], {a: a for a in range(n)},
                 3 * n, build)


def _plan_swap(split):
    n = len(split)

    def build(ins, outs, send_sems, recv_sems, base):
        x, y, c = _place()
        return [_remote(ins[a].at[j, 1 - c], outs[a].at[j], send_sems, recv_sems, base + N_CHIPS * a + j, (x, y, 1 - c))
                for a in range(n) for j in range(N_CHIPS)]

    return _Plan(split, [jax.ShapeDtypeStruct((N_CHIPS,) + g.shape[2:], g.dtype) for g in split], {}, N_CHIPS * n, build)


def _plan_scatter(parts):
    n = len(parts)

    def build(ins, outs, send_sems, recv_sems, base):
        x, y, c = _place()
        copies = []
        for a in range(n):
            for r in (1, 2, 3):
                px, py, shard = _chip_rel(x, y, r)
                copies.append(_remote(ins[a].at[shard], outs[a].at[r - 1], send_sems, recv_sems, base + 3 * a + r - 1,
                                      (px, py, c)))
        return copies

    return _Plan(parts, [jax.ShapeDtypeStruct((3,) + p.shape[1:], p.dtype) for p in parts], {}, 3 * n, build)


def _plan_join(bufs):
    n = len(bufs)

    def build(ins, outs, send_sems, recv_sems, base):
        x, y, c = _place()
        return [_remote(outs[a].at[c], outs[a].at[c], send_sems, recv_sems, base + a, (x, y, 1 - c)) for a in range(n)]

    return _Plan(bufs, [jax.ShapeDtypeStruct(b.shape, b.dtype) for b in bufs], {a: a for a in range(n)}, n, build)


def _carried_call(kern, plan, *, name, grid, in_specs, out_specs, out_shape, operands, scratch_shapes=(),
                  aliases=None, sem=None):
    n_in, n_out, n_sc = len(operands), len(out_shape), len(scratch_shapes)
    aliases = dict(aliases or {})
    if plan is None:
        outs = pl.pallas_call(kern, name=name, grid=grid, in_specs=in_specs, out_specs=out_specs, out_shape=out_shape,
                              scratch_shapes=list(scratch_shapes), input_output_aliases=aliases,
                              compiler_params=_params(sem))(*operands)
        return list(outs), []
    ci, co = len(plan.ins), len(plan.out_shapes)
    for k, v in plan.aliases.items():
        aliases[n_in + k] = n_out + v
    steps = tuple(grid)

    def body(*refs):
        ins, cins = refs[:n_in], refs[n_in:n_in + ci]
        outs = refs[n_in + ci:n_in + ci + n_out]
        couts = refs[n_in + ci + n_out:n_in + ci + n_out + co]
        scratch = refs[n_in + ci + n_out + co:n_in + ci + n_out + co + n_sc]
        send_sems, recv_sems = refs[-2], refs[-1]
        first = functools.reduce(jnp.logical_and, [pl.program_id(a) == 0 for a in range(len(steps))])
        last = functools.reduce(jnp.logical_and, [pl.program_id(a) == steps[a] - 1 for a in range(len(steps))])

        @pl.when(first)
        def _():
            for cp in plan.build(cins, couts, send_sems, recv_sems, 0):
                cp.start()

        kern(*ins, *outs, *scratch)

        @pl.when(last)
        def _():
            for cp in plan.build(cins, couts, send_sems, recv_sems, 0):
                cp.wait()

    anyspec = pl.BlockSpec(memory_space=pl.ANY)
    outs = pl.pallas_call(
        body, name=name, grid=grid, in_specs=list(in_specs) + [anyspec] * ci,
        out_specs=list(out_specs) + [anyspec] * co, out_shape=list(out_shape) + plan.out_shapes,
        scratch_shapes=list(scratch_shapes) + [pltpu.SemaphoreType.DMA((plan.n_sems,)),
                                               pltpu.SemaphoreType.DMA((plan.n_sems,))],
        input_output_aliases=aliases,
        compiler_params=_params(("arbitrary",) * len(steps)),
    )(*operands, *plan.ins)
    return list(outs[:n_out]), list(outs[n_out:])


def _chip_rel(x, y, r):
    px = x if r < 2 else 1 - x
    py = y if r % 2 == 0 else 1 - y
    return px, py, 2 * px + py


def _allgather_whole(arr, name):
    def body(in_ref, out_ref, send_sems, recv_sems, loc_sem):
        x, y, c = _place()
        me = 2 * x + y
        local = pltpu.make_async_copy(in_ref, out_ref.at[me], loc_sem)
        local.start()
        sends = []
        for r in (1, 2, 3):
            px, py, _ = _chip_rel(x, y, r)
            sends.append(pltpu.make_async_remote_copy(
                src_ref=in_ref, dst_ref=out_ref.at[me], send_sem=send_sems.at[r - 1], recv_sem=recv_sems.at[r - 1],
                device_id=(px, py, c), device_id_type=MESH))
        for cp in sends:
            cp.start()
        for r in (1, 2, 3):
            px, py, shard = _chip_rel(x, y, r)
            pltpu.make_async_remote_copy(
                src_ref=in_ref, dst_ref=out_ref.at[shard], send_sem=send_sems.at[r - 1], recv_sem=recv_sems.at[r - 1],
                device_id=(px, py, c), device_id_type=MESH).wait_recv()
        for cp in sends:
            cp.wait_send()
        local.wait()

    anyspec = pl.BlockSpec(memory_space=pl.ANY)
    return pl.pallas_call(
        body, name=name, in_specs=[anyspec], out_specs=anyspec,
        out_shape=jax.ShapeDtypeStruct((N_CHIPS,) + arr.shape, arr.dtype),
        scratch_shapes=[pltpu.SemaphoreType.DMA((3,)), pltpu.SemaphoreType.DMA((3,)), pltpu.SemaphoreType.DMA],
        compiler_params=pltpu.CompilerParams(has_side_effects=True),
    )(arr)


def _plan_gather_all(buf):
    def build(ins, outs, send_sems, recv_sems, base):
        x, y, c = _place()
        me = 4 * x + 2 * y + c
        copies = []
        for r in range(1, N_DEV):
            px, py, _ = _chip_rel(x, y, r // 2)
            pc = c if r % 2 == 0 else 1 - c
            copies.append(_remote(ins[0], outs[0].at[me], send_sems, recv_sems, base + r - 1, (px, py, pc)))
        return copies

    return _Plan([buf, jnp.zeros((N_DEV,) + buf.shape, buf.dtype)],
                 [jax.ShapeDtypeStruct((N_DEV,) + buf.shape, buf.dtype)], {1: 0}, N_DEV - 1, build)


def _add_half(grad, recv, sel, name):
    _, _, rh, cw = grad.shape
    rb = _pick(rh, (512, 256, 176, 128, 64, 32, 16, 8))

    def kern(sel_ref, g_ref, r_ref, o_ref):
        o_ref[...] = (g_ref[...] + r_ref[...]).astype(BF16)

    return pl.pallas_call(
        kern, name=name,
        grid_spec=pltpu.PrefetchScalarGridSpec(
            num_scalar_prefetch=1, grid=(N_CHIPS - 1, rh // rb),
            in_specs=[pl.BlockSpec((None, None, rb, cw), lambda j, i, s: (s[2 + j], s[0], i, 0)),
                      pl.BlockSpec((None, rb, cw), lambda j, i, s: (s[2 + j], i, 0))],
            out_specs=pl.BlockSpec((None, rb, cw), lambda j, i, s: (s[2 + j], i, 0))),
        out_shape=jax.ShapeDtypeStruct((N_CHIPS, rh, cw), BF16),
        compiler_params=_params(("parallel", "parallel")),
    )(sel, grad, recv)


def _add_own(grad, recv, got, sel, name):
    _, _, rh, cw = grad.shape
    rb = _pick(rh, (512, 256, 176, 128, 64, 32, 16, 8))

    def kern(sel_ref, g_ref, r_ref, b_ref, o_ref):
        own = g_ref[...] + r_ref[...]
        o_ref[...] = ((own + b_ref[0].astype(F32)) + b_ref[1].astype(F32)) + b_ref[2].astype(F32)

    return pl.pallas_call(
        kern, name=name,
        grid_spec=pltpu.PrefetchScalarGridSpec(
            num_scalar_prefetch=1, grid=(rh // rb,),
            in_specs=[pl.BlockSpec((None, None, rb, cw), lambda i, s: (s[1], s[0], i, 0)),
                      pl.BlockSpec((None, rb, cw), lambda i, s: (s[1], i, 0)),
                      pl.BlockSpec((3, rb, cw), lambda i, s: (0, i, 0))],
            out_specs=pl.BlockSpec((None, rb, cw), lambda i, s: (s[0], i, 0))),
        out_shape=jax.ShapeDtypeStruct((2, rh, cw), F32),
        compiler_params=_params(("parallel",)),
    )(sel, grad, recv, got)


def _stacked(g):
    return g.reshape(1, g.shape[0] * g.shape[1], g.shape[2])


def _halves(g):
    g = g.reshape(N_CHIPS, g.shape[0] * g.shape[1] // N_CHIPS, g.shape[2])
    return g.reshape(N_CHIPS, 2, g.shape[1] // 2, g.shape[2])


def _whole(f):
    return f.reshape(f.shape[0] * f.shape[1], f.shape[2])


def _step(x3, tgt, sm, w, mom, var):
    x, y, c = _place()
    sel = jnp.stack([c, 2 * x + y] + [_chip_rel(x, y, r)[2] for r in (1, 2, 3)]).astype(jnp.int32)
    wg = {}
    x2 = x3[0]
    cast, landed = _cast_bf16(
        [(x3, 0), (w["hg_w_out"], 0), (w["sg_w_in"], 0), (w["sg_w_out"], 0), (w["ffn_w_up"], 0), (w["ffn_w_up"], 1),
         (w["ffn_w_down"], 0), (w["ffn_w_down"], 1)], "cast_shards",
        comm=_plan_gather_ici([w["hg_w_in"][0].astype(BF16)]))
    xb = cast[0]
    sh = dict(zip(("hg_out", "sg_in", "sg_out", "up0", "up1", "dn0", "dn1"), cast[1:]))
    wg["hg_in"] = _comm_call(_plan_gather_pass(landed), "gather_hg_in_pass")[0]
    proj, landed = _matmul(xb, wg["hg_in"], mode="nn", nsh=N_CHIPS, name="hg_in",
                           comm=_plan_gather_ici([sh["hg_out"], sh["sg_in"]]))
    (yhg, o_raw, states), got = _hgrn2_fwd(
        proj, sm["lb_logits"], sm["hg_norm_g"], "hgrn2_fwd",
        comm=_merge(_plan_gather_pass(landed), _plan_gather_ici([sh["up0"], sh["up1"]])))
    wg["hg_out"], wg["sg_in"], landed = got[0], got[1], got[2:]
    xin1, got = _matmul(yhg, _stacked(wg["hg_out"]), mode="nn", nsh=1, resid=x2, alpha=ALPHA, name="hg_out",
                        comm=_plan_gather_pass(landed))
    wg["up0"], wg["up1"] = got
    h1, h1b = _ln_fwd(xin1, sm["ln1_g"][0:1], sm["ln1_b"][0:1], "l0_ln1")
    u0, landed = _matmul(h1b, wg["up0"], mode="nn", nsh=N_CHIPS, name="l0_ffn_up",
                         comm=_plan_gather_ici([sh["dn0"], sh["sg_out"]]))
    gact0, got = _conv_gate_fwd(u0, sm["conv_w"][0], sm["conv_b"][0:1], "l0_ffn_gate", comm=_plan_gather_pass(landed))
    wg["dn0"], wg["sg_out"] = got
    xin2, landed = _matmul(gact0, _stacked(wg["dn0"]), mode="nn", nsh=1, resid=h1, alpha=ALPHA, name="l0_ffn_down",
                           comm=_plan_gather_ici([sh["dn1"]]))
    h2, h2b = _ln_fwd(xin2, sm["ln2_g"][0:1], sm["ln2_b"][0:1], "l0_ffn_ln")
    pre, got = _matmul(h2b, wg["sg_in"], mode="nn", nsh=N_CHIPS, name="sg_in", comm=_plan_gather_pass(landed))
    wg["dn1"] = got[0]
    ysg = _sg_gate_fwd(pre, sm["sg_ln_g"], sm["sg_ln_b"], sm["sg_w_s"], sm["sg_b_s_t"], "sg_gate")
    xin3 = _matmul(ysg, _stacked(wg["sg_out"]), mode="nn", nsh=1, resid=h2, alpha=ALPHA, name="sg_out")
    h3, h3b = _ln_fwd(xin3, sm["ln1_g"][1:2], sm["ln1_b"][1:2], "l1_ln1")
    u1 = _matmul(h3b, wg["up1"], mode="nn", nsh=N_CHIPS, name="l1_ffn_up")
    gact1 = _conv_gate_fwd(u1, sm["conv_w"][1], sm["conv_b"][1:2], "l1_ffn_gate")
    xin4 = _matmul(gact1, _stacked(wg["dn1"]), mode="nn", nsh=1, resid=h3, alpha=ALPHA, name="l1_ffn_down")

    gs, grad, split, recv, part = {}, {}, {}, {}, {}

    def swap_on(call, keys):
        for k in keys:
            split[k] = _halves(grad[k])
        out, got = call(_plan_swap([split[k] for k in keys]))
        for k, r in zip(keys, got):
            recv[k] = r
            part[k] = _add_half(split[k], r, sel, f"rs_addhalf_{k}")
        return out

    def ffn_bwd(u, gact, hb_in, dxin, dxin_b, w_up, w_down, layer, tag, up, down, waiting):
        dgact = _matmul(dxin_b, _stacked(w_down), mode="nt", nsh=1, name=f"{tag}_ddown")
        grad[down] = _matmul(gact, dxin_b, mode="tn", nsh=1, name=f"{tag}_wdown")
        da, db, dcw, dcb = _conv_gate_bwd(u, dgact, sm["conv_w"][layer], sm["conv_b"][layer:layer + 1], f"{tag}_dgate")
        grad[up] = swap_on(lambda plan: _matmul(hb_in, [da, db], mode="tn", nsh=N_CHIPS, name=f"{tag}_wup", comm=plan),
                           waiting + [down])
        dh = swap_on(lambda plan: _matmul([da, db], w_up, mode="nt", nsh=N_CHIPS, resid=dxin, alpha=ALPHA,
                                          name=f"{tag}_dup", comm=plan), [up])
        return dh, dcw, dcb

    dx, dxb, dg4, db4, loss = _ln_bwd(xin4, tgt, sm["ln2_g"][1:2], sm["ln2_b"][1:2], "l1_ln2_bwd", loss_head=True)
    dh3, dcw1, dcb1 = ffn_bwd(u1, gact1, h3b, dx, dxb, wg["up1"], wg["dn1"], 1, "l1_ffn", "up1", "dn1", [])
    dx, dxb, dg3, db3 = _ln_bwd(xin3, dh3, sm["ln1_g"][1:2], sm["ln1_b"][1:2], "l1_ln1_bwd")
    grad["sg_out"] = _matmul(ysg, dxb, mode="tn", nsh=1, name="sg_wout")
    dysg = swap_on(lambda plan: _matmul(dxb, _stacked(wg["sg_out"]), mode="nt", nsh=1, name="sg_dout", comm=plan),
                   ["sg_out"])
    dpre, gs["sg_w_s"], gs["sg_b_s_t"], gs["sg_ln_g"], gs["sg_ln_b"] = _sg_gate_bwd(
        pre, dysg, sm["sg_ln_g"], sm["sg_ln_b"], sm["sg_w_s"], sm["sg_b_s_t"], "sg_gate_bwd")
    grad["sg_in"] = _matmul(h2b, dpre, mode="tn", nsh=N_CHIPS, name="sg_win")
    dh2 = _matmul(dpre, wg["sg_in"], mode="nt", nsh=N_CHIPS, resid=dx, alpha=ALPHA, name="sg_din")
    dx, dxb, dg2, db2 = _ln_bwd(xin2, dh2, sm["ln2_g"][0:1], sm["ln2_b"][0:1], "l0_ln2_bwd")
    dh1, dcw0, dcb0 = ffn_bwd(u0, gact0, h1b, dx, dxb, wg["up0"], wg["dn0"], 0, "l0_ffn", "up0", "dn0", ["sg_in"])
    dx, dxb, dg1, db1 = _ln_bwd(xin1, dh1, sm["ln1_g"][0:1], sm["ln1_b"][0:1], "l0_ln1_bwd")
    grad["hg_out"] = _matmul(yhg, dxb, mode="tn", nsh=1, name="hg_wout")
    dyhg = swap_on(lambda plan: _matmul(dxb, _stacked(wg["hg_out"]), mode="nt", nsh=1, name="hg_dout", comm=plan),
                   ["hg_out"])
    early = ("dn1", "up1", "sg_out", "sg_in", "dn0", "up0", "hg_out")
    dparts, got = _hgrn2_bwd(proj, sm["lb_logits"], sm["hg_norm_g"], o_raw, states, dyhg, "hgrn2_bwd",
                             comm=_plan_scatter([part[k] for k in early]))
    gs["lb"], gs["hg_norm_g"] = dparts[4], dparts[5]
    gs["ln1_g"] = jnp.concatenate([dg1, dg3], axis=0)
    gs["ln1_b"] = jnp.concatenate([db1, db3], axis=0)
    gs["ln2_g"] = jnp.concatenate([dg2, dg4], axis=0)
    gs["ln2_b"] = jnp.concatenate([db2, db4], axis=0)
    gs["conv_w"] = jnp.stack([dcw0, dcw1], axis=0)
    gs["conv_b"] = jnp.concatenate([dcb0, dcb1], axis=0)
    packed, layout = _pack(gs)
    mine = [_add_own(split[k], recv[k], b, sel, f"rs_addown_{k}") for k, b in zip(early, got)]
    grad["hg_in"], got = _matmul(xb, list(dparts[:4]), mode="tn", nsh=N_CHIPS, name="hg_win",
                                 comm=_merge(_plan_join(mine), _plan_gather_all(packed)))
    red = {k: _whole(f) for k, f in zip(early, got)}
    me8 = jnp.reshape(4 * x + 2 * y + c, (1,)).astype(jnp.int32)
    summed = _unpack(_sum_devices(got[len(early)], packed, me8, "sum_small_grads"), layout)
    gx = swap_on(lambda plan: _matmul(dparts[0], wg["hg_in"], mode="nt", nsh=1, b_off=0, resid=dx, alpha=ALPHA,
                                      name="hg_din_q", comm=plan), ["hg_in"])
    gx, got = _matmul(list(dparts[1:4]), wg["hg_in"], mode="nt", nsh=3, b_off=1, resid=gx, alpha=1.0, name="hg_din_fig",
                      comm=_plan_scatter([part["hg_in"]]))
    mine = _add_own(split["hg_in"], recv["hg_in"], got[0], sel, "rs_addown_hg_in")
    upd = {}
    upd["hg_w_out"], full = _adamw(w["hg_w_out"], [red["hg_out"]], mom["hg_w_out"], var["hg_w_out"], "adamw_hg_w_out",
                                   comm=_plan_join([mine]))
    red["hg_in"] = _whole(full[0])
    for k, src in (("ffn_w_up", ("up0", "up1")), ("ffn_w_down", ("dn0", "dn1")), ("sg_w_in", ("sg_in",)),
                   ("sg_w_out", ("sg_out",)), ("hg_w_in", ("hg_in",))):
        upd[k] = _adamw(w[k], [red[s] for s in src], mom[k], var[k], f"adamw_{k}")
    return loss, gx, summed, upd


_SMALL_ORDER = ("lb", "hg_norm_g", "sg_w_s", "sg_b_s_t", "conv_b", "ln1_g", "ln1_b", "ln2_g", "ln2_b",
                "conv_w", "sg_ln_g", "sg_ln_b")


PACK_ROWS = 512


def _pack(parts):
    flat, layout, off = [], [], 0
    for k in _SMALL_ORDER:
        a = parts[k]
        n = a.size
        pad = (-n) % LANES
        flat.append(jnp.pad(a.reshape(-1), (0, pad)))
        layout.append((k, off, n, a.shape))
        off += n + pad
    flat.append(jnp.zeros(((-off) % (PACK_ROWS * LANES),), F32))
    return jnp.concatenate(flat).reshape(-1, LANES), layout


def _unpack(buf, layout):
    flat = buf.reshape(-1)
    return {k: flat[off:off + n].reshape(shape) for k, off, n, shape in layout}


def kernel(x, lb_logits, hg_w_in, hg_norm_g, hg_w_out, sg_w_in, sg_ln_g, sg_ln_b, sg_w_s, sg_b_s, sg_w_out, ffn_w_up, ffn_conv_w, ffn_conv_b, ffn_w_down, ln1_g, ln1_b, ln2_g, ln2_b, loss_target, m_lb_logits, m_hg_w_in, m_hg_norm_g, m_hg_w_out, m_sg_w_in, m_sg_ln_g, m_sg_ln_b, m_sg_w_s, m_sg_b_s, m_sg_w_out, m_ffn_w_up, m_ffn_conv_w, m_ffn_conv_b, m_ffn_w_down, m_ln1_g, m_ln1_b, m_ln2_g, m_ln2_b, v_lb_logits, v_hg_w_in, v_hg_norm_g, v_hg_w_out, v_sg_w_in, v_sg_ln_g, v_sg_ln_b, v_sg_w_s, v_sg_b_s, v_sg_w_out, v_ffn_w_up, v_ffn_conv_w, v_ffn_conv_b, v_ffn_w_down, v_ln1_g, v_ln1_b, v_ln2_g, v_ln2_b):
    names = ("lb_logits", "hg_w_in", "hg_norm_g", "hg_w_out", "sg_w_in", "sg_ln_g", "sg_ln_b", "sg_w_s", "sg_b_s",
             "sg_w_out", "ffn_w_up", "ffn_conv_w", "ffn_conv_b", "ffn_w_down", "ln1_g", "ln1_b", "ln2_g", "ln2_b")
    w = dict(zip(names, (lb_logits, hg_w_in, hg_norm_g, hg_w_out, sg_w_in, sg_ln_g, sg_ln_b, sg_w_s, sg_b_s,
                         sg_w_out, ffn_w_up, ffn_conv_w, ffn_conv_b, ffn_w_down, ln1_g, ln1_b, ln2_g, ln2_b)))
    mom = dict(zip(names, (m_lb_logits, m_hg_w_in, m_hg_norm_g, m_hg_w_out, m_sg_w_in, m_sg_ln_g, m_sg_ln_b, m_sg_w_s,
                           m_sg_b_s, m_sg_w_out, m_ffn_w_up, m_ffn_conv_w, m_ffn_conv_b, m_ffn_w_down, m_ln1_g,
                           m_ln1_b, m_ln2_g, m_ln2_b)))
    var = dict(zip(names, (v_lb_logits, v_hg_w_in, v_hg_norm_g, v_hg_w_out, v_sg_w_in, v_sg_ln_g, v_sg_ln_b, v_sg_w_s,
                           v_sg_b_s, v_sg_w_out, v_ffn_w_up, v_ffn_conv_w, v_ffn_conv_b, v_ffn_w_down, v_ln1_g,
                           v_ln1_b, v_ln2_g, v_ln2_b)))
    tgt = loss_target[0]
    fq = ffn_conv_w.shape[2]
    dq = sg_ln_g.shape[1]
    cx, cy, _ = _place()
    me = 2 * cx + cy

    wide = max(fq, dq)
    tiny = jnp.concatenate([jnp.pad(ffn_conv_w.reshape(6, fq), ((0, 0), (0, wide - fq))),
                            jnp.pad(sg_ln_g, ((0, 0), (0, wide - dq))),
                            jnp.pad(sg_ln_b, ((0, 0), (0, wide - dq)))], axis=0)
    tiny_all = _allgather_whole(tiny, "gather_small")
    conv_w_full = jnp.transpose(tiny_all[:, 0:6, :fq].reshape(N_CHIPS, 2, 3, fq), (1, 2, 0, 3)).reshape(2, 3, N_CHIPS * fq)
    sm = {"lb_logits": lb_logits, "hg_norm_g": hg_norm_g, "ln1_g": ln1_g, "ln1_b": ln1_b, "ln2_g": ln2_g,
          "ln2_b": ln2_b, "conv_w": conv_w_full, "conv_b": ffn_conv_b,
          "sg_ln_g": tiny_all[:, 6, :dq].reshape(1, N_CHIPS * dq),
          "sg_ln_b": tiny_all[:, 7, :dq].reshape(1, N_CHIPS * dq),
          "sg_w_s": sg_w_s[0], "sg_b_s_t": jnp.transpose(sg_b_s[0])}

    loss_row, grad_x, summed, upd = _step(x, tgt, sm, w, mom, var)
    loss = lax.psum(loss_row[0, 0], ("x", "y", "c"))

    grads = {
        "lb_logits": _lb_logits_grad(lb_logits, summed["lb"], "lb_logits_grad"),
        "hg_norm_g": summed["hg_norm_g"],
        "sg_ln_g": lax.dynamic_slice_in_dim(summed["sg_ln_g"], me * dq, dq, axis=1),
        "sg_ln_b": lax.dynamic_slice_in_dim(summed["sg_ln_b"], me * dq, dq, axis=1),
        "sg_w_s": summed["sg_w_s"][None], "sg_b_s": jnp.transpose(summed["sg_b_s_t"])[None],
        "ffn_conv_w": lax.dynamic_slice_in_dim(summed["conv_w"], me * fq, fq, axis=2),
        "ffn_conv_b": summed["conv_b"],
        "ln1_g": summed["ln1_g"], "ln1_b": summed["ln1_b"], "ln2_g": summed["ln2_g"], "ln2_b": summed["ln2_b"],
    }

    delta, new_m, new_v = {}, {}, {}
    for k, (dlt, mm, vv, gg) in upd.items():
        delta[k], new_m[k], new_v[k], grads[k] = dlt, mm, vv, gg
    small_names = [k for k in names if k not in upd]

    def pack_small(src):
        flat = [src[k].reshape(-1) for k in small_names]
        n = sum(a.size for a in flat)
        flat.append(jnp.zeros(((-n) % (PACK_ROWS * LANES),), F32))
        return jnp.concatenate(flat).reshape(1, -1, LANES)

    outs = _adamw(pack_small(w), [pack_small(grads)[0]], pack_small(mom), pack_small(var), "adamw_small")
    off = 0
    for k in small_names:
        n = w[k].size
        for dst, o in zip((delta, new_m, new_v), outs):
            dst[k] = o.reshape(-1)[off:off + n].reshape(w[k].shape)
        off += n

    return (loss, grad_x[None], *[grads[k] for k in names], *[delta[k] for k in names],
            *[new_m[k] for k in names], *[new_v[k] for k in names])
```

```python
import functools

import jax
import jax.numpy as jnp
from jax import lax
from jax.experimental import pallas as pl
from jax.experimental.pallas import tpu as pltpu

F32 = jnp.float32
BF16 = jnp.bfloat16
MESH = pl.DeviceIdType.MESH

ALPHA = (2 * 2) ** 0.25
LN_EPS = 1e-5
RMS_EPS = 1e-6
ADAM_LR, ADAM_B1, ADAM_B2, ADAM_EPS, ADAM_WD, ADAM_STEP = 0.001, 0.9, 0.999, 1e-08, 0.01, 10

LANES = 128
SUB = 16
TILE = 8
GCHUNK = 128
VMEM_LIMIT = 56 * 1024 * 1024
N_CHIPS = 4
N_DEV = 8

NT = (((1,), (1,)), ((), ()))
TN = (((0,), (0,)), ((), ()))
NN = (((1,), (0,)), ((), ()))


def _pick(dim, prefs):
    for p in prefs:
        if dim % p == 0:
            return p
    return dim


def _params(sem=None, **kw):
    return pltpu.CompilerParams(dimension_semantics=sem, vmem_limit_bytes=VMEM_LIMIT, **kw)


def _sigmoid_pair(x):
    e = jnp.exp(-jnp.abs(x))
    inv = 1.0 / (1.0 + e)
    pos = x >= 0
    return jnp.where(pos, inv, e * inv), jnp.where(pos, e * inv, inv)


def _sigmoid_gate(x):
    t = 0.5 * jnp.tanh(0.5 * x)
    return 0.5 + t, 0.5 - t


def _ln_hat(x):
    mu = jnp.mean(x, axis=-1, keepdims=True)
    xc = x - mu
    var = jnp.mean(xc * xc, axis=-1, keepdims=True)
    rstd = lax.rsqrt(var + LN_EPS)
    return xc * rstd, rstd


def _lower_bound(logits):
    m = jnp.max(logits, axis=0, keepdims=True)
    e = jnp.exp(logits - m)
    return e[0:1, :] / jnp.sum(e, axis=0, keepdims=True)


MATMUL_VMEM_BUDGET = 40 * 1024 * 1024


def _fit_bk(kdim, bm, bn, out_dtype, has_resid, na=1, nb=1):
    fixed = bm * bn * (4 + 2 * jnp.dtype(out_dtype).itemsize + (8 if has_resid else 0))
    best = LANES
    for bk in range(LANES, kdim + 1, LANES):
        if kdim % bk == 0 and fixed + 4 * bk * (bm * na + bn * nb) <= MATMUL_VMEM_BUDGET:
            best = bk
    return best


def _fit_bm_bk(mdim, prefs, kdim, bn, out_dtype, has_resid, na=1, nb=1):
    best = None
    fits = [bm for bm in prefs if mdim % bm == 0][:2] or [mdim]
    for bm in fits:
        bk = _fit_bk(kdim, bm, bn, out_dtype, has_resid, na, nb)
        if best is None or kdim // bk < kdim // best[1]:
            best = (bm, bk)
    return best


def _matmul(a, b, *, mode, name, out_dtype=F32, resid=None, alpha=1.0, b_off=0, nsh=None, comm=None):
    a_parts = a if isinstance(a, (list, tuple)) else [a]
    b_parts = b if isinstance(b, (list, tuple)) else [b]
    n_parts = max(len(a_parts), len(b_parts))
    if mode == "nn":
        m, kdim = a.shape
        _, _, ns = b.shape
        bn = _pick(ns, (1024, 1408, 512, 256, 128))
        bm, bk = _fit_bm_bk(m, (1024, 512, 256, 128), kdim, bn, out_dtype, resid is not None)
        nps = ns // bn
        grid = (m // bm, nsh * nps, kdim // bk)
        a_specs = [pl.BlockSpec((bm, bk), lambda i, j, k: (i, k))]
        b_specs = [pl.BlockSpec((None, bk, bn), lambda i, j, k: (b_off + j // nps, k, j % nps))]
        o_spec = pl.BlockSpec((bm, bn), lambda i, j, k: (i, j))
        out_shape = jax.ShapeDtypeStruct((m, nsh * ns), out_dtype)
        dims, part_axis, per_part = NN, 2, grid[2]
    elif mode == "nt":
        m = a_parts[0].shape[0]
        _, kdim, ns = b.shape
        wide = n_parts > 1
        bn = _pick(kdim, (1024, 1408, 512, 256, 128) if wide else (512, 256, 128))
        bm, bk = _fit_bm_bk(m, (1024, 512, 256, 128) if wide else (2048, 1024, 512, 256, 128), ns, bn, out_dtype,
                            resid is not None, na=n_parts)
        kps = ns // bk
        per_part = nsh // n_parts * kps
        grid = (m // bm, kdim // bn, nsh * kps)
        a_specs = [pl.BlockSpec((bm, bk), lambda i, j, k, p=p: (jnp.where(k // per_part == p, i, 0),
                                                                  jnp.where(k // per_part == p, k % per_part, 0)))
                   for p in range(n_parts)]
        b_specs = [pl.BlockSpec((None, bn, bk), lambda i, j, k: (b_off + k // kps, j, k % kps))]
        o_spec = pl.BlockSpec((bm, bn), lambda i, j, k: (i, j))
        out_shape = jax.ShapeDtypeStruct((m, kdim), out_dtype)
        dims, part_axis = NT, 2
    else:
        t, kdim = a.shape
        ns = b_parts[0].shape[1] * n_parts // nsh
        bn = _pick(ns, (1024, 1408, 512, 256, 128))
        bm, bk = _fit_bm_bk(kdim, (1024, 1408, 512, 256, 128), t, bn, out_dtype, resid is not None, nb=n_parts)
        nps = ns // bn
        per_part = nsh // n_parts * nps
        grid = (kdim // bm, nsh * nps, t // bk)
        a_specs = [pl.BlockSpec((bk, bm), lambda i, j, k: (k, i))]
        b_specs = [pl.BlockSpec((bk, bn), lambda i, j, k, p=p: (jnp.where(j // per_part == p, k, 0),
                                                                  jnp.where(j // per_part == p, j % per_part, 0)))
                   for p in range(n_parts)]
        o_spec = pl.BlockSpec((None, bm, bn), lambda i, j, k: (j // nps, i, j % nps))
        out_shape = jax.ShapeDtypeStruct((nsh, kdim, ns), out_dtype)
        dims, part_axis = TN, 1
    nk = grid[2]
    na, nb_ = len(a_parts), len(b_parts)
    has_resid = resid is not None

    def kern(*refs):
        a_refs, b_refs = refs[:na], refs[na:na + nb_]
        r_ref = refs[na + nb_] if has_resid else None
        k = pl.program_id(2)

        def finish(r, o_ref):
            if has_resid:
                r = r + alpha * r_ref[...]
            o_ref[...] = r.astype(o_ref.dtype)

        def add(a_ref, b_ref):
            if nk == 1:
                finish(lax.dot_general(a_ref[...], b_ref[...], dims, preferred_element_type=F32), refs[-1])
                return
            refs[-1][...] += lax.dot_general(a_ref[...], b_ref[...], dims, preferred_element_type=F32)

        if nk > 1:
            @pl.when(k == 0)
            def _():
                refs[-1][...] = jnp.zeros_like(refs[-1])

        if n_parts == 1:
            add(a_refs[0], b_refs[0])
        else:
            which = pl.program_id(part_axis) // per_part
            for p in range(n_parts):
                pl.when(which == p)(functools.partial(add, a_refs[min(p, na - 1)], b_refs[min(p, nb_ - 1)]))
        if nk > 1:
            @pl.when(k == nk - 1)
            def _():
                finish(refs[-1][...], refs[-2])

    in_specs = a_specs + b_specs
    operands = list(a_parts) + list(b_parts)
    if has_resid:
        in_specs.append(pl.BlockSpec((bm, bn), lambda i, j, k: (i, j)))
        operands.append(resid)
    outs, carried = _carried_call(
        kern, comm, name=name, grid=grid, in_specs=in_specs, out_specs=[o_spec], out_shape=[out_shape],
        operands=operands, scratch_shapes=[pltpu.VMEM((bm, bn), F32)] if nk > 1 else [],
        sem=("parallel", "parallel", "arbitrary"))
    return outs[0] if comm is None else (outs[0], carried)


def _ln_fwd(xin, g, b, name):
    t, d = xin.shape
    tb = _pick(t, (256, 128, 64, 32, 16))

    def kern(x_ref, g_ref, b_ref, h_ref, hb_ref):
        xhat, _ = _ln_hat(x_ref[...])
        h = xhat * g_ref[...] + b_ref[...]
        h_ref[...] = h
        hb_ref[...] = h.astype(BF16)

    row = pl.BlockSpec((tb, d), lambda i: (i, 0))
    vec = pl.BlockSpec((1, d), lambda i: (0, 0))
    return pl.pallas_call(
        kern, name=name, grid=(t // tb,), in_specs=[row, vec, vec], out_specs=[row, row],
        out_shape=[jax.ShapeDtypeStruct((t, d), F32), jax.ShapeDtypeStruct((t, d), BF16)],
        compiler_params=_params(("parallel",)),
    )(xin, g, b)


def _ln_bwd(xin, dy_or_target, g, b, name, loss_head=False):
    t, d = xin.shape
    tb = _pick(t, (256, 128, 64, 32, 16))
    nb = t // tb

    def kern(x_ref, dy_ref, g_ref, b_ref, dx_ref, dxb_ref, dg_ref, db_ref, *rest):
        i = pl.program_id(0)
        xhat, rstd = _ln_hat(x_ref[...])
        gv = g_ref[...]
        if loss_head:
            loss_ref = rest[0]
            err = xhat * gv + b_ref[...] - dy_ref[...]
            dy = err * (1.0 / d)
            part = 0.5 * jnp.sum(jnp.sum(err * err, axis=1, keepdims=True), axis=0, keepdims=True) * (1.0 / d)
        else:
            dy = dy_ref[...]

        @pl.when(i == 0)
        def _():
            dg_ref[...] = jnp.zeros_like(dg_ref)
            db_ref[...] = jnp.zeros_like(db_ref)
            if loss_head:
                loss_ref[...] = jnp.zeros_like(loss_ref)

        dg_ref[...] += jnp.sum(dy * xhat, axis=0, keepdims=True)
        db_ref[...] += jnp.sum(dy, axis=0, keepdims=True)
        if loss_head:
            loss_ref[...] += jnp.broadcast_to(part, loss_ref.shape)
        dxh = dy * gv
        m1 = jnp.mean(dxh, axis=-1, keepdims=True)
        m2 = jnp.mean(dxh * xhat, axis=-1, keepdims=True)
        dx = rstd * (dxh - m1 - xhat * m2)
        dx_ref[...] = dx
        dxb_ref[...] = dx.astype(BF16)

    row = pl.BlockSpec((tb, d), lambda i: (i, 0))
    vec = pl.BlockSpec((1, d), lambda i: (0, 0))
    out_specs = [row, row, vec, vec]
    out_shape = [jax.ShapeDtypeStruct((t, d), F32), jax.ShapeDtypeStruct((t, d), BF16),
                 jax.ShapeDtypeStruct((1, d), F32), jax.ShapeDtypeStruct((1, d), F32)]
    if loss_head:
        out_specs.append(pl.BlockSpec((1, LANES), lambda i: (0, 0)))
        out_shape.append(jax.ShapeDtypeStruct((1, LANES), F32))
    return pl.pallas_call(
        kern, name=name, grid=(nb,), in_specs=[row, row, vec, vec], out_specs=out_specs, out_shape=out_shape,
        compiler_params=_params(("arbitrary",)),
    )(xin, dy_or_target, g, b)


def _conv_gate_fwd(u, conv_w, conv_b, name, comm=None):
    t, f2 = u.shape
    f = f2 // 2
    tb = _pick(t, (512, 256, 128, 64, 32, 16))
    cn = _pick(f, (1408, 1024, 512, 256, 128))
    ncb = f // cn
    hb = tb // 8

    def kern(a_ref, ah_ref, b_ref, w_ref, cb_ref, o_ref):
        i = pl.program_id(0)
        a = a_ref[...]
        halo = jnp.where(i > 0, ah_ref[...], 0.0)
        rid = lax.broadcasted_iota(jnp.int32, a.shape, 0)
        s1 = jnp.where(rid == 0, halo[7:8, :], pltpu.roll(a, 1, 0))
        s2 = jnp.where(rid == 0, halo[6:7, :], jnp.where(rid == 1, halo[7:8, :], pltpu.roll(a, 2, 0)))
        w = w_ref[...]
        conv = w[2:3, :] * a + w[1:2, :] * s1 + w[0:1, :] * s2 + cb_ref[...]
        sp, _ = _sigmoid_gate(conv)
        o_ref[...] = (conv * sp * b_ref[...]).astype(BF16)

    outs, carried = _carried_call(
        kern, comm, name=name, grid=(t // tb, ncb),
        in_specs=[pl.BlockSpec((tb, cn), lambda i, j: (i, j)),
                  pl.BlockSpec((8, cn), lambda i, j: (jnp.maximum(i * hb - 1, 0), j)),
                  pl.BlockSpec((tb, cn), lambda i, j: (i, j + ncb)),
                  pl.BlockSpec((3, cn), lambda i, j: (0, j)),
                  pl.BlockSpec((1, cn), lambda i, j: (0, j))],
        out_specs=[pl.BlockSpec((tb, cn), lambda i, j: (i, j))],
        out_shape=[jax.ShapeDtypeStruct((t, f), BF16)],
        operands=[u, u, u, conv_w, conv_b], sem=("parallel", "parallel"))
    return outs[0] if comm is None else (outs[0], carried)


def _conv_gate_bwd(u, dgact, conv_w, conv_b, name):
    t, f2 = u.shape
    f = f2 // 2
    tb = _pick(t, (512, 256, 128, 64, 32, 16))
    cn = _pick(f, (1408, 1024, 512, 256, 128))
    ncb = f // cn
    hb = tb // 8
    nb = t // tb
    last8 = t // 8 - 1

    def kern(a_ref, ap_ref, an_ref, b_ref, bn_ref, dg_ref, dgn_ref, w_ref, cb_ref,
             da_ref, db_ref, dw_ref, dcb_ref):
        i = pl.program_id(1)
        a = a_ref[...]
        w = w_ref[...]
        ext = jnp.concatenate([jnp.where(i > 0, ap_ref[...], 0.0), a, an_ref[...]], axis=0)
        e1 = pltpu.roll(ext, 1, 0)
        e2 = pltpu.roll(ext, 2, 0)
        conv = (w[2:3, :] * ext + w[1:2, :] * e1 + w[0:1, :] * e2 + cb_ref[...])[8:, :]
        bmn = jnp.concatenate([b_ref[...], bn_ref[...]], axis=0)
        dgmn = jnp.concatenate([dg_ref[...], jnp.where(i < nb - 1, dgn_ref[...], 0.0)], axis=0)
        sp, sn = _sigmoid_gate(conv)
        da = dgmn * bmn * (sp * (1.0 + conv * sn))
        n = tb + 8
        dap = w[2:3, :] * da + w[1:2, :] * pltpu.roll(da, n - 1, 0) + w[0:1, :] * pltpu.roll(da, n - 2, 0)
        da_ref[...] = dap[:tb, :].astype(BF16)
        db_ref[...] = (dg_ref[...] * (conv * sp)[:tb, :]).astype(BF16)
        dam = da[:tb, :]

        @pl.when(i == 0)
        def _():
            dw_ref[...] = jnp.zeros_like(dw_ref)
            dcb_ref[...] = jnp.zeros_like(dcb_ref)

        dw = jnp.concatenate([jnp.sum(dam * e2[8:8 + tb, :], axis=0, keepdims=True),
                              jnp.sum(dam * e1[8:8 + tb, :], axis=0, keepdims=True),
                              jnp.sum(dam * a, axis=0, keepdims=True)], axis=0)
        dw_ref[...] += dw
        dcb_ref[...] += jnp.sum(dam, axis=0, keepdims=True)

    main_a = pl.BlockSpec((tb, cn), lambda j, i: (i, j))
    prev_a = pl.BlockSpec((8, cn), lambda j, i: (jnp.maximum(i * hb - 1, 0), j))
    next_a = pl.BlockSpec((8, cn), lambda j, i: (jnp.minimum((i + 1) * hb, last8), j))
    main_b = pl.BlockSpec((tb, cn), lambda j, i: (i, j + ncb))
    next_b = pl.BlockSpec((8, cn), lambda j, i: (jnp.minimum((i + 1) * hb, last8), j + ncb))
    return pl.pallas_call(
        kern, name=name, grid=(ncb, nb),
        in_specs=[main_a, prev_a, next_a, main_b, next_b, main_a, next_a,
                  pl.BlockSpec((3, cn), lambda j, i: (0, j)), pl.BlockSpec((1, cn), lambda j, i: (0, j))],
        out_specs=[main_a, main_a, pl.BlockSpec((3, cn), lambda j, i: (0, j)),
                   pl.BlockSpec((1, cn), lambda j, i: (0, j))],
        out_shape=[jax.ShapeDtypeStruct((t, f), BF16), jax.ShapeDtypeStruct((t, f), BF16),
                   jax.ShapeDtypeStruct((3, f), F32), jax.ShapeDtypeStruct((1, f), F32)],
        compiler_params=_params(("parallel", "arbitrary")),
    )(u, u, u, u, u, dgact, dgact, conv_w, conv_b)


def _hg_gates(qp, fp, lb):
    sq, _ = _sigmoid_gate(qp)
    sf, snf = _sigmoid_pair(fp)
    forget = lb + (1.0 - lb) * sf
    return sq, sf, snf, forget, jnp.log(forget), (1.0 - lb) * snf


def _tri(lower):
    r = lax.broadcasted_iota(jnp.int32, (SUB, SUB), 0)
    c = lax.broadcasted_iota(jnp.int32, (SUB, SUB), 1)
    return ((r >= c) if lower else (r <= c)).astype(BF16)


def _split2(x):
    hi = x.astype(BF16)
    return hi, (x - hi.astype(F32)).astype(BF16)


def _dot3(a, b, dims):
    (ah, al), (bh, bl) = a, b
    return (lax.dot_general(ah, bh, dims, preferred_element_type=F32)
            + (lax.dot_general(ah, bl, dims, preferred_element_type=F32)
               + lax.dot_general(al, bh, dims, preferred_element_type=F32)))


def _running_sum(tri, x):
    hi, lo = _split2(x)
    rest = (x - hi.astype(F32)) - lo.astype(F32)
    return (lax.dot_general(tri, hi, NN, preferred_element_type=F32)
            + (lax.dot_general(tri, lo, NN, preferred_element_type=F32)
               + lax.dot_general(tri, rest.astype(BF16), NN, preferred_element_type=F32)))


HEADS_PER_STEP = 8
STEP_UNROLL = 2


def _hgrn2_fwd(proj, lb_logits, norm_g, name, comm=None):
    t, d4 = proj.shape
    d = d4 // 4
    nh = d // LANES
    hb = _pick(nh, (HEADS_PER_STEP, 2, 1))
    wb = hb * LANES
    tb = _pick(t, (256, 128, 64, 32, 16))
    nb = t // tb
    nsc = tb // SUB

    def kern(q_ref, f_ref, i_ref, g_ref, lbl_ref, ng_ref, y_ref, o_ref, st_ref, s_ref):
        @pl.when(pl.program_id(1) == 0)
        def _():
            s_ref[...] = jnp.zeros_like(s_ref)

        lb_all = _lower_bound(lbl_ref[...])
        ng_all = ng_ref[...]
        ltri = _tri(True)
        rcol = lax.broadcasted_iota(jnp.int32, (SUB, 1), 0)

        heads = [slice(h * LANES, (h + 1) * LANES) for h in range(hb)]

        def step(sc, carry):
            rows = pl.ds(pl.multiple_of(sc * SUB, SUB), SUB)
            qp, fp, v, gp = q_ref[rows, :], f_ref[rows, :], i_ref[rows, :], g_ref[rows, :]
            sq, _, _, _, lf, k = _hg_gates(qp, fp, lb_all)
            q = qp * sq
            bl = _running_sum(ltri, lf)
            bend = bl[SUB - 1:SUB, :]
            dec = jnp.exp(bend)
            qs2 = _split2(q * jnp.exp(bl))
            kd2 = _split2(k * jnp.exp(bend - bl))
            v2 = _split2(v)
            states = [s_ref[h] for h in range(hb)]
            o = [_dot3((qs2[0][:, c], qs2[1][:, c]), _split2(states[h]), NT) for h, c in enumerate(heads)]
            top, bot = [oh[:TILE] for oh in o], [oh[TILE:] for oh in o]
            for s in range(SUB):
                lo = 0 if s < TILE else TILE
                e = jnp.exp(jnp.minimum(bl[lo:] - bl[s:s + 1, :], 0.0))
                p = q[lo:] * e * k[s:s + 1, :]
                for h, c in enumerate(heads):
                    a = jnp.sum(p[:, c], axis=1, keepdims=True)
                    add = jnp.where(rcol[lo:] >= s, a, 0.0) * v[s:s + 1, c]
                    if lo == 0:
                        top[h], bot[h] = top[h] + add[:TILE], bot[h] + add[TILE:]
                    else:
                        bot[h] = bot[h] + add
            o = [jnp.concatenate([a, b], axis=0) for a, b in zip(top, bot)]
            for h, c in enumerate(heads):
                st_ref[sc, h] = states[h]
                s_ref[h] = states[h] * dec[:, c] + _dot3((v2[0][:, c], v2[1][:, c]), (kd2[0][:, c], kd2[1][:, c]), TN)
            o_ref[rows, :] = jnp.concatenate(o, axis=1)
            on = jnp.concatenate(
                [oh * lax.rsqrt(jnp.mean(oh * oh, axis=1, keepdims=True) + RMS_EPS) for oh in o], axis=1)
            sg, _ = _sigmoid_gate(gp)
            y_ref[rows, :] = (on * ng_all * (gp * sg)).astype(BF16)
            return carry

        lax.fori_loop(0, nsc, step, 0, unroll=STEP_UNROLL)

    def col(off):
        return pl.BlockSpec((tb, wb), lambda h, j: (j, h + off * (nh // hb)))

    return _carried_call(
        kern, comm, name=name, grid=(nh // hb, nb),
        in_specs=[col(0), col(1), col(2), col(3),
                  pl.BlockSpec((3, wb), lambda h, j: (0, h)), pl.BlockSpec((1, wb), lambda h, j: (0, h))],
        out_specs=[col(0), col(0), pl.BlockSpec((nsc, hb, LANES, LANES), lambda h, j: (j, h, 0, 0))],
        out_shape=[jax.ShapeDtypeStruct((t, d), BF16), jax.ShapeDtypeStruct((t, d), F32),
                   jax.ShapeDtypeStruct((t // SUB, nh, LANES, LANES), F32)],
        scratch_shapes=[pltpu.VMEM((hb, LANES, LANES), F32)],
        operands=[proj, proj, proj, proj, lb_logits, norm_g], sem=("parallel", "arbitrary"))


def _hgrn2_bwd(proj, lb_logits, norm_g, o_raw, states, dy, name, comm=None):
    t, d4 = proj.shape
    d = d4 // 4
    nh = d // LANES
    hb = _pick(nh, (HEADS_PER_STEP, 2, 1))
    wb = hb * LANES
    tb = _pick(t, (256, 128, 64, 32, 16))
    nb = t // tb
    nsc = tb // SUB

    def kern(q_ref, f_ref, i_ref, g_ref, lbl_ref, ng_ref, o_ref, st_ref, dy_ref,
             dq_ref, df_ref, di_ref, dgp_ref, dlb_ref, dng_ref, ds_ref, gc_ref):
        j = pl.program_id(1)

        @pl.when(j == 0)
        def _():
            ds_ref[...] = jnp.zeros_like(ds_ref)
            gc_ref[...] = jnp.zeros_like(gc_ref)
            dlb_ref[...] = jnp.zeros_like(dlb_ref)
            dng_ref[...] = jnp.zeros_like(dng_ref)

        lb_all = _lower_bound(lbl_ref[...])
        ng_all = ng_ref[...]
        ltri, utri = _tri(True), _tri(False)
        rcol = lax.broadcasted_iota(jnp.int32, (SUB, 1), 0)
        rid = lax.broadcasted_iota(jnp.int32, (SUB, wb), 0)

        heads = [slice(h * LANES, (h + 1) * LANES) for h in range(hb)]

        def per_head(fn, n=SUB):
            return jnp.concatenate([jnp.broadcast_to(fn(c), (n, LANES)) for c in heads], axis=1)

        def step(it, carry):
            sc = nsc - 1 - it
            rows = pl.ds(pl.multiple_of(sc * SUB, SUB), SUB)
            qp, fp, v, gp = q_ref[rows, :], f_ref[rows, :], i_ref[rows, :], g_ref[rows, :]
            o, dyv = o_ref[rows, :], dy_ref[rows, :]
            sq, sf, snf, forget, lf, k = _hg_gates(qp, fp, lb_all)
            q = qp * sq
            bl = _running_sum(ltri, lf)
            ebl = jnp.exp(bl)
            bend = bl[SUB - 1:SUB, :]
            dec = jnp.exp(bend)
            dte = jnp.exp(bend - bl)
            r = per_head(lambda c: lax.rsqrt(jnp.mean(o[:, c] * o[:, c], axis=1, keepdims=True) + RMS_EPS))
            ohat = o * r
            sg, sng = _sigmoid_gate(gp)
            don = dyv * (gp * sg)
            dgp_ref[rows, :] = (dyv * (ohat * ng_all) * (sg * (1.0 + gp * sng))).astype(BF16)
            dng_ref[...] += jnp.sum(don * ohat, axis=0, keepdims=True)
            doh = don * ng_all
            dot_oh = doh * ohat
            do = r * (doh - ohat * per_head(lambda c: jnp.mean(dot_oh[:, c], axis=1, keepdims=True)))
            do2, qs2, kd2, v2 = _split2(do), _split2(q * ebl), _split2(k * dte), _split2(v)
            dq_h, dk_h, dv_h = [], [], []
            for h, c in enumerate(heads):
                dstate = ds_ref[h]
                ds2 = _split2(dstate)
                doc = (do2[0][:, c], do2[1][:, c])
                dq_h.append(_dot3(doc, _split2(st_ref[sc, h]), NN))
                dv_h.append(_dot3((kd2[0][:, c], kd2[1][:, c]), ds2, NT))
                dk_h.append(_dot3((v2[0][:, c], v2[1][:, c]), ds2, NN))
                ds_ref[h] = dstate * dec[:, c] + _dot3(doc, (qs2[0][:, c], qs2[1][:, c]), TN)
            dq = jnp.concatenate(dq_h, axis=1) * ebl
            dk = jnp.concatenate(dk_h, axis=1) * dte
            dv = jnp.concatenate(dv_h, axis=1)
            dq_t, dq_b = dq[:TILE], dq[TILE:]
            dk_i = [jnp.zeros((TILE, wb), F32), jnp.zeros((TILE, wb), F32)]
            dv_i = [jnp.zeros((TILE, wb), F32), jnp.zeros((TILE, wb), F32)]
            for s in range(SUB):
                lo = 0 if s < TILE else TILE
                n = SUB - lo
                e = jnp.exp(jnp.minimum(bl[lo:] - bl[s:s + 1, :], 0.0))
                qe = q[lo:] * e
                ks = k[s:s + 1, :]
                live = rcol[lo:] >= s
                pk = qe * ks
                dor = do[lo:]
                pv = dor * v[s:s + 1, :]
                a = per_head(lambda c: jnp.where(live, jnp.sum(pk[:, c], axis=1, keepdims=True), 0.0), n)
                da = per_head(lambda c: jnp.where(live, jnp.sum(pv[:, c], axis=1, keepdims=True), 0.0), n)
                ddq = da * (e * ks)
                if lo == 0:
                    dq_t, dq_b = dq_t + ddq[:TILE], dq_b + ddq[TILE:]
                else:
                    dq_b = dq_b + ddq
                here = rid[:TILE] == s - lo
                dk_i[lo // TILE] = jnp.where(here, jnp.sum(da * qe, axis=0, keepdims=True), dk_i[lo // TILE])
                dv_i[lo // TILE] = jnp.where(here, jnp.sum(a * dor, axis=0, keepdims=True), dv_i[lo // TILE])
            dq = jnp.concatenate([dq_t, dq_b], axis=0)
            dk = dk + jnp.concatenate(dk_i, axis=0)
            dv = dv + jnp.concatenate(dv_i, axis=0)
            w = q * dq - k * dk
            gc = gc_ref[...]
            dlf = _running_sum(utri, w) + gc
            gc_ref[...] = gc + jnp.sum(w, axis=0, keepdims=True)
            t1 = dlf / forget - dk
            df_ref[rows, :] = ((1.0 - lb_all) * sf * snf * t1).astype(BF16)
            dlb_ref[...] += jnp.sum(snf * t1, axis=0, keepdims=True)
            dq_ref[rows, :] = (dq * (sq * (1.0 + qp * (1.0 - sq)))).astype(BF16)
            di_ref[rows, :] = dv.astype(BF16)
            return carry

        lax.fori_loop(0, nsc, step, 0, unroll=STEP_UNROLL)

    def col(off):
        return pl.BlockSpec((tb, wb), lambda h, j: (nb - 1 - j, h + off * (nh // hb)))

    vec = pl.BlockSpec((1, wb), lambda h, j: (0, h))
    return _carried_call(
        kern, comm, name=name, grid=(nh // hb, nb),
        in_specs=[col(0), col(1), col(2), col(3), pl.BlockSpec((3, wb), lambda h, j: (0, h)), vec,
                  col(0), pl.BlockSpec((nsc, hb, LANES, LANES), lambda h, j: (nb - 1 - j, h, 0, 0)), col(0)],
        out_specs=[col(0), col(0), col(0), col(0), vec, vec],
        out_shape=[jax.ShapeDtypeStruct((t, d), BF16)] * 4 + [jax.ShapeDtypeStruct((1, d), F32)] * 2,
        scratch_shapes=[pltpu.VMEM((hb, LANES, LANES), F32), pltpu.VMEM((1, wb), F32)],
        operands=[proj, proj, proj, proj, lb_logits, norm_g, o_raw, states, dy], sem=("parallel", "arbitrary"))


_INV_SQRT2 = 0.7071067811865476
_INV_SQRT2PI = 0.3989422804014327


def _gelu(x):
    return 0.5 * x * (1.0 + lax.erf(x * _INV_SQRT2))


def _gelu_grad(x):
    return 0.5 * (1.0 + lax.erf(x * _INV_SQRT2)) + x * jnp.exp(-0.5 * x * x) * _INV_SQRT2PI


def _causal(w):
    r = lax.broadcasted_iota(jnp.int32, (GCHUNK, GCHUNK), 0)
    c = lax.broadcasted_iota(jnp.int32, (GCHUNK, GCHUNK), 1)
    return jnp.where(r >= c, w, 0.0)


def _sg_gate_fwd(pre, ln_g, ln_b, w_s, b_s_t, name):
    t, d2 = pre.shape
    d = d2 // 2
    ng = d // LANES

    def kern(pre_ref, g_ref, b_ref, ws_ref, bs_ref, y_ref):
        z = _gelu(pre_ref[...])
        u = z[:, :d]
        vhat, _ = _ln_hat(z[:, d:])
        vn = (vhat * g_ref[...] + b_ref[...]).astype(BF16)
        bs = bs_ref[...]
        for g in range(ng):
            cols = slice(g * LANES, (g + 1) * LANES)
            wc = _causal(ws_ref[g]).astype(BF16)
            gate = jnp.dot(wc, vn[:, cols], preferred_element_type=F32) + bs[:, g:g + 1]
            y_ref[:, cols] = (u[:, cols] * gate).astype(BF16)

    vec = pl.BlockSpec((1, d), lambda i: (0, 0))
    return pl.pallas_call(
        kern, name=name, grid=(t // GCHUNK,),
        in_specs=[pl.BlockSpec((GCHUNK, d2), lambda i: (i, 0)), vec, vec,
                  pl.BlockSpec((ng, GCHUNK, GCHUNK), lambda i: (0, 0, 0)),
                  pl.BlockSpec((GCHUNK, ng), lambda i: (0, 0))],
        out_specs=pl.BlockSpec((GCHUNK, d), lambda i: (i, 0)),
        out_shape=jax.ShapeDtypeStruct((t, d), BF16),
        compiler_params=_params(("parallel",)),
    )(pre, ln_g, ln_b, w_s, b_s_t)


def _sg_gate_bwd(pre, dy, ln_g, ln_b, w_s, b_s_t, name):
    t, d2 = pre.shape
    d = d2 // 2
    ng = d // LANES

    def kern(pre_ref, dy_ref, g_ref, b_ref, ws_ref, bs_ref, dpre_ref, dws_ref, dbs_ref, dg_ref, db_ref, dvn_ref):
        @pl.when(pl.program_id(0) == 0)
        def _():
            dws_ref[...] = jnp.zeros_like(dws_ref)
            dbs_ref[...] = jnp.zeros_like(dbs_ref)
            dg_ref[...] = jnp.zeros_like(dg_ref)
            db_ref[...] = jnp.zeros_like(db_ref)

        pre = pre_ref[...]
        z = _gelu(pre)
        u = z[:, :d]
        vhat, rstd = _ln_hat(z[:, d:])
        gv = g_ref[...]
        vn = (vhat * gv + b_ref[...]).astype(BF16)
        bs = bs_ref[...]
        dyv = dy_ref[...]
        gp = _gelu_grad(pre)
        lane = lax.broadcasted_iota(jnp.int32, (GCHUNK, ng), 1)
        dbs = jnp.zeros((GCHUNK, ng), F32)
        for g in range(ng):
            cols = slice(g * LANES, (g + 1) * LANES)
            wc = _causal(ws_ref[g]).astype(BF16)
            vng = vn[:, cols]
            gate = jnp.dot(wc, vng, preferred_element_type=F32) + bs[:, g:g + 1]
            dpre_ref[:, cols] = (dyv[:, cols] * gate * gp[:, cols]).astype(BF16)
            dgate = dyv[:, cols] * u[:, cols]
            dbs = dbs + jnp.where(lane == g, jnp.sum(dgate, axis=1, keepdims=True), 0.0)
            dgb = dgate.astype(BF16)
            dws_ref[g] += _causal(lax.dot_general(dgb, vng, NT, preferred_element_type=F32))
            dvn_ref[:, cols] = lax.dot_general(wc, dgb, TN, preferred_element_type=F32)
        dbs_ref[...] += dbs
        dvn = dvn_ref[...]
        dg_ref[...] += jnp.sum(dvn * vhat, axis=0, keepdims=True)
        db_ref[...] += jnp.sum(dvn, axis=0, keepdims=True)
        dvh = dvn * gv
        m1 = jnp.mean(dvh, axis=-1, keepdims=True)
        m2 = jnp.mean(dvh * vhat, axis=-1, keepdims=True)
        dpre_ref[:, d:] = (rstd * (dvh - m1 - vhat * m2) * gp[:, d:]).astype(BF16)

    vec = pl.BlockSpec((1, d), lambda i: (0, 0))
    wsp = pl.BlockSpec((ng, GCHUNK, GCHUNK), lambda i: (0, 0, 0))
    bsp = pl.BlockSpec((GCHUNK, ng), lambda i: (0, 0))
    return pl.pallas_call(
        kern, name=name, grid=(t // GCHUNK,),
        in_specs=[pl.BlockSpec((GCHUNK, d2), lambda i: (i, 0)), pl.BlockSpec((GCHUNK, d), lambda i: (i, 0)),
                  vec, vec, wsp, bsp],
        out_specs=[pl.BlockSpec((GCHUNK, d2), lambda i: (i, 0)), wsp, bsp, vec, vec],
        out_shape=[jax.ShapeDtypeStruct((t, d2), BF16), jax.ShapeDtypeStruct((ng, GCHUNK, GCHUNK), F32),
                   jax.ShapeDtypeStruct((GCHUNK, ng), F32), jax.ShapeDtypeStruct((1, d), F32),
                   jax.ShapeDtypeStruct((1, d), F32)],
        scratch_shapes=[pltpu.VMEM((GCHUNK, d), F32)],
        compiler_params=_params(("arbitrary",)),
    )(pre, dy, ln_g, ln_b, w_s, b_s_t)


def _adamw_math(w, g, m, v):
    m = ADAM_B1 * m + (1.0 - ADAM_B1) * g
    v = ADAM_B2 * v + (1.0 - ADAM_B2) * (g * g)
    m_hat = m / (1.0 - ADAM_B1 ** ADAM_STEP)
    v_hat = v / (1.0 - ADAM_B2 ** ADAM_STEP)
    return -ADAM_LR * (m_hat / (jnp.sqrt(v_hat) + ADAM_EPS) + ADAM_WD * w), m, v


ADAMW_BLOCK_BYTES = 3 << 19


def _adamw(w, gs, m, v, name, comm=None):
    nl, r, c = w.shape
    rb = _pick(r, tuple(p for p in (512, 256, 128, 64, 32, 16, 8) if p * c * 4 <= ADAMW_BLOCK_BYTES))

    def kern(w_ref, m_ref, v_ref, *rest):
        g_refs, (d_ref, mo_ref, vo_ref, go_ref) = rest[:nl], rest[nl:]
        layer = pl.program_id(0)
        g = g_refs[0][...]
        for k in range(1, nl):
            g = jnp.where(layer == k, g_refs[k][...], g)
        dlt, mm, vv = _adamw_math(w_ref[...], g, m_ref[...], v_ref[...])
        d_ref[...] = dlt
        mo_ref[...] = mm
        vo_ref[...] = vv
        go_ref[...] = g

    blk = pl.BlockSpec((None, rb, c), lambda l, i: (l, i, 0))
    g_specs = [pl.BlockSpec((rb, c), lambda l, i, k=k: (jnp.where(l == k, i, 0), 0)) for k in range(nl)]
    outs, carried = _carried_call(
        kern, comm, name=name, grid=(nl, r // rb), in_specs=[blk] * 3 + g_specs, out_specs=[blk] * 4,
        out_shape=[jax.ShapeDtypeStruct((nl, r, c), F32)] * 4, operands=[w, m, v, *gs], sem=("parallel", "parallel"))
    return outs if comm is None else (outs, carried)


CAST_BLOCK_BYTES = 1 << 21


def _cast_bf16(items, name, comm=None):
    metas, start = [], 0
    for arr, _ in items:
        _, r, c = arr.shape
        rb = _pick(r, tuple(p for p in (1024, 512, 256, 128, 64, 32, 16) if p * c * 4 <= CAST_BLOCK_BYTES))
        metas.append((start, r // rb, rb, c))
        start += r // rb
    n = len(items)

    def spec(p, layer=None):
        s0, steps, rb, c = metas[p]
        if layer is None:
            return pl.BlockSpec((rb, c), lambda s: (jnp.clip(s - s0, 0, steps - 1), 0))
        return pl.BlockSpec((None, rb, c), lambda s: (layer, jnp.clip(s - s0, 0, steps - 1), 0))

    def kern(*refs):
        s = pl.program_id(0)
        for p in range(n):
            s0, steps, _, _ = metas[p]

            @pl.when(jnp.logical_and(s >= s0, s < s0 + steps))
            def _(p=p):
                refs[n + p][...] = refs[p][...].astype(BF16)

    outs, carried = _carried_call(
        kern, comm, name=name, grid=(start,), in_specs=[spec(p, layer) for p, (_, layer) in enumerate(items)],
        out_specs=[spec(p) for p in range(n)],
        out_shape=[jax.ShapeDtypeStruct(arr.shape[1:], BF16) for arr, _ in items],
        operands=[arr for arr, _ in items], sem=("arbitrary",))
    return outs, carried


def _comm_call(plan, name):
    ni, no = len(plan.ins), len(plan.out_shapes)

    def body(*refs):
        copies = plan.build(refs[:ni], refs[ni:ni + no], refs[-2], refs[-1], 0)
        for cp in copies:
            cp.start()
        for cp in copies:
            cp.wait()

    anyspec = pl.BlockSpec(memory_space=pl.ANY)
    return pl.pallas_call(
        body, name=name, in_specs=[anyspec] * ni, out_specs=[anyspec] * no, out_shape=plan.out_shapes,
        input_output_aliases=plan.aliases,
        scratch_shapes=[pltpu.SemaphoreType.DMA((plan.n_sems,)), pltpu.SemaphoreType.DMA((plan.n_sems,))],
        compiler_params=pltpu.CompilerParams(has_side_effects=True),
    )(*plan.ins)


def _lb_logits_grad(lb_logits, dlb, name):
    def kern(l_ref, d_ref, o_ref):
        lg = l_ref[...]
        m = jnp.max(lg, axis=0, keepdims=True)
        e = jnp.exp(lg - m)
        p = e / jnp.sum(e, axis=0, keepdims=True)
        row = lax.broadcasted_iota(jnp.int32, lg.shape, 0)
        o_ref[...] = d_ref[...] * p[0:1, :] * (jnp.where(row == 0, 1.0, 0.0) - p)

    return pl.pallas_call(kern, name=name, out_shape=jax.ShapeDtypeStruct(lb_logits.shape, F32))(lb_logits, dlb)


def _sum_devices(others, own, me, name):
    n, r, c = others.shape
    rb = _pick(r, (512, 256, 128, 64, 32, 16, 8))

    def kern(me_ref, a_ref, own_ref, o_ref):
        mine = own_ref[...]
        acc = jnp.where(me_ref[0] == 0, mine, a_ref[0])
        for i in range(1, n):
            acc = acc + jnp.where(me_ref[0] == i, mine, a_ref[i])
        o_ref[...] = acc

    return pl.pallas_call(
        kern, name=name,
        grid_spec=pltpu.PrefetchScalarGridSpec(
            num_scalar_prefetch=1, grid=(r // rb,),
            in_specs=[pl.BlockSpec((n, rb, c), lambda i, s: (0, i, 0)), pl.BlockSpec((rb, c), lambda i, s: (i, 0))],
            out_specs=pl.BlockSpec((rb, c), lambda i, s: (i, 0))),
        out_shape=jax.ShapeDtypeStruct((r, c), F32),
        compiler_params=_params(("parallel",)),
    )(me, others, own)


def _place():
    x, y, c = lax.axis_index("x"), lax.axis_index("y"), lax.axis_index("c")
    return x, y, c


class _Plan:
    def __init__(self, ins, out_shapes, aliases, n_sems, build):
        self.ins, self.out_shapes, self.aliases, self.n_sems, self.build = list(ins), list(out_shapes), aliases, n_sems, build


def _merge(*plans):
    ins, outs, aliases, subs, sems = [], [], {}, [], 0
    for p in plans:
        for k, v in p.aliases.items():
            aliases[len(ins) + k] = len(outs) + v
        subs.append((p, len(ins), len(outs), sems))
        ins += p.ins
        outs += p.out_shapes
        sems += p.n_sems

    def build(in_refs, out_refs, send_sems, recv_sems, base):
        copies = []
        for p, i0, o0, s0 in subs:
            copies += p.build(in_refs[i0:i0 + len(p.ins)], out_refs[o0:o0 + len(p.out_shapes)], send_sems, recv_sems,
                              base + s0)
        return copies

    return _Plan(ins, outs, aliases, sems, build)


def _remote(src, dst, send_sems, recv_sems, k, to):
    return pltpu.make_async_remote_copy(src_ref=src, dst_ref=dst, send_sem=send_sems.at[k], recv_sem=recv_sems.at[k],
                                        device_id=to, device_id_type=MESH)


def _plan_gather_ici(shards):
    n = len(shards)

    def build(ins, outs, send_sems, recv_sems, base):
        x, y, c = _place()
        me = 2 * x + y
        copies = []
        for a in range(n):
            h = ins[a].shape[0] // 2
            rows = pl.ds(c * h, h)
            for r in (1, 2, 3):
                px, py, _ = _chip_rel(x, y, r)
                copies.append(_remote(ins[a].at[rows, :], outs[a].at[me, rows, :], send_sems, recv_sems,
                                      base + 4 * a + r - 1, (px, py, c)))
            copies.append(_remote(ins[a], outs[a].at[me], send_sems, recv_sems, base + 4 * a + 3, (x, y, 1 - c)))
        return copies

    return _Plan(shards, [jax.ShapeDtypeStruct((N_CHIPS,) + s.shape, s.dtype) for s in shards], {}, 4 * n, build)


def _plan_gather_pass(gathered):
    n = len(gathered)

    def build(ins, outs, send_sems, recv_sems, base):
        x, y, c = _place()
        copies = []
        for a in range(n):
            h = outs[a].shape[1] // 2
            rows = pl.ds(c * h, h)
            for r in (1, 2, 3):
                _, _, shard = _chip_rel(x, y, r)
                piece = outs[a].at[shard, rows, :]
                copies.append(_remote(piece, piece, send_sems, recv_sems, base + 3 * a + r - 1, (x, y, 1 - c)))
        return copies

    return _Plan(gathered, [jax.ShapeDtypeStruct(g.shape, g.dtype) for g in gathered], {a: a for a in range(n)},
                 3 * n, build)


def _plan_swap(split):
    n = len(split)

    def build(ins, outs, send_sems, recv_sems, base):
        x, y, c = _place()
        return [_remote(ins[a].at[j, 1 - c], outs[a].at[j], send_sems, recv_sems, base + N_CHIPS * a + j, (x, y, 1 - c))
                for a in range(n) for j in range(N_CHIPS)]

    return _Plan(split, [jax.ShapeDtypeStruct((N_CHIPS,) + g.shape[2:], g.dtype) for g in split], {}, N_CHIPS * n, build)


def _plan_scatter(parts):
    n = len(parts)

    def build(ins, outs, send_sems, recv_sems, base):
        x, y, c = _place()
        copies = []
        for a in range(n):
            for r in (1, 2, 3):
                px, py, shard = _chip_rel(x, y, r)
                copies.append(_remote(ins[a].at[shard], outs[a].at[r - 1], send_sems, recv_sems, base + 3 * a + r - 1,
                                      (px, py, c)))
        return copies

    return _Plan(parts, [jax.ShapeDtypeStruct((3,) + p.shape[1:], p.dtype) for p in parts], {}, 3 * n, build)


def _plan_join(bufs):
    n = len(bufs)

    def build(ins, outs, send_sems, recv_sems, base):
        x, y, c = _place()
        return [_remote(outs[a].at[c], outs[a].at[c], send_sems, recv_sems, base + a, (x, y, 1 - c)) for a in range(n)]

    return _Plan(bufs, [jax.ShapeDtypeStruct(b.shape, b.dtype) for b in bufs], {a: a for a in range(n)}, n, build)


def _carried_call(kern, plan, *, name, grid, in_specs, out_specs, out_shape, operands, scratch_shapes=(),
                  aliases=None, sem=None):
    n_in, n_out, n_sc = len(operands), len(out_shape), len(scratch_shapes)
    aliases = dict(aliases or {})
    if plan is None:
        outs = pl.pallas_call(kern, name=name, grid=grid, in_specs=in_specs, out_specs=out_specs, out_shape=out_shape,
                              scratch_shapes=list(scratch_shapes), input_output_aliases=aliases,
                              compiler_params=_params(sem))(*operands)
        return list(outs), []
    ci, co = len(plan.ins), len(plan.out_shapes)
    for k, v in plan.aliases.items():
        aliases[n_in + k] = n_out + v
    steps = tuple(grid)

    def body(*refs):
        ins, cins = refs[:n_in], refs[n_in:n_in + ci]
        outs = refs[n_in + ci:n_in + ci + n_out]
        couts = refs[n_in + ci + n_out:n_in + ci + n_out + co]
        scratch = refs[n_in + ci + n_out + co:n_in + ci + n_out + co + n_sc]
        send_sems, recv_sems = refs[-2], refs[-1]
        first = functools.reduce(jnp.logical_and, [pl.program_id(a) == 0 for a in range(len(steps))])
        last = functools.reduce(jnp.logical_and, [pl.program_id(a) == steps[a] - 1 for a in range(len(steps))])

        @pl.when(first)
        def _():
            for cp in plan.build(cins, couts, send_sems, recv_sems, 0):
                cp.start()

        kern(*ins, *outs, *scratch)

        @pl.when(last)
        def _():
            for cp in plan.build(cins, couts, send_sems, recv_sems, 0):
                cp.wait()

    anyspec = pl.BlockSpec(memory_space=pl.ANY)
    outs = pl.pallas_call(
        body, name=name, grid=grid, in_specs=list(in_specs) + [anyspec] * ci,
        out_specs=list(out_specs) + [anyspec] * co, out_shape=list(out_shape) + plan.out_shapes,
        scratch_shapes=list(scratch_shapes) + [pltpu.SemaphoreType.DMA((plan.n_sems,)),
                                               pltpu.SemaphoreType.DMA((plan.n_sems,))],
        input_output_aliases=aliases,
        compiler_params=_params(("arbitrary",) * len(steps)),
    )(*operands, *plan.ins)
    return list(outs[:n_out]), list(outs[n_out:])


def _chip_rel(x, y, r):
    px = x if r < 2 else 1 - x
    py = y if r % 2 == 0 else 1 - y
    return px, py, 2 * px + py


def _allgather_whole(arr, name):
    def body(in_ref, out_ref, send_sems, recv_sems, loc_sem):
        x, y, c = _place()
        me = 2 * x + y
        local = pltpu.make_async_copy(in_ref, out_ref.at[me], loc_sem)
        local.start()
        sends = []
        for r in (1, 2, 3):
            px, py, _ = _chip_rel(x, y, r)
            sends.append(pltpu.make_async_remote_copy(
                src_ref=in_ref, dst_ref=out_ref.at[me], send_sem=send_sems.at[r - 1], recv_sem=recv_sems.at[r - 1],
                device_id=(px, py, c), device_id_type=MESH))
        for cp in sends:
            cp.start()
        for r in (1, 2, 3):
            px, py, shard = _chip_rel(x, y, r)
            pltpu.make_async_remote_copy(
                src_ref=in_ref, dst_ref=out_ref.at[shard], send_sem=send_sems.at[r - 1], recv_sem=recv_sems.at[r - 1],
                device_id=(px, py, c), device_id_type=MESH).wait_recv()
        for cp in sends:
            cp.wait_send()
        local.wait()

    anyspec = pl.BlockSpec(memory_space=pl.ANY)
    return pl.pallas_call(
        body, name=name, in_specs=[anyspec], out_specs=anyspec,
        out_shape=jax.ShapeDtypeStruct((N_CHIPS,) + arr.shape, arr.dtype),
        scratch_shapes=[pltpu.SemaphoreType.DMA((3,)), pltpu.SemaphoreType.DMA((3,)), pltpu.SemaphoreType.DMA],
        compiler_params=pltpu.CompilerParams(has_side_effects=True),
    )(arr)


def _plan_gather_all(buf):
    def build(ins, outs, send_sems, recv_sems, base):
        x, y, c = _place()
        me = 4 * x + 2 * y + c
        copies = []
        for r in range(1, N_DEV):
            px, py, _ = _chip_rel(x, y, r // 2)
            pc = c if r % 2 == 0 else 1 - c
            copies.append(_remote(ins[0], outs[0].at[me], send_sems, recv_sems, base + r - 1, (px, py, pc)))
        return copies

    return _Plan([buf, jnp.zeros((N_DEV,) + buf.shape, buf.dtype)],
                 [jax.ShapeDtypeStruct((N_DEV,) + buf.shape, buf.dtype)], {1: 0}, N_DEV - 1, build)


def _add_half(grad, recv, sel, name):
    _, _, rh, cw = grad.shape
    rb = _pick(rh, (512, 256, 176, 128, 64, 32, 16, 8))

    def kern(sel_ref, g_ref, r_ref, o_ref):
        o_ref[...] = (g_ref[...] + r_ref[...]).astype(BF16)

    return pl.pallas_call(
        kern, name=name,
        grid_spec=pltpu.PrefetchScalarGridSpec(
            num_scalar_prefetch=1, grid=(N_CHIPS - 1, rh // rb),
            in_specs=[pl.BlockSpec((None, None, rb, cw), lambda j, i, s: (s[2 + j], s[0], i, 0)),
                      pl.BlockSpec((None, rb, cw), lambda j, i, s: (s[2 + j], i, 0))],
            out_specs=pl.BlockSpec((None, rb, cw), lambda j, i, s: (s[2 + j], i, 0))),
        out_shape=jax.ShapeDtypeStruct((N_CHIPS, rh, cw), BF16),
        compiler_params=_params(("parallel", "parallel")),
    )(sel, grad, recv)


def _add_own(grad, recv, got, sel, name):
    _, _, rh, cw = grad.shape
    rb = _pick(rh, (512, 256, 176, 128, 64, 32, 16, 8))

    def kern(sel_ref, g_ref, r_ref, b_ref, o_ref):
        own = g_ref[...] + r_ref[...]
        o_ref[...] = ((own + b_ref[0].astype(F32)) + b_ref[1].astype(F32)) + b_ref[2].astype(F32)

    return pl.pallas_call(
        kern, name=name,
        grid_spec=pltpu.PrefetchScalarGridSpec(
            num_scalar_prefetch=1, grid=(rh // rb,),
            in_specs=[pl.BlockSpec((None, None, rb, cw), lambda i, s: (s[1], s[0], i, 0)),
                      pl.BlockSpec((None, rb, cw), lambda i, s: (s[1], i, 0)),
                      pl.BlockSpec((3, rb, cw), lambda i, s: (0, i, 0))],
            out_specs=pl.BlockSpec((None, rb, cw), lambda i, s: (s[0], i, 0))),
        out_shape=jax.ShapeDtypeStruct((2, rh, cw), F32),
        compiler_params=_params(("parallel",)),
    )(sel, grad, recv, got)


def _stacked(g):
    return g.reshape(1, g.shape[0] * g.shape[1], g.shape[2])


def _halves(g):
    g = g.reshape(N_CHIPS, g.shape[0] * g.shape[1] // N_CHIPS, g.shape[2])
    return g.reshape(N_CHIPS, 2, g.shape[1] // 2, g.shape[2])


def _whole(f):
    return f.reshape(f.shape[0] * f.shape[1], f.shape[2])


def _step(x3, tgt, sm, w, mom, var):
    x, y, c = _place()
    sel = jnp.stack([c, 2 * x + y] + [_chip_rel(x, y, r)[2] for r in (1, 2, 3)]).astype(jnp.int32)
    wg = {}
    x2 = x3[0]
    cast, landed = _cast_bf16(
        [(x3, 0), (w["hg_w_out"], 0), (w["sg_w_in"], 0), (w["sg_w_out"], 0), (w["ffn_w_up"], 0), (w["ffn_w_up"], 1),
         (w["ffn_w_down"], 0), (w["ffn_w_down"], 1)], "cast_shards",
        comm=_plan_gather_ici([w["hg_w_in"][0].astype(BF16)]))
    xb = cast[0]
    sh = dict(zip(("hg_out", "sg_in", "sg_out", "up0", "up1", "dn0", "dn1"), cast[1:]))
    wg["hg_in"] = _comm_call(_plan_gather_pass(landed), "gather_hg_in_pass")[0]
    proj, landed = _matmul(xb, wg["hg_in"], mode="nn", nsh=N_CHIPS, name="hg_in",
                           comm=_plan_gather_ici([sh["hg_out"], sh["sg_in"]]))
    (yhg, o_raw, states), got = _hgrn2_fwd(
        proj, sm["lb_logits"], sm["hg_norm_g"], "hgrn2_fwd",
        comm=_merge(_plan_gather_pass(landed), _plan_gather_ici([sh["up0"], sh["up1"]])))
    wg["hg_out"], wg["sg_in"], landed = got[0], got[1], got[2:]
    xin1, got = _matmul(yhg, _stacked(wg["hg_out"]), mode="nn", nsh=1, resid=x2, alpha=ALPHA, name="hg_out",
                        comm=_plan_gather_pass(landed))
    wg["up0"], wg["up1"] = got
    h1, h1b = _ln_fwd(xin1, sm["ln1_g"][0:1], sm["ln1_b"][0:1], "l0_ln1")
    u0, landed = _matmul(h1b, wg["up0"], mode="nn", nsh=N_CHIPS, name="l0_ffn_up",
                         comm=_plan_gather_ici([sh["dn0"], sh["sg_out"]]))
    gact0, got = _conv_gate_fwd(u0, sm["conv_w"][0], sm["conv_b"][0:1], "l0_ffn_gate", comm=_plan_gather_pass(landed))
    wg["dn0"], wg["sg_out"] = got
    xin2, landed = _matmul(gact0, _stacked(wg["dn0"]), mode="nn", nsh=1, resid=h1, alpha=ALPHA, name="l0_ffn_down",
                           comm=_plan_gather_ici([sh["dn1"]]))
    h2, h2b = _ln_fwd(xin2, sm["ln2_g"][0:1], sm["ln2_b"][0:1], "l0_ffn_ln")
    pre, got = _matmul(h2b, wg["sg_in"], mode="nn", nsh=N_CHIPS, name="sg_in", comm=_plan_gather_pass(landed))
    wg["dn1"] = got[0]
    ysg = _sg_gate_fwd(pre, sm["sg_ln_g"], sm["sg_ln_b"], sm["sg_w_s"], sm["sg_b_s_t"], "sg_gate")
    xin3 = _matmul(ysg, _stacked(wg["sg_out"]), mode="nn", nsh=1, resid=h2, alpha=ALPHA, name="sg_out")
    h3, h3b = _ln_fwd(xin3, sm["ln1_g"][1:2], sm["ln1_b"][1:2], "l1_ln1")
    u1 = _matmul(h3b, wg["up1"], mode="nn", nsh=N_CHIPS, name="l1_ffn_up")
    gact1 = _conv_gate_fwd(u1, sm["conv_w"][1], sm["conv_b"][1:2], "l1_ffn_gate")
    xin4 = _matmul(gact1, _stacked(wg["dn1"]), mode="nn", nsh=1, resid=h3, alpha=ALPHA, name="l1_ffn_down")

    gs, grad, split, recv, part = {}, {}, {}, {}, {}

    def swap_on(call, keys):
        for k in keys:
            split[k] = _halves(grad[k])
        out, got = call(_plan_swap([split[k] for k in keys]))
        for k, r in zip(keys, got):
            recv[k] = r
            part[k] = _add_half(split[k], r, sel, f"rs_addhalf_{k}")
        return out

    def ffn_bwd(u, gact, hb_in, dxin, dxin_b, w_up, w_down, layer, tag, up, down, waiting):
        dgact = _matmul(dxin_b, _stacked(w_down), mode="nt", nsh=1, name=f"{tag}_ddown")
        grad[down] = _matmul(gact, dxin_b, mode="tn", nsh=1, name=f"{tag}_wdown")
        da, db, dcw, dcb = _conv_gate_bwd(u, dgact, sm["conv_w"][layer], sm["conv_b"][layer:layer + 1], f"{tag}_dgate")
        grad[up] = swap_on(lambda plan: _matmul(hb_in, [da, db], mode="tn", nsh=N_CHIPS, name=f"{tag}_wup", comm=plan),
                           waiting + [down])
        dh = swap_on(lambda plan: _matmul([da, db], w_up, mode="nt", nsh=N_CHIPS, resid=dxin, alpha=ALPHA,
                                          name=f"{tag}_dup", comm=plan), [up])
        return dh, dcw, dcb

    dx, dxb, dg4, db4, loss = _ln_bwd(xin4, tgt, sm["ln2_g"][1:2], sm["ln2_b"][1:2], "l1_ln2_bwd", loss_head=True)
    dh3, dcw1, dcb1 = ffn_bwd(u1, gact1, h3b, dx, dxb, wg["up1"], wg["dn1"], 1, "l1_ffn", "up1", "dn1", [])
    dx, dxb, dg3, db3 = _ln_bwd(xin3, dh3, sm["ln1_g"][1:2], sm["ln1_b"][1:2], "l1_ln1_bwd")
    grad["sg_out"] = _matmul(ysg, dxb, mode="tn", nsh=1, name="sg_wout")
    dysg = swap_on(lambda plan: _matmul(dxb, _stacked(wg["sg_out"]), mode="nt", nsh=1, name="sg_dout", comm=plan),
                   ["sg_out"])
    dpre, gs["sg_w_s"], gs["sg_b_s_t"], gs["sg_ln_g"], gs["sg_ln_b"] = _sg_gate_bwd(
        pre, dysg, sm["sg_ln_g"], sm["sg_ln_b"], sm["sg_w_s"], sm["sg_b_s_t"], "sg_gate_bwd")
    grad["sg_in"] = _matmul(h2b, dpre, mode="tn", nsh=N_CHIPS, name="sg_win")
    dh2 = _matmul(dpre, wg["sg_in"], mode="nt", nsh=N_CHIPS, resid=dx, alpha=ALPHA, name="sg_din")
    dx, dxb, dg2, db2 = _ln_bwd(xin2, dh2, sm["ln2_g"][0:1], sm["ln2_b"][0:1], "l0_ln2_bwd")
    dh1, dcw0, dcb0 = ffn_bwd(u0, gact0, h1b, dx, dxb, wg["up0"], wg["dn0"], 0, "l0_ffn", "up0", "dn0", ["sg_in"])
    dx, dxb, dg1, db1 = _ln_bwd(xin1, dh1, sm["ln1_g"][0:1], sm["ln1_b"][0:1], "l0_ln1_bwd")
    grad["hg_out"] = _matmul(yhg, dxb, mode="tn", nsh=1, name="hg_wout")
    dyhg = swap_on(lambda plan: _matmul(dxb, _stacked(wg["hg_out"]), mode="nt", nsh=1, name="hg_dout", comm=plan),
                   ["hg_out"])
    early = ("dn1", "up1", "sg_out", "sg_in", "dn0", "up0", "hg_out")
    dparts, got = _hgrn2_bwd(proj, sm["lb_logits"], sm["hg_norm_g"], o_raw, states, dyhg, "hgrn2_bwd",
                             comm=_plan_scatter([part[k] for k in early]))
    gs["lb"], gs["hg_norm_g"] = dparts[4], dparts[5]
    gs["ln1_g"] = jnp.concatenate([dg1, dg3], axis=0)
    gs["ln1_b"] = jnp.concatenate([db1, db3], axis=0)
    gs["ln2_g"] = jnp.concatenate([dg2, dg4], axis=0)
    gs["ln2_b"] = jnp.concatenate([db2, db4], axis=0)
    gs["conv_w"] = jnp.stack([dcw0, dcw1], axis=0)
    gs["conv_b"] = jnp.concatenate([dcb0, dcb1], axis=0)
    packed, layout = _pack(gs)
    mine = [_add_own(split[k], recv[k], b, sel, f"rs_addown_{k}") for k, b in zip(early, got)]
    kh = xb.shape[1] // 2
    grad["hg_in_a"], got = _matmul(xb[:, :kh], list(dparts[:4]), mode="tn", nsh=N_CHIPS, name="hg_win_a",
                                   comm=_merge(_plan_join(mine), _plan_gather_all(packed)))
    red = {k: _whole(f) for k, f in zip(early, got)}
    me8 = jnp.reshape(4 * x + 2 * y + c, (1,)).astype(jnp.int32)
    summed = _unpack(_sum_devices(got[len(early)], packed, me8, "sum_small_grads"), layout)
    grad["hg_in_b"] = swap_on(lambda plan: _matmul(xb[:, kh:], list(dparts[:4]), mode="tn", nsh=N_CHIPS,
                                                   name="hg_win_b", comm=plan), ["hg_in_a"])
    split["hg_in_b"] = _halves(grad["hg_in_b"])
    gx, got = _matmul(dparts[0], wg["hg_in"], mode="nt", nsh=1, b_off=0, resid=dx, alpha=ALPHA, name="hg_din_q",
                      comm=_merge(_plan_swap([split["hg_in_b"]]), _plan_scatter([part["hg_in_a"]])))
    recv["hg_in_b"] = got[0]
    part["hg_in_b"] = _add_half(split["hg_in_b"], recv["hg_in_b"], sel, "rs_addhalf_hg_in_b")
    mine_a = _add_own(split["hg_in_a"], recv["hg_in_a"], got[1], sel, "rs_addown_hg_in_a")
    gx, got = _matmul(list(dparts[1:4]), wg["hg_in"], mode="nt", nsh=3, b_off=1, resid=gx, alpha=1.0, name="hg_din_fig",
                      comm=_merge(_plan_scatter([part["hg_in_b"]]), _plan_join([mine_a])))
    mine_b = _add_own(split["hg_in_b"], recv["hg_in_b"], got[0], sel, "rs_addown_hg_in_b")
    red["hg_in_a"] = _whole(got[1])
    upd = {}
    upd["hg_w_out"], full = _adamw(w["hg_w_out"], [red["hg_out"]], mom["hg_w_out"], var["hg_w_out"], "adamw_hg_w_out",
                                   comm=_plan_join([mine_b]))
    red["hg_in_b"] = _whole(full[0])
    for k, src in (("ffn_w_up", ("up0", "up1")), ("ffn_w_down", ("dn0", "dn1")), ("sg_w_in", ("sg_in",)),
                   ("sg_w_out", ("sg_out",))):
        upd[k] = _adamw(w[k], [red[s] for s in src], mom[k], var[k], f"adamw_{k}")
    rows_ab = (2, kh, w["hg_w_in"].shape[2])
    outs = _adamw(w["hg_w_in"].reshape(rows_ab), [red["hg_in_a"], red["hg_in_b"]], mom["hg_w_in"].reshape(rows_ab),
                  var["hg_w_in"].reshape(rows_ab), "adamw_hg_w_in")
    upd["hg_w_in"] = [o.reshape(w["hg_w_in"].shape) for o in outs]
    return loss, gx, summed, upd


_SMALL_ORDER = ("lb", "hg_norm_g", "sg_w_s", "sg_b_s_t", "conv_b", "ln1_g", "ln1_b", "ln2_g", "ln2_b",
                "conv_w", "sg_ln_g", "sg_ln_b")


PACK_ROWS = 512


def _pack(parts):
    flat, layout, off = [], [], 0
    for k in _SMALL_ORDER:
        a = parts[k]
        n = a.size
        pad = (-n) % LANES
        flat.append(jnp.pad(a.reshape(-1), (0, pad)))
        layout.append((k, off, n, a.shape))
        off += n + pad
    flat.append(jnp.zeros(((-off) % (PACK_ROWS * LANES),), F32))
    return jnp.concatenate(flat).reshape(-1, LANES), layout


def _unpack(buf, layout):
    flat = buf.reshape(-1)
    return {k: flat[off:off + n].reshape(shape) for k, off, n, shape in layout}


def kernel(x, lb_logits, hg_w_in, hg_norm_g, hg_w_out, sg_w_in, sg_ln_g, sg_ln_b, sg_w_s, sg_b_s, sg_w_out, ffn_w_up, ffn_conv_w, ffn_conv_b, ffn_w_down, ln1_g, ln1_b, ln2_g, ln2_b, loss_target, m_lb_logits, m_hg_w_in, m_hg_norm_g, m_hg_w_out, m_sg_w_in, m_sg_ln_g, m_sg_ln_b, m_sg_w_s, m_sg_b_s, m_sg_w_out, m_ffn_w_up, m_ffn_conv_w, m_ffn_conv_b, m_ffn_w_down, m_ln1_g, m_ln1_b, m_ln2_g, m_ln2_b, v_lb_logits, v_hg_w_in, v_hg_norm_g, v_hg_w_out, v_sg_w_in, v_sg_ln_g, v_sg_ln_b, v_sg_w_s, v_sg_b_s, v_sg_w_out, v_ffn_w_up, v_ffn_conv_w, v_ffn_conv_b, v_ffn_w_down, v_ln1_g, v_ln1_b, v_ln2_g, v_ln2_b):
    names = ("lb_logits", "hg_w_in", "hg_norm_g", "hg_w_out", "sg_w_in", "sg_ln_g", "sg_ln_b", "sg_w_s", "sg_b_s",
             "sg_w_out", "ffn_w_up", "ffn_conv_w", "ffn_conv_b", "ffn_w_down", "ln1_g", "ln1_b", "ln2_g", "ln2_b")
    w = dict(zip(names, (lb_logits, hg_w_in, hg_norm_g, hg_w_out, sg_w_in, sg_ln_g, sg_ln_b, sg_w_s, sg_b_s,
                         sg_w_out, ffn_w_up, ffn_conv_w, ffn_conv_b, ffn_w_down, ln1_g, ln1_b, ln2_g, ln2_b)))
    mom = dict(zip(names, (m_lb_logits, m_hg_w_in, m_hg_norm_g, m_hg_w_out, m_sg_w_in, m_sg_ln_g, m_sg_ln_b, m_sg_w_s,
                           m_sg_b_s, m_sg_w_out, m_ffn_w_up, m_ffn_conv_w, m_ffn_conv_b, m_ffn_w_down, m_ln1_g,
                           m_ln1_b, m_ln2_g, m_ln2_b)))
    var = dict(zip(names, (v_lb_logits, v_hg_w_in, v_hg_norm_g, v_hg_w_out, v_sg_w_in, v_sg_ln_g, v_sg_ln_b, v_sg_w_s,
                           v_sg_b_s, v_sg_w_out, v_ffn_w_up, v_ffn_conv_w, v_ffn_conv_b, v_ffn_w_down, v_ln1_g,
                           v_ln1_b, v_ln2_g, v_ln2_b)))
    tgt = loss_target[0]
    fq = ffn_conv_w.shape[2]
    dq = sg_ln_g.shape[1]
    cx, cy, _ = _place()
    me = 2 * cx + cy

    wide = max(fq, dq)
    tiny = jnp.concatenate([jnp.pad(ffn_conv_w.reshape(6, fq), ((0, 0), (0, wide - fq))),
                            jnp.pad(sg_ln_g, ((0, 0), (0, wide - dq))),
                            jnp.pad(sg_ln_b, ((0, 0), (0, wide - dq)))], axis=0)
    tiny_all = _allgather_whole(tiny, "gather_small")
    conv_w_full = jnp.transpose(tiny_all[:, 0:6, :fq].reshape(N_CHIPS, 2, 3, fq), (1, 2, 0, 3)).reshape(2, 3, N_CHIPS * fq)
    sm = {"lb_logits": lb_logits, "hg_norm_g": hg_norm_g, "ln1_g": ln1_g, "ln1_b": ln1_b, "ln2_g": ln2_g,
          "ln2_b": ln2_b, "conv_w": conv_w_full, "conv_b": ffn_conv_b,
          "sg_ln_g": tiny_all[:, 6, :dq].reshape(1, N_CHIPS * dq),
          "sg_ln_b": tiny_all[:, 7, :dq].reshape(1, N_CHIPS * dq),
          "sg_w_s": sg_w_s[0], "sg_b_s_t": jnp.transpose(sg_b_s[0])}

    loss_row, grad_x, summed, upd = _step(x, tgt, sm, w, mom, var)
    loss = lax.psum(loss_row[0, 0], ("x", "y", "c"))

    grads = {
        "lb_logits": _lb_logits_grad(lb_logits, summed["lb"], "lb_logits_grad"),
        "hg_norm_g": summed["hg_norm_g"],
        "sg_ln_g": lax.dynamic_slice_in_dim(summed["sg_ln_g"], me * dq, dq, axis=1),
        "sg_ln_b": lax.dynamic_slice_in_dim(summed["sg_ln_b"], me * dq, dq, axis=1),
        "sg_w_s": summed["sg_w_s"][None], "sg_b_s": jnp.transpose(summed["sg_b_s_t"])[None],
        "ffn_conv_w": lax.dynamic_slice_in_dim(summed["conv_w"], me * fq, fq, axis=2),
        "ffn_conv_b": summed["conv_b"],
        "ln1_g": summed["ln1_g"], "ln1_b": summed["ln1_b"], "ln2_g": summed["ln2_g"], "ln2_b": summed["ln2_b"],
    }

    delta, new_m, new_v = {}, {}, {}
    for k, (dlt, mm, vv, gg) in upd.items():
        delta[k], new_m[k], new_v[k], grads[k] = dlt, mm, vv, gg
    small_names = [k for k in names if k not in upd]

    def pack_small(src):
        flat = [src[k].reshape(-1) for k in small_names]
        n = sum(a.size for a in flat)
        flat.append(jnp.zeros(((-n) % (PACK_ROWS * LANES),), F32))
        return jnp.concatenate(flat).reshape(1, -1, LANES)

    outs = _adamw(pack_small(w), [pack_small(grads)[0]], pack_small(mom), pack_small(var), "adamw_small")
    off = 0
    for k in small_names:
        n = w[k].size
        for dst, o in zip((delta, new_m, new_v), outs):
            dst[k] = o.reshape(-1)[off:off + n].reshape(w[k].shape)
        off += n

    return (loss, grad_x[None], *[grads[k] for k in names], *[delta[k] for k in names],
            *[new_m[k] for k in names], *[new_v[k] for k in names])
```

```python
import functools

import jax
import jax.numpy as jnp
from jax import lax
from jax.experimental import pallas as pl
from jax.experimental.pallas import tpu as pltpu

F32 = jnp.float32
BF16 = jnp.bfloat16
MESH = pl.DeviceIdType.MESH

ALPHA = (2 * 2) ** 0.25
LN_EPS = 1e-5
RMS_EPS = 1e-6
ADAM_LR, ADAM_B1, ADAM_B2, ADAM_EPS, ADAM_WD, ADAM_STEP = 0.001, 0.9, 0.999, 1e-08, 0.01, 10

LANES = 128
SUB = 16
TILE = 8
GCHUNK = 128
VMEM_LIMIT = 56 * 1024 * 1024
N_CHIPS = 4
N_DEV = 8

NT = (((1,), (1,)), ((), ()))
TN = (((0,), (0,)), ((), ()))
NN = (((1,), (0,)), ((), ()))


def _pick(dim, prefs):
    for p in prefs:
        if dim % p == 0:
            return p
    return dim


def _params(sem=None, **kw):
    return pltpu.CompilerParams(dimension_semantics=sem, vmem_limit_bytes=VMEM_LIMIT, **kw)


def _sigmoid_pair(x):
    e = jnp.exp(-jnp.abs(x))
    inv = 1.0 / (1.0 + e)
    pos = x >= 0
    return jnp.where(pos, inv, e * inv), jnp.where(pos, e * inv, inv)


def _sigmoid_gate(x):
    t = 0.5 * jnp.tanh(0.5 * x)
    return 0.5 + t, 0.5 - t


def _ln_hat(x):
    mu = jnp.mean(x, axis=-1, keepdims=True)
    xc = x - mu
    var = jnp.mean(xc * xc, axis=-1, keepdims=True)
    rstd = lax.rsqrt(var + LN_EPS)
    return xc * rstd, rstd


def _lower_bound(logits):
    m = jnp.max(logits, axis=0, keepdims=True)
    e = jnp.exp(logits - m)
    return e[0:1, :] / jnp.sum(e, axis=0, keepdims=True)


MATMUL_VMEM_BUDGET = 40 * 1024 * 1024


def _fit_bk(kdim, bm, bn, out_dtype, has_resid, na=1, nb=1):
    fixed = bm * bn * (4 + 2 * jnp.dtype(out_dtype).itemsize + (8 if has_resid else 0))
    best = LANES
    for bk in range(LANES, kdim + 1, LANES):
        if kdim % bk == 0 and fixed + 4 * bk * (bm * na + bn * nb) <= MATMUL_VMEM_BUDGET:
            best = bk
    return best


def _fit_bm_bk(mdim, prefs, kdim, bn, out_dtype, has_resid, na=1, nb=1):
    best = None
    fits = [bm for bm in prefs if mdim % bm == 0][:2] or [mdim]
    for bm in fits:
        bk = _fit_bk(kdim, bm, bn, out_dtype, has_resid, na, nb)
        if best is None or kdim // bk < kdim // best[1]:
            best = (bm, bk)
    return best


def _matmul(a, b, *, mode, name, out_dtype=F32, resid=None, alpha=1.0, b_off=0, nsh=None, comm=None):
    a_parts = a if isinstance(a, (list, tuple)) else [a]
    b_parts = b if isinstance(b, (list, tuple)) else [b]
    n_parts = max(len(a_parts), len(b_parts))
    if mode == "nn":
        m, kdim = a.shape
        _, _, ns = b.shape
        bn = _pick(ns, (1024, 1408, 512, 256, 128))
        bm, bk = _fit_bm_bk(m, (1024, 512, 256, 128), kdim, bn, out_dtype, resid is not None)
        nps = ns // bn
        grid = (m // bm, nsh * nps, kdim // bk)
        a_specs = [pl.BlockSpec((bm, bk), lambda i, j, k: (i, k))]
        b_specs = [pl.BlockSpec((None, bk, bn), lambda i, j, k: (b_off + j // nps, k, j % nps))]
        o_spec = pl.BlockSpec((bm, bn), lambda i, j, k: (i, j))
        out_shape = jax.ShapeDtypeStruct((m, nsh * ns), out_dtype)
        dims, part_axis, per_part = NN, 2, grid[2]
    elif mode == "nt":
        m = a_parts[0].shape[0]
        _, kdim, ns = b.shape
        wide = n_parts > 1
        bn = _pick(kdim, (1024, 1408, 512, 256, 128) if wide else (512, 256, 128))
        bm, bk = _fit_bm_bk(m, (1024, 512, 256, 128) if wide else (2048, 1024, 512, 256, 128), ns, bn, out_dtype,
                            resid is not None, na=n_parts)
        kps = ns // bk
        per_part = nsh // n_parts * kps
        grid = (m // bm, kdim // bn, nsh * kps)
        a_specs = [pl.BlockSpec((bm, bk), lambda i, j, k, p=p: (jnp.where(k // per_part == p, i, 0),
                                                                  jnp.where(k // per_part == p, k % per_part, 0)))
                   for p in range(n_parts)]
        b_specs = [pl.BlockSpec((None, bn, bk), lambda i, j, k: (b_off + k // kps, j, k % kps))]
        o_spec = pl.BlockSpec((bm, bn), lambda i, j, k: (i, j))
        out_shape = jax.ShapeDtypeStruct((m, kdim), out_dtype)
        dims, part_axis = NT, 2
    else:
        t, kdim = a.shape
        ns = b_parts[0].shape[1] * n_parts // nsh
        bn = _pick(ns, (1024, 1408, 512, 256, 128))
        bm, bk = _fit_bm_bk(kdim, (1024, 1408, 512, 256, 128), t, bn, out_dtype, resid is not None, nb=n_parts)
        nps = ns // bn
        per_part = nsh // n_parts * nps
        grid = (kdim // bm, nsh * nps, t // bk)
        a_specs = [pl.BlockSpec((bk, bm), lambda i, j, k: (k, i))]
        b_specs = [pl.BlockSpec((bk, bn), lambda i, j, k, p=p: (jnp.where(j // per_part == p, k, 0),
                                                                  jnp.where(j // per_part == p, j % per_part, 0)))
                   for p in range(n_parts)]
        o_spec = pl.BlockSpec((None, bm, bn), lambda i, j, k: (j // nps, i, j % nps))
        out_shape = jax.ShapeDtypeStruct((nsh, kdim, ns), out_dtype)
        dims, part_axis = TN, 1
    nk = grid[2]
    na, nb_ = len(a_parts), len(b_parts)
    has_resid = resid is not None

    def kern(*refs):
        a_refs, b_refs = refs[:na], refs[na:na + nb_]
        r_ref = refs[na + nb_] if has_resid else None
        k = pl.program_id(2)

        def finish(r, o_ref):
            if has_resid:
                r = r + alpha * r_ref[...]
            o_ref[...] = r.astype(o_ref.dtype)

        def add(a_ref, b_ref):
            if nk == 1:
                finish(lax.dot_general(a_ref[...], b_ref[...], dims, preferred_element_type=F32), refs[-1])
                return
            refs[-1][...] += lax.dot_general(a_ref[...], b_ref[...], dims, preferred_element_type=F32)

        if nk > 1:
            @pl.when(k == 0)
            def _():
                refs[-1][...] = jnp.zeros_like(refs[-1])

        if n_parts == 1:
            add(a_refs[0], b_refs[0])
        else:
            which = pl.program_id(part_axis) // per_part
            for p in range(n_parts):
                pl.when(which == p)(functools.partial(add, a_refs[min(p, na - 1)], b_refs[min(p, nb_ - 1)]))
        if nk > 1:
            @pl.when(k == nk - 1)
            def _():
                finish(refs[-1][...], refs[-2])

    in_specs = a_specs + b_specs
    operands = list(a_parts) + list(b_parts)
    if has_resid:
        in_specs.append(pl.BlockSpec((bm, bn), lambda i, j, k: (i, j)))
        operands.append(resid)
    outs, carried = _carried_call(
        kern, comm, name=name, grid=grid, in_specs=in_specs, out_specs=[o_spec], out_shape=[out_shape],
        operands=operands, scratch_shapes=[pltpu.VMEM((bm, bn), F32)] if nk > 1 else [],
        sem=("parallel", "parallel", "arbitrary"))
    return outs[0] if comm is None else (outs[0], carried)


def _ln_fwd(xin, g, b, name):
    t, d = xin.shape
    tb = _pick(t, (256, 128, 64, 32, 16))

    def kern(x_ref, g_ref, b_ref, h_ref, hb_ref):
        xhat, _ = _ln_hat(x_ref[...])
        h = xhat * g_ref[...] + b_ref[...]
        h_ref[...] = h
        hb_ref[...] = h.astype(BF16)

    row = pl.BlockSpec((tb, d), lambda i: (i, 0))
    vec = pl.BlockSpec((1, d), lambda i: (0, 0))
    return pl.pallas_call(
        kern, name=name, grid=(t // tb,), in_specs=[row, vec, vec], out_specs=[row, row],
        out_shape=[jax.ShapeDtypeStruct((t, d), F32), jax.ShapeDtypeStruct((t, d), BF16)],
        compiler_params=_params(("parallel",)),
    )(xin, g, b)


def _ln_bwd(xin, dy_or_target, g, b, name, loss_head=False):
    t, d = xin.shape
    tb = _pick(t, (256, 128, 64, 32, 16))
    nb = t // tb

    def kern(x_ref, dy_ref, g_ref, b_ref, dx_ref, dxb_ref, dg_ref, db_ref, *rest):
        i = pl.program_id(0)
        xhat, rstd = _ln_hat(x_ref[...])
        gv = g_ref[...]
        if loss_head:
            loss_ref = rest[0]
            err = xhat * gv + b_ref[...] - dy_ref[...]
            dy = err * (1.0 / d)
            part = 0.5 * jnp.sum(jnp.sum(err * err, axis=1, keepdims=True), axis=0, keepdims=True) * (1.0 / d)
        else:
            dy = dy_ref[...]

        @pl.when(i == 0)
        def _():
            dg_ref[...] = jnp.zeros_like(dg_ref)
            db_ref[...] = jnp.zeros_like(db_ref)
            if loss_head:
                loss_ref[...] = jnp.zeros_like(loss_ref)

        dg_ref[...] += jnp.sum(dy * xhat, axis=0, keepdims=True)
        db_ref[...] += jnp.sum(dy, axis=0, keepdims=True)
        if loss_head:
            loss_ref[...] += jnp.broadcast_to(part, loss_ref.shape)
        dxh = dy * gv
        m1 = jnp.mean(dxh, axis=-1, keepdims=True)
        m2 = jnp.mean(dxh * xhat, axis=-1, keepdims=True)
        dx = rstd * (dxh - m1 - xhat * m2)
        dx_ref[...] = dx
        dxb_ref[...] = dx.astype(BF16)

    row = pl.BlockSpec((tb, d), lambda i: (i, 0))
    vec = pl.BlockSpec((1, d), lambda i: (0, 0))
    out_specs = [row, row, vec, vec]
    out_shape = [jax.ShapeDtypeStruct((t, d), F32), jax.ShapeDtypeStruct((t, d), BF16),
                 jax.ShapeDtypeStruct((1, d), F32), jax.ShapeDtypeStruct((1, d), F32)]
    if loss_head:
        out_specs.append(pl.BlockSpec((1, LANES), lambda i: (0, 0)))
        out_shape.append(jax.ShapeDtypeStruct((1, LANES), F32))
    return pl.pallas_call(
        kern, name=name, grid=(nb,), in_specs=[row, row, vec, vec], out_specs=out_specs, out_shape=out_shape,
        compiler_params=_params(("arbitrary",)),
    )(xin, dy_or_target, g, b)


def _conv_gate_fwd(u, conv_w, conv_b, name, comm=None):
    t, f2 = u.shape
    f = f2 // 2
    tb = _pick(t, (512, 256, 128, 64, 32, 16))
    cn = _pick(f, (1408, 1024, 512, 256, 128))
    ncb = f // cn
    hb = tb // 8

    def kern(a_ref, ah_ref, b_ref, w_ref, cb_ref, o_ref):
        i = pl.program_id(0)
        a = a_ref[...]
        halo = jnp.where(i > 0, ah_ref[...], 0.0)
        rid = lax.broadcasted_iota(jnp.int32, a.shape, 0)
        s1 = jnp.where(rid == 0, halo[7:8, :], pltpu.roll(a, 1, 0))
        s2 = jnp.where(rid == 0, halo[6:7, :], jnp.where(rid == 1, halo[7:8, :], pltpu.roll(a, 2, 0)))
        w = w_ref[...]
        conv = w[2:3, :] * a + w[1:2, :] * s1 + w[0:1, :] * s2 + cb_ref[...]
        sp, _ = _sigmoid_gate(conv)
        o_ref[...] = (conv * sp * b_ref[...]).astype(BF16)

    outs, carried = _carried_call(
        kern, comm, name=name, grid=(t // tb, ncb),
        in_specs=[pl.BlockSpec((tb, cn), lambda i, j: (i, j)),
                  pl.BlockSpec((8, cn), lambda i, j: (jnp.maximum(i * hb - 1, 0), j)),
                  pl.BlockSpec((tb, cn), lambda i, j: (i, j + ncb)),
                  pl.BlockSpec((3, cn), lambda i, j: (0, j)),
                  pl.BlockSpec((1, cn), lambda i, j: (0, j))],
        out_specs=[pl.BlockSpec((tb, cn), lambda i, j: (i, j))],
        out_shape=[jax.ShapeDtypeStruct((t, f), BF16)],
        operands=[u, u, u, conv_w, conv_b], sem=("parallel", "parallel"))
    return outs[0] if comm is None else (outs[0], carried)


def _conv_gate_bwd(u, dgact, conv_w, conv_b, name):
    t, f2 = u.shape
    f = f2 // 2
    tb = _pick(t, (512, 256, 128, 64, 32, 16))
    cn = _pick(f, (1408, 1024, 512, 256, 128))
    ncb = f // cn
    hb = tb // 8
    nb = t // tb
    last8 = t // 8 - 1

    def kern(a_ref, ap_ref, an_ref, b_ref, bn_ref, dg_ref, dgn_ref, w_ref, cb_ref,
             da_ref, db_ref, dw_ref, dcb_ref):
        i = pl.program_id(1)
        a = a_ref[...]
        w = w_ref[...]
        ext = jnp.concatenate([jnp.where(i > 0, ap_ref[...], 0.0), a, an_ref[...]], axis=0)
        e1 = pltpu.roll(ext, 1, 0)
        e2 = pltpu.roll(ext, 2, 0)
        conv = (w[2:3, :] * ext + w[1:2, :] * e1 + w[0:1, :] * e2 + cb_ref[...])[8:, :]
        bmn = jnp.concatenate([b_ref[...], bn_ref[...]], axis=0)
        dgmn = jnp.concatenate([dg_ref[...], jnp.where(i < nb - 1, dgn_ref[...], 0.0)], axis=0)
        sp, sn = _sigmoid_gate(conv)
        da = dgmn * bmn * (sp * (1.0 + conv * sn))
        n = tb + 8
        dap = w[2:3, :] * da + w[1:2, :] * pltpu.roll(da, n - 1, 0) + w[0:1, :] * pltpu.roll(da, n - 2, 0)
        da_ref[...] = dap[:tb, :].astype(BF16)
        db_ref[...] = (dg_ref[...] * (conv * sp)[:tb, :]).astype(BF16)
        dam = da[:tb, :]

        @pl.when(i == 0)
        def _():
            dw_ref[...] = jnp.zeros_like(dw_ref)
            dcb_ref[...] = jnp.zeros_like(dcb_ref)

        dw = jnp.concatenate([jnp.sum(dam * e2[8:8 + tb, :], axis=0, keepdims=True),
                              jnp.sum(dam * e1[8:8 + tb, :], axis=0, keepdims=True),
                              jnp.sum(dam * a, axis=0, keepdims=True)], axis=0)
        dw_ref[...] += dw
        dcb_ref[...] += jnp.sum(dam, axis=0, keepdims=True)

    main_a = pl.BlockSpec((tb, cn), lambda j, i: (i, j))
    prev_a = pl.BlockSpec((8, cn), lambda j, i: (jnp.maximum(i * hb - 1, 0), j))
    next_a = pl.BlockSpec((8, cn), lambda j, i: (jnp.minimum((i + 1) * hb, last8), j))
    main_b = pl.BlockSpec((tb, cn), lambda j, i: (i, j + ncb))
    next_b = pl.BlockSpec((8, cn), lambda j, i: (jnp.minimum((i + 1) * hb, last8), j + ncb))
    return pl.pallas_call(
        kern, name=name, grid=(ncb, nb),
        in_specs=[main_a, prev_a, next_a, main_b, next_b, main_a, next_a,
                  pl.BlockSpec((3, cn), lambda j, i: (0, j)), pl.BlockSpec((1, cn), lambda j, i: (0, j))],
        out_specs=[main_a, main_a, pl.BlockSpec((3, cn), lambda j, i: (0, j)),
                   pl.BlockSpec((1, cn), lambda j, i: (0, j))],
        out_shape=[jax.ShapeDtypeStruct((t, f), BF16), jax.ShapeDtypeStruct((t, f), BF16),
                   jax.ShapeDtypeStruct((3, f), F32), jax.ShapeDtypeStruct((1, f), F32)],
        compiler_params=_params(("parallel", "arbitrary")),
    )(u, u, u, u, u, dgact, dgact, conv_w, conv_b)


def _hg_gates(qp, fp, lb):
    sq, _ = _sigmoid_gate(qp)
    sf, snf = _sigmoid_pair(fp)
    forget = lb + (1.0 - lb) * sf
    return sq, sf, snf, forget, jnp.log(forget), (1.0 - lb) * snf


def _tri(lower):
    r = lax.broadcasted_iota(jnp.int32, (SUB, SUB), 0)
    c = lax.broadcasted_iota(jnp.int32, (SUB, SUB), 1)
    return ((r >= c) if lower else (r <= c)).astype(BF16)


def _split2(x):
    hi = x.astype(BF16)
    return hi, (x - hi.astype(F32)).astype(BF16)


def _dot3(a, b, dims):
    (ah, al), (bh, bl) = a, b
    return (lax.dot_general(ah, bh, dims, preferred_element_type=F32)
            + (lax.dot_general(ah, bl, dims, preferred_element_type=F32)
               + lax.dot_general(al, bh, dims, preferred_element_type=F32)))


def _running_sum(tri, x):
    hi, lo = _split2(x)
    rest = (x - hi.astype(F32)) - lo.astype(F32)
    return (lax.dot_general(tri, hi, NN, preferred_element_type=F32)
            + (lax.dot_general(tri, lo, NN, preferred_element_type=F32)
               + lax.dot_general(tri, rest.astype(BF16), NN, preferred_element_type=F32)))


HEADS_PER_STEP = 8
STEP_UNROLL = 2
FWD_STEP_UNROLL = 4


def _hgrn2_fwd(proj, lb_logits, norm_g, name, comm=None):
    t, d4 = proj.shape
    d = d4 // 4
    nh = d // LANES
    hb = _pick(nh, (HEADS_PER_STEP, 2, 1))
    wb = hb * LANES
    tb = _pick(t, (256, 128, 64, 32, 16))
    nb = t // tb
    nsc = tb // SUB

    def kern(q_ref, f_ref, i_ref, g_ref, lbl_ref, ng_ref, y_ref, o_ref, st_ref, s_ref):
        @pl.when(pl.program_id(1) == 0)
        def _():
            s_ref[...] = jnp.zeros_like(s_ref)

        lb_all = _lower_bound(lbl_ref[...])
        ng_all = ng_ref[...]
        ltri = _tri(True)
        rcol = lax.broadcasted_iota(jnp.int32, (SUB, 1), 0)

        heads = [slice(h * LANES, (h + 1) * LANES) for h in range(hb)]

        def step(sc, carry):
            rows = pl.ds(pl.multiple_of(sc * SUB, SUB), SUB)
            qp, fp, v, gp = q_ref[rows, :], f_ref[rows, :], i_ref[rows, :], g_ref[rows, :]
            sq, _, _, _, lf, k = _hg_gates(qp, fp, lb_all)
            q = qp * sq
            bl = _running_sum(ltri, lf)
            bend = bl[SUB - 1:SUB, :]
            dec = jnp.exp(bend)
            qs2 = _split2(q * jnp.exp(bl))
            kd2 = _split2(k * jnp.exp(bend - bl))
            v2 = _split2(v)
            states = [s_ref[h] for h in range(hb)]
            o = [_dot3((qs2[0][:, c], qs2[1][:, c]), _split2(states[h]), NT) for h, c in enumerate(heads)]
            top, bot = [oh[:TILE] for oh in o], [oh[TILE:] for oh in o]
            for s in range(SUB):
                lo = 0 if s < TILE else TILE
                e = jnp.exp(jnp.minimum(bl[lo:] - bl[s:s + 1, :], 0.0))
                p = q[lo:] * e * k[s:s + 1, :]
                for h, c in enumerate(heads):
                    a = jnp.sum(p[:, c], axis=1, keepdims=True)
                    add = jnp.where(rcol[lo:] >= s, a, 0.0) * v[s:s + 1, c]
                    if lo == 0:
                        top[h], bot[h] = top[h] + add[:TILE], bot[h] + add[TILE:]
                    else:
                        bot[h] = bot[h] + add
            o = [jnp.concatenate([a, b], axis=0) for a, b in zip(top, bot)]
            for h, c in enumerate(heads):
                st_ref[sc, h] = states[h]
                s_ref[h] = states[h] * dec[:, c] + _dot3((v2[0][:, c], v2[1][:, c]), (kd2[0][:, c], kd2[1][:, c]), TN)
            o_ref[rows, :] = jnp.concatenate(o, axis=1)
            on = jnp.concatenate(
                [oh * lax.rsqrt(jnp.mean(oh * oh, axis=1, keepdims=True) + RMS_EPS) for oh in o], axis=1)
            sg, _ = _sigmoid_gate(gp)
            y_ref[rows, :] = (on * ng_all * (gp * sg)).astype(BF16)
            return carry

        lax.fori_loop(0, nsc, step, 0, unroll=FWD_STEP_UNROLL)

    def col(off):
        return pl.BlockSpec((tb, wb), lambda h, j: (j, h + off * (nh // hb)))

    return _carried_call(
        kern, comm, name=name, grid=(nh // hb, nb),
        in_specs=[col(0), col(1), col(2), col(3),
                  pl.BlockSpec((3, wb), lambda h, j: (0, h)), pl.BlockSpec((1, wb), lambda h, j: (0, h))],
        out_specs=[col(0), col(0), pl.BlockSpec((nsc, hb, LANES, LANES), lambda h, j: (j, h, 0, 0))],
        out_shape=[jax.ShapeDtypeStruct((t, d), BF16), jax.ShapeDtypeStruct((t, d), F32),
                   jax.ShapeDtypeStruct((t // SUB, nh, LANES, LANES), F32)],
        scratch_shapes=[pltpu.VMEM((hb, LANES, LANES), F32)],
        operands=[proj, proj, proj, proj, lb_logits, norm_g], sem=("parallel", "arbitrary"))


def _hgrn2_bwd(proj, lb_logits, norm_g, o_raw, states, dy, name, comm=None):
    t, d4 = proj.shape
    d = d4 // 4
    nh = d // LANES
    hb = _pick(nh, (HEADS_PER_STEP, 2, 1))
    wb = hb * LANES
    tb = _pick(t, (256, 128, 64, 32, 16))
    nb = t // tb
    nsc = tb // SUB

    def kern(q_ref, f_ref, i_ref, g_ref, lbl_ref, ng_ref, o_ref, st_ref, dy_ref,
             dq_ref, df_ref, di_ref, dgp_ref, dlb_ref, dng_ref, ds_ref, gc_ref):
        j = pl.program_id(1)

        @pl.when(j == 0)
        def _():
            ds_ref[...] = jnp.zeros_like(ds_ref)
            gc_ref[...] = jnp.zeros_like(gc_ref)
            dlb_ref[...] = jnp.zeros_like(dlb_ref)
            dng_ref[...] = jnp.zeros_like(dng_ref)

        lb_all = _lower_bound(lbl_ref[...])
        ng_all = ng_ref[...]
        ltri, utri = _tri(True), _tri(False)
        rcol = lax.broadcasted_iota(jnp.int32, (SUB, 1), 0)
        rid = lax.broadcasted_iota(jnp.int32, (SUB, wb), 0)

        heads = [slice(h * LANES, (h + 1) * LANES) for h in range(hb)]

        def per_head(fn, n=SUB):
            return jnp.concatenate([jnp.broadcast_to(fn(c), (n, LANES)) for c in heads], axis=1)

        def step(it, carry):
            sc = nsc - 1 - it
            rows = pl.ds(pl.multiple_of(sc * SUB, SUB), SUB)
            qp, fp, v, gp = q_ref[rows, :], f_ref[rows, :], i_ref[rows, :], g_ref[rows, :]
            o, dyv = o_ref[rows, :], dy_ref[rows, :]
            sq, sf, snf, forget, lf, k = _hg_gates(qp, fp, lb_all)
            q = qp * sq
            bl = _running_sum(ltri, lf)
            ebl = jnp.exp(bl)
            bend = bl[SUB - 1:SUB, :]
            dec = jnp.exp(bend)
            dte = jnp.exp(bend - bl)
            r = per_head(lambda c: lax.rsqrt(jnp.mean(o[:, c] * o[:, c], axis=1, keepdims=True) + RMS_EPS))
            ohat = o * r
            sg, sng = _sigmoid_gate(gp)
            don = dyv * (gp * sg)
            dgp_ref[rows, :] = (dyv * (ohat * ng_all) * (sg * (1.0 + gp * sng))).astype(BF16)
            dng_ref[...] += jnp.sum(don * ohat, axis=0, keepdims=True)
            doh = don * ng_all
            dot_oh = doh * ohat
            do = r * (doh - ohat * per_head(lambda c: jnp.mean(dot_oh[:, c], axis=1, keepdims=True)))
            do2, qs2, kd2, v2 = _split2(do), _split2(q * ebl), _split2(k * dte), _split2(v)
            dq_h, dk_h, dv_h = [], [], []
            for h, c in enumerate(heads):
                dstate = ds_ref[h]
                ds2 = _split2(dstate)
                doc = (do2[0][:, c], do2[1][:, c])
                dq_h.append(_dot3(doc, _split2(st_ref[sc, h]), NN))
                dv_h.append(_dot3((kd2[0][:, c], kd2[1][:, c]), ds2, NT))
                dk_h.append(_dot3((v2[0][:, c], v2[1][:, c]), ds2, NN))
                ds_ref[h] = dstate * dec[:, c] + _dot3(doc, (qs2[0][:, c], qs2[1][:, c]), TN)
            dq = jnp.concatenate(dq_h, axis=1) * ebl
            dk = jnp.concatenate(dk_h, axis=1) * dte
            dv = jnp.concatenate(dv_h, axis=1)
            dq_t, dq_b = dq[:TILE], dq[TILE:]
            dk_i = [jnp.zeros((TILE, wb), F32), jnp.zeros((TILE, wb), F32)]
            dv_i = [jnp.zeros((TILE, wb), F32), jnp.zeros((TILE, wb), F32)]
            for s in range(SUB):
                lo = 0 if s < TILE else TILE
                n = SUB - lo
                e = jnp.exp(jnp.minimum(bl[lo:] - bl[s:s + 1, :], 0.0))
                qe = q[lo:] * e
                ks = k[s:s + 1, :]
                live = rcol[lo:] >= s
                pk = qe * ks
                dor = do[lo:]
                pv = dor * v[s:s + 1, :]
                a = per_head(lambda c: jnp.where(live, jnp.sum(pk[:, c], axis=1, keepdims=True), 0.0), n)
                da = per_head(lambda c: jnp.where(live, jnp.sum(pv[:, c], axis=1, keepdims=True), 0.0), n)
                ddq = da * (e * ks)
                if lo == 0:
                    dq_t, dq_b = dq_t + ddq[:TILE], dq_b + ddq[TILE:]
                else:
                    dq_b = dq_b + ddq
                here = rid[:TILE] == s - lo
                dk_i[lo // TILE] = jnp.where(here, jnp.sum(da * qe, axis=0, keepdims=True), dk_i[lo // TILE])
                dv_i[lo // TILE] = jnp.where(here, jnp.sum(a * dor, axis=0, keepdims=True), dv_i[lo // TILE])
            dq = jnp.concatenate([dq_t, dq_b], axis=0)
            dk = dk + jnp.concatenate(dk_i, axis=0)
            dv = dv + jnp.concatenate(dv_i, axis=0)
            w = q * dq - k * dk
            gc = gc_ref[...]
            dlf = _running_sum(utri, w) + gc
            gc_ref[...] = gc + jnp.sum(w, axis=0, keepdims=True)
            t1 = dlf / forget - dk
            df_ref[rows, :] = ((1.0 - lb_all) * sf * snf * t1).astype(BF16)
            dlb_ref[...] += jnp.sum(snf * t1, axis=0, keepdims=True)
            dq_ref[rows, :] = (dq * (sq * (1.0 + qp * (1.0 - sq)))).astype(BF16)
            di_ref[rows, :] = dv.astype(BF16)
            return carry

        lax.fori_loop(0, nsc, step, 0, unroll=STEP_UNROLL)

    def col(off):
        return pl.BlockSpec((tb, wb), lambda h, j: (nb - 1 - j, h + off * (nh // hb)))

    vec = pl.BlockSpec((1, wb), lambda h, j: (0, h))
    return _carried_call(
        kern, comm, name=name, grid=(nh // hb, nb),
        in_specs=[col(0), col(1), col(2), col(3), pl.BlockSpec((3, wb), lambda h, j: (0, h)), vec,
                  col(0), pl.BlockSpec((nsc, hb, LANES, LANES), lambda h, j: (nb - 1 - j, h, 0, 0)), col(0)],
        out_specs=[col(0), col(0), col(0), col(0), vec, vec],
        out_shape=[jax.ShapeDtypeStruct((t, d), BF16)] * 4 + [jax.ShapeDtypeStruct((1, d), F32)] * 2,
        scratch_shapes=[pltpu.VMEM((hb, LANES, LANES), F32), pltpu.VMEM((1, wb), F32)],
        operands=[proj, proj, proj, proj, lb_logits, norm_g, o_raw, states, dy], sem=("parallel", "arbitrary"))


_INV_SQRT2 = 0.7071067811865476
_INV_SQRT2PI = 0.3989422804014327


def _gelu(x):
    return 0.5 * x * (1.0 + lax.erf(x * _INV_SQRT2))


def _gelu_grad(x):
    return 0.5 * (1.0 + lax.erf(x * _INV_SQRT2)) + x * jnp.exp(-0.5 * x * x) * _INV_SQRT2PI


def _causal(w):
    r = lax.broadcasted_iota(jnp.int32, (GCHUNK, GCHUNK), 0)
    c = lax.broadcasted_iota(jnp.int32, (GCHUNK, GCHUNK), 1)
    return jnp.where(r >= c, w, 0.0)


def _sg_gate_fwd(pre, ln_g, ln_b, w_s, b_s_t, name):
    t, d2 = pre.shape
    d = d2 // 2
    ng = d // LANES

    def kern(pre_ref, g_ref, b_ref, ws_ref, bs_ref, y_ref):
        z = _gelu(pre_ref[...])
        u = z[:, :d]
        vhat, _ = _ln_hat(z[:, d:])
        vn = (vhat * g_ref[...] + b_ref[...]).astype(BF16)
        bs = bs_ref[...]
        for g in range(ng):
            cols = slice(g * LANES, (g + 1) * LANES)
            wc = _causal(ws_ref[g]).astype(BF16)
            gate = jnp.dot(wc, vn[:, cols], preferred_element_type=F32) + bs[:, g:g + 1]
            y_ref[:, cols] = (u[:, cols] * gate).astype(BF16)

    vec = pl.BlockSpec((1, d), lambda i: (0, 0))
    return pl.pallas_call(
        kern, name=name, grid=(t // GCHUNK,),
        in_specs=[pl.BlockSpec((GCHUNK, d2), lambda i: (i, 0)), vec, vec,
                  pl.BlockSpec((ng, GCHUNK, GCHUNK), lambda i: (0, 0, 0)),
                  pl.BlockSpec((GCHUNK, ng), lambda i: (0, 0))],
        out_specs=pl.BlockSpec((GCHUNK, d), lambda i: (i, 0)),
        out_shape=jax.ShapeDtypeStruct((t, d), BF16),
        compiler_params=_params(("parallel",)),
    )(pre, ln_g, ln_b, w_s, b_s_t)


def _sg_gate_bwd(pre, dy, ln_g, ln_b, w_s, b_s_t, name):
    t, d2 = pre.shape
    d = d2 // 2
    ng = d // LANES

    def kern(pre_ref, dy_ref, g_ref, b_ref, ws_ref, bs_ref, dpre_ref, dws_ref, dbs_ref, dg_ref, db_ref, dvn_ref):
        @pl.when(pl.program_id(0) == 0)
        def _():
            dws_ref[...] = jnp.zeros_like(dws_ref)
            dbs_ref[...] = jnp.zeros_like(dbs_ref)
            dg_ref[...] = jnp.zeros_like(dg_ref)
            db_ref[...] = jnp.zeros_like(db_ref)

        pre = pre_ref[...]
        z = _gelu(pre)
        u = z[:, :d]
        vhat, rstd = _ln_hat(z[:, d:])
        gv = g_ref[...]
        vn = (vhat * gv + b_ref[...]).astype(BF16)
        bs = bs_ref[...]
        dyv = dy_ref[...]
        gp = _gelu_grad(pre)
        lane = lax.broadcasted_iota(jnp.int32, (GCHUNK, ng), 1)
        dbs = jnp.zeros((GCHUNK, ng), F32)
        for g in range(ng):
            cols = slice(g * LANES, (g + 1) * LANES)
            wc = _causal(ws_ref[g]).astype(BF16)
            vng = vn[:, cols]
            gate = jnp.dot(wc, vng, preferred_element_type=F32) + bs[:, g:g + 1]
            dpre_ref[:, cols] = (dyv[:, cols] * gate * gp[:, cols]).astype(BF16)
            dgate = dyv[:, cols] * u[:, cols]
            dbs = dbs + jnp.where(lane == g, jnp.sum(dgate, axis=1, keepdims=True), 0.0)
            dgb = dgate.astype(BF16)
            dws_ref[g] += _causal(lax.dot_general(dgb, vng, NT, preferred_element_type=F32))
            dvn_ref[:, cols] = lax.dot_general(wc, dgb, TN, preferred_element_type=F32)
        dbs_ref[...] += dbs
        dvn = dvn_ref[...]
        dg_ref[...] += jnp.sum(dvn * vhat, axis=0, keepdims=True)
        db_ref[...] += jnp.sum(dvn, axis=0, keepdims=True)
        dvh = dvn * gv
        m1 = jnp.mean(dvh, axis=-1, keepdims=True)
        m2 = jnp.mean(dvh * vhat, axis=-1, keepdims=True)
        dpre_ref[:, d:] = (rstd * (dvh - m1 - vhat * m2) * gp[:, d:]).astype(BF16)

    vec = pl.BlockSpec((1, d), lambda i: (0, 0))
    wsp = pl.BlockSpec((ng, GCHUNK, GCHUNK), lambda i: (0, 0, 0))
    bsp = pl.BlockSpec((GCHUNK, ng), lambda i: (0, 0))
    return pl.pallas_call(
        kern, name=name, grid=(t // GCHUNK,),
        in_specs=[pl.BlockSpec((GCHUNK, d2), lambda i: (i, 0)), pl.BlockSpec((GCHUNK, d), lambda i: (i, 0)),
                  vec, vec, wsp, bsp],
        out_specs=[pl.BlockSpec((GCHUNK, d2), lambda i: (i, 0)), wsp, bsp, vec, vec],
        out_shape=[jax.ShapeDtypeStruct((t, d2), BF16), jax.ShapeDtypeStruct((ng, GCHUNK, GCHUNK), F32),
                   jax.ShapeDtypeStruct((GCHUNK, ng), F32), jax.ShapeDtypeStruct((1, d), F32),
                   jax.ShapeDtypeStruct((1, d), F32)],
        scratch_shapes=[pltpu.VMEM((GCHUNK, d), F32)],
        compiler_params=_params(("arbitrary",)),
    )(pre, dy, ln_g, ln_b, w_s, b_s_t)


def _adamw_math(w, g, m, v):
    m = ADAM_B1 * m + (1.0 - ADAM_B1) * g
    v = ADAM_B2 * v + (1.0 - ADAM_B2) * (g * g)
    m_hat = m / (1.0 - ADAM_B1 ** ADAM_STEP)
    v_hat = v / (1.0 - ADAM_B2 ** ADAM_STEP)
    return -ADAM_LR * (m_hat / (jnp.sqrt(v_hat) + ADAM_EPS) + ADAM_WD * w), m, v


ADAMW_BLOCK_BYTES = 3 << 19


def _adamw(w, gs, m, v, name, comm=None):
    nl, r, c = w.shape
    rb = _pick(r, tuple(p for p in (512, 256, 128, 64, 32, 16, 8) if p * c * 4 <= ADAMW_BLOCK_BYTES))

    def kern(w_ref, m_ref, v_ref, *rest):
        g_refs, (d_ref, mo_ref, vo_ref, go_ref) = rest[:nl], rest[nl:]
        layer = pl.program_id(0)
        g = g_refs[0][...]
        for k in range(1, nl):
            g = jnp.where(layer == k, g_refs[k][...], g)
        dlt, mm, vv = _adamw_math(w_ref[...], g, m_ref[...], v_ref[...])
        d_ref[...] = dlt
        mo_ref[...] = mm
        vo_ref[...] = vv
        go_ref[...] = g

    blk = pl.BlockSpec((None, rb, c), lambda l, i: (l, i, 0))
    g_specs = [pl.BlockSpec((rb, c), lambda l, i, k=k: (jnp.where(l == k, i, 0), 0)) for k in range(nl)]
    outs, carried = _carried_call(
        kern, comm, name=name, grid=(nl, r // rb), in_specs=[blk] * 3 + g_specs, out_specs=[blk] * 4,
        out_shape=[jax.ShapeDtypeStruct((nl, r, c), F32)] * 4, operands=[w, m, v, *gs], sem=("parallel", "parallel"))
    return outs if comm is None else (outs, carried)


CAST_BLOCK_BYTES = 1 << 21


def _cast_bf16(items, name, comm=None):
    metas, start = [], 0
    for arr, _ in items:
        _, r, c = arr.shape
        rb = _pick(r, tuple(p for p in (1024, 512, 256, 128, 64, 32, 16) if p * c * 4 <= CAST_BLOCK_BYTES))
        metas.append((start, r // rb, rb, c))
        start += r // rb
    n = len(items)

    def spec(p, layer=None):
        s0, steps, rb, c = metas[p]
        if layer is None:
            return pl.BlockSpec((rb, c), lambda s: (jnp.clip(s - s0, 0, steps - 1), 0))
        return pl.BlockSpec((None, rb, c), lambda s: (layer, jnp.clip(s - s0, 0, steps - 1), 0))

    def kern(*refs):
        s = pl.program_id(0)
        for p in range(n):
            s0, steps, _, _ = metas[p]

            @pl.when(jnp.logical_and(s >= s0, s < s0 + steps))
            def _(p=p):
                refs[n + p][...] = refs[p][...].astype(BF16)

    outs, carried = _carried_call(
        kern, comm, name=name, grid=(start,), in_specs=[spec(p, layer) for p, (_, layer) in enumerate(items)],
        out_specs=[spec(p) for p in range(n)],
        out_shape=[jax.ShapeDtypeStruct(arr.shape[1:], BF16) for arr, _ in items],
        operands=[arr for arr, _ in items], sem=("arbitrary",))
    return outs, carried


def _comm_call(plan, name):
    ni, no = len(plan.ins), len(plan.out_shapes)

    def body(*refs):
        copies = plan.build(refs[:ni], refs[ni:ni + no], refs[-2], refs[-1], 0)
        for cp in copies:
            cp.start()
        for cp in copies:
            cp.wait()

    anyspec = pl.BlockSpec(memory_space=pl.ANY)
    return pl.pallas_call(
        body, name=name, in_specs=[anyspec] * ni, out_specs=[anyspec] * no, out_shape=plan.out_shapes,
        input_output_aliases=plan.aliases,
        scratch_shapes=[pltpu.SemaphoreType.DMA((plan.n_sems,)), pltpu.SemaphoreType.DMA((plan.n_sems,))],
        compiler_params=pltpu.CompilerParams(has_side_effects=True),
    )(*plan.ins)


def _lb_logits_grad(lb_logits, dlb, name):
    def kern(l_ref, d_ref, o_ref):
        lg = l_ref[...]
        m = jnp.max(lg, axis=0, keepdims=True)
        e = jnp.exp(lg - m)
        p = e / jnp.sum(e, axis=0, keepdims=True)
        row = lax.broadcasted_iota(jnp.int32, lg.shape, 0)
        o_ref[...] = d_ref[...] * p[0:1, :] * (jnp.where(row == 0, 1.0, 0.0) - p)

    return pl.pallas_call(kern, name=name, out_shape=jax.ShapeDtypeStruct(lb_logits.shape, F32))(lb_logits, dlb)


def _sum_devices(others, own, me, name):
    n, r, c = others.shape
    rb = _pick(r, (512, 256, 128, 64, 32, 16, 8))

    def kern(me_ref, a_ref, own_ref, o_ref):
        mine = own_ref[...]
        acc = jnp.where(me_ref[0] == 0, mine, a_ref[0])
        for i in range(1, n):
            acc = acc + jnp.where(me_ref[0] == i, mine, a_ref[i])
        o_ref[...] = acc

    return pl.pallas_call(
        kern, name=name,
        grid_spec=pltpu.PrefetchScalarGridSpec(
            num_scalar_prefetch=1, grid=(r // rb,),
            in_specs=[pl.BlockSpec((n, rb, c), lambda i, s: (0, i, 0)), pl.BlockSpec((rb, c), lambda i, s: (i, 0))],
            out_specs=pl.BlockSpec((rb, c), lambda i, s: (i, 0))),
        out_shape=jax.ShapeDtypeStruct((r, c), F32),
        compiler_params=_params(("parallel",)),
    )(me, others, own)


def _place():
    x, y, c = lax.axis_index("x"), lax.axis_index("y"), lax.axis_index("c")
    return x, y, c


class _Plan:
    def __init__(self, ins, out_shapes, aliases, n_sems, build):
        self.ins, self.out_shapes, self.aliases, self.n_sems, self.build = list(ins), list(out_shapes), aliases, n_sems, build


def _merge(*plans):
    ins, outs, aliases, subs, sems = [], [], {}, [], 0
    for p in plans:
        for k, v in p.aliases.items():
            aliases[len(ins) + k] = len(outs) + v
        subs.append((p, len(ins), len(outs), sems))
        ins += p.ins
        outs += p.out_shapes
        sems += p.n_sems

    def build(in_refs, out_refs, send_sems, recv_sems, base):
        copies = []
        for p, i0, o0, s0 in subs:
            copies += p.build(in_refs[i0:i0 + len(p.ins)], out_refs[o0:o0 + len(p.out_shapes)], send_sems, recv_sems,
                              base + s0)
        return copies

    return _Plan(ins, outs, aliases, sems, build)


def _remote(src, dst, send_sems, recv_sems, k, to):
    return pltpu.make_async_remote_copy(src_ref=src, dst_ref=dst, send_sem=send_sems.at[k], recv_sem=recv_sems.at[k],
                                        device_id=to, device_id_type=MESH)


def _plan_gather_ici(shards):
    n = len(shards)

    def build(ins, outs, send_sems, recv_sems, base):
        x, y, c = _place()
        me = 2 * x + y
        copies = []
        for a in range(n):
            h = ins[a].shape[0] // 2
            rows = pl.ds(c * h, h)
            for r in (1, 2, 3):
                px, py, _ = _chip_rel(x, y, r)
                copies.append(_remote(ins[a].at[rows, :], outs[a].at[me, rows, :], send_sems, recv_sems,
                                      base + 4 * a + r - 1, (px, py, c)))
            copies.append(_remote(ins[a], outs[a].at[me], send_sems, recv_sems, base + 4 * a + 3, (x, y, 1 - c)))
        return copies

    return _Plan(shards, [jax.ShapeDtypeStruct((N_CHIPS,) + s.shape, s.dtype) for s in shards], {}, 4 * n, build)


def _plan_gather_pass(gathered):
    n = len(gathered)

    def build(ins, outs, send_sems, recv_sems, base):
        x, y, c = _place()
        copies = []
        for a in range(n):
            h = outs[a].shape[1] // 2
            rows = pl.ds(c * h, h)
            for r in (1, 2, 3):
                _, _, shard = _chip_rel(x, y, r)
                piece = outs[a].at[shard, rows, :]
                copies.append(_remote(piece, piece, send_sems, recv_sems, base + 3 * a + r - 1, (x, y, 1 - c)))
        return copies

    return _Plan(gathered, [jax.ShapeDtypeStruct(g.shape, g.dtype) for g in gathered], {a: a for a in range(n)},
                 3 * n, build)


def _plan_swap(split):
    n = len(split)

    def build(ins, outs, send_sems, recv_sems, base):
        x, y, c = _place()
        return [_remote(ins[a].at[j, 1 - c], outs[a].at[j], send_sems, recv_sems, base + N_CHIPS * a + j, (x, y, 1 - c))
                for a in range(n) for j in range(N_CHIPS)]

    return _Plan(split, [jax.ShapeDtypeStruct((N_CHIPS,) + g.shape[2:], g.dtype) for g in split], {}, N_CHIPS * n, build)


def _plan_scatter(parts):
    n = len(parts)

    def build(ins, outs, send_sems, recv_sems, base):
        x, y, c = _place()
        copies = []
        for a in range(n):
            for r in (1, 2, 3):
                px, py, shard = _chip_rel(x, y, r)
                copies.append(_remote(ins[a].at[shard], outs[a].at[r - 1], send_sems, recv_sems, base + 3 * a + r - 1,
                                      (px, py, c)))
        return copies

    return _Plan(parts, [jax.ShapeDtypeStruct((3,) + p.shape[1:], p.dtype) for p in parts], {}, 3 * n, build)


def _plan_join(bufs):
    n = len(bufs)

    def build(ins, outs, send_sems, recv_sems, base):
        x, y, c = _place()
        return [_remote(outs[a].at[c], outs[a].at[c], send_sems, recv_sems, base + a, (x, y, 1 - c)) for a in range(n)]

    return _Plan(bufs, [jax.ShapeDtypeStruct(b.shape, b.dtype) for b in bufs], {a: a for a in range(n)}, n, build)


def _carried_call(kern, plan, *, name, grid, in_specs, out_specs, out_shape, operands, scratch_shapes=(),
                  aliases=None, sem=None):
    n_in, n_out, n_sc = len(operands), len(out_shape), len(scratch_shapes)
    aliases = dict(aliases or {})
    if plan is None:
        outs = pl.pallas_call(kern, name=name, grid=grid, in_specs=in_specs, out_specs=out_specs, out_shape=out_shape,
                              scratch_shapes=list(scratch_shapes), input_output_aliases=aliases,
                              compiler_params=_params(sem))(*operands)
        return list(outs), []
    ci, co = len(plan.ins), len(plan.out_shapes)
    for k, v in plan.aliases.items():
        aliases[n_in + k] = n_out + v
    steps = tuple(grid)

    def body(*refs):
        ins, cins = refs[:n_in], refs[n_in:n_in + ci]
        outs = refs[n_in + ci:n_in + ci + n_out]
        couts = refs[n_in + ci + n_out:n_in + ci + n_out + co]
        scratch = refs[n_in + ci + n_out + co:n_in + ci + n_out + co + n_sc]
        send_sems, recv_sems = refs[-2], refs[-1]
        first = functools.reduce(jnp.logical_and, [pl.program_id(a) == 0 for a in range(len(steps))])
        last = functools.reduce(jnp.logical_and, [pl.program_id(a) == steps[a] - 1 for a in range(len(steps))])

        @pl.when(first)
        def _():
            for cp in plan.build(cins, couts, send_sems, recv_sems, 0):
                cp.start()

        kern(*ins, *outs, *scratch)

        @pl.when(last)
        def _():
            for cp in plan.build(cins, couts, send_sems, recv_sems, 0):
                cp.wait()

    anyspec = pl.BlockSpec(memory_space=pl.ANY)
    outs = pl.pallas_call(
        body, name=name, grid=grid, in_specs=list(in_specs) + [anyspec] * ci,
        out_specs=list(out_specs) + [anyspec] * co, out_shape=list(out_shape) + plan.out_shapes,
        scratch_shapes=list(scratch_shapes) + [pltpu.SemaphoreType.DMA((plan.n_sems,)),
                                               pltpu.SemaphoreType.DMA((plan.n_sems,))],
        input_output_aliases=aliases,
        compiler_params=_params(("arbitrary",) * len(steps)),
    )(*operands, *plan.ins)
    return list(outs[:n_out]), list(outs[n_out:])


def _chip_rel(x, y, r):
    px = x if r < 2 else 1 - x
    py = y if r % 2 == 0 else 1 - y
    return px, py, 2 * px + py


def _allgather_whole(arr, name):
    def body(in_ref, out_ref, send_sems, recv_sems, loc_sem):
        x, y, c = _place()
        me = 2 * x + y
        local = pltpu.make_async_copy(in_ref, out_ref.at[me], loc_sem)
        local.start()
        sends = []
        for r in (1, 2, 3):
            px, py, _ = _chip_rel(x, y, r)
            sends.append(pltpu.make_async_remote_copy(
                src_ref=in_ref, dst_ref=out_ref.at[me], send_sem=send_sems.at[r - 1], recv_sem=recv_sems.at[r - 1],
                device_id=(px, py, c), device_id_type=MESH))
        for cp in sends:
            cp.start()
        for r in (1, 2, 3):
            px, py, shard = _chip_rel(x, y, r)
            pltpu.make_async_remote_copy(
                src_ref=in_ref, dst_ref=out_ref.at[shard], send_sem=send_sems.at[r - 1], recv_sem=recv_sems.at[r - 1],
                device_id=(px, py, c), device_id_type=MESH).wait_recv()
        for cp in sends:
            cp.wait_send()
        local.wait()

    anyspec = pl.BlockSpec(memory_space=pl.ANY)
    return pl.pallas_call(
        body, name=name, in_specs=[anyspec], out_specs=anyspec,
        out_shape=jax.ShapeDtypeStruct((N_CHIPS,) + arr.shape, arr.dtype),
        scratch_shapes=[pltpu.SemaphoreType.DMA((3,)), pltpu.SemaphoreType.DMA((3,)), pltpu.SemaphoreType.DMA],
        compiler_params=pltpu.CompilerParams(has_side_effects=True),
    )(arr)


def _plan_gather_all(buf):
    def build(ins, outs, send_sems, recv_sems, base):
        x, y, c = _place()
        me = 4 * x + 2 * y + c
        copies = []
        for r in range(1, N_DEV):
            px, py, _ = _chip_rel(x, y, r // 2)
            pc = c if r % 2 == 0 else 1 - c
            copies.append(_remote(ins[0], outs[0].at[me], send_sems, recv_sems, base + r - 1, (px, py, pc)))
        return copies

    return _Plan([buf, jnp.zeros((N_DEV,) + buf.shape, buf.dtype)],
                 [jax.ShapeDtypeStruct((N_DEV,) + buf.shape, buf.dtype)], {1: 0}, N_DEV - 1, build)


def _add_half(grad, recv, sel, name):
    _, _, rh, cw = grad.shape
    rb = _pick(rh, (512, 256, 176, 128, 64, 32, 16, 8))

    def kern(sel_ref, g_ref, r_ref, o_ref):
        o_ref[...] = (g_ref[...] + r_ref[...]).astype(BF16)

    return pl.pallas_call(
        kern, name=name,
        grid_spec=pltpu.PrefetchScalarGridSpec(
            num_scalar_prefetch=1, grid=(N_CHIPS - 1, rh // rb),
            in_specs=[pl.BlockSpec((None, None, rb, cw), lambda j, i, s: (s[2 + j], s[0], i, 0)),
                      pl.BlockSpec((None, rb, cw), lambda j, i, s: (s[2 + j], i, 0))],
            out_specs=pl.BlockSpec((None, rb, cw), lambda j, i, s: (s[2 + j], i, 0))),
        out_shape=jax.ShapeDtypeStruct((N_CHIPS, rh, cw), BF16),
        compiler_params=_params(("parallel", "parallel")),
    )(sel, grad, recv)


def _add_own(grad, recv, got, sel, name):
    _, _, rh, cw = grad.shape
    rb = _pick(rh, (512, 256, 176, 128, 64, 32, 16, 8))

    def kern(sel_ref, g_ref, r_ref, b_ref, o_ref):
        own = g_ref[...] + r_ref[...]
        o_ref[...] = ((own + b_ref[0].astype(F32)) + b_ref[1].astype(F32)) + b_ref[2].astype(F32)

    return pl.pallas_call(
        kern, name=name,
        grid_spec=pltpu.PrefetchScalarGridSpec(
            num_scalar_prefetch=1, grid=(rh // rb,),
            in_specs=[pl.BlockSpec((None, None, rb, cw), lambda i, s: (s[1], s[0], i, 0)),
                      pl.BlockSpec((None, rb, cw), lambda i, s: (s[1], i, 0)),
                      pl.BlockSpec((3, rb, cw), lambda i, s: (0, i, 0))],
            out_specs=pl.BlockSpec((None, rb, cw), lambda i, s: (s[0], i, 0))),
        out_shape=jax.ShapeDtypeStruct((2, rh, cw), F32),
        compiler_params=_params(("parallel",)),
    )(sel, grad, recv, got)


def _stacked(g):
    return g.reshape(1, g.shape[0] * g.shape[1], g.shape[2])


def _halves(g):
    g = g.reshape(N_CHIPS, g.shape[0] * g.shape[1] // N_CHIPS, g.shape[2])
    return g.reshape(N_CHIPS, 2, g.shape[1] // 2, g.shape[2])


def _whole(f):
    return f.reshape(f.shape[0] * f.shape[1], f.shape[2])


def _step(x3, tgt, sm, w, mom, var):
    x, y, c = _place()
    sel = jnp.stack([c, 2 * x + y] + [_chip_rel(x, y, r)[2] for r in (1, 2, 3)]).astype(jnp.int32)
    wg = {}
    x2 = x3[0]
    cast, landed = _cast_bf16(
        [(x3, 0), (w["hg_w_out"], 0), (w["sg_w_in"], 0), (w["sg_w_out"], 0), (w["ffn_w_up"], 0), (w["ffn_w_up"], 1),
         (w["ffn_w_down"], 0), (w["ffn_w_down"], 1)], "cast_shards",
        comm=_plan_gather_ici([w["hg_w_in"][0].astype(BF16)]))
    xb = cast[0]
    sh = dict(zip(("hg_out", "sg_in", "sg_out", "up0", "up1", "dn0", "dn1"), cast[1:]))
    wg["hg_in"] = _comm_call(_plan_gather_pass(landed), "gather_hg_in_pass")[0]
    proj, landed = _matmul(xb, wg["hg_in"], mode="nn", nsh=N_CHIPS, name="hg_in",
                           comm=_plan_gather_ici([sh["hg_out"], sh["sg_in"]]))
    (yhg, o_raw, states), got = _hgrn2_fwd(
        proj, sm["lb_logits"], sm["hg_norm_g"], "hgrn2_fwd",
        comm=_merge(_plan_gather_pass(landed), _plan_gather_ici([sh["up0"], sh["up1"]])))
    wg["hg_out"], wg["sg_in"], landed = got[0], got[1], got[2:]
    xin1, got = _matmul(yhg, _stacked(wg["hg_out"]), mode="nn", nsh=1, resid=x2, alpha=ALPHA, name="hg_out",
                        comm=_plan_gather_pass(landed))
    wg["up0"], wg["up1"] = got
    h1, h1b = _ln_fwd(xin1, sm["ln1_g"][0:1], sm["ln1_b"][0:1], "l0_ln1")
    u0, landed = _matmul(h1b, wg["up0"], mode="nn", nsh=N_CHIPS, name="l0_ffn_up",
                         comm=_plan_gather_ici([sh["dn0"], sh["sg_out"]]))
    gact0, got = _conv_gate_fwd(u0, sm["conv_w"][0], sm["conv_b"][0:1], "l0_ffn_gate", comm=_plan_gather_pass(landed))
    wg["dn0"], wg["sg_out"] = got
    xin2, landed = _matmul(gact0, _stacked(wg["dn0"]), mode="nn", nsh=1, resid=h1, alpha=ALPHA, name="l0_ffn_down",
                           comm=_plan_gather_ici([sh["dn1"]]))
    h2, h2b = _ln_fwd(xin2, sm["ln2_g"][0:1], sm["ln2_b"][0:1], "l0_ffn_ln")
    pre, got = _matmul(h2b, wg["sg_in"], mode="nn", nsh=N_CHIPS, name="sg_in", comm=_plan_gather_pass(landed))
    wg["dn1"] = got[0]
    ysg = _sg_gate_fwd(pre, sm["sg_ln_g"], sm["sg_ln_b"], sm["sg_w_s"], sm["sg_b_s_t"], "sg_gate")
    xin3 = _matmul(ysg, _stacked(wg["sg_out"]), mode="nn", nsh=1, resid=h2, alpha=ALPHA, name="sg_out")
    h3, h3b = _ln_fwd(xin3, sm["ln1_g"][1:2], sm["ln1_b"][1:2], "l1_ln1")
    u1 = _matmul(h3b, wg["up1"], mode="nn", nsh=N_CHIPS, name="l1_ffn_up")
    gact1 = _conv_gate_fwd(u1, sm["conv_w"][1], sm["conv_b"][1:2], "l1_ffn_gate")
    xin4 = _matmul(gact1, _stacked(wg["dn1"]), mode="nn", nsh=1, resid=h3, alpha=ALPHA, name="l1_ffn_down")

    gs, grad, split, recv, part = {}, {}, {}, {}, {}

    def swap_on(call, keys):
        for k in keys:
            split[k] = _halves(grad[k])
        out, got = call(_plan_swap([split[k] for k in keys]))
        for k, r in zip(keys, got):
            recv[k] = r
            part[k] = _add_half(split[k], r, sel, f"rs_addhalf_{k}")
        return out

    def ffn_bwd(u, gact, hb_in, dxin, dxin_b, w_up, w_down, layer, tag, up, down, waiting):
        dgact = _matmul(dxin_b, _stacked(w_down), mode="nt", nsh=1, name=f"{tag}_ddown")
        grad[down] = _matmul(gact, dxin_b, mode="tn", nsh=1, name=f"{tag}_wdown")
        da, db, dcw, dcb = _conv_gate_bwd(u, dgact, sm["conv_w"][layer], sm["conv_b"][layer:layer + 1], f"{tag}_dgate")
        grad[up] = swap_on(lambda plan: _matmul(hb_in, [da, db], mode="tn", nsh=N_CHIPS, name=f"{tag}_wup", comm=plan),
                           waiting + [down])
        dh = swap_on(lambda plan: _matmul([da, db], w_up, mode="nt", nsh=N_CHIPS, resid=dxin, alpha=ALPHA,
                                          name=f"{tag}_dup", comm=plan), [up])
        return dh, dcw, dcb

    dx, dxb, dg4, db4, loss = _ln_bwd(xin4, tgt, sm["ln2_g"][1:2], sm["ln2_b"][1:2], "l1_ln2_bwd", loss_head=True)
    dh3, dcw1, dcb1 = ffn_bwd(u1, gact1, h3b, dx, dxb, wg["up1"], wg["dn1"], 1, "l1_ffn", "up1", "dn1", [])
    dx, dxb, dg3, db3 = _ln_bwd(xin3, dh3, sm["ln1_g"][1:2], sm["ln1_b"][1:2], "l1_ln1_bwd")
    grad["sg_out"] = _matmul(ysg, dxb, mode="tn", nsh=1, name="sg_wout")
    dysg = swap_on(lambda plan: _matmul(dxb, _stacked(wg["sg_out"]), mode="nt", nsh=1, name="sg_dout", comm=plan),
                   ["sg_out"])
    dpre, gs["sg_w_s"], gs["sg_b_s_t"], gs["sg_ln_g"], gs["sg_ln_b"] = _sg_gate_bwd(
        pre, dysg, sm["sg_ln_g"], sm["sg_ln_b"], sm["sg_w_s"], sm["sg_b_s_t"], "sg_gate_bwd")
    grad["sg_in"] = _matmul(h2b, dpre, mode="tn", nsh=N_CHIPS, name="sg_win")
    dh2 = _matmul(dpre, wg["sg_in"], mode="nt", nsh=N_CHIPS, resid=dx, alpha=ALPHA, name="sg_din")
    dx, dxb, dg2, db2 = _ln_bwd(xin2, dh2, sm["ln2_g"][0:1], sm["ln2_b"][0:1], "l0_ln2_bwd")
    dh1, dcw0, dcb0 = ffn_bwd(u0, gact0, h1b, dx, dxb, wg["up0"], wg["dn0"], 0, "l0_ffn", "up0", "dn0", ["sg_in"])
    dx, dxb, dg1, db1 = _ln_bwd(xin1, dh1, sm["ln1_g"][0:1], sm["ln1_b"][0:1], "l0_ln1_bwd")
    grad["hg_out"] = _matmul(yhg, dxb, mode="tn", nsh=1, name="hg_wout")
    dyhg = swap_on(lambda plan: _matmul(dxb, _stacked(wg["hg_out"]), mode="nt", nsh=1, name="hg_dout", comm=plan),
                   ["hg_out"])
    early = ("dn1", "up1", "sg_out", "sg_in", "dn0", "up0", "hg_out")
    dparts, got = _hgrn2_bwd(proj, sm["lb_logits"], sm["hg_norm_g"], o_raw, states, dyhg, "hgrn2_bwd",
                             comm=_plan_scatter([part[k] for k in early]))
    gs["lb"], gs["hg_norm_g"] = dparts[4], dparts[5]
    gs["ln1_g"] = jnp.concatenate([dg1, dg3], axis=0)
    gs["ln1_b"] = jnp.concatenate([db1, db3], axis=0)
    gs["ln2_g"] = jnp.concatenate([dg2, dg4], axis=0)
    gs["ln2_b"] = jnp.concatenate([db2, db4], axis=0)
    gs["conv_w"] = jnp.stack([dcw0, dcw1], axis=0)
    gs["conv_b"] = jnp.concatenate([dcb0, dcb1], axis=0)
    packed, layout = _pack(gs)
    mine = [_add_own(split[k], recv[k], b, sel, f"rs_addown_{k}") for k, b in zip(early, got)]
    grad["hg_in"], got = _matmul(xb, list(dparts[:4]), mode="tn", nsh=N_CHIPS, name="hg_win",
                                 comm=_merge(_plan_join(mine), _plan_gather_all(packed)))
    red = {k: _whole(f) for k, f in zip(early, got)}
    me8 = jnp.reshape(4 * x + 2 * y + c, (1,)).astype(jnp.int32)
    summed = _unpack(_sum_devices(got[len(early)], packed, me8, "sum_small_grads"), layout)
    gx = swap_on(lambda plan: _matmul(dparts[0], wg["hg_in"], mode="nt", nsh=1, b_off=0, resid=dx, alpha=ALPHA,
                                      name="hg_din_q", comm=plan), ["hg_in"])
    gx, got = _matmul(list(dparts[1:4]), wg["hg_in"], mode="nt", nsh=3, b_off=1, resid=gx, alpha=1.0, name="hg_din_fig",
                      comm=_plan_scatter([part["hg_in"]]))
    mine = _add_own(split["hg_in"], recv["hg_in"], got[0], sel, "rs_addown_hg_in")
    upd = {}
    upd["hg_w_out"], full = _adamw(w["hg_w_out"], [red["hg_out"]], mom["hg_w_out"], var["hg_w_out"], "adamw_hg_w_out",
                                   comm=_plan_join([mine]))
    red["hg_in"] = _whole(full[0])
    for k, src in (("ffn_w_up", ("up0", "up1")), ("ffn_w_down", ("dn0", "dn1")), ("sg_w_in", ("sg_in",)),
                   ("sg_w_out", ("sg_out",)), ("hg_w_in", ("hg_in",))):
        upd[k] = _adamw(w[k], [red[s] for s in src], mom[k], var[k], f"adamw_{k}")
    return loss, gx, summed, upd


_SMALL_ORDER = ("lb", "hg_norm_g", "sg_w_s", "sg_b_s_t", "conv_b", "ln1_g", "ln1_b", "ln2_g", "ln2_b",
                "conv_w", "sg_ln_g", "sg_ln_b")


PACK_ROWS = 512


def _pack(parts):
    flat, layout, off = [], [], 0
    for k in _SMALL_ORDER:
        a = parts[k]
        n = a.size
        pad = (-n) % LANES
        flat.append(jnp.pad(a.reshape(-1), (0, pad)))
        layout.append((k, off, n, a.shape))
        off += n + pad
    flat.append(jnp.zeros(((-off) % (PACK_ROWS * LANES),), F32))
    return jnp.concatenate(flat).reshape(-1, LANES), layout


def _unpack(buf, layout):
    flat = buf.reshape(-1)
    return {k: flat[off:off + n].reshape(shape) for k, off, n, shape in layout}


def kernel(x, lb_logits, hg_w_in, hg_norm_g, hg_w_out, sg_w_in, sg_ln_g, sg_ln_b, sg_w_s, sg_b_s, sg_w_out, ffn_w_up, ffn_conv_w, ffn_conv_b, ffn_w_down, ln1_g, ln1_b, ln2_g, ln2_b, loss_target, m_lb_logits, m_hg_w_in, m_hg_norm_g, m_hg_w_out, m_sg_w_in, m_sg_ln_g, m_sg_ln_b, m_sg_w_s, m_sg_b_s, m_sg_w_out, m_ffn_w_up, m_ffn_conv_w, m_ffn_conv_b, m_ffn_w_down, m_ln1_g, m_ln1_b, m_ln2_g, m_ln2_b, v_lb_logits, v_hg_w_in, v_hg_norm_g, v_hg_w_out, v_sg_w_in, v_sg_ln_g, v_sg_ln_b, v_sg_w_s, v_sg_b_s, v_sg_w_out, v_ffn_w_up, v_ffn_conv_w, v_ffn_conv_b, v_ffn_w_down, v_ln1_g, v_ln1_b, v_ln2_g, v_ln2_b):
    names = ("lb_logits", "hg_w_in", "hg_norm_g", "hg_w_out", "sg_w_in", "sg_ln_g", "sg_ln_b", "sg_w_s", "sg_b_s",
             "sg_w_out", "ffn_w_up", "ffn_conv_w", "ffn_conv_b", "ffn_w_down", "ln1_g", "ln1_b", "ln2_g", "ln2_b")
    w = dict(zip(names, (lb_logits, hg_w_in, hg_norm_g, hg_w_out, sg_w_in, sg_ln_g, sg_ln_b, sg_w_s, sg_b_s,
                         sg_w_out, ffn_w_up, ffn_conv_w, ffn_conv_b, ffn_w_down, ln1_g, ln1_b, ln2_g, ln2_b)))
    mom = dict(zip(names, (m_lb_logits, m_hg_w_in, m_hg_norm_g, m_hg_w_out, m_sg_w_in, m_sg_ln_g, m_sg_ln_b, m_sg_w_s,
                           m_sg_b_s, m_sg_w_out, m_ffn_w_up, m_ffn_conv_w, m_ffn_conv_b, m_ffn_w_down, m_ln1_g,
                           m_ln1_b, m_ln2_g, m_ln2_b)))
    var = dict(zip(names, (v_lb_logits, v_hg_w_in, v_hg_norm_g, v_hg_w_out, v_sg_w_in, v_sg_ln_g, v_sg_ln_b, v_sg_w_s,
                           v_sg_b_s, v_sg_w_out, v_ffn_w_up, v_ffn_conv_w, v_ffn_conv_b, v_ffn_w_down, v_ln1_g,
                           v_ln1_b, v_ln2_g, v_ln2_b)))
    tgt = loss_target[0]
    fq = ffn_conv_w.shape[2]
    dq = sg_ln_g.shape[1]
    cx, cy, _ = _place()
    me = 2 * cx + cy

    wide = max(fq, dq)
    tiny = jnp.concatenate([jnp.pad(ffn_conv_w.reshape(6, fq), ((0, 0), (0, wide - fq))),
                            jnp.pad(sg_ln_g, ((0, 0), (0, wide - dq))),
                            jnp.pad(sg_ln_b, ((0, 0), (0, wide - dq)))], axis=0)
    tiny_all = _allgather_whole(tiny, "gather_small")
    conv_w_full = jnp.transpose(tiny_all[:, 0:6, :fq].reshape(N_CHIPS, 2, 3, fq), (1, 2, 0, 3)).reshape(2, 3, N_CHIPS * fq)
    sm = {"lb_logits": lb_logits, "hg_norm_g": hg_norm_g, "ln1_g": ln1_g, "ln1_b": ln1_b, "ln2_g": ln2_g,
          "ln2_b": ln2_b, "conv_w": conv_w_full, "conv_b": ffn_conv_b,
          "sg_ln_g": tiny_all[:, 6, :dq].reshape(1, N_CHIPS * dq),
          "sg_ln_b": tiny_all[:, 7, :dq].reshape(1, N_CHIPS * dq),
          "sg_w_s": sg_w_s[0], "sg_b_s_t": jnp.transpose(sg_b_s[0])}

    loss_row, grad_x, summed, upd = _step(x, tgt, sm, w, mom, var)
    loss = lax.psum(loss_row[0, 0], ("x", "y", "c"))

    grads = {
        "lb_logits": _lb_logits_grad(lb_logits, summed["lb"], "lb_logits_grad"),
        "hg_norm_g": summed["hg_norm_g"],
        "sg_ln_g": lax.dynamic_slice_in_dim(summed["sg_ln_g"], me * dq, dq, axis=1),
        "sg_ln_b": lax.dynamic_slice_in_dim(summed["sg_ln_b"], me * dq, dq, axis=1),
        "sg_w_s": summed["sg_w_s"][None], "sg_b_s": jnp.transpose(summed["sg_b_s_t"])[None],
        "ffn_conv_w": lax.dynamic_slice_in_dim(summed["conv_w"], me * fq, fq, axis=2),
        "ffn_conv_b": summed["conv_b"],
        "ln1_g": summed["ln1_g"], "ln1_b": summed["ln1_b"], "ln2_g": summed["ln2_g"], "ln2_b": summed["ln2_b"],
    }

    delta, new_m, new_v = {}, {}, {}
    for k, (dlt, mm, vv, gg) in upd.items():
        delta[k], new_m[k], new_v[k], grads[k] = dlt, mm, vv, gg
    small_names = [k for k in names if k not in upd]

    def pack_small(src):
        flat = [src[k].reshape(-1) for k in small_names]
        n = sum(a.size for a in flat)
        flat.append(jnp.zeros(((-n) % (PACK_ROWS * LANES),), F32))
        return jnp.concatenate(flat).reshape(1, -1, LANES)

    outs = _adamw(pack_small(w), [pack_small(grads)[0]], pack_small(mom), pack_small(var), "adamw_small")
    off = 0
    for k in small_names:
        n = w[k].size
        for dst, o in zip((delta, new_m, new_v), outs):
            dst[k] = o.reshape(-1)[off:off + n].reshape(w[k].shape)
        off += n

    return (loss, grad_x[None], *[grads[k] for k in names], *[delta[k] for k in names],
            *[new_m[k] for k in names], *[new_v[k] for k in names])
```

```python
import functools

import jax
import jax.numpy as jnp
from jax import lax
from jax.experimental import pallas as pl
from jax.experimental.pallas import tpu as pltpu

F32 = jnp.float32
BF16 = jnp.bfloat16
MESH = pl.DeviceIdType.MESH

ALPHA = (2 * 2) ** 0.25
LN_EPS = 1e-5
RMS_EPS = 1e-6
ADAM_LR, ADAM_B1, ADAM_B2, ADAM_EPS, ADAM_WD, ADAM_STEP = 0.001, 0.9, 0.999, 1e-08, 0.01, 10

LANES = 128
SUB = 16
TILE = 8
GCHUNK = 128
VMEM_LIMIT = 56 * 1024 * 1024
N_CHIPS = 4
N_DEV = 8

NT = (((1,), (1,)), ((), ()))
TN = (((0,), (0,)), ((), ()))
NN = (((1,), (0,)), ((), ()))


def _pick(dim, prefs):
    for p in prefs:
        if dim % p == 0:
            return p
    return dim


def _params(sem=None, **kw):
    return pltpu.CompilerParams(dimension_semantics=sem, vmem_limit_bytes=VMEM_LIMIT, **kw)


def _sigmoid_pair(x):
    e = jnp.exp(-jnp.abs(x))
    inv = 1.0 / (1.0 + e)
    pos = x >= 0
    return jnp.where(pos, inv, e * inv), jnp.where(pos, e * inv, inv)


def _sigmoid_gate(x):
    t = 0.5 * jnp.tanh(0.5 * x)
    return 0.5 + t, 0.5 - t


def _ln_hat(x):
    mu = jnp.mean(x, axis=-1, keepdims=True)
    xc = x - mu
    var = jnp.mean(xc * xc, axis=-1, keepdims=True)
    rstd = lax.rsqrt(var + LN_EPS)
    return xc * rstd, rstd


def _lower_bound(logits):
    m = jnp.max(logits, axis=0, keepdims=True)
    e = jnp.exp(logits - m)
    return e[0:1, :] / jnp.sum(e, axis=0, keepdims=True)


MATMUL_VMEM_BUDGET = 46 * 1024 * 1024


def _fit_bk(kdim, bm, bn, out_dtype, has_resid, na=1, nb=1):
    fixed = bm * bn * (4 + 2 * jnp.dtype(out_dtype).itemsize + (8 if has_resid else 0))
    best = LANES
    for bk in range(LANES, kdim + 1, LANES):
        if kdim % bk == 0 and fixed + 4 * bk * (bm * na + bn * nb) <= MATMUL_VMEM_BUDGET:
            best = bk
    return best


def _fit_bm_bk(mdim, prefs, kdim, bn, out_dtype, has_resid, na=1, nb=1):
    best = None
    fits = [bm for bm in prefs if mdim % bm == 0][:2] or [mdim]
    for bm in fits:
        bk = _fit_bk(kdim, bm, bn, out_dtype, has_resid, na, nb)
        if best is None or kdim // bk < kdim // best[1]:
            best = (bm, bk)
    return best


def _matmul(a, b, *, mode, name, out_dtype=F32, resid=None, alpha=1.0, b_off=0, nsh=None, comm=None):
    a_parts = a if isinstance(a, (list, tuple)) else [a]
    b_parts = b if isinstance(b, (list, tuple)) else [b]
    n_parts = max(len(a_parts), len(b_parts))
    if mode == "nn":
        m, kdim = a.shape
        _, _, ns = b.shape
        bn = _pick(ns, (1024, 1408, 512, 256, 128))
        bm, bk = _fit_bm_bk(m, (1024, 512, 256, 128), kdim, bn, out_dtype, resid is not None)
        nps = ns // bn
        grid = (m // bm, nsh * nps, kdim // bk)
        a_specs = [pl.BlockSpec((bm, bk), lambda i, j, k: (i, k))]
        b_specs = [pl.BlockSpec((None, bk, bn), lambda i, j, k: (b_off + j // nps, k, j % nps))]
        o_spec = pl.BlockSpec((bm, bn), lambda i, j, k: (i, j))
        out_shape = jax.ShapeDtypeStruct((m, nsh * ns), out_dtype)
        dims, part_axis, per_part = NN, 2, grid[2]
    elif mode == "nt":
        m = a_parts[0].shape[0]
        _, kdim, ns = b.shape
        wide = n_parts > 1
        bn = _pick(kdim, (1024, 1408, 512, 256, 128) if wide else (512, 256, 128))
        bm, bk = _fit_bm_bk(m, (1024, 512, 256, 128) if wide else (2048, 1024, 512, 256, 128), ns, bn, out_dtype,
                            resid is not None, na=n_parts)
        kps = ns // bk
        per_part = nsh // n_parts * kps
        grid = (m // bm, kdim // bn, nsh * kps)
        a_specs = [pl.BlockSpec((bm, bk), lambda i, j, k, p=p: (jnp.where(k // per_part == p, i, 0),
                                                                  jnp.where(k // per_part == p, k % per_part, 0)))
                   for p in range(n_parts)]
        b_specs = [pl.BlockSpec((None, bn, bk), lambda i, j, k: (b_off + k // kps, j, k % kps))]
        o_spec = pl.BlockSpec((bm, bn), lambda i, j, k: (i, j))
        out_shape = jax.ShapeDtypeStruct((m, kdim), out_dtype)
        dims, part_axis = NT, 2
    else:
        t, kdim = a.shape
        ns = b_parts[0].shape[1] * n_parts // nsh
        bn = _pick(ns, (1024, 1408, 512, 256, 128))
        bm, bk = _fit_bm_bk(kdim, (1024, 1408, 512, 256, 128), t, bn, out_dtype, resid is not None, nb=n_parts)
        nps = ns // bn
        per_part = nsh // n_parts * nps
        grid = (kdim // bm, nsh * nps, t // bk)
        a_specs = [pl.BlockSpec((bk, bm), lambda i, j, k: (k, i))]
        b_specs = [pl.BlockSpec((bk, bn), lambda i, j, k, p=p: (jnp.where(j // per_part == p, k, 0),
                                                                  jnp.where(j // per_part == p, j % per_part, 0)))
                   for p in range(n_parts)]
        o_spec = pl.BlockSpec((None, bm, bn), lambda i, j, k: (j // nps, i, j % nps))
        out_shape = jax.ShapeDtypeStruct((nsh, kdim, ns), out_dtype)
        dims, part_axis = TN, 1
    nk = grid[2]
    na, nb_ = len(a_parts), len(b_parts)
    has_resid = resid is not None

    def kern(*refs):
        a_refs, b_refs = refs[:na], refs[na:na + nb_]
        r_ref = refs[na + nb_] if has_resid else None
        k = pl.program_id(2)

        def finish(r, o_ref):
            if has_resid:
                r = r + alpha * r_ref[...]
            o_ref[...] = r.astype(o_ref.dtype)

        def add(a_ref, b_ref):
            if nk == 1:
                finish(lax.dot_general(a_ref[...], b_ref[...], dims, preferred_element_type=F32), refs[-1])
                return
            refs[-1][...] += lax.dot_general(a_ref[...], b_ref[...], dims, preferred_element_type=F32)

        if nk > 1:
            @pl.when(k == 0)
            def _():
                refs[-1][...] = jnp.zeros_like(refs[-1])

        if n_parts == 1:
            add(a_refs[0], b_refs[0])
        else:
            which = pl.program_id(part_axis) // per_part
            for p in range(n_parts):
                pl.when(which == p)(functools.partial(add, a_refs[min(p, na - 1)], b_refs[min(p, nb_ - 1)]))
        if nk > 1:
            @pl.when(k == nk - 1)
            def _():
                finish(refs[-1][...], refs[-2])

    in_specs = a_specs + b_specs
    operands = list(a_parts) + list(b_parts)
    if has_resid:
        in_specs.append(pl.BlockSpec((bm, bn), lambda i, j, k: (i, j)))
        operands.append(resid)
    outs, carried = _carried_call(
        kern, comm, name=name, grid=grid, in_specs=in_specs, out_specs=[o_spec], out_shape=[out_shape],
        operands=operands, scratch_shapes=[pltpu.VMEM((bm, bn), F32)] if nk > 1 else [],
        sem=("parallel", "parallel", "arbitrary"))
    return outs[0] if comm is None else (outs[0], carried)


def _ln_fwd(xin, g, b, name):
    t, d = xin.shape
    tb = _pick(t, (256, 128, 64, 32, 16))

    def kern(x_ref, g_ref, b_ref, h_ref, hb_ref):
        xhat, _ = _ln_hat(x_ref[...])
        h = xhat * g_ref[...] + b_ref[...]
        h_ref[...] = h
        hb_ref[...] = h.astype(BF16)

    row = pl.BlockSpec((tb, d), lambda i: (i, 0))
    vec = pl.BlockSpec((1, d), lambda i: (0, 0))
    return pl.pallas_call(
        kern, name=name, grid=(t // tb,), in_specs=[row, vec, vec], out_specs=[row, row],
        out_shape=[jax.ShapeDtypeStruct((t, d), F32), jax.ShapeDtypeStruct((t, d), BF16)],
        compiler_params=_params(("parallel",)),
    )(xin, g, b)


def _ln_bwd(xin, dy_or_target, g, b, name, loss_head=False):
    t, d = xin.shape
    tb = _pick(t, (256, 128, 64, 32, 16))
    nb = t // tb

    def kern(x_ref, dy_ref, g_ref, b_ref, dx_ref, dxb_ref, dg_ref, db_ref, *rest):
        i = pl.program_id(0)
        xhat, rstd = _ln_hat(x_ref[...])
        gv = g_ref[...]
        if loss_head:
            loss_ref = rest[0]
            err = xhat * gv + b_ref[...] - dy_ref[...]
            dy = err * (1.0 / d)
            part = 0.5 * jnp.sum(jnp.sum(err * err, axis=1, keepdims=True), axis=0, keepdims=True) * (1.0 / d)
        else:
            dy = dy_ref[...]

        @pl.when(i == 0)
        def _():
            dg_ref[...] = jnp.zeros_like(dg_ref)
            db_ref[...] = jnp.zeros_like(db_ref)
            if loss_head:
                loss_ref[...] = jnp.zeros_like(loss_ref)

        dg_ref[...] += jnp.sum(dy * xhat, axis=0, keepdims=True)
        db_ref[...] += jnp.sum(dy, axis=0, keepdims=True)
        if loss_head:
            loss_ref[...] += jnp.broadcast_to(part, loss_ref.shape)
        dxh = dy * gv
        m1 = jnp.mean(dxh, axis=-1, keepdims=True)
        m2 = jnp.mean(dxh * xhat, axis=-1, keepdims=True)
        dx = rstd * (dxh - m1 - xhat * m2)
        dx_ref[...] = dx
        dxb_ref[...] = dx.astype(BF16)

    row = pl.BlockSpec((tb, d), lambda i: (i, 0))
    vec = pl.BlockSpec((1, d), lambda i: (0, 0))
    out_specs = [row, row, vec, vec]
    out_shape = [jax.ShapeDtypeStruct((t, d), F32), jax.ShapeDtypeStruct((t, d), BF16),
                 jax.ShapeDtypeStruct((1, d), F32), jax.ShapeDtypeStruct((1, d), F32)]
    if loss_head:
        out_specs.append(pl.BlockSpec((1, LANES), lambda i: (0, 0)))
        out_shape.append(jax.ShapeDtypeStruct((1, LANES), F32))
    return pl.pallas_call(
        kern, name=name, grid=(nb,), in_specs=[row, row, vec, vec], out_specs=out_specs, out_shape=out_shape,
        compiler_params=_params(("arbitrary",)),
    )(xin, dy_or_target, g, b)


def _conv_gate_fwd(u, conv_w, conv_b, name, comm=None):
    t, f2 = u.shape
    f = f2 // 2
    tb = _pick(t, (512, 256, 128, 64, 32, 16))
    cn = _pick(f, (1408, 1024, 512, 256, 128))
    ncb = f // cn
    hb = tb // 8

    def kern(a_ref, ah_ref, b_ref, w_ref, cb_ref, o_ref):
        i = pl.program_id(0)
        a = a_ref[...]
        halo = jnp.where(i > 0, ah_ref[...], 0.0)
        rid = lax.broadcasted_iota(jnp.int32, a.shape, 0)
        s1 = jnp.where(rid == 0, halo[7:8, :], pltpu.roll(a, 1, 0))
        s2 = jnp.where(rid == 0, halo[6:7, :], jnp.where(rid == 1, halo[7:8, :], pltpu.roll(a, 2, 0)))
        w = w_ref[...]
        conv = w[2:3, :] * a + w[1:2, :] * s1 + w[0:1, :] * s2 + cb_ref[...]
        sp, _ = _sigmoid_gate(conv)
        o_ref[...] = (conv * sp * b_ref[...]).astype(BF16)

    outs, carried = _carried_call(
        kern, comm, name=name, grid=(t // tb, ncb),
        in_specs=[pl.BlockSpec((tb, cn), lambda i, j: (i, j)),
                  pl.BlockSpec((8, cn), lambda i, j: (jnp.maximum(i * hb - 1, 0), j)),
                  pl.BlockSpec((tb, cn), lambda i, j: (i, j + ncb)),
                  pl.BlockSpec((3, cn), lambda i, j: (0, j)),
                  pl.BlockSpec((1, cn), lambda i, j: (0, j))],
        out_specs=[pl.BlockSpec((tb, cn), lambda i, j: (i, j))],
        out_shape=[jax.ShapeDtypeStruct((t, f), BF16)],
        operands=[u, u, u, conv_w, conv_b], sem=("parallel", "parallel"))
    return outs[0] if comm is None else (outs[0], carried)


def _conv_gate_bwd(u, dgact, conv_w, conv_b, name):
    t, f2 = u.shape
    f = f2 // 2
    tb = _pick(t, (512, 256, 128, 64, 32, 16))
    cn = _pick(f, (1408, 1024, 512, 256, 128))
    ncb = f // cn
    hb = tb // 8
    nb = t // tb
    last8 = t // 8 - 1

    def kern(a_ref, ap_ref, an_ref, b_ref, bn_ref, dg_ref, dgn_ref, w_ref, cb_ref,
             da_ref, db_ref, dw_ref, dcb_ref):
        i = pl.program_id(1)
        a = a_ref[...]
        w = w_ref[...]
        ext = jnp.concatenate([jnp.where(i > 0, ap_ref[...], 0.0), a, an_ref[...]], axis=0)
        e1 = pltpu.roll(ext, 1, 0)
        e2 = pltpu.roll(ext, 2, 0)
        conv = (w[2:3, :] * ext + w[1:2, :] * e1 + w[0:1, :] * e2 + cb_ref[...])[8:, :]
        bmn = jnp.concatenate([b_ref[...], bn_ref[...]], axis=0)
        dgmn = jnp.concatenate([dg_ref[...], jnp.where(i < nb - 1, dgn_ref[...], 0.0)], axis=0)
        sp, sn = _sigmoid_gate(conv)
        da = dgmn * bmn * (sp * (1.0 + conv * sn))
        n = tb + 8
        dap = w[2:3, :] * da + w[1:2, :] * pltpu.roll(da, n - 1, 0) + w[0:1, :] * pltpu.roll(da, n - 2, 0)
        da_ref[...] = dap[:tb, :].astype(BF16)
        db_ref[...] = (dg_ref[...] * (conv * sp)[:tb, :]).astype(BF16)
        dam = da[:tb, :]

        @pl.when(i == 0)
        def _():
            dw_ref[...] = jnp.zeros_like(dw_ref)
            dcb_ref[...] = jnp.zeros_like(dcb_ref)

        dw = jnp.concatenate([jnp.sum(dam * e2[8:8 + tb, :], axis=0, keepdims=True),
                              jnp.sum(dam * e1[8:8 + tb, :], axis=0, keepdims=True),
                              jnp.sum(dam * a, axis=0, keepdims=True)], axis=0)
        dw_ref[...] += dw
        dcb_ref[...] += jnp.sum(dam, axis=0, keepdims=True)

    main_a = pl.BlockSpec((tb, cn), lambda j, i: (i, j))
    prev_a = pl.BlockSpec((8, cn), lambda j, i: (jnp.maximum(i * hb - 1, 0), j))
    next_a = pl.BlockSpec((8, cn), lambda j, i: (jnp.minimum((i + 1) * hb, last8), j))
    main_b = pl.BlockSpec((tb, cn), lambda j, i: (i, j + ncb))
    next_b = pl.BlockSpec((8, cn), lambda j, i: (jnp.minimum((i + 1) * hb, last8), j + ncb))
    return pl.pallas_call(
        kern, name=name, grid=(ncb, nb),
        in_specs=[main_a, prev_a, next_a, main_b, next_b, main_a, next_a,
                  pl.BlockSpec((3, cn), lambda j, i: (0, j)), pl.BlockSpec((1, cn), lambda j, i: (0, j))],
        out_specs=[main_a, main_a, pl.BlockSpec((3, cn), lambda j, i: (0, j)),
                   pl.BlockSpec((1, cn), lambda j, i: (0, j))],
        out_shape=[jax.ShapeDtypeStruct((t, f), BF16), jax.ShapeDtypeStruct((t, f), BF16),
                   jax.ShapeDtypeStruct((3, f), F32), jax.ShapeDtypeStruct((1, f), F32)],
        compiler_params=_params(("parallel", "arbitrary")),
    )(u, u, u, u, u, dgact, dgact, conv_w, conv_b)


def _hg_gates(qp, fp, lb):
    sq, _ = _sigmoid_gate(qp)
    sf, snf = _sigmoid_pair(fp)
    forget = lb + (1.0 - lb) * sf
    return sq, sf, snf, forget, jnp.log(forget), (1.0 - lb) * snf


def _tri(lower):
    r = lax.broadcasted_iota(jnp.int32, (SUB, SUB), 0)
    c = lax.broadcasted_iota(jnp.int32, (SUB, SUB), 1)
    return ((r >= c) if lower else (r <= c)).astype(BF16)


def _split2(x):
    hi = x.astype(BF16)
    return hi, (x - hi.astype(F32)).astype(BF16)


def _dot3(a, b, dims):
    (ah, al), (bh, bl) = a, b
    return (lax.dot_general(ah, bh, dims, preferred_element_type=F32)
            + (lax.dot_general(ah, bl, dims, preferred_element_type=F32)
               + lax.dot_general(al, bh, dims, preferred_element_type=F32)))


def _running_sum(tri, x):
    hi, lo = _split2(x)
    rest = (x - hi.astype(F32)) - lo.astype(F32)
    return (lax.dot_general(tri, hi, NN, preferred_element_type=F32)
            + (lax.dot_general(tri, lo, NN, preferred_element_type=F32)
               + lax.dot_general(tri, rest.astype(BF16), NN, preferred_element_type=F32)))


HEADS_PER_STEP = 8
STEP_UNROLL = 2
FWD_STEP_UNROLL = 4


def _hgrn2_fwd(proj, lb_logits, norm_g, name, comm=None):
    t, d4 = proj.shape
    d = d4 // 4
    nh = d // LANES
    hb = _pick(nh, (HEADS_PER_STEP, 2, 1))
    wb = hb * LANES
    tb = _pick(t, (256, 128, 64, 32, 16))
    nb = t // tb
    nsc = tb // SUB

    def kern(q_ref, f_ref, i_ref, g_ref, lbl_ref, ng_ref, y_ref, o_ref, st_ref, s_ref):
        @pl.when(pl.program_id(1) == 0)
        def _():
            s_ref[...] = jnp.zeros_like(s_ref)

        lb_all = _lower_bound(lbl_ref[...])
        ng_all = ng_ref[...]
        ltri = _tri(True)
        rcol = lax.broadcasted_iota(jnp.int32, (SUB, 1), 0)

        heads = [slice(h * LANES, (h + 1) * LANES) for h in range(hb)]

        def step(sc, carry):
            rows = pl.ds(pl.multiple_of(sc * SUB, SUB), SUB)
            qp, fp, v, gp = q_ref[rows, :], f_ref[rows, :], i_ref[rows, :], g_ref[rows, :]
            sq, _, _, _, lf, k = _hg_gates(qp, fp, lb_all)
            q = qp * sq
            bl = _running_sum(ltri, lf)
            bend = bl[SUB - 1:SUB, :]
            dec = jnp.exp(bend)
            qs2 = _split2(q * jnp.exp(bl))
            kd2 = _split2(k * jnp.exp(bend - bl))
            v2 = _split2(v)
            states = [s_ref[h] for h in range(hb)]
            o = [_dot3((qs2[0][:, c], qs2[1][:, c]), _split2(states[h]), NT) for h, c in enumerate(heads)]
            top, bot = [oh[:TILE] for oh in o], [oh[TILE:] for oh in o]
            for s in range(SUB):
                lo = 0 if s < TILE else TILE
                e = jnp.exp(jnp.minimum(bl[lo:] - bl[s:s + 1, :], 0.0))
                p = q[lo:] * e * k[s:s + 1, :]
                for h, c in enumerate(heads):
                    a = jnp.sum(p[:, c], axis=1, keepdims=True)
                    add = jnp.where(rcol[lo:] >= s, a, 0.0) * v[s:s + 1, c]
                    if lo == 0:
                        top[h], bot[h] = top[h] + add[:TILE], bot[h] + add[TILE:]
                    else:
                        bot[h] = bot[h] + add
            o = [jnp.concatenate([a, b], axis=0) for a, b in zip(top, bot)]
            for h, c in enumerate(heads):
                st_ref[sc, h] = states[h]
                s_ref[h] = states[h] * dec[:, c] + _dot3((v2[0][:, c], v2[1][:, c]), (kd2[0][:, c], kd2[1][:, c]), TN)
            o_ref[rows, :] = jnp.concatenate(o, axis=1)
            on = jnp.concatenate(
                [oh * lax.rsqrt(jnp.mean(oh * oh, axis=1, keepdims=True) + RMS_EPS) for oh in o], axis=1)
            sg, _ = _sigmoid_gate(gp)
            y_ref[rows, :] = (on * ng_all * (gp * sg)).astype(BF16)
            return carry

        lax.fori_loop(0, nsc, step, 0, unroll=FWD_STEP_UNROLL)

    def col(off):
        return pl.BlockSpec((tb, wb), lambda h, j: (j, h + off * (nh // hb)))

    return _carried_call(
        kern, comm, name=name, grid=(nh // hb, nb),
        in_specs=[col(0), col(1), col(2), col(3),
                  pl.BlockSpec((3, wb), lambda h, j: (0, h)), pl.BlockSpec((1, wb), lambda h, j: (0, h))],
        out_specs=[col(0), col(0), pl.BlockSpec((nsc, hb, LANES, LANES), lambda h, j: (j, h, 0, 0))],
        out_shape=[jax.ShapeDtypeStruct((t, d), BF16), jax.ShapeDtypeStruct((t, d), F32),
                   jax.ShapeDtypeStruct((t // SUB, nh, LANES, LANES), F32)],
        scratch_shapes=[pltpu.VMEM((hb, LANES, LANES), F32)],
        operands=[proj, proj, proj, proj, lb_logits, norm_g], sem=("parallel", "arbitrary"))


def _hgrn2_bwd(proj, lb_logits, norm_g, o_raw, states, dy, name, comm=None):
    t, d4 = proj.shape
    d = d4 // 4
    nh = d // LANES
    hb = _pick(nh, (HEADS_PER_STEP, 2, 1))
    wb = hb * LANES
    tb = _pick(t, (256, 128, 64, 32, 16))
    nb = t // tb
    nsc = tb // SUB

    def kern(q_ref, f_ref, i_ref, g_ref, lbl_ref, ng_ref, o_ref, st_ref, dy_ref,
             dq_ref, df_ref, di_ref, dgp_ref, dlb_ref, dng_ref, ds_ref, gc_ref):
        j = pl.program_id(1)

        @pl.when(j == 0)
        def _():
            ds_ref[...] = jnp.zeros_like(ds_ref)
            gc_ref[...] = jnp.zeros_like(gc_ref)
            dlb_ref[...] = jnp.zeros_like(dlb_ref)
            dng_ref[...] = jnp.zeros_like(dng_ref)

        lb_all = _lower_bound(lbl_ref[...])
        ng_all = ng_ref[...]
        ltri, utri = _tri(True), _tri(False)
        rcol = lax.broadcasted_iota(jnp.int32, (SUB, 1), 0)
        rid = lax.broadcasted_iota(jnp.int32, (SUB, wb), 0)

        heads = [slice(h * LANES, (h + 1) * LANES) for h in range(hb)]

        def per_head(fn, n=SUB):
            return jnp.concatenate([jnp.broadcast_to(fn(c), (n, LANES)) for c in heads], axis=1)

        def step(it, carry):
            sc = nsc - 1 - it
            rows = pl.ds(pl.multiple_of(sc * SUB, SUB), SUB)
            qp, fp, v, gp = q_ref[rows, :], f_ref[rows, :], i_ref[rows, :], g_ref[rows, :]
            o, dyv = o_ref[rows, :], dy_ref[rows, :]
            sq, sf, snf, forget, lf, k = _hg_gates(qp, fp, lb_all)
            q = qp * sq
            bl = _running_sum(ltri, lf)
            ebl = jnp.exp(bl)
            bend = bl[SUB - 1:SUB, :]
            dec = jnp.exp(bend)
            dte = jnp.exp(bend - bl)
            r = per_head(lambda c: lax.rsqrt(jnp.mean(o[:, c] * o[:, c], axis=1, keepdims=True) + RMS_EPS))
            ohat = o * r
            sg, sng = _sigmoid_gate(gp)
            don = dyv * (gp * sg)
            dgp_ref[rows, :] = (dyv * (ohat * ng_all) * (sg * (1.0 + gp * sng))).astype(BF16)
            dng_ref[...] += jnp.sum(don * ohat, axis=0, keepdims=True)
            doh = don * ng_all
            dot_oh = doh * ohat
            do = r * (doh - ohat * per_head(lambda c: jnp.mean(dot_oh[:, c], axis=1, keepdims=True)))
            do2, qs2, kd2, v2 = _split2(do), _split2(q * ebl), _split2(k * dte), _split2(v)
            dq_h, dk_h, dv_h = [], [], []
            for h, c in enumerate(heads):
                dstate = ds_ref[h]
                ds2 = _split2(dstate)
                doc = (do2[0][:, c], do2[1][:, c])
                dq_h.append(_dot3(doc, _split2(st_ref[sc, h]), NN))
                dv_h.append(_dot3((kd2[0][:, c], kd2[1][:, c]), ds2, NT))
                dk_h.append(_dot3((v2[0][:, c], v2[1][:, c]), ds2, NN))
                ds_ref[h] = dstate * dec[:, c] + _dot3(doc, (qs2[0][:, c], qs2[1][:, c]), TN)
            dq = jnp.concatenate(dq_h, axis=1) * ebl
            dk = jnp.concatenate(dk_h, axis=1) * dte
            dv = jnp.concatenate(dv_h, axis=1)
            dq_t, dq_b = dq[:TILE], dq[TILE:]
            dk_i = [jnp.zeros((TILE, wb), F32), jnp.zeros((TILE, wb), F32)]
            dv_i = [jnp.zeros((TILE, wb), F32), jnp.zeros((TILE, wb), F32)]
            for s in range(SUB):
                lo = 0 if s < TILE else TILE
                n = SUB - lo
                e = jnp.exp(jnp.minimum(bl[lo:] - bl[s:s + 1, :], 0.0))
                qe = q[lo:] * e
                ks = k[s:s + 1, :]
                live = rcol[lo:] >= s
                pk = qe * ks
                dor = do[lo:]
                pv = dor * v[s:s + 1, :]
                a = per_head(lambda c: jnp.where(live, jnp.sum(pk[:, c], axis=1, keepdims=True), 0.0), n)
                da = per_head(lambda c: jnp.where(live, jnp.sum(pv[:, c], axis=1, keepdims=True), 0.0), n)
                ddq = da * (e * ks)
                if lo == 0:
                    dq_t, dq_b = dq_t + ddq[:TILE], dq_b + ddq[TILE:]
                else:
                    dq_b = dq_b + ddq
                here = rid[:TILE] == s - lo
                dk_i[lo // TILE] = jnp.where(here, jnp.sum(da * qe, axis=0, keepdims=True), dk_i[lo // TILE])
                dv_i[lo // TILE] = jnp.where(here, jnp.sum(a * dor, axis=0, keepdims=True), dv_i[lo // TILE])
            dq = jnp.concatenate([dq_t, dq_b], axis=0)
            dk = dk + jnp.concatenate(dk_i, axis=0)
            dv = dv + jnp.concatenate(dv_i, axis=0)
            w = q * dq - k * dk
            gc = gc_ref[...]
            dlf = _running_sum(utri, w) + gc
            gc_ref[...] = gc + jnp.sum(w, axis=0, keepdims=True)
            t1 = dlf / forget - dk
            df_ref[rows, :] = ((1.0 - lb_all) * sf * snf * t1).astype(BF16)
            dlb_ref[...] += jnp.sum(snf * t1, axis=0, keepdims=True)
            dq_ref[rows, :] = (dq * (sq * (1.0 + qp * (1.0 - sq)))).astype(BF16)
            di_ref[rows, :] = dv.astype(BF16)
            return carry

        lax.fori_loop(0, nsc, step, 0, unroll=STEP_UNROLL)

    def col(off):
        return pl.BlockSpec((tb, wb), lambda h, j: (nb - 1 - j, h + off * (nh // hb)))

    vec = pl.BlockSpec((1, wb), lambda h, j: (0, h))
    return _carried_call(
        kern, comm, name=name, grid=(nh // hb, nb),
        in_specs=[col(0), col(1), col(2), col(3), pl.BlockSpec((3, wb), lambda h, j: (0, h)), vec,
                  col(0), pl.BlockSpec((nsc, hb, LANES, LANES), lambda h, j: (nb - 1 - j, h, 0, 0)), col(0)],
        out_specs=[col(0), col(0), col(0), col(0), vec, vec],
        out_shape=[jax.ShapeDtypeStruct((t, d), BF16)] * 4 + [jax.ShapeDtypeStruct((1, d), F32)] * 2,
        scratch_shapes=[pltpu.VMEM((hb, LANES, LANES), F32), pltpu.VMEM((1, wb), F32)],
        operands=[proj, proj, proj, proj, lb_logits, norm_g, o_raw, states, dy], sem=("parallel", "arbitrary"))


_INV_SQRT2 = 0.7071067811865476
_INV_SQRT2PI = 0.3989422804014327


def _gelu(x):
    return 0.5 * x * (1.0 + lax.erf(x * _INV_SQRT2))


def _gelu_grad(x):
    return 0.5 * (1.0 + lax.erf(x * _INV_SQRT2)) + x * jnp.exp(-0.5 * x * x) * _INV_SQRT2PI


def _causal(w):
    r = lax.broadcasted_iota(jnp.int32, (GCHUNK, GCHUNK), 0)
    c = lax.broadcasted_iota(jnp.int32, (GCHUNK, GCHUNK), 1)
    return jnp.where(r >= c, w, 0.0)


def _sg_gate_fwd(pre, ln_g, ln_b, w_s, b_s_t, name):
    t, d2 = pre.shape
    d = d2 // 2
    ng = d // LANES

    def kern(pre_ref, g_ref, b_ref, ws_ref, bs_ref, y_ref):
        z = _gelu(pre_ref[...])
        u = z[:, :d]
        vhat, _ = _ln_hat(z[:, d:])
        vn = (vhat * g_ref[...] + b_ref[...]).astype(BF16)
        bs = bs_ref[...]
        for g in range(ng):
            cols = slice(g * LANES, (g + 1) * LANES)
            wc = _causal(ws_ref[g]).astype(BF16)
            gate = jnp.dot(wc, vn[:, cols], preferred_element_type=F32) + bs[:, g:g + 1]
            y_ref[:, cols] = (u[:, cols] * gate).astype(BF16)

    vec = pl.BlockSpec((1, d), lambda i: (0, 0))
    return pl.pallas_call(
        kern, name=name, grid=(t // GCHUNK,),
        in_specs=[pl.BlockSpec((GCHUNK, d2), lambda i: (i, 0)), vec, vec,
                  pl.BlockSpec((ng, GCHUNK, GCHUNK), lambda i: (0, 0, 0)),
                  pl.BlockSpec((GCHUNK, ng), lambda i: (0, 0))],
        out_specs=pl.BlockSpec((GCHUNK, d), lambda i: (i, 0)),
        out_shape=jax.ShapeDtypeStruct((t, d), BF16),
        compiler_params=_params(("parallel",)),
    )(pre, ln_g, ln_b, w_s, b_s_t)


def _sg_gate_bwd(pre, dy, ln_g, ln_b, w_s, b_s_t, name):
    t, d2 = pre.shape
    d = d2 // 2
    ng = d // LANES

    def kern(pre_ref, dy_ref, g_ref, b_ref, ws_ref, bs_ref, dpre_ref, dws_ref, dbs_ref, dg_ref, db_ref, dvn_ref):
        @pl.when(pl.program_id(0) == 0)
        def _():
            dws_ref[...] = jnp.zeros_like(dws_ref)
            dbs_ref[...] = jnp.zeros_like(dbs_ref)
            dg_ref[...] = jnp.zeros_like(dg_ref)
            db_ref[...] = jnp.zeros_like(db_ref)

        pre = pre_ref[...]
        z = _gelu(pre)
        u = z[:, :d]
        vhat, rstd = _ln_hat(z[:, d:])
        gv = g_ref[...]
        vn = (vhat * gv + b_ref[...]).astype(BF16)
        bs = bs_ref[...]
        dyv = dy_ref[...]
        gp = _gelu_grad(pre)
        lane = lax.broadcasted_iota(jnp.int32, (GCHUNK, ng), 1)
        dbs = jnp.zeros((GCHUNK, ng), F32)
        for g in range(ng):
            cols = slice(g * LANES, (g + 1) * LANES)
            wc = _causal(ws_ref[g]).astype(BF16)
            vng = vn[:, cols]
            gate = jnp.dot(wc, vng, preferred_element_type=F32) + bs[:, g:g + 1]
            dpre_ref[:, cols] = (dyv[:, cols] * gate * gp[:, cols]).astype(BF16)
            dgate = dyv[:, cols] * u[:, cols]
            dbs = dbs + jnp.where(lane == g, jnp.sum(dgate, axis=1, keepdims=True), 0.0)
            dgb = dgate.astype(BF16)
            dws_ref[g] += _causal(lax.dot_general(dgb, vng, NT, preferred_element_type=F32))
            dvn_ref[:, cols] = lax.dot_general(wc, dgb, TN, preferred_element_type=F32)
        dbs_ref[...] += dbs
        dvn = dvn_ref[...]
        dg_ref[...] += jnp.sum(dvn * vhat, axis=0, keepdims=True)
        db_ref[...] += jnp.sum(dvn, axis=0, keepdims=True)
        dvh = dvn * gv
        m1 = jnp.mean(dvh, axis=-1, keepdims=True)
        m2 = jnp.mean(dvh * vhat, axis=-1, keepdims=True)
        dpre_ref[:, d:] = (rstd * (dvh - m1 - vhat * m2) * gp[:, d:]).astype(BF16)

    vec = pl.BlockSpec((1, d), lambda i: (0, 0))
    wsp = pl.BlockSpec((ng, GCHUNK, GCHUNK), lambda i: (0, 0, 0))
    bsp = pl.BlockSpec((GCHUNK, ng), lambda i: (0, 0))
    return pl.pallas_call(
        kern, name=name, grid=(t // GCHUNK,),
        in_specs=[pl.BlockSpec((GCHUNK, d2), lambda i: (i, 0)), pl.BlockSpec((GCHUNK, d), lambda i: (i, 0)),
                  vec, vec, wsp, bsp],
        out_specs=[pl.BlockSpec((GCHUNK, d2), lambda i: (i, 0)), wsp, bsp, vec, vec],
        out_shape=[jax.ShapeDtypeStruct((t, d2), BF16), jax.ShapeDtypeStruct((ng, GCHUNK, GCHUNK), F32),
                   jax.ShapeDtypeStruct((GCHUNK, ng), F32), jax.ShapeDtypeStruct((1, d), F32),
                   jax.ShapeDtypeStruct((1, d), F32)],
        scratch_shapes=[pltpu.VMEM((GCHUNK, d), F32)],
        compiler_params=_params(("arbitrary",)),
    )(pre, dy, ln_g, ln_b, w_s, b_s_t)


def _adamw_math(w, g, m, v):
    m = ADAM_B1 * m + (1.0 - ADAM_B1) * g
    v = ADAM_B2 * v + (1.0 - ADAM_B2) * (g * g)
    m_hat = m / (1.0 - ADAM_B1 ** ADAM_STEP)
    v_hat = v / (1.0 - ADAM_B2 ** ADAM_STEP)
    return -ADAM_LR * (m_hat / (jnp.sqrt(v_hat) + ADAM_EPS) + ADAM_WD * w), m, v


ADAMW_BLOCK_BYTES = 3 << 19


def _adamw(w, gs, m, v, name, comm=None):
    nl, r, c = w.shape
    rb = _pick(r, tuple(p for p in (512, 256, 128, 64, 32, 16, 8) if p * c * 4 <= ADAMW_BLOCK_BYTES))
    n_out = 4 if nl > 1 else 3

    def kern(w_ref, m_ref, v_ref, *rest):
        g_refs, out_refs = rest[:nl], rest[nl:]
        layer = pl.program_id(0)
        g = g_refs[0][...]
        for k in range(1, nl):
            g = jnp.where(layer == k, g_refs[k][...], g)
        dlt, mm, vv = _adamw_math(w_ref[...], g, m_ref[...], v_ref[...])
        out_refs[0][...] = dlt
        out_refs[1][...] = mm
        out_refs[2][...] = vv
        if nl > 1:
            out_refs[3][...] = g

    blk = pl.BlockSpec((None, rb, c), lambda l, i: (l, i, 0))
    g_specs = [pl.BlockSpec((rb, c), lambda l, i, k=k: (jnp.where(l == k, i, 0), 0)) for k in range(nl)]
    outs, carried = _carried_call(
        kern, comm, name=name, grid=(nl, r // rb), in_specs=[blk] * 3 + g_specs, out_specs=[blk] * n_out,
        out_shape=[jax.ShapeDtypeStruct((nl, r, c), F32)] * n_out, operands=[w, m, v, *gs],
        sem=("parallel", "parallel"))
    if nl == 1:
        outs = outs + [gs[0].reshape(1, r, c)]
    return outs if comm is None else (outs, carried)


CAST_BLOCK_BYTES = 1 << 21


def _cast_bf16(items, name, comm=None):
    metas, start = [], 0
    for arr, _ in items:
        _, r, c = arr.shape
        rb = _pick(r, tuple(p for p in (1024, 512, 256, 128, 64, 32, 16) if p * c * 4 <= CAST_BLOCK_BYTES))
        metas.append((start, r // rb, rb, c))
        start += r // rb
    n = len(items)

    def spec(p, layer=None):
        s0, steps, rb, c = metas[p]
        if layer is None:
            return pl.BlockSpec((rb, c), lambda s: (jnp.clip(s - s0, 0, steps - 1), 0))
        return pl.BlockSpec((None, rb, c), lambda s: (layer, jnp.clip(s - s0, 0, steps - 1), 0))

    def kern(*refs):
        s = pl.program_id(0)
        for p in range(n):
            s0, steps, _, _ = metas[p]

            @pl.when(jnp.logical_and(s >= s0, s < s0 + steps))
            def _(p=p):
                refs[n + p][...] = refs[p][...].astype(BF16)

    outs, carried = _carried_call(
        kern, comm, name=name, grid=(start,), in_specs=[spec(p, layer) for p, (_, layer) in enumerate(items)],
        out_specs=[spec(p) for p in range(n)],
        out_shape=[jax.ShapeDtypeStruct(arr.shape[1:], BF16) for arr, _ in items],
        operands=[arr for arr, _ in items], sem=("arbitrary",))
    return outs, carried


def _comm_call(plan, name):
    ni, no = len(plan.ins), len(plan.out_shapes)

    def body(*refs):
        copies = plan.build(refs[:ni], refs[ni:ni + no], refs[-2], refs[-1], 0)
        for cp in copies:
            cp.start()
        for cp in copies:
            cp.wait()

    anyspec = pl.BlockSpec(memory_space=pl.ANY)
    return pl.pallas_call(
        body, name=name, in_specs=[anyspec] * ni, out_specs=[anyspec] * no, out_shape=plan.out_shapes,
        input_output_aliases=plan.aliases,
        scratch_shapes=[pltpu.SemaphoreType.DMA((plan.n_sems,)), pltpu.SemaphoreType.DMA((plan.n_sems,))],
        compiler_params=pltpu.CompilerParams(has_side_effects=True),
    )(*plan.ins)


def _lb_logits_grad(lb_logits, dlb, name):
    def kern(l_ref, d_ref, o_ref):
        lg = l_ref[...]
        m = jnp.max(lg, axis=0, keepdims=True)
        e = jnp.exp(lg - m)
        p = e / jnp.sum(e, axis=0, keepdims=True)
        row = lax.broadcasted_iota(jnp.int32, lg.shape, 0)
        o_ref[...] = d_ref[...] * p[0:1, :] * (jnp.where(row == 0, 1.0, 0.0) - p)

    return pl.pallas_call(kern, name=name, out_shape=jax.ShapeDtypeStruct(lb_logits.shape, F32))(lb_logits, dlb)


def _sum_devices(others, own, me, name):
    n, r, c = others.shape
    rb = _pick(r, (512, 256, 128, 64, 32, 16, 8))

    def kern(me_ref, a_ref, own_ref, o_ref):
        mine = own_ref[...]
        acc = jnp.where(me_ref[0] == 0, mine, a_ref[0])
        for i in range(1, n):
            acc = acc + jnp.where(me_ref[0] == i, mine, a_ref[i])
        o_ref[...] = acc

    return pl.pallas_call(
        kern, name=name,
        grid_spec=pltpu.PrefetchScalarGridSpec(
            num_scalar_prefetch=1, grid=(r // rb,),
            in_specs=[pl.BlockSpec((n, rb, c), lambda i, s: (0, i, 0)), pl.BlockSpec((rb, c), lambda i, s: (i, 0))],
            out_specs=pl.BlockSpec((rb, c), lambda i, s: (i, 0))),
        out_shape=jax.ShapeDtypeStruct((r, c), F32),
        compiler_params=_params(("parallel",)),
    )(me, others, own)


def _place():
    x, y, c = lax.axis_index("x"), lax.axis_index("y"), lax.axis_index("c")
    return x, y, c


class _Plan:
    def __init__(self, ins, out_shapes, aliases, n_sems, build):
        self.ins, self.out_shapes, self.aliases, self.n_sems, self.build = list(ins), list(out_shapes), aliases, n_sems, build


def _merge(*plans):
    ins, outs, aliases, subs, sems = [], [], {}, [], 0
    for p in plans:
        for k, v in p.aliases.items():
            aliases[len(ins) + k] = len(outs) + v
        subs.append((p, len(ins), len(outs), sems))
        ins += p.ins
        outs += p.out_shapes
        sems += p.n_sems

    def build(in_refs, out_refs, send_sems, recv_sems, base):
        copies = []
        for p, i0, o0, s0 in subs:
            copies += p.build(in_refs[i0:i0 + len(p.ins)], out_refs[o0:o0 + len(p.out_shapes)], send_sems, recv_sems,
                              base + s0)
        return copies

    return _Plan(ins, outs, aliases, sems, build)


def _remote(src, dst, send_sems, recv_sems, k, to):
    return pltpu.make_async_remote_copy(src_ref=src, dst_ref=dst, send_sem=send_sems.at[k], recv_sem=recv_sems.at[k],
                                        device_id=to, device_id_type=MESH)


def _plan_gather_ici(shards):
    n = len(shards)

    def build(ins, outs, send_sems, recv_sems, base):
        x, y, c = _place()
        me = 2 * x + y
        copies = []
        for a in range(n):
            h = ins[a].shape[0] // 2
            rows = pl.ds(c * h, h)
            for r in (1, 2, 3):
                px, py, _ = _chip_rel(x, y, r)
                copies.append(_remote(ins[a].at[rows, :], outs[a].at[me, rows, :], send_sems, recv_sems,
                                      base + 4 * a + r - 1, (px, py, c)))
            copies.append(_remote(ins[a], outs[a].at[me], send_sems, recv_sems, base + 4 * a + 3, (x, y, 1 - c)))
        return copies

    return _Plan(shards, [jax.ShapeDtypeStruct((N_CHIPS,) + s.shape, s.dtype) for s in shards], {}, 4 * n, build)


def _plan_gather_pass(gathered):
    n = len(gathered)

    def build(ins, outs, send_sems, recv_sems, base):
        x, y, c = _place()
        copies = []
        for a in range(n):
            h = outs[a].shape[1] // 2
            rows = pl.ds(c * h, h)
            for r in (1, 2, 3):
                _, _, shard = _chip_rel(x, y, r)
                piece = outs[a].at[shard, rows, :]
                copies.append(_remote(piece, piece, send_sems, recv_sems, base + 3 * a + r - 1, (x, y, 1 - c)))
        return copies

    return _Plan(gathered, [jax.ShapeDtypeStruct(g.shape, g.dtype) for g in gathered], {a: a for a in range(n)},
                 3 * n, build)


def _plan_swap(split):
    n = len(split)

    def build(ins, outs, send_sems, recv_sems, base):
        x, y, c = _place()
        return [_remote(ins[a].at[j, 1 - c], outs[a].at[j], send_sems, recv_sems, base + N_CHIPS * a + j, (x, y, 1 - c))
                for a in range(n) for j in range(N_CHIPS)]

    return _Plan(split, [jax.ShapeDtypeStruct((N_CHIPS,) + g.shape[2:], g.dtype) for g in split], {}, N_CHIPS * n, build)


def _plan_scatter(parts):
    n = len(parts)

    def build(ins, outs, send_sems, recv_sems, base):
        x, y, c = _place()
        copies = []
        for a in range(n):
            for r in (1, 2, 3):
                px, py, shard = _chip_rel(x, y, r)
                copies.append(_remote(ins[a].at[shard], outs[a].at[r - 1], send_sems, recv_sems, base + 3 * a + r - 1,
                                      (px, py, c)))
        return copies

    return _Plan(parts, [jax.ShapeDtypeStruct((3,) + p.shape[1:], p.dtype) for p in parts], {}, 3 * n, build)


def _plan_join(bufs):
    n = len(bufs)

    def build(ins, outs, send_sems, recv_sems, base):
        x, y, c = _place()
        return [_remote(outs[a].at[c], outs[a].at[c], send_sems, recv_sems, base + a, (x, y, 1 - c)) for a in range(n)]

    return _Plan(bufs, [jax.ShapeDtypeStruct(b.shape, b.dtype) for b in bufs], {a: a for a in range(n)}, n, build)


def _carried_call(kern, plan, *, name, grid, in_specs, out_specs, out_shape, operands, scratch_shapes=(),
                  aliases=None, sem=None):
    n_in, n_out, n_sc = len(operands), len(out_shape), len(scratch_shapes)
    aliases = dict(aliases or {})
    if plan is None:
        outs = pl.pallas_call(kern, name=name, grid=grid, in_specs=in_specs, out_specs=out_specs, out_shape=out_shape,
                              scratch_shapes=list(scratch_shapes), input_output_aliases=aliases,
                              compiler_params=_params(sem))(*operands)
        return list(outs), []
    ci, co = len(plan.ins), len(plan.out_shapes)
    for k, v in plan.aliases.items():
        aliases[n_in + k] = n_out + v
    steps = tuple(grid)

    def body(*refs):
        ins, cins = refs[:n_in], refs[n_in:n_in + ci]
        outs = refs[n_in + ci:n_in + ci + n_out]
        couts = refs[n_in + ci + n_out:n_in + ci + n_out + co]
        scratch = refs[n_in + ci + n_out + co:n_in + ci + n_out + co + n_sc]
        send_sems, recv_sems = refs[-2], refs[-1]
        first = functools.reduce(jnp.logical_and, [pl.program_id(a) == 0 for a in range(len(steps))])
        last = functools.reduce(jnp.logical_and, [pl.program_id(a) == steps[a] - 1 for a in range(len(steps))])

        @pl.when(first)
        def _():
            for cp in plan.build(cins, couts, send_sems, recv_sems, 0):
                cp.start()

        kern(*ins, *outs, *scratch)

        @pl.when(last)
        def _():
            for cp in plan.build(cins, couts, send_sems, recv_sems, 0):
                cp.wait()

    anyspec = pl.BlockSpec(memory_space=pl.ANY)
    outs = pl.pallas_call(
        body, name=name, grid=grid, in_specs=list(in_specs) + [anyspec] * ci,
        out_specs=list(out_specs) + [anyspec] * co, out_shape=list(out_shape) + plan.out_shapes,
        scratch_shapes=list(scratch_shapes) + [pltpu.SemaphoreType.DMA((plan.n_sems,)),
                                               pltpu.SemaphoreType.DMA((plan.n_sems,))],
        input_output_aliases=aliases,
        compiler_params=_params(("arbitrary",) * len(steps)),
    )(*operands, *plan.ins)
    return list(outs[:n_out]), list(outs[n_out:])


def _chip_rel(x, y, r):
    px = x if r < 2 else 1 - x
    py = y if r % 2 == 0 else 1 - y
    return px, py, 2 * px + py


def _allgather_whole(arr, name):
    def body(in_ref, out_ref, send_sems, recv_sems, loc_sem):
        x, y, c = _place()
        me = 2 * x + y
        local = pltpu.make_async_copy(in_ref, out_ref.at[me], loc_sem)
        local.start()
        sends = []
        for r in (1, 2, 3):
            px, py, _ = _chip_rel(x, y, r)
            sends.append(pltpu.make_async_remote_copy(
                src_ref=in_ref, dst_ref=out_ref.at[me], send_sem=send_sems.at[r - 1], recv_sem=recv_sems.at[r - 1],
                device_id=(px, py, c), device_id_type=MESH))
        for cp in sends:
            cp.start()
        for r in (1, 2, 3):
            px, py, shard = _chip_rel(x, y, r)
            pltpu.make_async_remote_copy(
                src_ref=in_ref, dst_ref=out_ref.at[shard], send_sem=send_sems.at[r - 1], recv_sem=recv_sems.at[r - 1],
                device_id=(px, py, c), device_id_type=MESH).wait_recv()
        for cp in sends:
            cp.wait_send()
        local.wait()

    anyspec = pl.BlockSpec(memory_space=pl.ANY)
    return pl.pallas_call(
        body, name=name, in_specs=[anyspec], out_specs=anyspec,
        out_shape=jax.ShapeDtypeStruct((N_CHIPS,) + arr.shape, arr.dtype),
        scratch_shapes=[pltpu.SemaphoreType.DMA((3,)), pltpu.SemaphoreType.DMA((3,)), pltpu.SemaphoreType.DMA],
        compiler_params=pltpu.CompilerParams(has_side_effects=True),
    )(arr)


def _plan_gather_all(buf):
    def build(ins, outs, send_sems, recv_sems, base):
        x, y, c = _place()
        me = 4 * x + 2 * y + c
        copies = []
        for r in range(1, N_DEV):
            px, py, _ = _chip_rel(x, y, r // 2)
            pc = c if r % 2 == 0 else 1 - c
            copies.append(_remote(ins[0], outs[0].at[me], send_sems, recv_sems, base + r - 1, (px, py, pc)))
        return copies

    return _Plan([buf, jnp.zeros((N_DEV,) + buf.shape, buf.dtype)],
                 [jax.ShapeDtypeStruct((N_DEV,) + buf.shape, buf.dtype)], {1: 0}, N_DEV - 1, build)


def _add_half(grad, recv, sel, name):
    _, _, rh, cw = grad.shape
    rb = _pick(rh, (512, 256, 176, 128, 64, 32, 16, 8))

    def kern(sel_ref, g_ref, r_ref, o_ref):
        o_ref[...] = (g_ref[...] + r_ref[...]).astype(BF16)

    return pl.pallas_call(
        kern, name=name,
        grid_spec=pltpu.PrefetchScalarGridSpec(
            num_scalar_prefetch=1, grid=(N_CHIPS - 1, rh // rb),
            in_specs=[pl.BlockSpec((None, None, rb, cw), lambda j, i, s: (s[2 + j], s[0], i, 0)),
                      pl.BlockSpec((None, rb, cw), lambda j, i, s: (s[2 + j], i, 0))],
            out_specs=pl.BlockSpec((None, rb, cw), lambda j, i, s: (s[2 + j], i, 0))),
        out_shape=jax.ShapeDtypeStruct((N_CHIPS, rh, cw), BF16),
        compiler_params=_params(("parallel", "parallel")),
    )(sel, grad, recv)


def _add_own(grad, recv, got, sel, name):
    _, _, rh, cw = grad.shape
    rb = _pick(rh, (512, 256, 176, 128, 64, 32, 16, 8))

    def kern(sel_ref, g_ref, r_ref, b_ref, o_ref):
        own = g_ref[...] + r_ref[...]
        o_ref[...] = ((own + b_ref[0].astype(F32)) + b_ref[1].astype(F32)) + b_ref[2].astype(F32)

    return pl.pallas_call(
        kern, name=name,
        grid_spec=pltpu.PrefetchScalarGridSpec(
            num_scalar_prefetch=1, grid=(rh // rb,),
            in_specs=[pl.BlockSpec((None, None, rb, cw), lambda i, s: (s[1], s[0], i, 0)),
                      pl.BlockSpec((None, rb, cw), lambda i, s: (s[1], i, 0)),
                      pl.BlockSpec((3, rb, cw), lambda i, s: (0, i, 0))],
            out_specs=pl.BlockSpec((None, rb, cw), lambda i, s: (s[0], i, 0))),
        out_shape=jax.ShapeDtypeStruct((2, rh, cw), F32),
        compiler_params=_params(("parallel",)),
    )(sel, grad, recv, got)


def _stacked(g):
    return g.reshape(1, g.shape[0] * g.shape[1], g.shape[2])


def _halves(g):
    g = g.reshape(N_CHIPS, g.shape[0] * g.shape[1] // N_CHIPS, g.shape[2])
    return g.reshape(N_CHIPS, 2, g.shape[1] // 2, g.shape[2])


def _whole(f):
    return f.reshape(f.shape[0] * f.shape[1], f.shape[2])


def _step(x3, tgt, sm, w, mom, var):
    x, y, c = _place()
    sel = jnp.stack([c, 2 * x + y] + [_chip_rel(x, y, r)[2] for r in (1, 2, 3)]).astype(jnp.int32)
    wg = {}
    x2 = x3[0]
    cast, landed = _cast_bf16(
        [(x3, 0), (w["hg_w_out"], 0), (w["sg_w_in"], 0), (w["sg_w_out"], 0), (w["ffn_w_up"], 0), (w["ffn_w_up"], 1),
         (w["ffn_w_down"], 0), (w["ffn_w_down"], 1)], "cast_shards",
        comm=_plan_gather_ici([w["hg_w_in"][0].astype(BF16)]))
    xb = cast[0]
    sh = dict(zip(("hg_out", "sg_in", "sg_out", "up0", "up1", "dn0", "dn1"), cast[1:]))
    wg["hg_in"] = _comm_call(_plan_gather_pass(landed), "gather_hg_in_pass")[0]
    proj, landed = _matmul(xb, wg["hg_in"], mode="nn", nsh=N_CHIPS, name="hg_in",
                           comm=_plan_gather_ici([sh["hg_out"], sh["sg_in"]]))
    (yhg, o_raw, states), got = _hgrn2_fwd(
        proj, sm["lb_logits"], sm["hg_norm_g"], "hgrn2_fwd",
        comm=_merge(_plan_gather_pass(landed), _plan_gather_ici([sh["up0"], sh["up1"]])))
    wg["hg_out"], wg["sg_in"], landed = got[0], got[1], got[2:]
    xin1, got = _matmul(yhg, _stacked(wg["hg_out"]), mode="nn", nsh=1, resid=x2, alpha=ALPHA, name="hg_out",
                        comm=_plan_gather_pass(landed))
    wg["up0"], wg["up1"] = got
    h1, h1b = _ln_fwd(xin1, sm["ln1_g"][0:1], sm["ln1_b"][0:1], "l0_ln1")
    u0, landed = _matmul(h1b, wg["up0"], mode="nn", nsh=N_CHIPS, name="l0_ffn_up",
                         comm=_plan_gather_ici([sh["dn0"], sh["sg_out"]]))
    gact0, got = _conv_gate_fwd(u0, sm["conv_w"][0], sm["conv_b"][0:1], "l0_ffn_gate", comm=_plan_gather_pass(landed))
    wg["dn0"], wg["sg_out"] = got
    xin2, landed = _matmul(gact0, _stacked(wg["dn0"]), mode="nn", nsh=1, resid=h1, alpha=ALPHA, name="l0_ffn_down",
                           comm=_plan_gather_ici([sh["dn1"]]))
    h2, h2b = _ln_fwd(xin2, sm["ln2_g"][0:1], sm["ln2_b"][0:1], "l0_ffn_ln")
    pre, got = _matmul(h2b, wg["sg_in"], mode="nn", nsh=N_CHIPS, name="sg_in", comm=_plan_gather_pass(landed))
    wg["dn1"] = got[0]
    ysg = _sg_gate_fwd(pre, sm["sg_ln_g"], sm["sg_ln_b"], sm["sg_w_s"], sm["sg_b_s_t"], "sg_gate")
    xin3 = _matmul(ysg, _stacked(wg["sg_out"]), mode="nn", nsh=1, resid=h2, alpha=ALPHA, name="sg_out")
    h3, h3b = _ln_fwd(xin3, sm["ln1_g"][1:2], sm["ln1_b"][1:2], "l1_ln1")
    u1 = _matmul(h3b, wg["up1"], mode="nn", nsh=N_CHIPS, name="l1_ffn_up")
    gact1 = _conv_gate_fwd(u1, sm["conv_w"][1], sm["conv_b"][1:2], "l1_ffn_gate")
    xin4 = _matmul(gact1, _stacked(wg["dn1"]), mode="nn", nsh=1, resid=h3, alpha=ALPHA, name="l1_ffn_down")

    gs, grad, split, recv, part = {}, {}, {}, {}, {}

    def swap_on(call, keys):
        for k in keys:
            split[k] = _halves(grad[k])
        out, got = call(_plan_swap([split[k] for k in keys]))
        for k, r in zip(keys, got):
            recv[k] = r
            part[k] = _add_half(split[k], r, sel, f"rs_addhalf_{k}")
        return out

    def ffn_bwd(u, gact, hb_in, dxin, dxin_b, w_up, w_down, layer, tag, up, down, waiting):
        dgact = _matmul(dxin_b, _stacked(w_down), mode="nt", nsh=1, name=f"{tag}_ddown")
        grad[down] = _matmul(gact, dxin_b, mode="tn", nsh=1, name=f"{tag}_wdown")
        da, db, dcw, dcb = _conv_gate_bwd(u, dgact, sm["conv_w"][layer], sm["conv_b"][layer:layer + 1], f"{tag}_dgate")
        grad[up] = swap_on(lambda plan: _matmul(hb_in, [da, db], mode="tn", nsh=N_CHIPS, name=f"{tag}_wup", comm=plan),
                           waiting + [down])
        dh = swap_on(lambda plan: _matmul([da, db], w_up, mode="nt", nsh=N_CHIPS, resid=dxin, alpha=ALPHA,
                                          name=f"{tag}_dup", comm=plan), [up])
        return dh, dcw, dcb

    dx, dxb, dg4, db4, loss = _ln_bwd(xin4, tgt, sm["ln2_g"][1:2], sm["ln2_b"][1:2], "l1_ln2_bwd", loss_head=True)
    dh3, dcw1, dcb1 = ffn_bwd(u1, gact1, h3b, dx, dxb, wg["up1"], wg["dn1"], 1, "l1_ffn", "up1", "dn1", [])
    dx, dxb, dg3, db3 = _ln_bwd(xin3, dh3, sm["ln1_g"][1:2], sm["ln1_b"][1:2], "l1_ln1_bwd")
    grad["sg_out"] = _matmul(ysg, dxb, mode="tn", nsh=1, name="sg_wout")
    dysg = swap_on(lambda plan: _matmul(dxb, _stacked(wg["sg_out"]), mode="nt", nsh=1, name="sg_dout", comm=plan),
                   ["sg_out"])
    dpre, gs["sg_w_s"], gs["sg_b_s_t"], gs["sg_ln_g"], gs["sg_ln_b"] = _sg_gate_bwd(
        pre, dysg, sm["sg_ln_g"], sm["sg_ln_b"], sm["sg_w_s"], sm["sg_b_s_t"], "sg_gate_bwd")
    grad["sg_in"] = _matmul(h2b, dpre, mode="tn", nsh=N_CHIPS, name="sg_win")
    dh2 = _matmul(dpre, wg["sg_in"], mode="nt", nsh=N_CHIPS, resid=dx, alpha=ALPHA, name="sg_din")
    dx, dxb, dg2, db2 = _ln_bwd(xin2, dh2, sm["ln2_g"][0:1], sm["ln2_b"][0:1], "l0_ln2_bwd")
    dh1, dcw0, dcb0 = ffn_bwd(u0, gact0, h1b, dx, dxb, wg["up0"], wg["dn0"], 0, "l0_ffn", "up0", "dn0", ["sg_in"])
    dx, dxb, dg1, db1 = _ln_bwd(xin1, dh1, sm["ln1_g"][0:1], sm["ln1_b"][0:1], "l0_ln1_bwd")
    grad["hg_out"] = _matmul(yhg, dxb, mode="tn", nsh=1, name="hg_wout")
    dyhg = swap_on(lambda plan: _matmul(dxb, _stacked(wg["hg_out"]), mode="nt", nsh=1, name="hg_dout", comm=plan),
                   ["hg_out"])
    early = ("dn1", "up1", "sg_out", "sg_in", "dn0", "up0", "hg_out")
    dparts, got = _hgrn2_bwd(proj, sm["lb_logits"], sm["hg_norm_g"], o_raw, states, dyhg, "hgrn2_bwd",
                             comm=_plan_scatter([part[k] for k in early]))
    gs["lb"], gs["hg_norm_g"] = dparts[4], dparts[5]
    gs["ln1_g"] = jnp.concatenate([dg1, dg3], axis=0)
    gs["ln1_b"] = jnp.concatenate([db1, db3], axis=0)
    gs["ln2_g"] = jnp.concatenate([dg2, dg4], axis=0)
    gs["ln2_b"] = jnp.concatenate([db2, db4], axis=0)
    gs["conv_w"] = jnp.stack([dcw0, dcw1], axis=0)
    gs["conv_b"] = jnp.concatenate([dcb0, dcb1], axis=0)
    packed, layout = _pack(gs)
    mine = [_add_own(split[k], recv[k], b, sel, f"rs_addown_{k}") for k, b in zip(early, got)]
    grad["hg_in"], got = _matmul(xb, list(dparts[:4]), mode="tn", nsh=N_CHIPS, name="hg_win",
                                 comm=_merge(_plan_join(mine), _plan_gather_all(packed)))
    red = {k: _whole(f) for k, f in zip(early, got)}
    me8 = jnp.reshape(4 * x + 2 * y + c, (1,)).astype(jnp.int32)
    summed = _unpack(_sum_devices(got[len(early)], packed, me8, "sum_small_grads"), layout)
    gx = swap_on(lambda plan: _matmul(dparts[0], wg["hg_in"], mode="nt", nsh=1, b_off=0, resid=dx, alpha=ALPHA,
                                      name="hg_din_q", comm=plan), ["hg_in"])
    gx, got = _matmul(list(dparts[1:4]), wg["hg_in"], mode="nt", nsh=3, b_off=1, resid=gx, alpha=1.0, name="hg_din_fig",
                      comm=_plan_scatter([part["hg_in"]]))
    mine = _add_own(split["hg_in"], recv["hg_in"], got[0], sel, "rs_addown_hg_in")
    upd = {}
    upd["hg_w_out"], full = _adamw(w["hg_w_out"], [red["hg_out"]], mom["hg_w_out"], var["hg_w_out"], "adamw_hg_w_out",
                                   comm=_plan_join([mine]))
    red["hg_in"] = _whole(full[0])
    for k, src in (("ffn_w_up", ("up0", "up1")), ("ffn_w_down", ("dn0", "dn1")), ("sg_w_in", ("sg_in",)),
                   ("sg_w_out", ("sg_out",)), ("hg_w_in", ("hg_in",))):
        upd[k] = _adamw(w[k], [red[s] for s in src], mom[k], var[k], f"adamw_{k}")
    return loss, gx, summed, upd


_SMALL_ORDER = ("lb", "hg_norm_g", "sg_w_s", "sg_b_s_t", "conv_b", "ln1_g", "ln1_b", "ln2_g", "ln2_b",
                "conv_w", "sg_ln_g", "sg_ln_b")


PACK_ROWS = 512


def _pack(parts):
    flat, layout, off = [], [], 0
    for k in _SMALL_ORDER:
        a = parts[k]
        n = a.size
        pad = (-n) % LANES
        flat.append(jnp.pad(a.reshape(-1), (0, pad)))
        layout.append((k, off, n, a.shape))
        off += n + pad
    flat.append(jnp.zeros(((-off) % (PACK_ROWS * LANES),), F32))
    return jnp.concatenate(flat).reshape(-1, LANES), layout


def _unpack(buf, layout):
    flat = buf.reshape(-1)
    return {k: flat[off:off + n].reshape(shape) for k, off, n, shape in layout}


def kernel(x, lb_logits, hg_w_in, hg_norm_g, hg_w_out, sg_w_in, sg_ln_g, sg_ln_b, sg_w_s, sg_b_s, sg_w_out, ffn_w_up, ffn_conv_w, ffn_conv_b, ffn_w_down, ln1_g, ln1_b, ln2_g, ln2_b, loss_target, m_lb_logits, m_hg_w_in, m_hg_norm_g, m_hg_w_out, m_sg_w_in, m_sg_ln_g, m_sg_ln_b, m_sg_w_s, m_sg_b_s, m_sg_w_out, m_ffn_w_up, m_ffn_conv_w, m_ffn_conv_b, m_ffn_w_down, m_ln1_g, m_ln1_b, m_ln2_g, m_ln2_b, v_lb_logits, v_hg_w_in, v_hg_norm_g, v_hg_w_out, v_sg_w_in, v_sg_ln_g, v_sg_ln_b, v_sg_w_s, v_sg_b_s, v_sg_w_out, v_ffn_w_up, v_ffn_conv_w, v_ffn_conv_b, v_ffn_w_down, v_ln1_g, v_ln1_b, v_ln2_g, v_ln2_b):
    names = ("lb_logits", "hg_w_in", "hg_norm_g", "hg_w_out", "sg_w_in", "sg_ln_g", "sg_ln_b", "sg_w_s", "sg_b_s",
             "sg_w_out", "ffn_w_up", "ffn_conv_w", "ffn_conv_b", "ffn_w_down", "ln1_g", "ln1_b", "ln2_g", "ln2_b")
    w = dict(zip(names, (lb_logits, hg_w_in, hg_norm_g, hg_w_out, sg_w_in, sg_ln_g, sg_ln_b, sg_w_s, sg_b_s,
                         sg_w_out, ffn_w_up, ffn_conv_w, ffn_conv_b, ffn_w_down, ln1_g, ln1_b, ln2_g, ln2_b)))
    mom = dict(zip(names, (m_lb_logits, m_hg_w_in, m_hg_norm_g, m_hg_w_out, m_sg_w_in, m_sg_ln_g, m_sg_ln_b, m_sg_w_s,
                           m_sg_b_s, m_sg_w_out, m_ffn_w_up, m_ffn_conv_w, m_ffn_conv_b, m_ffn_w_down, m_ln1_g,
                           m_ln1_b, m_ln2_g, m_ln2_b)))
    var = dict(zip(names, (v_lb_logits, v_hg_w_in, v_hg_norm_g, v_hg_w_out, v_sg_w_in, v_sg_ln_g, v_sg_ln_b, v_sg_w_s,
                           v_sg_b_s, v_sg_w_out, v_ffn_w_up, v_ffn_conv_w, v_ffn_conv_b, v_ffn_w_down, v_ln1_g,
                           v_ln1_b, v_ln2_g, v_ln2_b)))
    tgt = loss_target[0]
    fq = ffn_conv_w.shape[2]
    dq = sg_ln_g.shape[1]
    cx, cy, _ = _place()
    me = 2 * cx + cy

    wide = max(fq, dq)
    tiny = jnp.concatenate([jnp.pad(ffn_conv_w.reshape(6, fq), ((0, 0), (0, wide - fq))),
                            jnp.pad(sg_ln_g, ((0, 0), (0, wide - dq))),
                            jnp.pad(sg_ln_b, ((0, 0), (0, wide - dq)))], axis=0)
    tiny_all = _allgather_whole(tiny, "gather_small")
    conv_w_full = jnp.transpose(tiny_all[:, 0:6, :fq].reshape(N_CHIPS, 2, 3, fq), (1, 2, 0, 3)).reshape(2, 3, N_CHIPS * fq)
    sm = {"lb_logits": lb_logits, "hg_norm_g": hg_norm_g, "ln1_g": ln1_g, "ln1_b": ln1_b, "ln2_g": ln2_g,
          "ln2_b": ln2_b, "conv_w": conv_w_full, "conv_b": ffn_conv_b,
          "sg_ln_g": tiny_all[:, 6, :dq].reshape(1, N_CHIPS * dq),
          "sg_ln_b": tiny_all[:, 7, :dq].reshape(1, N_CHIPS * dq),
          "sg_w_s": sg_w_s[0], "sg_b_s_t": jnp.transpose(sg_b_s[0])}

    loss_row, grad_x, summed, upd = _step(x, tgt, sm, w, mom, var)
    loss = lax.psum(loss_row[0, 0], ("x", "y", "c"))

    grads = {
        "lb_logits": _lb_logits_grad(lb_logits, summed["lb"], "lb_logits_grad"),
        "hg_norm_g": summed["hg_norm_g"],
        "sg_ln_g": lax.dynamic_slice_in_dim(summed["sg_ln_g"], me * dq, dq, axis=1),
        "sg_ln_b": lax.dynamic_slice_in_dim(summed["sg_ln_b"], me * dq, dq, axis=1),
        "sg_w_s": summed["sg_w_s"][None], "sg_b_s": jnp.transpose(summed["sg_b_s_t"])[None],
        "ffn_conv_w": lax.dynamic_slice_in_dim(summed["conv_w"], me * fq, fq, axis=2),
        "ffn_conv_b": summed["conv_b"],
        "ln1_g": summed["ln1_g"], "ln1_b": summed["ln1_b"], "ln2_g": summed["ln2_g"], "ln2_b": summed["ln2_b"],
    }

    delta, new_m, new_v = {}, {}, {}
    for k, (dlt, mm, vv, gg) in upd.items():
        delta[k], new_m[k], new_v[k], grads[k] = dlt, mm, vv, gg
    small_names = [k for k in names if k not in upd]

    def pack_small(src):
        flat = [src[k].reshape(-1) for k in small_names]
        n = sum(a.size for a in flat)
        flat.append(jnp.zeros(((-n) % (PACK_ROWS * LANES),), F32))
        return jnp.concatenate(flat).reshape(1, -1, LANES)

    outs = _adamw(pack_small(w), [pack_small(grads)[0]], pack_small(mom), pack_small(var), "adamw_small")
    off = 0
    for k in small_names:
        n = w[k].size
        for dst, o in zip((delta, new_m, new_v), outs):
            dst[k] = o.reshape(-1)[off:off + n].reshape(w[k].shape)
        off += n

    return (loss, grad_x[None], *[grads[k] for k in names], *[delta[k] for k in names],
            *[new_m[k] for k in names], *[new_v[k] for k in names])
```

```python
import functools

import jax
import jax.numpy as jnp
from jax import lax
from jax.experimental import pallas as pl
from jax.experimental.pallas import tpu as pltpu

F32 = jnp.float32
BF16 = jnp.bfloat16
MESH = pl.DeviceIdType.MESH

ALPHA = (2 * 2) ** 0.25
LN_EPS = 1e-5
RMS_EPS = 1e-6
ADAM_LR, ADAM_B1, ADAM_B2, ADAM_EPS, ADAM_WD, ADAM_STEP = 0.001, 0.9, 0.999, 1e-08, 0.01, 10

LANES = 128
SUB = 16
TILE = 8
GCHUNK = 128
VMEM_LIMIT = 56 * 1024 * 1024
N_CHIPS = 4
N_DEV = 8

NT = (((1,), (1,)), ((), ()))
TN = (((0,), (0,)), ((), ()))
NN = (((1,), (0,)), ((), ()))


def _pick(dim, prefs):
    for p in prefs:
        if dim % p == 0:
            return p
    return dim


def _params(sem=None, **kw):
    return pltpu.CompilerParams(dimension_semantics=sem, vmem_limit_bytes=VMEM_LIMIT, **kw)


def _sigmoid_pair(x):
    e = jnp.exp(-jnp.abs(x))
    inv = 1.0 / (1.0 + e)
    pos = x >= 0
    return jnp.where(pos, inv, e * inv), jnp.where(pos, e * inv, inv)


def _sigmoid_gate(x):
    t = 0.5 * jnp.tanh(0.5 * x)
    return 0.5 + t, 0.5 - t


def _ln_hat(x):
    mu = jnp.mean(x, axis=-1, keepdims=True)
    xc = x - mu
    var = jnp.mean(xc * xc, axis=-1, keepdims=True)
    rstd = lax.rsqrt(var + LN_EPS)
    return xc * rstd, rstd


def _lower_bound(logits):
    m = jnp.max(logits, axis=0, keepdims=True)
    e = jnp.exp(logits - m)
    return e[0:1, :] / jnp.sum(e, axis=0, keepdims=True)


MATMUL_VMEM_BUDGET = 40 * 1024 * 1024
NN_VMEM_BUDGET = 46 * 1024 * 1024


def _fit_bk(kdim, bm, bn, out_dtype, has_resid, na=1, nb=1, budget=MATMUL_VMEM_BUDGET):
    fixed = bm * bn * (4 + 2 * jnp.dtype(out_dtype).itemsize + (8 if has_resid else 0))
    best = LANES
    for bk in range(LANES, kdim + 1, LANES):
        if kdim % bk == 0 and fixed + 4 * bk * (bm * na + bn * nb) <= budget:
            best = bk
    return best


def _fit_bm_bk(mdim, prefs, kdim, bn, out_dtype, has_resid, na=1, nb=1, budget=MATMUL_VMEM_BUDGET):
    best = None
    fits = [bm for bm in prefs if mdim % bm == 0][:2] or [mdim]
    for bm in fits:
        bk = _fit_bk(kdim, bm, bn, out_dtype, has_resid, na, nb, budget)
        if best is None or kdim // bk < kdim // best[1]:
            best = (bm, bk)
    return best


def _matmul(a, b, *, mode, name, out_dtype=F32, resid=None, alpha=1.0, b_off=0, nsh=None, comm=None):
    a_parts = a if isinstance(a, (list, tuple)) else [a]
    b_parts = b if isinstance(b, (list, tuple)) else [b]
    n_parts = max(len(a_parts), len(b_parts))
    if mode == "nn":
        m, kdim = a.shape
        _, _, ns = b.shape
        bn = _pick(ns, (1024, 1408, 512, 256, 128))
        bm, bk = _fit_bm_bk(m, (1024, 512, 256, 128), kdim, bn, out_dtype, resid is not None, budget=NN_VMEM_BUDGET)
        nps = ns // bn
        grid = (m // bm, nsh * nps, kdim // bk)
        a_specs = [pl.BlockSpec((bm, bk), lambda i, j, k: (i, k))]
        b_specs = [pl.BlockSpec((None, bk, bn), lambda i, j, k: (b_off + j // nps, k, j % nps))]
        o_spec = pl.BlockSpec((bm, bn), lambda i, j, k: (i, j))
        out_shape = jax.ShapeDtypeStruct((m, nsh * ns), out_dtype)
        dims, part_axis, per_part = NN, 2, grid[2]
    elif mode == "nt":
        m = a_parts[0].shape[0]
        _, kdim, ns = b.shape
        wide = n_parts > 1
        bn = _pick(kdim, (1024, 1408, 512, 256, 128) if wide else (512, 256, 128))
        bm, bk = _fit_bm_bk(m, (1024, 512, 256, 128) if wide else (2048, 1024, 512, 256, 128), ns, bn, out_dtype,
                            resid is not None, na=n_parts)
        kps = ns // bk
        per_part = nsh // n_parts * kps
        grid = (m // bm, kdim // bn, nsh * kps)
        a_specs = [pl.BlockSpec((bm, bk), lambda i, j, k, p=p: (jnp.where(k // per_part == p, i, 0),
                                                                  jnp.where(k // per_part == p, k % per_part, 0)))
                   for p in range(n_parts)]
        b_specs = [pl.BlockSpec((None, bn, bk), lambda i, j, k: (b_off + k // kps, j, k % kps))]
        o_spec = pl.BlockSpec((bm, bn), lambda i, j, k: (i, j))
        out_shape = jax.ShapeDtypeStruct((m, kdim), out_dtype)
        dims, part_axis = NT, 2
    else:
        t, kdim = a.shape
        ns = b_parts[0].shape[1] * n_parts // nsh
        bn = _pick(ns, (1024, 1408, 512, 256, 128))
        bm, bk = _fit_bm_bk(kdim, (1024, 1408, 512, 256, 128), t, bn, out_dtype, resid is not None, nb=n_parts)
        nps = ns // bn
        per_part = nsh // n_parts * nps
        grid = (kdim // bm, nsh * nps, t // bk)
        a_specs = [pl.BlockSpec((bk, bm), lambda i, j, k: (k, i))]
        b_specs = [pl.BlockSpec((bk, bn), lambda i, j, k, p=p: (jnp.where(j // per_part == p, k, 0),
                                                                  jnp.where(j // per_part == p, j % per_part, 0)))
                   for p in range(n_parts)]
        o_spec = pl.BlockSpec((None, bm, bn), lambda i, j, k: (j // nps, i, j % nps))
        out_shape = jax.ShapeDtypeStruct((nsh, kdim, ns), out_dtype)
        dims, part_axis = TN, 1
    nk = grid[2]
    na, nb_ = len(a_parts), len(b_parts)
    has_resid = resid is not None

    def kern(*refs):
        a_refs, b_refs = refs[:na], refs[na:na + nb_]
        r_ref = refs[na + nb_] if has_resid else None
        k = pl.program_id(2)

        def finish(r, o_ref):
            if has_resid:
                r = r + alpha * r_ref[...]
            o_ref[...] = r.astype(o_ref.dtype)

        def add(a_ref, b_ref):
            if nk == 1:
                finish(lax.dot_general(a_ref[...], b_ref[...], dims, preferred_element_type=F32), refs[-1])
                return
            refs[-1][...] += lax.dot_general(a_ref[...], b_ref[...], dims, preferred_element_type=F32)

        if nk > 1:
            @pl.when(k == 0)
            def _():
                refs[-1][...] = jnp.zeros_like(refs[-1])

        if n_parts == 1:
            add(a_refs[0], b_refs[0])
        else:
            which = pl.program_id(part_axis) // per_part
            for p in range(n_parts):
                pl.when(which == p)(functools.partial(add, a_refs[min(p, na - 1)], b_refs[min(p, nb_ - 1)]))
        if nk > 1:
            @pl.when(k == nk - 1)
            def _():
                finish(refs[-1][...], refs[-2])

    in_specs = a_specs + b_specs
    operands = list(a_parts) + list(b_parts)
    if has_resid:
        in_specs.append(pl.BlockSpec((bm, bn), lambda i, j, k: (i, j)))
        operands.append(resid)
    outs, carried = _carried_call(
        kern, comm, name=name, grid=grid, in_specs=in_specs, out_specs=[o_spec], out_shape=[out_shape],
        operands=operands, scratch_shapes=[pltpu.VMEM((bm, bn), F32)] if nk > 1 else [],
        sem=("parallel", "parallel", "arbitrary"))
    return outs[0] if comm is None else (outs[0], carried)


def _ln_fwd(xin, g, b, name):
    t, d = xin.shape
    tb = _pick(t, (256, 128, 64, 32, 16))

    def kern(x_ref, g_ref, b_ref, h_ref, hb_ref):
        xhat, _ = _ln_hat(x_ref[...])
        h = xhat * g_ref[...] + b_ref[...]
        h_ref[...] = h
        hb_ref[...] = h.astype(BF16)

    row = pl.BlockSpec((tb, d), lambda i: (i, 0))
    vec = pl.BlockSpec((1, d), lambda i: (0, 0))
    return pl.pallas_call(
        kern, name=name, grid=(t // tb,), in_specs=[row, vec, vec], out_specs=[row, row],
        out_shape=[jax.ShapeDtypeStruct((t, d), F32), jax.ShapeDtypeStruct((t, d), BF16)],
        compiler_params=_params(("parallel",)),
    )(xin, g, b)


def _ln_bwd(xin, dy_or_target, g, b, name, loss_head=False):
    t, d = xin.shape
    tb = _pick(t, (256, 128, 64, 32, 16))
    nb = t // tb

    def kern(x_ref, dy_ref, g_ref, b_ref, dx_ref, dxb_ref, dg_ref, db_ref, *rest):
        i = pl.program_id(0)
        xhat, rstd = _ln_hat(x_ref[...])
        gv = g_ref[...]
        if loss_head:
            loss_ref = rest[0]
            err = xhat * gv + b_ref[...] - dy_ref[...]
            dy = err * (1.0 / d)
            part = 0.5 * jnp.sum(jnp.sum(err * err, axis=1, keepdims=True), axis=0, keepdims=True) * (1.0 / d)
        else:
            dy = dy_ref[...]

        @pl.when(i == 0)
        def _():
            dg_ref[...] = jnp.zeros_like(dg_ref)
            db_ref[...] = jnp.zeros_like(db_ref)
            if loss_head:
                loss_ref[...] = jnp.zeros_like(loss_ref)

        dg_ref[...] += jnp.sum(dy * xhat, axis=0, keepdims=True)
        db_ref[...] += jnp.sum(dy, axis=0, keepdims=True)
        if loss_head:
            loss_ref[...] += jnp.broadcast_to(part, loss_ref.shape)
        dxh = dy * gv
        m1 = jnp.mean(dxh, axis=-1, keepdims=True)
        m2 = jnp.mean(dxh * xhat, axis=-1, keepdims=True)
        dx = rstd * (dxh - m1 - xhat * m2)
        dx_ref[...] = dx
        dxb_ref[...] = dx.astype(BF16)

    row = pl.BlockSpec((tb, d), lambda i: (i, 0))
    vec = pl.BlockSpec((1, d), lambda i: (0, 0))
    out_specs = [row, row, vec, vec]
    out_shape = [jax.ShapeDtypeStruct((t, d), F32), jax.ShapeDtypeStruct((t, d), BF16),
                 jax.ShapeDtypeStruct((1, d), F32), jax.ShapeDtypeStruct((1, d), F32)]
    if loss_head:
        out_specs.append(pl.BlockSpec((1, LANES), lambda i: (0, 0)))
        out_shape.append(jax.ShapeDtypeStruct((1, LANES), F32))
    return pl.pallas_call(
        kern, name=name, grid=(nb,), in_specs=[row, row, vec, vec], out_specs=out_specs, out_shape=out_shape,
        compiler_params=_params(("arbitrary",)),
    )(xin, dy_or_target, g, b)


def _conv_gate_fwd(u, conv_w, conv_b, name, comm=None):
    t, f2 = u.shape
    f = f2 // 2
    tb = _pick(t, (512, 256, 128, 64, 32, 16))
    cn = _pick(f, (1408, 1024, 512, 256, 128))
    ncb = f // cn
    hb = tb // 8

    def kern(a_ref, ah_ref, b_ref, w_ref, cb_ref, o_ref):
        i = pl.program_id(0)
        a = a_ref[...]
        halo = jnp.where(i > 0, ah_ref[...], 0.0)
        rid = lax.broadcasted_iota(jnp.int32, a.shape, 0)
        s1 = jnp.where(rid == 0, halo[7:8, :], pltpu.roll(a, 1, 0))
        s2 = jnp.where(rid == 0, halo[6:7, :], jnp.where(rid == 1, halo[7:8, :], pltpu.roll(a, 2, 0)))
        w = w_ref[...]
        conv = w[2:3, :] * a + w[1:2, :] * s1 + w[0:1, :] * s2 + cb_ref[...]
        sp, _ = _sigmoid_gate(conv)
        o_ref[...] = (conv * sp * b_ref[...]).astype(BF16)

    outs, carried = _carried_call(
        kern, comm, name=name, grid=(t // tb, ncb),
        in_specs=[pl.BlockSpec((tb, cn), lambda i, j: (i, j)),
                  pl.BlockSpec((8, cn), lambda i, j: (jnp.maximum(i * hb - 1, 0), j)),
                  pl.BlockSpec((tb, cn), lambda i, j: (i, j + ncb)),
                  pl.BlockSpec((3, cn), lambda i, j: (0, j)),
                  pl.BlockSpec((1, cn), lambda i, j: (0, j))],
        out_specs=[pl.BlockSpec((tb, cn), lambda i, j: (i, j))],
        out_shape=[jax.ShapeDtypeStruct((t, f), BF16)],
        operands=[u, u, u, conv_w, conv_b], sem=("parallel", "parallel"))
    return outs[0] if comm is None else (outs[0], carried)


def _conv_gate_bwd(u, dgact, conv_w, conv_b, name):
    t, f2 = u.shape
    f = f2 // 2
    tb = _pick(t, (512, 256, 128, 64, 32, 16))
    cn = _pick(f, (1408, 1024, 512, 256, 128))
    ncb = f // cn
    hb = tb // 8
    nb = t // tb
    last8 = t // 8 - 1

    def kern(a_ref, ap_ref, an_ref, b_ref, bn_ref, dg_ref, dgn_ref, w_ref, cb_ref,
             da_ref, db_ref, dw_ref, dcb_ref):
        i = pl.program_id(1)
        a = a_ref[...]
        w = w_ref[...]
        ext = jnp.concatenate([jnp.where(i > 0, ap_ref[...], 0.0), a, an_ref[...]], axis=0)
        e1 = pltpu.roll(ext, 1, 0)
        e2 = pltpu.roll(ext, 2, 0)
        conv = (w[2:3, :] * ext + w[1:2, :] * e1 + w[0:1, :] * e2 + cb_ref[...])[8:, :]
        bmn = jnp.concatenate([b_ref[...], bn_ref[...]], axis=0)
        dgmn = jnp.concatenate([dg_ref[...], jnp.where(i < nb - 1, dgn_ref[...], 0.0)], axis=0)
        sp, sn = _sigmoid_gate(conv)
        da = dgmn * bmn * (sp * (1.0 + conv * sn))
        n = tb + 8
        dap = w[2:3, :] * da + w[1:2, :] * pltpu.roll(da, n - 1, 0) + w[0:1, :] * pltpu.roll(da, n - 2, 0)
        da_ref[...] = dap[:tb, :].astype(BF16)
        db_ref[...] = (dg_ref[...] * (conv * sp)[:tb, :]).astype(BF16)
        dam = da[:tb, :]

        @pl.when(i == 0)
        def _():
            dw_ref[...] = jnp.zeros_like(dw_ref)
            dcb_ref[...] = jnp.zeros_like(dcb_ref)

        dw = jnp.concatenate([jnp.sum(dam * e2[8:8 + tb, :], axis=0, keepdims=True),
                              jnp.sum(dam * e1[8:8 + tb, :], axis=0, keepdims=True),
                              jnp.sum(dam * a, axis=0, keepdims=True)], axis=0)
        dw_ref[...] += dw
        dcb_ref[...] += jnp.sum(dam, axis=0, keepdims=True)

    main_a = pl.BlockSpec((tb, cn), lambda j, i: (i, j))
    prev_a = pl.BlockSpec((8, cn), lambda j, i: (jnp.maximum(i * hb - 1, 0), j))
    next_a = pl.BlockSpec((8, cn), lambda j, i: (jnp.minimum((i + 1) * hb, last8), j))
    main_b = pl.BlockSpec((tb, cn), lambda j, i: (i, j + ncb))
    next_b = pl.BlockSpec((8, cn), lambda j, i: (jnp.minimum((i + 1) * hb, last8), j + ncb))
    return pl.pallas_call(
        kern, name=name, grid=(ncb, nb),
        in_specs=[main_a, prev_a, next_a, main_b, next_b, main_a, next_a,
                  pl.BlockSpec((3, cn), lambda j, i: (0, j)), pl.BlockSpec((1, cn), lambda j, i: (0, j))],
        out_specs=[main_a, main_a, pl.BlockSpec((3, cn), lambda j, i: (0, j)),
                   pl.BlockSpec((1, cn), lambda j, i: (0, j))],
        out_shape=[jax.ShapeDtypeStruct((t, f), BF16), jax.ShapeDtypeStruct((t, f), BF16),
                   jax.ShapeDtypeStruct((3, f), F32), jax.ShapeDtypeStruct((1, f), F32)],
        compiler_params=_params(("parallel", "arbitrary")),
    )(u, u, u, u, u, dgact, dgact, conv_w, conv_b)


def _hg_gates(qp, fp, lb):
    sq, _ = _sigmoid_gate(qp)
    sf, snf = _sigmoid_pair(fp)
    forget = lb + (1.0 - lb) * sf
    return sq, sf, snf, forget, jnp.log(forget), (1.0 - lb) * snf


def _tri(lower):
    r = lax.broadcasted_iota(jnp.int32, (SUB, SUB), 0)
    c = lax.broadcasted_iota(jnp.int32, (SUB, SUB), 1)
    return ((r >= c) if lower else (r <= c)).astype(BF16)


def _split2(x):
    hi = x.astype(BF16)
    return hi, (x - hi.astype(F32)).astype(BF16)


def _dot3(a, b, dims):
    (ah, al), (bh, bl) = a, b
    return (lax.dot_general(ah, bh, dims, preferred_element_type=F32)
            + (lax.dot_general(ah, bl, dims, preferred_element_type=F32)
               + lax.dot_general(al, bh, dims, preferred_element_type=F32)))


def _running_sum(tri, x):
    hi, lo = _split2(x)
    rest = (x - hi.astype(F32)) - lo.astype(F32)
    return (lax.dot_general(tri, hi, NN, preferred_element_type=F32)
            + (lax.dot_general(tri, lo, NN, preferred_element_type=F32)
               + lax.dot_general(tri, rest.astype(BF16), NN, preferred_element_type=F32)))


HEADS_PER_STEP = 8
STEP_UNROLL = 2
FWD_STEP_UNROLL = 4


def _hgrn2_fwd(proj, lb_logits, norm_g, name, comm=None):
    t, d4 = proj.shape
    d = d4 // 4
    nh = d // LANES
    hb = _pick(nh, (HEADS_PER_STEP, 2, 1))
    wb = hb * LANES
    tb = _pick(t, (256, 128, 64, 32, 16))
    nb = t // tb
    nsc = tb // SUB

    def kern(q_ref, f_ref, i_ref, g_ref, lbl_ref, ng_ref, y_ref, o_ref, st_ref, s_ref):
        @pl.when(pl.program_id(1) == 0)
        def _():
            s_ref[...] = jnp.zeros_like(s_ref)

        lb_all = _lower_bound(lbl_ref[...])
        ng_all = ng_ref[...]
        ltri = _tri(True)
        rcol = lax.broadcasted_iota(jnp.int32, (SUB, 1), 0)

        heads = [slice(h * LANES, (h + 1) * LANES) for h in range(hb)]

        def step(sc, carry):
            rows = pl.ds(pl.multiple_of(sc * SUB, SUB), SUB)
            qp, fp, v, gp = q_ref[rows, :], f_ref[rows, :], i_ref[rows, :], g_ref[rows, :]
            sq, _, _, _, lf, k = _hg_gates(qp, fp, lb_all)
            q = qp * sq
            bl = _running_sum(ltri, lf)
            bend = bl[SUB - 1:SUB, :]
            dec = jnp.exp(bend)
            qs2 = _split2(q * jnp.exp(bl))
            kd2 = _split2(k * jnp.exp(bend - bl))
            v2 = _split2(v)
            states = [s_ref[h] for h in range(hb)]
            o = [_dot3((qs2[0][:, c], qs2[1][:, c]), _split2(states[h]), NT) for h, c in enumerate(heads)]
            top, bot = [oh[:TILE] for oh in o], [oh[TILE:] for oh in o]
            for s in range(SUB):
                lo = 0 if s < TILE else TILE
                e = jnp.exp(jnp.minimum(bl[lo:] - bl[s:s + 1, :], 0.0))
                p = q[lo:] * e * k[s:s + 1, :]
                for h, c in enumerate(heads):
                    a = jnp.sum(p[:, c], axis=1, keepdims=True)
                    add = jnp.where(rcol[lo:] >= s, a, 0.0) * v[s:s + 1, c]
                    if lo == 0:
                        top[h], bot[h] = top[h] + add[:TILE], bot[h] + add[TILE:]
                    else:
                        bot[h] = bot[h] + add
            o = [jnp.concatenate([a, b], axis=0) for a, b in zip(top, bot)]
            for h, c in enumerate(heads):
                st_ref[sc, h] = states[h]
                s_ref[h] = states[h] * dec[:, c] + _dot3((v2[0][:, c], v2[1][:, c]), (kd2[0][:, c], kd2[1][:, c]), TN)
            o_ref[rows, :] = jnp.concatenate(o, axis=1)
            on = jnp.concatenate(
                [oh * lax.rsqrt(jnp.mean(oh * oh, axis=1, keepdims=True) + RMS_EPS) for oh in o], axis=1)
            sg, _ = _sigmoid_gate(gp)
            y_ref[rows, :] = (on * ng_all * (gp * sg)).astype(BF16)
            return carry

        lax.fori_loop(0, nsc, step, 0, unroll=FWD_STEP_UNROLL)

    def col(off):
        return pl.BlockSpec((tb, wb), lambda h, j: (j, h + off * (nh // hb)))

    return _carried_call(
        kern, comm, name=name, grid=(nh // hb, nb),
        in_specs=[col(0), col(1), col(2), col(3),
                  pl.BlockSpec((3, wb), lambda h, j: (0, h)), pl.BlockSpec((1, wb), lambda h, j: (0, h))],
        out_specs=[col(0), col(0), pl.BlockSpec((nsc, hb, LANES, LANES), lambda h, j: (j, h, 0, 0))],
        out_shape=[jax.ShapeDtypeStruct((t, d), BF16), jax.ShapeDtypeStruct((t, d), F32),
                   jax.ShapeDtypeStruct((t // SUB, nh, LANES, LANES), F32)],
        scratch_shapes=[pltpu.VMEM((hb, LANES, LANES), F32)],
        operands=[proj, proj, proj, proj, lb_logits, norm_g], sem=("parallel", "arbitrary"))


def _hgrn2_bwd(proj, lb_logits, norm_g, o_raw, states, dy, name, comm=None):
    t, d4 = proj.shape
    d = d4 // 4
    nh = d // LANES
    hb = _pick(nh, (HEADS_PER_STEP, 2, 1))
    wb = hb * LANES
    tb = _pick(t, (256, 128, 64, 32, 16))
    nb = t // tb
    nsc = tb // SUB

    def kern(q_ref, f_ref, i_ref, g_ref, lbl_ref, ng_ref, o_ref, st_ref, dy_ref,
             dq_ref, df_ref, di_ref, dgp_ref, dlb_ref, dng_ref, ds_ref, gc_ref):
        j = pl.program_id(1)

        @pl.when(j == 0)
        def _():
            ds_ref[...] = jnp.zeros_like(ds_ref)
            gc_ref[...] = jnp.zeros_like(gc_ref)
            dlb_ref[...] = jnp.zeros_like(dlb_ref)
            dng_ref[...] = jnp.zeros_like(dng_ref)

        lb_all = _lower_bound(lbl_ref[...])
        ng_all = ng_ref[...]
        ltri, utri = _tri(True), _tri(False)
        rcol = lax.broadcasted_iota(jnp.int32, (SUB, 1), 0)
        rid = lax.broadcasted_iota(jnp.int32, (SUB, wb), 0)

        heads = [slice(h * LANES, (h + 1) * LANES) for h in range(hb)]

        def per_head(fn, n=SUB):
            return jnp.concatenate([jnp.broadcast_to(fn(c), (n, LANES)) for c in heads], axis=1)

        def step(it, carry):
            sc = nsc - 1 - it
            rows = pl.ds(pl.multiple_of(sc * SUB, SUB), SUB)
            qp, fp, v, gp = q_ref[rows, :], f_ref[rows, :], i_ref[rows, :], g_ref[rows, :]
            o, dyv = o_ref[rows, :], dy_ref[rows, :]
            sq, sf, snf, forget, lf, k = _hg_gates(qp, fp, lb_all)
            q = qp * sq
            bl = _running_sum(ltri, lf)
            ebl = jnp.exp(bl)
            bend = bl[SUB - 1:SUB, :]
            dec = jnp.exp(bend)
            dte = jnp.exp(bend - bl)
            r = per_head(lambda c: lax.rsqrt(jnp.mean(o[:, c] * o[:, c], axis=1, keepdims=True) + RMS_EPS))
            ohat = o * r
            sg, sng = _sigmoid_gate(gp)
            don = dyv * (gp * sg)
            dgp_ref[rows, :] = (dyv * (ohat * ng_all) * (sg * (1.0 + gp * sng))).astype(BF16)
            dng_ref[...] += jnp.sum(don * ohat, axis=0, keepdims=True)
            doh = don * ng_all
            dot_oh = doh * ohat
            do = r * (doh - ohat * per_head(lambda c: jnp.mean(dot_oh[:, c], axis=1, keepdims=True)))
            do2, qs2, kd2, v2 = _split2(do), _split2(q * ebl), _split2(k * dte), _split2(v)
            dq_h, dk_h, dv_h = [], [], []
            for h, c in enumerate(heads):
                dstate = ds_ref[h]
                ds2 = _split2(dstate)
                doc = (do2[0][:, c], do2[1][:, c])
                dq_h.append(_dot3(doc, _split2(st_ref[sc, h]), NN))
                dv_h.append(_dot3((kd2[0][:, c], kd2[1][:, c]), ds2, NT))
                dk_h.append(_dot3((v2[0][:, c], v2[1][:, c]), ds2, NN))
                ds_ref[h] = dstate * dec[:, c] + _dot3(doc, (qs2[0][:, c], qs2[1][:, c]), TN)
            dq = jnp.concatenate(dq_h, axis=1) * ebl
            dk = jnp.concatenate(dk_h, axis=1) * dte
            dv = jnp.concatenate(dv_h, axis=1)
            dq_t, dq_b = dq[:TILE], dq[TILE:]
            dk_i = [jnp.zeros((TILE, wb), F32), jnp.zeros((TILE, wb), F32)]
            dv_i = [jnp.zeros((TILE, wb), F32), jnp.zeros((TILE, wb), F32)]
            for s in range(SUB):
                lo = 0 if s < TILE else TILE
                n = SUB - lo
                e = jnp.exp(jnp.minimum(bl[lo:] - bl[s:s + 1, :], 0.0))
                qe = q[lo:] * e
                ks = k[s:s + 1, :]
                live = rcol[lo:] >= s
                pk = qe * ks
                dor = do[lo:]
                pv = dor * v[s:s + 1, :]
                a = per_head(lambda c: jnp.where(live, jnp.sum(pk[:, c], axis=1, keepdims=True), 0.0), n)
                da = per_head(lambda c: jnp.where(live, jnp.sum(pv[:, c], axis=1, keepdims=True), 0.0), n)
                ddq = da * (e * ks)
                if lo == 0:
                    dq_t, dq_b = dq_t + ddq[:TILE], dq_b + ddq[TILE:]
                else:
                    dq_b = dq_b + ddq
                here = rid[:TILE] == s - lo
                dk_i[lo // TILE] = jnp.where(here, jnp.sum(da * qe, axis=0, keepdims=True), dk_i[lo // TILE])
                dv_i[lo // TILE] = jnp.where(here, jnp.sum(a * dor, axis=0, keepdims=True), dv_i[lo // TILE])
            dq = jnp.concatenate([dq_t, dq_b], axis=0)
            dk = dk + jnp.concatenate(dk_i, axis=0)
            dv = dv + jnp.concatenate(dv_i, axis=0)
            w = q * dq - k * dk
            gc = gc_ref[...]
            dlf = _running_sum(utri, w) + gc
            gc_ref[...] = gc + jnp.sum(w, axis=0, keepdims=True)
            t1 = dlf / forget - dk
            df_ref[rows, :] = ((1.0 - lb_all) * sf * snf * t1).astype(BF16)
            dlb_ref[...] += jnp.sum(snf * t1, axis=0, keepdims=True)
            dq_ref[rows, :] = (dq * (sq * (1.0 + qp * (1.0 - sq)))).astype(BF16)
            di_ref[rows, :] = dv.astype(BF16)
            return carry

        lax.fori_loop(0, nsc, step, 0, unroll=STEP_UNROLL)

    def col(off):
        return pl.BlockSpec((tb, wb), lambda h, j: (nb - 1 - j, h + off * (nh // hb)))

    vec = pl.BlockSpec((1, wb), lambda h, j: (0, h))
    return _carried_call(
        kern, comm, name=name, grid=(nh // hb, nb),
        in_specs=[col(0), col(1), col(2), col(3), pl.BlockSpec((3, wb), lambda h, j: (0, h)), vec,
                  col(0), pl.BlockSpec((nsc, hb, LANES, LANES), lambda h, j: (nb - 1 - j, h, 0, 0)), col(0)],
        out_specs=[col(0), col(0), col(0), col(0), vec, vec],
        out_shape=[jax.ShapeDtypeStruct((t, d), BF16)] * 4 + [jax.ShapeDtypeStruct((1, d), F32)] * 2,
        scratch_shapes=[pltpu.VMEM((hb, LANES, LANES), F32), pltpu.VMEM((1, wb), F32)],
        operands=[proj, proj, proj, proj, lb_logits, norm_g, o_raw, states, dy], sem=("parallel", "arbitrary"))


_INV_SQRT2 = 0.7071067811865476
_INV_SQRT2PI = 0.3989422804014327


def _gelu(x):
    return 0.5 * x * (1.0 + lax.erf(x * _INV_SQRT2))


def _gelu_grad(x):
    return 0.5 * (1.0 + lax.erf(x * _INV_SQRT2)) + x * jnp.exp(-0.5 * x * x) * _INV_SQRT2PI


def _causal(w):
    r = lax.broadcasted_iota(jnp.int32, (GCHUNK, GCHUNK), 0)
    c = lax.broadcasted_iota(jnp.int32, (GCHUNK, GCHUNK), 1)
    return jnp.where(r >= c, w, 0.0)


def _sg_gate_fwd(pre, ln_g, ln_b, w_s, b_s_t, name):
    t, d2 = pre.shape
    d = d2 // 2
    ng = d // LANES

    def kern(pre_ref, g_ref, b_ref, ws_ref, bs_ref, y_ref):
        z = _gelu(pre_ref[...])
        u = z[:, :d]
        vhat, _ = _ln_hat(z[:, d:])
        vn = (vhat * g_ref[...] + b_ref[...]).astype(BF16)
        bs = bs_ref[...]
        for g in range(ng):
            cols = slice(g * LANES, (g + 1) * LANES)
            wc = _causal(ws_ref[g]).astype(BF16)
            gate = jnp.dot(wc, vn[:, cols], preferred_element_type=F32) + bs[:, g:g + 1]
            y_ref[:, cols] = (u[:, cols] * gate).astype(BF16)

    vec = pl.BlockSpec((1, d), lambda i: (0, 0))
    return pl.pallas_call(
        kern, name=name, grid=(t // GCHUNK,),
        in_specs=[pl.BlockSpec((GCHUNK, d2), lambda i: (i, 0)), vec, vec,
                  pl.BlockSpec((ng, GCHUNK, GCHUNK), lambda i: (0, 0, 0)),
                  pl.BlockSpec((GCHUNK, ng), lambda i: (0, 0))],
        out_specs=pl.BlockSpec((GCHUNK, d), lambda i: (i, 0)),
        out_shape=jax.ShapeDtypeStruct((t, d), BF16),
        compiler_params=_params(("parallel",)),
    )(pre, ln_g, ln_b, w_s, b_s_t)


def _sg_gate_bwd(pre, dy, ln_g, ln_b, w_s, b_s_t, name):
    t, d2 = pre.shape
    d = d2 // 2
    ng = d // LANES

    def kern(pre_ref, dy_ref, g_ref, b_ref, ws_ref, bs_ref, dpre_ref, dws_ref, dbs_ref, dg_ref, db_ref, dvn_ref):
        @pl.when(pl.program_id(0) == 0)
        def _():
            dws_ref[...] = jnp.zeros_like(dws_ref)
            dbs_ref[...] = jnp.zeros_like(dbs_ref)
            dg_ref[...] = jnp.zeros_like(dg_ref)
            db_ref[...] = jnp.zeros_like(db_ref)

        pre = pre_ref[...]
        z = _gelu(pre)
        u = z[:, :d]
        vhat, rstd = _ln_hat(z[:, d:])
        gv = g_ref[...]
        vn = (vhat * gv + b_ref[...]).astype(BF16)
        bs = bs_ref[...]
        dyv = dy_ref[...]
        gp = _gelu_grad(pre)
        lane = lax.broadcasted_iota(jnp.int32, (GCHUNK, ng), 1)
        dbs = jnp.zeros((GCHUNK, ng), F32)
        for g in range(ng):
            cols = slice(g * LANES, (g + 1) * LANES)
            wc = _causal(ws_ref[g]).astype(BF16)
            vng = vn[:, cols]
            gate = jnp.dot(wc, vng, preferred_element_type=F32) + bs[:, g:g + 1]
            dpre_ref[:, cols] = (dyv[:, cols] * gate * gp[:, cols]).astype(BF16)
            dgate = dyv[:, cols] * u[:, cols]
            dbs = dbs + jnp.where(lane == g, jnp.sum(dgate, axis=1, keepdims=True), 0.0)
            dgb = dgate.astype(BF16)
            dws_ref[g] += _causal(lax.dot_general(dgb, vng, NT, preferred_element_type=F32))
            dvn_ref[:, cols] = lax.dot_general(wc, dgb, TN, preferred_element_type=F32)
        dbs_ref[...] += dbs
        dvn = dvn_ref[...]
        dg_ref[...] += jnp.sum(dvn * vhat, axis=0, keepdims=True)
        db_ref[...] += jnp.sum(dvn, axis=0, keepdims=True)
        dvh = dvn * gv
        m1 = jnp.mean(dvh, axis=-1, keepdims=True)
        m2 = jnp.mean(dvh * vhat, axis=-1, keepdims=True)
        dpre_ref[:, d:] = (rstd * (dvh - m1 - vhat * m2) * gp[:, d:]).astype(BF16)

    vec = pl.BlockSpec((1, d), lambda i: (0, 0))
    wsp = pl.BlockSpec((ng, GCHUNK, GCHUNK), lambda i: (0, 0, 0))
    bsp = pl.BlockSpec((GCHUNK, ng), lambda i: (0, 0))
    return pl.pallas_call(
        kern, name=name, grid=(t // GCHUNK,),
        in_specs=[pl.BlockSpec((GCHUNK, d2), lambda i: (i, 0)), pl.BlockSpec((GCHUNK, d), lambda i: (i, 0)),
                  vec, vec, wsp, bsp],
        out_specs=[pl.BlockSpec((GCHUNK, d2), lambda i: (i, 0)), wsp, bsp, vec, vec],
        out_shape=[jax.ShapeDtypeStruct((t, d2), BF16), jax.ShapeDtypeStruct((ng, GCHUNK, GCHUNK), F32),
                   jax.ShapeDtypeStruct((GCHUNK, ng), F32), jax.ShapeDtypeStruct((1, d), F32),
                   jax.ShapeDtypeStruct((1, d), F32)],
        scratch_shapes=[pltpu.VMEM((GCHUNK, d), F32)],
        compiler_params=_params(("arbitrary",)),
    )(pre, dy, ln_g, ln_b, w_s, b_s_t)


def _adamw_math(w, g, m, v):
    m = ADAM_B1 * m + (1.0 - ADAM_B1) * g
    v = ADAM_B2 * v + (1.0 - ADAM_B2) * (g * g)
    m_hat = m / (1.0 - ADAM_B1 ** ADAM_STEP)
    v_hat = v / (1.0 - ADAM_B2 ** ADAM_STEP)
    return -ADAM_LR * (m_hat / (jnp.sqrt(v_hat) + ADAM_EPS) + ADAM_WD * w), m, v


ADAMW_BLOCK_BYTES = 3 << 19


def _adamw(w, gs, m, v, name, comm=None):
    nl, r, c = w.shape
    rb = _pick(r, tuple(p for p in (512, 256, 128, 64, 32, 16, 8) if p * c * 4 <= ADAMW_BLOCK_BYTES))

    def kern(w_ref, m_ref, v_ref, *rest):
        g_refs, (d_ref, mo_ref, vo_ref, go_ref) = rest[:nl], rest[nl:]
        layer = pl.program_id(0)
        g = g_refs[0][...]
        for k in range(1, nl):
            g = jnp.where(layer == k, g_refs[k][...], g)
        dlt, mm, vv = _adamw_math(w_ref[...], g, m_ref[...], v_ref[...])
        d_ref[...] = dlt
        mo_ref[...] = mm
        vo_ref[...] = vv
        go_ref[...] = g

    blk = pl.BlockSpec((None, rb, c), lambda l, i: (l, i, 0))
    g_specs = [pl.BlockSpec((rb, c), lambda l, i, k=k: (jnp.where(l == k, i, 0), 0)) for k in range(nl)]
    outs, carried = _carried_call(
        kern, comm, name=name, grid=(nl, r // rb), in_specs=[blk] * 3 + g_specs, out_specs=[blk] * 4,
        out_shape=[jax.ShapeDtypeStruct((nl, r, c), F32)] * 4, operands=[w, m, v, *gs], sem=("parallel", "parallel"))
    return outs if comm is None else (outs, carried)


CAST_BLOCK_BYTES = 1 << 21


def _cast_bf16(items, name, comm=None):
    metas, start = [], 0
    for arr, _ in items:
        _, r, c = arr.shape
        rb = _pick(r, tuple(p for p in (1024, 512, 256, 128, 64, 32, 16) if p * c * 4 <= CAST_BLOCK_BYTES))
        metas.append((start, r // rb, rb, c))
        start += r // rb
    n = len(items)

    def spec(p, layer=None):
        s0, steps, rb, c = metas[p]
        if layer is None:
            return pl.BlockSpec((rb, c), lambda s: (jnp.clip(s - s0, 0, steps - 1), 0))
        return pl.BlockSpec((None, rb, c), lambda s: (layer, jnp.clip(s - s0, 0, steps - 1), 0))

    def kern(*refs):
        s = pl.program_id(0)
        for p in range(n):
            s0, steps, _, _ = metas[p]

            @pl.when(jnp.logical_and(s >= s0, s < s0 + steps))
            def _(p=p):
                refs[n + p][...] = refs[p][...].astype(BF16)

    outs, carried = _carried_call(
        kern, comm, name=name, grid=(start,), in_specs=[spec(p, layer) for p, (_, layer) in enumerate(items)],
        out_specs=[spec(p) for p in range(n)],
        out_shape=[jax.ShapeDtypeStruct(arr.shape[1:], BF16) for arr, _ in items],
        operands=[arr for arr, _ in items], sem=("arbitrary",))
    return outs, carried


def _comm_call(plan, name):
    ni, no = len(plan.ins), len(plan.out_shapes)

    def body(*refs):
        copies = plan.build(refs[:ni], refs[ni:ni + no], refs[-2], refs[-1], 0)
        for cp in copies:
            cp.start()
        for cp in copies:
            cp.wait()

    anyspec = pl.BlockSpec(memory_space=pl.ANY)
    return pl.pallas_call(
        body, name=name, in_specs=[anyspec] * ni, out_specs=[anyspec] * no, out_shape=plan.out_shapes,
        input_output_aliases=plan.aliases,
        scratch_shapes=[pltpu.SemaphoreType.DMA((plan.n_sems,)), pltpu.SemaphoreType.DMA((plan.n_sems,))],
        compiler_params=pltpu.CompilerParams(has_side_effects=True),
    )(*plan.ins)


def _lb_logits_grad(lb_logits, dlb, name):
    def kern(l_ref, d_ref, o_ref):
        lg = l_ref[...]
        m = jnp.max(lg, axis=0, keepdims=True)
        e = jnp.exp(lg - m)
        p = e / jnp.sum(e, axis=0, keepdims=True)
        row = lax.broadcasted_iota(jnp.int32, lg.shape, 0)
        o_ref[...] = d_ref[...] * p[0:1, :] * (jnp.where(row == 0, 1.0, 0.0) - p)

    return pl.pallas_call(kern, name=name, out_shape=jax.ShapeDtypeStruct(lb_logits.shape, F32))(lb_logits, dlb)


def _sum_devices(others, own, me, name):
    n, r, c = others.shape
    rb = _pick(r, (512, 256, 128, 64, 32, 16, 8))

    def kern(me_ref, a_ref, own_ref, o_ref):
        mine = own_ref[...]
        acc = jnp.where(me_ref[0] == 0, mine, a_ref[0])
        for i in range(1, n):
            acc = acc + jnp.where(me_ref[0] == i, mine, a_ref[i])
        o_ref[...] = acc

    return pl.pallas_call(
        kern, name=name,
        grid_spec=pltpu.PrefetchScalarGridSpec(
            num_scalar_prefetch=1, grid=(r // rb,),
            in_specs=[pl.BlockSpec((n, rb, c), lambda i, s: (0, i, 0)), pl.BlockSpec((rb, c), lambda i, s: (i, 0))],
            out_specs=pl.BlockSpec((rb, c), lambda i, s: (i, 0))),
        out_shape=jax.ShapeDtypeStruct((r, c), F32),
        compiler_params=_params(("parallel",)),
    )(me, others, own)


def _place():
    x, y, c = lax.axis_index("x"), lax.axis_index("y"), lax.axis_index("c")
    return x, y, c


class _Plan:
    def __init__(self, ins, out_shapes, aliases, n_sems, build):
        self.ins, self.out_shapes, self.aliases, self.n_sems, self.build = list(ins), list(out_shapes), aliases, n_sems, build


def _merge(*plans):
    ins, outs, aliases, subs, sems = [], [], {}, [], 0
    for p in plans:
        for k, v in p.aliases.items():
            aliases[len(ins) + k] = len(outs) + v
        subs.append((p, len(ins), len(outs), sems))
        ins += p.ins
        outs += p.out_shapes
        sems += p.n_sems

    def build(in_refs, out_refs, send_sems, recv_sems, base):
        copies = []
        for p, i0, o0, s0 in subs:
            copies += p.build(in_refs[i0:i0 + len(p.ins)], out_refs[o0:o0 + len(p.out_shapes)], send_sems, recv_sems,
                              base + s0)
        return copies

    return _Plan(ins, outs, aliases, sems, build)


def _remote(src, dst, send_sems, recv_sems, k, to):
    return pltpu.make_async_remote_copy(src_ref=src, dst_ref=dst, send_sem=send_sems.at[k], recv_sem=recv_sems.at[k],
                                        device_id=to, device_id_type=MESH)


def _plan_gather_ici(shards):
    n = len(shards)

    def build(ins, outs, send_sems, recv_sems, base):
        x, y, c = _place()
        me = 2 * x + y
        copies = []
        for a in range(n):
            h = ins[a].shape[0] // 2
            rows = pl.ds(c * h, h)
            for r in (1, 2, 3):
                px, py, _ = _chip_rel(x, y, r)
                copies.append(_remote(ins[a].at[rows, :], outs[a].at[me, rows, :], send_sems, recv_sems,
                                      base + 4 * a + r - 1, (px, py, c)))
            copies.append(_remote(ins[a], outs[a].at[me], send_sems, recv_sems, base + 4 * a + 3, (x, y, 1 - c)))
        return copies

    return _Plan(shards, [jax.ShapeDtypeStruct((N_CHIPS,) + s.shape, s.dtype) for s in shards], {}, 4 * n, build)


def _plan_gather_pass(gathered):
    n = len(gathered)

    def build(ins, outs, send_sems, recv_sems, base):
        x, y, c = _place()
        copies = []
        for a in range(n):
            h = outs[a].shape[1] // 2
            rows = pl.ds(c * h, h)
            for r in (1, 2, 3):
                _, _, shard = _chip_rel(x, y, r)
                piece = outs[a].at[shard, rows, :]
                copies.append(_remote(piece, piece, send_sems, recv_sems, base + 3 * a + r - 1, (x, y, 1 - c)))
        return copies

    return _Plan(gathered, [jax.ShapeDtypeStruct(g.shape, g.dtype) for g in gathered], {a: a for a in range(n)},
                 3 * n, build)


def _plan_swap(split):
    n = len(split)

    def build(ins, outs, send_sems, recv_sems, base):
        x, y, c = _place()
        return [_remote(ins[a].at[j, 1 - c], outs[a].at[j], send_sems, recv_sems, base + N_CHIPS * a + j, (x, y, 1 - c))
                for a in range(n) for j in range(N_CHIPS)]

    return _Plan(split, [jax.ShapeDtypeStruct((N_CHIPS,) + g.shape[2:], g.dtype) for g in split], {}, N_CHIPS * n, build)


def _plan_scatter(parts):
    n = len(parts)

    def build(ins, outs, send_sems, recv_sems, base):
        x, y, c = _place()
        copies = []
        for a in range(n):
            for r in (1, 2, 3):
                px, py, shard = _chip_rel(x, y, r)
                copies.append(_remote(ins[a].at[shard], outs[a].at[r - 1], send_sems, recv_sems, base + 3 * a + r - 1,
                                      (px, py, c)))
        return copies

    return _Plan(parts, [jax.ShapeDtypeStruct((3,) + p.shape[1:], p.dtype) for p in parts], {}, 3 * n, build)


def _plan_join(bufs):
    n = len(bufs)

    def build(ins, outs, send_sems, recv_sems, base):
        x, y, c = _place()
        return [_remote(outs[a].at[c], outs[a].at[c], send_sems, recv_sems, base + a, (x, y, 1 - c)) for a in range(n)]

    return _Plan(bufs, [jax.ShapeDtypeStruct(b.shape, b.dtype) for b in bufs], {a: a for a in range(n)}, n, build)


def _carried_call(kern, plan, *, name, grid, in_specs, out_specs, out_shape, operands, scratch_shapes=(),
                  aliases=None, sem=None):
    n_in, n_out, n_sc = len(operands), len(out_shape), len(scratch_shapes)
    aliases = dict(aliases or {})
    if plan is None:
        outs = pl.pallas_call(kern, name=name, grid=grid, in_specs=in_specs, out_specs=out_specs, out_shape=out_shape,
                              scratch_shapes=list(scratch_shapes), input_output_aliases=aliases,
                              compiler_params=_params(sem))(*operands)
        return list(outs), []
    ci, co = len(plan.ins), len(plan.out_shapes)
    for k, v in plan.aliases.items():
        aliases[n_in + k] = n_out + v
    steps = tuple(grid)

    def body(*refs):
        ins, cins = refs[:n_in], refs[n_in:n_in + ci]
        outs = refs[n_in + ci:n_in + ci + n_out]
        couts = refs[n_in + ci + n_out:n_in + ci + n_out + co]
        scratch = refs[n_in + ci + n_out + co:n_in + ci + n_out + co + n_sc]
        send_sems, recv_sems = refs[-2], refs[-1]
        first = functools.reduce(jnp.logical_and, [pl.program_id(a) == 0 for a in range(len(steps))])
        last = functools.reduce(jnp.logical_and, [pl.program_id(a) == steps[a] - 1 for a in range(len(steps))])

        @pl.when(first)
        def _():
            for cp in plan.build(cins, couts, send_sems, recv_sems, 0):
                cp.start()

        kern(*ins, *outs, *scratch)

        @pl.when(last)
        def _():
            for cp in plan.build(cins, couts, send_sems, recv_sems, 0):
                cp.wait()

    anyspec = pl.BlockSpec(memory_space=pl.ANY)
    outs = pl.pallas_call(
        body, name=name, grid=grid, in_specs=list(in_specs) + [anyspec] * ci,
        out_specs=list(out_specs) + [anyspec] * co, out_shape=list(out_shape) + plan.out_shapes,
        scratch_shapes=list(scratch_shapes) + [pltpu.SemaphoreType.DMA((plan.n_sems,)),
                                               pltpu.SemaphoreType.DMA((plan.n_sems,))],
        input_output_aliases=aliases,
        compiler_params=_params(("arbitrary",) * len(steps)),
    )(*operands, *plan.ins)
    return list(outs[:n_out]), list(outs[n_out:])


def _chip_rel(x, y, r):
    px = x if r < 2 else 1 - x
    py = y if r % 2 == 0 else 1 - y
    return px, py, 2 * px + py


def _allgather_whole(arr, name):
    def body(in_ref, out_ref, send_sems, recv_sems, loc_sem):
        x, y, c = _place()
        me = 2 * x + y
        local = pltpu.make_async_copy(in_ref, out_ref.at[me], loc_sem)
        local.start()
        sends = []
        for r in (1, 2, 3):
            px, py, _ = _chip_rel(x, y, r)
            sends.append(pltpu.make_async_remote_copy(
                src_ref=in_ref, dst_ref=out_ref.at[me], send_sem=send_sems.at[r - 1], recv_sem=recv_sems.at[r - 1],
                device_id=(px, py, c), device_id_type=MESH))
        for cp in sends:
            cp.start()
        for r in (1, 2, 3):
            px, py, shard = _chip_rel(x, y, r)
            pltpu.make_async_remote_copy(
                src_ref=in_ref, dst_ref=out_ref.at[shard], send_sem=send_sems.at[r - 1], recv_sem=recv_sems.at[r - 1],
                device_id=(px, py, c), device_id_type=MESH).wait_recv()
        for cp in sends:
            cp.wait_send()
        local.wait()

    anyspec = pl.BlockSpec(memory_space=pl.ANY)
    return pl.pallas_call(
        body, name=name, in_specs=[anyspec], out_specs=anyspec,
        out_shape=jax.ShapeDtypeStruct((N_CHIPS,) + arr.shape, arr.dtype),
        scratch_shapes=[pltpu.SemaphoreType.DMA((3,)), pltpu.SemaphoreType.DMA((3,)), pltpu.SemaphoreType.DMA],
        compiler_params=pltpu.CompilerParams(has_side_effects=True),
    )(arr)


def _plan_gather_all(buf):
    def build(ins, outs, send_sems, recv_sems, base):
        x, y, c = _place()
        me = 4 * x + 2 * y + c
        copies = []
        for r in range(1, N_DEV):
            px, py, _ = _chip_rel(x, y, r // 2)
            pc = c if r % 2 == 0 else 1 - c
            copies.append(_remote(ins[0], outs[0].at[me], send_sems, recv_sems, base + r - 1, (px, py, pc)))
        return copies

    return _Plan([buf, jnp.zeros((N_DEV,) + buf.shape, buf.dtype)],
                 [jax.ShapeDtypeStruct((N_DEV,) + buf.shape, buf.dtype)], {1: 0}, N_DEV - 1, build)


def _add_half(grad, recv, sel, name):
    _, _, rh, cw = grad.shape
    rb = _pick(rh, (512, 256, 176, 128, 64, 32, 16, 8))

    def kern(sel_ref, g_ref, r_ref, o_ref):
        o_ref[...] = (g_ref[...] + r_ref[...]).astype(BF16)

    return pl.pallas_call(
        kern, name=name,
        grid_spec=pltpu.PrefetchScalarGridSpec(
            num_scalar_prefetch=1, grid=(N_CHIPS - 1, rh // rb),
            in_specs=[pl.BlockSpec((None, None, rb, cw), lambda j, i, s: (s[2 + j], s[0], i, 0)),
                      pl.BlockSpec((None, rb, cw), lambda j, i, s: (s[2 + j], i, 0))],
            out_specs=pl.BlockSpec((None, rb, cw), lambda j, i, s: (s[2 + j], i, 0))),
        out_shape=jax.ShapeDtypeStruct((N_CHIPS, rh, cw), BF16),
        compiler_params=_params(("parallel", "parallel")),
    )(sel, grad, recv)


def _add_own(grad, recv, got, sel, name):
    _, _, rh, cw = grad.shape
    rb = _pick(rh, (512, 256, 176, 128, 64, 32, 16, 8))

    def kern(sel_ref, g_ref, r_ref, b_ref, o_ref):
        own = g_ref[...] + r_ref[...]
        o_ref[...] = ((own + b_ref[0].astype(F32)) + b_ref[1].astype(F32)) + b_ref[2].astype(F32)

    return pl.pallas_call(
        kern, name=name,
        grid_spec=pltpu.PrefetchScalarGridSpec(
            num_scalar_prefetch=1, grid=(rh // rb,),
            in_specs=[pl.BlockSpec((None, None, rb, cw), lambda i, s: (s[1], s[0], i, 0)),
                      pl.BlockSpec((None, rb, cw), lambda i, s: (s[1], i, 0)),
                      pl.BlockSpec((3, rb, cw), lambda i, s: (0, i, 0))],
            out_specs=pl.BlockSpec((None, rb, cw), lambda i, s: (s[0], i, 0))),
        out_shape=jax.ShapeDtypeStruct((2, rh, cw), F32),
        compiler_params=_params(("parallel",)),
    )(sel, grad, recv, got)


def _stacked(g):
    return g.reshape(1, g.shape[0] * g.shape[1], g.shape[2])


def _halves(g):
    g = g.reshape(N_CHIPS, g.shape[0] * g.shape[1] // N_CHIPS, g.shape[2])
    return g.reshape(N_CHIPS, 2, g.shape[1] // 2, g.shape[2])


def _whole(f):
    return f.reshape(f.shape[0] * f.shape[1], f.shape[2])


def _step(x3, tgt, sm, w, mom, var):
    x, y, c = _place()
    sel = jnp.stack([c, 2 * x + y] + [_chip_rel(x, y, r)[2] for r in (1, 2, 3)]).astype(jnp.int32)
    wg = {}
    x2 = x3[0]
    cast, landed = _cast_bf16(
        [(x3, 0), (w["hg_w_out"], 0), (w["sg_w_in"], 0), (w["sg_w_out"], 0), (w["ffn_w_up"], 0), (w["ffn_w_up"], 1),
         (w["ffn_w_down"], 0), (w["ffn_w_down"], 1)], "cast_shards",
        comm=_plan_gather_ici([w["hg_w_in"][0].astype(BF16)]))
    xb = cast[0]
    sh = dict(zip(("hg_out", "sg_in", "sg_out", "up0", "up1", "dn0", "dn1"), cast[1:]))
    wg["hg_in"] = _comm_call(_plan_gather_pass(landed), "gather_hg_in_pass")[0]
    proj, landed = _matmul(xb, wg["hg_in"], mode="nn", nsh=N_CHIPS, name="hg_in",
                           comm=_plan_gather_ici([sh["hg_out"], sh["sg_in"]]))
    (yhg, o_raw, states), got = _hgrn2_fwd(
        proj, sm["lb_logits"], sm["hg_norm_g"], "hgrn2_fwd",
        comm=_merge(_plan_gather_pass(landed), _plan_gather_ici([sh["up0"], sh["up1"]])))
    wg["hg_out"], wg["sg_in"], landed = got[0], got[1], got[2:]
    xin1, got = _matmul(yhg, _stacked(wg["hg_out"]), mode="nn", nsh=1, resid=x2, alpha=ALPHA, name="hg_out",
                        comm=_plan_gather_pass(landed))
    wg["up0"], wg["up1"] = got
    h1, h1b = _ln_fwd(xin1, sm["ln1_g"][0:1], sm["ln1_b"][0:1], "l0_ln1")
    u0, landed = _matmul(h1b, wg["up0"], mode="nn", nsh=N_CHIPS, name="l0_ffn_up",
                         comm=_plan_gather_ici([sh["dn0"], sh["sg_out"]]))
    gact0, got = _conv_gate_fwd(u0, sm["conv_w"][0], sm["conv_b"][0:1], "l0_ffn_gate", comm=_plan_gather_pass(landed))
    wg["dn0"], wg["sg_out"] = got
    xin2, landed = _matmul(gact0, _stacked(wg["dn0"]), mode="nn", nsh=1, resid=h1, alpha=ALPHA, name="l0_ffn_down",
                           comm=_plan_gather_ici([sh["dn1"]]))
    h2, h2b = _ln_fwd(xin2, sm["ln2_g"][0:1], sm["ln2_b"][0:1], "l0_ffn_ln")
    pre, got = _matmul(h2b, wg["sg_in"], mode="nn", nsh=N_CHIPS, name="sg_in", comm=_plan_gather_pass(landed))
    wg["dn1"] = got[0]
    ysg = _sg_gate_fwd(pre, sm["sg_ln_g"], sm["sg_ln_b"], sm["sg_w_s"], sm["sg_b_s_t"], "sg_gate")
    xin3 = _matmul(ysg, _stacked(wg["sg_out"]), mode="nn", nsh=1, resid=h2, alpha=ALPHA, name="sg_out")
    h3, h3b = _ln_fwd(xin3, sm["ln1_g"][1:2], sm["ln1_b"][1:2], "l1_ln1")
    u1 = _matmul(h3b, wg["up1"], mode="nn", nsh=N_CHIPS, name="l1_ffn_up")
    gact1 = _conv_gate_fwd(u1, sm["conv_w"][1], sm["conv_b"][1:2], "l1_ffn_gate")
    xin4 = _matmul(gact1, _stacked(wg["dn1"]), mode="nn", nsh=1, resid=h3, alpha=ALPHA, name="l1_ffn_down")

    gs, grad, split, recv, part = {}, {}, {}, {}, {}

    def swap_on(call, keys):
        for k in keys:
            split[k] = _halves(grad[k])
        out, got = call(_plan_swap([split[k] for k in keys]))
        for k, r in zip(keys, got):
            recv[k] = r
            part[k] = _add_half(split[k], r, sel, f"rs_addhalf_{k}")
        return out

    def ffn_bwd(u, gact, hb_in, dxin, dxin_b, w_up, w_down, layer, tag, up, down, waiting):
        dgact = _matmul(dxin_b, _stacked(w_down), mode="nt", nsh=1, name=f"{tag}_ddown")
        grad[down] = _matmul(gact, dxin_b, mode="tn", nsh=1, name=f"{tag}_wdown")
        da, db, dcw, dcb = _conv_gate_bwd(u, dgact, sm["conv_w"][layer], sm["conv_b"][layer:layer + 1], f"{tag}_dgate")
        grad[up] = swap_on(lambda plan: _matmul(hb_in, [da, db], mode="tn", nsh=N_CHIPS, name=f"{tag}_wup", comm=plan),
                           waiting + [down])
        dh = swap_on(lambda plan: _matmul([da, db], w_up, mode="nt", nsh=N_CHIPS, resid=dxin, alpha=ALPHA,
                                          name=f"{tag}_dup", comm=plan), [up])
        return dh, dcw, dcb

    dx, dxb, dg4, db4, loss = _ln_bwd(xin4, tgt, sm["ln2_g"][1:2], sm["ln2_b"][1:2], "l1_ln2_bwd", loss_head=True)
    dh3, dcw1, dcb1 = ffn_bwd(u1, gact1, h3b, dx, dxb, wg["up1"], wg["dn1"], 1, "l1_ffn", "up1", "dn1", [])
    dx, dxb, dg3, db3 = _ln_bwd(xin3, dh3, sm["ln1_g"][1:2], sm["ln1_b"][1:2], "l1_ln1_bwd")
    grad["sg_out"] = _matmul(ysg, dxb, mode="tn", nsh=1, name="sg_wout")
    dysg = swap_on(lambda plan: _matmul(dxb, _stacked(wg["sg_out"]), mode="nt", nsh=1, name="sg_dout", comm=plan),
                   ["sg_out"])
    dpre, gs["sg_w_s"], gs["sg_b_s_t"], gs["sg_ln_g"], gs["sg_ln_b"] = _sg_gate_bwd(
        pre, dysg, sm["sg_ln_g"], sm["sg_ln_b"], sm["sg_w_s"], sm["sg_b_s_t"], "sg_gate_bwd")
    grad["sg_in"] = _matmul(h2b, dpre, mode="tn", nsh=N_CHIPS, name="sg_win")
    dh2 = _matmul(dpre, wg["sg_in"], mode="nt", nsh=N_CHIPS, resid=dx, alpha=ALPHA, name="sg_din")
    dx, dxb, dg2, db2 = _ln_bwd(xin2, dh2, sm["ln2_g"][0:1], sm["ln2_b"][0:1], "l0_ln2_bwd")
    dh1, dcw0, dcb0 = ffn_bwd(u0, gact0, h1b, dx, dxb, wg["up0"], wg["dn0"], 0, "l0_ffn", "up0", "dn0", ["sg_in"])
    dx, dxb, dg1, db1 = _ln_bwd(xin1, dh1, sm["ln1_g"][0:1], sm["ln1_b"][0:1], "l0_ln1_bwd")
    grad["hg_out"] = _matmul(yhg, dxb, mode="tn", nsh=1, name="hg_wout")
    dyhg = swap_on(lambda plan: _matmul(dxb, _stacked(wg["hg_out"]), mode="nt", nsh=1, name="hg_dout", comm=plan),
                   ["hg_out"])
    early = ("dn1", "up1", "sg_out", "sg_in", "dn0", "up0", "hg_out")
    dparts, got = _hgrn2_bwd(proj, sm["lb_logits"], sm["hg_norm_g"], o_raw, states, dyhg, "hgrn2_bwd",
                             comm=_plan_scatter([part[k] for k in early]))
    gs["lb"], gs["hg_norm_g"] = dparts[4], dparts[5]
    gs["ln1_g"] = jnp.concatenate([dg1, dg3], axis=0)
    gs["ln1_b"] = jnp.concatenate([db1, db3], axis=0)
    gs["ln2_g"] = jnp.concatenate([dg2, dg4], axis=0)
    gs["ln2_b"] = jnp.concatenate([db2, db4], axis=0)
    gs["conv_w"] = jnp.stack([dcw0, dcw1], axis=0)
    gs["conv_b"] = jnp.concatenate([dcb0, dcb1], axis=0)
    packed, layout = _pack(gs)
    mine = [_add_own(split[k], recv[k], b, sel, f"rs_addown_{k}") for k, b in zip(early, got)]
    grad["hg_in"], got = _matmul(xb, list(dparts[:4]), mode="tn", nsh=N_CHIPS, name="hg_win",
                                 comm=_merge(_plan_join(mine), _plan_gather_all(packed)))
    red = {k: _whole(f) for k, f in zip(early, got)}
    me8 = jnp.reshape(4 * x + 2 * y + c, (1,)).astype(jnp.int32)
    summed = _unpack(_sum_devices(got[len(early)], packed, me8, "sum_small_grads"), layout)
    gx = swap_on(lambda plan: _matmul(dparts[0], wg["hg_in"], mode="nt", nsh=1, b_off=0, resid=dx, alpha=ALPHA,
                                      name="hg_din_q", comm=plan), ["hg_in"])
    gx, got = _matmul(list(dparts[1:4]), wg["hg_in"], mode="nt", nsh=3, b_off=1, resid=gx, alpha=1.0, name="hg_din_fig",
                      comm=_plan_scatter([part["hg_in"]]))
    mine = _add_own(split["hg_in"], recv["hg_in"], got[0], sel, "rs_addown_hg_in")
    upd = {}
    upd["hg_w_out"], full = _adamw(w["hg_w_out"], [red["hg_out"]], mom["hg_w_out"], var["hg_w_out"], "adamw_hg_w_out",
                                   comm=_plan_join([mine]))
    red["hg_in"] = _whole(full[0])
    for k, src in (("ffn_w_up", ("up0", "up1")), ("ffn_w_down", ("dn0", "dn1")), ("sg_w_in", ("sg_in",)),
                   ("sg_w_out", ("sg_out",)), ("hg_w_in", ("hg_in",))):
        upd[k] = _adamw(w[k], [red[s] for s in src], mom[k], var[k], f"adamw_{k}")
    return loss, gx, summed, upd


_SMALL_ORDER = ("lb", "hg_norm_g", "sg_w_s", "sg_b_s_t", "conv_b", "ln1_g", "ln1_b", "ln2_g", "ln2_b",
                "conv_w", "sg_ln_g", "sg_ln_b")


PACK_ROWS = 512


def _pack(parts):
    flat, layout, off = [], [], 0
    for k in _SMALL_ORDER:
        a = parts[k]
        n = a.size
        pad = (-n) % LANES
        flat.append(jnp.pad(a.reshape(-1), (0, pad)))
        layout.append((k, off, n, a.shape))
        off += n + pad
    flat.append(jnp.zeros(((-off) % (PACK_ROWS * LANES),), F32))
    return jnp.concatenate(flat).reshape(-1, LANES), layout


def _unpack(buf, layout):
    flat = buf.reshape(-1)
    return {k: flat[off:off + n].reshape(shape) for k, off, n, shape in layout}


def kernel(x, lb_logits, hg_w_in, hg_norm_g, hg_w_out, sg_w_in, sg_ln_g, sg_ln_b, sg_w_s, sg_b_s, sg_w_out, ffn_w_up, ffn_conv_w, ffn_conv_b, ffn_w_down, ln1_g, ln1_b, ln2_g, ln2_b, loss_target, m_lb_logits, m_hg_w_in, m_hg_norm_g, m_hg_w_out, m_sg_w_in, m_sg_ln_g, m_sg_ln_b, m_sg_w_s, m_sg_b_s, m_sg_w_out, m_ffn_w_up, m_ffn_conv_w, m_ffn_conv_b, m_ffn_w_down, m_ln1_g, m_ln1_b, m_ln2_g, m_ln2_b, v_lb_logits, v_hg_w_in, v_hg_norm_g, v_hg_w_out, v_sg_w_in, v_sg_ln_g, v_sg_ln_b, v_sg_w_s, v_sg_b_s, v_sg_w_out, v_ffn_w_up, v_ffn_conv_w, v_ffn_conv_b, v_ffn_w_down, v_ln1_g, v_ln1_b, v_ln2_g, v_ln2_b):
    names = ("lb_logits", "hg_w_in", "hg_norm_g", "hg_w_out", "sg_w_in", "sg_ln_g", "sg_ln_b", "sg_w_s", "sg_b_s",
             "sg_w_out", "ffn_w_up", "ffn_conv_w", "ffn_conv_b", "ffn_w_down", "ln1_g", "ln1_b", "ln2_g", "ln2_b")
    w = dict(zip(names, (lb_logits, hg_w_in, hg_norm_g, hg_w_out, sg_w_in, sg_ln_g, sg_ln_b, sg_w_s, sg_b_s,
                         sg_w_out, ffn_w_up, ffn_conv_w, ffn_conv_b, ffn_w_down, ln1_g, ln1_b, ln2_g, ln2_b)))
    mom = dict(zip(names, (m_lb_logits, m_hg_w_in, m_hg_norm_g, m_hg_w_out, m_sg_w_in, m_sg_ln_g, m_sg_ln_b, m_sg_w_s,
                           m_sg_b_s, m_sg_w_out, m_ffn_w_up, m_ffn_conv_w, m_ffn_conv_b, m_ffn_w_down, m_ln1_g,
                           m_ln1_b, m_ln2_g, m_ln2_b)))
    var = dict(zip(names, (v_lb_logits, v_hg_w_in, v_hg_norm_g, v_hg_w_out, v_sg_w_in, v_sg_ln_g, v_sg_ln_b, v_sg_w_s,
                           v_sg_b_s, v_sg_w_out, v_ffn_w_up, v_ffn_conv_w, v_ffn_conv_b, v_ffn_w_down, v_ln1_g,
                           v_ln1_b, v_ln2_g, v_ln2_b)))
    tgt = loss_target[0]
    fq = ffn_conv_w.shape[2]
    dq = sg_ln_g.shape[1]
    cx, cy, _ = _place()
    me = 2 * cx + cy

    wide = max(fq, dq)
    tiny = jnp.concatenate([jnp.pad(ffn_conv_w.reshape(6, fq), ((0, 0), (0, wide - fq))),
                            jnp.pad(sg_ln_g, ((0, 0), (0, wide - dq))),
                            jnp.pad(sg_ln_b, ((0, 0), (0, wide - dq)))], axis=0)
    tiny_all = _allgather_whole(tiny, "gather_small")
    conv_w_full = jnp.transpose(tiny_all[:, 0:6, :fq].reshape(N_CHIPS, 2, 3, fq), (1, 2, 0, 3)).reshape(2, 3, N_CHIPS * fq)
    sm = {"lb_logits": lb_logits, "hg_norm_g": hg_norm_g, "ln1_g": ln1_g, "ln1_b": ln1_b, "ln2_g": ln2_g,
          "ln2_b": ln2_b, "conv_w": conv_w_full, "conv_b": ffn_conv_b,
          "sg_ln_g": tiny_all[:, 6, :dq].reshape(1, N_CHIPS * dq),
          "sg_ln_b": tiny_all[:, 7, :dq].reshape(1, N_CHIPS * dq),
          "sg_w_s": sg_w_s[0], "sg_b_s_t": jnp.transpose(sg_b_s[0])}

    loss_row, grad_x, summed, upd = _step(x, tgt, sm, w, mom, var)
    loss = lax.psum(loss_row[0, 0], ("x", "y", "c"))

    grads = {
        "lb_logits": _lb_logits_grad(lb_logits, summed["lb"], "lb_logits_grad"),
        "hg_norm_g": summed["hg_norm_g"],
        "sg_ln_g": lax.dynamic_slice_in_dim(summed["sg_ln_g"], me * dq, dq, axis=1),
        "sg_ln_b": lax.dynamic_slice_in_dim(summed["sg_ln_b"], me * dq, dq, axis=1),
        "sg_w_s": summed["sg_w_s"][None], "sg_b_s": jnp.transpose(summed["sg_b_s_t"])[None],
        "ffn_conv_w": lax.dynamic_slice_in_dim(summed["conv_w"], me * fq, fq, axis=2),
        "ffn_conv_b": summed["conv_b"],
        "ln1_g": summed["ln1_g"], "ln1_b": summed["ln1_b"], "ln2_g": summed["ln2_g"], "ln2_b": summed["ln2_b"],
    }

    delta, new_m, new_v = {}, {}, {}
    for k, (dlt, mm, vv, gg) in upd.items():
        delta[k], new_m[k], new_v[k], grads[k] = dlt, mm, vv, gg
    small_names = [k for k in names if k not in upd]

    def pack_small(src):
        flat = [src[k].reshape(-1) for k in small_names]
        n = sum(a.size for a in flat)
        flat.append(jnp.zeros(((-n) % (PACK_ROWS * LANES),), F32))
        return jnp.concatenate(flat).reshape(1, -1, LANES)

    outs = _adamw(pack_small(w), [pack_small(grads)[0]], pack_small(mom), pack_small(var), "adamw_small")
    off = 0
    for k in small_names:
        n = w[k].size
        for dst, o in zip((delta, new_m, new_v), outs):
            dst[k] = o.reshape(-1)[off:off + n].reshape(w[k].shape)
        off += n

    return (loss, grad_x[None], *[grads[k] for k in names], *[delta[k] for k in names],
            *[new_m[k] for k in names], *[new_v[k] for k in names])
```
